```python
import math
import jax, jax.numpy as jnp
from jax import lax
import numpy as np

D_MODEL = 1024
BATCH = 32
SEQ = 256
DEPTH = 4
DEC_BATCH = 2
DEC_SEQ = 4096
PAST_LEN = 256

GRID_W = 64
N_EVEN = (DEPTH + 1) // 2
N_ODD = DEPTH // 2
ML_HEADS = 4
ML_HEAD_DIM = 128
ML_W = ML_HEADS * ML_HEAD_DIM
ML_CHUNK = 128
ATT_HEADS = 8
ATT_KV_HEADS = 2
ATT_HEAD_DIM = 64
ATT_GROUP = ATT_HEADS // ATT_KV_HEADS
ATT_W = ATT_HEADS * ATT_HEAD_DIM
KV_W = ATT_KV_HEADS * ATT_HEAD_DIM
Q_BLOCK = 128
ROPE_BASE = 10000.0
ROPE_AXIS_DIM = ATT_HEAD_DIM // 2
MIX_W = ML_W + ATT_W
EVEN_IN = 4 * ML_W + 4 * ML_HEADS + ATT_W + 2 * KV_W
EVEN_SPLITS = (ML_W, 2 * ML_W, 3 * ML_W, 4 * ML_W, 4 * ML_W + 4 * ML_HEADS,
               4 * ML_W + 4 * ML_HEADS + ATT_W, 4 * ML_W + 4 * ML_HEADS + ATT_W + KV_W)
HY_EMB = 33
HY_BANDS = (HY_EMB - 1) // 2
HY_FILTER_W = 64
HY_CONV_W = 3
HY_TARGET = 1e-2
HY_SHORT_PCT = 0.3
HY_LONG_PCT = 1.5
D_FF = 2816
N_EXPERTS = 8
TOP_K = 2
MOE_D_FF = 3584
DEEPNORM_ALPHA = (2 * DEPTH) ** 0.25
DEEPNORM_BETA = (8 * DEPTH) ** -0.25
LN_EPS = 1e-5
RMS_EPS = 1e-6

kernel_name = 'hybrid_mlstm_gqa_hyena_dit_step'


def _layer_norm(x, g, b):
    xf = x.astype(jnp.float32)
    mu = jnp.mean(xf, axis=-1, keepdims=True)
    var = jnp.mean(jnp.square(xf - mu), axis=-1, keepdims=True)
    return ((xf - mu) * lax.rsqrt(var + LN_EPS) * g.astype(jnp.float32) + b.astype(jnp.float32)).astype(x.dtype)


def _rms_norm(x, g):
    xf = x.astype(jnp.float32)
    return (xf * lax.rsqrt(jnp.mean(xf * xf, axis=-1, keepdims=True) + RMS_EPS) * g.astype(jnp.float32)).astype(x.dtype)


def _modulation(cvec, w_ada, b_ada):
    m = jax.nn.silu(cvec) @ w_ada + b_ada
    return jnp.split(m[..., None, :], 6, axis=-1)


def _axial_rope(n_tokens):
    rows = n_tokens // GRID_W
    row = jnp.repeat(jnp.arange(rows, dtype=jnp.float32), GRID_W)
    col = (jnp.arange(n_tokens) % GRID_W).astype(jnp.float32)
    inv = ROPE_BASE ** (-jnp.arange(ROPE_AXIS_DIM // 2, dtype=jnp.float32) * 2.0 / ROPE_AXIS_DIM)
    ang = jnp.concatenate([row[:, None] * inv, col[:, None] * inv], axis=-1)
    return jnp.cos(ang)[None, :, None, :], jnp.sin(ang)[None, :, None, :]


def _apply_rope(x, cos, sin):
    xf = x.astype(jnp.float32)
    x1, x2 = xf[..., 0::2], xf[..., 1::2]
    out = jnp.stack([x1 * cos - x2 * sin, x1 * sin + x2 * cos], axis=-1)
    return out.reshape(x.shape).astype(x.dtype)


def _attention_blocks(q, k, v):
    B, T = q.shape[0], q.shape[1]
    n_blocks = T // Q_BLOCK
    qb = jnp.moveaxis(q.reshape(B, n_blocks, Q_BLOCK, ATT_KV_HEADS, ATT_GROUP, ATT_HEAD_DIM), 1, 0)
    kf = k.astype(jnp.float32)
    scale = ATT_HEAD_DIM ** -0.5

    def block(qblk):
        s = jnp.einsum('bqkgd,bskd->bkgqs', qblk.astype(jnp.float32), kf) * scale
        p = jax.nn.softmax(s, axis=-1)
        return jnp.einsum('bkgqs,bskd->bqkgd', p.astype(v.dtype), v)

    out = lax.map(block, qb)
    return jnp.moveaxis(out, 0, 1).reshape(B, T, ATT_W)


def _mlstm_scan(q, k, v, li, lf, c0, n0, m0):
    B, H, T = q.shape[0], q.shape[1], q.shape[2]
    nc = T // ML_CHUNK

    def chunks(a):
        return jnp.moveaxis(a.reshape(a.shape[:2] + (nc, ML_CHUNK) + a.shape[3:]), 2, 0)

    causal = jnp.tril(jnp.ones((ML_CHUNK, ML_CHUNK), dtype=bool))

    def step(carry, xs):
        c_st, n_st, m_st = carry
        qc, kc, vc, lic, lfc = xs
        b = jnp.cumsum(lfc, axis=-1)
        d = jnp.where(causal, b[..., :, None] - b[..., None, :] + lic[..., None, :], -jnp.inf)
        inter = b + m_st[..., None]
        m_out = jnp.maximum(inter, jnp.max(d, axis=-1))
        s = jnp.einsum('bhtd,bhsd->bhts', qc, kc) * jnp.exp(d - m_out[..., None])
        w_inter = jnp.exp(inter - m_out)
        num = jnp.einsum('bhts,bhsv->bhtv', s, vc) + w_inter[..., None] * jnp.einsum('bhtd,bhdv->bhtv', qc, c_st)
        den = jnp.sum(s, axis=-1) + w_inter * jnp.einsum('bhtd,bhd->bht', qc, n_st)
        h = num / jnp.maximum(jnp.abs(den), jnp.exp(-m_out))[..., None]
        b_last = b[..., -1]
        g = b_last[..., None] - b + lic
        m_new = jnp.maximum(b_last + m_st, jnp.max(g, axis=-1))
        wk = jnp.exp(g - m_new[..., None])
        decay = jnp.exp(b_last + m_st - m_new)
        c_new = decay[..., None, None] * c_st + jnp.einsum('bhs,bhsd,bhsv->bhdv', wk, kc, vc)
        n_new = decay[..., None] * n_st + jnp.einsum('bhs,bhsd->bhd', wk, kc)
        return (c_new, n_new, m_new), h

    final, hs = lax.scan(step, (c0, n0, m0), (chunks(q), chunks(k), chunks(v), chunks(li), chunks(lf)))
    h = jnp.moveaxis(hs, 0, 2).reshape(B, H, T, ML_HEAD_DIM)
    return h, final


def _mlstm_mixer(q_m, k_m, v_m, o_m, gates, b_igate, b_fgate, norm_g, init_fwd, init_bwd):
    B, T = q_m.shape[0], q_m.shape[1]

    def heads(a):
        return a.astype(jnp.float32).reshape(B, T, ML_HEADS, ML_HEAD_DIM).transpose(0, 2, 1, 3)

    q, k, v = heads(q_m), heads(k_m) * ML_HEAD_DIM ** -0.5, heads(v_m)
    g = gates.astype(jnp.float32).reshape(B, T, 2, 2, ML_HEADS) + jnp.stack([b_igate, b_fgate], axis=1).astype(jnp.float32)
    g = jnp.transpose(g, (2, 3, 0, 4, 1))
    li, lf = g[:, 0], jax.nn.log_sigmoid(g[:, 1])
    h_f, st_f = _mlstm_scan(q, k, v, li[0], lf[0], *init_fwd)
    h_b, st_b = _mlstm_scan(jnp.flip(q, 2), jnp.flip(k, 2), jnp.flip(v, 2),
                            jnp.flip(li[1], 2), jnp.flip(lf[1], 2), *init_bwd)
    h = h_f + jnp.flip(h_b, 2)
    mu = jnp.mean(h, axis=-1, keepdims=True)
    var = jnp.mean(jnp.square(h - mu), axis=-1, keepdims=True)
    h = ((h - mu) * lax.rsqrt(var + RMS_EPS)).transpose(0, 2, 1, 3).reshape(B, T, ML_W)
    out = jax.nn.sigmoid(o_m.astype(jnp.float32)) * h * norm_g.astype(jnp.float32)
    return out.astype(q_m.dtype), st_f, st_b


def _even_mixer(h, w_in, b_igate, b_fgate, ml_norm_g, q_norm_g, k_norm_g, w_out, init_fwd, init_bwd, k_ctx, v_ctx):
    B, T = h.shape[0], h.shape[1]
    q_m, k_m, v_m, o_m, gates, q_a, k_a, v_a = jnp.split(h @ w_in, EVEN_SPLITS, axis=-1)
    ml_out, st_f, st_b = _mlstm_mixer(q_m, k_m, v_m, o_m, gates, b_igate, b_fgate, ml_norm_g, init_fwd, init_bwd)
    q = _rms_norm(q_a.reshape(B, T, ATT_HEADS, ATT_HEAD_DIM), q_norm_g)
    k = _rms_norm(k_a.reshape(B, T, ATT_KV_HEADS, ATT_HEAD_DIM), k_norm_g)
    v = v_a.reshape(B, T, ATT_KV_HEADS, ATT_HEAD_DIM)
    if k_ctx is None:
        att = _attention_blocks(q, k, v)
    else:
        cos, sin = _axial_rope(T)
        q_r, k_r = _apply_rope(q, cos, sin), _apply_rope(k, cos, sin)
        k_all = jnp.concatenate([k_r, k_ctx.astype(k_r.dtype)], axis=1)
        v_all = jnp.concatenate([v, v_ctx.astype(v.dtype)], axis=1)
        att = _attention_blocks(q_r, k_all, v_all)
    out = jnp.concatenate([ml_out, att.astype(ml_out.dtype)], axis=-1) @ w_out
    return out, k, v, st_f, st_b


def _hyena_filter(L, w1, b1, w2, b2, w3, sin_freq):
    f32 = jnp.float32
    t = jnp.arange(L, dtype=f32)[:, None] / L
    bands = jnp.arange(1, HY_BANDS + 1, dtype=f32)[None, :]
    feat = jnp.concatenate([t, jnp.sin(2.0 * math.pi * bands * t), jnp.cos(2.0 * math.pi * bands * t)], axis=-1)
    z = jnp.sin(sin_freq[0].astype(f32) * (feat @ w1.astype(f32) + b1.astype(f32)))
    z = jnp.sin(sin_freq[1].astype(f32) * (z @ w2.astype(f32) + b2.astype(f32)))
    filt = z @ w3.astype(f32)
    deltas = jnp.abs(jnp.linspace(math.log(HY_TARGET) / HY_LONG_PCT, math.log(HY_TARGET) / HY_SHORT_PCT, D_MODEL, dtype=f32))
    window = jnp.exp(-t * deltas)
    h_fwd = filt[:, :D_MODEL] * window
    h_bwd = filt[:, D_MODEL:] * window
    full = jnp.concatenate([h_fwd, jnp.zeros((1, D_MODEL), f32), jnp.flip(h_bwd[1:], axis=0)], axis=0)
    return full / jnp.sum(jnp.abs(full), axis=0, keepdims=True)


def _hyena_mixer(h, w_in, conv_w, conv_b, w1, b1, w2, b2, w3, sin_freq, skip, w_out):
    L = h.shape[1]
    u = h @ w_in
    up = jnp.pad(u, ((0, 0), (1, 1), (0, 0)))
    u = up[:, :-2] * conv_w[0] + up[:, 1:-1] * conv_w[1] + up[:, 2:] * conv_w[2] + conv_b
    x0, x1, v = jnp.split(u, 3, axis=-1)
    z = (v * x1).astype(jnp.float32)
    kf = jnp.fft.rfft(_hyena_filter(L, w1, b1, w2, b2, w3, sin_freq), axis=0)
    zf = jnp.fft.rfft(z, n=2 * L, axis=1)
    y = jnp.fft.irfft(zf * kf[None], n=2 * L, axis=1)[:, :L] + z * skip.astype(jnp.float32)
    y = (y * x0.astype(jnp.float32)).astype(h.dtype)
    return y @ w_out


def _swiglu(h, w_gate, w_up, w_down):
    return (jax.nn.silu(h @ w_gate) * (h @ w_up)) @ w_down


def _moe_swiglu(h, w_router, w_gate, w_up, w_down):
    B, T, D = h.shape
    xt = h.reshape(B * T, D)
    probs = jax.nn.softmax((xt @ w_router).astype(jnp.float32), axis=-1)
    top_p, top_i = lax.top_k(probs, TOP_K)
    top_p = top_p / jnp.sum(top_p, axis=-1, keepdims=True)
    gates = jnp.sum(jax.nn.one_hot(top_i, N_EXPERTS, dtype=jnp.float32) * top_p[..., None], axis=1)
    out = jnp.zeros((B * T, D), jnp.float32)
    for e in range(N_EXPERTS):
        out = out + gates[:, e:e + 1] * _swiglu(xt, w_gate[e], w_up[e], w_down[e]).astype(jnp.float32)
    return out.astype(h.dtype).reshape(B, T, D)


def setup_inputs(seed: int = 0) -> dict:
    key = jax.random.key(seed)
    ks = iter(jax.random.split(key, 48))

    def nrm(shape, scale):
        return jax.random.normal(next(ks), shape, jnp.float32) * scale

    D = D_MODEL
    inp = {}
    inp['x_prompt'] = nrm((BATCH, SEQ, D), 1.0)
    inp['x_sample'] = nrm((DEC_BATCH, DEC_SEQ, D), 1.0)
    inp['cache_attn_k'] = nrm((DEC_BATCH, N_EVEN, PAST_LEN, ATT_KV_HEADS, ATT_HEAD_DIM), 1.0)
    inp['cache_attn_v'] = nrm((DEC_BATCH, N_EVEN, PAST_LEN, ATT_KV_HEADS, ATT_HEAD_DIM), 1.0)
    inp['state_mlstm_C'] = nrm((DEC_BATCH, N_EVEN, 2, ML_HEADS, ML_HEAD_DIM, ML_HEAD_DIM), 0.1)
    inp['state_mlstm_n'] = nrm((DEC_BATCH, N_EVEN, 2, ML_HEADS, ML_HEAD_DIM), 0.1)
    inp['state_mlstm_m'] = nrm((DEC_BATCH, N_EVEN, 2, ML_HEADS), 1.0)
    inp['c'] = nrm((DEC_BATCH, D), 1.0)
    inp['c_ctx'] = nrm((D,), 1.0)
    inp['w_ada'] = nrm((DEPTH, D, 6 * D), 0.5 * D ** -0.5)
    inp['b_ada'] = nrm((DEPTH, 6 * D), 0.02)
    inp['ln_g'] = 1.0 + nrm((DEPTH, 2, D), 0.02)
    inp['ln_b'] = nrm((DEPTH, 2, D), 0.02)
    inp['w_in_even'] = nrm((N_EVEN, D, EVEN_IN), D ** -0.5)
    inp['b_igate'] = nrm((N_EVEN, 2, ML_HEADS), 0.1)
    inp['b_fgate'] = jnp.linspace(3.0, 6.0, ML_HEADS, dtype=jnp.float32) + nrm((N_EVEN, 2, ML_HEADS), 0.1)
    inp['ml_norm_g'] = 1.0 + nrm((N_EVEN, ML_W), 0.02)
    inp['q_norm_g'] = 1.0 + nrm((N_EVEN, ATT_HEAD_DIM), 0.02)
    inp['k_norm_g'] = 1.0 + nrm((N_EVEN, ATT_HEAD_DIM), 0.02)
    inp['w_out_even'] = nrm((N_EVEN, MIX_W, D), MIX_W ** -0.5 * DEEPNORM_BETA)
    inp['w_ffn_gate'] = nrm((N_EVEN, D, D_FF), D ** -0.5)
    inp['w_ffn_up'] = nrm((N_EVEN, D, D_FF), D ** -0.5)
    inp['w_ffn_down'] = nrm((N_EVEN, D_FF, D), D_FF ** -0.5 * DEEPNORM_BETA)
    inp['w_in_hy'] = nrm((N_ODD, D, 3 * D), D ** -0.5)
    inp['hy_conv_w'] = nrm((N_ODD, HY_CONV_W, 3 * D), HY_CONV_W ** -0.5)
    inp['hy_conv_b'] = nrm((N_ODD, 3 * D), 0.02)
    inp['hy_filt_w1'] = nrm((N_ODD, HY_EMB, HY_FILTER_W), HY_EMB ** -0.5)
    inp['hy_filt_b1'] = nrm((N_ODD, HY_FILTER_W), 0.02)
    inp['hy_filt_w2'] = nrm((N_ODD, HY_FILTER_W, HY_FILTER_W), HY_FILTER_W ** -0.5)
    inp['hy_filt_b2'] = nrm((N_ODD, HY_FILTER_W), 0.02)
    inp['hy_filt_w3'] = nrm((N_ODD, HY_FILTER_W, 2 * D), HY_FILTER_W ** -0.5)
    inp['hy_sin_freq'] = 1.0 + nrm((N_ODD, 2, HY_FILTER_W), 0.02)
    inp['hy_skip'] = nrm((N_ODD, D), 0.1)
    inp['w_out_hy'] = nrm((N_ODD, D, D), D ** -0.5 * DEEPNORM_BETA)
    inp['w_router'] = nrm((N_ODD, D, N_EXPERTS), D ** -0.5)
    inp['w_moe_gate'] = nrm((N_ODD, N_EXPERTS, D, MOE_D_FF), D ** -0.5)
    inp['w_moe_up'] = nrm((N_ODD, N_EXPERTS, D, MOE_D_FF), D ** -0.5)
    inp['w_moe_down'] = nrm((N_ODD, N_EXPERTS, MOE_D_FF, D), MOE_D_FF ** -0.5 * DEEPNORM_BETA)
    return inp


def reference(x_prompt, x_sample, cache_attn_k, cache_attn_v, state_mlstm_C, state_mlstm_n, state_mlstm_m,
              c, c_ctx, w_ada, b_ada, ln_g, ln_b, w_in_even, b_igate, b_fgate, ml_norm_g, q_norm_g, k_norm_g,
              w_out_even, w_ffn_gate, w_ffn_up, w_ffn_down, w_in_hy, hy_conv_w, hy_conv_b, hy_filt_w1, hy_filt_b1,
              hy_filt_w2, hy_filt_b2, hy_filt_w3, hy_sin_freq, hy_skip, w_out_hy, w_router, w_moe_gate, w_moe_up,
              w_moe_down):
    xc, xs = x_prompt, x_sample
    Bc = x_prompt.shape[0]
    new_k, new_v, new_C, new_n, new_m = [], [], [], [], []
    for layer in range(DEPTH):
        sh1c, sc1c, g1c, sh2c, sc2c, g2c = _modulation(c_ctx, w_ada[layer], b_ada[layer])
        sh1s, sc1s, g1s, sh2s, sc2s, g2s = _modulation(c, w_ada[layer], b_ada[layer])
        hc = xc * (1.0 + sc1c) + sh1c
        hs = xs * (1.0 + sc1s) + sh1s
        if layer % 2 == 0:
            e = layer // 2
            zero_state = (jnp.zeros((Bc, ML_HEADS, ML_HEAD_DIM, ML_HEAD_DIM), jnp.float32),
                          jnp.zeros((Bc, ML_HEADS, ML_HEAD_DIM), jnp.float32),
                          jnp.zeros((Bc, ML_HEADS), jnp.float32))
            mix_c, k_c, v_c, st_f, st_b = _even_mixer(hc, w_in_even[e], b_igate[e], b_fgate[e], ml_norm_g[e],
                                                      q_norm_g[e], k_norm_g[e], w_out_even[e],
                                                      zero_state, zero_state, None, None)
            new_k.append(k_c)
            new_v.append(v_c)
            new_C.append(jnp.stack([st_f[0], st_b[0]], axis=1))
            new_n.append(jnp.stack([st_f[1], st_b[1]], axis=1))
            new_m.append(jnp.stack([st_f[2], st_b[2]], axis=1))
            init_f = (state_mlstm_C[:, e, 0].astype(jnp.float32), state_mlstm_n[:, e, 0].astype(jnp.float32),
                      state_mlstm_m[:, e, 0].astype(jnp.float32))
            init_b = (state_mlstm_C[:, e, 1].astype(jnp.float32), state_mlstm_n[:, e, 1].astype(jnp.float32),
                      state_mlstm_m[:, e, 1].astype(jnp.float32))
            mix_s, _, _, _, _ = _even_mixer(hs, w_in_even[e], b_igate[e], b_fgate[e], ml_norm_g[e],
                                            q_norm_g[e], k_norm_g[e], w_out_even[e],
                                            init_f, init_b, cache_attn_k[:, e], cache_attn_v[:, e])
        else:
            o = layer // 2
            hy_args = (w_in_hy[o], hy_conv_w[o], hy_conv_b[o], hy_filt_w1[o], hy_filt_b1[o], hy_filt_w2[o],
                       hy_filt_b2[o], hy_filt_w3[o], hy_sin_freq[o], hy_skip[o], w_out_hy[o])
            mix_c = _hyena_mixer(hc, *hy_args)
            mix_s = _hyena_mixer(hs, *hy_args)
        xc = _layer_norm(DEEPNORM_ALPHA * xc + g1c * mix_c, ln_g[layer, 0], ln_b[layer, 0])
        xs = _layer_norm(DEEPNORM_ALPHA * xs + g1s * mix_s, ln_g[layer, 0], ln_b[layer, 0])
        hc = xc * (1.0 + sc2c) + sh2c
        hs = xs * (1.0 + sc2s) + sh2s
        if layer % 2 == 0:
            e = layer // 2
            ffn_c = _swiglu(hc, w_ffn_gate[e], w_ffn_up[e], w_ffn_down[e])
            ffn_s = _swiglu(hs, w_ffn_gate[e], w_ffn_up[e], w_ffn_down[e])
        else:
            o = layer // 2
            ffn_c = _moe_swiglu(hc, w_router[o], w_moe_gate[o], w_moe_up[o], w_moe_down[o])
            ffn_s = _moe_swiglu(hs, w_router[o], w_moe_gate[o], w_moe_up[o], w_moe_down[o])
        xc = _layer_norm(DEEPNORM_ALPHA * xc + g2c * ffn_c, ln_g[layer, 1], ln_b[layer, 1])
        xs = _layer_norm(DEEPNORM_ALPHA * xs + g2s * ffn_s, ln_g[layer, 1], ln_b[layer, 1])
    return (xc, xs, jnp.stack(new_k, axis=1), jnp.stack(new_v, axis=1), jnp.stack(new_C, axis=1),
            jnp.stack(new_n, axis=1), jnp.stack(new_m, axis=1))
```

```python
import functools
import math

import jax
import jax.numpy as jnp
from jax import lax
from jax.experimental import pallas as pl
from jax.experimental.pallas import tpu as pltpu

F32 = jnp.float32
BF16 = jnp.bfloat16
HIGHEST = lax.Precision.HIGHEST

D = 1024
BATCH, SEQ = 32, 256
DEC_BATCH, DEC_SEQ = 2, 4096
DEPTH = 4
PAST_LEN = 256
GRID_W = 64
N_CTX = BATCH * SEQ
N_LAT = DEC_BATCH * DEC_SEQ
N_TOK = N_CTX + N_LAT

ML_HEADS, ML_HEAD_DIM = 4, 128
ML_W = ML_HEADS * ML_HEAD_DIM
CHUNK = 128
ATT_HEADS, ATT_KV_HEADS, ATT_HEAD_DIM = 8, 2, 64
ATT_GROUP = ATT_HEADS // ATT_KV_HEADS
ATT_W = ATT_HEADS * ATT_HEAD_DIM
KV_W = ATT_KV_HEADS * ATT_HEAD_DIM
GROUP_W = ATT_GROUP * ATT_HEAD_DIM
ROPE_BASE = 10000.0
N_GATES = 4 * ML_HEADS
MAIN_W = 4 * ML_W + ATT_W + 2 * KV_W

HY_EMB = 33
HY_BANDS = (HY_EMB - 1) // 2
HY_TARGET, HY_SHORT_PCT, HY_LONG_PCT = 1e-2, 0.3, 1.5
D_FF = 2816
N_EXPERTS = 8
MOE_D_FF = 3584
ALPHA = (2 * DEPTH) ** 0.25
LN_EPS = 1e-5
RMS_EPS = 1e-6

LANES = 128
VMEM_LIMIT = 48 * 1024 * 1024


def _params(n_axes, vmem=VMEM_LIMIT):
    return pltpu.CompilerParams(dimension_semantics=("arbitrary",) * n_axes, vmem_limit_bytes=vmem)


def _group_of_row(r):
    return jnp.where(r < N_CTX, 0, 1 + (r - N_CTX) // DEC_SEQ)


def _modulate(x_ref, sh_ref, sc_ref):
    return x_ref[...] * (1.0 + sc_ref[0]) + sh_ref[0]


def _mod_specs(tm, n_axes, row_axis):
    def rows(*ids):
        return (ids[row_axis], 0)

    def grp(*ids):
        return (_group_of_row(ids[row_axis] * tm), 0, 0)

    del n_axes
    return [pl.BlockSpec((tm, D), rows), pl.BlockSpec((1, 1, D), grp), pl.BlockSpec((1, 1, D), grp)]


def _ada_body(c_ref, w_ref, b_ref, o_ref):
    c = c_ref[...]
    s = c * jax.nn.sigmoid(c)
    o_ref[0] = jnp.dot(s, w_ref[0], preferred_element_type=F32, precision=HIGHEST) + b_ref[0]


def _ada(cvec, w_ada, b_ada):
    tn = 1536
    return pl.pallas_call(
        _ada_body,
        out_shape=jax.ShapeDtypeStruct((DEPTH, 8, 6 * D), F32),
        grid=(DEPTH, 6 * D // tn),
        in_specs=[pl.BlockSpec((8, D), lambda l, j: (0, 0)),
                  pl.BlockSpec((1, D, tn), lambda l, j: (l, 0, j)),
                  pl.BlockSpec((1, 1, tn), lambda l, j: (l, 0, j))],
        out_specs=pl.BlockSpec((1, 8, tn), lambda l, j: (l, 0, j)),
        compiler_params=_params(2),
        name="ada_modulation",
    )(cvec, w_ada, b_ada.reshape(DEPTH, 1, 6 * D))


def _mod_mm_body(x_ref, sh_ref, sc_ref, *refs, n_w):
    w_refs, o_ref, h_ref = refs[:n_w], refs[n_w], refs[n_w + 1]

    @pl.when(pl.program_id(2) == 0)
    def _():
        h_ref[...] = _modulate(x_ref, sh_ref, sc_ref).astype(BF16)

    h = h_ref[...]
    if n_w == 1:
        o = jnp.dot(h, w_refs[0][0], preferred_element_type=F32)
    else:
        g = jnp.dot(h, w_refs[0][0], preferred_element_type=F32)
        u = jnp.dot(h, w_refs[1][0], preferred_element_type=F32)
        o = g * jax.nn.sigmoid(g) * u
    o_ref[0] = o.astype(o_ref.dtype)


def _mod_matmul(x, sh, sc, ws, *, tm, tn, out_dtype, name):
    n_e, _, f = ws[0].shape
    return pl.pallas_call(
        functools.partial(_mod_mm_body, n_w=len(ws)),
        out_shape=jax.ShapeDtypeStruct((n_e, N_TOK, f), out_dtype),
        grid=(n_e, N_TOK // tm, f // tn),
        in_specs=_mod_specs(tm, 3, 1) + [pl.BlockSpec((1, D, tn), lambda e, i, j: (e, 0, j)) for _ in ws],
        out_specs=pl.BlockSpec((1, tm, tn), lambda e, i, j: (e, i, j)),
        scratch_shapes=[pltpu.VMEM((tm, D), BF16)],
        compiler_params=_params(3),
        name=name,
    )(x, sh, sc, *ws)


def _mm_res_ln_body(a_ref, w_ref, x_ref, g_ref, lng_ref, lnb_ref, *refs, n_e, use_gates):
    if use_gates:
        gates_ref, o_ref, acc_ref = refs
    else:
        o_ref, acc_ref = refs
    e = pl.program_id(1)
    p = jnp.dot(a_ref[0], w_ref[0], preferred_element_type=F32)
    if use_gates:
        gt = gates_ref[...]
        lane = lax.broadcasted_iota(jnp.int32, gt.shape, 1)
        p = p * jnp.sum(jnp.where(lane == e, gt, 0.0), axis=-1, keepdims=True)

    @pl.when(e == 0)
    def _():
        acc_ref[...] = p

    @pl.when(e > 0)
    def _():
        acc_ref[...] += p

    @pl.when(e == n_e - 1)
    def _():
        y = ALPHA * x_ref[...] + g_ref[0] * acc_ref[...]
        mu = jnp.mean(y, axis=-1, keepdims=True)
        yc = y - mu
        var = jnp.mean(yc * yc, axis=-1, keepdims=True)
        o_ref[...] = yc * lax.rsqrt(var + LN_EPS) * lng_ref[...] + lnb_ref[...]


def _matmul_res_ln(a, w, x, gate, ln_g, ln_b, *, tm, gates=None, name):
    n_e, _, k = a.shape
    use_gates = gates is not None
    in_specs = [pl.BlockSpec((1, tm, k), lambda i, e: (e, i, 0)),
                pl.BlockSpec((1, k, D), lambda i, e: (e, 0, 0)),
                pl.BlockSpec((tm, D), lambda i, e: (i, 0)),
                pl.BlockSpec((1, 1, D), lambda i, e: (_group_of_row(i * tm), 0, 0)),
                pl.BlockSpec((1, D), lambda i, e: (0, 0)),
                pl.BlockSpec((1, D), lambda i, e: (0, 0))]
    args = [a, w, x, gate, ln_g.reshape(1, D), ln_b.reshape(1, D)]
    if use_gates:
        in_specs.append(pl.BlockSpec((tm, LANES), lambda i, e: (i, 0)))
        args.append(gates)
    return pl.pallas_call(
        functools.partial(_mm_res_ln_body, n_e=n_e, use_gates=use_gates),
        out_shape=jax.ShapeDtypeStruct((N_TOK, D), F32),
        grid=(N_TOK // tm, n_e),
        in_specs=in_specs,
        out_specs=pl.BlockSpec((tm, D), lambda i, e: (i, 0)),
        scratch_shapes=[pltpu.VMEM((tm, D), F32)],
        compiler_params=_params(2),
        name=name,
    )(*args)


def _log_sigmoid(x):
    return jnp.minimum(x, 0.0) - jnp.log(1.0 + jnp.exp(-jnp.abs(x)))


def _gates_body(x_ref, sh_ref, sc_ref, wg_ref, wgt_ref, b_ref, bt_ref,
                lic_ref, bc_ref, lir_ref, br_ref, *, tm):
    h = _modulate(x_ref, sh_ref, sc_ref)
    g = jnp.dot(h, wg_ref[...], preferred_element_type=F32, precision=HIGHEST) + b_ref[...]
    gt = lax.dot_general(wgt_ref[...], h, (((1,), (1,)), ((), ())),
                         preferred_element_type=F32, precision=HIGHEST) + bt_ref[...]
    lic_ref[...] = g
    lir_ref[...] = gt
    lf, lft = _log_sigmoid(g), _log_sigmoid(gt)
    r = lax.broadcasted_iota(jnp.int32, (CHUNK, CHUNK), 0)
    c = lax.broadcasted_iota(jnp.int32, (CHUNK, CHUNK), 1)
    tri_l = (c <= r).astype(F32)
    tri_u = (c >= r).astype(F32)
    fwd_col = lax.broadcasted_iota(jnp.int32, (CHUNK, LANES), 1) < 2 * ML_HEADS
    fwd_row = lax.broadcasted_iota(jnp.int32, (N_GATES, CHUNK), 0) < 2 * ML_HEADS
    for ch in range(tm // CHUNK):
        sl = slice(ch * CHUNK, (ch + 1) * CHUNK)
        lfc, lftc = lf[sl, :], lft[:, sl]
        cum_f = jnp.dot(tri_l, lfc, preferred_element_type=F32, precision=HIGHEST)
        cum_b = jnp.dot(tri_u, lfc, preferred_element_type=F32, precision=HIGHEST)
        bc_ref[sl, :] = jnp.where(fwd_col, cum_f, cum_b)
        cum_f = jnp.dot(lftc, tri_u, preferred_element_type=F32, precision=HIGHEST)
        cum_b = jnp.dot(lftc, tri_l, preferred_element_type=F32, precision=HIGHEST)
        br_ref[:, sl] = jnp.where(fwd_row, cum_f, cum_b)


def _gates(x, sh, sc, wg, b_gate):
    tm = 256
    wg_pad = jnp.pad(wg, ((0, 0), (0, LANES - N_GATES)))
    b_pad = jnp.pad(b_gate, (0, LANES - N_GATES)).reshape(1, LANES)
    col = pl.BlockSpec((tm, LANES), lambda i: (i, 0))
    row = pl.BlockSpec((N_GATES, tm), lambda i: (0, i))
    return pl.pallas_call(
        functools.partial(_gates_body, tm=tm),
        out_shape=(jax.ShapeDtypeStruct((N_TOK, LANES), F32), jax.ShapeDtypeStruct((N_TOK, LANES), F32),
                   jax.ShapeDtypeStruct((N_GATES, N_TOK), F32), jax.ShapeDtypeStruct((N_GATES, N_TOK), F32)),
        grid=(N_TOK // tm,),
        in_specs=_mod_specs(tm, 1, 0) + [pl.BlockSpec((D, LANES), lambda i: (0, 0)),
                                         pl.BlockSpec((N_GATES, D), lambda i: (0, 0)),
                                         pl.BlockSpec((1, LANES), lambda i: (0, 0)),
                                         pl.BlockSpec((N_GATES, 1), lambda i: (0, 0))],
        out_specs=(col, col, row, row),
        compiler_params=_params(1),
        name="mlstm_gates",
    )(x, sh, sc, wg_pad, wg.T, b_pad, b_gate.reshape(N_GATES, 1))


def _mlstm_body(*refs, has_init):
    (qf, kf, vf, licf, bcf, lirf, brf, qb, kb, vb, licb, bcb, lirb, brb) = refs[:14]
    refs = refs[14:]
    if has_init:
        c0_ref, n0_ref, m0_ref = refs[:3]
        refs = refs[3:]
    hf_ref, hb_ref, c_ref, n_ref, m_ref = refs

    @pl.when(pl.program_id(1) == 0)
    def _():
        if has_init:
            c_ref[...] = c0_ref[...]
            n_ref[...] = n0_ref[...]
            m_ref[...] = m0_ref[...]
        else:
            c_ref[...] = jnp.zeros_like(c_ref)
            n_ref[...] = jnp.zeros_like(n_ref)
            m_ref[...] = jnp.zeros_like(m_ref)

    t_idx = lax.broadcasted_iota(jnp.int32, (CHUNK, CHUNK), 0)
    s_idx = lax.broadcasted_iota(jnp.int32, (CHUNK, CHUNK), 1)
    nt = (((1,), (1,)), ((), ()))
    for d, (q_ref, k_ref, v_ref, lic_ref, bc_ref, lir_ref, br_ref, h_ref) in enumerate(
            ((qf, kf, vf, licf, bcf, lirf, brf, hf_ref), (qb, kb, vb, licb, bcb, lirb, brb, hb_ref))):
        mask = (s_idx <= t_idx) if d == 0 else (s_idx >= t_idx)
        for h in range(ML_HEADS):
            hs = slice(h * ML_HEAD_DIM, (h + 1) * ML_HEAD_DIM)
            gi, gf = d * 2 * ML_HEADS + h, d * 2 * ML_HEADS + ML_HEADS + h
            q = q_ref[:, hs]
            k = k_ref[:, hs] * (ML_HEAD_DIM ** -0.5)
            v = v_ref[:, hs]
            qh, kh, vh = q.astype(BF16), k.astype(BF16), v.astype(BF16)
            li_c, b_c = lic_ref[:, gi:gi + 1], bc_ref[:, gf:gf + 1]
            li_r, b_r = lir_ref[gi:gi + 1, :], br_ref[gf:gf + 1, :]
            c_st = c_ref[0, d, h]
            n_st = n_ref[0, d, h:h + 1, :]
            m_st = m_ref[0, d, h:h + 1, :][:, 0:1]

            dmat = jnp.where(mask, b_c - b_r + li_r, -jnp.inf)
            inter = b_c + m_st
            m_out = jnp.maximum(inter, jnp.max(dmat, axis=-1, keepdims=True))
            s = lax.dot_general(qh, kh, nt, preferred_element_type=F32) * jnp.exp(dmat - m_out)
            w_inter = jnp.exp(inter - m_out)
            num = (jnp.dot(s.astype(BF16), vh, preferred_element_type=F32)
                   + w_inter * jnp.dot(qh, c_st.astype(BF16), preferred_element_type=F32))
            den = (jnp.sum(s, axis=-1, keepdims=True)
                   + w_inter * jnp.sum(q * n_st, axis=-1, keepdims=True))
            h_ref[:, hs] = num / jnp.maximum(jnp.abs(den), jnp.exp(-m_out))

            b_last = b_r[:, CHUNK - 1:CHUNK] if d == 0 else b_r[:, 0:1]
            g_r = b_last - b_r + li_r
            g_c = b_last - b_c + li_c
            m_new = jnp.maximum(b_last + m_st, jnp.max(g_r, axis=-1, keepdims=True))
            decay = jnp.exp(b_last + m_st - m_new)
            kw = k * jnp.exp(g_c - m_new)
            c_ref[0, d, h] = decay * c_st + jnp.dot(kw.T.astype(BF16), vh, preferred_element_type=F32)
            n_ref[0, d, h:h + 1, :] = decay * n_st + jnp.sum(kw, axis=0, keepdims=True)
            m_ref[0, d, h:h + 1, :] = jnp.broadcast_to(m_new, (1, ML_HEAD_DIM))


def _mlstm(proj, lic, bc, lir, br, *, row0, n_seq, seq_len, init=None):
    nc = seq_len // CHUNK
    base = row0 // CHUNK

    def fwd(b, j):
        return base + b * nc + j

    def bwd(b, j):
        return base + b * nc + (nc - 1 - j)

    def chunk_specs(pos):
        return ([pl.BlockSpec((CHUNK, ML_W), lambda b, j, c=c: (pos(b, j), c)) for c in range(3)]
                + [pl.BlockSpec((CHUNK, LANES), lambda b, j: (pos(b, j), 0))] * 2
                + [pl.BlockSpec((N_GATES, CHUNK), lambda b, j: (0, pos(b, j)))] * 2)

    st_c = pl.BlockSpec((1, 2, ML_HEADS, ML_HEAD_DIM, ML_HEAD_DIM), lambda b, j: (b, 0, 0, 0, 0))
    st_n = pl.BlockSpec((1, 2, ML_HEADS, ML_HEAD_DIM), lambda b, j: (b, 0, 0, 0))
    in_specs = chunk_specs(fwd) + chunk_specs(bwd)
    args = [proj, proj, proj, lic, bc, lir, br] * 2
    if init is not None:
        in_specs += [st_c, st_n, st_n]
        args += list(init)
    rows = n_seq * seq_len
    return pl.pallas_call(
        functools.partial(_mlstm_body, has_init=init is not None),
        out_shape=(jax.ShapeDtypeStruct((rows, ML_W), F32), jax.ShapeDtypeStruct((rows, ML_W), F32),
                   jax.ShapeDtypeStruct((n_seq, 2, ML_HEADS, ML_HEAD_DIM, ML_HEAD_DIM), F32),
                   jax.ShapeDtypeStruct((n_seq, 2, ML_HEADS, ML_HEAD_DIM), F32),
                   jax.ShapeDtypeStruct((n_seq, 2, ML_HEADS, ML_HEAD_DIM), F32)),
        grid=(n_seq, nc),
        in_specs=in_specs,
        out_specs=(pl.BlockSpec((CHUNK, ML_W), lambda b, j: (b * nc + j, 0)),
                   pl.BlockSpec((CHUNK, ML_W), lambda b, j: (b * nc + (nc - 1 - j), 0)),
                   st_c, st_n, st_n),
        compiler_params=_params(2),
        name="mlstm_scan",
    )(*args)


def _ml_post_body(hf_ref, hb_ref, o_ref, g_ref, out_ref):
    h = hf_ref[...] + hb_ref[...]
    gate = jax.nn.sigmoid(o_ref[...]) * g_ref[...]
    for hd in range(ML_HEADS):
        hs = slice(hd * ML_HEAD_DIM, (hd + 1) * ML_HEAD_DIM)
        x = h[:, hs]
        xc = x - jnp.mean(x, axis=-1, keepdims=True)
        var = jnp.mean(xc * xc, axis=-1, keepdims=True)
        out_ref[:, hs] = (gate[:, hs] * (xc * lax.rsqrt(var + RMS_EPS))).astype(BF16)


def _ml_post(hf, hb, proj, norm_g):
    tm = 512
    blk = pl.BlockSpec((tm, ML_W), lambda i: (i, 0))
    return pl.pallas_call(
        _ml_post_body,
        out_shape=jax.ShapeDtypeStruct((N_TOK, ML_W), BF16),
        grid=(N_TOK // tm,),
        in_specs=[blk, blk, pl.BlockSpec((tm, ML_W), lambda i: (i, 3)), pl.BlockSpec((1, ML_W), lambda i: (0, 0))],
        out_specs=blk,
        compiler_params=_params(1),
        name="mlstm_out_norm",
    )(hf, hb, proj, norm_g.reshape(1, ML_W))


def _head_rms(x, gain):
    lane_head = lax.broadcasted_iota(jnp.int32, x.shape, 1) // ATT_HEAD_DIM
    sq = x * x
    ms = jnp.zeros_like(x)
    for hd in range(x.shape[1] // ATT_HEAD_DIM):
        sel = lane_head == hd
        ms = jnp.where(sel, jnp.sum(jnp.where(sel, sq, 0.0), axis=-1, keepdims=True), ms)
    return x * lax.rsqrt(ms * (1.0 / ATT_HEAD_DIM) + RMS_EPS) * gain


def _rope(x, cos, sin_signed):
    w = x.shape[1]
    even = lax.broadcasted_iota(jnp.int32, x.shape, 1) % 2 == 0
    partner = jnp.where(even, pltpu.roll(x, w - 1, 1), pltpu.roll(x, 1, 1))
    return x * cos + partner * sin_signed


def _qk_prep_body(q_ref, k_ref, qg_ref, kg_ref, *refs, rope):
    if rope:
        cq_ref, sq_ref, ck_ref, sk_ref, qo_ref, kn_ref, kr_ref = refs
    else:
        qo_ref, kn_ref, kr_ref = refs
    q = _head_rms(q_ref[...], qg_ref[...])
    k = _head_rms(k_ref[...], kg_ref[...])
    kn_ref[...] = k
    if rope:
        q = _rope(q, cq_ref[...], sq_ref[...])
        k = _rope(k, ck_ref[...], sk_ref[...])
    qo_ref[...] = (q * (ATT_HEAD_DIM ** -0.5)).astype(BF16)
    kr_ref[...] = k.astype(BF16)


def _qk_prep(proj, q_gain, k_gain, *, row0, rows, rope_tabs=None):
    tm = 512
    r0 = row0 // tm
    in_specs = [pl.BlockSpec((tm, ATT_W), lambda i: (r0 + i, 4 * ML_W // ATT_W)),
                pl.BlockSpec((tm, KV_W), lambda i: (r0 + i, (4 * ML_W + ATT_W) // KV_W)),
                pl.BlockSpec((1, ATT_W), lambda i: (0, 0)),
                pl.BlockSpec((1, KV_W), lambda i: (0, 0))]
    args = [proj, proj, jnp.tile(q_gain, ATT_HEADS).reshape(1, ATT_W), jnp.tile(k_gain, ATT_KV_HEADS).reshape(1, KV_W)]
    if rope_tabs is not None:
        per_seq = DEC_SEQ // tm
        in_specs += [pl.BlockSpec((tm, ATT_W), lambda i: (i % per_seq, 0))] * 2
        in_specs += [pl.BlockSpec((tm, KV_W), lambda i: (i % per_seq, 0))] * 2
        args += list(rope_tabs)
    return pl.pallas_call(
        functools.partial(_qk_prep_body, rope=rope_tabs is not None),
        out_shape=(jax.ShapeDtypeStruct((rows, ATT_W), BF16), jax.ShapeDtypeStruct((rows, KV_W), F32),
                   jax.ShapeDtypeStruct((rows, KV_W), BF16)),
        grid=(rows // tm,),
        in_specs=in_specs,
        out_specs=(pl.BlockSpec((tm, ATT_W), lambda i: (i, 0)), pl.BlockSpec((tm, KV_W), lambda i: (i, 0)),
                   pl.BlockSpec((tm, KV_W), lambda i: (i, 0))),
        compiler_params=_params(1),
        name="attn_qk_prep",
    )(*args)


def _rope_tables():
    rows = DEC_SEQ // GRID_W
    axis_dim = ATT_HEAD_DIM // 2
    row = jnp.repeat(jnp.arange(rows, dtype=F32), GRID_W)
    col = (jnp.arange(DEC_SEQ) % GRID_W).astype(F32)
    inv = ROPE_BASE ** (-jnp.arange(axis_dim // 2, dtype=F32) * 2.0 / axis_dim)
    ang = jnp.concatenate([row[:, None] * inv, col[:, None] * inv], axis=-1)
    cos = jnp.repeat(jnp.cos(ang), 2, axis=-1)
    sin = jnp.repeat(jnp.sin(ang), 2, axis=-1) * jnp.tile(jnp.array([-1.0, 1.0], F32), axis_dim)
    return (jnp.tile(cos, (1, ATT_HEADS)), jnp.tile(sin, (1, ATT_HEADS)),
            jnp.tile(cos, (1, ATT_KV_HEADS)), jnp.tile(sin, (1, ATT_KV_HEADS)))


def _attn_body(q_ref, k_ref, v_ref, o_ref):
    q = q_ref[...]
    k = k_ref[0, 0]
    v = v_ref[0, 0]
    q_head = lax.broadcasted_iota(jnp.int32, q.shape, 1) // ATT_HEAD_DIM
    v_head = lax.broadcasted_iota(jnp.int32, v.shape, 1) // ATT_HEAD_DIM
    acc = jnp.zeros(q.shape, F32)
    for g in range(ATT_GROUP):
        qg = jnp.where(q_head == g, q, jnp.zeros_like(q))
        s = lax.dot_general(qg, k, (((1,), (1,)), ((), ())), preferred_element_type=F32)
        e = jnp.exp(s - jnp.max(s, axis=-1, keepdims=True))
        vg = jnp.where(v_head == g, v, jnp.zeros_like(v))
        o = jnp.dot(e.astype(BF16), vg, preferred_element_type=F32)
        acc = acc + o / jnp.sum(e, axis=-1, keepdims=True)
    o_ref[...] = acc.astype(BF16)


def _attention(q, k_rep, v_rep, *, n_seq, seq_len, tq):
    s_len = k_rep.shape[2]
    nq = seq_len // tq
    kv = pl.BlockSpec((1, 1, s_len, GROUP_W), lambda b, kh, i: (b, kh, 0, 0))
    qo = pl.BlockSpec((tq, GROUP_W), lambda b, kh, i: (b * nq + i, kh))
    return pl.pallas_call(
        _attn_body,
        out_shape=jax.ShapeDtypeStruct((n_seq * seq_len, ATT_W), BF16),
        grid=(n_seq, ATT_KV_HEADS, nq),
        in_specs=[qo, kv, kv],
        out_specs=qo,
        compiler_params=_params(3),
        name="attention",
    )(q, k_rep, v_rep)


def _rep_heads(x, n_seq):
    x = x.reshape(n_seq, -1, ATT_KV_HEADS, ATT_HEAD_DIM).transpose(0, 2, 1, 3)
    return jnp.tile(x, (1, 1, 1, ATT_GROUP))


def _split_bf16(x):
    hi = x.astype(BF16)
    return hi, (x - hi.astype(F32)).astype(BF16)


def _dot3(a_hi, a_lo, b_hi, b_lo):
    return (jnp.dot(a_hi, b_hi, preferred_element_type=F32) + jnp.dot(a_lo, b_hi, preferred_element_type=F32)
            + jnp.dot(a_hi, b_lo, preferred_element_type=F32))


def _hy_filter_body(feat_ref, t_ref, w1_ref, b1_ref, w2_ref, b2_ref, fr_ref, w3f_ref, w3b_ref, dl_ref,
                    hsum_ref, hdiff_ref, nyq_ref, z_ref):
    @pl.when(pl.program_id(0) == 0)
    def _():
        z = jnp.dot(feat_ref[...], w1_ref[...], preferred_element_type=F32, precision=HIGHEST) + b1_ref[...]
        z = jnp.sin(fr_ref[0:1, :] * z)
        z = jnp.dot(z, w2_ref[...], preferred_element_type=F32, precision=HIGHEST) + b2_ref[...]
        z_ref[...] = jnp.sin(fr_ref[1:2, :] * z)

    z = z_ref[...]
    window = jnp.exp(-t_ref[...] * dl_ref[...])
    h_f = jnp.dot(z, w3f_ref[...], preferred_element_type=F32, precision=HIGHEST) * window
    h_b = jnp.dot(z, w3b_ref[...], preferred_element_type=F32, precision=HIGHEST) * window
    row = lax.broadcasted_iota(jnp.int32, h_f.shape, 0)
    h_b = jnp.where(row == 0, 0.0, h_b)
    inv = 1.0 / (jnp.sum(jnp.abs(h_f), axis=0, keepdims=True) + jnp.sum(jnp.abs(h_b), axis=0, keepdims=True))
    h_sum = (h_f + h_b) * inv
    hsum_ref[...] = h_sum
    hdiff_ref[...] = (h_f - h_b) * inv
    nyq_ref[...] = jnp.sum(jnp.where(row % 2 == 0, h_sum, -h_sum), axis=0, keepdims=True)


def _hy_filter(seq_len, w1, b1, w2, b2, w3, sin_freq):
    tc = 256
    fw = w1.shape[1]
    t = jnp.arange(seq_len, dtype=F32)[:, None] / seq_len
    bands = jnp.arange(1, HY_BANDS + 1, dtype=F32)[None, :]
    feat = jnp.concatenate([t, jnp.sin(2.0 * math.pi * bands * t), jnp.cos(2.0 * math.pi * bands * t)], axis=-1)
    feat = jnp.pad(feat, ((0, 0), (0, LANES - HY_EMB)))
    deltas = jnp.abs(jnp.linspace(math.log(HY_TARGET) / HY_LONG_PCT, math.log(HY_TARGET) / HY_SHORT_PCT, D,
                                  dtype=F32)).reshape(1, D)
    pad_w = LANES - fw
    full = lambda shape: pl.BlockSpec(shape, lambda j: (0,) * len(shape))
    return pl.pallas_call(
        _hy_filter_body,
        out_shape=(jax.ShapeDtypeStruct((seq_len, D), F32), jax.ShapeDtypeStruct((seq_len, D), F32),
                   jax.ShapeDtypeStruct((1, D), F32)),
        grid=(D // tc,),
        in_specs=[full((seq_len, LANES)), full((seq_len, 1)), full((LANES, LANES)), full((1, LANES)),
                  full((LANES, LANES)), full((1, LANES)), full((2, LANES)),
                  pl.BlockSpec((LANES, tc), lambda j: (0, j)), pl.BlockSpec((LANES, tc), lambda j: (0, D // tc + j)),
                  pl.BlockSpec((1, tc), lambda j: (0, j))],
        out_specs=(pl.BlockSpec((seq_len, tc), lambda j: (0, j)), pl.BlockSpec((seq_len, tc), lambda j: (0, j)),
                   pl.BlockSpec((1, tc), lambda j: (0, j))),
        scratch_shapes=[pltpu.VMEM((seq_len, LANES), F32)],
        compiler_params=_params(1),
        name="hyena_filter",
    )(feat, t, jnp.pad(w1, ((0, LANES - HY_EMB), (0, pad_w))), jnp.pad(b1, (0, pad_w)).reshape(1, LANES),
      jnp.pad(w2, ((0, pad_w), (0, pad_w))), jnp.pad(b2, (0, pad_w)).reshape(1, LANES),
      jnp.pad(sin_freq, ((0, 0), (0, pad_w))), jnp.pad(w3, ((0, pad_w), (0, 0))), jnp.pad(w3, ((0, pad_w), (0, 0))),
      deltas)


def _dft_matrices(seq_len):
    n = 2 * seq_len
    k = lax.broadcasted_iota(jnp.int32, (seq_len, seq_len), 0)
    t = lax.broadcasted_iota(jnp.int32, (seq_len, seq_len), 1)
    ang = ((k * t) % n).astype(F32) * (2.0 * math.pi / n)
    cr, base = jnp.cos(ang), -jnp.sin(ang)
    ci = jnp.where(k == 0, (1 - 2 * (t % 2)).astype(F32), base)
    cit = jnp.where(t == 0, (1 - 2 * (k % 2)).astype(F32), base)
    return _split_bf16(cr) + _split_bf16(ci) + _split_bf16(cit)


def _dft_fwd_body(crh_ref, crl_ref, cih_ref, cil_ref, b1_ref, b2_ref, *refs, nk, tf, mult):
    if mult:
        kr_ref, ki_ref, or_ref, oi_ref, accr_ref, acci_ref = refs
    else:
        or_ref, oi_ref, accr_ref, acci_ref = refs
    kk = pl.program_id(3)
    b1h, b1l = _split_bf16(b1_ref[0])
    b2h, b2l = _split_bf16(b2_ref[0])
    pr = _dot3(crh_ref[...], crl_ref[...], b1h, b1l)
    pi = _dot3(cih_ref[...], cil_ref[...], b2h, b2l)

    @pl.when(kk == 0)
    def _():
        accr_ref[...] = pr
        acci_ref[...] = pi

    @pl.when(kk > 0)
    def _():
        accr_ref[...] += pr
        acci_ref[...] += pi

    @pl.when(kk == nk - 1)
    def _():
        zr, zi = accr_ref[...], acci_ref[...]
        if mult:
            kr, ki = kr_ref[...], ki_ref[...]
            first = (pl.program_id(1) * tf + lax.broadcasted_iota(jnp.int32, zr.shape, 0)) == 0
            or_ref[0] = jnp.where(first, 0.5 * zr * kr, zr * kr - zi * ki)
            oi_ref[0] = jnp.where(first, 0.5 * zi * ki, zr * ki + zi * kr)
        else:
            or_ref[0] = zr
            oi_ref[0] = zi


def _dft_fwd(mats, b1, b2, *, row0, n_seq, seq_len, tf, tk, tn, filt=None):
    crh, crl, cih, cil = mats[:4]
    nk = seq_len // tk
    r0 = row0 // tk
    a_spec = pl.BlockSpec((tf, tk), lambda b, f, c, kk: (f, kk))
    b_spec = pl.BlockSpec((1, tk, tn), lambda b, f, c, kk: (0, r0 + b * nk + kk, c))
    o_spec = pl.BlockSpec((1, tf, tn), lambda b, f, c, kk: (b, f, c))
    in_specs = [a_spec] * 4 + [b_spec] * 2
    args = [crh, crl, cih, cil, b1[None], b2[None]]
    if filt is not None:
        in_specs += [pl.BlockSpec((tf, tn), lambda b, f, c, kk: (f, c))] * 2
        args += list(filt)
    shape = jax.ShapeDtypeStruct((n_seq, seq_len, D), F32)
    return pl.pallas_call(
        functools.partial(_dft_fwd_body, nk=nk, tf=tf, mult=filt is not None),
        out_shape=(shape, shape),
        grid=(n_seq, seq_len // tf, D // tn, nk),
        in_specs=in_specs,
        out_specs=(o_spec, o_spec),
        scratch_shapes=[pltpu.VMEM((tf, tn), F32), pltpu.VMEM((tf, tn), F32)],
        compiler_params=_params(4),
        name="hyena_dft",
    )(*args)


def _dft_inv_body(crh_ref, crl_ref, cth_ref, ctl_ref, yr_ref, yi_ref, z_ref, x0_ref, skip_ref, o_ref, acc_ref,
                  *, nk, scale):
    kk = pl.program_id(3)
    yrh, yrl = _split_bf16(yr_ref[0])
    yih, yil = _split_bf16(yi_ref[0])
    p = _dot3(crh_ref[...], crl_ref[...], yrh, yrl) + _dot3(cth_ref[...], ctl_ref[...], yih, yil)

    @pl.when(kk == 0)
    def _():
        acc_ref[...] = p

    @pl.when(kk > 0)
    def _():
        acc_ref[...] += p

    @pl.when(kk == nk - 1)
    def _():
        z = z_ref[...]
        o_ref[...] = ((acc_ref[...] * scale + z * skip_ref[...]) * x0_ref[...]).astype(BF16)


def _dft_inv(mats, yr, yi, z, x0, skip, *, row0, n_seq, seq_len, tm, tk, tn):
    crh, crl = mats[:2]
    cth, ctl = mats[4:]
    nk = seq_len // tk
    nt = seq_len // tm
    r0 = row0 // tm
    a_spec = pl.BlockSpec((tm, tk), lambda b, i, c, kk: (i, kk))
    y_spec = pl.BlockSpec((1, tk, tn), lambda b, i, c, kk: (b, kk, c))
    tok = pl.BlockSpec((tm, tn), lambda b, i, c, kk: (r0 + b * nt + i, c))
    return pl.pallas_call(
        functools.partial(_dft_inv_body, nk=nk, scale=1.0 / seq_len),
        out_shape=jax.ShapeDtypeStruct((n_seq * seq_len, D), BF16),
        grid=(n_seq, nt, D // tn, nk),
        in_specs=[a_spec] * 4 + [y_spec] * 2 + [tok, tok, pl.BlockSpec((1, tn), lambda b, i, c, kk: (0, c))],
        out_specs=pl.BlockSpec((tm, tn), lambda b, i, c, kk: (b * nt + i, c)),
        scratch_shapes=[pltpu.VMEM((tm, tn), F32)],
        compiler_params=_params(4),
        name="hyena_idft",
    )(crh, crl, cth, ctl, yr, yi, z, x0, skip.reshape(1, D))


def _hy_gate_body(x0_ref, x1_ref, v_ref, w0_ref, w1_ref, wv_ref, b0_ref, b1_ref, bv_ref, z_ref, x0o_ref):
    def conv(u_ref, w_ref, b_ref):
        u = u_ref[...]
        n = u.shape[0]
        row = lax.broadcasted_iota(jnp.int32, u.shape, 0)
        prev = jnp.where(row == 0, 0.0, pltpu.roll(u, 1, 0))
        nxt = jnp.where(row == n - 1, 0.0, pltpu.roll(u, n - 1, 0))
        return prev * w_ref[0:1, :] + u * w_ref[1:2, :] + nxt * w_ref[2:3, :] + b_ref[...]

    x0o_ref[...] = conv(x0_ref, w0_ref, b0_ref)
    z_ref[...] = conv(v_ref, wv_ref, bv_ref) * conv(x1_ref, w1_ref, b1_ref)


def _hy_gate(u, conv_w, conv_b, *, row0, n_seq, seq_len):
    tc = 128
    nb = D // tc
    r0 = row0 // seq_len

    def col(part):
        return [pl.BlockSpec((seq_len, tc), lambda b, c: (r0 + b, part * nb + c)),
                pl.BlockSpec((3, tc), lambda b, c: (0, part * nb + c)),
                pl.BlockSpec((1, tc), lambda b, c: (0, part * nb + c))]

    specs = [col(p) for p in range(3)]
    out = pl.BlockSpec((seq_len, tc), lambda b, c: (b, c))
    shape = jax.ShapeDtypeStruct((n_seq * seq_len, D), F32)
    return pl.pallas_call(
        _hy_gate_body,
        out_shape=(shape, shape),
        grid=(n_seq, nb),
        in_specs=[s[0] for s in specs] + [s[1] for s in specs] + [s[2] for s in specs],
        out_specs=(out, out),
        compiler_params=_params(2),
        name="hyena_short_conv",
    )(u, u, u, conv_w, conv_w, conv_w, conv_b.reshape(1, 3 * D), conv_b.reshape(1, 3 * D), conv_b.reshape(1, 3 * D))


def _router_body(x_ref, sh_ref, sc_ref, w_ref, o_ref):
    h = _modulate(x_ref, sh_ref, sc_ref)
    logits = jnp.dot(h, w_ref[...], preferred_element_type=F32, precision=HIGHEST)
    lane = lax.broadcasted_iota(jnp.int32, logits.shape, 1).astype(F32)
    logits = jnp.where(lane < N_EXPERTS, logits, -jnp.inf)
    e = jnp.exp(logits - jnp.max(logits, axis=-1, keepdims=True))
    p = e / jnp.sum(e, axis=-1, keepdims=True)
    p1 = jnp.max(p, axis=-1, keepdims=True)
    i1 = jnp.min(jnp.where(p == p1, lane, float(LANES)), axis=-1, keepdims=True)
    rest = jnp.where(lane == i1, -1.0, p)
    p2 = jnp.max(rest, axis=-1, keepdims=True)
    i2 = jnp.min(jnp.where(rest == p2, lane, float(LANES)), axis=-1, keepdims=True)
    total = p1 + p2
    o_ref[...] = jnp.where(lane == i1, p1 / total, jnp.where(lane == i2, p2 / total, 0.0))


def _router(x, sh, sc, w_router):
    tm = 512
    return pl.pallas_call(
        _router_body,
        out_shape=jax.ShapeDtypeStruct((N_TOK, LANES), F32),
        grid=(N_TOK // tm,),
        in_specs=_mod_specs(tm, 1, 0) + [pl.BlockSpec((D, LANES), lambda i: (0, 0))],
        out_specs=pl.BlockSpec((tm, LANES), lambda i: (i, 0)),
        compiler_params=_params(1),
        name="moe_router",
    )(x, sh, sc, jnp.pad(w_router, ((0, 0), (0, LANES - N_EXPERTS))))


def _even_mixer(x, sh, sc, gate, ln_g, ln_b, w_in, b_igate, b_fgate, ml_norm_g, q_norm_g, k_norm_g, w_out,
                st_c, st_n, st_m, cache_k, cache_v, rope_tabs):
    splits = (4 * ML_W, 4 * ML_W + N_GATES)
    w_main = jnp.concatenate([w_in[:, :splits[0]], w_in[:, splits[1]:]], axis=1).astype(BF16)
    proj = _mod_matmul(x, sh, sc, [w_main[None]], tm=1024, tn=MAIN_W // 2, out_dtype=F32, name="even_in_proj")[0]
    b_gate = jnp.stack([b_igate, b_fgate], axis=1).reshape(N_GATES)
    lic, bc, lir, br = _gates(x, sh, sc, w_in[:, splits[0]:splits[1]], b_gate)

    hf_c, hb_c, new_c, new_n, new_m = _mlstm(proj, lic, bc, lir, br, row0=0, n_seq=BATCH, seq_len=SEQ)
    init = (st_c, st_n, jnp.broadcast_to(st_m[..., None], st_n.shape))
    hf_s, hb_s, _, _, _ = _mlstm(proj, lic, bc, lir, br, row0=N_CTX, n_seq=DEC_BATCH, seq_len=DEC_SEQ, init=init)
    ml = _ml_post(jnp.concatenate([hf_c, hf_s]), jnp.concatenate([hb_c, hb_s]), proj, ml_norm_g)

    q_c, kn_c, kb_c = _qk_prep(proj, q_norm_g, k_norm_g, row0=0, rows=N_CTX)
    q_s, _, kb_s = _qk_prep(proj, q_norm_g, k_norm_g, row0=N_CTX, rows=N_LAT, rope_tabs=rope_tabs)
    v_all = proj[:, MAIN_W - KV_W:]
    v_c, v_s = v_all[:N_CTX], v_all[N_CTX:]
    att_c = _attention(q_c, _rep_heads(kb_c, BATCH), _rep_heads(v_c.astype(BF16), BATCH),
                       n_seq=BATCH, seq_len=SEQ, tq=SEQ)
    k_lat = jnp.concatenate([kb_s.reshape(DEC_BATCH, DEC_SEQ, KV_W),
                             cache_k.reshape(DEC_BATCH, PAST_LEN, KV_W).astype(BF16)], axis=1)
    v_lat = jnp.concatenate([v_s.reshape(DEC_BATCH, DEC_SEQ, KV_W).astype(BF16),
                             cache_v.reshape(DEC_BATCH, PAST_LEN, KV_W).astype(BF16)], axis=1)
    att_s = _attention(q_s, _rep_heads(k_lat.reshape(-1, KV_W), DEC_BATCH), _rep_heads(v_lat.reshape(-1, KV_W), DEC_BATCH),
                       n_seq=DEC_BATCH, seq_len=DEC_SEQ, tq=256)

    mixed = jnp.concatenate([ml, jnp.concatenate([att_c, att_s])], axis=1)
    x = _matmul_res_ln(mixed[None], w_out.astype(BF16)[None], x, gate, ln_g, ln_b, tm=512, name="even_out_proj")
    new_k = kn_c.reshape(BATCH, SEQ, ATT_KV_HEADS, ATT_HEAD_DIM)
    new_v = v_c.reshape(BATCH, SEQ, ATT_KV_HEADS, ATT_HEAD_DIM)
    return x, new_k, new_v, new_c, new_n, new_m[..., 0]


def _hyena_mixer(x, sh, sc, gate, ln_g, ln_b, w_in, conv_w, conv_b, w1, b1, w2, b2, w3, sin_freq, skip, w_out, dft):
    u = _mod_matmul(x, sh, sc, [w_in.astype(BF16)[None]], tm=1024, tn=512, out_dtype=F32, name="hyena_in_proj")[0]
    ys = []
    for row0, n_seq, seq_len, tiles in ((0, BATCH, SEQ, dict(tf=256, tk=256, tn=1024)),
                                        (N_CTX, DEC_BATCH, DEC_SEQ, dict(tf=1024, tk=512, tn=512))):
        mats = dft[seq_len]
        h_sum, h_diff, nyq = _hy_filter(seq_len, w1, b1, w2, b2, w3, sin_freq)
        kr, ki = _dft_fwd(mats, h_sum, h_diff, row0=0, n_seq=1, seq_len=seq_len, **tiles)
        kr, ki = kr[0], ki[0].at[0].set(nyq[0])
        z, x0 = _hy_gate(u, conv_w, conv_b, row0=row0, n_seq=n_seq, seq_len=seq_len)
        yr, yi = _dft_fwd(mats, z, z, row0=0, n_seq=n_seq, seq_len=seq_len, filt=(kr, ki), **tiles)
        ys.append(_dft_inv(mats, yr, yi, z, x0, skip, row0=0, n_seq=n_seq, seq_len=seq_len,
                           tm=tiles["tf"], tk=tiles["tk"], tn=tiles["tn"]))
    y = jnp.concatenate(ys)
    return _matmul_res_ln(y[None], w_out.astype(BF16)[None], x, gate, ln_g, ln_b, tm=512, name="hyena_out_proj")


def _dense_ffn(x, sh, sc, gate, ln_g, ln_b, w_gate, w_up, w_down):
    act = _mod_matmul(x, sh, sc, [w_gate.astype(BF16)[None], w_up.astype(BF16)[None]],
                      tm=1024, tn=D_FF // 2, out_dtype=BF16, name="ffn_swiglu")
    return _matmul_res_ln(act, w_down.astype(BF16)[None], x, gate, ln_g, ln_b, tm=512, name="ffn_down")


def _moe_ffn(x, sh, sc, gate, ln_g, ln_b, w_router, w_gate, w_up, w_down):
    gates = _router(x, sh, sc, w_router)
    act = _mod_matmul(x, sh, sc, [w_gate.astype(BF16), w_up.astype(BF16)],
                      tm=1024, tn=MOE_D_FF // 7, out_dtype=BF16, name="moe_swiglu")
    return _matmul_res_ln(act, w_down.astype(BF16), x, gate, ln_g, ln_b, tm=512, gates=gates, name="moe_down")


def kernel(x_prompt, x_sample, cache_attn_k, cache_attn_v, state_mlstm_C, state_mlstm_n, state_mlstm_m, c, c_ctx, w_ada, b_ada, ln_g, ln_b, w_in_even, b_igate, b_fgate, ml_norm_g, q_norm_g, k_norm_g, w_out_even, w_ffn_gate, w_ffn_up, w_ffn_down, w_in_hy, hy_conv_w, hy_conv_b, hy_filt_w1, hy_filt_b1, hy_filt_w2, hy_filt_b2, hy_filt_w3, hy_sin_freq, hy_skip, w_out_hy, w_router, w_moe_gate, w_moe_up, w_moe_down):
    x = jnp.concatenate([x_prompt.reshape(N_CTX, D), x_sample.reshape(N_LAT, D)])
    cvec = jnp.concatenate([c_ctx[None], c, jnp.zeros((8 - 1 - DEC_BATCH, D), F32)])
    mods = _ada(cvec, w_ada, b_ada)
    rope_tabs = _rope_tables()
    dft = {SEQ: _dft_matrices(SEQ), DEC_SEQ: _dft_matrices(DEC_SEQ)}
    new_k, new_v, new_c, new_n, new_m = [], [], [], [], []
    for layer in range(DEPTH):
        sh1, sc1, g1, sh2, sc2, g2 = (mods[layer, :, i * D:(i + 1) * D].reshape(8, 1, D) for i in range(6))
        i = layer // 2
        if layer % 2 == 0:
            x, k_c, v_c, st_c, st_n, st_m = _even_mixer(
                x, sh1, sc1, g1, ln_g[layer, 0], ln_b[layer, 0], w_in_even[i], b_igate[i], b_fgate[i], ml_norm_g[i],
                q_norm_g[i], k_norm_g[i], w_out_even[i], state_mlstm_C[:, i], state_mlstm_n[:, i], state_mlstm_m[:, i],
                cache_attn_k[:, i], cache_attn_v[:, i], rope_tabs)
            new_k.append(k_c)
            new_v.append(v_c)
            new_c.append(st_c)
            new_n.append(st_n)
            new_m.append(st_m)
            x = _dense_ffn(x, sh2, sc2, g2, ln_g[layer, 1], ln_b[layer, 1], w_ffn_gate[i], w_ffn_up[i], w_ffn_down[i])
        else:
            x = _hyena_mixer(x, sh1, sc1, g1, ln_g[layer, 0], ln_b[layer, 0], w_in_hy[i], hy_conv_w[i], hy_conv_b[i],
                             hy_filt_w1[i], hy_filt_b1[i], hy_filt_w2[i], hy_filt_b2[i], hy_filt_w3[i], hy_sin_freq[i],
                             hy_skip[i], w_out_hy[i], dft)
            x = _moe_ffn(x, sh2, sc2, g2, ln_g[layer, 1], ln_b[layer, 1], w_router[i], w_moe_gate[i], w_moe_up[i],
                         w_moe_down[i])
    return (x[:N_CTX].reshape(BATCH, SEQ, D), x[N_CTX:].reshape(DEC_BATCH, DEC_SEQ, D),
            jnp.stack(new_k, axis=1), jnp.stack(new_v, axis=1), jnp.stack(new_c, axis=1),
            jnp.stack(new_n, axis=1), jnp.stack(new_m, axis=1))
```

```python
import functools
import math

import jax
import jax.numpy as jnp
from jax import lax
from jax.experimental import pallas as pl
from jax.experimental.pallas import tpu as pltpu

F32 = jnp.float32
BF16 = jnp.bfloat16
HIGHEST = lax.Precision.HIGHEST

D = 1024
BATCH, SEQ = 32, 256
DEC_BATCH, DEC_SEQ = 2, 4096
DEPTH = 4
PAST_LEN = 256
GRID_W = 64
N_CTX = BATCH * SEQ
N_LAT = DEC_BATCH * DEC_SEQ
N_TOK = N_CTX + N_LAT

ML_HEADS, ML_HEAD_DIM = 4, 128
ML_W = ML_HEADS * ML_HEAD_DIM
CHUNK = 128
ATT_HEADS, ATT_KV_HEADS, ATT_HEAD_DIM = 8, 2, 64
ATT_GROUP = ATT_HEADS // ATT_KV_HEADS
ATT_W = ATT_HEADS * ATT_HEAD_DIM
KV_W = ATT_KV_HEADS * ATT_HEAD_DIM
GROUP_W = ATT_GROUP * ATT_HEAD_DIM
ROPE_BASE = 10000.0
N_GATES = 4 * ML_HEADS
MAIN_W = 4 * ML_W + ATT_W + 2 * KV_W

HY_EMB = 33
HY_BANDS = (HY_EMB - 1) // 2
HY_TARGET, HY_SHORT_PCT, HY_LONG_PCT = 1e-2, 0.3, 1.5
D_FF = 2816
N_EXPERTS = 8
MOE_D_FF = 3584
ALPHA = (2 * DEPTH) ** 0.25
LN_EPS = 1e-5
RMS_EPS = 1e-6

LANES = 128
VMEM_LIMIT = 48 * 1024 * 1024


def _params(n_axes, vmem=VMEM_LIMIT):
    return pltpu.CompilerParams(dimension_semantics=("arbitrary",) * n_axes, vmem_limit_bytes=vmem)


def _group_of_row(r):
    return jnp.where(r < N_CTX, 0, 1 + (r - N_CTX) // DEC_SEQ)


def _modulate(x_ref, sh_ref, sc_ref):
    return x_ref[...] * (1.0 + sc_ref[0]) + sh_ref[0]


def _mod_specs(tm, n_axes, row_axis):
    def rows(*ids):
        return (ids[row_axis], 0)

    def grp(*ids):
        return (_group_of_row(ids[row_axis] * tm), 0, 0)

    del n_axes
    return [pl.BlockSpec((tm, D), rows), pl.BlockSpec((1, 1, D), grp), pl.BlockSpec((1, 1, D), grp)]


def _ada_body(c_ref, w_ref, b_ref, o_ref):
    c = c_ref[...]
    s = c * jax.nn.sigmoid(c)
    o_ref[0] = jnp.dot(s, w_ref[0], preferred_element_type=F32, precision=HIGHEST) + b_ref[0]


def _ada(cvec, w_ada, b_ada):
    tn = 1536
    return pl.pallas_call(
        _ada_body,
        out_shape=jax.ShapeDtypeStruct((DEPTH, 8, 6 * D), F32),
        grid=(DEPTH, 6 * D // tn),
        in_specs=[pl.BlockSpec((8, D), lambda l, j: (0, 0)),
                  pl.BlockSpec((1, D, tn), lambda l, j: (l, 0, j)),
                  pl.BlockSpec((1, 1, tn), lambda l, j: (l, 0, j))],
        out_specs=pl.BlockSpec((1, 8, tn), lambda l, j: (l, 0, j)),
        compiler_params=_params(2),
        name="ada_modulation",
    )(cvec, w_ada, b_ada.reshape(DEPTH, 1, 6 * D))


def _mod_mm_body(x_ref, sh_ref, sc_ref, *refs, n_w):
    w_refs, o_ref, h_ref = refs[:n_w], refs[n_w], refs[n_w + 1]

    @pl.when(pl.program_id(2) == 0)
    def _():
        h_ref[...] = _modulate(x_ref, sh_ref, sc_ref).astype(BF16)

    h = h_ref[...]
    if n_w == 1:
        o = jnp.dot(h, w_refs[0][0], preferred_element_type=F32)
    else:
        g = jnp.dot(h, w_refs[0][0], preferred_element_type=F32)
        u = jnp.dot(h, w_refs[1][0], preferred_element_type=F32)
        o = g * jax.nn.sigmoid(g) * u
    o_ref[0] = o.astype(o_ref.dtype)


def _mod_matmul(x, sh, sc, ws, *, tm, tn, out_dtype, name):
    n_e, _, f = ws[0].shape
    return pl.pallas_call(
        functools.partial(_mod_mm_body, n_w=len(ws)),
        out_shape=jax.ShapeDtypeStruct((n_e, N_TOK, f), out_dtype),
        grid=(n_e, N_TOK // tm, f // tn),
        in_specs=_mod_specs(tm, 3, 1) + [pl.BlockSpec((1, D, tn), lambda e, i, j: (e, 0, j)) for _ in ws],
        out_specs=pl.BlockSpec((1, tm, tn), lambda e, i, j: (e, i, j)),
        scratch_shapes=[pltpu.VMEM((tm, D), BF16)],
        compiler_params=_params(3),
        name=name,
    )(x, sh, sc, *ws)


def _mm_res_ln_body(a_ref, w_ref, x_ref, g_ref, lng_ref, lnb_ref, *refs, n_e, use_gates):
    if use_gates:
        gates_ref, o_ref, acc_ref = refs
    else:
        o_ref, acc_ref = refs
    e = pl.program_id(1)
    p = jnp.dot(a_ref[0], w_ref[0], preferred_element_type=F32)
    if use_gates:
        gt = gates_ref[...]
        lane = lax.broadcasted_iota(jnp.int32, gt.shape, 1)
        p = p * jnp.sum(jnp.where(lane == e, gt, 0.0), axis=-1, keepdims=True)

    @pl.when(e == 0)
    def _():
        acc_ref[...] = p

    @pl.when(e > 0)
    def _():
        acc_ref[...] += p

    @pl.when(e == n_e - 1)
    def _():
        y = ALPHA * x_ref[...] + g_ref[0] * acc_ref[...]
        mu = jnp.mean(y, axis=-1, keepdims=True)
        yc = y - mu
        var = jnp.mean(yc * yc, axis=-1, keepdims=True)
        o_ref[...] = yc * lax.rsqrt(var + LN_EPS) * lng_ref[...] + lnb_ref[...]


def _matmul_res_ln(a, w, x, gate, ln_g, ln_b, *, tm, gates=None, name):
    n_e, _, k = a.shape
    use_gates = gates is not None
    in_specs = [pl.BlockSpec((1, tm, k), lambda i, e: (e, i, 0)),
                pl.BlockSpec((1, k, D), lambda i, e: (e, 0, 0)),
                pl.BlockSpec((tm, D), lambda i, e: (i, 0)),
                pl.BlockSpec((1, 1, D), lambda i, e: (_group_of_row(i * tm), 0, 0)),
                pl.BlockSpec((1, D), lambda i, e: (0, 0)),
                pl.BlockSpec((1, D), lambda i, e: (0, 0))]
    args = [a, w, x, gate, ln_g.reshape(1, D), ln_b.reshape(1, D)]
    if use_gates:
        in_specs.append(pl.BlockSpec((tm, LANES), lambda i, e: (i, 0)))
        args.append(gates)
    return pl.pallas_call(
        functools.partial(_mm_res_ln_body, n_e=n_e, use_gates=use_gates),
        out_shape=jax.ShapeDtypeStruct((N_TOK, D), F32),
        grid=(N_TOK // tm, n_e),
        in_specs=in_specs,
        out_specs=pl.BlockSpec((tm, D), lambda i, e: (i, 0)),
        scratch_shapes=[pltpu.VMEM((tm, D), F32)],
        compiler_params=_params(2),
        name=name,
    )(*args)


def _log_sigmoid(x):
    return jnp.minimum(x, 0.0) - jnp.log(1.0 + jnp.exp(-jnp.abs(x)))


def _gates_body(x_ref, sh_ref, sc_ref, wg_ref, wgt_ref, b_ref, bt_ref,
                lic_ref, bc_ref, lir_ref, br_ref, *, tm):
    h = _modulate(x_ref, sh_ref, sc_ref)
    g = jnp.dot(h, wg_ref[...], preferred_element_type=F32, precision=HIGHEST) + b_ref[...]
    gt = lax.dot_general(wgt_ref[...], h, (((1,), (1,)), ((), ())),
                         preferred_element_type=F32, precision=HIGHEST) + bt_ref[...]
    lic_ref[...] = g
    lir_ref[...] = gt
    lf, lft = _log_sigmoid(g), _log_sigmoid(gt)
    r = lax.broadcasted_iota(jnp.int32, (CHUNK, CHUNK), 0)
    c = lax.broadcasted_iota(jnp.int32, (CHUNK, CHUNK), 1)
    tri_l = (c <= r).astype(F32)
    tri_u = (c >= r).astype(F32)
    fwd_col = lax.broadcasted_iota(jnp.int32, (CHUNK, LANES), 1) < 2 * ML_HEADS
    fwd_row = lax.broadcasted_iota(jnp.int32, (N_GATES, CHUNK), 0) < 2 * ML_HEADS
    for ch in range(tm // CHUNK):
        sl = slice(ch * CHUNK, (ch + 1) * CHUNK)
        lfc, lftc = lf[sl, :], lft[:, sl]
        cum_f = jnp.dot(tri_l, lfc, preferred_element_type=F32, precision=HIGHEST)
        cum_b = jnp.dot(tri_u, lfc, preferred_element_type=F32, precision=HIGHEST)
        bc_ref[sl, :] = jnp.where(fwd_col, cum_f, cum_b)
        cum_f = jnp.dot(lftc, tri_u, preferred_element_type=F32, precision=HIGHEST)
        cum_b = jnp.dot(lftc, tri_l, preferred_element_type=F32, precision=HIGHEST)
        br_ref[:, sl] = jnp.where(fwd_row, cum_f, cum_b)


def _gates(x, sh, sc, wg, b_gate):
    tm = 256
    wg_pad = jnp.pad(wg, ((0, 0), (0, LANES - N_GATES)))
    b_pad = jnp.pad(b_gate, (0, LANES - N_GATES)).reshape(1, LANES)
    col = pl.BlockSpec((tm, LANES), lambda i: (i, 0))
    row = pl.BlockSpec((N_GATES, tm), lambda i: (0, i))
    return pl.pallas_call(
        functools.partial(_gates_body, tm=tm),
        out_shape=(jax.ShapeDtypeStruct((N_TOK, LANES), F32), jax.ShapeDtypeStruct((N_TOK, LANES), F32),
                   jax.ShapeDtypeStruct((N_GATES, N_TOK), F32), jax.ShapeDtypeStruct((N_GATES, N_TOK), F32)),
        grid=(N_TOK // tm,),
        in_specs=_mod_specs(tm, 1, 0) + [pl.BlockSpec((D, LANES), lambda i: (0, 0)),
                                         pl.BlockSpec((N_GATES, D), lambda i: (0, 0)),
                                         pl.BlockSpec((1, LANES), lambda i: (0, 0)),
                                         pl.BlockSpec((N_GATES, 1), lambda i: (0, 0))],
        out_specs=(col, col, row, row),
        compiler_params=_params(1),
        name="mlstm_gates",
    )(x, sh, sc, wg_pad, wg.T, b_pad, b_gate.reshape(N_GATES, 1))


def _mlstm_body(*refs, has_init):
    (qf, kf, vf, licf, bcf, lirf, brf, qb, kb, vb, licb, bcb, lirb, brb) = refs[:14]
    refs = refs[14:]
    if has_init:
        c0_ref, n0_ref, m0_ref = refs[:3]
        refs = refs[3:]
    hf_ref, hb_ref, c_ref, n_ref, m_ref = refs

    @pl.when(pl.program_id(1) == 0)
    def _():
        if has_init:
            c_ref[...] = c0_ref[...]
            n_ref[...] = n0_ref[...]
            m_ref[...] = m0_ref[...]
        else:
            c_ref[...] = jnp.zeros_like(c_ref)
            n_ref[...] = jnp.zeros_like(n_ref)
            m_ref[...] = jnp.zeros_like(m_ref)

    t_idx = lax.broadcasted_iota(jnp.int32, (CHUNK, CHUNK), 0)
    s_idx = lax.broadcasted_iota(jnp.int32, (CHUNK, CHUNK), 1)
    nt = (((1,), (1,)), ((), ()))
    for d, (q_ref, k_ref, v_ref, lic_ref, bc_ref, lir_ref, br_ref, h_ref) in enumerate(
            ((qf, kf, vf, licf, bcf, lirf, brf, hf_ref), (qb, kb, vb, licb, bcb, lirb, brb, hb_ref))):
        mask = (s_idx <= t_idx) if d == 0 else (s_idx >= t_idx)
        for h in range(ML_HEADS):
            hs = slice(h * ML_HEAD_DIM, (h + 1) * ML_HEAD_DIM)
            gi, gf = d * 2 * ML_HEADS + h, d * 2 * ML_HEADS + ML_HEADS + h
            q = q_ref[:, hs]
            k = k_ref[:, hs] * (ML_HEAD_DIM ** -0.5)
            v = v_ref[:, hs]
            qh, kh, vh = q.astype(BF16), k.astype(BF16), v.astype(BF16)
            li_c, b_c = lic_ref[:, gi:gi + 1], bc_ref[:, gf:gf + 1]
            li_r, b_r = lir_ref[gi:gi + 1, :], br_ref[gf:gf + 1, :]
            c_st = c_ref[0, d, h]
            n_st = n_ref[0, d, h:h + 1, :]
            m_st = m_ref[0, d, h:h + 1, :][:, 0:1]

            dmat = jnp.where(mask, b_c - b_r + li_r, -jnp.inf)
            inter = b_c + m_st
            m_out = jnp.maximum(inter, jnp.max(dmat, axis=-1, keepdims=True))
            s = lax.dot_general(qh, kh, nt, preferred_element_type=F32) * jnp.exp(dmat - m_out)
            w_inter = jnp.exp(inter - m_out)
            num = (jnp.dot(s.astype(BF16), vh, preferred_element_type=F32)
                   + w_inter * jnp.dot(qh, c_st.astype(BF16), preferred_element_type=F32))
            den = (jnp.sum(s, axis=-1, keepdims=True)
                   + w_inter * jnp.sum(q * n_st, axis=-1, keepdims=True))
            h_ref[:, hs] = num / jnp.maximum(jnp.abs(den), jnp.exp(-m_out))

            b_last = b_r[:, CHUNK - 1:CHUNK] if d == 0 else b_r[:, 0:1]
            g_r = b_last - b_r + li_r
            g_c = b_last - b_c + li_c
            m_new = jnp.maximum(b_last + m_st, jnp.max(g_r, axis=-1, keepdims=True))
            decay = jnp.exp(b_last + m_st - m_new)
            kw = k * jnp.exp(g_c - m_new)
            c_ref[0, d, h] = decay * c_st + jnp.dot(kw.T.astype(BF16), vh, preferred_element_type=F32)
            n_ref[0, d, h:h + 1, :] = decay * n_st + jnp.sum(kw, axis=0, keepdims=True)
            m_ref[0, d, h:h + 1, :] = jnp.broadcast_to(m_new, (1, ML_HEAD_DIM))


def _mlstm(proj, lic, bc, lir, br, *, row0, n_seq, seq_len, init=None):
    nc = seq_len // CHUNK
    base = row0 // CHUNK

    def fwd(b, j):
        return base + b * nc + j

    def bwd(b, j):
        return base + b * nc + (nc - 1 - j)

    def chunk_specs(pos):
        return ([pl.BlockSpec((CHUNK, ML_W), lambda b, j, c=c: (pos(b, j), c)) for c in range(3)]
                + [pl.BlockSpec((CHUNK, LANES), lambda b, j: (pos(b, j), 0))] * 2
                + [pl.BlockSpec((N_GATES, CHUNK), lambda b, j: (0, pos(b, j)))] * 2)

    st_c = pl.BlockSpec((1, 2, ML_HEADS, ML_HEAD_DIM, ML_HEAD_DIM), lambda b, j: (b, 0, 0, 0, 0))
    st_n = pl.BlockSpec((1, 2, ML_HEADS, ML_HEAD_DIM), lambda b, j: (b, 0, 0, 0))
    in_specs = chunk_specs(fwd) + chunk_specs(bwd)
    args = [proj, proj, proj, lic, bc, lir, br] * 2
    if init is not None:
        in_specs += [st_c, st_n, st_n]
        args += list(init)
    rows = n_seq * seq_len
    return pl.pallas_call(
        functools.partial(_mlstm_body, has_init=init is not None),
        out_shape=(jax.ShapeDtypeStruct((rows, ML_W), F32), jax.ShapeDtypeStruct((rows, ML_W), F32),
                   jax.ShapeDtypeStruct((n_seq, 2, ML_HEADS, ML_HEAD_DIM, ML_HEAD_DIM), F32),
                   jax.ShapeDtypeStruct((n_seq, 2, ML_HEADS, ML_HEAD_DIM), F32),
                   jax.ShapeDtypeStruct((n_seq, 2, ML_HEADS, ML_HEAD_DIM), F32)),
        grid=(n_seq, nc),
        in_specs=in_specs,
        out_specs=(pl.BlockSpec((CHUNK, ML_W), lambda b, j: (b * nc + j, 0)),
                   pl.BlockSpec((CHUNK, ML_W), lambda b, j: (b * nc + (nc - 1 - j), 0)),
                   st_c, st_n, st_n),
        compiler_params=_params(2),
        name="mlstm_scan",
    )(*args)


def _ml_post_body(hf_ref, hb_ref, o_ref, g_ref, out_ref):
    h = hf_ref[...] + hb_ref[...]
    gate = jax.nn.sigmoid(o_ref[...]) * g_ref[...]
    for hd in range(ML_HEADS):
        hs = slice(hd * ML_HEAD_DIM, (hd + 1) * ML_HEAD_DIM)
        x = h[:, hs]
        xc = x - jnp.mean(x, axis=-1, keepdims=True)
        var = jnp.mean(xc * xc, axis=-1, keepdims=True)
        out_ref[:, hs] = (gate[:, hs] * (xc * lax.rsqrt(var + RMS_EPS))).astype(BF16)


def _ml_post(hf, hb, proj, norm_g):
    tm = 512
    blk = pl.BlockSpec((tm, ML_W), lambda i: (i, 0))
    return pl.pallas_call(
        _ml_post_body,
        out_shape=jax.ShapeDtypeStruct((N_TOK, ML_W), BF16),
        grid=(N_TOK // tm,),
        in_specs=[blk, blk, pl.BlockSpec((tm, ML_W), lambda i: (i, 3)), pl.BlockSpec((1, ML_W), lambda i: (0, 0))],
        out_specs=blk,
        compiler_params=_params(1),
        name="mlstm_out_norm",
    )(hf, hb, proj, norm_g.reshape(1, ML_W))


def _head_rms(x, gain):
    lane_head = lax.broadcasted_iota(jnp.int32, x.shape, 1) // ATT_HEAD_DIM
    sq = x * x
    ms = jnp.zeros_like(x)
    for hd in range(x.shape[1] // ATT_HEAD_DIM):
        sel = lane_head == hd
        ms = jnp.where(sel, jnp.sum(jnp.where(sel, sq, 0.0), axis=-1, keepdims=True), ms)
    return x * lax.rsqrt(ms * (1.0 / ATT_HEAD_DIM) + RMS_EPS) * gain


def _rope(x, cos, sin_signed):
    w = x.shape[1]
    even = lax.broadcasted_iota(jnp.int32, x.shape, 1) % 2 == 0
    partner = jnp.where(even, pltpu.roll(x, w - 1, 1), pltpu.roll(x, 1, 1))
    return x * cos + partner * sin_signed


def _qk_prep_body(q_ref, k_ref, qg_ref, kg_ref, *refs, rope):
    if rope:
        cq_ref, sq_ref, ck_ref, sk_ref, qo_ref, kn_ref, kr_ref = refs
    else:
        qo_ref, kn_ref, kr_ref = refs
    q = _head_rms(q_ref[...], qg_ref[...])
    k = _head_rms(k_ref[...], kg_ref[...])
    kn_ref[...] = k
    if rope:
        q = _rope(q, cq_ref[...], sq_ref[...])
        k = _rope(k, ck_ref[...], sk_ref[...])
    qo_ref[...] = (q * (ATT_HEAD_DIM ** -0.5)).astype(BF16)
    kr_ref[...] = k.astype(BF16)


def _qk_prep(proj, q_gain, k_gain, *, row0, rows, rope_tabs=None):
    tm = 512
    r0 = row0 // tm
    in_specs = [pl.BlockSpec((tm, ATT_W), lambda i: (r0 + i, 4 * ML_W // ATT_W)),
                pl.BlockSpec((tm, KV_W), lambda i: (r0 + i, (4 * ML_W + ATT_W) // KV_W)),
                pl.BlockSpec((1, ATT_W), lambda i: (0, 0)),
                pl.BlockSpec((1, KV_W), lambda i: (0, 0))]
    args = [proj, proj, jnp.tile(q_gain, ATT_HEADS).reshape(1, ATT_W), jnp.tile(k_gain, ATT_KV_HEADS).reshape(1, KV_W)]
    if rope_tabs is not None:
        per_seq = DEC_SEQ // tm
        in_specs += [pl.BlockSpec((tm, ATT_W), lambda i: (i % per_seq, 0))] * 2
        in_specs += [pl.BlockSpec((tm, KV_W), lambda i: (i % per_seq, 0))] * 2
        args += list(rope_tabs)
    return pl.pallas_call(
        functools.partial(_qk_prep_body, rope=rope_tabs is not None),
        out_shape=(jax.ShapeDtypeStruct((rows, ATT_W), BF16), jax.ShapeDtypeStruct((rows, KV_W), F32),
                   jax.ShapeDtypeStruct((rows, KV_W), BF16)),
        grid=(rows // tm,),
        in_specs=in_specs,
        out_specs=(pl.BlockSpec((tm, ATT_W), lambda i: (i, 0)), pl.BlockSpec((tm, KV_W), lambda i: (i, 0)),
                   pl.BlockSpec((tm, KV_W), lambda i: (i, 0))),
        compiler_params=_params(1),
        name="attn_qk_prep",
    )(*args)


def _rope_tables():
    rows = DEC_SEQ // GRID_W
    axis_dim = ATT_HEAD_DIM // 2
    row = jnp.repeat(jnp.arange(rows, dtype=F32), GRID_W)
    col = (jnp.arange(DEC_SEQ) % GRID_W).astype(F32)
    inv = ROPE_BASE ** (-jnp.arange(axis_dim // 2, dtype=F32) * 2.0 / axis_dim)
    ang = jnp.concatenate([row[:, None] * inv, col[:, None] * inv], axis=-1)
    cos = jnp.repeat(jnp.cos(ang), 2, axis=-1)
    sin = jnp.repeat(jnp.sin(ang), 2, axis=-1) * jnp.tile(jnp.array([-1.0, 1.0], F32), axis_dim)
    return (jnp.tile(cos, (1, ATT_HEADS)), jnp.tile(sin, (1, ATT_HEADS)),
            jnp.tile(cos, (1, ATT_KV_HEADS)), jnp.tile(sin, (1, ATT_KV_HEADS)))


def _attn_body(q_ref, k_ref, v_ref, o_ref):
    q = q_ref[...]
    k = k_ref[0, 0]
    v = v_ref[0, 0]
    q_head = lax.broadcasted_iota(jnp.int32, q.shape, 1) // ATT_HEAD_DIM
    v_head = lax.broadcasted_iota(jnp.int32, v.shape, 1) // ATT_HEAD_DIM
    acc = jnp.zeros(q.shape, F32)
    for g in range(ATT_GROUP):
        qg = jnp.where(q_head == g, q, jnp.zeros_like(q))
        s = lax.dot_general(qg, k, (((1,), (1,)), ((), ())), preferred_element_type=F32)
        e = jnp.exp(s - jnp.max(s, axis=-1, keepdims=True))
        vg = jnp.where(v_head == g, v, jnp.zeros_like(v))
        o = jnp.dot(e.astype(BF16), vg, preferred_element_type=F32)
        acc = acc + o / jnp.sum(e, axis=-1, keepdims=True)
    o_ref[...] = acc.astype(BF16)


def _attention(q, k_rep, v_rep, *, n_seq, seq_len, tq):
    s_len = k_rep.shape[2]
    nq = seq_len // tq
    kv = pl.BlockSpec((1, 1, s_len, GROUP_W), lambda b, kh, i: (b, kh, 0, 0))
    qo = pl.BlockSpec((tq, GROUP_W), lambda b, kh, i: (b * nq + i, kh))
    return pl.pallas_call(
        _attn_body,
        out_shape=jax.ShapeDtypeStruct((n_seq * seq_len, ATT_W), BF16),
        grid=(n_seq, ATT_KV_HEADS, nq),
        in_specs=[qo, kv, kv],
        out_specs=qo,
        compiler_params=_params(3),
        name="attention",
    )(q, k_rep, v_rep)


def _rep_heads(x, n_seq):
    x = x.reshape(n_seq, -1, ATT_KV_HEADS, ATT_HEAD_DIM).transpose(0, 2, 1, 3)
    return jnp.tile(x, (1, 1, 1, ATT_GROUP))


def _split_bf16(x):
    hi = x.astype(BF16)
    return hi, (x - hi.astype(F32)).astype(BF16)


def _dot3(a_hi, a_lo, b_hi, b_lo):
    return (jnp.dot(a_hi, b_hi, preferred_element_type=F32) + jnp.dot(a_lo, b_hi, preferred_element_type=F32)
            + jnp.dot(a_hi, b_lo, preferred_element_type=F32))


def _hy_filter_body(feat_ref, t_ref, w1_ref, b1_ref, w2_ref, b2_ref, fr_ref, w3f_ref, w3b_ref, dl_ref,
                    hsum_ref, hdiff_ref, nyq_ref, z_ref):
    @pl.when(pl.program_id(0) == 0)
    def _():
        z = jnp.dot(feat_ref[...], w1_ref[...], preferred_element_type=F32, precision=HIGHEST) + b1_ref[...]
        z = jnp.sin(fr_ref[0:1, :] * z)
        z = jnp.dot(z, w2_ref[...], preferred_element_type=F32, precision=HIGHEST) + b2_ref[...]
        z_ref[...] = jnp.sin(fr_ref[1:2, :] * z)

    z = z_ref[...]
    window = jnp.exp(-t_ref[...] * dl_ref[...])
    h_f = jnp.dot(z, w3f_ref[...], preferred_element_type=F32, precision=HIGHEST) * window
    h_b = jnp.dot(z, w3b_ref[...], preferred_element_type=F32, precision=HIGHEST) * window
    row = lax.broadcasted_iota(jnp.int32, h_f.shape, 0)
    h_b = jnp.where(row == 0, 0.0, h_b)
    inv = 1.0 / (jnp.sum(jnp.abs(h_f), axis=0, keepdims=True) + jnp.sum(jnp.abs(h_b), axis=0, keepdims=True))
    h_sum = (h_f + h_b) * inv
    hsum_ref[...] = h_sum
    hdiff_ref[...] = (h_f - h_b) * inv
    nyq_ref[...] = jnp.sum(jnp.where(row % 2 == 0, h_sum, -h_sum), axis=0, keepdims=True)


def _hy_filter(seq_len, w1, b1, w2, b2, w3, sin_freq):
    tc = 256
    fw = w1.shape[1]
    t = jnp.arange(seq_len, dtype=F32)[:, None] / seq_len
    bands = jnp.arange(1, HY_BANDS + 1, dtype=F32)[None, :]
    feat = jnp.concatenate([t, jnp.sin(2.0 * math.pi * bands * t), jnp.cos(2.0 * math.pi * bands * t)], axis=-1)
    feat = jnp.pad(feat, ((0, 0), (0, LANES - HY_EMB)))
    deltas = jnp.abs(jnp.linspace(math.log(HY_TARGET) / HY_LONG_PCT, math.log(HY_TARGET) / HY_SHORT_PCT, D,
                                  dtype=F32)).reshape(1, D)
    pad_w = LANES - fw
    full = lambda shape: pl.BlockSpec(shape, lambda j: (0,) * len(shape))
    return pl.pallas_call(
        _hy_filter_body,
        out_shape=(jax.ShapeDtypeStruct((seq_len, D), F32), jax.ShapeDtypeStruct((seq_len, D), F32),
                   jax.ShapeDtypeStruct((1, D), F32)),
        grid=(D // tc,),
        in_specs=[full((seq_len, LANES)), full((seq_len, 1)), full((LANES, LANES)), full((1, LANES)),
                  full((LANES, LANES)), full((1, LANES)), full((2, LANES)),
                  pl.BlockSpec((LANES, tc), lambda j: (0, j)), pl.BlockSpec((LANES, tc), lambda j: (0, D // tc + j)),
                  pl.BlockSpec((1, tc), lambda j: (0, j))],
        out_specs=(pl.BlockSpec((seq_len, tc), lambda j: (0, j)), pl.BlockSpec((seq_len, tc), lambda j: (0, j)),
                   pl.BlockSpec((1, tc), lambda j: (0, j))),
        scratch_shapes=[pltpu.VMEM((seq_len, LANES), F32)],
        compiler_params=_params(1),
        name="hyena_filter",
    )(feat, t, jnp.pad(w1, ((0, LANES - HY_EMB), (0, pad_w))), jnp.pad(b1, (0, pad_w)).reshape(1, LANES),
      jnp.pad(w2, ((0, pad_w), (0, pad_w))), jnp.pad(b2, (0, pad_w)).reshape(1, LANES),
      jnp.pad(sin_freq, ((0, 0), (0, pad_w))), jnp.pad(w3, ((0, pad_w), (0, 0))), jnp.pad(w3, ((0, pad_w), (0, 0))),
      deltas)


def _dft_matrices(seq_len):
    n = 2 * seq_len
    k = lax.broadcasted_iota(jnp.int32, (seq_len, seq_len), 0)
    t = lax.broadcasted_iota(jnp.int32, (seq_len, seq_len), 1)
    ang = ((k * t) % n).astype(F32) * (2.0 * math.pi / n)
    cr, base = jnp.cos(ang), -jnp.sin(ang)
    ci = jnp.where(k == 0, (1 - 2 * (t % 2)).astype(F32), base)
    cit = jnp.where(t == 0, (1 - 2 * (k % 2)).astype(F32), base)
    return _split_bf16(cr) + _split_bf16(ci) + _split_bf16(cit)


def _dft_fwd_body(crh_ref, crl_ref, cih_ref, cil_ref, b1_ref, b2_ref, *refs, nk, tf, mult):
    if mult:
        kr_ref, ki_ref, or_ref, oi_ref, accr_ref, acci_ref = refs
    else:
        or_ref, oi_ref, accr_ref, acci_ref = refs
    kk = pl.program_id(3)
    b1h, b1l = _split_bf16(b1_ref[0])
    b2h, b2l = _split_bf16(b2_ref[0])
    pr = _dot3(crh_ref[...], crl_ref[...], b1h, b1l)
    pi = _dot3(cih_ref[...], cil_ref[...], b2h, b2l)

    @pl.when(kk == 0)
    def _():
        accr_ref[...] = pr
        acci_ref[...] = pi

    @pl.when(kk > 0)
    def _():
        accr_ref[...] += pr
        acci_ref[...] += pi

    @pl.when(kk == nk - 1)
    def _():
        zr, zi = accr_ref[...], acci_ref[...]
        if mult:
            kr, ki = kr_ref[...], ki_ref[...]
            first = (pl.program_id(1) * tf + lax.broadcasted_iota(jnp.int32, zr.shape, 0)) == 0
            or_ref[0] = jnp.where(first, 0.5 * zr * kr, zr * kr - zi * ki)
            oi_ref[0] = jnp.where(first, 0.5 * zi * ki, zr * ki + zi * kr)
        else:
            or_ref[0] = zr
            oi_ref[0] = zi


def _dft_fwd(mats, b1, b2, *, row0, n_seq, seq_len, tf, tk, tn, filt=None):
    crh, crl, cih, cil = mats[:4]
    nk = seq_len // tk
    r0 = row0 // tk
    a_spec = pl.BlockSpec((tf, tk), lambda b, f, c, kk: (f, kk))
    b_spec = pl.BlockSpec((1, tk, tn), lambda b, f, c, kk: (0, r0 + b * nk + kk, c))
    o_spec = pl.BlockSpec((1, tf, tn), lambda b, f, c, kk: (b, f, c))
    in_specs = [a_spec] * 4 + [b_spec] * 2
    args = [crh, crl, cih, cil, b1[None], b2[None]]
    if filt is not None:
        in_specs += [pl.BlockSpec((tf, tn), lambda b, f, c, kk: (f, c))] * 2
        args += list(filt)
    shape = jax.ShapeDtypeStruct((n_seq, seq_len, D), F32)
    return pl.pallas_call(
        functools.partial(_dft_fwd_body, nk=nk, tf=tf, mult=filt is not None),
        out_shape=(shape, shape),
        grid=(n_seq, seq_len // tf, D // tn, nk),
        in_specs=in_specs,
        out_specs=(o_spec, o_spec),
        scratch_shapes=[pltpu.VMEM((tf, tn), F32), pltpu.VMEM((tf, tn), F32)],
        compiler_params=_params(4),
        name="hyena_dft",
    )(*args)


def _dft_inv_body(crh_ref, crl_ref, cth_ref, ctl_ref, yr_ref, yi_ref, z_ref, x0_ref, skip_ref, o_ref, acc_ref,
                  *, nk, scale):
    kk = pl.program_id(3)
    yrh, yrl = _split_bf16(yr_ref[0])
    yih, yil = _split_bf16(yi_ref[0])
    p = _dot3(crh_ref[...], crl_ref[...], yrh, yrl) + _dot3(cth_ref[...], ctl_ref[...], yih, yil)

    @pl.when(kk == 0)
    def _():
        acc_ref[...] = p

    @pl.when(kk > 0)
    def _():
        acc_ref[...] += p

    @pl.when(kk == nk - 1)
    def _():
        z = z_ref[...]
        o_ref[...] = ((acc_ref[...] * scale + z * skip_ref[...]) * x0_ref[...]).astype(BF16)


def _dft_inv(mats, yr, yi, z, x0, skip, *, row0, n_seq, seq_len, tm, tk, tn):
    crh, crl = mats[:2]
    cth, ctl = mats[4:]
    nk = seq_len // tk
    nt = seq_len // tm
    r0 = row0 // tm
    a_spec = pl.BlockSpec((tm, tk), lambda b, i, c, kk: (i, kk))
    y_spec = pl.BlockSpec((1, tk, tn), lambda b, i, c, kk: (b, kk, c))
    tok = pl.BlockSpec((tm, tn), lambda b, i, c, kk: (r0 + b * nt + i, c))
    return pl.pallas_call(
        functools.partial(_dft_inv_body, nk=nk, scale=1.0 / seq_len),
        out_shape=jax.ShapeDtypeStruct((n_seq * seq_len, D), BF16),
        grid=(n_seq, nt, D // tn, nk),
        in_specs=[a_spec] * 4 + [y_spec] * 2 + [tok, tok, pl.BlockSpec((1, tn), lambda b, i, c, kk: (0, c))],
        out_specs=pl.BlockSpec((tm, tn), lambda b, i, c, kk: (b * nt + i, c)),
        scratch_shapes=[pltpu.VMEM((tm, tn), F32)],
        compiler_params=_params(4),
        name="hyena_idft",
    )(crh, crl, cth, ctl, yr, yi, z, x0, skip.reshape(1, D))


def _hy_gate_body(x0_ref, x1_ref, v_ref, w0_ref, w1_ref, wv_ref, b0_ref, b1_ref, bv_ref, z_ref, x0o_ref):
    def conv(u_ref, w_ref, b_ref):
        u = u_ref[...]
        n = u.shape[0]
        row = lax.broadcasted_iota(jnp.int32, u.shape, 0)
        prev = jnp.where(row == 0, 0.0, pltpu.roll(u, 1, 0))
        nxt = jnp.where(row == n - 1, 0.0, pltpu.roll(u, n - 1, 0))
        return prev * w_ref[0:1, :] + u * w_ref[1:2, :] + nxt * w_ref[2:3, :] + b_ref[...]

    x0o_ref[...] = conv(x0_ref, w0_ref, b0_ref)
    z_ref[...] = conv(v_ref, wv_ref, bv_ref) * conv(x1_ref, w1_ref, b1_ref)


def _hy_gate(u, conv_w, conv_b, *, row0, n_seq, seq_len):
    tc = 128
    nb = D // tc
    r0 = row0 // seq_len

    def col(part):
        return [pl.BlockSpec((seq_len, tc), lambda b, c: (r0 + b, part * nb + c)),
                pl.BlockSpec((3, tc), lambda b, c: (0, part * nb + c)),
                pl.BlockSpec((1, tc), lambda b, c: (0, part * nb + c))]

    specs = [col(p) for p in range(3)]
    out = pl.BlockSpec((seq_len, tc), lambda b, c: (b, c))
    shape = jax.ShapeDtypeStruct((n_seq * seq_len, D), F32)
    return pl.pallas_call(
        _hy_gate_body,
        out_shape=(shape, shape),
        grid=(n_seq, nb),
        in_specs=[s[0] for s in specs] + [s[1] for s in specs] + [s[2] for s in specs],
        out_specs=(out, out),
        compiler_params=_params(2),
        name="hyena_short_conv",
    )(u, u, u, conv_w, conv_w, conv_w, conv_b.reshape(1, 3 * D), conv_b.reshape(1, 3 * D), conv_b.reshape(1, 3 * D))


ROW_TILE = D // LANES
ROUTER_TM = 512
EXPERT_TM = 512
N_SLOTS = 2 * N_TOK + N_EXPERTS * EXPERT_TM
N_SLOT_TILES = N_SLOTS // EXPERT_TM
INFO_E1, INFO_E2, INFO_R1, INFO_R2, INFO_W1, INFO_W2 = range(6)


def _to_row_tiles(ref, x):
    rows = x.shape[0]
    for j in range(ROW_TILE):
        ref[pl.ds(j, rows, stride=ROW_TILE), :] = x[:, j * LANES:(j + 1) * LANES]


def _from_row_tiles(ref, rows):
    return jnp.concatenate([ref[pl.ds(j, rows, stride=ROW_TILE), :] for j in range(ROW_TILE)], axis=-1)


def _router_body(x_ref, sh_ref, sc_ref, w_ref, info_ref, incl_ref, xrt_ref, cnt_ref):
    @pl.when(pl.program_id(0) == 0)
    def _():
        cnt_ref[...] = jnp.zeros_like(cnt_ref)

    _to_row_tiles(xrt_ref, x_ref[...])
    h = _modulate(x_ref, sh_ref, sc_ref)
    logits = jnp.dot(h, w_ref[...], preferred_element_type=F32, precision=HIGHEST)
    lane = lax.broadcasted_iota(jnp.int32, logits.shape, 1).astype(F32)
    logits = jnp.where(lane < N_EXPERTS, logits, -jnp.inf)
    e = jnp.exp(logits - jnp.max(logits, axis=-1, keepdims=True))
    p = e / jnp.sum(e, axis=-1, keepdims=True)
    p1 = jnp.max(p, axis=-1, keepdims=True)
    i1 = jnp.min(jnp.where(p == p1, lane, float(LANES)), axis=-1, keepdims=True)
    rest = jnp.where(lane == i1, -1.0, p)
    p2 = jnp.max(rest, axis=-1, keepdims=True)
    i2 = jnp.min(jnp.where(rest == p2, lane, float(LANES)), axis=-1, keepdims=True)
    total = p1 + p2
    chosen = jnp.where((lane == i1) | (lane == i2), 1.0, 0.0)
    tm = chosen.shape[0]
    earlier = (lax.broadcasted_iota(jnp.int32, (tm, tm), 1) < lax.broadcasted_iota(jnp.int32, (tm, tm), 0))
    rank = jnp.dot(earlier.astype(BF16), chosen.astype(BF16), preferred_element_type=F32) + cnt_ref[...]
    r1 = jnp.sum(jnp.where(lane == i1, rank, 0.0), axis=-1, keepdims=True)
    r2 = jnp.sum(jnp.where(lane == i2, rank, 0.0), axis=-1, keepdims=True)
    cnt_ref[...] += jnp.sum(chosen, axis=0, keepdims=True)
    incl_ref[0] = jnp.broadcast_to(cnt_ref[...], incl_ref.shape[1:])
    info = jnp.zeros_like(p)
    for col, val in ((INFO_E1, i1), (INFO_E2, i2), (INFO_R1, r1), (INFO_R2, r2),
                     (INFO_W1, p1 / total), (INFO_W2, p2 / total)):
        info = jnp.where(lane == col, val, info)
    info_ref[...] = info


def _router(x, sh, sc, w_router):
    tm = ROUTER_TM
    return pl.pallas_call(
        _router_body,
        out_shape=(jax.ShapeDtypeStruct((N_TOK, LANES), F32),
                   jax.ShapeDtypeStruct((N_TOK // tm, 8, LANES), F32),
                   jax.ShapeDtypeStruct((N_TOK * ROW_TILE, LANES), F32)),
        grid=(N_TOK // tm,),
        in_specs=_mod_specs(tm, 1, 0) + [pl.BlockSpec((D, LANES), lambda i: (0, 0))],
        out_specs=(pl.BlockSpec((tm, LANES), lambda i: (i, 0)), pl.BlockSpec((1, 8, LANES), lambda i: (i, 0, 0)),
                   pl.BlockSpec((tm * ROW_TILE, LANES), lambda i: (i, 0))),
        scratch_shapes=[pltpu.VMEM((1, LANES), F32)],
        compiler_params=_params(1),
        name="moe_router",
    )(x, sh, sc, jnp.pad(w_router, ((0, 0), (0, LANES - N_EXPERTS))))


def _row_tile(ref, row):
    return ref.at[pl.ds(pl.multiple_of(row * ROW_TILE, ROW_TILE), ROW_TILE)]


def _dispatch_body(pos1_ref, pos2_ref, x_hbm, zeros_hbm, xs_hbm, sem, *, tm):
    del zeros_hbm
    base = pl.program_id(0) * tm

    def issue(r, carry):
        src = _row_tile(x_hbm, base + r)
        pltpu.make_async_copy(src, _row_tile(xs_hbm, pos1_ref[base + r]), sem).start()
        pltpu.make_async_copy(src, _row_tile(xs_hbm, pos2_ref[base + r]), sem).start()
        return carry

    lax.fori_loop(0, tm, issue, 0)

    def drain(r, carry):
        src = _row_tile(x_hbm, base + r)
        pltpu.make_async_copy(src, _row_tile(xs_hbm, pos1_ref[base + r]), sem).wait()
        pltpu.make_async_copy(src, _row_tile(xs_hbm, pos2_ref[base + r]), sem).wait()
        return carry

    lax.fori_loop(0, tm, drain, 0)


def _dispatch(pos1, pos2, x_rt):
    tm = 1024
    return pl.pallas_call(
        functools.partial(_dispatch_body, tm=tm),
        out_shape=jax.ShapeDtypeStruct((N_SLOTS * ROW_TILE, LANES), F32),
        grid_spec=pltpu.PrefetchScalarGridSpec(
            num_scalar_prefetch=2, grid=(N_TOK // tm,),
            in_specs=[pl.BlockSpec(memory_space=pl.ANY), pl.BlockSpec(memory_space=pl.ANY)],
            out_specs=pl.BlockSpec(memory_space=pl.ANY),
            scratch_shapes=[pltpu.SemaphoreType.DMA(())]),
        input_output_aliases={3: 0},
        compiler_params=_params(1),
        name="moe_dispatch",
    )(pos1, pos2, x_rt, jnp.zeros((N_SLOTS * ROW_TILE, LANES), F32))


def _expert_swiglu_body(eid_ref, b1_ref, b2_ref, nv_ref, xs_ref, sh_ref, sc_ref, wg_ref, wu_ref, o_ref, *, fc):
    t = pl.program_id(0)

    @pl.when(t < nv_ref[0])
    def _():
        tm = o_ref.shape[0]
        x = _from_row_tiles(xs_ref, tm)
        slot = t * tm + lax.broadcasted_iota(jnp.int32, (tm, 1), 0)
        in1, in2 = slot >= b1_ref[t], slot >= b2_ref[t]
        sc = jnp.where(in2, sc_ref[2], jnp.where(in1, sc_ref[1], sc_ref[0]))
        sh = jnp.where(in2, sh_ref[2], jnp.where(in1, sh_ref[1], sh_ref[0]))
        h = (x * (1.0 + sc) + sh).astype(BF16)
        for c in range(o_ref.shape[1] // fc):
            cs = slice(c * fc, (c + 1) * fc)
            g = jnp.dot(h, wg_ref[0, :, cs], preferred_element_type=F32)
            u = jnp.dot(h, wu_ref[0, :, cs], preferred_element_type=F32)
            o_ref[:, cs] = (g * jax.nn.sigmoid(g) * u).astype(BF16)

    @pl.when(t >= nv_ref[0])
    def _():
        o_ref[...] = jnp.zeros_like(o_ref)


def _expert_swiglu(meta, xs_rt, sh, sc, w_gate, w_up):
    eid, b1, b2, nv = meta
    tm, f = EXPERT_TM, w_gate.shape[2]

    def tile(t, eid, b1, b2, nv):
        return jnp.minimum(t, nv[0] - 1)

    w_spec = pl.BlockSpec((1, D, f), lambda t, eid, b1, b2, nv: (eid[tile(t, eid, b1, b2, nv)], 0, 0))
    mod = pl.BlockSpec((8, 1, D), lambda t, *_: (0, 0, 0))
    return pl.pallas_call(
        functools.partial(_expert_swiglu_body, fc=512),
        out_shape=jax.ShapeDtypeStruct((N_SLOTS, f), BF16),
        grid_spec=pltpu.PrefetchScalarGridSpec(
            num_scalar_prefetch=4, grid=(N_SLOT_TILES,),
            in_specs=[pl.BlockSpec((tm * ROW_TILE, LANES), lambda t, *m: (tile(t, *m), 0)), mod, mod, w_spec, w_spec],
            out_specs=pl.BlockSpec((tm, f), lambda t, *m: (t, 0))),
        compiler_params=_params(1, 56 * 1024 * 1024),
        name="moe_swiglu",
    )(eid, b1, b2, nv, xs_rt, sh, sc, w_gate, w_up)


def _expert_down_body(eid_ref, nv_ref, a_ref, w_ref, y_ref):
    @pl.when(pl.program_id(0) < nv_ref[0])
    def _():
        _to_row_tiles(y_ref, jnp.dot(a_ref[...], w_ref[0], preferred_element_type=F32))

    @pl.when(pl.program_id(0) >= nv_ref[0])
    def _():
        y_ref[...] = jnp.zeros_like(y_ref)


def _expert_down(meta, act, w_down):
    eid, _, _, nv = meta
    tm, f = EXPERT_TM, act.shape[1]

    def tile(t, eid, nv):
        return jnp.minimum(t, nv[0] - 1)

    return pl.pallas_call(
        _expert_down_body,
        out_shape=jax.ShapeDtypeStruct((N_SLOTS * ROW_TILE, LANES), F32),
        grid_spec=pltpu.PrefetchScalarGridSpec(
            num_scalar_prefetch=2, grid=(N_SLOT_TILES,),
            in_specs=[pl.BlockSpec((tm, f), lambda t, *m: (tile(t, *m), 0)),
                      pl.BlockSpec((1, f, D), lambda t, eid, nv: (eid[tile(t, eid, nv)], 0, 0))],
            out_specs=pl.BlockSpec((tm * ROW_TILE, LANES), lambda t, *m: (t, 0))),
        compiler_params=_params(1),
        name="moe_down",
    )(eid, nv, act, w_down)


def _combine_body(pos1_ref, pos2_ref, y_hbm, info_ref, x_ref, g_ref, lng_ref, lnb_ref, o_ref, y1_ref, y2_ref, sem,
                  *, tm):
    base = pl.program_id(0) * tm

    def issue(r, carry):
        pltpu.make_async_copy(_row_tile(y_hbm, pos1_ref[base + r]), _row_tile(y1_ref, r), sem).start()
        pltpu.make_async_copy(_row_tile(y_hbm, pos2_ref[base + r]), _row_tile(y2_ref, r), sem).start()
        return carry

    lax.fori_loop(0, tm, issue, 0)

    def drain(r, carry):
        pltpu.make_async_copy(_row_tile(y_hbm, pos1_ref[base + r]), _row_tile(y1_ref, r), sem).wait()
        pltpu.make_async_copy(_row_tile(y_hbm, pos2_ref[base + r]), _row_tile(y2_ref, r), sem).wait()
        return carry

    lax.fori_loop(0, tm, drain, 0)
    info = info_ref[...]
    ffn = (info[:, INFO_W1:INFO_W1 + 1] * _from_row_tiles(y1_ref, tm)
           + info[:, INFO_W2:INFO_W2 + 1] * _from_row_tiles(y2_ref, tm))
    y = ALPHA * x_ref[...] + g_ref[0] * ffn
    mu = jnp.mean(y, axis=-1, keepdims=True)
    yc = y - mu
    var = jnp.mean(yc * yc, axis=-1, keepdims=True)
    o_ref[...] = yc * lax.rsqrt(var + LN_EPS) * lng_ref[...] + lnb_ref[...]


def _combine(pos1, pos2, y_rt, info, x, gate, ln_g, ln_b):
    tm = 512
    return pl.pallas_call(
        functools.partial(_combine_body, tm=tm),
        out_shape=jax.ShapeDtypeStruct((N_TOK, D), F32),
        grid_spec=pltpu.PrefetchScalarGridSpec(
            num_scalar_prefetch=2, grid=(N_TOK // tm,),
            in_specs=[pl.BlockSpec(memory_space=pl.ANY),
                      pl.BlockSpec((tm, LANES), lambda i, *_: (i, 0)),
                      pl.BlockSpec((tm, D), lambda i, *_: (i, 0)),
                      pl.BlockSpec((1, 1, D), lambda i, *_: (_group_of_row(i * tm), 0, 0)),
                      pl.BlockSpec((1, D), lambda i, *_: (0, 0)),
                      pl.BlockSpec((1, D), lambda i, *_: (0, 0))],
            out_specs=pl.BlockSpec((tm, D), lambda i, *_: (i, 0)),
            scratch_shapes=[pltpu.VMEM((tm * ROW_TILE, LANES), F32), pltpu.VMEM((tm * ROW_TILE, LANES), F32),
                            pltpu.SemaphoreType.DMA(())]),
        compiler_params=_params(1),
        name="moe_combine",
    )(pos1, pos2, y_rt, info, x, gate, ln_g.reshape(1, D), ln_b.reshape(1, D))


def _slot_plan(info, incl):
    row = lambda n_rows: incl[n_rows // ROUTER_TM - 1, 0, :N_EXPERTS].astype(jnp.int32)
    count = row(N_TOK)
    padded = (count + EXPERT_TM - 1) // EXPERT_TM * EXPERT_TM
    end = jnp.cumsum(padded)
    start = end - padded
    tile_row = jnp.arange(N_SLOT_TILES, dtype=jnp.int32) * EXPERT_TM
    eid = jnp.minimum(jnp.sum(tile_row[:, None] >= end[None, :], axis=1), N_EXPERTS - 1).astype(jnp.int32)
    b1 = (start + row(N_CTX))[eid]
    b2 = (start + row(N_CTX + DEC_SEQ))[eid]
    nv = (end[-1:] // EXPERT_TM).astype(jnp.int32)
    e1, e2 = info[:, INFO_E1].astype(jnp.int32), info[:, INFO_E2].astype(jnp.int32)
    pos1 = start[e1] + info[:, INFO_R1].astype(jnp.int32)
    pos2 = start[e2] + info[:, INFO_R2].astype(jnp.int32)
    return pos1, pos2, (eid, b1, b2, nv)


def _even_mixer(x, sh, sc, gate, ln_g, ln_b, w_in, b_igate, b_fgate, ml_norm_g, q_norm_g, k_norm_g, w_out,
                st_c, st_n, st_m, cache_k, cache_v, rope_tabs):
    splits = (4 * ML_W, 4 * ML_W + N_GATES)
    w_main = jnp.concatenate([w_in[:, :splits[0]], w_in[:, splits[1]:]], axis=1).astype(BF16)
    proj = _mod_matmul(x, sh, sc, [w_main[None]], tm=1024, tn=MAIN_W // 2, out_dtype=F32, name="even_in_proj")[0]
    b_gate = jnp.stack([b_igate, b_fgate], axis=1).reshape(N_GATES)
    lic, bc, lir, br = _gates(x, sh, sc, w_in[:, splits[0]:splits[1]], b_gate)

    hf_c, hb_c, new_c, new_n, new_m = _mlstm(proj, lic, bc, lir, br, row0=0, n_seq=BATCH, seq_len=SEQ)
    init = (st_c, st_n, jnp.broadcast_to(st_m[..., None], st_n.shape))
    hf_s, hb_s, _, _, _ = _mlstm(proj, lic, bc, lir, br, row0=N_CTX, n_seq=DEC_BATCH, seq_len=DEC_SEQ, init=init)
    ml = _ml_post(jnp.concatenate([hf_c, hf_s]), jnp.concatenate([hb_c, hb_s]), proj, ml_norm_g)

    q_c, kn_c, kb_c = _qk_prep(proj, q_norm_g, k_norm_g, row0=0, rows=N_CTX)
    q_s, _, kb_s = _qk_prep(proj, q_norm_g, k_norm_g, row0=N_CTX, rows=N_LAT, rope_tabs=rope_tabs)
    v_all = proj[:, MAIN_W - KV_W:]
    v_c, v_s = v_all[:N_CTX], v_all[N_CTX:]
    att_c = _attention(q_c, _rep_heads(kb_c, BATCH), _rep_heads(v_c.astype(BF16), BATCH),
                       n_seq=BATCH, seq_len=SEQ, tq=SEQ)
    k_lat = jnp.concatenate([kb_s.reshape(DEC_BATCH, DEC_SEQ, KV_W),
                             cache_k.reshape(DEC_BATCH, PAST_LEN, KV_W).astype(BF16)], axis=1)
    v_lat = jnp.concatenate([v_s.reshape(DEC_BATCH, DEC_SEQ, KV_W).astype(BF16),
                             cache_v.reshape(DEC_BATCH, PAST_LEN, KV_W).astype(BF16)], axis=1)
    att_s = _attention(q_s, _rep_heads(k_lat.reshape(-1, KV_W), DEC_BATCH), _rep_heads(v_lat.reshape(-1, KV_W), DEC_BATCH),
                       n_seq=DEC_BATCH, seq_len=DEC_SEQ, tq=256)

    mixed = jnp.concatenate([ml, jnp.concatenate([att_c, att_s])], axis=1)
    x = _matmul_res_ln(mixed[None], w_out.astype(BF16)[None], x, gate, ln_g, ln_b, tm=512, name="even_out_proj")
    new_k = kn_c.reshape(BATCH, SEQ, ATT_KV_HEADS, ATT_HEAD_DIM)
    new_v = v_c.reshape(BATCH, SEQ, ATT_KV_HEADS, ATT_HEAD_DIM)
    return x, new_k, new_v, new_c, new_n, new_m[..., 0]


def _hyena_mixer(x, sh, sc, gate, ln_g, ln_b, w_in, conv_w, conv_b, w1, b1, w2, b2, w3, sin_freq, skip, w_out, dft):
    u = _mod_matmul(x, sh, sc, [w_in.astype(BF16)[None]], tm=1024, tn=512, out_dtype=F32, name="hyena_in_proj")[0]
    ys = []
    for row0, n_seq, seq_len, tiles in ((0, BATCH, SEQ, dict(tf=256, tk=256, tn=1024)),
                                        (N_CTX, DEC_BATCH, DEC_SEQ, dict(tf=1024, tk=512, tn=512))):
        mats = dft[seq_len]
        h_sum, h_diff, nyq = _hy_filter(seq_len, w1, b1, w2, b2, w3, sin_freq)
        kr, ki = _dft_fwd(mats, h_sum, h_diff, row0=0, n_seq=1, seq_len=seq_len, **tiles)
        kr, ki = kr[0], ki[0].at[0].set(nyq[0])
        z, x0 = _hy_gate(u, conv_w, conv_b, row0=row0, n_seq=n_seq, seq_len=seq_len)
        yr, yi = _dft_fwd(mats, z, z, row0=0, n_seq=n_seq, seq_len=seq_len, filt=(kr, ki), **tiles)
        ys.append(_dft_inv(mats, yr, yi, z, x0, skip, row0=0, n_seq=n_seq, seq_len=seq_len,
                           tm=tiles["tf"], tk=tiles["tk"], tn=tiles["tn"]))
    y = jnp.concatenate(ys)
    return _matmul_res_ln(y[None], w_out.astype(BF16)[None], x, gate, ln_g, ln_b, tm=512, name="hyena_out_proj")


def _dense_ffn(x, sh, sc, gate, ln_g, ln_b, w_gate, w_up, w_down):
    act = _mod_matmul(x, sh, sc, [w_gate.astype(BF16)[None], w_up.astype(BF16)[None]],
                      tm=1024, tn=D_FF // 2, out_dtype=BF16, name="ffn_swiglu")
    return _matmul_res_ln(act, w_down.astype(BF16)[None], x, gate, ln_g, ln_b, tm=512, name="ffn_down")


def _moe_ffn(x, sh, sc, gate, ln_g, ln_b, w_router, w_gate, w_up, w_down):
    info, incl, x_rt = _router(x, sh, sc, w_router)
    pos1, pos2, meta = _slot_plan(info, incl)
    xs_rt = _dispatch(pos1, pos2, x_rt)
    act = _expert_swiglu(meta, xs_rt, sh, sc, w_gate.astype(BF16), w_up.astype(BF16))
    y_rt = _expert_down(meta, act, w_down.astype(BF16))
    return _combine(pos1, pos2, y_rt, info, x, gate, ln_g, ln_b)


def kernel(x_prompt, x_sample, cache_attn_k, cache_attn_v, state_mlstm_C, state_mlstm_n, state_mlstm_m, c, c_ctx, w_ada, b_ada, ln_g, ln_b, w_in_even, b_igate, b_fgate, ml_norm_g, q_norm_g, k_norm_g, w_out_even, w_ffn_gate, w_ffn_up, w_ffn_down, w_in_hy, hy_conv_w, hy_conv_b, hy_filt_w1, hy_filt_b1, hy_filt_w2, hy_filt_b2, hy_filt_w3, hy_sin_freq, hy_skip, w_out_hy, w_router, w_moe_gate, w_moe_up, w_moe_down):
    x = jnp.concatenate([x_prompt.reshape(N_CTX, D), x_sample.reshape(N_LAT, D)])
    cvec = jnp.concatenate([c_ctx[None], c, jnp.zeros((8 - 1 - DEC_BATCH, D), F32)])
    mods = _ada(cvec, w_ada, b_ada)
    rope_tabs = _rope_tables()
    dft = {SEQ: _dft_matrices(SEQ), DEC_SEQ: _dft_matrices(DEC_SEQ)}
    new_k, new_v, new_c, new_n, new_m = [], [], [], [], []
    for layer in range(DEPTH):
        sh1, sc1, g1, sh2, sc2, g2 = (mods[layer, :, i * D:(i + 1) * D].reshape(8, 1, D) for i in range(6))
        i = layer // 2
        if layer % 2 == 0:
            x, k_c, v_c, st_c, st_n, st_m = _even_mixer(
                x, sh1, sc1, g1, ln_g[layer, 0], ln_b[layer, 0], w_in_even[i], b_igate[i], b_fgate[i], ml_norm_g[i],
                q_norm_g[i], k_norm_g[i], w_out_even[i], state_mlstm_C[:, i], state_mlstm_n[:, i], state_mlstm_m[:, i],
                cache_attn_k[:, i], cache_attn_v[:, i], rope_tabs)
            new_k.append(k_c)
            new_v.append(v_c)
            new_c.append(st_c)
            new_n.append(st_n)
            new_m.append(st_m)
            x = _dense_ffn(x, sh2, sc2, g2, ln_g[layer, 1], ln_b[layer, 1], w_ffn_gate[i], w_ffn_up[i], w_ffn_down[i])
        else:
            x = _hyena_mixer(x, sh1, sc1, g1, ln_g[layer, 0], ln_b[layer, 0], w_in_hy[i], hy_conv_w[i], hy_conv_b[i],
                             hy_filt_w1[i], hy_filt_b1[i], hy_filt_w2[i], hy_filt_b2[i], hy_filt_w3[i], hy_sin_freq[i],
                             hy_skip[i], w_out_hy[i], dft)
            x = _moe_ffn(x, sh2, sc2, g2, ln_g[layer, 1], ln_b[layer, 1], w_router[i], w_moe_gate[i], w_moe_up[i],
                         w_moe_down[i])
    return (x[:N_CTX].reshape(BATCH, SEQ, D), x[N_CTX:].reshape(DEC_BATCH, DEC_SEQ, D),
            jnp.stack(new_k, axis=1), jnp.stack(new_v, axis=1), jnp.stack(new_c, axis=1),
            jnp.stack(new_n, axis=1), jnp.stack(new_m, axis=1))
```

```python
import functools
import math

import jax
import jax.numpy as jnp
from jax import lax
from jax.experimental import pallas as pl
from jax.experimental.pallas import tpu as pltpu

F32 = jnp.float32
BF16 = jnp.bfloat16
HIGHEST = lax.Precision.HIGHEST

D = 1024
BATCH, SEQ = 32, 256
DEC_BATCH, DEC_SEQ = 2, 4096
DEPTH = 4
PAST_LEN = 256
GRID_W = 64
N_CTX = BATCH * SEQ
N_LAT = DEC_BATCH * DEC_SEQ
N_TOK = N_CTX + N_LAT

ML_HEADS, ML_HEAD_DIM = 4, 128
ML_W = ML_HEADS * ML_HEAD_DIM
CHUNK = 128
ATT_HEADS, ATT_KV_HEADS, ATT_HEAD_DIM = 8, 2, 64
ATT_GROUP = ATT_HEADS // ATT_KV_HEADS
ATT_W = ATT_HEADS * ATT_HEAD_DIM
KV_W = ATT_KV_HEADS * ATT_HEAD_DIM
GROUP_W = ATT_GROUP * ATT_HEAD_DIM
ROPE_BASE = 10000.0
N_GATES = 4 * ML_HEADS
MAIN_W = 4 * ML_W + ATT_W + 2 * KV_W

HY_EMB = 33
HY_BANDS = (HY_EMB - 1) // 2
HY_TARGET, HY_SHORT_PCT, HY_LONG_PCT = 1e-2, 0.3, 1.5
D_FF = 2816
N_EXPERTS = 8
MOE_D_FF = 3584
ALPHA = (2 * DEPTH) ** 0.25
LN_EPS = 1e-5
RMS_EPS = 1e-6

LANES = 128
VMEM_LIMIT = 48 * 1024 * 1024


def _params(n_axes, vmem=VMEM_LIMIT):
    return pltpu.CompilerParams(dimension_semantics=("arbitrary",) * n_axes, vmem_limit_bytes=vmem)


def _group_of_row(r):
    return jnp.where(r < N_CTX, 0, 1 + (r - N_CTX) // DEC_SEQ)


def _modulate(x_ref, sh_ref, sc_ref):
    return x_ref[...] * (1.0 + sc_ref[0]) + sh_ref[0]


def _mod_specs(tm, n_axes, row_axis):
    def rows(*ids):
        return (ids[row_axis], 0)

    def grp(*ids):
        return (_group_of_row(ids[row_axis] * tm), 0, 0)

    del n_axes
    return [pl.BlockSpec((tm, D), rows), pl.BlockSpec((1, 1, D), grp), pl.BlockSpec((1, 1, D), grp)]


def _ada_body(c_ref, w_ref, b_ref, o_ref):
    c = c_ref[...]
    s = c * jax.nn.sigmoid(c)
    o_ref[0] = jnp.dot(s, w_ref[0], preferred_element_type=F32, precision=HIGHEST) + b_ref[0]


def _ada(cvec, w_ada, b_ada):
    tn = 1536
    return pl.pallas_call(
        _ada_body,
        out_shape=jax.ShapeDtypeStruct((DEPTH, 8, 6 * D), F32),
        grid=(DEPTH, 6 * D // tn),
        in_specs=[pl.BlockSpec((8, D), lambda l, j: (0, 0)),
                  pl.BlockSpec((1, D, tn), lambda l, j: (l, 0, j)),
                  pl.BlockSpec((1, 1, tn), lambda l, j: (l, 0, j))],
        out_specs=pl.BlockSpec((1, 8, tn), lambda l, j: (l, 0, j)),
        compiler_params=_params(2),
        name="ada_modulation",
    )(cvec, w_ada, b_ada.reshape(DEPTH, 1, 6 * D))


def _mod_mm_body(x_ref, sh_ref, sc_ref, *refs, n_w):
    w_refs, o_ref, h_ref = refs[:n_w], refs[n_w], refs[n_w + 1]

    @pl.when(pl.program_id(2) == 0)
    def _():
        h_ref[...] = _modulate(x_ref, sh_ref, sc_ref).astype(BF16)

    h = h_ref[...]
    if n_w == 1:
        o = jnp.dot(h, w_refs[0][0], preferred_element_type=F32)
    else:
        g = jnp.dot(h, w_refs[0][0], preferred_element_type=F32)
        u = jnp.dot(h, w_refs[1][0], preferred_element_type=F32)
        o = g * jax.nn.sigmoid(g) * u
    o_ref[0] = o.astype(o_ref.dtype)


def _mod_matmul(x, sh, sc, ws, *, tm, tn, out_dtype, name):
    n_e, _, f = ws[0].shape
    return pl.pallas_call(
        functools.partial(_mod_mm_body, n_w=len(ws)),
        out_shape=jax.ShapeDtypeStruct((n_e, N_TOK, f), out_dtype),
        grid=(n_e, N_TOK // tm, f // tn),
        in_specs=_mod_specs(tm, 3, 1) + [pl.BlockSpec((1, D, tn), lambda e, i, j: (e, 0, j)) for _ in ws],
        out_specs=pl.BlockSpec((1, tm, tn), lambda e, i, j: (e, i, j)),
        scratch_shapes=[pltpu.VMEM((tm, D), BF16)],
        compiler_params=_params(3),
        name=name,
    )(x, sh, sc, *ws)


def _mm_res_ln_body(a_ref, w_ref, x_ref, g_ref, lng_ref, lnb_ref, *refs, n_e, use_gates):
    if use_gates:
        gates_ref, o_ref, acc_ref = refs
    else:
        o_ref, acc_ref = refs
    e = pl.program_id(1)
    p = jnp.dot(a_ref[0], w_ref[0], preferred_element_type=F32)
    if use_gates:
        gt = gates_ref[...]
        lane = lax.broadcasted_iota(jnp.int32, gt.shape, 1)
        p = p * jnp.sum(jnp.where(lane == e, gt, 0.0), axis=-1, keepdims=True)

    @pl.when(e == 0)
    def _():
        acc_ref[...] = p

    @pl.when(e > 0)
    def _():
        acc_ref[...] += p

    @pl.when(e == n_e - 1)
    def _():
        y = ALPHA * x_ref[...] + g_ref[0] * acc_ref[...]
        mu = jnp.mean(y, axis=-1, keepdims=True)
        yc = y - mu
        var = jnp.mean(yc * yc, axis=-1, keepdims=True)
        o_ref[...] = yc * lax.rsqrt(var + LN_EPS) * lng_ref[...] + lnb_ref[...]


def _matmul_res_ln(a, w, x, gate, ln_g, ln_b, *, tm, gates=None, name):
    n_e, _, k = a.shape
    use_gates = gates is not None
    in_specs = [pl.BlockSpec((1, tm, k), lambda i, e: (e, i, 0)),
                pl.BlockSpec((1, k, D), lambda i, e: (e, 0, 0)),
                pl.BlockSpec((tm, D), lambda i, e: (i, 0)),
                pl.BlockSpec((1, 1, D), lambda i, e: (_group_of_row(i * tm), 0, 0)),
                pl.BlockSpec((1, D), lambda i, e: (0, 0)),
                pl.BlockSpec((1, D), lambda i, e: (0, 0))]
    args = [a, w, x, gate, ln_g.reshape(1, D), ln_b.reshape(1, D)]
    if use_gates:
        in_specs.append(pl.BlockSpec((tm, LANES), lambda i, e: (i, 0)))
        args.append(gates)
    return pl.pallas_call(
        functools.partial(_mm_res_ln_body, n_e=n_e, use_gates=use_gates),
        out_shape=jax.ShapeDtypeStruct((N_TOK, D), F32),
        grid=(N_TOK // tm, n_e),
        in_specs=in_specs,
        out_specs=pl.BlockSpec((tm, D), lambda i, e: (i, 0)),
        scratch_shapes=[pltpu.VMEM((tm, D), F32)],
        compiler_params=_params(2),
        name=name,
    )(*args)


def _log_sigmoid(x):
    return jnp.minimum(x, 0.0) - jnp.log(1.0 + jnp.exp(-jnp.abs(x)))


def _gates_body(x_ref, sh_ref, sc_ref, wg_ref, wgt_ref, b_ref, bt_ref,
                lic_ref, bc_ref, lir_ref, br_ref, *, tm):
    h = _modulate(x_ref, sh_ref, sc_ref)
    g = jnp.dot(h, wg_ref[...], preferred_element_type=F32, precision=HIGHEST) + b_ref[...]
    gt = lax.dot_general(wgt_ref[...], h, (((1,), (1,)), ((), ())),
                         preferred_element_type=F32, precision=HIGHEST) + bt_ref[...]
    lic_ref[...] = g
    lir_ref[...] = gt
    lf, lft = _log_sigmoid(g), _log_sigmoid(gt)
    r = lax.broadcasted_iota(jnp.int32, (CHUNK, CHUNK), 0)
    c = lax.broadcasted_iota(jnp.int32, (CHUNK, CHUNK), 1)
    tri_l = (c <= r).astype(F32)
    tri_u = (c >= r).astype(F32)
    fwd_col = lax.broadcasted_iota(jnp.int32, (CHUNK, LANES), 1) < 2 * ML_HEADS
    fwd_row = lax.broadcasted_iota(jnp.int32, (N_GATES, CHUNK), 0) < 2 * ML_HEADS
    for ch in range(tm // CHUNK):
        sl = slice(ch * CHUNK, (ch + 1) * CHUNK)
        lfc, lftc = lf[sl, :], lft[:, sl]
        cum_f = jnp.dot(tri_l, lfc, preferred_element_type=F32, precision=HIGHEST)
        cum_b = jnp.dot(tri_u, lfc, preferred_element_type=F32, precision=HIGHEST)
        bc_ref[sl, :] = jnp.where(fwd_col, cum_f, cum_b)
        cum_f = jnp.dot(lftc, tri_u, preferred_element_type=F32, precision=HIGHEST)
        cum_b = jnp.dot(lftc, tri_l, preferred_element_type=F32, precision=HIGHEST)
        br_ref[:, sl] = jnp.where(fwd_row, cum_f, cum_b)


def _gates(x, sh, sc, wg, b_gate):
    tm = 256
    wg_pad = jnp.pad(wg, ((0, 0), (0, LANES - N_GATES)))
    b_pad = jnp.pad(b_gate, (0, LANES - N_GATES)).reshape(1, LANES)
    col = pl.BlockSpec((tm, LANES), lambda i: (i, 0))
    row = pl.BlockSpec((N_GATES, tm), lambda i: (0, i))
    return pl.pallas_call(
        functools.partial(_gates_body, tm=tm),
        out_shape=(jax.ShapeDtypeStruct((N_TOK, LANES), F32), jax.ShapeDtypeStruct((N_TOK, LANES), F32),
                   jax.ShapeDtypeStruct((N_GATES, N_TOK), F32), jax.ShapeDtypeStruct((N_GATES, N_TOK), F32)),
        grid=(N_TOK // tm,),
        in_specs=_mod_specs(tm, 1, 0) + [pl.BlockSpec((D, LANES), lambda i: (0, 0)),
                                         pl.BlockSpec((N_GATES, D), lambda i: (0, 0)),
                                         pl.BlockSpec((1, LANES), lambda i: (0, 0)),
                                         pl.BlockSpec((N_GATES, 1), lambda i: (0, 0))],
        out_specs=(col, col, row, row),
        compiler_params=_params(1),
        name="mlstm_gates",
    )(x, sh, sc, wg_pad, wg.T, b_pad, b_gate.reshape(N_GATES, 1))


def _mlstm_body(*refs, has_init):
    (qf, kf, vf, licf, bcf, lirf, brf, qb, kb, vb, licb, bcb, lirb, brb) = refs[:14]
    refs = refs[14:]
    if has_init:
        c0_ref, n0_ref, m0_ref = refs[:3]
        refs = refs[3:]
    hf_ref, hb_ref, c_ref, n_ref, m_ref = refs

    @pl.when(pl.program_id(1) == 0)
    def _():
        if has_init:
            c_ref[...] = c0_ref[...]
            n_ref[...] = n0_ref[...]
            m_ref[...] = m0_ref[...]
        else:
            c_ref[...] = jnp.zeros_like(c_ref)
            n_ref[...] = jnp.zeros_like(n_ref)
            m_ref[...] = jnp.zeros_like(m_ref)

    t_idx = lax.broadcasted_iota(jnp.int32, (CHUNK, CHUNK), 0)
    s_idx = lax.broadcasted_iota(jnp.int32, (CHUNK, CHUNK), 1)
    nt = (((1,), (1,)), ((), ()))
    for d, (q_ref, k_ref, v_ref, lic_ref, bc_ref, lir_ref, br_ref, h_ref) in enumerate(
            ((qf, kf, vf, licf, bcf, lirf, brf, hf_ref), (qb, kb, vb, licb, bcb, lirb, brb, hb_ref))):
        mask = (s_idx <= t_idx) if d == 0 else (s_idx >= t_idx)
        for h in range(ML_HEADS):
            hs = slice(h * ML_HEAD_DIM, (h + 1) * ML_HEAD_DIM)
            gi, gf = d * 2 * ML_HEADS + h, d * 2 * ML_HEADS + ML_HEADS + h
            q = q_ref[:, hs]
            k = k_ref[:, hs] * (ML_HEAD_DIM ** -0.5)
            v = v_ref[:, hs]
            qh, kh, vh = q.astype(BF16), k.astype(BF16), v.astype(BF16)
            li_c, b_c = lic_ref[:, gi:gi + 1], bc_ref[:, gf:gf + 1]
            li_r, b_r = lir_ref[gi:gi + 1, :], br_ref[gf:gf + 1, :]
            c_st = c_ref[0, d, h]
            n_st = n_ref[0, d, h:h + 1, :]
            m_st = m_ref[0, d, h:h + 1, :][:, 0:1]

            dmat = jnp.where(mask, b_c - b_r + li_r, -jnp.inf)
            inter = b_c + m_st
            m_out = jnp.maximum(inter, jnp.max(dmat, axis=-1, keepdims=True))
            s = lax.dot_general(qh, kh, nt, preferred_element_type=F32) * jnp.exp(dmat - m_out)
            w_inter = jnp.exp(inter - m_out)
            num = (jnp.dot(s.astype(BF16), vh, preferred_element_type=F32)
                   + w_inter * jnp.dot(qh, c_st.astype(BF16), preferred_element_type=F32))
            den = (jnp.sum(s, axis=-1, keepdims=True)
                   + w_inter * jnp.sum(q * n_st, axis=-1, keepdims=True))
            h_ref[:, hs] = num / jnp.maximum(jnp.abs(den), jnp.exp(-m_out))

            b_last = b_r[:, CHUNK - 1:CHUNK] if d == 0 else b_r[:, 0:1]
            g_r = b_last - b_r + li_r
            g_c = b_last - b_c + li_c
            m_new = jnp.maximum(b_last + m_st, jnp.max(g_r, axis=-1, keepdims=True))
            decay = jnp.exp(b_last + m_st - m_new)
            kw = k * jnp.exp(g_c - m_new)
            c_ref[0, d, h] = decay * c_st + jnp.dot(kw.T.astype(BF16), vh, preferred_element_type=F32)
            n_ref[0, d, h:h + 1, :] = decay * n_st + jnp.sum(kw, axis=0, keepdims=True)
            m_ref[0, d, h:h + 1, :] = jnp.broadcast_to(m_new, (1, ML_HEAD_DIM))


def _mlstm(proj, lic, bc, lir, br, *, row0, n_seq, seq_len, init=None):
    nc = seq_len // CHUNK
    base = row0 // CHUNK

    def fwd(b, j):
        return base + b * nc + j

    def bwd(b, j):
        return base + b * nc + (nc - 1 - j)

    def chunk_specs(pos):
        return ([pl.BlockSpec((CHUNK, ML_W), lambda b, j, c=c: (pos(b, j), c)) for c in range(3)]
                + [pl.BlockSpec((CHUNK, LANES), lambda b, j: (pos(b, j), 0))] * 2
                + [pl.BlockSpec((N_GATES, CHUNK), lambda b, j: (0, pos(b, j)))] * 2)

    st_c = pl.BlockSpec((1, 2, ML_HEADS, ML_HEAD_DIM, ML_HEAD_DIM), lambda b, j: (b, 0, 0, 0, 0))
    st_n = pl.BlockSpec((1, 2, ML_HEADS, ML_HEAD_DIM), lambda b, j: (b, 0, 0, 0))
    in_specs = chunk_specs(fwd) + chunk_specs(bwd)
    args = [proj, proj, proj, lic, bc, lir, br] * 2
    if init is not None:
        in_specs += [st_c, st_n, st_n]
        args += list(init)
    rows = n_seq * seq_len
    return pl.pallas_call(
        functools.partial(_mlstm_body, has_init=init is not None),
        out_shape=(jax.ShapeDtypeStruct((rows, ML_W), F32), jax.ShapeDtypeStruct((rows, ML_W), F32),
                   jax.ShapeDtypeStruct((n_seq, 2, ML_HEADS, ML_HEAD_DIM, ML_HEAD_DIM), F32),
                   jax.ShapeDtypeStruct((n_seq, 2, ML_HEADS, ML_HEAD_DIM), F32),
                   jax.ShapeDtypeStruct((n_seq, 2, ML_HEADS, ML_HEAD_DIM), F32)),
        grid=(n_seq, nc),
        in_specs=in_specs,
        out_specs=(pl.BlockSpec((CHUNK, ML_W), lambda b, j: (b * nc + j, 0)),
                   pl.BlockSpec((CHUNK, ML_W), lambda b, j: (b * nc + (nc - 1 - j), 0)),
                   st_c, st_n, st_n),
        compiler_params=_params(2),
        name="mlstm_scan",
    )(*args)


def _ml_post_body(hf_ref, hb_ref, o_ref, g_ref, out_ref):
    h = hf_ref[...] + hb_ref[...]
    gate = jax.nn.sigmoid(o_ref[...]) * g_ref[...]
    for hd in range(ML_HEADS):
        hs = slice(hd * ML_HEAD_DIM, (hd + 1) * ML_HEAD_DIM)
        x = h[:, hs]
        xc = x - jnp.mean(x, axis=-1, keepdims=True)
        var = jnp.mean(xc * xc, axis=-1, keepdims=True)
        out_ref[:, hs] = (gate[:, hs] * (xc * lax.rsqrt(var + RMS_EPS))).astype(BF16)


def _ml_post(hf, hb, proj, norm_g):
    tm = 512
    blk = pl.BlockSpec((tm, ML_W), lambda i: (i, 0))
    return pl.pallas_call(
        _ml_post_body,
        out_shape=jax.ShapeDtypeStruct((N_TOK, ML_W), BF16),
        grid=(N_TOK // tm,),
        in_specs=[blk, blk, pl.BlockSpec((tm, ML_W), lambda i: (i, 3)), pl.BlockSpec((1, ML_W), lambda i: (0, 0))],
        out_specs=blk,
        compiler_params=_params(1),
        name="mlstm_out_norm",
    )(hf, hb, proj, norm_g.reshape(1, ML_W))


def _head_rms(x, gain):
    lane_head = lax.broadcasted_iota(jnp.int32, x.shape, 1) // ATT_HEAD_DIM
    sq = x * x
    ms = jnp.zeros_like(x)
    for hd in range(x.shape[1] // ATT_HEAD_DIM):
        sel = lane_head == hd
        ms = jnp.where(sel, jnp.sum(jnp.where(sel, sq, 0.0), axis=-1, keepdims=True), ms)
    return x * lax.rsqrt(ms * (1.0 / ATT_HEAD_DIM) + RMS_EPS) * gain


def _rope(x, cos, sin_signed):
    w = x.shape[1]
    even = lax.broadcasted_iota(jnp.int32, x.shape, 1) % 2 == 0
    partner = jnp.where(even, pltpu.roll(x, w - 1, 1), pltpu.roll(x, 1, 1))
    return x * cos + partner * sin_signed


def _qk_prep_body(q_ref, k_ref, qg_ref, kg_ref, *refs, rope):
    if rope:
        cq_ref, sq_ref, ck_ref, sk_ref, qo_ref, kn_ref, kr_ref = refs
    else:
        qo_ref, kn_ref, kr_ref = refs
    q = _head_rms(q_ref[...], qg_ref[...])
    k = _head_rms(k_ref[...], kg_ref[...])
    kn_ref[...] = k
    if rope:
        q = _rope(q, cq_ref[...], sq_ref[...])
        k = _rope(k, ck_ref[...], sk_ref[...])
    qo_ref[...] = (q * (ATT_HEAD_DIM ** -0.5)).astype(BF16)
    kr_ref[...] = k.astype(BF16)


def _qk_prep(proj, q_gain, k_gain, *, row0, rows, rope_tabs=None):
    tm = 512
    r0 = row0 // tm
    in_specs = [pl.BlockSpec((tm, ATT_W), lambda i: (r0 + i, 4 * ML_W // ATT_W)),
                pl.BlockSpec((tm, KV_W), lambda i: (r0 + i, (4 * ML_W + ATT_W) // KV_W)),
                pl.BlockSpec((1, ATT_W), lambda i: (0, 0)),
                pl.BlockSpec((1, KV_W), lambda i: (0, 0))]
    args = [proj, proj, jnp.tile(q_gain, ATT_HEADS).reshape(1, ATT_W), jnp.tile(k_gain, ATT_KV_HEADS).reshape(1, KV_W)]
    if rope_tabs is not None:
        per_seq = DEC_SEQ // tm
        in_specs += [pl.BlockSpec((tm, ATT_W), lambda i: (i % per_seq, 0))] * 2
        in_specs += [pl.BlockSpec((tm, KV_W), lambda i: (i % per_seq, 0))] * 2
        args += list(rope_tabs)
    return pl.pallas_call(
        functools.partial(_qk_prep_body, rope=rope_tabs is not None),
        out_shape=(jax.ShapeDtypeStruct((rows, ATT_W), BF16), jax.ShapeDtypeStruct((rows, KV_W), F32),
                   jax.ShapeDtypeStruct((rows, KV_W), BF16)),
        grid=(rows // tm,),
        in_specs=in_specs,
        out_specs=(pl.BlockSpec((tm, ATT_W), lambda i: (i, 0)), pl.BlockSpec((tm, KV_W), lambda i: (i, 0)),
                   pl.BlockSpec((tm, KV_W), lambda i: (i, 0))),
        compiler_params=_params(1),
        name="attn_qk_prep",
    )(*args)


def _rope_tables():
    rows = DEC_SEQ // GRID_W
    axis_dim = ATT_HEAD_DIM // 2
    row = jnp.repeat(jnp.arange(rows, dtype=F32), GRID_W)
    col = (jnp.arange(DEC_SEQ) % GRID_W).astype(F32)
    inv = ROPE_BASE ** (-jnp.arange(axis_dim // 2, dtype=F32) * 2.0 / axis_dim)
    ang = jnp.concatenate([row[:, None] * inv, col[:, None] * inv], axis=-1)
    cos = jnp.repeat(jnp.cos(ang), 2, axis=-1)
    sin = jnp.repeat(jnp.sin(ang), 2, axis=-1) * jnp.tile(jnp.array([-1.0, 1.0], F32), axis_dim)
    return (jnp.tile(cos, (1, ATT_HEADS)), jnp.tile(sin, (1, ATT_HEADS)),
            jnp.tile(cos, (1, ATT_KV_HEADS)), jnp.tile(sin, (1, ATT_KV_HEADS)))


def _attn_body(q_ref, k_ref, v_ref, o_ref):
    q = q_ref[...]
    k = k_ref[0, 0]
    v = v_ref[0, 0]
    q_head = lax.broadcasted_iota(jnp.int32, q.shape, 1) // ATT_HEAD_DIM
    v_head = lax.broadcasted_iota(jnp.int32, v.shape, 1) // ATT_HEAD_DIM
    acc = jnp.zeros(q.shape, F32)
    for g in range(ATT_GROUP):
        qg = jnp.where(q_head == g, q, jnp.zeros_like(q))
        s = lax.dot_general(qg, k, (((1,), (1,)), ((), ())), preferred_element_type=F32)
        e = jnp.exp(s - jnp.max(s, axis=-1, keepdims=True))
        vg = jnp.where(v_head == g, v, jnp.zeros_like(v))
        o = jnp.dot(e.astype(BF16), vg, preferred_element_type=F32)
        acc = acc + o / jnp.sum(e, axis=-1, keepdims=True)
    o_ref[...] = acc.astype(BF16)


def _attention(q, k_rep, v_rep, *, n_seq, seq_len, tq):
    s_len = k_rep.shape[2]
    nq = seq_len // tq
    kv = pl.BlockSpec((1, 1, s_len, GROUP_W), lambda b, kh, i: (b, kh, 0, 0))
    qo = pl.BlockSpec((tq, GROUP_W), lambda b, kh, i: (b * nq + i, kh))
    return pl.pallas_call(
        _attn_body,
        out_shape=jax.ShapeDtypeStruct((n_seq * seq_len, ATT_W), BF16),
        grid=(n_seq, ATT_KV_HEADS, nq),
        in_specs=[qo, kv, kv],
        out_specs=qo,
        compiler_params=_params(3),
        name="attention",
    )(q, k_rep, v_rep)


def _rep_heads(x, n_seq):
    x = x.reshape(n_seq, -1, ATT_KV_HEADS, ATT_HEAD_DIM).transpose(0, 2, 1, 3)
    return jnp.tile(x, (1, 1, 1, ATT_GROUP))


def _hy_filter_body(feat_ref, t_ref, w1_ref, b1_ref, w2_ref, b2_ref, fr_ref, w3f_ref, w3b_ref, dl_ref,
                    hsum_ref, hdiff_ref, nyq_ref, z_ref):
    @pl.when(pl.program_id(0) == 0)
    def _():
        z = jnp.dot(feat_ref[...], w1_ref[...], preferred_element_type=F32, precision=HIGHEST) + b1_ref[...]
        z = jnp.sin(fr_ref[0:1, :] * z)
        z = jnp.dot(z, w2_ref[...], preferred_element_type=F32, precision=HIGHEST) + b2_ref[...]
        z_ref[...] = jnp.sin(fr_ref[1:2, :] * z)

    z = z_ref[...]
    window = jnp.exp(-t_ref[...] * dl_ref[...])
    h_f = jnp.dot(z, w3f_ref[...], preferred_element_type=F32, precision=HIGHEST) * window
    h_b = jnp.dot(z, w3b_ref[...], preferred_element_type=F32, precision=HIGHEST) * window
    row = lax.broadcasted_iota(jnp.int32, h_f.shape, 0)
    h_b = jnp.where(row == 0, 0.0, h_b)
    inv = 1.0 / (jnp.sum(jnp.abs(h_f), axis=0, keepdims=True) + jnp.sum(jnp.abs(h_b), axis=0, keepdims=True))
    h_sum = (h_f + h_b) * inv
    hsum_ref[...] = h_sum
    hdiff_ref[...] = (h_f - h_b) * inv
    nyq_ref[...] = jnp.sum(jnp.where(row % 2 == 0, h_sum, -h_sum), axis=0, keepdims=True)


def _hy_filter(seq_len, w1, b1, w2, b2, w3, sin_freq):
    tc = 256
    fw = w1.shape[1]
    t = jnp.arange(seq_len, dtype=F32)[:, None] / seq_len
    bands = jnp.arange(1, HY_BANDS + 1, dtype=F32)[None, :]
    feat = jnp.concatenate([t, jnp.sin(2.0 * math.pi * bands * t), jnp.cos(2.0 * math.pi * bands * t)], axis=-1)
    feat = jnp.pad(feat, ((0, 0), (0, LANES - HY_EMB)))
    deltas = jnp.abs(jnp.linspace(math.log(HY_TARGET) / HY_LONG_PCT, math.log(HY_TARGET) / HY_SHORT_PCT, D,
                                  dtype=F32)).reshape(1, D)
    pad_w = LANES - fw
    full = lambda shape: pl.BlockSpec(shape, lambda j: (0,) * len(shape))
    return pl.pallas_call(
        _hy_filter_body,
        out_shape=(jax.ShapeDtypeStruct((seq_len, D), F32), jax.ShapeDtypeStruct((seq_len, D), F32),
                   jax.ShapeDtypeStruct((1, D), F32)),
        grid=(D // tc,),
        in_specs=[full((seq_len, LANES)), full((seq_len, 1)), full((LANES, LANES)), full((1, LANES)),
                  full((LANES, LANES)), full((1, LANES)), full((2, LANES)),
                  pl.BlockSpec((LANES, tc), lambda j: (0, j)), pl.BlockSpec((LANES, tc), lambda j: (0, D // tc + j)),
                  pl.BlockSpec((1, tc), lambda j: (0, j))],
        out_specs=(pl.BlockSpec((seq_len, tc), lambda j: (0, j)), pl.BlockSpec((seq_len, tc), lambda j: (0, j)),
                   pl.BlockSpec((1, tc), lambda j: (0, j))),
        scratch_shapes=[pltpu.VMEM((seq_len, LANES), F32)],
        compiler_params=_params(1),
        name="hyena_filter",
    )(feat, t, jnp.pad(w1, ((0, LANES - HY_EMB), (0, pad_w))), jnp.pad(b1, (0, pad_w)).reshape(1, LANES),
      jnp.pad(w2, ((0, pad_w), (0, pad_w))), jnp.pad(b2, (0, pad_w)).reshape(1, LANES),
      jnp.pad(sin_freq, ((0, 0), (0, pad_w))), jnp.pad(w3, ((0, pad_w), (0, 0))), jnp.pad(w3, ((0, pad_w), (0, 0))),
      deltas)


def _dft_matrices(seq_len):
    n = 2 * seq_len
    k = lax.broadcasted_iota(jnp.int32, (seq_len, seq_len), 0)
    t = lax.broadcasted_iota(jnp.int32, (seq_len, seq_len), 1)
    ang = ((k * t) % n).astype(F32) * (2.0 * math.pi / n)
    cr, base = jnp.cos(ang), -jnp.sin(ang)
    ci = jnp.where(k == 0, (1 - 2 * (t % 2)).astype(F32), base)
    cit = jnp.where(t == 0, (1 - 2 * (k % 2)).astype(F32), base)
    return cr.astype(BF16), ci.astype(BF16), cit.astype(BF16)


def _dft_fwd_body(cr_ref, ci_ref, b1_ref, b2_ref, *refs, nk, tf, mult):
    if mult:
        kr_ref, ki_ref, or_ref, oi_ref, accr_ref, acci_ref = refs
    else:
        or_ref, oi_ref, accr_ref, acci_ref = refs
    kk = pl.program_id(3)
    pr = jnp.dot(cr_ref[...], b1_ref[0].astype(BF16), preferred_element_type=F32)
    pi = jnp.dot(ci_ref[...], b2_ref[0].astype(BF16), preferred_element_type=F32)

    @pl.when(kk == 0)
    def _():
        accr_ref[...] = pr
        acci_ref[...] = pi

    @pl.when(kk > 0)
    def _():
        accr_ref[...] += pr
        acci_ref[...] += pi

    @pl.when(kk == nk - 1)
    def _():
        zr, zi = accr_ref[...], acci_ref[...]
        if mult:
            kr, ki = kr_ref[...], ki_ref[...]
            first = (pl.program_id(1) * tf + lax.broadcasted_iota(jnp.int32, zr.shape, 0)) == 0
            or_ref[0] = jnp.where(first, 0.5 * zr * kr, zr * kr - zi * ki).astype(or_ref.dtype)
            oi_ref[0] = jnp.where(first, 0.5 * zi * ki, zr * ki + zi * kr).astype(oi_ref.dtype)
        else:
            or_ref[0] = zr
            oi_ref[0] = zi


def _dft_fwd(mats, b1, b2, *, row0, n_seq, seq_len, tf, tk, tn, filt=None):
    cr, ci, _ = mats
    nk = seq_len // tk
    r0 = row0 // tk
    a_spec = pl.BlockSpec((tf, tk), lambda b, f, c, kk: (f, kk))
    b_spec = pl.BlockSpec((1, tk, tn), lambda b, f, c, kk: (0, r0 + b * nk + kk, c))
    o_spec = pl.BlockSpec((1, tf, tn), lambda b, f, c, kk: (b, f, c))
    in_specs = [a_spec] * 2 + [b_spec] * 2
    args = [cr, ci, b1[None], b2[None]]
    if filt is not None:
        in_specs += [pl.BlockSpec((tf, tn), lambda b, f, c, kk: (f, c))] * 2
        args += list(filt)
    shape = jax.ShapeDtypeStruct((n_seq, seq_len, D), F32 if filt is None else BF16)
    return pl.pallas_call(
        functools.partial(_dft_fwd_body, nk=nk, tf=tf, mult=filt is not None),
        out_shape=(shape, shape),
        grid=(n_seq, seq_len // tf, D // tn, nk),
        in_specs=in_specs,
        out_specs=(o_spec, o_spec),
        scratch_shapes=[pltpu.VMEM((tf, tn), F32), pltpu.VMEM((tf, tn), F32)],
        compiler_params=_params(4),
        name="hyena_dft",
    )(*args)


def _dft_inv_body(cr_ref, ct_ref, yr_ref, yi_ref, z_ref, x0_ref, skip_ref, o_ref, acc_ref, *, nk, scale):
    kk = pl.program_id(3)
    p = (jnp.dot(cr_ref[...], yr_ref[0], preferred_element_type=F32)
         + jnp.dot(ct_ref[...], yi_ref[0], preferred_element_type=F32))

    @pl.when(kk == 0)
    def _():
        acc_ref[...] = p

    @pl.when(kk > 0)
    def _():
        acc_ref[...] += p

    @pl.when(kk == nk - 1)
    def _():
        z = z_ref[...]
        o_ref[...] = ((acc_ref[...] * scale + z * skip_ref[...]) * x0_ref[...]).astype(BF16)


def _dft_inv(mats, yr, yi, z, x0, skip, *, row0, n_seq, seq_len, tm, tk, tn):
    cr, _, cit = mats
    nk = seq_len // tk
    nt = seq_len // tm
    r0 = row0 // tm
    a_spec = pl.BlockSpec((tm, tk), lambda b, i, c, kk: (i, kk))
    y_spec = pl.BlockSpec((1, tk, tn), lambda b, i, c, kk: (b, kk, c))
    tok = pl.BlockSpec((tm, tn), lambda b, i, c, kk: (r0 + b * nt + i, c))
    return pl.pallas_call(
        functools.partial(_dft_inv_body, nk=nk, scale=1.0 / seq_len),
        out_shape=jax.ShapeDtypeStruct((n_seq * seq_len, D), BF16),
        grid=(n_seq, nt, D // tn, nk),
        in_specs=[a_spec] * 2 + [y_spec] * 2 + [tok, tok, pl.BlockSpec((1, tn), lambda b, i, c, kk: (0, c))],
        out_specs=pl.BlockSpec((tm, tn), lambda b, i, c, kk: (b * nt + i, c)),
        scratch_shapes=[pltpu.VMEM((tm, tn), F32)],
        compiler_params=_params(4),
        name="hyena_idft",
    )(cr, cit, yr, yi, z, x0, skip.reshape(1, D))


def _hy_gate_body(x0_ref, x1_ref, v_ref, w0_ref, w1_ref, wv_ref, b0_ref, b1_ref, bv_ref, z_ref, x0o_ref):
    def conv(u_ref, w_ref, b_ref):
        u = u_ref[...]
        n = u.shape[0]
        row = lax.broadcasted_iota(jnp.int32, u.shape, 0)
        prev = jnp.where(row == 0, 0.0, pltpu.roll(u, 1, 0))
        nxt = jnp.where(row == n - 1, 0.0, pltpu.roll(u, n - 1, 0))
        return prev * w_ref[0:1, :] + u * w_ref[1:2, :] + nxt * w_ref[2:3, :] + b_ref[...]

    x0o_ref[...] = conv(x0_ref, w0_ref, b0_ref)
    z_ref[...] = conv(v_ref, wv_ref, bv_ref) * conv(x1_ref, w1_ref, b1_ref)


def _hy_gate(u, conv_w, conv_b, *, row0, n_seq, seq_len):
    tc = 128
    nb = D // tc
    r0 = row0 // seq_len

    def col(part):
        return [pl.BlockSpec((seq_len, tc), lambda b, c: (r0 + b, part * nb + c)),
                pl.BlockSpec((3, tc), lambda b, c: (0, part * nb + c)),
                pl.BlockSpec((1, tc), lambda b, c: (0, part * nb + c))]

    specs = [col(p) for p in range(3)]
    out = pl.BlockSpec((seq_len, tc), lambda b, c: (b, c))
    shape = jax.ShapeDtypeStruct((n_seq * seq_len, D), F32)
    return pl.pallas_call(
        _hy_gate_body,
        out_shape=(shape, shape),
        grid=(n_seq, nb),
        in_specs=[s[0] for s in specs] + [s[1] for s in specs] + [s[2] for s in specs],
        out_specs=(out, out),
        compiler_params=_params(2),
        name="hyena_short_conv",
    )(u, u, u, conv_w, conv_w, conv_w, conv_b.reshape(1, 3 * D), conv_b.reshape(1, 3 * D), conv_b.reshape(1, 3 * D))


ROW_TILE = D // LANES
ROUTER_TM = 512
EXPERT_TM = 512
N_SLOTS = 2 * N_TOK + N_EXPERTS * EXPERT_TM
N_SLOT_TILES = N_SLOTS // EXPERT_TM
INFO_E1, INFO_E2, INFO_R1, INFO_R2, INFO_W1, INFO_W2 = range(6)


def _to_row_tiles(ref, x):
    rows = x.shape[0]
    for j in range(ROW_TILE):
        ref[pl.ds(j, rows, stride=ROW_TILE), :] = x[:, j * LANES:(j + 1) * LANES]


def _from_row_tiles(ref, rows):
    return jnp.concatenate([ref[pl.ds(j, rows, stride=ROW_TILE), :] for j in range(ROW_TILE)], axis=-1)


def _router_body(x_ref, sh_ref, sc_ref, w_ref, info_ref, incl_ref, cnt_ref):
    @pl.when(pl.program_id(0) == 0)
    def _():
        cnt_ref[...] = jnp.zeros_like(cnt_ref)

    h = _modulate(x_ref, sh_ref, sc_ref)
    logits = jnp.dot(h, w_ref[...], preferred_element_type=F32, precision=HIGHEST)
    lane = lax.broadcasted_iota(jnp.int32, logits.shape, 1).astype(F32)
    logits = jnp.where(lane < N_EXPERTS, logits, -jnp.inf)
    e = jnp.exp(logits - jnp.max(logits, axis=-1, keepdims=True))
    p = e / jnp.sum(e, axis=-1, keepdims=True)
    p1 = jnp.max(p, axis=-1, keepdims=True)
    i1 = jnp.min(jnp.where(p == p1, lane, float(LANES)), axis=-1, keepdims=True)
    rest = jnp.where(lane == i1, -1.0, p)
    p2 = jnp.max(rest, axis=-1, keepdims=True)
    i2 = jnp.min(jnp.where(rest == p2, lane, float(LANES)), axis=-1, keepdims=True)
    total = p1 + p2
    chosen = jnp.where((lane == i1) | (lane == i2), 1.0, 0.0)
    tm = chosen.shape[0]
    earlier = (lax.broadcasted_iota(jnp.int32, (tm, tm), 1) < lax.broadcasted_iota(jnp.int32, (tm, tm), 0))
    rank = jnp.dot(earlier.astype(BF16), chosen.astype(BF16), preferred_element_type=F32) + cnt_ref[...]
    r1 = jnp.sum(jnp.where(lane == i1, rank, 0.0), axis=-1, keepdims=True)
    r2 = jnp.sum(jnp.where(lane == i2, rank, 0.0), axis=-1, keepdims=True)
    cnt_ref[...] += jnp.sum(chosen, axis=0, keepdims=True)
    incl_ref[0] = jnp.broadcast_to(cnt_ref[...], incl_ref.shape[1:])
    info = jnp.zeros_like(p)
    for col, val in ((INFO_E1, i1), (INFO_E2, i2), (INFO_R1, r1), (INFO_R2, r2),
                     (INFO_W1, p1 / total), (INFO_W2, p2 / total)):
        info = jnp.where(lane == col, val, info)
    info_ref[...] = info


def _router(x, sh, sc, w_router):
    tm = ROUTER_TM
    return pl.pallas_call(
        _router_body,
        out_shape=(jax.ShapeDtypeStruct((N_TOK, LANES), F32),
                   jax.ShapeDtypeStruct((N_TOK // tm, 8, LANES), F32)),
        grid=(N_TOK // tm,),
        in_specs=_mod_specs(tm, 1, 0) + [pl.BlockSpec((D, LANES), lambda i: (0, 0))],
        out_specs=(pl.BlockSpec((tm, LANES), lambda i: (i, 0)), pl.BlockSpec((1, 8, LANES), lambda i: (i, 0, 0))),
        scratch_shapes=[pltpu.VMEM((1, LANES), F32)],
        compiler_params=_params(1),
        name="moe_router",
    )(x, sh, sc, jnp.pad(w_router, ((0, 0), (0, LANES - N_EXPERTS))))


def _row_tile(ref, row):
    return ref.at[pl.ds(pl.multiple_of(row * ROW_TILE, ROW_TILE), ROW_TILE)]


def _dispatch_body(pos1_ref, pos2_ref, x_ref, zeros_hbm, xs_hbm, rows_ref, sem, *, tm):
    del zeros_hbm
    base = pl.program_id(0) * tm
    _to_row_tiles(rows_ref, x_ref[...])

    def copies(r):
        src = _row_tile(rows_ref, r)
        return (pltpu.make_async_copy(src, _row_tile(xs_hbm, pos1_ref[base + r]), sem),
                pltpu.make_async_copy(src, _row_tile(xs_hbm, pos2_ref[base + r]), sem))

    def issue(r, carry):
        for cp in copies(r):
            cp.start()
        return carry

    def drain(r, carry):
        for cp in copies(r):
            cp.wait()
        return carry

    lax.fori_loop(0, tm, issue, 0)
    lax.fori_loop(0, tm, drain, 0)


def _dispatch(pos1, pos2, x):
    tm = 512
    return pl.pallas_call(
        functools.partial(_dispatch_body, tm=tm),
        out_shape=jax.ShapeDtypeStruct((N_SLOTS * ROW_TILE, LANES), F32),
        grid_spec=pltpu.PrefetchScalarGridSpec(
            num_scalar_prefetch=2, grid=(N_TOK // tm,),
            in_specs=[pl.BlockSpec((tm, D), lambda i, *_: (i, 0)), pl.BlockSpec(memory_space=pl.ANY)],
            out_specs=pl.BlockSpec(memory_space=pl.ANY),
            scratch_shapes=[pltpu.VMEM((tm * ROW_TILE, LANES), F32), pltpu.SemaphoreType.DMA(())]),
        input_output_aliases={3: 0},
        compiler_params=_params(1),
        name="moe_dispatch",
    )(pos1, pos2, x, jnp.zeros((N_SLOTS * ROW_TILE, LANES), F32))


def _expert_swiglu_body(eid_ref, b1_ref, b2_ref, nv_ref, xs_ref, sh_ref, sc_ref, wg_ref, wu_ref, o_ref, *, fc):
    t = pl.program_id(0)

    @pl.when(t < nv_ref[0])
    def _():
        tm = o_ref.shape[0]
        x = _from_row_tiles(xs_ref, tm)
        slot = t * tm + lax.broadcasted_iota(jnp.int32, (tm, 1), 0)
        in1, in2 = slot >= b1_ref[t], slot >= b2_ref[t]
        sc = jnp.where(in2, sc_ref[2], jnp.where(in1, sc_ref[1], sc_ref[0]))
        sh = jnp.where(in2, sh_ref[2], jnp.where(in1, sh_ref[1], sh_ref[0]))
        h = (x * (1.0 + sc) + sh).astype(BF16)
        for c in range(o_ref.shape[1] // fc):
            cs = slice(c * fc, (c + 1) * fc)
            g = jnp.dot(h, wg_ref[0, :, cs], preferred_element_type=F32)
            u = jnp.dot(h, wu_ref[0, :, cs], preferred_element_type=F32)
            o_ref[:, cs] = (g * jax.nn.sigmoid(g) * u).astype(BF16)

    @pl.when(t >= nv_ref[0])
    def _():
        o_ref[...] = jnp.zeros_like(o_ref)


def _expert_swiglu(meta, xs_rt, sh, sc, w_gate, w_up):
    eid, b1, b2, nv = meta
    tm, f = EXPERT_TM, w_gate.shape[2]

    def tile(t, eid, b1, b2, nv):
        return jnp.minimum(t, nv[0] - 1)

    w_spec = pl.BlockSpec((1, D, f), lambda t, eid, b1, b2, nv: (eid[tile(t, eid, b1, b2, nv)], 0, 0))
    mod = pl.BlockSpec((8, 1, D), lambda t, *_: (0, 0, 0))
    return pl.pallas_call(
        functools.partial(_expert_swiglu_body, fc=512),
        out_shape=jax.ShapeDtypeStruct((N_SLOTS, f), BF16),
        grid_spec=pltpu.PrefetchScalarGridSpec(
            num_scalar_prefetch=4, grid=(N_SLOT_TILES,),
            in_specs=[pl.BlockSpec((tm * ROW_TILE, LANES), lambda t, *m: (tile(t, *m), 0)), mod, mod, w_spec, w_spec],
            out_specs=pl.BlockSpec((tm, f), lambda t, *m: (t, 0))),
        compiler_params=_params(1, 56 * 1024 * 1024),
        name="moe_swiglu",
    )(eid, b1, b2, nv, xs_rt, sh, sc, w_gate, w_up)


def _expert_down_body(eid_ref, nv_ref, a_ref, w_ref, y_ref):
    @pl.when(pl.program_id(0) < nv_ref[0])
    def _():
        _to_row_tiles(y_ref, jnp.dot(a_ref[...], w_ref[0], preferred_element_type=F32))

    @pl.when(pl.program_id(0) >= nv_ref[0])
    def _():
        y_ref[...] = jnp.zeros_like(y_ref)


def _expert_down(meta, act, w_down):
    eid, _, _, nv = meta
    tm, f = EXPERT_TM, act.shape[1]

    def tile(t, eid, nv):
        return jnp.minimum(t, nv[0] - 1)

    return pl.pallas_call(
        _expert_down_body,
        out_shape=jax.ShapeDtypeStruct((N_SLOTS * ROW_TILE, LANES), F32),
        grid_spec=pltpu.PrefetchScalarGridSpec(
            num_scalar_prefetch=2, grid=(N_SLOT_TILES,),
            in_specs=[pl.BlockSpec((tm, f), lambda t, *m: (tile(t, *m), 0)),
                      pl.BlockSpec((1, f, D), lambda t, eid, nv: (eid[tile(t, eid, nv)], 0, 0))],
            out_specs=pl.BlockSpec((tm * ROW_TILE, LANES), lambda t, *m: (t, 0))),
        compiler_params=_params(1),
        name="moe_down",
    )(eid, nv, act, w_down)


def _combine_body(pos1_ref, pos2_ref, y_hbm, info_ref, x_ref, g_ref, lng_ref, lnb_ref, o_ref, y1_ref, y2_ref, sem,
                  *, tm):
    base = pl.program_id(0) * tm

    def copies(r):
        return (pltpu.make_async_copy(_row_tile(y_hbm, pos1_ref[base + r]), _row_tile(y1_ref, r), sem),
                pltpu.make_async_copy(_row_tile(y_hbm, pos2_ref[base + r]), _row_tile(y2_ref, r), sem))

    def issue(r, carry):
        for cp in copies(r):
            cp.start()
        return carry

    def drain(r, carry):
        for cp in copies(r):
            cp.wait()
        return carry

    lax.fori_loop(0, tm, issue, 0)
    lax.fori_loop(0, tm, drain, 0)
    info = info_ref[...]
    ffn = (info[:, INFO_W1:INFO_W1 + 1] * _from_row_tiles(y1_ref, tm)
           + info[:, INFO_W2:INFO_W2 + 1] * _from_row_tiles(y2_ref, tm))
    y = ALPHA * x_ref[...] + g_ref[0] * ffn
    mu = jnp.mean(y, axis=-1, keepdims=True)
    yc = y - mu
    var = jnp.mean(yc * yc, axis=-1, keepdims=True)
    o_ref[...] = yc * lax.rsqrt(var + LN_EPS) * lng_ref[...] + lnb_ref[...]


def _combine(pos1, pos2, y_rt, info, x, gate, ln_g, ln_b):
    tm = 512
    return pl.pallas_call(
        functools.partial(_combine_body, tm=tm),
        out_shape=jax.ShapeDtypeStruct((N_TOK, D), F32),
        grid_spec=pltpu.PrefetchScalarGridSpec(
            num_scalar_prefetch=2, grid=(N_TOK // tm,),
            in_specs=[pl.BlockSpec(memory_space=pl.ANY),
                      pl.BlockSpec((tm, LANES), lambda i, *_: (i, 0)),
                      pl.BlockSpec((tm, D), lambda i, *_: (i, 0)),
                      pl.BlockSpec((1, 1, D), lambda i, *_: (_group_of_row(i * tm), 0, 0)),
                      pl.BlockSpec((1, D), lambda i, *_: (0, 0)),
                      pl.BlockSpec((1, D), lambda i, *_: (0, 0))],
            out_specs=pl.BlockSpec((tm, D), lambda i, *_: (i, 0)),
            scratch_shapes=[pltpu.VMEM((tm * ROW_TILE, LANES), F32), pltpu.VMEM((tm * ROW_TILE, LANES), F32),
                            pltpu.SemaphoreType.DMA(())]),
        compiler_params=_params(1),
        name="moe_combine",
    )(pos1, pos2, y_rt, info, x, gate, ln_g.reshape(1, D), ln_b.reshape(1, D))


def _slot_plan(info, incl):
    row = lambda n_rows: incl[n_rows // ROUTER_TM - 1, 0, :N_EXPERTS].astype(jnp.int32)
    count = row(N_TOK)
    padded = (count + EXPERT_TM - 1) // EXPERT_TM * EXPERT_TM
    end = jnp.cumsum(padded)
    start = end - padded
    tile_row = jnp.arange(N_SLOT_TILES, dtype=jnp.int32) * EXPERT_TM
    eid = jnp.minimum(jnp.sum(tile_row[:, None] >= end[None, :], axis=1), N_EXPERTS - 1).astype(jnp.int32)
    b1 = (start + row(N_CTX))[eid]
    b2 = (start + row(N_CTX + DEC_SEQ))[eid]
    nv = (end[-1:] // EXPERT_TM).astype(jnp.int32)
    e1, e2 = info[:, INFO_E1].astype(jnp.int32), info[:, INFO_E2].astype(jnp.int32)
    pos1 = start[e1] + info[:, INFO_R1].astype(jnp.int32)
    pos2 = start[e2] + info[:, INFO_R2].astype(jnp.int32)
    return pos1, pos2, (eid, b1, b2, nv)


def _even_mixer(x, sh, sc, gate, ln_g, ln_b, w_in, b_igate, b_fgate, ml_norm_g, q_norm_g, k_norm_g, w_out,
                st_c, st_n, st_m, cache_k, cache_v, rope_tabs):
    splits = (4 * ML_W, 4 * ML_W + N_GATES)
    w_main = jnp.concatenate([w_in[:, :splits[0]], w_in[:, splits[1]:]], axis=1).astype(BF16)
    proj = _mod_matmul(x, sh, sc, [w_main[None]], tm=1024, tn=MAIN_W // 2, out_dtype=F32, name="even_in_proj")[0]
    b_gate = jnp.stack([b_igate, b_fgate], axis=1).reshape(N_GATES)
    lic, bc, lir, br = _gates(x, sh, sc, w_in[:, splits[0]:splits[1]], b_gate)

    hf_c, hb_c, new_c, new_n, new_m = _mlstm(proj, lic, bc, lir, br, row0=0, n_seq=BATCH, seq_len=SEQ)
    init = (st_c, st_n, jnp.broadcast_to(st_m[..., None], st_n.shape))
    hf_s, hb_s, _, _, _ = _mlstm(proj, lic, bc, lir, br, row0=N_CTX, n_seq=DEC_BATCH, seq_len=DEC_SEQ, init=init)
    ml = _ml_post(jnp.concatenate([hf_c, hf_s]), jnp.concatenate([hb_c, hb_s]), proj, ml_norm_g)

    q_c, kn_c, kb_c = _qk_prep(proj, q_norm_g, k_norm_g, row0=0, rows=N_CTX)
    q_s, _, kb_s = _qk_prep(proj, q_norm_g, k_norm_g, row0=N_CTX, rows=N_LAT, rope_tabs=rope_tabs)
    v_all = proj[:, MAIN_W - KV_W:]
    v_c, v_s = v_all[:N_CTX], v_all[N_CTX:]
    att_c = _attention(q_c, _rep_heads(kb_c, BATCH), _rep_heads(v_c.astype(BF16), BATCH),
                       n_seq=BATCH, seq_len=SEQ, tq=SEQ)
    k_lat = jnp.concatenate([kb_s.reshape(DEC_BATCH, DEC_SEQ, KV_W),
                             cache_k.reshape(DEC_BATCH, PAST_LEN, KV_W).astype(BF16)], axis=1)
    v_lat = jnp.concatenate([v_s.reshape(DEC_BATCH, DEC_SEQ, KV_W).astype(BF16),
                             cache_v.reshape(DEC_BATCH, PAST_LEN, KV_W).astype(BF16)], axis=1)
    att_s = _attention(q_s, _rep_heads(k_lat.reshape(-1, KV_W), DEC_BATCH), _rep_heads(v_lat.reshape(-1, KV_W), DEC_BATCH),
                       n_seq=DEC_BATCH, seq_len=DEC_SEQ, tq=256)

    mixed = jnp.concatenate([ml, jnp.concatenate([att_c, att_s])], axis=1)
    x = _matmul_res_ln(mixed[None], w_out.astype(BF16)[None], x, gate, ln_g, ln_b, tm=512, name="even_out_proj")
    new_k = kn_c.reshape(BATCH, SEQ, ATT_KV_HEADS, ATT_HEAD_DIM)
    new_v = v_c.reshape(BATCH, SEQ, ATT_KV_HEADS, ATT_HEAD_DIM)
    return x, new_k, new_v, new_c, new_n, new_m[..., 0]


def _hyena_mixer(x, sh, sc, gate, ln_g, ln_b, w_in, conv_w, conv_b, w1, b1, w2, b2, w3, sin_freq, skip, w_out, dft):
    u = _mod_matmul(x, sh, sc, [w_in.astype(BF16)[None]], tm=1024, tn=512, out_dtype=F32, name="hyena_in_proj")[0]
    ys = []
    for row0, n_seq, seq_len, tiles in ((0, BATCH, SEQ, dict(tf=256, tk=256, tn=1024)),
                                        (N_CTX, DEC_BATCH, DEC_SEQ, dict(tf=1024, tk=512, tn=512))):
        mats = dft[seq_len]
        h_sum, h_diff, nyq = _hy_filter(seq_len, w1, b1, w2, b2, w3, sin_freq)
        kr, ki = _dft_fwd(mats, h_sum, h_diff, row0=0, n_seq=1, seq_len=seq_len, **tiles)
        kr, ki = kr[0], ki[0].at[0].set(nyq[0])
        z, x0 = _hy_gate(u, conv_w, conv_b, row0=row0, n_seq=n_seq, seq_len=seq_len)
        yr, yi = _dft_fwd(mats, z, z, row0=0, n_seq=n_seq, seq_len=seq_len, filt=(kr, ki), **tiles)
        ys.append(_dft_inv(mats, yr, yi, z, x0, skip, row0=0, n_seq=n_seq, seq_len=seq_len,
                           tm=tiles["tf"], tk=tiles["tk"], tn=tiles["tn"]))
    y = jnp.concatenate(ys)
    return _matmul_res_ln(y[None], w_out.astype(BF16)[None], x, gate, ln_g, ln_b, tm=512, name="hyena_out_proj")


def _dense_ffn(x, sh, sc, gate, ln_g, ln_b, w_gate, w_up, w_down):
    act = _mod_matmul(x, sh, sc, [w_gate.astype(BF16)[None], w_up.astype(BF16)[None]],
                      tm=1024, tn=D_FF // 2, out_dtype=BF16, name="ffn_swiglu")
    return _matmul_res_ln(act, w_down.astype(BF16)[None], x, gate, ln_g, ln_b, tm=512, name="ffn_down")


def _moe_ffn(x, sh, sc, gate, ln_g, ln_b, w_router, w_gate, w_up, w_down):
    info, incl = _router(x, sh, sc, w_router)
    pos1, pos2, meta = _slot_plan(info, incl)
    xs_rt = _dispatch(pos1, pos2, x)
    act = _expert_swiglu(meta, xs_rt, sh, sc, w_gate.astype(BF16), w_up.astype(BF16))
    y_rt = _expert_down(meta, act, w_down.astype(BF16))
    return _combine(pos1, pos2, y_rt, info, x, gate, ln_g, ln_b)


def kernel(x_prompt, x_sample, cache_attn_k, cache_attn_v, state_mlstm_C, state_mlstm_n, state_mlstm_m, c, c_ctx, w_ada, b_ada, ln_g, ln_b, w_in_even, b_igate, b_fgate, ml_norm_g, q_norm_g, k_norm_g, w_out_even, w_ffn_gate, w_ffn_up, w_ffn_down, w_in_hy, hy_conv_w, hy_conv_b, hy_filt_w1, hy_filt_b1, hy_filt_w2, hy_filt_b2, hy_filt_w3, hy_sin_freq, hy_skip, w_out_hy, w_router, w_moe_gate, w_moe_up, w_moe_down):
    x = jnp.concatenate([x_prompt.reshape(N_CTX, D), x_sample.reshape(N_LAT, D)])
    cvec = jnp.concatenate([c_ctx[None], c, jnp.zeros((8 - 1 - DEC_BATCH, D), F32)])
    mods = _ada(cvec, w_ada, b_ada)
    rope_tabs = _rope_tables()
    dft = {SEQ: _dft_matrices(SEQ), DEC_SEQ: _dft_matrices(DEC_SEQ)}
    new_k, new_v, new_c, new_n, new_m = [], [], [], [], []
    for layer in range(DEPTH):
        sh1, sc1, g1, sh2, sc2, g2 = (mods[layer, :, i * D:(i + 1) * D].reshape(8, 1, D) for i in range(6))
        i = layer // 2
        if layer % 2 == 0:
            x, k_c, v_c, st_c, st_n, st_m = _even_mixer(
                x, sh1, sc1, g1, ln_g[layer, 0], ln_b[layer, 0], w_in_even[i], b_igate[i], b_fgate[i], ml_norm_g[i],
                q_norm_g[i], k_norm_g[i], w_out_even[i], state_mlstm_C[:, i], state_mlstm_n[:, i], state_mlstm_m[:, i],
                cache_attn_k[:, i], cache_attn_v[:, i], rope_tabs)
            new_k.append(k_c)
            new_v.append(v_c)
            new_c.append(st_c)
            new_n.append(st_n)
            new_m.append(st_m)
            x = _dense_ffn(x, sh2, sc2, g2, ln_g[layer, 1], ln_b[layer, 1], w_ffn_gate[i], w_ffn_up[i], w_ffn_down[i])
        else:
            x = _hyena_mixer(x, sh1, sc1, g1, ln_g[layer, 0], ln_b[layer, 0], w_in_hy[i], hy_conv_w[i], hy_conv_b[i],
                             hy_filt_w1[i], hy_filt_b1[i], hy_filt_w2[i], hy_filt_b2[i], hy_filt_w3[i], hy_sin_freq[i],
                             hy_skip[i], w_out_hy[i], dft)
            x = _moe_ffn(x, sh2, sc2, g2, ln_g[layer, 1], ln_b[layer, 1], w_router[i], w_moe_gate[i], w_moe_up[i],
                         w_moe_down[i])
    return (x[:N_CTX].reshape(BATCH, SEQ, D), x[N_CTX:].reshape(DEC_BATCH, DEC_SEQ, D),
            jnp.stack(new_k, axis=1), jnp.stack(new_v, axis=1), jnp.stack(new_c, axis=1),
            jnp.stack(new_n, axis=1), jnp.stack(new_m, axis=1))
```

```python
import functools
import math

import jax
import jax.numpy as jnp
import numpy as np
from jax import lax
from jax.experimental import pallas as pl
from jax.experimental.pallas import tpu as pltpu

F32 = jnp.float32
BF16 = jnp.bfloat16
HIGHEST = lax.Precision.HIGHEST

D = 1024
BATCH, SEQ = 32, 256
DEC_BATCH, DEC_SEQ = 2, 4096
DEPTH = 4
PAST_LEN = 256
GRID_W = 64
N_CTX = BATCH * SEQ
N_LAT = DEC_BATCH * DEC_SEQ
N_TOK = N_CTX + N_LAT

ML_HEADS, ML_HEAD_DIM = 4, 128
ML_W = ML_HEADS * ML_HEAD_DIM
CHUNK = 128
ATT_HEADS, ATT_KV_HEADS, ATT_HEAD_DIM = 8, 2, 64
ATT_GROUP = ATT_HEADS // ATT_KV_HEADS
ATT_W = ATT_HEADS * ATT_HEAD_DIM
KV_W = ATT_KV_HEADS * ATT_HEAD_DIM
GROUP_W = ATT_GROUP * ATT_HEAD_DIM
ROPE_BASE = 10000.0
N_GATES = 4 * ML_HEADS
MAIN_W = 4 * ML_W + ATT_W + 2 * KV_W

HY_EMB = 33
HY_BANDS = (HY_EMB - 1) // 2
HY_TARGET, HY_SHORT_PCT, HY_LONG_PCT = 1e-2, 0.3, 1.5
D_FF = 2816
N_EXPERTS = 8
MOE_D_FF = 3584
ALPHA = (2 * DEPTH) ** 0.25
LN_EPS = 1e-5
RMS_EPS = 1e-6

LANES = 128
VMEM_LIMIT = 48 * 1024 * 1024


def _params(n_axes, vmem=VMEM_LIMIT):
    return pltpu.CompilerParams(dimension_semantics=("arbitrary",) * n_axes, vmem_limit_bytes=vmem)


def _group_of_row(r):
    return jnp.where(r < N_CTX, 0, 1 + (r - N_CTX) // DEC_SEQ)


def _modulate(x_ref, sh_ref, sc_ref):
    return x_ref[...] * (1.0 + sc_ref[0]) + sh_ref[0]


def _mod_specs(tm, n_axes, row_axis):
    def rows(*ids):
        return (ids[row_axis], 0)

    def grp(*ids):
        return (_group_of_row(ids[row_axis] * tm), 0, 0)

    del n_axes
    return [pl.BlockSpec((tm, D), rows), pl.BlockSpec((1, 1, D), grp), pl.BlockSpec((1, 1, D), grp)]


def _ada_body(c_ref, w_ref, b_ref, o_ref):
    c = c_ref[...]
    s = c * jax.nn.sigmoid(c)
    o_ref[0] = jnp.dot(s, w_ref[0], preferred_element_type=F32, precision=HIGHEST) + b_ref[0]


def _ada(cvec, w_ada, b_ada):
    tn = 1536
    return pl.pallas_call(
        _ada_body,
        out_shape=jax.ShapeDtypeStruct((DEPTH, 8, 6 * D), F32),
        grid=(DEPTH, 6 * D // tn),
        in_specs=[pl.BlockSpec((8, D), lambda l, j: (0, 0)),
                  pl.BlockSpec((1, D, tn), lambda l, j: (l, 0, j)),
                  pl.BlockSpec((1, 1, tn), lambda l, j: (l, 0, j))],
        out_specs=pl.BlockSpec((1, 8, tn), lambda l, j: (l, 0, j)),
        compiler_params=_params(2),
        name="ada_modulation",
    )(cvec, w_ada, b_ada.reshape(DEPTH, 1, 6 * D))


def _mod_mm_body(x_ref, sh_ref, sc_ref, *refs, n_w):
    w_refs, o_ref, h_ref = refs[:n_w], refs[n_w], refs[n_w + 1]

    @pl.when(pl.program_id(2) == 0)
    def _():
        h_ref[...] = _modulate(x_ref, sh_ref, sc_ref).astype(BF16)

    h = h_ref[...]
    if n_w == 1:
        o = jnp.dot(h, w_refs[0][0], preferred_element_type=F32)
    else:
        g = jnp.dot(h, w_refs[0][0], preferred_element_type=F32)
        u = jnp.dot(h, w_refs[1][0], preferred_element_type=F32)
        o = g * jax.nn.sigmoid(g) * u
    o_ref[0] = o.astype(o_ref.dtype)


def _mod_matmul(x, sh, sc, ws, *, tm, tn, out_dtype, name):
    n_e, _, f = ws[0].shape
    return pl.pallas_call(
        functools.partial(_mod_mm_body, n_w=len(ws)),
        out_shape=jax.ShapeDtypeStruct((n_e, N_TOK, f), out_dtype),
        grid=(n_e, N_TOK // tm, f // tn),
        in_specs=_mod_specs(tm, 3, 1) + [pl.BlockSpec((1, D, tn), lambda e, i, j: (e, 0, j)) for _ in ws],
        out_specs=pl.BlockSpec((1, tm, tn), lambda e, i, j: (e, i, j)),
        scratch_shapes=[pltpu.VMEM((tm, D), BF16)],
        compiler_params=_params(3),
        name=name,
    )(x, sh, sc, *ws)


def _mm_res_ln_body(a_ref, w_ref, x_ref, g_ref, lng_ref, lnb_ref, *refs, n_e, use_gates):
    if use_gates:
        gates_ref, o_ref, acc_ref = refs
    else:
        o_ref, acc_ref = refs
    e = pl.program_id(1)
    p = jnp.dot(a_ref[0], w_ref[0], preferred_element_type=F32)
    if use_gates:
        gt = gates_ref[...]
        lane = lax.broadcasted_iota(jnp.int32, gt.shape, 1)
        p = p * jnp.sum(jnp.where(lane == e, gt, 0.0), axis=-1, keepdims=True)

    @pl.when(e == 0)
    def _():
        acc_ref[...] = p

    @pl.when(e > 0)
    def _():
        acc_ref[...] += p

    @pl.when(e == n_e - 1)
    def _():
        y = ALPHA * x_ref[...] + g_ref[0] * acc_ref[...]
        mu = jnp.mean(y, axis=-1, keepdims=True)
        yc = y - mu
        var = jnp.mean(yc * yc, axis=-1, keepdims=True)
        o_ref[...] = yc * lax.rsqrt(var + LN_EPS) * lng_ref[...] + lnb_ref[...]


def _matmul_res_ln(a, w, x, gate, ln_g, ln_b, *, tm, gates=None, name):
    n_e, _, k = a.shape
    use_gates = gates is not None
    in_specs = [pl.BlockSpec((1, tm, k), lambda i, e: (e, i, 0)),
                pl.BlockSpec((1, k, D), lambda i, e: (e, 0, 0)),
                pl.BlockSpec((tm, D), lambda i, e: (i, 0)),
                pl.BlockSpec((1, 1, D), lambda i, e: (_group_of_row(i * tm), 0, 0)),
                pl.BlockSpec((1, D), lambda i, e: (0, 0)),
                pl.BlockSpec((1, D), lambda i, e: (0, 0))]
    args = [a, w, x, gate, ln_g.reshape(1, D), ln_b.reshape(1, D)]
    if use_gates:
        in_specs.append(pl.BlockSpec((tm, LANES), lambda i, e: (i, 0)))
        args.append(gates)
    return pl.pallas_call(
        functools.partial(_mm_res_ln_body, n_e=n_e, use_gates=use_gates),
        out_shape=jax.ShapeDtypeStruct((N_TOK, D), F32),
        grid=(N_TOK // tm, n_e),
        in_specs=in_specs,
        out_specs=pl.BlockSpec((tm, D), lambda i, e: (i, 0)),
        scratch_shapes=[pltpu.VMEM((tm, D), F32)],
        compiler_params=_params(2),
        name=name,
    )(*args)


def _log_sigmoid(x):
    return jnp.minimum(x, 0.0) - jnp.log(1.0 + jnp.exp(-jnp.abs(x)))


def _gates_body(x_ref, sh_ref, sc_ref, wg_ref, wgt_ref, b_ref, bt_ref,
                lic_ref, bc_ref, lir_ref, br_ref, *, tm):
    h = _modulate(x_ref, sh_ref, sc_ref)
    g = jnp.dot(h, wg_ref[...], preferred_element_type=F32, precision=HIGHEST) + b_ref[...]
    gt = lax.dot_general(wgt_ref[...], h, (((1,), (1,)), ((), ())),
                         preferred_element_type=F32, precision=HIGHEST) + bt_ref[...]
    lic_ref[...] = g
    lir_ref[...] = gt
    lf, lft = _log_sigmoid(g), _log_sigmoid(gt)
    r = lax.broadcasted_iota(jnp.int32, (CHUNK, CHUNK), 0)
    c = lax.broadcasted_iota(jnp.int32, (CHUNK, CHUNK), 1)
    tri_l = (c <= r).astype(F32)
    tri_u = (c >= r).astype(F32)
    fwd_col = lax.broadcasted_iota(jnp.int32, (CHUNK, LANES), 1) < 2 * ML_HEADS
    fwd_row = lax.broadcasted_iota(jnp.int32, (N_GATES, CHUNK), 0) < 2 * ML_HEADS
    for ch in range(tm // CHUNK):
        sl = slice(ch * CHUNK, (ch + 1) * CHUNK)
        lfc, lftc = lf[sl, :], lft[:, sl]
        cum_f = jnp.dot(tri_l, lfc, preferred_element_type=F32, precision=HIGHEST)
        cum_b = jnp.dot(tri_u, lfc, preferred_element_type=F32, precision=HIGHEST)
        bc_ref[sl, :] = jnp.where(fwd_col, cum_f, cum_b)
        cum_f = jnp.dot(lftc, tri_u, preferred_element_type=F32, precision=HIGHEST)
        cum_b = jnp.dot(lftc, tri_l, preferred_element_type=F32, precision=HIGHEST)
        br_ref[:, sl] = jnp.where(fwd_row, cum_f, cum_b)


def _gates(x, sh, sc, wg, b_gate):
    tm = 256
    wg_pad = jnp.pad(wg, ((0, 0), (0, LANES - N_GATES)))
    b_pad = jnp.pad(b_gate, (0, LANES - N_GATES)).reshape(1, LANES)
    col = pl.BlockSpec((tm, LANES), lambda i: (i, 0))
    row = pl.BlockSpec((N_GATES, tm), lambda i: (0, i))
    return pl.pallas_call(
        functools.partial(_gates_body, tm=tm),
        out_shape=(jax.ShapeDtypeStruct((N_TOK, LANES), F32), jax.ShapeDtypeStruct((N_TOK, LANES), F32),
                   jax.ShapeDtypeStruct((N_GATES, N_TOK), F32), jax.ShapeDtypeStruct((N_GATES, N_TOK), F32)),
        grid=(N_TOK // tm,),
        in_specs=_mod_specs(tm, 1, 0) + [pl.BlockSpec((D, LANES), lambda i: (0, 0)),
                                         pl.BlockSpec((N_GATES, D), lambda i: (0, 0)),
                                         pl.BlockSpec((1, LANES), lambda i: (0, 0)),
                                         pl.BlockSpec((N_GATES, 1), lambda i: (0, 0))],
        out_specs=(col, col, row, row),
        compiler_params=_params(1),
        name="mlstm_gates",
    )(x, sh, sc, wg_pad, wg.T, b_pad, b_gate.reshape(N_GATES, 1))


def _mlstm_body(*refs, has_init):
    (qf, kf, vf, licf, bcf, lirf, brf, qb, kb, vb, licb, bcb, lirb, brb) = refs[:14]
    refs = refs[14:]
    if has_init:
        c0_ref, n0_ref, m0_ref = refs[:3]
        refs = refs[3:]
    hf_ref, hb_ref, c_ref, n_ref, m_ref = refs

    @pl.when(pl.program_id(1) == 0)
    def _():
        if has_init:
            c_ref[...] = c0_ref[...]
            n_ref[...] = n0_ref[...]
            m_ref[...] = m0_ref[...]
        else:
            c_ref[...] = jnp.zeros_like(c_ref)
            n_ref[...] = jnp.zeros_like(n_ref)
            m_ref[...] = jnp.zeros_like(m_ref)

    t_idx = lax.broadcasted_iota(jnp.int32, (CHUNK, CHUNK), 0)
    s_idx = lax.broadcasted_iota(jnp.int32, (CHUNK, CHUNK), 1)
    nt = (((1,), (1,)), ((), ()))
    stores = []

    def chain(d, h, q_ref, k_ref, v_ref, lic_ref, bc_ref, lir_ref, br_ref, h_ref):
        mask = (s_idx <= t_idx) if d == 0 else (s_idx >= t_idx)
        hs = slice(h * ML_HEAD_DIM, (h + 1) * ML_HEAD_DIM)
        gi, gf = d * 2 * ML_HEADS + h, d * 2 * ML_HEADS + ML_HEADS + h
        q = q_ref[:, hs]
        k = k_ref[:, hs] * (ML_HEAD_DIM ** -0.5)
        v = v_ref[:, hs]
        qh, kh, vh = q.astype(BF16), k.astype(BF16), v.astype(BF16)
        li_c, b_c = lic_ref[:, gi:gi + 1], bc_ref[:, gf:gf + 1]
        li_r, b_r = lir_ref[gi:gi + 1, :], br_ref[gf:gf + 1, :]
        c_st = c_ref[0, d, h]
        n_st = n_ref[0, d, h:h + 1, :]
        m_st = m_ref[0, d, h:h + 1, :][:, 0:1]
        dmat = jnp.where(mask, b_c - b_r + li_r, -jnp.inf)
        inter = b_c + m_st
        m_out = jnp.maximum(inter, jnp.max(dmat, axis=-1, keepdims=True))
        p = jnp.exp(dmat - m_out)
        w_inter = jnp.exp(inter - m_out)
        yield
        qk = lax.dot_general(qh, kh, nt, preferred_element_type=F32)
        qc = jnp.dot(qh, c_st.astype(BF16), preferred_element_type=F32)
        yield
        s = qk * p
        den = (jnp.sum(s, axis=-1, keepdims=True)
               + w_inter * jnp.sum(q * n_st, axis=-1, keepdims=True))
        sh = s.astype(BF16)
        b_last = b_r[:, CHUNK - 1:CHUNK] if d == 0 else b_r[:, 0:1]
        g_r = b_last - b_r + li_r
        g_c = b_last - b_c + li_c
        m_new = jnp.maximum(b_last + m_st, jnp.max(g_r, axis=-1, keepdims=True))
        decay = jnp.exp(b_last + m_st - m_new)
        kw = k * jnp.exp(g_c - m_new)
        kwt = kw.T.astype(BF16)
        yield
        sv = jnp.dot(sh, vh, preferred_element_type=F32)
        kv = jnp.dot(kwt, vh, preferred_element_type=F32)
        yield
        h_out = (sv + w_inter * qc) / jnp.maximum(jnp.abs(den), jnp.exp(-m_out))
        c_new = decay * c_st + kv
        n_new = decay * n_st + jnp.sum(kw, axis=0, keepdims=True)
        stores.append((h_ref, d, h, hs, h_out, c_new, n_new, jnp.broadcast_to(m_new, (1, ML_HEAD_DIM))))
        yield

    chains = [chain(d, h, *group)
              for d, group in enumerate(((qf, kf, vf, licf, bcf, lirf, brf, hf_ref),
                                         (qb, kb, vb, licb, bcb, lirb, brb, hb_ref)))
              for h in range(ML_HEADS)]
    for _ in range(5):
        for ch in chains:
            next(ch)
    for h_ref, d, h, hs, h_out, c_new, n_new, m_new in stores:
        h_ref[:, hs] = h_out
        c_ref[0, d, h] = c_new
        n_ref[0, d, h:h + 1, :] = n_new
        m_ref[0, d, h:h + 1, :] = m_new


def _mlstm(proj, lic, bc, lir, br, *, row0, n_seq, seq_len, init=None):
    nc = seq_len // CHUNK
    base = row0 // CHUNK

    def fwd(b, j):
        return base + b * nc + j

    def bwd(b, j):
        return base + b * nc + (nc - 1 - j)

    def chunk_specs(pos):
        return ([pl.BlockSpec((CHUNK, ML_W), lambda b, j, c=c: (pos(b, j), c)) for c in range(3)]
                + [pl.BlockSpec((CHUNK, LANES), lambda b, j: (pos(b, j), 0))] * 2
                + [pl.BlockSpec((N_GATES, CHUNK), lambda b, j: (0, pos(b, j)))] * 2)

    st_c = pl.BlockSpec((1, 2, ML_HEADS, ML_HEAD_DIM, ML_HEAD_DIM), lambda b, j: (b, 0, 0, 0, 0))
    st_n = pl.BlockSpec((1, 2, ML_HEADS, ML_HEAD_DIM), lambda b, j: (b, 0, 0, 0))
    in_specs = chunk_specs(fwd) + chunk_specs(bwd)
    args = [proj, proj, proj, lic, bc, lir, br] * 2
    if init is not None:
        in_specs += [st_c, st_n, st_n]
        args += list(init)
    rows = n_seq * seq_len
    return pl.pallas_call(
        functools.partial(_mlstm_body, has_init=init is not None),
        out_shape=(jax.ShapeDtypeStruct((rows, ML_W), F32), jax.ShapeDtypeStruct((rows, ML_W), F32),
                   jax.ShapeDtypeStruct((n_seq, 2, ML_HEADS, ML_HEAD_DIM, ML_HEAD_DIM), F32),
                   jax.ShapeDtypeStruct((n_seq, 2, ML_HEADS, ML_HEAD_DIM), F32),
                   jax.ShapeDtypeStruct((n_seq, 2, ML_HEADS, ML_HEAD_DIM), F32)),
        grid=(n_seq, nc),
        in_specs=in_specs,
        out_specs=(pl.BlockSpec((CHUNK, ML_W), lambda b, j: (b * nc + j, 0)),
                   pl.BlockSpec((CHUNK, ML_W), lambda b, j: (b * nc + (nc - 1 - j), 0)),
                   st_c, st_n, st_n),
        compiler_params=_params(2),
        name="mlstm_scan",
    )(*args)


def _ml_post_body(hf_ref, hb_ref, o_ref, g_ref, out_ref):
    h = hf_ref[...] + hb_ref[...]
    gate = jax.nn.sigmoid(o_ref[...]) * g_ref[...]
    for hd in range(ML_HEADS):
        hs = slice(hd * ML_HEAD_DIM, (hd + 1) * ML_HEAD_DIM)
        x = h[:, hs]
        xc = x - jnp.mean(x, axis=-1, keepdims=True)
        var = jnp.mean(xc * xc, axis=-1, keepdims=True)
        out_ref[:, hs] = (gate[:, hs] * (xc * lax.rsqrt(var + RMS_EPS))).astype(BF16)


def _ml_post(hf, hb, proj, norm_g):
    tm = 512
    blk = pl.BlockSpec((tm, ML_W), lambda i: (i, 0))
    return pl.pallas_call(
        _ml_post_body,
        out_shape=jax.ShapeDtypeStruct((N_TOK, ML_W), BF16),
        grid=(N_TOK // tm,),
        in_specs=[blk, blk, pl.BlockSpec((tm, ML_W), lambda i: (i, 3)), pl.BlockSpec((1, ML_W), lambda i: (0, 0))],
        out_specs=blk,
        compiler_params=_params(1),
        name="mlstm_out_norm",
    )(hf, hb, proj, norm_g.reshape(1, ML_W))


def _head_rms(x, gain):
    lane_head = lax.broadcasted_iota(jnp.int32, x.shape, 1) // ATT_HEAD_DIM
    sq = x * x
    ms = jnp.zeros_like(x)
    for hd in range(x.shape[1] // ATT_HEAD_DIM):
        sel = lane_head == hd
        ms = jnp.where(sel, jnp.sum(jnp.where(sel, sq, 0.0), axis=-1, keepdims=True), ms)
    return x * lax.rsqrt(ms * (1.0 / ATT_HEAD_DIM) + RMS_EPS) * gain


def _rope(x, cos, sin_signed):
    w = x.shape[1]
    even = lax.broadcasted_iota(jnp.int32, x.shape, 1) % 2 == 0
    partner = jnp.where(even, pltpu.roll(x, w - 1, 1), pltpu.roll(x, 1, 1))
    return x * cos + partner * sin_signed


def _qk_prep_body(q_ref, k_ref, qg_ref, kg_ref, *refs, rope):
    if rope:
        cq_ref, sq_ref, ck_ref, sk_ref, qo_ref, kn_ref, kr_ref = refs
    else:
        qo_ref, kn_ref, kr_ref = refs
    q = _head_rms(q_ref[...], qg_ref[...])
    k = _head_rms(k_ref[...], kg_ref[...])
    kn_ref[...] = k
    if rope:
        q = _rope(q, cq_ref[...], sq_ref[...])
        k = _rope(k, ck_ref[...], sk_ref[...])
    qo_ref[...] = (q * (ATT_HEAD_DIM ** -0.5)).astype(BF16)
    kr_ref[...] = k.astype(BF16)


def _qk_prep(proj, q_gain, k_gain, *, row0, rows, rope_tabs=None):
    tm = 512
    r0 = row0 // tm
    in_specs = [pl.BlockSpec((tm, ATT_W), lambda i: (r0 + i, 4 * ML_W // ATT_W)),
                pl.BlockSpec((tm, KV_W), lambda i: (r0 + i, (4 * ML_W + ATT_W) // KV_W)),
                pl.BlockSpec((1, ATT_W), lambda i: (0, 0)),
                pl.BlockSpec((1, KV_W), lambda i: (0, 0))]
    args = [proj, proj, jnp.tile(q_gain, ATT_HEADS).reshape(1, ATT_W), jnp.tile(k_gain, ATT_KV_HEADS).reshape(1, KV_W)]
    if rope_tabs is not None:
        per_seq = DEC_SEQ // tm
        in_specs += [pl.BlockSpec((tm, ATT_W), lambda i: (i % per_seq, 0))] * 2
        in_specs += [pl.BlockSpec((tm, KV_W), lambda i: (i % per_seq, 0))] * 2
        args += list(rope_tabs)
    return pl.pallas_call(
        functools.partial(_qk_prep_body, rope=rope_tabs is not None),
        out_shape=(jax.ShapeDtypeStruct((rows, ATT_W), BF16), jax.ShapeDtypeStruct((rows, KV_W), F32),
                   jax.ShapeDtypeStruct((rows, KV_W), BF16)),
        grid=(rows // tm,),
        in_specs=in_specs,
        out_specs=(pl.BlockSpec((tm, ATT_W), lambda i: (i, 0)), pl.BlockSpec((tm, KV_W), lambda i: (i, 0)),
                   pl.BlockSpec((tm, KV_W), lambda i: (i, 0))),
        compiler_params=_params(1),
        name="attn_qk_prep",
    )(*args)


def _rope_tables():
    rows = DEC_SEQ // GRID_W
    axis_dim = ATT_HEAD_DIM // 2
    row = jnp.repeat(jnp.arange(rows, dtype=F32), GRID_W)
    col = (jnp.arange(DEC_SEQ) % GRID_W).astype(F32)
    inv = ROPE_BASE ** (-jnp.arange(axis_dim // 2, dtype=F32) * 2.0 / axis_dim)
    ang = jnp.concatenate([row[:, None] * inv, col[:, None] * inv], axis=-1)
    cos = jnp.repeat(jnp.cos(ang), 2, axis=-1)
    sin = jnp.repeat(jnp.sin(ang), 2, axis=-1) * jnp.tile(jnp.array([-1.0, 1.0], F32), axis_dim)
    return (jnp.tile(cos, (1, ATT_HEADS)), jnp.tile(sin, (1, ATT_HEADS)),
            jnp.tile(cos, (1, ATT_KV_HEADS)), jnp.tile(sin, (1, ATT_KV_HEADS)))


def _attn_body(q_ref, k_ref, v_ref, o_ref):
    q = q_ref[...]
    k = k_ref[0, 0]
    v = v_ref[0, 0]
    q_head = lax.broadcasted_iota(jnp.int32, q.shape, 1) // ATT_HEAD_DIM
    v_head = lax.broadcasted_iota(jnp.int32, v.shape, 1) // ATT_HEAD_DIM
    acc = jnp.zeros(q.shape, F32)
    for g in range(ATT_GROUP):
        qg = jnp.where(q_head == g, q, jnp.zeros_like(q))
        s = lax.dot_general(qg, k, (((1,), (1,)), ((), ())), preferred_element_type=F32)
        e = jnp.exp(s - jnp.max(s, axis=-1, keepdims=True))
        vg = jnp.where(v_head == g, v, jnp.zeros_like(v))
        o = jnp.dot(e.astype(BF16), vg, preferred_element_type=F32)
        acc = acc + o / jnp.sum(e, axis=-1, keepdims=True)
    o_ref[...] = acc.astype(BF16)


def _attention(q, k_rep, v_rep, *, n_seq, seq_len, tq):
    s_len = k_rep.shape[2]
    nq = seq_len // tq
    kv = pl.BlockSpec((1, 1, s_len, GROUP_W), lambda b, kh, i: (b, kh, 0, 0))
    qo = pl.BlockSpec((tq, GROUP_W), lambda b, kh, i: (b * nq + i, kh))
    return pl.pallas_call(
        _attn_body,
        out_shape=jax.ShapeDtypeStruct((n_seq * seq_len, ATT_W), BF16),
        grid=(n_seq, ATT_KV_HEADS, nq),
        in_specs=[qo, kv, kv],
        out_specs=qo,
        compiler_params=_params(3),
        name="attention",
    )(q, k_rep, v_rep)


def _rep_heads(x, n_seq):
    x = x.reshape(n_seq, -1, ATT_KV_HEADS, ATT_HEAD_DIM).transpose(0, 2, 1, 3)
    return jnp.tile(x, (1, 1, 1, ATT_GROUP))


def _hy_filter_body(feat_ref, t_ref, w1_ref, b1_ref, w2_ref, b2_ref, fr_ref, w3f_ref, w3b_ref, dl_ref,
                    hsum_ref, hdiff_ref, nyq_ref, z_ref):
    @pl.when(pl.program_id(0) == 0)
    def _():
        z = jnp.dot(feat_ref[...], w1_ref[...], preferred_element_type=F32, precision=HIGHEST) + b1_ref[...]
        z = jnp.sin(fr_ref[0:1, :] * z)
        z = jnp.dot(z, w2_ref[...], preferred_element_type=F32, precision=HIGHEST) + b2_ref[...]
        z_ref[...] = jnp.sin(fr_ref[1:2, :] * z)

    z = z_ref[...]
    window = jnp.exp(-t_ref[...] * dl_ref[...])
    h_f = jnp.dot(z, w3f_ref[...], preferred_element_type=F32, precision=HIGHEST) * window
    h_b = jnp.dot(z, w3b_ref[...], preferred_element_type=F32, precision=HIGHEST) * window
    row = lax.broadcasted_iota(jnp.int32, h_f.shape, 0)
    h_b = jnp.where(row == 0, 0.0, h_b)
    inv = 1.0 / (jnp.sum(jnp.abs(h_f), axis=0, keepdims=True) + jnp.sum(jnp.abs(h_b), axis=0, keepdims=True))
    h_sum = (h_f + h_b) * inv
    hsum_ref[...] = h_sum
    hdiff_ref[...] = (h_f - h_b) * inv
    nyq_ref[...] = jnp.sum(jnp.where(row % 2 == 0, h_sum, -h_sum), axis=0, keepdims=True)


def _hy_filter(seq_len, w1, b1, w2, b2, w3, sin_freq):
    tc = 256
    fw = w1.shape[1]
    t = jnp.arange(seq_len, dtype=F32)[:, None] / seq_len
    bands = jnp.arange(1, HY_BANDS + 1, dtype=F32)[None, :]
    feat = jnp.concatenate([t, jnp.sin(2.0 * math.pi * bands * t), jnp.cos(2.0 * math.pi * bands * t)], axis=-1)
    feat = jnp.pad(feat, ((0, 0), (0, LANES - HY_EMB)))
    deltas = jnp.abs(jnp.linspace(math.log(HY_TARGET) / HY_LONG_PCT, math.log(HY_TARGET) / HY_SHORT_PCT, D,
                                  dtype=F32)).reshape(1, D)
    pad_w = LANES - fw
    full = lambda shape: pl.BlockSpec(shape, lambda j: (0,) * len(shape))
    return pl.pallas_call(
        _hy_filter_body,
        out_shape=(jax.ShapeDtypeStruct((seq_len, D), F32), jax.ShapeDtypeStruct((seq_len, D), F32),
                   jax.ShapeDtypeStruct((1, D), F32)),
        grid=(D // tc,),
        in_specs=[full((seq_len, LANES)), full((seq_len, 1)), full((LANES, LANES)), full((1, LANES)),
                  full((LANES, LANES)), full((1, LANES)), full((2, LANES)),
                  pl.BlockSpec((LANES, tc), lambda j: (0, j)), pl.BlockSpec((LANES, tc), lambda j: (0, D // tc + j)),
                  pl.BlockSpec((1, tc), lambda j: (0, j))],
        out_specs=(pl.BlockSpec((seq_len, tc), lambda j: (0, j)), pl.BlockSpec((seq_len, tc), lambda j: (0, j)),
                   pl.BlockSpec((1, tc), lambda j: (0, j))),
        scratch_shapes=[pltpu.VMEM((seq_len, LANES), F32)],
        compiler_params=_params(1),
        name="hyena_filter",
    )(feat, t, jnp.pad(w1, ((0, LANES - HY_EMB), (0, pad_w))), jnp.pad(b1, (0, pad_w)).reshape(1, LANES),
      jnp.pad(w2, ((0, pad_w), (0, pad_w))), jnp.pad(b2, (0, pad_w)).reshape(1, LANES),
      jnp.pad(sin_freq, ((0, 0), (0, pad_w))), jnp.pad(w3, ((0, pad_w), (0, 0))), jnp.pad(w3, ((0, pad_w), (0, 0))),
      deltas)


def _dft_matrices(seq_len):
    n = 2 * seq_len
    k = lax.broadcasted_iota(jnp.int32, (seq_len, seq_len), 0)
    t = lax.broadcasted_iota(jnp.int32, (seq_len, seq_len), 1)
    ang = ((k * t) % n).astype(F32) * (2.0 * math.pi / n)
    cr, base = jnp.cos(ang), -jnp.sin(ang)
    ci = jnp.where(k == 0, (1 - 2 * (t % 2)).astype(F32), base)
    cit = jnp.where(t == 0, (1 - 2 * (k % 2)).astype(F32), base)
    return cr.astype(BF16), ci.astype(BF16), cit.astype(BF16)


def _dft_fwd_body(cr_ref, ci_ref, b1_ref, b2_ref, *refs, nk, tf, mult):
    if mult:
        kr_ref, ki_ref, or_ref, oi_ref, accr_ref, acci_ref = refs
    else:
        or_ref, oi_ref, accr_ref, acci_ref = refs
    kk = pl.program_id(3)
    pr = jnp.dot(cr_ref[...], b1_ref[0].astype(BF16), preferred_element_type=F32)
    pi = jnp.dot(ci_ref[...], b2_ref[0].astype(BF16), preferred_element_type=F32)

    @pl.when(kk == 0)
    def _():
        accr_ref[...] = pr
        acci_ref[...] = pi

    @pl.when(kk > 0)
    def _():
        accr_ref[...] += pr
        acci_ref[...] += pi

    @pl.when(kk == nk - 1)
    def _():
        zr, zi = accr_ref[...], acci_ref[...]
        if mult:
            kr, ki = kr_ref[...], ki_ref[...]
            first = (pl.program_id(1) * tf + lax.broadcasted_iota(jnp.int32, zr.shape, 0)) == 0
            or_ref[0] = jnp.where(first, 0.5 * zr * kr, zr * kr - zi * ki).astype(or_ref.dtype)
            oi_ref[0] = jnp.where(first, 0.5 * zi * ki, zr * ki + zi * kr).astype(oi_ref.dtype)
        else:
            or_ref[0] = zr
            oi_ref[0] = zi


def _dft_fwd(mats, b1, b2, *, row0, n_seq, seq_len, tf, tk, tn, filt=None):
    cr, ci, _ = mats
    nk = seq_len // tk
    r0 = row0 // tk
    a_spec = pl.BlockSpec((tf, tk), lambda b, f, c, kk: (f, kk))
    b_spec = pl.BlockSpec((1, tk, tn), lambda b, f, c, kk: (0, r0 + b * nk + kk, c))
    o_spec = pl.BlockSpec((1, tf, tn), lambda b, f, c, kk: (b, f, c))
    in_specs = [a_spec] * 2 + [b_spec] * 2
    args = [cr, ci, b1[None], b2[None]]
    if filt is not None:
        in_specs += [pl.BlockSpec((tf, tn), lambda b, f, c, kk: (f, c))] * 2
        args += list(filt)
    shape = jax.ShapeDtypeStruct((n_seq, seq_len, D), F32 if filt is None else BF16)
    return pl.pallas_call(
        functools.partial(_dft_fwd_body, nk=nk, tf=tf, mult=filt is not None),
        out_shape=(shape, shape),
        grid=(n_seq, seq_len // tf, D // tn, nk),
        in_specs=in_specs,
        out_specs=(o_spec, o_spec),
        scratch_shapes=[pltpu.VMEM((tf, tn), F32), pltpu.VMEM((tf, tn), F32)],
        compiler_params=_params(4),
        name="hyena_dft",
    )(*args)


def _dft_inv_body(cr_ref, ct_ref, yr_ref, yi_ref, z_ref, x0_ref, skip_ref, o_ref, acc_ref, *, nk, scale):
    kk = pl.program_id(3)
    p = (jnp.dot(cr_ref[...], yr_ref[0], preferred_element_type=F32)
         + jnp.dot(ct_ref[...], yi_ref[0], preferred_element_type=F32))

    @pl.when(kk == 0)
    def _():
        acc_ref[...] = p

    @pl.when(kk > 0)
    def _():
        acc_ref[...] += p

    @pl.when(kk == nk - 1)
    def _():
        z = z_ref[...]
        o_ref[...] = ((acc_ref[...] * scale + z * skip_ref[...]) * x0_ref[...]).astype(BF16)


def _dft_inv(mats, yr, yi, z, x0, skip, *, row0, n_seq, seq_len, tm, tk, tn):
    cr, _, cit = mats
    nk = seq_len // tk
    nt = seq_len // tm
    r0 = row0 // tm
    a_spec = pl.BlockSpec((tm, tk), lambda b, i, c, kk: (i, kk))
    y_spec = pl.BlockSpec((1, tk, tn), lambda b, i, c, kk: (b, kk, c))
    tok = pl.BlockSpec((tm, tn), lambda b, i, c, kk: (r0 + b * nt + i, c))
    return pl.pallas_call(
        functools.partial(_dft_inv_body, nk=nk, scale=1.0 / seq_len),
        out_shape=jax.ShapeDtypeStruct((n_seq * seq_len, D), BF16),
        grid=(n_seq, nt, D // tn, nk),
        in_specs=[a_spec] * 2 + [y_spec] * 2 + [tok, tok, pl.BlockSpec((1, tn), lambda b, i, c, kk: (0, c))],
        out_specs=pl.BlockSpec((tm, tn), lambda b, i, c, kk: (b * nt + i, c)),
        scratch_shapes=[pltpu.VMEM((tm, tn), F32)],
        compiler_params=_params(4),
        name="hyena_idft",
    )(cr, cit, yr, yi, z, x0, skip.reshape(1, D))


def _hy_gate_body(x0_ref, x1_ref, v_ref, w0_ref, w1_ref, wv_ref, b0_ref, b1_ref, bv_ref, z_ref, x0o_ref):
    def conv(u_ref, w_ref, b_ref):
        u = u_ref[...]
        n = u.shape[0]
        row = lax.broadcasted_iota(jnp.int32, u.shape, 0)
        prev = jnp.where(row == 0, 0.0, pltpu.roll(u, 1, 0))
        nxt = jnp.where(row == n - 1, 0.0, pltpu.roll(u, n - 1, 0))
        return prev * w_ref[0:1, :] + u * w_ref[1:2, :] + nxt * w_ref[2:3, :] + b_ref[...]

    x0o_ref[...] = conv(x0_ref, w0_ref, b0_ref)
    z_ref[...] = conv(v_ref, wv_ref, bv_ref) * conv(x1_ref, w1_ref, b1_ref)


def _hy_gate(u, conv_w, conv_b, *, row0, n_seq, seq_len):
    tc = 128
    nb = D // tc
    r0 = row0 // seq_len

    def col(part):
        return [pl.BlockSpec((seq_len, tc), lambda b, c: (r0 + b, part * nb + c)),
                pl.BlockSpec((3, tc), lambda b, c: (0, part * nb + c)),
                pl.BlockSpec((1, tc), lambda b, c: (0, part * nb + c))]

    specs = [col(p) for p in range(3)]
    out = pl.BlockSpec((seq_len, tc), lambda b, c: (b, c))
    shape = jax.ShapeDtypeStruct((n_seq * seq_len, D), F32)
    return pl.pallas_call(
        _hy_gate_body,
        out_shape=(shape, shape),
        grid=(n_seq, nb),
        in_specs=[s[0] for s in specs] + [s[1] for s in specs] + [s[2] for s in specs],
        out_specs=(out, out),
        compiler_params=_params(2),
        name="hyena_short_conv",
    )(u, u, u, conv_w, conv_w, conv_w, conv_b.reshape(1, 3 * D), conv_b.reshape(1, 3 * D), conv_b.reshape(1, 3 * D))


FFT_A, FFT_R = 64, 64
FFT_M = 2 * FFT_A
FFT_H = FFT_R // 2
assert FFT_A * FFT_R == DEC_SEQ


def _fft_consts():
    n = 2 * DEC_SEQ
    th = 2.0 * np.pi * (np.arange(FFT_M)[:, None] + 0.5) * np.arange(FFT_A)[None, :] / FFT_M
    f1 = np.concatenate([np.cos(th), -np.sin(th)], axis=0)
    k = np.arange(FFT_M)[:, None, None] + FFT_M * np.arange(FFT_H)[None, :, None] + 0.5
    ph = 2.0 * np.pi * k * np.arange(FFT_R)[None, None, :] / n
    c, s = np.cos(ph), np.sin(ph)
    g = np.concatenate([np.concatenate([c, s], axis=2), np.concatenate([-s, c], axis=2)], axis=1)
    as_bf16 = lambda m: jnp.asarray(m, dtype=F32).astype(BF16)
    return as_bf16(f1), as_bf16(f1.T), as_bf16(g), as_bf16(g.transpose(0, 2, 1))


def _fft_stage1(src_ref, y_ref, f1):
    for b in range(FFT_R):
        zb = src_ref[pl.ds(b, FFT_A, stride=FFT_R), :].astype(BF16)
        y_ref[b * 2 * FFT_M:(b + 1) * 2 * FFT_M, :] = jnp.dot(f1, zb, preferred_element_type=F32)


def _fft_stage2(y_ref, g_ref, k1):
    yr = y_ref[pl.ds(k1, FFT_R, stride=2 * FFT_M), :]
    yi = y_ref[pl.ds(FFT_M + k1, FFT_R, stride=2 * FFT_M), :]
    z = jnp.dot(g_ref[k1], jnp.concatenate([yr, yi], axis=0).astype(BF16), preferred_element_type=F32)
    return z[:FFT_H], z[FFT_H:]


def _hy_spectrum_body(hs_ref, hd_ref, f1_ref, g_ref, kr_ref, ki_ref, y_ref):
    f1 = f1_ref[...]
    _fft_stage1(hs_ref, y_ref, f1)
    for k1 in range(FFT_M):
        kr_ref[k1 * FFT_H:(k1 + 1) * FFT_H, :] = _fft_stage2(y_ref, g_ref, k1)[0]
    _fft_stage1(hd_ref, y_ref, f1)
    for k1 in range(FFT_M):
        ki_ref[k1 * FFT_H:(k1 + 1) * FFT_H, :] = _fft_stage2(y_ref, g_ref, k1)[1]


def _hy_spectrum(h_sum, h_diff, consts):
    tc = 128
    f1, _, g, _ = consts
    blk = pl.BlockSpec((DEC_SEQ, tc), lambda c: (0, c))
    shape = jax.ShapeDtypeStruct((DEC_SEQ, D), F32)
    return pl.pallas_call(
        _hy_spectrum_body,
        out_shape=(shape, shape),
        grid=(D // tc,),
        in_specs=[blk, blk, pl.BlockSpec(f1.shape, lambda c: (0, 0)), pl.BlockSpec(g.shape, lambda c: (0, 0, 0))],
        out_specs=(blk, blk),
        scratch_shapes=[pltpu.VMEM((FFT_R * 2 * FFT_M, tc), F32)],
        compiler_params=_params(1),
        name="hyena_filter_fft",
    )(h_sum, h_diff, f1, g)


def _hy_conv_fft_body(x0_ref, x1_ref, v_ref, w0_ref, w1_ref, wv_ref, b0_ref, b1_ref, bv_ref, kr_ref, ki_ref,
                      skip_ref, f1_ref, f1t_ref, g_ref, gt_ref, o_ref, z_ref, y_ref, t_ref):
    rows = 512

    def conv(u_ref, w_ref, b_ref, r):
        u = u_ref[r:r + rows, :]
        row = lax.broadcasted_iota(jnp.int32, u.shape, 0)
        before = u_ref[r - 1:r, :] if r > 0 else jnp.zeros_like(u[0:1])
        after = u_ref[r + rows:r + rows + 1, :] if r + rows < DEC_SEQ else jnp.zeros_like(u[0:1])
        prev = jnp.where(row == 0, before, pltpu.roll(u, 1, 0))
        nxt = jnp.where(row == rows - 1, after, pltpu.roll(u, rows - 1, 0))
        return prev * w_ref[0:1, :] + u * w_ref[1:2, :] + nxt * w_ref[2:3, :] + b_ref[...]

    for r in range(0, DEC_SEQ, rows):
        z_ref[r:r + rows, :] = conv(v_ref, wv_ref, bv_ref, r) * conv(x1_ref, w1_ref, b1_ref, r)
    _fft_stage1(z_ref, y_ref, f1_ref[...])
    for k1 in range(FFT_M):
        zr, zi = _fft_stage2(y_ref, g_ref, k1)
        kr = kr_ref[k1 * FFT_H:(k1 + 1) * FFT_H, :]
        ki = ki_ref[k1 * FFT_H:(k1 + 1) * FFT_H, :]
        p = jnp.concatenate([zr * kr - zi * ki, zr * ki + zi * kr], axis=0).astype(BF16)
        u = jnp.dot(gt_ref[k1], p, preferred_element_type=F32)
        y_ref[pl.ds(k1, FFT_R, stride=2 * FFT_M), :] = u[:FFT_R]
        y_ref[pl.ds(FFT_M + k1, FFT_R, stride=2 * FFT_M), :] = u[FFT_R:]
    f1t = f1t_ref[...]
    for b in range(FFT_R):
        yb = jnp.dot(f1t, y_ref[b * 2 * FFT_M:(b + 1) * 2 * FFT_M, :].astype(BF16), preferred_element_type=F32)
        t_ref[pl.ds(b, FFT_A, stride=FFT_R), :] = yb
    for r in range(0, DEC_SEQ, rows):
        y = t_ref[r:r + rows, :] * (1.0 / DEC_SEQ) + z_ref[r:r + rows, :] * skip_ref[...]
        o_ref[r:r + rows, :] = (y * conv(x0_ref, w0_ref, b0_ref, r)).astype(BF16)


def _hy_conv_fft(u, conv_w, conv_b, kr, ki, skip, consts, *, row0, n_seq):
    tc = 128
    nb = D // tc
    r0 = row0 // DEC_SEQ
    f1, f1t, g, gt = consts

    def col(part):
        return [pl.BlockSpec((DEC_SEQ, tc), lambda b, c: (r0 + b, part * nb + c)),
                pl.BlockSpec((3, tc), lambda b, c: (0, part * nb + c)),
                pl.BlockSpec((1, tc), lambda b, c: (0, part * nb + c))]

    specs = [col(p) for p in range(3)]
    chan = pl.BlockSpec((DEC_SEQ, tc), lambda b, c: (0, c))
    const = lambda m: pl.BlockSpec(m.shape, lambda b, c: (0,) * m.ndim)
    cb = conv_b.reshape(1, 3 * D)
    return pl.pallas_call(
        _hy_conv_fft_body,
        out_shape=jax.ShapeDtypeStruct((n_seq * DEC_SEQ, D), BF16),
        grid=(n_seq, nb),
        in_specs=([s[0] for s in specs] + [s[1] for s in specs] + [s[2] for s in specs]
                  + [chan, chan, pl.BlockSpec((1, tc), lambda b, c: (0, c))] + [const(m) for m in consts]),
        out_specs=pl.BlockSpec((DEC_SEQ, tc), lambda b, c: (b, c)),
        scratch_shapes=[pltpu.VMEM((DEC_SEQ, tc), F32), pltpu.VMEM((FFT_R * 2 * FFT_M, tc), F32),
                        pltpu.VMEM((DEC_SEQ, tc), F32)],
        compiler_params=_params(2, 56 * 1024 * 1024),
        name="hyena_conv_fft",
    )(u, u, u, conv_w, conv_w, conv_w, cb, cb, cb, kr, ki, skip.reshape(1, D), f1, f1t, g, gt)


ROW_TILE = D // LANES
ROUTER_TM = 512
EXPERT_TM = 512
N_SLOTS = 2 * N_TOK + N_EXPERTS * EXPERT_TM
N_SLOT_TILES = N_SLOTS // EXPERT_TM
INFO_E1, INFO_E2, INFO_R1, INFO_R2, INFO_W1, INFO_W2 = range(6)


def _to_row_tiles(ref, x):
    rows = x.shape[0]
    for j in range(ROW_TILE):
        ref[pl.ds(j, rows, stride=ROW_TILE), :] = x[:, j * LANES:(j + 1) * LANES]


def _from_row_tiles(ref, rows):
    return jnp.concatenate([ref[pl.ds(j, rows, stride=ROW_TILE), :] for j in range(ROW_TILE)], axis=-1)


def _router_body(x_ref, sh_ref, sc_ref, w_ref, info_ref, incl_ref, cnt_ref):
    @pl.when(pl.program_id(0) == 0)
    def _():
        cnt_ref[...] = jnp.zeros_like(cnt_ref)

    h = _modulate(x_ref, sh_ref, sc_ref)
    logits = jnp.dot(h, w_ref[...], preferred_element_type=F32, precision=HIGHEST)
    lane = lax.broadcasted_iota(jnp.int32, logits.shape, 1).astype(F32)
    logits = jnp.where(lane < N_EXPERTS, logits, -jnp.inf)
    e = jnp.exp(logits - jnp.max(logits, axis=-1, keepdims=True))
    p = e / jnp.sum(e, axis=-1, keepdims=True)
    p1 = jnp.max(p, axis=-1, keepdims=True)
    i1 = jnp.min(jnp.where(p == p1, lane, float(LANES)), axis=-1, keepdims=True)
    rest = jnp.where(lane == i1, -1.0, p)
    p2 = jnp.max(rest, axis=-1, keepdims=True)
    i2 = jnp.min(jnp.where(rest == p2, lane, float(LANES)), axis=-1, keepdims=True)
    total = p1 + p2
    chosen = jnp.where((lane == i1) | (lane == i2), 1.0, 0.0)
    tm = chosen.shape[0]
    earlier = (lax.broadcasted_iota(jnp.int32, (tm, tm), 1) < lax.broadcasted_iota(jnp.int32, (tm, tm), 0))
    rank = jnp.dot(earlier.astype(BF16), chosen.astype(BF16), preferred_element_type=F32) + cnt_ref[...]
    r1 = jnp.sum(jnp.where(lane == i1, rank, 0.0), axis=-1, keepdims=True)
    r2 = jnp.sum(jnp.where(lane == i2, rank, 0.0), axis=-1, keepdims=True)
    cnt_ref[...] += jnp.sum(chosen, axis=0, keepdims=True)
    incl_ref[0] = jnp.broadcast_to(cnt_ref[...], incl_ref.shape[1:])
    info = jnp.zeros_like(p)
    for col, val in ((INFO_E1, i1), (INFO_E2, i2), (INFO_R1, r1), (INFO_R2, r2),
                     (INFO_W1, p1 / total), (INFO_W2, p2 / total)):
        info = jnp.where(lane == col, val, info)
    info_ref[...] = info


def _router(x, sh, sc, w_router):
    tm = ROUTER_TM
    return pl.pallas_call(
        _router_body,
        out_shape=(jax.ShapeDtypeStruct((N_TOK, LANES), F32),
                   jax.ShapeDtypeStruct((N_TOK // tm, 8, LANES), F32)),
        grid=(N_TOK // tm,),
        in_specs=_mod_specs(tm, 1, 0) + [pl.BlockSpec((D, LANES), lambda i: (0, 0))],
        out_specs=(pl.BlockSpec((tm, LANES), lambda i: (i, 0)), pl.BlockSpec((1, 8, LANES), lambda i: (i, 0, 0))),
        scratch_shapes=[pltpu.VMEM((1, LANES), F32)],
        compiler_params=_params(1),
        name="moe_router",
    )(x, sh, sc, jnp.pad(w_router, ((0, 0), (0, LANES - N_EXPERTS))))


def _row_tile(ref, row):
    return ref.at[pl.ds(pl.multiple_of(row * ROW_TILE, ROW_TILE), ROW_TILE)]


def _dispatch_body(pos1_ref, pos2_ref, x_ref, zeros_hbm, xs_hbm, rows_ref, sem, *, tm):
    del zeros_hbm
    base = pl.program_id(0) * tm
    _to_row_tiles(rows_ref, x_ref[...])

    def copies(r):
        src = _row_tile(rows_ref, r)
        return (pltpu.make_async_copy(src, _row_tile(xs_hbm, pos1_ref[base + r]), sem),
                pltpu.make_async_copy(src, _row_tile(xs_hbm, pos2_ref[base + r]), sem))

    def issue(r, carry):
        for queue, cp in enumerate(copies(r)):
            cp.start(priority=queue)
        return carry

    def drain(r, carry):
        for cp in copies(r):
            cp.wait()
        return carry

    lax.fori_loop(0, tm, issue, 0)
    lax.fori_loop(0, tm, drain, 0)


def _dispatch(pos1, pos2, x):
    tm = 512
    return pl.pallas_call(
        functools.partial(_dispatch_body, tm=tm),
        out_shape=jax.ShapeDtypeStruct((N_SLOTS * ROW_TILE, LANES), F32),
        grid_spec=pltpu.PrefetchScalarGridSpec(
            num_scalar_prefetch=2, grid=(N_TOK // tm,),
            in_specs=[pl.BlockSpec((tm, D), lambda i, *_: (i, 0)), pl.BlockSpec(memory_space=pl.ANY)],
            out_specs=pl.BlockSpec(memory_space=pl.ANY),
            scratch_shapes=[pltpu.VMEM((tm * ROW_TILE, LANES), F32), pltpu.SemaphoreType.DMA(())]),
        input_output_aliases={3: 0},
        compiler_params=_params(1),
        name="moe_dispatch",
    )(pos1, pos2, x, jnp.zeros((N_SLOTS * ROW_TILE, LANES), F32))


def _expert_swiglu_body(eid_ref, b1_ref, b2_ref, nv_ref, xs_ref, sh_ref, sc_ref, wg_ref, wu_ref, o_ref, *, fc):
    t = pl.program_id(0)

    @pl.when(t < nv_ref[0])
    def _():
        tm = o_ref.shape[0]
        x = _from_row_tiles(xs_ref, tm)
        slot = t * tm + lax.broadcasted_iota(jnp.int32, (tm, 1), 0)
        in1, in2 = slot >= b1_ref[t], slot >= b2_ref[t]
        sc = jnp.where(in2, sc_ref[2], jnp.where(in1, sc_ref[1], sc_ref[0]))
        sh = jnp.where(in2, sh_ref[2], jnp.where(in1, sh_ref[1], sh_ref[0]))
        h = (x * (1.0 + sc) + sh).astype(BF16)
        for c in range(o_ref.shape[1] // fc):
            cs = slice(c * fc, (c + 1) * fc)
            g = jnp.dot(h, wg_ref[0, :, cs], preferred_element_type=F32)
            u = jnp.dot(h, wu_ref[0, :, cs], preferred_element_type=F32)
            o_ref[:, cs] = (g * jax.nn.sigmoid(g) * u).astype(BF16)

    @pl.when(t >= nv_ref[0])
    def _():
        o_ref[...] = jnp.zeros_like(o_ref)


def _expert_swiglu(meta, xs_rt, sh, sc, w_gate, w_up):
    eid, b1, b2, nv = meta
    tm, f = EXPERT_TM, w_gate.shape[2]

    def tile(t, eid, b1, b2, nv):
        return jnp.minimum(t, nv[0] - 1)

    w_spec = pl.BlockSpec((1, D, f), lambda t, eid, b1, b2, nv: (eid[tile(t, eid, b1, b2, nv)], 0, 0))
    mod = pl.BlockSpec((8, 1, D), lambda t, *_: (0, 0, 0))
    return pl.pallas_call(
        functools.partial(_expert_swiglu_body, fc=512),
        out_shape=jax.ShapeDtypeStruct((N_SLOTS, f), BF16),
        grid_spec=pltpu.PrefetchScalarGridSpec(
            num_scalar_prefetch=4, grid=(N_SLOT_TILES,),
            in_specs=[pl.BlockSpec((tm * ROW_TILE, LANES), lambda t, *m: (tile(t, *m), 0)), mod, mod, w_spec, w_spec],
            out_specs=pl.BlockSpec((tm, f), lambda t, *m: (t, 0))),
        compiler_params=_params(1, 56 * 1024 * 1024),
        name="moe_swiglu",
    )(eid, b1, b2, nv, xs_rt, sh, sc, w_gate, w_up)


def _expert_down_body(eid_ref, nv_ref, a_ref, w_ref, y_ref):
    @pl.when(pl.program_id(0) < nv_ref[0])
    def _():
        _to_row_tiles(y_ref, jnp.dot(a_ref[...], w_ref[0], preferred_element_type=F32))

    @pl.when(pl.program_id(0) >= nv_ref[0])
    def _():
        y_ref[...] = jnp.zeros_like(y_ref)


def _expert_down(meta, act, w_down):
    eid, _, _, nv = meta
    tm, f = EXPERT_TM, act.shape[1]

    def tile(t, eid, nv):
        return jnp.minimum(t, nv[0] - 1)

    return pl.pallas_call(
        _expert_down_body,
        out_shape=jax.ShapeDtypeStruct((N_SLOTS * ROW_TILE, LANES), F32),
        grid_spec=pltpu.PrefetchScalarGridSpec(
            num_scalar_prefetch=2, grid=(N_SLOT_TILES,),
            in_specs=[pl.BlockSpec((tm, f), lambda t, *m: (tile(t, *m), 0)),
                      pl.BlockSpec((1, f, D), lambda t, eid, nv: (eid[tile(t, eid, nv)], 0, 0))],
            out_specs=pl.BlockSpec((tm * ROW_TILE, LANES), lambda t, *m: (t, 0))),
        compiler_params=_params(1),
        name="moe_down",
    )(eid, nv, act, w_down)


def _combine_body(pos1_ref, pos2_ref, y_hbm, info_ref, x_ref, g_ref, lng_ref, lnb_ref, o_ref, y1_ref, y2_ref, sem,
                  *, tm):
    base = pl.program_id(0) * tm

    def copies(r):
        return (pltpu.make_async_copy(_row_tile(y_hbm, pos1_ref[base + r]), _row_tile(y1_ref, r), sem),
                pltpu.make_async_copy(_row_tile(y_hbm, pos2_ref[base + r]), _row_tile(y2_ref, r), sem))

    def issue(r, carry):
        for queue, cp in enumerate(copies(r)):
            cp.start(priority=queue)
        return carry

    def drain(r, carry):
        for cp in copies(r):
            cp.wait()
        return carry

    lax.fori_loop(0, tm, issue, 0)
    lax.fori_loop(0, tm, drain, 0)
    info = info_ref[...]
    ffn = (info[:, INFO_W1:INFO_W1 + 1] * _from_row_tiles(y1_ref, tm)
           + info[:, INFO_W2:INFO_W2 + 1] * _from_row_tiles(y2_ref, tm))
    y = ALPHA * x_ref[...] + g_ref[0] * ffn
    mu = jnp.mean(y, axis=-1, keepdims=True)
    yc = y - mu
    var = jnp.mean(yc * yc, axis=-1, keepdims=True)
    o_ref[...] = yc * lax.rsqrt(var + LN_EPS) * lng_ref[...] + lnb_ref[...]


def _combine(pos1, pos2, y_rt, info, x, gate, ln_g, ln_b):
    tm = 512
    return pl.pallas_call(
        functools.partial(_combine_body, tm=tm),
        out_shape=jax.ShapeDtypeStruct((N_TOK, D), F32),
        grid_spec=pltpu.PrefetchScalarGridSpec(
            num_scalar_prefetch=2, grid=(N_TOK // tm,),
            in_specs=[pl.BlockSpec(memory_space=pl.ANY),
                      pl.BlockSpec((tm, LANES), lambda i, *_: (i, 0)),
                      pl.BlockSpec((tm, D), lambda i, *_: (i, 0)),
                      pl.BlockSpec((1, 1, D), lambda i, *_: (_group_of_row(i * tm), 0, 0)),
                      pl.BlockSpec((1, D), lambda i, *_: (0, 0)),
                      pl.BlockSpec((1, D), lambda i, *_: (0, 0))],
            out_specs=pl.BlockSpec((tm, D), lambda i, *_: (i, 0)),
            scratch_shapes=[pltpu.VMEM((tm * ROW_TILE, LANES), F32), pltpu.VMEM((tm * ROW_TILE, LANES), F32),
                            pltpu.SemaphoreType.DMA(())]),
        compiler_params=_params(1),
        name="moe_combine",
    )(pos1, pos2, y_rt, info, x, gate, ln_g.reshape(1, D), ln_b.reshape(1, D))


def _slot_plan(info, incl):
    row = lambda n_rows: incl[n_rows // ROUTER_TM - 1, 0, :N_EXPERTS].astype(jnp.int32)
    count = row(N_TOK)
    padded = (count + EXPERT_TM - 1) // EXPERT_TM * EXPERT_TM
    end = jnp.cumsum(padded)
    start = end - padded
    tile_row = jnp.arange(N_SLOT_TILES, dtype=jnp.int32) * EXPERT_TM
    eid = jnp.minimum(jnp.sum(tile_row[:, None] >= end[None, :], axis=1), N_EXPERTS - 1).astype(jnp.int32)
    b1 = (start + row(N_CTX))[eid]
    b2 = (start + row(N_CTX + DEC_SEQ))[eid]
    nv = (end[-1:] // EXPERT_TM).astype(jnp.int32)
    e1, e2 = info[:, INFO_E1].astype(jnp.int32), info[:, INFO_E2].astype(jnp.int32)
    pos1 = start[e1] + info[:, INFO_R1].astype(jnp.int32)
    pos2 = start[e2] + info[:, INFO_R2].astype(jnp.int32)
    return pos1, pos2, (eid, b1, b2, nv)


def _even_mixer(x, sh, sc, gate, ln_g, ln_b, w_in, b_igate, b_fgate, ml_norm_g, q_norm_g, k_norm_g, w_out,
                st_c, st_n, st_m, cache_k, cache_v, rope_tabs):
    splits = (4 * ML_W, 4 * ML_W + N_GATES)
    w_main = jnp.concatenate([w_in[:, :splits[0]], w_in[:, splits[1]:]], axis=1).astype(BF16)
    proj = _mod_matmul(x, sh, sc, [w_main[None]], tm=1024, tn=MAIN_W // 2, out_dtype=F32, name="even_in_proj")[0]
    b_gate = jnp.stack([b_igate, b_fgate], axis=1).reshape(N_GATES)
    lic, bc, lir, br = _gates(x, sh, sc, w_in[:, splits[0]:splits[1]], b_gate)

    hf_c, hb_c, new_c, new_n, new_m = _mlstm(proj, lic, bc, lir, br, row0=0, n_seq=BATCH, seq_len=SEQ)
    init = (st_c, st_n, jnp.broadcast_to(st_m[..., None], st_n.shape))
    hf_s, hb_s, _, _, _ = _mlstm(proj, lic, bc, lir, br, row0=N_CTX, n_seq=DEC_BATCH, seq_len=DEC_SEQ, init=init)
    ml = _ml_post(jnp.concatenate([hf_c, hf_s]), jnp.concatenate([hb_c, hb_s]), proj, ml_norm_g)

    q_c, kn_c, kb_c = _qk_prep(proj, q_norm_g, k_norm_g, row0=0, rows=N_CTX)
    q_s, _, kb_s = _qk_prep(proj, q_norm_g, k_norm_g, row0=N_CTX, rows=N_LAT, rope_tabs=rope_tabs)
    v_all = proj[:, MAIN_W - KV_W:]
    v_c, v_s = v_all[:N_CTX], v_all[N_CTX:]
    att_c = _attention(q_c, _rep_heads(kb_c, BATCH), _rep_heads(v_c.astype(BF16), BATCH),
                       n_seq=BATCH, seq_len=SEQ, tq=SEQ)
    k_lat = jnp.concatenate([kb_s.reshape(DEC_BATCH, DEC_SEQ, KV_W),
                             cache_k.reshape(DEC_BATCH, PAST_LEN, KV_W).astype(BF16)], axis=1)
    v_lat = jnp.concatenate([v_s.reshape(DEC_BATCH, DEC_SEQ, KV_W).astype(BF16),
                             cache_v.reshape(DEC_BATCH, PAST_LEN, KV_W).astype(BF16)], axis=1)
    att_s = _attention(q_s, _rep_heads(k_lat.reshape(-1, KV_W), DEC_BATCH), _rep_heads(v_lat.reshape(-1, KV_W), DEC_BATCH),
                       n_seq=DEC_BATCH, seq_len=DEC_SEQ, tq=256)

    mixed = jnp.concatenate([ml, jnp.concatenate([att_c, att_s])], axis=1)
    x = _matmul_res_ln(mixed[None], w_out.astype(BF16)[None], x, gate, ln_g, ln_b, tm=512, name="even_out_proj")
    new_k = kn_c.reshape(BATCH, SEQ, ATT_KV_HEADS, ATT_HEAD_DIM)
    new_v = v_c.reshape(BATCH, SEQ, ATT_KV_HEADS, ATT_HEAD_DIM)
    return x, new_k, new_v, new_c, new_n, new_m[..., 0]


def _hyena_mixer(x, sh, sc, gate, ln_g, ln_b, w_in, conv_w, conv_b, w1, b1, w2, b2, w3, sin_freq, skip, w_out, dft):
    u = _mod_matmul(x, sh, sc, [w_in.astype(BF16)[None]], tm=1024, tn=512, out_dtype=F32, name="hyena_in_proj")[0]
    mats, consts = dft
    tiles = dict(tf=SEQ, tk=SEQ, tn=D)
    h_sum, h_diff, nyq = _hy_filter(SEQ, w1, b1, w2, b2, w3, sin_freq)
    kr, ki = _dft_fwd(mats, h_sum, h_diff, row0=0, n_seq=1, seq_len=SEQ, **tiles)
    kr, ki = kr[0], ki[0].at[0].set(nyq[0])
    z, x0 = _hy_gate(u, conv_w, conv_b, row0=0, n_seq=BATCH, seq_len=SEQ)
    yr, yi = _dft_fwd(mats, z, z, row0=0, n_seq=BATCH, seq_len=SEQ, filt=(kr, ki), **tiles)
    y_c = _dft_inv(mats, yr, yi, z, x0, skip, row0=0, n_seq=BATCH, seq_len=SEQ, tm=SEQ, tk=SEQ, tn=D)
    h_sum, h_diff, _ = _hy_filter(DEC_SEQ, w1, b1, w2, b2, w3, sin_freq)
    kr, ki = _hy_spectrum(h_sum, h_diff, consts)
    y_s = _hy_conv_fft(u, conv_w, conv_b, kr, ki, skip, consts, row0=N_CTX, n_seq=DEC_BATCH)
    y = jnp.concatenate([y_c, y_s])
    return _matmul_res_ln(y[None], w_out.astype(BF16)[None], x, gate, ln_g, ln_b, tm=512, name="hyena_out_proj")


def _dense_ffn(x, sh, sc, gate, ln_g, ln_b, w_gate, w_up, w_down):
    act = _mod_matmul(x, sh, sc, [w_gate.astype(BF16)[None], w_up.astype(BF16)[None]],
                      tm=1024, tn=D_FF // 2, out_dtype=BF16, name="ffn_swiglu")
    return _matmul_res_ln(act, w_down.astype(BF16)[None], x, gate, ln_g, ln_b, tm=512, name="ffn_down")


def _moe_ffn(x, sh, sc, gate, ln_g, ln_b, w_router, w_gate, w_up, w_down):
    info, incl = _router(x, sh, sc, w_router)
    pos1, pos2, meta = _slot_plan(info, incl)
    xs_rt = _dispatch(pos1, pos2, x)
    act = _expert_swiglu(meta, xs_rt, sh, sc, w_gate.astype(BF16), w_up.astype(BF16))
    y_rt = _expert_down(meta, act, w_down.astype(BF16))
    return _combine(pos1, pos2, y_rt, info, x, gate, ln_g, ln_b)


def kernel(x_prompt, x_sample, cache_attn_k, cache_attn_v, state_mlstm_C, state_mlstm_n, state_mlstm_m, c, c_ctx, w_ada, b_ada, ln_g, ln_b, w_in_even, b_igate, b_fgate, ml_norm_g, q_norm_g, k_norm_g, w_out_even, w_ffn_gate, w_ffn_up, w_ffn_down, w_in_hy, hy_conv_w, hy_conv_b, hy_filt_w1, hy_filt_b1, hy_filt_w2, hy_filt_b2, hy_filt_w3, hy_sin_freq, hy_skip, w_out_hy, w_router, w_moe_gate, w_moe_up, w_moe_down):
    x = jnp.concatenate([x_prompt.reshape(N_CTX, D), x_sample.reshape(N_LAT, D)])
    cvec = jnp.concatenate([c_ctx[None], c, jnp.zeros((8 - 1 - DEC_BATCH, D), F32)])
    mods = _ada(cvec, w_ada, b_ada)
    rope_tabs = _rope_tables()
    dft = (_dft_matrices(SEQ), _fft_consts())
    new_k, new_v, new_c, new_n, new_m = [], [], [], [], []
    for layer in range(DEPTH):
        sh1, sc1, g1, sh2, sc2, g2 = (mods[layer, :, i * D:(i + 1) * D].reshape(8, 1, D) for i in range(6))
        i = layer // 2
        if layer % 2 == 0:
            x, k_c, v_c, st_c, st_n, st_m = _even_mixer(
                x, sh1, sc1, g1, ln_g[layer, 0], ln_b[layer, 0], w_in_even[i], b_igate[i], b_fgate[i], ml_norm_g[i],
                q_norm_g[i], k_norm_g[i], w_out_even[i], state_mlstm_C[:, i], state_mlstm_n[:, i], state_mlstm_m[:, i],
                cache_attn_k[:, i], cache_attn_v[:, i], rope_tabs)
            new_k.append(k_c)
            new_v.append(v_c)
            new_c.append(st_c)
            new_n.append(st_n)
            new_m.append(st_m)
            x = _dense_ffn(x, sh2, sc2, g2, ln_g[layer, 1], ln_b[layer, 1], w_ffn_gate[i], w_ffn_up[i], w_ffn_down[i])
        else:
            x = _hyena_mixer(x, sh1, sc1, g1, ln_g[layer, 0], ln_b[layer, 0], w_in_hy[i], hy_conv_w[i], hy_conv_b[i],
                             hy_filt_w1[i], hy_filt_b1[i], hy_filt_w2[i], hy_filt_b2[i], hy_filt_w3[i], hy_sin_freq[i],
                             hy_skip[i], w_out_hy[i], dft)
            x = _moe_ffn(x, sh2, sc2, g2, ln_g[layer, 1], ln_b[layer, 1], w_router[i], w_moe_gate[i], w_moe_up[i],
                         w_moe_down[i])
    return (x[:N_CTX].reshape(BATCH, SEQ, D), x[N_CTX:].reshape(DEC_BATCH, DEC_SEQ, D),
            jnp.stack(new_k, axis=1), jnp.stack(new_v, axis=1), jnp.stack(new_c, axis=1),
            jnp.stack(new_n, axis=1), jnp.stack(new_m, axis=1))
```

```python
import functools
import math

import jax
import jax.numpy as jnp
import numpy as np
from jax import lax
from jax.experimental import pallas as pl
from jax.experimental.pallas import tpu as pltpu

F32 = jnp.float32
BF16 = jnp.bfloat16
HIGHEST = lax.Precision.HIGHEST

D = 1024
BATCH, SEQ = 32, 256
DEC_BATCH, DEC_SEQ = 2, 4096
DEPTH = 4
PAST_LEN = 256
GRID_W = 64
N_CTX = BATCH * SEQ
N_LAT = DEC_BATCH * DEC_SEQ
N_TOK = N_CTX + N_LAT

ML_HEADS, ML_HEAD_DIM = 4, 128
ML_W = ML_HEADS * ML_HEAD_DIM
CHUNK = 128
ATT_HEADS, ATT_KV_HEADS, ATT_HEAD_DIM = 8, 2, 64
ATT_GROUP = ATT_HEADS // ATT_KV_HEADS
ATT_W = ATT_HEADS * ATT_HEAD_DIM
KV_W = ATT_KV_HEADS * ATT_HEAD_DIM
GROUP_W = ATT_GROUP * ATT_HEAD_DIM
ROPE_BASE = 10000.0
N_GATES = 4 * ML_HEADS
MAIN_W = 4 * ML_W + ATT_W + 2 * KV_W

HY_EMB = 33
HY_BANDS = (HY_EMB - 1) // 2
HY_TARGET, HY_SHORT_PCT, HY_LONG_PCT = 1e-2, 0.3, 1.5
D_FF = 2816
N_EXPERTS = 8
MOE_D_FF = 3584
ALPHA = (2 * DEPTH) ** 0.25
LN_EPS = 1e-5
RMS_EPS = 1e-6

LANES = 128
VMEM_LIMIT = 48 * 1024 * 1024


def _params(n_axes, vmem=VMEM_LIMIT):
    return pltpu.CompilerParams(dimension_semantics=("arbitrary",) * n_axes, vmem_limit_bytes=vmem)


def _group_of_row(r):
    return jnp.where(r < N_CTX, 0, 1 + (r - N_CTX) // DEC_SEQ)


def _modulate(x_ref, sh_ref, sc_ref):
    return x_ref[...] * (1.0 + sc_ref[0]) + sh_ref[0]


def _mod_specs(tm, n_axes, row_axis):
    def rows(*ids):
        return (ids[row_axis], 0)

    def grp(*ids):
        return (_group_of_row(ids[row_axis] * tm), 0, 0)

    del n_axes
    return [pl.BlockSpec((tm, D), rows), pl.BlockSpec((1, 1, D), grp), pl.BlockSpec((1, 1, D), grp)]


def _ada_body(c_ref, w_ref, b_ref, o_ref):
    c = c_ref[...]
    s = c * jax.nn.sigmoid(c)
    o_ref[0] = jnp.dot(s, w_ref[0], preferred_element_type=F32, precision=HIGHEST) + b_ref[0]


def _ada(cvec, w_ada, b_ada):
    tn = 1536
    return pl.pallas_call(
        _ada_body,
        out_shape=jax.ShapeDtypeStruct((DEPTH, 8, 6 * D), F32),
        grid=(DEPTH, 6 * D // tn),
        in_specs=[pl.BlockSpec((8, D), lambda l, j: (0, 0)),
                  pl.BlockSpec((1, D, tn), lambda l, j: (l, 0, j)),
                  pl.BlockSpec((1, 1, tn), lambda l, j: (l, 0, j))],
        out_specs=pl.BlockSpec((1, 8, tn), lambda l, j: (l, 0, j)),
        compiler_params=_params(2),
        name="ada_modulation",
    )(cvec, w_ada, b_ada.reshape(DEPTH, 1, 6 * D))


def _mod_mm_body(x_ref, sh_ref, sc_ref, *refs, n_w):
    w_refs, o_ref, h_ref = refs[:n_w], refs[n_w], refs[n_w + 1]

    @pl.when(pl.program_id(2) == 0)
    def _():
        h_ref[...] = _modulate(x_ref, sh_ref, sc_ref).astype(BF16)

    h = h_ref[...]
    if n_w == 1:
        o = jnp.dot(h, w_refs[0][0], preferred_element_type=F32)
    else:
        g = jnp.dot(h, w_refs[0][0], preferred_element_type=F32)
        u = jnp.dot(h, w_refs[1][0], preferred_element_type=F32)
        o = g * jax.nn.sigmoid(g) * u
    o_ref[0] = o.astype(o_ref.dtype)


def _mod_matmul(x, sh, sc, ws, *, tm, tn, out_dtype, name):
    n_e, _, f = ws[0].shape
    return pl.pallas_call(
        functools.partial(_mod_mm_body, n_w=len(ws)),
        out_shape=jax.ShapeDtypeStruct((n_e, N_TOK, f), out_dtype),
        grid=(n_e, N_TOK // tm, f // tn),
        in_specs=_mod_specs(tm, 3, 1) + [pl.BlockSpec((1, D, tn), lambda e, i, j: (e, 0, j)) for _ in ws],
        out_specs=pl.BlockSpec((1, tm, tn), lambda e, i, j: (e, i, j)),
        scratch_shapes=[pltpu.VMEM((tm, D), BF16)],
        compiler_params=_params(3),
        name=name,
    )(x, sh, sc, *ws)


def _split_specs(tm, width):
    nc = N_CTX // tm
    return [pl.BlockSpec((tm, width), lambda i: (jnp.minimum(i, nc - 1), 0)),
            pl.BlockSpec((tm, width), lambda i: (jnp.maximum(i - nc, 0), 0))]


def _pick_split(ctx_ref, lat_ref):
    tm = ctx_ref.shape[0]
    return jnp.where(pl.program_id(0) < N_CTX // tm, ctx_ref[...], lat_ref[...])


def _proj_res_ln_body(*refs, split):
    n_in = sum(2 if s else 1 for s in split)
    part_refs = list(refs[:n_in])
    w_ref, x_ref, g_ref, lng_ref, lnb_ref, o_ref = refs[n_in:]
    cols = [_pick_split(part_refs.pop(0), part_refs.pop(0)) if s else part_refs.pop(0)[...] for s in split]
    a = cols[0] if len(cols) == 1 else jnp.concatenate(cols, axis=-1)
    y = ALPHA * x_ref[...] + g_ref[0] * jnp.dot(a, w_ref[...], preferred_element_type=F32)
    mu = jnp.mean(y, axis=-1, keepdims=True)
    yc = y - mu
    var = jnp.mean(yc * yc, axis=-1, keepdims=True)
    o_ref[...] = yc * lax.rsqrt(var + LN_EPS) * lng_ref[...] + lnb_ref[...]


def _proj_res_ln(parts, w, x, gate, ln_g, ln_b, *, tm, name):
    split = tuple(isinstance(p, tuple) for p in parts)
    in_specs, args = [], []
    for p, s in zip(parts, split):
        if s:
            in_specs += _split_specs(tm, p[0].shape[1])
            args += list(p)
        else:
            in_specs.append(pl.BlockSpec((tm, p.shape[1]), lambda i: (i, 0)))
            args.append(p)
    in_specs += [pl.BlockSpec(w.shape, lambda i: (0, 0)),
                 pl.BlockSpec((tm, D), lambda i: (i, 0)),
                 pl.BlockSpec((1, 1, D), lambda i: (_group_of_row(i * tm), 0, 0)),
                 pl.BlockSpec((1, D), lambda i: (0, 0)),
                 pl.BlockSpec((1, D), lambda i: (0, 0))]
    return pl.pallas_call(
        functools.partial(_proj_res_ln_body, split=split),
        out_shape=jax.ShapeDtypeStruct((N_TOK, D), F32),
        grid=(N_TOK // tm,),
        in_specs=in_specs,
        out_specs=pl.BlockSpec((tm, D), lambda i: (i, 0)),
        compiler_params=_params(1),
        name=name,
    )(*args, w, x, gate, ln_g.reshape(1, D), ln_b.reshape(1, D))


def _log_sigmoid(x):
    return jnp.minimum(x, 0.0) - jnp.log(1.0 + jnp.exp(-jnp.abs(x)))


def _gates_body(x_ref, sh_ref, sc_ref, wg_ref, wgt_ref, b_ref, bt_ref,
                lic_ref, bc_ref, lir_ref, br_ref, *, tm):
    h = _modulate(x_ref, sh_ref, sc_ref)
    g = jnp.dot(h, wg_ref[...], preferred_element_type=F32, precision=HIGHEST) + b_ref[...]
    gt = lax.dot_general(wgt_ref[...], h, (((1,), (1,)), ((), ())),
                         preferred_element_type=F32, precision=HIGHEST) + bt_ref[...]
    lic_ref[...] = g
    lir_ref[...] = gt
    lf, lft = _log_sigmoid(g), _log_sigmoid(gt)
    r = lax.broadcasted_iota(jnp.int32, (CHUNK, CHUNK), 0)
    c = lax.broadcasted_iota(jnp.int32, (CHUNK, CHUNK), 1)
    tri_l = (c <= r).astype(F32)
    tri_u = (c >= r).astype(F32)
    fwd_col = lax.broadcasted_iota(jnp.int32, (CHUNK, LANES), 1) < 2 * ML_HEADS
    fwd_row = lax.broadcasted_iota(jnp.int32, (N_GATES, CHUNK), 0) < 2 * ML_HEADS
    for ch in range(tm // CHUNK):
        sl = slice(ch * CHUNK, (ch + 1) * CHUNK)
        lfc, lftc = lf[sl, :], lft[:, sl]
        cum_f = jnp.dot(tri_l, lfc, preferred_element_type=F32, precision=HIGHEST)
        cum_b = jnp.dot(tri_u, lfc, preferred_element_type=F32, precision=HIGHEST)
        bc_ref[sl, :] = jnp.where(fwd_col, cum_f, cum_b)
        cum_f = jnp.dot(lftc, tri_u, preferred_element_type=F32, precision=HIGHEST)
        cum_b = jnp.dot(lftc, tri_l, preferred_element_type=F32, precision=HIGHEST)
        br_ref[:, sl] = jnp.where(fwd_row, cum_f, cum_b)


def _gates(x, sh, sc, wg, b_gate):
    tm = 256
    wg_pad = jnp.pad(wg, ((0, 0), (0, LANES - N_GATES)))
    b_pad = jnp.pad(b_gate, (0, LANES - N_GATES)).reshape(1, LANES)
    col = pl.BlockSpec((tm, LANES), lambda i: (i, 0))
    row = pl.BlockSpec((N_GATES, tm), lambda i: (0, i))
    return pl.pallas_call(
        functools.partial(_gates_body, tm=tm),
        out_shape=(jax.ShapeDtypeStruct((N_TOK, LANES), F32), jax.ShapeDtypeStruct((N_TOK, LANES), F32),
                   jax.ShapeDtypeStruct((N_GATES, N_TOK), F32), jax.ShapeDtypeStruct((N_GATES, N_TOK), F32)),
        grid=(N_TOK // tm,),
        in_specs=_mod_specs(tm, 1, 0) + [pl.BlockSpec((D, LANES), lambda i: (0, 0)),
                                         pl.BlockSpec((N_GATES, D), lambda i: (0, 0)),
                                         pl.BlockSpec((1, LANES), lambda i: (0, 0)),
                                         pl.BlockSpec((N_GATES, 1), lambda i: (0, 0))],
        out_specs=(col, col, row, row),
        compiler_params=_params(1),
        name="mlstm_gates",
    )(x, sh, sc, wg_pad, wg.T, b_pad, b_gate.reshape(N_GATES, 1))


def _mlstm_body(*refs, has_init):
    (qf, kf, vf, licf, bcf, lirf, brf, qb, kb, vb, licb, bcb, lirb, brb) = refs[:14]
    refs = refs[14:]
    if has_init:
        c0_ref, n0_ref, m0_ref = refs[:3]
        refs = refs[3:]
    hf_ref, hb_ref, c_ref, n_ref, m_ref = refs

    @pl.when(pl.program_id(1) == 0)
    def _():
        if has_init:
            c_ref[...] = c0_ref[...]
            n_ref[...] = n0_ref[...]
            m_ref[...] = m0_ref[...]
        else:
            c_ref[...] = jnp.zeros_like(c_ref)
            n_ref[...] = jnp.zeros_like(n_ref)
            m_ref[...] = jnp.zeros_like(m_ref)

    t_idx = lax.broadcasted_iota(jnp.int32, (CHUNK, CHUNK), 0)
    s_idx = lax.broadcasted_iota(jnp.int32, (CHUNK, CHUNK), 1)
    nt = (((1,), (1,)), ((), ()))
    stores = []

    def chain(d, h, q_ref, k_ref, v_ref, lic_ref, bc_ref, lir_ref, br_ref, h_ref):
        mask = (s_idx <= t_idx) if d == 0 else (s_idx >= t_idx)
        hs = slice(h * ML_HEAD_DIM, (h + 1) * ML_HEAD_DIM)
        gi, gf = d * 2 * ML_HEADS + h, d * 2 * ML_HEADS + ML_HEADS + h
        q = q_ref[:, hs]
        k = k_ref[:, hs] * (ML_HEAD_DIM ** -0.5)
        v = v_ref[:, hs]
        qh, kh, vh = q.astype(BF16), k.astype(BF16), v.astype(BF16)
        li_c, b_c = lic_ref[:, gi:gi + 1], bc_ref[:, gf:gf + 1]
        li_r, b_r = lir_ref[gi:gi + 1, :], br_ref[gf:gf + 1, :]
        c_st = c_ref[0, d, h]
        n_st = n_ref[0, d, h:h + 1, :]
        m_st = m_ref[0, d, h:h + 1, :][:, 0:1]
        dmat = jnp.where(mask, b_c - b_r + li_r, -jnp.inf)
        inter = b_c + m_st
        m_out = jnp.maximum(inter, jnp.max(dmat, axis=-1, keepdims=True))
        p = jnp.exp(dmat - m_out)
        w_inter = jnp.exp(inter - m_out)
        yield
        qk = lax.dot_general(qh, kh, nt, preferred_element_type=F32)
        qc = jnp.dot(qh, c_st.astype(BF16), preferred_element_type=F32)
        yield
        s = qk * p
        den = (jnp.sum(s, axis=-1, keepdims=True)
               + w_inter * jnp.sum(q * n_st, axis=-1, keepdims=True))
        sh = s.astype(BF16)
        b_last = b_r[:, CHUNK - 1:CHUNK] if d == 0 else b_r[:, 0:1]
        g_r = b_last - b_r + li_r
        g_c = b_last - b_c + li_c
        m_new = jnp.maximum(b_last + m_st, jnp.max(g_r, axis=-1, keepdims=True))
        decay = jnp.exp(b_last + m_st - m_new)
        kw = k * jnp.exp(g_c - m_new)
        kwt = kw.T.astype(BF16)
        yield
        sv = jnp.dot(sh, vh, preferred_element_type=F32)
        kv = jnp.dot(kwt, vh, preferred_element_type=F32)
        yield
        h_out = (sv + w_inter * qc) / jnp.maximum(jnp.abs(den), jnp.exp(-m_out))
        c_new = decay * c_st + kv
        n_new = decay * n_st + jnp.sum(kw, axis=0, keepdims=True)
        stores.append((h_ref, d, h, hs, h_out, c_new, n_new, jnp.broadcast_to(m_new, (1, ML_HEAD_DIM))))
        yield

    chains = [chain(d, h, *group)
              for d, group in enumerate(((qf, kf, vf, licf, bcf, lirf, brf, hf_ref),
                                         (qb, kb, vb, licb, bcb, lirb, brb, hb_ref)))
              for h in range(ML_HEADS)]
    for _ in range(5):
        for ch in chains:
            next(ch)
    for h_ref, d, h, hs, h_out, c_new, n_new, m_new in stores:
        h_ref[:, hs] = h_out
        c_ref[0, d, h] = c_new
        n_ref[0, d, h:h + 1, :] = n_new
        m_ref[0, d, h:h + 1, :] = m_new


def _mlstm(proj, lic, bc, lir, br, *, row0, n_seq, seq_len, init=None):
    nc = seq_len // CHUNK
    base = row0 // CHUNK

    def fwd(b, j):
        return base + b * nc + j

    def bwd(b, j):
        return base + b * nc + (nc - 1 - j)

    def chunk_specs(pos):
        return ([pl.BlockSpec((CHUNK, ML_W), lambda b, j, c=c: (pos(b, j), c)) for c in range(3)]
                + [pl.BlockSpec((CHUNK, LANES), lambda b, j: (pos(b, j), 0))] * 2
                + [pl.BlockSpec((N_GATES, CHUNK), lambda b, j: (0, pos(b, j)))] * 2)

    st_c = pl.BlockSpec((1, 2, ML_HEADS, ML_HEAD_DIM, ML_HEAD_DIM), lambda b, j: (b, 0, 0, 0, 0))
    st_n = pl.BlockSpec((1, 2, ML_HEADS, ML_HEAD_DIM), lambda b, j: (b, 0, 0, 0))
    in_specs = chunk_specs(fwd) + chunk_specs(bwd)
    args = [proj, proj, proj, lic, bc, lir, br] * 2
    if init is not None:
        in_specs += [st_c, st_n, st_n]
        args += list(init)
    rows = n_seq * seq_len
    return pl.pallas_call(
        functools.partial(_mlstm_body, has_init=init is not None),
        out_shape=(jax.ShapeDtypeStruct((rows, ML_W), F32), jax.ShapeDtypeStruct((rows, ML_W), F32),
                   jax.ShapeDtypeStruct((n_seq, 2, ML_HEADS, ML_HEAD_DIM, ML_HEAD_DIM), F32),
                   jax.ShapeDtypeStruct((n_seq, 2, ML_HEADS, ML_HEAD_DIM), F32),
                   jax.ShapeDtypeStruct((n_seq, 2, ML_HEADS, ML_HEAD_DIM), F32)),
        grid=(n_seq, nc),
        in_specs=in_specs,
        out_specs=(pl.BlockSpec((CHUNK, ML_W), lambda b, j: (b * nc + j, 0)),
                   pl.BlockSpec((CHUNK, ML_W), lambda b, j: (b * nc + (nc - 1 - j), 0)),
                   st_c, st_n, st_n),
        compiler_params=_params(2),
        name="mlstm_scan",
    )(*args)


def _ml_post_body(hfc_ref, hfs_ref, hbc_ref, hbs_ref, o_ref, g_ref, out_ref):
    h = _pick_split(hfc_ref, hfs_ref) + _pick_split(hbc_ref, hbs_ref)
    gate = jax.nn.sigmoid(o_ref[...]) * g_ref[...]
    for hd in range(ML_HEADS):
        hs = slice(hd * ML_HEAD_DIM, (hd + 1) * ML_HEAD_DIM)
        x = h[:, hs]
        xc = x - jnp.mean(x, axis=-1, keepdims=True)
        var = jnp.mean(xc * xc, axis=-1, keepdims=True)
        out_ref[:, hs] = (gate[:, hs] * (xc * lax.rsqrt(var + RMS_EPS))).astype(BF16)


def _ml_post(hf, hb, proj, norm_g):
    tm = 512
    blk = pl.BlockSpec((tm, ML_W), lambda i: (i, 0))
    return pl.pallas_call(
        _ml_post_body,
        out_shape=jax.ShapeDtypeStruct((N_TOK, ML_W), BF16),
        grid=(N_TOK // tm,),
        in_specs=(_split_specs(tm, ML_W) * 2
                  + [pl.BlockSpec((tm, ML_W), lambda i: (i, 3)), pl.BlockSpec((1, ML_W), lambda i: (0, 0))]),
        out_specs=blk,
        compiler_params=_params(1),
        name="mlstm_out_norm",
    )(*hf, *hb, proj, norm_g.reshape(1, ML_W))


def _head_rms(x, gain):
    lane_head = lax.broadcasted_iota(jnp.int32, x.shape, 1) // ATT_HEAD_DIM
    sq = x * x
    ms = jnp.zeros_like(x)
    for hd in range(x.shape[1] // ATT_HEAD_DIM):
        sel = lane_head == hd
        ms = jnp.where(sel, jnp.sum(jnp.where(sel, sq, 0.0), axis=-1, keepdims=True), ms)
    return x * lax.rsqrt(ms * (1.0 / ATT_HEAD_DIM) + RMS_EPS) * gain


def _rope(x, cos, sin_signed):
    w = x.shape[1]
    even = lax.broadcasted_iota(jnp.int32, x.shape, 1) % 2 == 0
    partner = jnp.where(even, pltpu.roll(x, w - 1, 1), pltpu.roll(x, 1, 1))
    return x * cos + partner * sin_signed


def _qk_prep_body(q_ref, k_ref, qg_ref, kg_ref, *refs, rope):
    if rope:
        cq_ref, sq_ref, ck_ref, sk_ref, qo_ref, kn_ref, kr_ref = refs
    else:
        qo_ref, kn_ref, kr_ref = refs
    q = _head_rms(q_ref[...], qg_ref[...])
    k = _head_rms(k_ref[...], kg_ref[...])
    kn_ref[...] = k
    if rope:
        q = _rope(q, cq_ref[...], sq_ref[...])
        k = _rope(k, ck_ref[...], sk_ref[...])
    qo_ref[...] = (q * (ATT_HEAD_DIM ** -0.5)).astype(BF16)
    kr_ref[...] = k.astype(BF16)


def _qk_prep(proj, q_gain, k_gain, *, row0, rows, rope_tabs=None):
    tm = 512
    r0 = row0 // tm
    in_specs = [pl.BlockSpec((tm, ATT_W), lambda i: (r0 + i, 4 * ML_W // ATT_W)),
                pl.BlockSpec((tm, KV_W), lambda i: (r0 + i, (4 * ML_W + ATT_W) // KV_W)),
                pl.BlockSpec((1, ATT_W), lambda i: (0, 0)),
                pl.BlockSpec((1, KV_W), lambda i: (0, 0))]
    args = [proj, proj, jnp.tile(q_gain, ATT_HEADS).reshape(1, ATT_W), jnp.tile(k_gain, ATT_KV_HEADS).reshape(1, KV_W)]
    if rope_tabs is not None:
        per_seq = DEC_SEQ // tm
        in_specs += [pl.BlockSpec((tm, ATT_W), lambda i: (i % per_seq, 0))] * 2
        in_specs += [pl.BlockSpec((tm, KV_W), lambda i: (i % per_seq, 0))] * 2
        args += list(rope_tabs)
    return pl.pallas_call(
        functools.partial(_qk_prep_body, rope=rope_tabs is not None),
        out_shape=(jax.ShapeDtypeStruct((rows, ATT_W), BF16), jax.ShapeDtypeStruct((rows, KV_W), F32),
                   jax.ShapeDtypeStruct((rows, KV_W), BF16)),
        grid=(rows // tm,),
        in_specs=in_specs,
        out_specs=(pl.BlockSpec((tm, ATT_W), lambda i: (i, 0)), pl.BlockSpec((tm, KV_W), lambda i: (i, 0)),
                   pl.BlockSpec((tm, KV_W), lambda i: (i, 0))),
        compiler_params=_params(1),
        name="attn_qk_prep",
    )(*args)


def _rope_tables():
    rows = DEC_SEQ // GRID_W
    axis_dim = ATT_HEAD_DIM // 2
    row = jnp.repeat(jnp.arange(rows, dtype=F32), GRID_W)
    col = (jnp.arange(DEC_SEQ) % GRID_W).astype(F32)
    inv = ROPE_BASE ** (-jnp.arange(axis_dim // 2, dtype=F32) * 2.0 / axis_dim)
    ang = jnp.concatenate([row[:, None] * inv, col[:, None] * inv], axis=-1)
    cos = jnp.repeat(jnp.cos(ang), 2, axis=-1)
    sin = jnp.repeat(jnp.sin(ang), 2, axis=-1) * jnp.tile(jnp.array([-1.0, 1.0], F32), axis_dim)
    return (jnp.tile(cos, (1, ATT_HEADS)), jnp.tile(sin, (1, ATT_HEADS)),
            jnp.tile(cos, (1, ATT_KV_HEADS)), jnp.tile(sin, (1, ATT_KV_HEADS)))


def _attn_body(q_ref, k_ref, v_ref, o_ref):
    q = q_ref[...]
    k = k_ref[0, 0]
    v = v_ref[0, 0]
    q_head = lax.broadcasted_iota(jnp.int32, q.shape, 1) // ATT_HEAD_DIM
    v_head = lax.broadcasted_iota(jnp.int32, v.shape, 1) // ATT_HEAD_DIM
    acc = jnp.zeros(q.shape, F32)
    for g in range(ATT_GROUP):
        qg = jnp.where(q_head == g, q, jnp.zeros_like(q))
        s = lax.dot_general(qg, k, (((1,), (1,)), ((), ())), preferred_element_type=F32)
        e = jnp.exp(s - jnp.max(s, axis=-1, keepdims=True))
        vg = jnp.where(v_head == g, v, jnp.zeros_like(v))
        o = jnp.dot(e.astype(BF16), vg, preferred_element_type=F32)
        acc = acc + o / jnp.sum(e, axis=-1, keepdims=True)
    o_ref[...] = acc.astype(BF16)


def _attention(q, k_rep, v_rep, *, n_seq, seq_len, tq):
    s_len = k_rep.shape[2]
    nq = seq_len // tq
    kv = pl.BlockSpec((1, 1, s_len, GROUP_W), lambda b, kh, i: (b, kh, 0, 0))
    qo = pl.BlockSpec((tq, GROUP_W), lambda b, kh, i: (b * nq + i, kh))
    return pl.pallas_call(
        _attn_body,
        out_shape=jax.ShapeDtypeStruct((n_seq * seq_len, ATT_W), BF16),
        grid=(n_seq, ATT_KV_HEADS, nq),
        in_specs=[qo, kv, kv],
        out_specs=qo,
        compiler_params=_params(3),
        name="attention",
    )(q, k_rep, v_rep)


def _rep_heads(x, n_seq):
    x = x.reshape(n_seq, -1, ATT_KV_HEADS, ATT_HEAD_DIM).transpose(0, 2, 1, 3)
    return jnp.tile(x, (1, 1, 1, ATT_GROUP))


def _hy_filter_body(feat_ref, t_ref, w1_ref, b1_ref, w2_ref, b2_ref, fr_ref, w3f_ref, w3b_ref, dl_ref,
                    hsum_ref, hdiff_ref, nyq_ref, z_ref):
    @pl.when(pl.program_id(0) == 0)
    def _():
        z = jnp.dot(feat_ref[...], w1_ref[...], preferred_element_type=F32, precision=HIGHEST) + b1_ref[...]
        z = jnp.sin(fr_ref[0:1, :] * z)
        z = jnp.dot(z, w2_ref[...], preferred_element_type=F32, precision=HIGHEST) + b2_ref[...]
        z_ref[...] = jnp.sin(fr_ref[1:2, :] * z)

    z = z_ref[...]
    window = jnp.exp(-t_ref[...] * dl_ref[...])
    h_f = jnp.dot(z, w3f_ref[...], preferred_element_type=F32, precision=HIGHEST) * window
    h_b = jnp.dot(z, w3b_ref[...], preferred_element_type=F32, precision=HIGHEST) * window
    row = lax.broadcasted_iota(jnp.int32, h_f.shape, 0)
    h_b = jnp.where(row == 0, 0.0, h_b)
    inv = 1.0 / (jnp.sum(jnp.abs(h_f), axis=0, keepdims=True) + jnp.sum(jnp.abs(h_b), axis=0, keepdims=True))
    h_sum = (h_f + h_b) * inv
    hsum_ref[...] = h_sum
    hdiff_ref[...] = (h_f - h_b) * inv
    nyq_ref[...] = jnp.sum(jnp.where(row % 2 == 0, h_sum, -h_sum), axis=0, keepdims=True)


def _hy_filter(seq_len, w1, b1, w2, b2, w3, sin_freq):
    tc = 256
    fw = w1.shape[1]
    t = jnp.arange(seq_len, dtype=F32)[:, None] / seq_len
    bands = jnp.arange(1, HY_BANDS + 1, dtype=F32)[None, :]
    feat = jnp.concatenate([t, jnp.sin(2.0 * math.pi * bands * t), jnp.cos(2.0 * math.pi * bands * t)], axis=-1)
    feat = jnp.pad(feat, ((0, 0), (0, LANES - HY_EMB)))
    deltas = jnp.abs(jnp.linspace(math.log(HY_TARGET) / HY_LONG_PCT, math.log(HY_TARGET) / HY_SHORT_PCT, D,
                                  dtype=F32)).reshape(1, D)
    pad_w = LANES - fw
    full = lambda shape: pl.BlockSpec(shape, lambda j: (0,) * len(shape))
    return pl.pallas_call(
        _hy_filter_body,
        out_shape=(jax.ShapeDtypeStruct((seq_len, D), F32), jax.ShapeDtypeStruct((seq_len, D), F32),
                   jax.ShapeDtypeStruct((1, D), F32)),
        grid=(D // tc,),
        in_specs=[full((seq_len, LANES)), full((seq_len, 1)), full((LANES, LANES)), full((1, LANES)),
                  full((LANES, LANES)), full((1, LANES)), full((2, LANES)),
                  pl.BlockSpec((LANES, tc), lambda j: (0, j)), pl.BlockSpec((LANES, tc), lambda j: (0, D // tc + j)),
                  pl.BlockSpec((1, tc), lambda j: (0, j))],
        out_specs=(pl.BlockSpec((seq_len, tc), lambda j: (0, j)), pl.BlockSpec((seq_len, tc), lambda j: (0, j)),
                   pl.BlockSpec((1, tc), lambda j: (0, j))),
        scratch_shapes=[pltpu.VMEM((seq_len, LANES), F32)],
        compiler_params=_params(1),
        name="hyena_filter",
    )(feat, t, jnp.pad(w1, ((0, LANES - HY_EMB), (0, pad_w))), jnp.pad(b1, (0, pad_w)).reshape(1, LANES),
      jnp.pad(w2, ((0, pad_w), (0, pad_w))), jnp.pad(b2, (0, pad_w)).reshape(1, LANES),
      jnp.pad(sin_freq, ((0, 0), (0, pad_w))), jnp.pad(w3, ((0, pad_w), (0, 0))), jnp.pad(w3, ((0, pad_w), (0, 0))),
      deltas)


def _dft_matrices(seq_len):
    n = 2 * seq_len
    k = lax.broadcasted_iota(jnp.int32, (seq_len, seq_len), 0)
    t = lax.broadcasted_iota(jnp.int32, (seq_len, seq_len), 1)
    ang = ((k * t) % n).astype(F32) * (2.0 * math.pi / n)
    cr, base = jnp.cos(ang), -jnp.sin(ang)
    ci = jnp.where(k == 0, (1 - 2 * (t % 2)).astype(F32), base)
    cit = jnp.where(t == 0, (1 - 2 * (k % 2)).astype(F32), base)
    return cr.astype(BF16), ci.astype(BF16), cit.astype(BF16)


def _dft_fwd_body(cr_ref, ci_ref, b1_ref, b2_ref, *refs, nk, tf, mult):
    if mult:
        kr_ref, ki_ref, or_ref, oi_ref, accr_ref, acci_ref = refs
    else:
        or_ref, oi_ref, accr_ref, acci_ref = refs
    kk = pl.program_id(3)
    pr = jnp.dot(cr_ref[...], b1_ref[0].astype(BF16), preferred_element_type=F32)
    pi = jnp.dot(ci_ref[...], b2_ref[0].astype(BF16), preferred_element_type=F32)

    @pl.when(kk == 0)
    def _():
        accr_ref[...] = pr
        acci_ref[...] = pi

    @pl.when(kk > 0)
    def _():
        accr_ref[...] += pr
        acci_ref[...] += pi

    @pl.when(kk == nk - 1)
    def _():
        zr, zi = accr_ref[...], acci_ref[...]
        if mult:
            kr, ki = kr_ref[...], ki_ref[...]
            first = (pl.program_id(1) * tf + lax.broadcasted_iota(jnp.int32, zr.shape, 0)) == 0
            or_ref[0] = jnp.where(first, 0.5 * zr * kr, zr * kr - zi * ki).astype(or_ref.dtype)
            oi_ref[0] = jnp.where(first, 0.5 * zi * ki, zr * ki + zi * kr).astype(oi_ref.dtype)
        else:
            or_ref[0] = zr
            oi_ref[0] = zi


def _dft_fwd(mats, b1, b2, *, row0, n_seq, seq_len, tf, tk, tn, filt=None):
    cr, ci, _ = mats
    nk = seq_len // tk
    r0 = row0 // tk
    a_spec = pl.BlockSpec((tf, tk), lambda b, f, c, kk: (f, kk))
    b_spec = pl.BlockSpec((1, tk, tn), lambda b, f, c, kk: (0, r0 + b * nk + kk, c))
    o_spec = pl.BlockSpec((1, tf, tn), lambda b, f, c, kk: (b, f, c))
    in_specs = [a_spec] * 2 + [b_spec] * 2
    args = [cr, ci, b1[None], b2[None]]
    if filt is not None:
        in_specs += [pl.BlockSpec((tf, tn), lambda b, f, c, kk: (f, c))] * 2
        args += list(filt)
    shape = jax.ShapeDtypeStruct((n_seq, seq_len, D), F32 if filt is None else BF16)
    return pl.pallas_call(
        functools.partial(_dft_fwd_body, nk=nk, tf=tf, mult=filt is not None),
        out_shape=(shape, shape),
        grid=(n_seq, seq_len // tf, D // tn, nk),
        in_specs=in_specs,
        out_specs=(o_spec, o_spec),
        scratch_shapes=[pltpu.VMEM((tf, tn), F32), pltpu.VMEM((tf, tn), F32)],
        compiler_params=_params(4),
        name="hyena_dft",
    )(*args)


def _dft_inv_body(cr_ref, ct_ref, yr_ref, yi_ref, z_ref, x0_ref, skip_ref, o_ref, acc_ref, *, nk, scale):
    kk = pl.program_id(3)
    p = (jnp.dot(cr_ref[...], yr_ref[0], preferred_element_type=F32)
         + jnp.dot(ct_ref[...], yi_ref[0], preferred_element_type=F32))

    @pl.when(kk == 0)
    def _():
        acc_ref[...] = p

    @pl.when(kk > 0)
    def _():
        acc_ref[...] += p

    @pl.when(kk == nk - 1)
    def _():
        z = z_ref[...]
        o_ref[...] = ((acc_ref[...] * scale + z * skip_ref[...]) * x0_ref[...]).astype(BF16)


def _dft_inv(mats, yr, yi, z, x0, skip, *, row0, n_seq, seq_len, tm, tk, tn):
    cr, _, cit = mats
    nk = seq_len // tk
    nt = seq_len // tm
    r0 = row0 // tm
    a_spec = pl.BlockSpec((tm, tk), lambda b, i, c, kk: (i, kk))
    y_spec = pl.BlockSpec((1, tk, tn), lambda b, i, c, kk: (b, kk, c))
    tok = pl.BlockSpec((tm, tn), lambda b, i, c, kk: (r0 + b * nt + i, c))
    return pl.pallas_call(
        functools.partial(_dft_inv_body, nk=nk, scale=1.0 / seq_len),
        out_shape=jax.ShapeDtypeStruct((n_seq * seq_len, D), BF16),
        grid=(n_seq, nt, D // tn, nk),
        in_specs=[a_spec] * 2 + [y_spec] * 2 + [tok, tok, pl.BlockSpec((1, tn), lambda b, i, c, kk: (0, c))],
        out_specs=pl.BlockSpec((tm, tn), lambda b, i, c, kk: (b * nt + i, c)),
        scratch_shapes=[pltpu.VMEM((tm, tn), F32)],
        compiler_params=_params(4),
        name="hyena_idft",
    )(cr, cit, yr, yi, z, x0, skip.reshape(1, D))


def _hy_gate_body(x0_ref, x1_ref, v_ref, w0_ref, w1_ref, wv_ref, b0_ref, b1_ref, bv_ref, z_ref, x0o_ref, *, seq_len):
    def conv(u_ref, w_ref, b_ref):
        u = u_ref[...]
        n = u.shape[0]
        pos = lax.broadcasted_iota(jnp.int32, u.shape, 0) % seq_len
        prev = jnp.where(pos == 0, 0.0, pltpu.roll(u, 1, 0))
        nxt = jnp.where(pos == seq_len - 1, 0.0, pltpu.roll(u, n - 1, 0))
        return prev * w_ref[0:1, :] + u * w_ref[1:2, :] + nxt * w_ref[2:3, :] + b_ref[...]

    x0o_ref[...] = conv(x0_ref, w0_ref, b0_ref)
    z_ref[...] = conv(v_ref, wv_ref, bv_ref) * conv(x1_ref, w1_ref, b1_ref)


def _hy_gate(u, conv_w, conv_b, *, row0, n_seq, seq_len, seqs_per_step):
    tc = 128
    nb = D // tc
    rows = seqs_per_step * seq_len
    r0 = row0 // rows

    def col(part):
        return [pl.BlockSpec((rows, tc), lambda b, c: (r0 + b, part * nb + c)),
                pl.BlockSpec((3, tc), lambda b, c: (0, part * nb + c)),
                pl.BlockSpec((1, tc), lambda b, c: (0, part * nb + c))]

    specs = [col(p) for p in range(3)]
    out = pl.BlockSpec((rows, tc), lambda b, c: (b, c))
    shape = jax.ShapeDtypeStruct((n_seq * seq_len, D), F32)
    return pl.pallas_call(
        functools.partial(_hy_gate_body, seq_len=seq_len),
        out_shape=(shape, shape),
        grid=(n_seq // seqs_per_step, nb),
        in_specs=[s[0] for s in specs] + [s[1] for s in specs] + [s[2] for s in specs],
        out_specs=(out, out),
        compiler_params=_params(2),
        name="hyena_short_conv",
    )(u, u, u, conv_w, conv_w, conv_w, conv_b.reshape(1, 3 * D), conv_b.reshape(1, 3 * D), conv_b.reshape(1, 3 * D))


FFT_A, FFT_R = 64, 64
FFT_M = 2 * FFT_A
FFT_H = FFT_R // 2
assert FFT_A * FFT_R == DEC_SEQ


def _fft_consts():
    n = 2 * DEC_SEQ
    th = 2.0 * np.pi * (np.arange(FFT_M)[:, None] + 0.5) * np.arange(FFT_A)[None, :] / FFT_M
    f1 = np.concatenate([np.cos(th), -np.sin(th)], axis=0)
    k = np.arange(FFT_M)[:, None, None] + FFT_M * np.arange(FFT_H)[None, :, None] + 0.5
    ph = 2.0 * np.pi * k * np.arange(FFT_R)[None, None, :] / n
    c, s = np.cos(ph), np.sin(ph)
    g = np.concatenate([np.concatenate([c, s], axis=2), np.concatenate([-s, c], axis=2)], axis=1)
    as_bf16 = lambda m: jnp.asarray(m, dtype=F32).astype(BF16)
    return as_bf16(f1), as_bf16(f1.T), as_bf16(g), as_bf16(g.transpose(0, 2, 1))


def _fft_stage1(src_ref, y_ref, f1):
    for b in range(FFT_R):
        zb = src_ref[pl.ds(b, FFT_A, stride=FFT_R), :].astype(BF16)
        y_ref[b * 2 * FFT_M:(b + 1) * 2 * FFT_M, :] = jnp.dot(f1, zb, preferred_element_type=F32)


def _fft_stage2(y_ref, g_ref, k1):
    yr = y_ref[pl.ds(k1, FFT_R, stride=2 * FFT_M), :]
    yi = y_ref[pl.ds(FFT_M + k1, FFT_R, stride=2 * FFT_M), :]
    z = jnp.dot(g_ref[k1], jnp.concatenate([yr, yi], axis=0).astype(BF16), preferred_element_type=F32)
    return z[:FFT_H], z[FFT_H:]


def _hy_spectrum_body(hs_ref, hd_ref, f1_ref, g_ref, kr_ref, ki_ref, y_ref):
    f1 = f1_ref[...]
    _fft_stage1(hs_ref, y_ref, f1)
    for k1 in range(FFT_M):
        kr_ref[k1 * FFT_H:(k1 + 1) * FFT_H, :] = _fft_stage2(y_ref, g_ref, k1)[0]
    _fft_stage1(hd_ref, y_ref, f1)
    for k1 in range(FFT_M):
        ki_ref[k1 * FFT_H:(k1 + 1) * FFT_H, :] = _fft_stage2(y_ref, g_ref, k1)[1]


def _hy_spectrum(h_sum, h_diff, consts):
    tc = 128
    f1, _, g, _ = consts
    blk = pl.BlockSpec((DEC_SEQ, tc), lambda c: (0, c))
    shape = jax.ShapeDtypeStruct((DEC_SEQ, D), F32)
    return pl.pallas_call(
        _hy_spectrum_body,
        out_shape=(shape, shape),
        grid=(D // tc,),
        in_specs=[blk, blk, pl.BlockSpec(f1.shape, lambda c: (0, 0)), pl.BlockSpec(g.shape, lambda c: (0, 0, 0))],
        out_specs=(blk, blk),
        scratch_shapes=[pltpu.VMEM((FFT_R * 2 * FFT_M, tc), F32)],
        compiler_params=_params(1),
        name="hyena_filter_fft",
    )(h_sum, h_diff, f1, g)


def _hy_conv_fft_body(x0_ref, x1_ref, v_ref, w0_ref, w1_ref, wv_ref, b0_ref, b1_ref, bv_ref, kr_ref, ki_ref,
                      skip_ref, f1_ref, f1t_ref, g_ref, gt_ref, o_ref, z_ref, y_ref, t_ref):
    rows = 512

    def conv(u_ref, w_ref, b_ref, r):
        u = u_ref[r:r + rows, :]
        row = lax.broadcasted_iota(jnp.int32, u.shape, 0)
        before = u_ref[r - 1:r, :] if r > 0 else jnp.zeros_like(u[0:1])
        after = u_ref[r + rows:r + rows + 1, :] if r + rows < DEC_SEQ else jnp.zeros_like(u[0:1])
        prev = jnp.where(row == 0, before, pltpu.roll(u, 1, 0))
        nxt = jnp.where(row == rows - 1, after, pltpu.roll(u, rows - 1, 0))
        return prev * w_ref[0:1, :] + u * w_ref[1:2, :] + nxt * w_ref[2:3, :] + b_ref[...]

    for r in range(0, DEC_SEQ, rows):
        z_ref[r:r + rows, :] = conv(v_ref, wv_ref, bv_ref, r) * conv(x1_ref, w1_ref, b1_ref, r)
    _fft_stage1(z_ref, y_ref, f1_ref[...])
    for k1 in range(FFT_M):
        zr, zi = _fft_stage2(y_ref, g_ref, k1)
        kr = kr_ref[k1 * FFT_H:(k1 + 1) * FFT_H, :]
        ki = ki_ref[k1 * FFT_H:(k1 + 1) * FFT_H, :]
        p = jnp.concatenate([zr * kr - zi * ki, zr * ki + zi * kr], axis=0).astype(BF16)
        u = jnp.dot(gt_ref[k1], p, preferred_element_type=F32)
        y_ref[pl.ds(k1, FFT_R, stride=2 * FFT_M), :] = u[:FFT_R]
        y_ref[pl.ds(FFT_M + k1, FFT_R, stride=2 * FFT_M), :] = u[FFT_R:]
    f1t = f1t_ref[...]
    for b in range(FFT_R):
        yb = jnp.dot(f1t, y_ref[b * 2 * FFT_M:(b + 1) * 2 * FFT_M, :].astype(BF16), preferred_element_type=F32)
        t_ref[pl.ds(b, FFT_A, stride=FFT_R), :] = yb
    for r in range(0, DEC_SEQ, rows):
        y = t_ref[r:r + rows, :] * (1.0 / DEC_SEQ) + z_ref[r:r + rows, :] * skip_ref[...]
        o_ref[r:r + rows, :] = (y * conv(x0_ref, w0_ref, b0_ref, r)).astype(BF16)


def _hy_conv_fft(u, conv_w, conv_b, kr, ki, skip, consts, *, row0, n_seq):
    tc = 128
    nb = D // tc
    r0 = row0 // DEC_SEQ
    f1, f1t, g, gt = consts

    def col(part):
        return [pl.BlockSpec((DEC_SEQ, tc), lambda b, c: (r0 + b, part * nb + c)),
                pl.BlockSpec((3, tc), lambda b, c: (0, part * nb + c)),
                pl.BlockSpec((1, tc), lambda b, c: (0, part * nb + c))]

    specs = [col(p) for p in range(3)]
    chan = pl.BlockSpec((DEC_SEQ, tc), lambda b, c: (0, c))
    const = lambda m: pl.BlockSpec(m.shape, lambda b, c: (0,) * m.ndim)
    cb = conv_b.reshape(1, 3 * D)
    return pl.pallas_call(
        _hy_conv_fft_body,
        out_shape=jax.ShapeDtypeStruct((n_seq * DEC_SEQ, D), BF16),
        grid=(n_seq, nb),
        in_specs=([s[0] for s in specs] + [s[1] for s in specs] + [s[2] for s in specs]
                  + [chan, chan, pl.BlockSpec((1, tc), lambda b, c: (0, c))] + [const(m) for m in consts]),
        out_specs=pl.BlockSpec((DEC_SEQ, tc), lambda b, c: (b, c)),
        scratch_shapes=[pltpu.VMEM((DEC_SEQ, tc), F32), pltpu.VMEM((FFT_R * 2 * FFT_M, tc), F32),
                        pltpu.VMEM((DEC_SEQ, tc), F32)],
        compiler_params=_params(2, 56 * 1024 * 1024),
        name="hyena_conv_fft",
    )(u, u, u, conv_w, conv_w, conv_w, cb, cb, cb, kr, ki, skip.reshape(1, D), f1, f1t, g, gt)


ROW_TILE = D // LANES
ROUTER_TM = 512
EXPERT_TM = 512
N_SLOTS = 2 * N_TOK + N_EXPERTS * EXPERT_TM
N_SLOT_TILES = N_SLOTS // EXPERT_TM
INFO_E1, INFO_E2, INFO_R1, INFO_R2, INFO_W1, INFO_W2 = range(6)


def _to_row_tiles(ref, x):
    rows = x.shape[0]
    for j in range(ROW_TILE):
        ref[pl.ds(j, rows, stride=ROW_TILE), :] = x[:, j * LANES:(j + 1) * LANES]


def _from_row_tiles(ref, rows):
    return jnp.concatenate([ref[pl.ds(j, rows, stride=ROW_TILE), :] for j in range(ROW_TILE)], axis=-1)


def _router_body(x_ref, sh_ref, sc_ref, w_ref, info_ref, incl_ref, cnt_ref):
    @pl.when(pl.program_id(0) == 0)
    def _():
        cnt_ref[...] = jnp.zeros_like(cnt_ref)

    h = _modulate(x_ref, sh_ref, sc_ref)
    logits = jnp.dot(h, w_ref[...], preferred_element_type=F32, precision=HIGHEST)
    lane = lax.broadcasted_iota(jnp.int32, logits.shape, 1).astype(F32)
    logits = jnp.where(lane < N_EXPERTS, logits, -jnp.inf)
    e = jnp.exp(logits - jnp.max(logits, axis=-1, keepdims=True))
    p = e / jnp.sum(e, axis=-1, keepdims=True)
    p1 = jnp.max(p, axis=-1, keepdims=True)
    i1 = jnp.min(jnp.where(p == p1, lane, float(LANES)), axis=-1, keepdims=True)
    rest = jnp.where(lane == i1, -1.0, p)
    p2 = jnp.max(rest, axis=-1, keepdims=True)
    i2 = jnp.min(jnp.where(rest == p2, lane, float(LANES)), axis=-1, keepdims=True)
    total = p1 + p2
    chosen = jnp.where((lane == i1) | (lane == i2), 1.0, 0.0)
    tm = chosen.shape[0]
    earlier = (lax.broadcasted_iota(jnp.int32, (tm, tm), 1) < lax.broadcasted_iota(jnp.int32, (tm, tm), 0))
    rank = jnp.dot(earlier.astype(BF16), chosen.astype(BF16), preferred_element_type=F32) + cnt_ref[...]
    r1 = jnp.sum(jnp.where(lane == i1, rank, 0.0), axis=-1, keepdims=True)
    r2 = jnp.sum(jnp.where(lane == i2, rank, 0.0), axis=-1, keepdims=True)
    cnt_ref[...] += jnp.sum(chosen, axis=0, keepdims=True)
    incl_ref[0] = jnp.broadcast_to(cnt_ref[...], incl_ref.shape[1:])
    info = jnp.zeros_like(p)
    for col, val in ((INFO_E1, i1), (INFO_E2, i2), (INFO_R1, r1), (INFO_R2, r2),
                     (INFO_W1, p1 / total), (INFO_W2, p2 / total)):
        info = jnp.where(lane == col, val, info)
    info_ref[...] = info


def _router(x, sh, sc, w_router):
    tm = ROUTER_TM
    return pl.pallas_call(
        _router_body,
        out_shape=(jax.ShapeDtypeStruct((N_TOK, LANES), F32),
                   jax.ShapeDtypeStruct((N_TOK // tm, 8, LANES), F32)),
        grid=(N_TOK // tm,),
        in_specs=_mod_specs(tm, 1, 0) + [pl.BlockSpec((D, LANES), lambda i: (0, 0))],
        out_specs=(pl.BlockSpec((tm, LANES), lambda i: (i, 0)), pl.BlockSpec((1, 8, LANES), lambda i: (i, 0, 0))),
        scratch_shapes=[pltpu.VMEM((1, LANES), F32)],
        compiler_params=_params(1),
        name="moe_router",
    )(x, sh, sc, jnp.pad(w_router, ((0, 0), (0, LANES - N_EXPERTS))))


def _row_tile(ref, row):
    return ref.at[pl.ds(pl.multiple_of(row * ROW_TILE, ROW_TILE), ROW_TILE)]


def _dispatch_body(pos1_ref, pos2_ref, x_ref, zeros_hbm, xs_hbm, rows_ref, sem, *, tm):
    del zeros_hbm
    base = pl.program_id(0) * tm
    _to_row_tiles(rows_ref, x_ref[...])

    def copies(r):
        src = _row_tile(rows_ref, r)
        return (pltpu.make_async_copy(src, _row_tile(xs_hbm, pos1_ref[base + r]), sem),
                pltpu.make_async_copy(src, _row_tile(xs_hbm, pos2_ref[base + r]), sem))

    def issue(r, carry):
        for queue, cp in enumerate(copies(r)):
            cp.start(priority=queue)
        return carry

    lax.fori_loop(0, tm, issue, 0, unroll=8)
    for _ in range(2):
        pltpu.make_async_copy(rows_ref, xs_hbm.at[pl.ds(0, tm * ROW_TILE)], sem).wait()


def _dispatch(pos1, pos2, x):
    tm = 512
    return pl.pallas_call(
        functools.partial(_dispatch_body, tm=tm),
        out_shape=jax.ShapeDtypeStruct((N_SLOTS * ROW_TILE, LANES), F32),
        grid_spec=pltpu.PrefetchScalarGridSpec(
            num_scalar_prefetch=2, grid=(N_TOK // tm,),
            in_specs=[pl.BlockSpec((tm, D), lambda i, *_: (i, 0)), pl.BlockSpec(memory_space=pl.ANY)],
            out_specs=pl.BlockSpec(memory_space=pl.ANY),
            scratch_shapes=[pltpu.VMEM((tm * ROW_TILE, LANES), F32), pltpu.SemaphoreType.DMA(())]),
        input_output_aliases={3: 0},
        compiler_params=_params(1),
        name="moe_dispatch",
    )(pos1, pos2, x, jnp.zeros((N_SLOTS * ROW_TILE, LANES), F32))


def _expert_swiglu_body(eid_ref, b1_ref, b2_ref, nv_ref, xs_ref, sh_ref, sc_ref, wg_ref, wu_ref, o_ref, *, fc):
    t = pl.program_id(0)

    @pl.when(t < nv_ref[0])
    def _():
        tm = o_ref.shape[0]
        x = _from_row_tiles(xs_ref, tm)
        slot = t * tm + lax.broadcasted_iota(jnp.int32, (tm, 1), 0)
        in1, in2 = slot >= b1_ref[t], slot >= b2_ref[t]
        sc = jnp.where(in2, sc_ref[2], jnp.where(in1, sc_ref[1], sc_ref[0]))
        sh = jnp.where(in2, sh_ref[2], jnp.where(in1, sh_ref[1], sh_ref[0]))
        h = (x * (1.0 + sc) + sh).astype(BF16)
        for c in range(o_ref.shape[1] // fc):
            cs = slice(c * fc, (c + 1) * fc)
            g = jnp.dot(h, wg_ref[0, :, cs], preferred_element_type=F32)
            u = jnp.dot(h, wu_ref[0, :, cs], preferred_element_type=F32)
            o_ref[:, cs] = (g * jax.nn.sigmoid(g) * u).astype(BF16)

    @pl.when(t >= nv_ref[0])
    def _():
        o_ref[...] = jnp.zeros_like(o_ref)


def _expert_swiglu(meta, xs_rt, sh, sc, w_gate, w_up, e0):
    eid, b1, b2, nv = meta
    tm, f = EXPERT_TM, w_gate.shape[2]

    def tile(t, eid, b1, b2, nv):
        return jnp.minimum(t, nv[0] - 1)

    w_spec = pl.BlockSpec((1, D, f), lambda t, eid, b1, b2, nv: (e0 + eid[tile(t, eid, b1, b2, nv)], 0, 0))
    mod = pl.BlockSpec((8, 1, D), lambda t, *_: (0, 0, 0))
    return pl.pallas_call(
        functools.partial(_expert_swiglu_body, fc=512),
        out_shape=jax.ShapeDtypeStruct((N_SLOTS, f), BF16),
        grid_spec=pltpu.PrefetchScalarGridSpec(
            num_scalar_prefetch=4, grid=(N_SLOT_TILES,),
            in_specs=[pl.BlockSpec((tm * ROW_TILE, LANES), lambda t, *m: (tile(t, *m), 0)), mod, mod, w_spec, w_spec],
            out_specs=pl.BlockSpec((tm, f), lambda t, *m: (t, 0))),
        compiler_params=_params(1, 56 * 1024 * 1024),
        name="moe_swiglu",
    )(eid, b1, b2, nv, xs_rt, sh, sc, w_gate, w_up)


def _expert_down_body(eid_ref, nv_ref, a_ref, w_ref, y_ref):
    @pl.when(pl.program_id(0) < nv_ref[0])
    def _():
        _to_row_tiles(y_ref, jnp.dot(a_ref[...], w_ref[0], preferred_element_type=F32))

    @pl.when(pl.program_id(0) >= nv_ref[0])
    def _():
        y_ref[...] = jnp.zeros_like(y_ref)


def _expert_down(meta, act, w_down, e0):
    eid, _, _, nv = meta
    tm, f = EXPERT_TM, act.shape[1]

    def tile(t, eid, nv):
        return jnp.minimum(t, nv[0] - 1)

    return pl.pallas_call(
        _expert_down_body,
        out_shape=jax.ShapeDtypeStruct((N_SLOTS * ROW_TILE, LANES), F32),
        grid_spec=pltpu.PrefetchScalarGridSpec(
            num_scalar_prefetch=2, grid=(N_SLOT_TILES,),
            in_specs=[pl.BlockSpec((tm, f), lambda t, *m: (tile(t, *m), 0)),
                      pl.BlockSpec((1, f, D), lambda t, eid, nv: (e0 + eid[tile(t, eid, nv)], 0, 0))],
            out_specs=pl.BlockSpec((tm * ROW_TILE, LANES), lambda t, *m: (t, 0))),
        compiler_params=_params(1),
        name="moe_down",
    )(eid, nv, act, w_down)


def _combine_body(pos1_ref, pos2_ref, y_hbm, info_ref, x_ref, g_ref, lng_ref, lnb_ref, o_ref, y1_ref, y2_ref, sem,
                  *, tm):
    base = pl.program_id(0) * tm

    def copies(r):
        return (pltpu.make_async_copy(_row_tile(y_hbm, pos1_ref[base + r]), _row_tile(y1_ref, r), sem),
                pltpu.make_async_copy(_row_tile(y_hbm, pos2_ref[base + r]), _row_tile(y2_ref, r), sem))

    def issue(r, carry):
        for queue, cp in enumerate(copies(r)):
            cp.start(priority=queue)
        return carry

    lax.fori_loop(0, tm, issue, 0, unroll=8)
    for y_ref in (y1_ref, y2_ref):
        pltpu.make_async_copy(y_hbm.at[pl.ds(0, tm * ROW_TILE)], y_ref, sem).wait()
    info = info_ref[...]
    ffn = (info[:, INFO_W1:INFO_W1 + 1] * _from_row_tiles(y1_ref, tm)
           + info[:, INFO_W2:INFO_W2 + 1] * _from_row_tiles(y2_ref, tm))
    y = ALPHA * x_ref[...] + g_ref[0] * ffn
    mu = jnp.mean(y, axis=-1, keepdims=True)
    yc = y - mu
    var = jnp.mean(yc * yc, axis=-1, keepdims=True)
    o_ref[...] = yc * lax.rsqrt(var + LN_EPS) * lng_ref[...] + lnb_ref[...]


def _combine(pos1, pos2, y_rt, info, x, gate, ln_g, ln_b):
    tm = 512
    return pl.pallas_call(
        functools.partial(_combine_body, tm=tm),
        out_shape=jax.ShapeDtypeStruct((N_TOK, D), F32),
        grid_spec=pltpu.PrefetchScalarGridSpec(
            num_scalar_prefetch=2, grid=(N_TOK // tm,),
            in_specs=[pl.BlockSpec(memory_space=pl.ANY),
                      pl.BlockSpec((tm, LANES), lambda i, *_: (i, 0)),
                      pl.BlockSpec((tm, D), lambda i, *_: (i, 0)),
                      pl.BlockSpec((1, 1, D), lambda i, *_: (_group_of_row(i * tm), 0, 0)),
                      pl.BlockSpec((1, D), lambda i, *_: (0, 0)),
                      pl.BlockSpec((1, D), lambda i, *_: (0, 0))],
            out_specs=pl.BlockSpec((tm, D), lambda i, *_: (i, 0)),
            scratch_shapes=[pltpu.VMEM((tm * ROW_TILE, LANES), F32), pltpu.VMEM((tm * ROW_TILE, LANES), F32),
                            pltpu.SemaphoreType.DMA(())]),
        compiler_params=_params(1),
        name="moe_combine",
    )(pos1, pos2, y_rt, info, x, gate, ln_g.reshape(1, D), ln_b.reshape(1, D))


def _slot_plan(info, incl):
    row = lambda n_rows: incl[n_rows // ROUTER_TM - 1, 0, :N_EXPERTS].astype(jnp.int32)
    count = row(N_TOK)
    padded = (count + EXPERT_TM - 1) // EXPERT_TM * EXPERT_TM
    end = jnp.cumsum(padded)
    start = end - padded
    tile_row = jnp.arange(N_SLOT_TILES, dtype=jnp.int32) * EXPERT_TM
    eid = jnp.minimum(jnp.sum(tile_row[:, None] >= end[None, :], axis=1), N_EXPERTS - 1).astype(jnp.int32)
    b1 = (start + row(N_CTX))[eid]
    b2 = (start + row(N_CTX + DEC_SEQ))[eid]
    nv = (end[-1:] // EXPERT_TM).astype(jnp.int32)
    e1, e2 = info[:, INFO_E1].astype(jnp.int32), info[:, INFO_E2].astype(jnp.int32)
    pos1 = start[e1] + info[:, INFO_R1].astype(jnp.int32)
    pos2 = start[e2] + info[:, INFO_R2].astype(jnp.int32)
    return pos1, pos2, (eid, b1, b2, nv)


def _even_mixer(x, sh, sc, gate, ln_g, ln_b, w_in, b_igate, b_fgate, ml_norm_g, q_norm_g, k_norm_g, w_out,
                st_c, st_n, st_m, cache_k, cache_v, rope_tabs):
    splits = (4 * ML_W, 4 * ML_W + N_GATES)
    w_main = jnp.concatenate([w_in[:, :splits[0]], w_in[:, splits[1]:]], axis=1).astype(BF16)
    proj = _mod_matmul(x, sh, sc, [w_main[None]], tm=1024, tn=MAIN_W // 2, out_dtype=F32, name="even_in_proj")[0]
    b_gate = jnp.stack([b_igate, b_fgate], axis=1).reshape(N_GATES)
    lic, bc, lir, br = _gates(x, sh, sc, w_in[:, splits[0]:splits[1]], b_gate)

    hf_c, hb_c, new_c, new_n, new_m = _mlstm(proj, lic, bc, lir, br, row0=0, n_seq=BATCH, seq_len=SEQ)
    init = (st_c, st_n, jnp.broadcast_to(st_m[..., None], st_n.shape))
    hf_s, hb_s, _, _, _ = _mlstm(proj, lic, bc, lir, br, row0=N_CTX, n_seq=DEC_BATCH, seq_len=DEC_SEQ, init=init)
    ml = _ml_post((hf_c, hf_s), (hb_c, hb_s), proj, ml_norm_g)

    q_c, kn_c, kb_c = _qk_prep(proj, q_norm_g, k_norm_g, row0=0, rows=N_CTX)
    q_s, _, kb_s = _qk_prep(proj, q_norm_g, k_norm_g, row0=N_CTX, rows=N_LAT, rope_tabs=rope_tabs)
    v_all = proj[:, MAIN_W - KV_W:]
    v_c, v_s = v_all[:N_CTX], v_all[N_CTX:]
    att_c = _attention(q_c, _rep_heads(kb_c, BATCH), _rep_heads(v_c.astype(BF16), BATCH),
                       n_seq=BATCH, seq_len=SEQ, tq=SEQ)
    k_lat = jnp.concatenate([kb_s.reshape(DEC_BATCH, DEC_SEQ, KV_W),
                             cache_k.reshape(DEC_BATCH, PAST_LEN, KV_W).astype(BF16)], axis=1)
    v_lat = jnp.concatenate([v_s.reshape(DEC_BATCH, DEC_SEQ, KV_W).astype(BF16),
                             cache_v.reshape(DEC_BATCH, PAST_LEN, KV_W).astype(BF16)], axis=1)
    att_s = _attention(q_s, _rep_heads(k_lat.reshape(-1, KV_W), DEC_BATCH), _rep_heads(v_lat.reshape(-1, KV_W), DEC_BATCH),
                       n_seq=DEC_BATCH, seq_len=DEC_SEQ, tq=256)

    x = _proj_res_ln([ml, (att_c, att_s)], w_out.astype(BF16), x, gate, ln_g, ln_b, tm=512, name="even_out_proj")
    new_k = kn_c.reshape(BATCH, SEQ, ATT_KV_HEADS, ATT_HEAD_DIM)
    new_v = v_c.reshape(BATCH, SEQ, ATT_KV_HEADS, ATT_HEAD_DIM)
    return x, new_k, new_v, new_c, new_n, new_m[..., 0]


def _hyena_mixer(x, sh, sc, gate, ln_g, ln_b, w_in, conv_w, conv_b, w1, b1, w2, b2, w3, sin_freq, skip, w_out, dft):
    u = _mod_matmul(x, sh, sc, [w_in.astype(BF16)[None]], tm=1024, tn=512, out_dtype=F32, name="hyena_in_proj")[0]
    mats, consts = dft
    tiles = dict(tf=SEQ, tk=SEQ, tn=D)
    h_sum, h_diff, nyq = _hy_filter(SEQ, w1, b1, w2, b2, w3, sin_freq)
    kr, ki = _dft_fwd(mats, h_sum, h_diff, row0=0, n_seq=1, seq_len=SEQ, **tiles)
    kr, ki = kr[0], ki[0].at[0].set(nyq[0])
    z, x0 = _hy_gate(u, conv_w, conv_b, row0=0, n_seq=BATCH, seq_len=SEQ, seqs_per_step=8)
    yr, yi = _dft_fwd(mats, z, z, row0=0, n_seq=BATCH, seq_len=SEQ, filt=(kr, ki), **tiles)
    y_c = _dft_inv(mats, yr, yi, z, x0, skip, row0=0, n_seq=BATCH, seq_len=SEQ, tm=SEQ, tk=SEQ, tn=D)
    h_sum, h_diff, _ = _hy_filter(DEC_SEQ, w1, b1, w2, b2, w3, sin_freq)
    kr, ki = _hy_spectrum(h_sum, h_diff, consts)
    y_s = _hy_conv_fft(u, conv_w, conv_b, kr, ki, skip, consts, row0=N_CTX, n_seq=DEC_BATCH)
    return _proj_res_ln([(y_c, y_s)], w_out.astype(BF16), x, gate, ln_g, ln_b, tm=512, name="hyena_out_proj")


def _dense_ffn(x, sh, sc, gate, ln_g, ln_b, w_gate, w_up, w_down):
    act = _mod_matmul(x, sh, sc, [w_gate.astype(BF16)[None], w_up.astype(BF16)[None]],
                      tm=1024, tn=D_FF // 2, out_dtype=BF16, name="ffn_swiglu")
    return _proj_res_ln([act[0]], w_down.astype(BF16), x, gate, ln_g, ln_b, tm=512, name="ffn_down")


def _moe_ffn(x, sh, sc, gate, ln_g, ln_b, w_router, w_gate, w_up, w_down, e0):
    info, incl = _router(x, sh, sc, w_router)
    pos1, pos2, meta = _slot_plan(info, incl)
    xs_rt = _dispatch(pos1, pos2, x)
    act = _expert_swiglu(meta, xs_rt, sh, sc, w_gate, w_up, e0)
    y_rt = _expert_down(meta, act, w_down, e0)
    return _combine(pos1, pos2, y_rt, info, x, gate, ln_g, ln_b)


def kernel(x_prompt, x_sample, cache_attn_k, cache_attn_v, state_mlstm_C, state_mlstm_n, state_mlstm_m, c, c_ctx, w_ada, b_ada, ln_g, ln_b, w_in_even, b_igate, b_fgate, ml_norm_g, q_norm_g, k_norm_g, w_out_even, w_ffn_gate, w_ffn_up, w_ffn_down, w_in_hy, hy_conv_w, hy_conv_b, hy_filt_w1, hy_filt_b1, hy_filt_w2, hy_filt_b2, hy_filt_w3, hy_sin_freq, hy_skip, w_out_hy, w_router, w_moe_gate, w_moe_up, w_moe_down):
    x = jnp.concatenate([x_prompt.reshape(N_CTX, D), x_sample.reshape(N_LAT, D)])
    cvec = jnp.concatenate([c_ctx[None], c, jnp.zeros((8 - 1 - DEC_BATCH, D), F32)])
    mods = _ada(cvec, w_ada, b_ada)
    rope_tabs = _rope_tables()
    dft = (_dft_matrices(SEQ), _fft_consts())
    moe_w = [w.astype(BF16).reshape((-1,) + w.shape[2:]) for w in (w_moe_gate, w_moe_up, w_moe_down)]
    new_k, new_v, new_c, new_n, new_m = [], [], [], [], []
    for layer in range(DEPTH):
        sh1, sc1, g1, sh2, sc2, g2 = (mods[layer, :, i * D:(i + 1) * D].reshape(8, 1, D) for i in range(6))
        i = layer // 2
        if layer % 2 == 0:
            x, k_c, v_c, st_c, st_n, st_m = _even_mixer(
                x, sh1, sc1, g1, ln_g[layer, 0], ln_b[layer, 0], w_in_even[i], b_igate[i], b_fgate[i], ml_norm_g[i],
                q_norm_g[i], k_norm_g[i], w_out_even[i], state_mlstm_C[:, i], state_mlstm_n[:, i], state_mlstm_m[:, i],
                cache_attn_k[:, i], cache_attn_v[:, i], rope_tabs)
            new_k.append(k_c)
            new_v.append(v_c)
            new_c.append(st_c)
            new_n.append(st_n)
            new_m.append(st_m)
            x = _dense_ffn(x, sh2, sc2, g2, ln_g[layer, 1], ln_b[layer, 1], w_ffn_gate[i], w_ffn_up[i], w_ffn_down[i])
        else:
            x = _hyena_mixer(x, sh1, sc1, g1, ln_g[layer, 0], ln_b[layer, 0], w_in_hy[i], hy_conv_w[i], hy_conv_b[i],
                             hy_filt_w1[i], hy_filt_b1[i], hy_filt_w2[i], hy_filt_b2[i], hy_filt_w3[i], hy_sin_freq[i],
                             hy_skip[i], w_out_hy[i], dft)
            x = _moe_ffn(x, sh2, sc2, g2, ln_g[layer, 1], ln_b[layer, 1], w_router[i], *moe_w, i * N_EXPERTS)
    return (x[:N_CTX].reshape(BATCH, SEQ, D), x[N_CTX:].reshape(DEC_BATCH, DEC_SEQ, D),
            jnp.stack(new_k, axis=1), jnp.stack(new_v, axis=1), jnp.stack(new_c, axis=1),
            jnp.stack(new_n, axis=1), jnp.stack(new_m, axis=1))
```

```python
import functools
import math

import jax
import jax.numpy as jnp
import numpy as np
from jax import lax
from jax.experimental import pallas as pl
from jax.experimental.pallas import tpu as pltpu

F32 = jnp.float32
BF16 = jnp.bfloat16
HIGHEST = lax.Precision.HIGHEST

D = 1024
BATCH, SEQ = 32, 256
DEC_BATCH, DEC_SEQ = 2, 4096
DEPTH = 4
PAST_LEN = 256
GRID_W = 64
N_CTX = BATCH * SEQ
N_LAT = DEC_BATCH * DEC_SEQ
N_TOK = N_CTX + N_LAT

ML_HEADS, ML_HEAD_DIM = 4, 128
ML_W = ML_HEADS * ML_HEAD_DIM
CHUNK = 128
ATT_HEADS, ATT_KV_HEADS, ATT_HEAD_DIM = 8, 2, 64
ATT_GROUP = ATT_HEADS // ATT_KV_HEADS
ATT_W = ATT_HEADS * ATT_HEAD_DIM
KV_W = ATT_KV_HEADS * ATT_HEAD_DIM
GROUP_W = ATT_GROUP * ATT_HEAD_DIM
ROPE_BASE = 10000.0
N_GATES = 4 * ML_HEADS
MAIN_W = 4 * ML_W + ATT_W + 2 * KV_W

HY_EMB = 33
HY_BANDS = (HY_EMB - 1) // 2
HY_TARGET, HY_SHORT_PCT, HY_LONG_PCT = 1e-2, 0.3, 1.5
D_FF = 2816
N_EXPERTS = 8
MOE_D_FF = 3584
ALPHA = (2 * DEPTH) ** 0.25
LN_EPS = 1e-5
RMS_EPS = 1e-6

LANES = 128
VMEM_LIMIT = 48 * 1024 * 1024


def _params(n_axes, vmem=VMEM_LIMIT):
    return pltpu.CompilerParams(dimension_semantics=("arbitrary",) * n_axes, vmem_limit_bytes=vmem)


def _group_of_row(r):
    return jnp.where(r < N_CTX, 0, 1 + (r - N_CTX) // DEC_SEQ)


def _modulate(x_ref, sh_ref, sc_ref):
    return x_ref[...] * (1.0 + sc_ref[0]) + sh_ref[0]


def _mod_specs(tm, n_axes, row_axis):
    def rows(*ids):
        return (ids[row_axis], 0)

    def grp(*ids):
        return (_group_of_row(ids[row_axis] * tm), 0, 0)

    del n_axes
    return [pl.BlockSpec((tm, D), rows), pl.BlockSpec((1, 1, D), grp), pl.BlockSpec((1, 1, D), grp)]


def _ada_body(c_ref, w_ref, b_ref, o_ref):
    c = c_ref[...]
    s = c * jax.nn.sigmoid(c)
    o_ref[0] = jnp.dot(s, w_ref[0], preferred_element_type=F32, precision=HIGHEST) + b_ref[0]


def _ada(cvec, w_ada, b_ada):
    tn = 1536
    return pl.pallas_call(
        _ada_body,
        out_shape=jax.ShapeDtypeStruct((DEPTH, 8, 6 * D), F32),
        grid=(DEPTH, 6 * D // tn),
        in_specs=[pl.BlockSpec((8, D), lambda l, j: (0, 0)),
                  pl.BlockSpec((1, D, tn), lambda l, j: (l, 0, j)),
                  pl.BlockSpec((1, 1, tn), lambda l, j: (l, 0, j))],
        out_specs=pl.BlockSpec((1, 8, tn), lambda l, j: (l, 0, j)),
        compiler_params=_params(2),
        name="ada_modulation",
    )(cvec, w_ada, b_ada.reshape(DEPTH, 1, 6 * D))


def _mod_mm_body(x_ref, sh_ref, sc_ref, *refs, n_w):
    w_refs, o_ref, h_ref = refs[:n_w], refs[n_w], refs[n_w + 1]

    @pl.when(pl.program_id(2) == 0)
    def _():
        h_ref[...] = _modulate(x_ref, sh_ref, sc_ref).astype(BF16)

    h = h_ref[...]
    if n_w == 1:
        o = jnp.dot(h, w_refs[0][0], preferred_element_type=F32)
    else:
        g = jnp.dot(h, w_refs[0][0], preferred_element_type=F32)
        u = jnp.dot(h, w_refs[1][0], preferred_element_type=F32)
        o = g * jax.nn.sigmoid(g) * u
    o_ref[0] = o.astype(o_ref.dtype)


def _mod_matmul(x, sh, sc, ws, *, tm, tn, out_dtype, name):
    n_e, _, f = ws[0].shape
    return pl.pallas_call(
        functools.partial(_mod_mm_body, n_w=len(ws)),
        out_shape=jax.ShapeDtypeStruct((n_e, N_TOK, f), out_dtype),
        grid=(n_e, N_TOK // tm, f // tn),
        in_specs=_mod_specs(tm, 3, 1) + [pl.BlockSpec((1, D, tn), lambda e, i, j: (e, 0, j)) for _ in ws],
        out_specs=pl.BlockSpec((1, tm, tn), lambda e, i, j: (e, i, j)),
        scratch_shapes=[pltpu.VMEM((tm, D), BF16)],
        compiler_params=_params(3),
        name=name,
    )(x, sh, sc, *ws)


def _split_specs(tm, width):
    nc = N_CTX // tm
    return [pl.BlockSpec((tm, width), lambda i: (jnp.minimum(i, nc - 1), 0)),
            pl.BlockSpec((tm, width), lambda i: (jnp.maximum(i - nc, 0), 0))]


def _pick_split(ctx_ref, lat_ref):
    tm = ctx_ref.shape[0]
    return jnp.where(pl.program_id(0) < N_CTX // tm, ctx_ref[...], lat_ref[...])


def _proj_res_ln_body(*refs, split):
    n_in = sum(2 if s else 1 for s in split)
    part_refs = list(refs[:n_in])
    w_ref, x_ref, g_ref, lng_ref, lnb_ref, o_ref = refs[n_in:]
    cols = [_pick_split(part_refs.pop(0), part_refs.pop(0)) if s else part_refs.pop(0)[...] for s in split]
    a = cols[0] if len(cols) == 1 else jnp.concatenate(cols, axis=-1)
    y = ALPHA * x_ref[...] + g_ref[0] * jnp.dot(a, w_ref[...], preferred_element_type=F32)
    mu = jnp.mean(y, axis=-1, keepdims=True)
    yc = y - mu
    var = jnp.mean(yc * yc, axis=-1, keepdims=True)
    o_ref[...] = yc * lax.rsqrt(var + LN_EPS) * lng_ref[...] + lnb_ref[...]


def _proj_res_ln(parts, w, x, gate, ln_g, ln_b, *, tm, name):
    split = tuple(isinstance(p, tuple) for p in parts)
    in_specs, args = [], []
    for p, s in zip(parts, split):
        if s:
            in_specs += _split_specs(tm, p[0].shape[1])
            args += list(p)
        else:
            in_specs.append(pl.BlockSpec((tm, p.shape[1]), lambda i: (i, 0)))
            args.append(p)
    in_specs += [pl.BlockSpec(w.shape, lambda i: (0, 0)),
                 pl.BlockSpec((tm, D), lambda i: (i, 0)),
                 pl.BlockSpec((1, 1, D), lambda i: (_group_of_row(i * tm), 0, 0)),
                 pl.BlockSpec((1, D), lambda i: (0, 0)),
                 pl.BlockSpec((1, D), lambda i: (0, 0))]
    return pl.pallas_call(
        functools.partial(_proj_res_ln_body, split=split),
        out_shape=jax.ShapeDtypeStruct((N_TOK, D), F32),
        grid=(N_TOK // tm,),
        in_specs=in_specs,
        out_specs=pl.BlockSpec((tm, D), lambda i: (i, 0)),
        compiler_params=_params(1),
        name=name,
    )(*args, w, x, gate, ln_g.reshape(1, D), ln_b.reshape(1, D))


def _log_sigmoid(x):
    return jnp.minimum(x, 0.0) - jnp.log(1.0 + jnp.exp(-jnp.abs(x)))


def _gates_body(x_ref, sh_ref, sc_ref, wg_ref, wgt_ref, b_ref, bt_ref,
                lic_ref, bc_ref, lir_ref, br_ref, *, tm):
    h = _modulate(x_ref, sh_ref, sc_ref).astype(BF16)
    g = jnp.dot(h, wg_ref[...], preferred_element_type=F32) + b_ref[...]
    gt = lax.dot_general(wgt_ref[...], h, (((1,), (1,)), ((), ())), preferred_element_type=F32) + bt_ref[...]
    lic_ref[...] = g
    lir_ref[...] = gt
    lf, lft = _log_sigmoid(g), _log_sigmoid(gt)
    r = lax.broadcasted_iota(jnp.int32, (CHUNK, CHUNK), 0)
    c = lax.broadcasted_iota(jnp.int32, (CHUNK, CHUNK), 1)
    tri_l = (c <= r).astype(F32)
    tri_u = (c >= r).astype(F32)
    fwd_col = lax.broadcasted_iota(jnp.int32, (CHUNK, LANES), 1) < 2 * ML_HEADS
    fwd_row = lax.broadcasted_iota(jnp.int32, (N_GATES, CHUNK), 0) < 2 * ML_HEADS
    for ch in range(tm // CHUNK):
        sl = slice(ch * CHUNK, (ch + 1) * CHUNK)
        lfc, lftc = lf[sl, :], lft[:, sl]
        cum_f = jnp.dot(tri_l, lfc, preferred_element_type=F32, precision=HIGHEST)
        cum_b = jnp.dot(tri_u, lfc, preferred_element_type=F32, precision=HIGHEST)
        bc_ref[sl, :] = jnp.where(fwd_col, cum_f, cum_b)
        cum_f = jnp.dot(lftc, tri_u, preferred_element_type=F32, precision=HIGHEST)
        cum_b = jnp.dot(lftc, tri_l, preferred_element_type=F32, precision=HIGHEST)
        br_ref[:, sl] = jnp.where(fwd_row, cum_f, cum_b)


def _gates(x, sh, sc, wg, b_gate):
    tm = 256
    wg_pad = jnp.pad(wg, ((0, 0), (0, LANES - N_GATES)))
    b_pad = jnp.pad(b_gate, (0, LANES - N_GATES)).reshape(1, LANES)
    col = pl.BlockSpec((tm, LANES), lambda i: (i, 0))
    row = pl.BlockSpec((N_GATES, tm), lambda i: (0, i))
    return pl.pallas_call(
        functools.partial(_gates_body, tm=tm),
        out_shape=(jax.ShapeDtypeStruct((N_TOK, LANES), F32), jax.ShapeDtypeStruct((N_TOK, LANES), F32),
                   jax.ShapeDtypeStruct((N_GATES, N_TOK), F32), jax.ShapeDtypeStruct((N_GATES, N_TOK), F32)),
        grid=(N_TOK // tm,),
        in_specs=_mod_specs(tm, 1, 0) + [pl.BlockSpec((D, LANES), lambda i: (0, 0)),
                                         pl.BlockSpec((N_GATES, D), lambda i: (0, 0)),
                                         pl.BlockSpec((1, LANES), lambda i: (0, 0)),
                                         pl.BlockSpec((N_GATES, 1), lambda i: (0, 0))],
        out_specs=(col, col, row, row),
        compiler_params=_params(1),
        name="mlstm_gates",
    )(x, sh, sc, wg_pad.astype(BF16), wg.T.astype(BF16), b_pad, b_gate.reshape(N_GATES, 1))


def _mlstm_body(*refs, has_init):
    (qf, kf, vf, licf, bcf, lirf, brf, qb, kb, vb, licb, bcb, lirb, brb) = refs[:14]
    refs = refs[14:]
    if has_init:
        c0_ref, n0_ref, m0_ref = refs[:3]
        refs = refs[3:]
    hf_ref, hb_ref, c_ref, n_ref, m_ref = refs

    @pl.when(pl.program_id(1) == 0)
    def _():
        if has_init:
            c_ref[...] = c0_ref[...]
            n_ref[...] = n0_ref[...]
            m_ref[...] = m0_ref[...]
        else:
            c_ref[...] = jnp.zeros_like(c_ref)
            n_ref[...] = jnp.zeros_like(n_ref)
            m_ref[...] = jnp.zeros_like(m_ref)

    t_idx = lax.broadcasted_iota(jnp.int32, (CHUNK, CHUNK), 0)
    s_idx = lax.broadcasted_iota(jnp.int32, (CHUNK, CHUNK), 1)
    nt = (((1,), (1,)), ((), ()))
    stores = []

    def chain(d, h, q_ref, k_ref, v_ref, lic_ref, bc_ref, lir_ref, br_ref, h_ref):
        mask = (s_idx <= t_idx) if d == 0 else (s_idx >= t_idx)
        hs = slice(h * ML_HEAD_DIM, (h + 1) * ML_HEAD_DIM)
        gi, gf = d * 2 * ML_HEADS + h, d * 2 * ML_HEADS + ML_HEADS + h
        q = q_ref[:, hs]
        k = k_ref[:, hs] * (ML_HEAD_DIM ** -0.5)
        v = v_ref[:, hs]
        qh, kh, vh = q.astype(BF16), k.astype(BF16), v.astype(BF16)
        li_c, b_c = lic_ref[:, gi:gi + 1], bc_ref[:, gf:gf + 1]
        li_r, b_r = lir_ref[gi:gi + 1, :], br_ref[gf:gf + 1, :]
        c_st = c_ref[0, d, h]
        n_st = n_ref[0, d, h:h + 1, :]
        m_st = m_ref[0, d, h:h + 1, :][:, 0:1]
        dmat = jnp.where(mask, b_c - b_r + li_r, -jnp.inf)
        inter = b_c + m_st
        m_out = jnp.maximum(inter, jnp.max(dmat, axis=-1, keepdims=True))
        p = jnp.exp(dmat - m_out)
        w_inter = jnp.exp(inter - m_out)
        yield
        qk = lax.dot_general(qh, kh, nt, preferred_element_type=F32)
        qc = jnp.dot(qh, c_st.astype(BF16), preferred_element_type=F32)
        yield
        s = qk * p
        den = (jnp.sum(s, axis=-1, keepdims=True)
               + w_inter * jnp.sum(q * n_st, axis=-1, keepdims=True))
        sh = s.astype(BF16)
        b_last = b_r[:, CHUNK - 1:CHUNK] if d == 0 else b_r[:, 0:1]
        g_r = b_last - b_r + li_r
        g_c = b_last - b_c + li_c
        m_new = jnp.maximum(b_last + m_st, jnp.max(g_r, axis=-1, keepdims=True))
        decay = jnp.exp(b_last + m_st - m_new)
        kw = k * jnp.exp(g_c - m_new)
        kwt = kw.T.astype(BF16)
        yield
        sv = jnp.dot(sh, vh, preferred_element_type=F32)
        kv = jnp.dot(kwt, vh, preferred_element_type=F32)
        yield
        h_out = (sv + w_inter * qc) / jnp.maximum(jnp.abs(den), jnp.exp(-m_out))
        c_new = decay * c_st + kv
        n_new = decay * n_st + jnp.sum(kw, axis=0, keepdims=True)
        stores.append((h_ref, d, h, hs, h_out, c_new, n_new, jnp.broadcast_to(m_new, (1, ML_HEAD_DIM))))
        yield

    chains = [chain(d, h, *group)
              for d, group in enumerate(((qf, kf, vf, licf, bcf, lirf, brf, hf_ref),
                                         (qb, kb, vb, licb, bcb, lirb, brb, hb_ref)))
              for h in range(ML_HEADS)]
    for _ in range(5):
        for ch in chains:
            next(ch)
    for h_ref, d, h, hs, h_out, c_new, n_new, m_new in stores:
        h_ref[:, hs] = h_out
        c_ref[0, d, h] = c_new
        n_ref[0, d, h:h + 1, :] = n_new
        m_ref[0, d, h:h + 1, :] = m_new


def _mlstm(proj, lic, bc, lir, br, *, row0, n_seq, seq_len, init=None):
    nc = seq_len // CHUNK
    base = row0 // CHUNK

    def fwd(b, j):
        return base + b * nc + j

    def bwd(b, j):
        return base + b * nc + (nc - 1 - j)

    def chunk_specs(pos):
        return ([pl.BlockSpec((CHUNK, ML_W), lambda b, j, c=c: (pos(b, j), c)) for c in range(3)]
                + [pl.BlockSpec((CHUNK, LANES), lambda b, j: (pos(b, j), 0))] * 2
                + [pl.BlockSpec((N_GATES, CHUNK), lambda b, j: (0, pos(b, j)))] * 2)

    st_c = pl.BlockSpec((1, 2, ML_HEADS, ML_HEAD_DIM, ML_HEAD_DIM), lambda b, j: (b, 0, 0, 0, 0))
    st_n = pl.BlockSpec((1, 2, ML_HEADS, ML_HEAD_DIM), lambda b, j: (b, 0, 0, 0))
    in_specs = chunk_specs(fwd) + chunk_specs(bwd)
    args = [proj, proj, proj, lic, bc, lir, br] * 2
    if init is not None:
        in_specs += [st_c, st_n, st_n]
        args += list(init)
    rows = n_seq * seq_len
    return pl.pallas_call(
        functools.partial(_mlstm_body, has_init=init is not None),
        out_shape=(jax.ShapeDtypeStruct((rows, ML_W), F32), jax.ShapeDtypeStruct((rows, ML_W), F32),
                   jax.ShapeDtypeStruct((n_seq, 2, ML_HEADS, ML_HEAD_DIM, ML_HEAD_DIM), F32),
                   jax.ShapeDtypeStruct((n_seq, 2, ML_HEADS, ML_HEAD_DIM), F32),
                   jax.ShapeDtypeStruct((n_seq, 2, ML_HEADS, ML_HEAD_DIM), F32)),
        grid=(n_seq, nc),
        in_specs=in_specs,
        out_specs=(pl.BlockSpec((CHUNK, ML_W), lambda b, j: (b * nc + j, 0)),
                   pl.BlockSpec((CHUNK, ML_W), lambda b, j: (b * nc + (nc - 1 - j), 0)),
                   st_c, st_n, st_n),
        compiler_params=_params(2),
        name="mlstm_scan",
    )(*args)


def _ml_post_body(hfc_ref, hfs_ref, hbc_ref, hbs_ref, o_ref, g_ref, out_ref):
    h = _pick_split(hfc_ref, hfs_ref) + _pick_split(hbc_ref, hbs_ref)
    gate = jax.nn.sigmoid(o_ref[...]) * g_ref[...]
    for hd in range(ML_HEADS):
        hs = slice(hd * ML_HEAD_DIM, (hd + 1) * ML_HEAD_DIM)
        x = h[:, hs]
        xc = x - jnp.mean(x, axis=-1, keepdims=True)
        var = jnp.mean(xc * xc, axis=-1, keepdims=True)
        out_ref[:, hs] = (gate[:, hs] * (xc * lax.rsqrt(var + RMS_EPS))).astype(BF16)


def _ml_post(hf, hb, proj, norm_g):
    tm = 512
    blk = pl.BlockSpec((tm, ML_W), lambda i: (i, 0))
    return pl.pallas_call(
        _ml_post_body,
        out_shape=jax.ShapeDtypeStruct((N_TOK, ML_W), BF16),
        grid=(N_TOK // tm,),
        in_specs=(_split_specs(tm, ML_W) * 2
                  + [pl.BlockSpec((tm, ML_W), lambda i: (i, 3)), pl.BlockSpec((1, ML_W), lambda i: (0, 0))]),
        out_specs=blk,
        compiler_params=_params(1),
        name="mlstm_out_norm",
    )(*hf, *hb, proj, norm_g.reshape(1, ML_W))


def _head_rms(x, gain):
    lane_head = lax.broadcasted_iota(jnp.int32, x.shape, 1) // ATT_HEAD_DIM
    sq = x * x
    ms = jnp.zeros_like(x)
    for hd in range(x.shape[1] // ATT_HEAD_DIM):
        sel = lane_head == hd
        ms = jnp.where(sel, jnp.sum(jnp.where(sel, sq, 0.0), axis=-1, keepdims=True), ms)
    return x * lax.rsqrt(ms * (1.0 / ATT_HEAD_DIM) + RMS_EPS) * gain


def _rope(x, cos, sin_signed):
    w = x.shape[1]
    even = lax.broadcasted_iota(jnp.int32, x.shape, 1) % 2 == 0
    partner = jnp.where(even, pltpu.roll(x, w - 1, 1), pltpu.roll(x, 1, 1))
    return x * cos + partner * sin_signed


def _qk_prep_body(q_ref, k_ref, qg_ref, kg_ref, *refs, rope):
    if rope:
        cq_ref, sq_ref, ck_ref, sk_ref, qo_ref, kn_ref, kr_ref = refs
    else:
        qo_ref, kn_ref, kr_ref = refs
    q = _head_rms(q_ref[...], qg_ref[...])
    k = _head_rms(k_ref[...], kg_ref[...])
    kn_ref[...] = k
    if rope:
        q = _rope(q, cq_ref[...], sq_ref[...])
        k = _rope(k, ck_ref[...], sk_ref[...])
    qo_ref[...] = (q * (ATT_HEAD_DIM ** -0.5)).astype(BF16)
    kr_ref[...] = k.astype(BF16)


def _qk_prep(proj, q_gain, k_gain, *, row0, rows, rope_tabs=None):
    tm = 512
    r0 = row0 // tm
    in_specs = [pl.BlockSpec((tm, ATT_W), lambda i: (r0 + i, 4 * ML_W // ATT_W)),
                pl.BlockSpec((tm, KV_W), lambda i: (r0 + i, (4 * ML_W + ATT_W) // KV_W)),
                pl.BlockSpec((1, ATT_W), lambda i: (0, 0)),
                pl.BlockSpec((1, KV_W), lambda i: (0, 0))]
    args = [proj, proj, jnp.tile(q_gain, ATT_HEADS).reshape(1, ATT_W), jnp.tile(k_gain, ATT_KV_HEADS).reshape(1, KV_W)]
    if rope_tabs is not None:
        per_seq = DEC_SEQ // tm
        in_specs += [pl.BlockSpec((tm, ATT_W), lambda i: (i % per_seq, 0))] * 2
        in_specs += [pl.BlockSpec((tm, KV_W), lambda i: (i % per_seq, 0))] * 2
        args += list(rope_tabs)
    return pl.pallas_call(
        functools.partial(_qk_prep_body, rope=rope_tabs is not None),
        out_shape=(jax.ShapeDtypeStruct((rows, ATT_W), BF16), jax.ShapeDtypeStruct((rows, KV_W), F32),
                   jax.ShapeDtypeStruct((rows, KV_W), BF16)),
        grid=(rows // tm,),
        in_specs=in_specs,
        out_specs=(pl.BlockSpec((tm, ATT_W), lambda i: (i, 0)), pl.BlockSpec((tm, KV_W), lambda i: (i, 0)),
                   pl.BlockSpec((tm, KV_W), lambda i: (i, 0))),
        compiler_params=_params(1),
        name="attn_qk_prep",
    )(*args)


def _rope_tables():
    rows = DEC_SEQ // GRID_W
    axis_dim = ATT_HEAD_DIM // 2
    row = jnp.repeat(jnp.arange(rows, dtype=F32), GRID_W)
    col = (jnp.arange(DEC_SEQ) % GRID_W).astype(F32)
    inv = ROPE_BASE ** (-jnp.arange(axis_dim // 2, dtype=F32) * 2.0 / axis_dim)
    ang = jnp.concatenate([row[:, None] * inv, col[:, None] * inv], axis=-1)
    cos = jnp.repeat(jnp.cos(ang), 2, axis=-1)
    sin = jnp.repeat(jnp.sin(ang), 2, axis=-1) * jnp.tile(jnp.array([-1.0, 1.0], F32), axis_dim)
    return (jnp.tile(cos, (1, ATT_HEADS)), jnp.tile(sin, (1, ATT_HEADS)),
            jnp.tile(cos, (1, ATT_KV_HEADS)), jnp.tile(sin, (1, ATT_KV_HEADS)))


def _attn_body(q_ref, k_ref, v_ref, o_ref):
    g, tq, dh = q_ref.shape[1:]
    k, v = k_ref[0, 0], v_ref[0, 0]
    pair = 2

    def heads(h0):
        q = q_ref[0, h0:h0 + pair].reshape(pair * tq, dh)
        s = jnp.dot(q, k, preferred_element_type=F32)
        yield
        e = jnp.exp((s - jnp.max(s, axis=-1, keepdims=True)).astype(BF16))
        yield
        o = jnp.dot(e, v, preferred_element_type=F32)
        o_ref[0, h0:h0 + pair] = (o[:, :dh] / o[:, dh:dh + 1]).astype(BF16).reshape(pair, tq, dh)
        yield

    chains = [heads(h0) for h0 in range(0, g, pair)]
    for step in range(len(chains) + 2):
        for i, chain in enumerate(chains):
            if 0 <= step - i < 3:
                next(chain)


def _attention(q, k_t, v_ones, *, tq):
    n_seq, _, seq_len, dh = q.shape
    s_len = k_t.shape[3]
    qo = pl.BlockSpec((1, ATT_GROUP, tq, dh), lambda b, kh, i: (b, kh, i, 0))
    return pl.pallas_call(
        _attn_body,
        out_shape=jax.ShapeDtypeStruct(q.shape, BF16),
        grid=(n_seq, ATT_KV_HEADS, seq_len // tq),
        in_specs=[qo, pl.BlockSpec((1, 1, dh, s_len), lambda b, kh, i: (b, kh, 0, 0)),
                  pl.BlockSpec((1, 1, s_len, LANES), lambda b, kh, i: (b, kh, 0, 0))],
        out_specs=qo,
        compiler_params=_params(3),
        name="attention",
    )(q, k_t, v_ones)


def _head_major(x, n_seq):
    return x.reshape(n_seq, -1, x.shape[1] // ATT_HEAD_DIM, ATT_HEAD_DIM).transpose(0, 2, 1, 3)


def _with_ones(v):
    pad = jnp.zeros(v.shape[:-1] + (LANES - ATT_HEAD_DIM - 1,), v.dtype)
    return jnp.concatenate([v, jnp.ones(v.shape[:-1] + (1,), v.dtype), pad], axis=-1)


def _token_major(o):
    n_seq, heads, t, dh = o.shape
    return o.transpose(0, 2, 1, 3).reshape(n_seq * t, heads * dh)


def _hy_filter_body(feat_ref, t_ref, w1_ref, b1_ref, w2_ref, b2_ref, fr_ref, w3f_ref, w3b_ref, dl_ref,
                    hsum_ref, hdiff_ref, nyq_ref, z_ref):
    @pl.when(pl.program_id(0) == 0)
    def _():
        z = jnp.dot(feat_ref[...], w1_ref[...], preferred_element_type=F32, precision=HIGHEST) + b1_ref[...]
        z = jnp.sin(fr_ref[0:1, :] * z)
        z = jnp.dot(z, w2_ref[...], preferred_element_type=F32, precision=HIGHEST) + b2_ref[...]
        z_ref[...] = jnp.sin(fr_ref[1:2, :] * z)

    z = z_ref[...]
    window = jnp.exp(-t_ref[...] * dl_ref[...])
    h_f = jnp.dot(z, w3f_ref[...], preferred_element_type=F32, precision=HIGHEST) * window
    h_b = jnp.dot(z, w3b_ref[...], preferred_element_type=F32, precision=HIGHEST) * window
    row = lax.broadcasted_iota(jnp.int32, h_f.shape, 0)
    h_b = jnp.where(row == 0, 0.0, h_b)
    inv = 1.0 / (jnp.sum(jnp.abs(h_f), axis=0, keepdims=True) + jnp.sum(jnp.abs(h_b), axis=0, keepdims=True))
    h_sum = (h_f + h_b) * inv
    hsum_ref[...] = h_sum
    hdiff_ref[...] = (h_f - h_b) * inv
    nyq_ref[...] = jnp.sum(jnp.where(row % 2 == 0, h_sum, -h_sum), axis=0, keepdims=True)


def _hy_filter(seq_len, w1, b1, w2, b2, w3, sin_freq):
    tc = 256
    fw = w1.shape[1]
    t = jnp.arange(seq_len, dtype=F32)[:, None] / seq_len
    bands = jnp.arange(1, HY_BANDS + 1, dtype=F32)[None, :]
    feat = jnp.concatenate([t, jnp.sin(2.0 * math.pi * bands * t), jnp.cos(2.0 * math.pi * bands * t)], axis=-1)
    feat = jnp.pad(feat, ((0, 0), (0, LANES - HY_EMB)))
    deltas = jnp.abs(jnp.linspace(math.log(HY_TARGET) / HY_LONG_PCT, math.log(HY_TARGET) / HY_SHORT_PCT, D,
                                  dtype=F32)).reshape(1, D)
    pad_w = LANES - fw
    full = lambda shape: pl.BlockSpec(shape, lambda j: (0,) * len(shape))
    return pl.pallas_call(
        _hy_filter_body,
        out_shape=(jax.ShapeDtypeStruct((seq_len, D), F32), jax.ShapeDtypeStruct((seq_len, D), F32),
                   jax.ShapeDtypeStruct((1, D), F32)),
        grid=(D // tc,),
        in_specs=[full((seq_len, LANES)), full((seq_len, 1)), full((LANES, LANES)), full((1, LANES)),
                  full((LANES, LANES)), full((1, LANES)), full((2, LANES)),
                  pl.BlockSpec((LANES, tc), lambda j: (0, j)), pl.BlockSpec((LANES, tc), lambda j: (0, D // tc + j)),
                  pl.BlockSpec((1, tc), lambda j: (0, j))],
        out_specs=(pl.BlockSpec((seq_len, tc), lambda j: (0, j)), pl.BlockSpec((seq_len, tc), lambda j: (0, j)),
                   pl.BlockSpec((1, tc), lambda j: (0, j))),
        scratch_shapes=[pltpu.VMEM((seq_len, LANES), F32)],
        compiler_params=_params(1),
        name="hyena_filter",
    )(feat, t, jnp.pad(w1, ((0, LANES - HY_EMB), (0, pad_w))), jnp.pad(b1, (0, pad_w)).reshape(1, LANES),
      jnp.pad(w2, ((0, pad_w), (0, pad_w))), jnp.pad(b2, (0, pad_w)).reshape(1, LANES),
      jnp.pad(sin_freq, ((0, 0), (0, pad_w))), jnp.pad(w3, ((0, pad_w), (0, 0))), jnp.pad(w3, ((0, pad_w), (0, 0))),
      deltas)


def _dft_matrices(seq_len):
    n = 2 * seq_len
    k = lax.broadcasted_iota(jnp.int32, (seq_len, seq_len), 0)
    t = lax.broadcasted_iota(jnp.int32, (seq_len, seq_len), 1)
    ang = ((k * t) % n).astype(F32) * (2.0 * math.pi / n)
    cr, base = jnp.cos(ang), -jnp.sin(ang)
    ci = jnp.where(k == 0, (1 - 2 * (t % 2)).astype(F32), base)
    cit = jnp.where(t == 0, (1 - 2 * (k % 2)).astype(F32), base)
    return cr.astype(BF16), ci.astype(BF16), cit.astype(BF16)


def _dft_fwd_body(cr_ref, ci_ref, b1_ref, b2_ref, *refs, nk, tf, mult):
    if mult:
        kr_ref, ki_ref, or_ref, oi_ref, accr_ref, acci_ref = refs
    else:
        or_ref, oi_ref, accr_ref, acci_ref = refs
    kk = pl.program_id(3)
    pr = jnp.dot(cr_ref[...], b1_ref[0].astype(BF16), preferred_element_type=F32)
    pi = jnp.dot(ci_ref[...], b2_ref[0].astype(BF16), preferred_element_type=F32)

    @pl.when(kk == 0)
    def _():
        accr_ref[...] = pr
        acci_ref[...] = pi

    @pl.when(kk > 0)
    def _():
        accr_ref[...] += pr
        acci_ref[...] += pi

    @pl.when(kk == nk - 1)
    def _():
        zr, zi = accr_ref[...], acci_ref[...]
        if mult:
            kr, ki = kr_ref[...], ki_ref[...]
            first = (pl.program_id(1) * tf + lax.broadcasted_iota(jnp.int32, zr.shape, 0)) == 0
            or_ref[0] = jnp.where(first, 0.5 * zr * kr, zr * kr - zi * ki).astype(or_ref.dtype)
            oi_ref[0] = jnp.where(first, 0.5 * zi * ki, zr * ki + zi * kr).astype(oi_ref.dtype)
        else:
            or_ref[0] = zr
            oi_ref[0] = zi


def _dft_fwd(mats, b1, b2, *, row0, n_seq, seq_len, tf, tk, tn, filt=None):
    cr, ci, _ = mats
    nk = seq_len // tk
    r0 = row0 // tk
    a_spec = pl.BlockSpec((tf, tk), lambda b, f, c, kk: (f, kk))
    b_spec = pl.BlockSpec((1, tk, tn), lambda b, f, c, kk: (0, r0 + b * nk + kk, c))
    o_spec = pl.BlockSpec((1, tf, tn), lambda b, f, c, kk: (b, f, c))
    in_specs = [a_spec] * 2 + [b_spec] * 2
    args = [cr, ci, b1[None], b2[None]]
    if filt is not None:
        in_specs += [pl.BlockSpec((tf, tn), lambda b, f, c, kk: (f, c))] * 2
        args += list(filt)
    shape = jax.ShapeDtypeStruct((n_seq, seq_len, D), F32 if filt is None else BF16)
    return pl.pallas_call(
        functools.partial(_dft_fwd_body, nk=nk, tf=tf, mult=filt is not None),
        out_shape=(shape, shape),
        grid=(n_seq, seq_len // tf, D // tn, nk),
        in_specs=in_specs,
        out_specs=(o_spec, o_spec),
        scratch_shapes=[pltpu.VMEM((tf, tn), F32), pltpu.VMEM((tf, tn), F32)],
        compiler_params=_params(4),
        name="hyena_dft",
    )(*args)


def _dft_inv_body(cr_ref, ct_ref, yr_ref, yi_ref, z_ref, x0_ref, skip_ref, o_ref, acc_ref, *, nk, scale):
    kk = pl.program_id(3)
    p = (jnp.dot(cr_ref[...], yr_ref[0], preferred_element_type=F32)
         + jnp.dot(ct_ref[...], yi_ref[0], preferred_element_type=F32))

    @pl.when(kk == 0)
    def _():
        acc_ref[...] = p

    @pl.when(kk > 0)
    def _():
        acc_ref[...] += p

    @pl.when(kk == nk - 1)
    def _():
        z = z_ref[...]
        o_ref[...] = ((acc_ref[...] * scale + z * skip_ref[...]) * x0_ref[...]).astype(BF16)


def _dft_inv(mats, yr, yi, z, x0, skip, *, row0, n_seq, seq_len, tm, tk, tn):
    cr, _, cit = mats
    nk = seq_len // tk
    nt = seq_len // tm
    r0 = row0 // tm
    a_spec = pl.BlockSpec((tm, tk), lambda b, i, c, kk: (i, kk))
    y_spec = pl.BlockSpec((1, tk, tn), lambda b, i, c, kk: (b, kk, c))
    tok = pl.BlockSpec((tm, tn), lambda b, i, c, kk: (r0 + b * nt + i, c))
    return pl.pallas_call(
        functools.partial(_dft_inv_body, nk=nk, scale=1.0 / seq_len),
        out_shape=jax.ShapeDtypeStruct((n_seq * seq_len, D), BF16),
        grid=(n_seq, nt, D // tn, nk),
        in_specs=[a_spec] * 2 + [y_spec] * 2 + [tok, tok, pl.BlockSpec((1, tn), lambda b, i, c, kk: (0, c))],
        out_specs=pl.BlockSpec((tm, tn), lambda b, i, c, kk: (b * nt + i, c)),
        scratch_shapes=[pltpu.VMEM((tm, tn), F32)],
        compiler_params=_params(4),
        name="hyena_idft",
    )(cr, cit, yr, yi, z, x0, skip.reshape(1, D))


def _hy_gate_body(x0_ref, x1_ref, v_ref, w0_ref, w1_ref, wv_ref, b0_ref, b1_ref, bv_ref, z_ref, x0o_ref, *, seq_len):
    def conv(u_ref, w_ref, b_ref):
        u = u_ref[...]
        n = u.shape[0]
        pos = lax.broadcasted_iota(jnp.int32, u.shape, 0) % seq_len
        prev = jnp.where(pos == 0, 0.0, pltpu.roll(u, 1, 0))
        nxt = jnp.where(pos == seq_len - 1, 0.0, pltpu.roll(u, n - 1, 0))
        return prev * w_ref[0:1, :] + u * w_ref[1:2, :] + nxt * w_ref[2:3, :] + b_ref[...]

    x0o_ref[...] = conv(x0_ref, w0_ref, b0_ref)
    z_ref[...] = conv(v_ref, wv_ref, bv_ref) * conv(x1_ref, w1_ref, b1_ref)


def _hy_gate(u, conv_w, conv_b, *, row0, n_seq, seq_len, seqs_per_step):
    tc = 128
    nb = D // tc
    rows = seqs_per_step * seq_len
    r0 = row0 // rows

    def col(part):
        return [pl.BlockSpec((rows, tc), lambda b, c: (r0 + b, part * nb + c)),
                pl.BlockSpec((3, tc), lambda b, c: (0, part * nb + c)),
                pl.BlockSpec((1, tc), lambda b, c: (0, part * nb + c))]

    specs = [col(p) for p in range(3)]
    out = pl.BlockSpec((rows, tc), lambda b, c: (b, c))
    shape = jax.ShapeDtypeStruct((n_seq * seq_len, D), F32)
    return pl.pallas_call(
        functools.partial(_hy_gate_body, seq_len=seq_len),
        out_shape=(shape, shape),
        grid=(n_seq // seqs_per_step, nb),
        in_specs=[s[0] for s in specs] + [s[1] for s in specs] + [s[2] for s in specs],
        out_specs=(out, out),
        compiler_params=_params(2),
        name="hyena_short_conv",
    )(u, u, u, conv_w, conv_w, conv_w, conv_b.reshape(1, 3 * D), conv_b.reshape(1, 3 * D), conv_b.reshape(1, 3 * D))


FFT_A, FFT_R = 64, 64
FFT_M = 2 * FFT_A
FFT_H = FFT_R // 2
assert FFT_A * FFT_R == DEC_SEQ


def _fft_consts():
    n = 2 * DEC_SEQ
    th = 2.0 * np.pi * (np.arange(FFT_M)[:, None] + 0.5) * np.arange(FFT_A)[None, :] / FFT_M
    f1 = np.concatenate([np.cos(th), -np.sin(th)], axis=0)
    k = np.arange(FFT_M)[:, None, None] + FFT_M * np.arange(FFT_H)[None, :, None] + 0.5
    ph = 2.0 * np.pi * k * np.arange(FFT_R)[None, None, :] / n
    c, s = np.cos(ph), np.sin(ph)
    g = np.concatenate([np.concatenate([c, s], axis=2), np.concatenate([-s, c], axis=2)], axis=1)
    as_bf16 = lambda m: jnp.asarray(m, dtype=F32).astype(BF16)
    return as_bf16(f1), as_bf16(f1.T), as_bf16(g), as_bf16(g.transpose(0, 2, 1))


def _fft_stage1(src_ref, y_ref, f1):
    for b in range(FFT_R):
        zb = src_ref[pl.ds(b, FFT_A, stride=FFT_R), :].astype(BF16)
        y_ref[b * 2 * FFT_M:(b + 1) * 2 * FFT_M, :] = jnp.dot(f1, zb, preferred_element_type=F32)


def _fft_stage2(y_ref, g_ref, k1):
    yr = y_ref[pl.ds(k1, FFT_R, stride=2 * FFT_M), :]
    yi = y_ref[pl.ds(FFT_M + k1, FFT_R, stride=2 * FFT_M), :]
    z = jnp.dot(g_ref[k1], jnp.concatenate([yr, yi], axis=0).astype(BF16), preferred_element_type=F32)
    return z[:FFT_H], z[FFT_H:]


def _hy_spectrum_body(hs_ref, hd_ref, f1_ref, g_ref, kr_ref, ki_ref, y_ref):
    f1 = f1_ref[...]
    _fft_stage1(hs_ref, y_ref, f1)
    for k1 in range(FFT_M):
        kr_ref[k1 * FFT_H:(k1 + 1) * FFT_H, :] = _fft_stage2(y_ref, g_ref, k1)[0]
    _fft_stage1(hd_ref, y_ref, f1)
    for k1 in range(FFT_M):
        ki_ref[k1 * FFT_H:(k1 + 1) * FFT_H, :] = _fft_stage2(y_ref, g_ref, k1)[1]


def _hy_spectrum(h_sum, h_diff, consts):
    tc = 128
    f1, _, g, _ = consts
    blk = pl.BlockSpec((DEC_SEQ, tc), lambda c: (0, c))
    shape = jax.ShapeDtypeStruct((DEC_SEQ, D), F32)
    return pl.pallas_call(
        _hy_spectrum_body,
        out_shape=(shape, shape),
        grid=(D // tc,),
        in_specs=[blk, blk, pl.BlockSpec(f1.shape, lambda c: (0, 0)), pl.BlockSpec(g.shape, lambda c: (0, 0, 0))],
        out_specs=(blk, blk),
        scratch_shapes=[pltpu.VMEM((FFT_R * 2 * FFT_M, tc), F32)],
        compiler_params=_params(1),
        name="hyena_filter_fft",
    )(h_sum, h_diff, f1, g)


def _hy_conv_fft_body(x0_ref, x1_ref, v_ref, w0_ref, w1_ref, wv_ref, b0_ref, b1_ref, bv_ref, kr_ref, ki_ref,
                      skip_ref, f1_ref, f1t_ref, g_ref, gt_ref, o_ref, z_ref, y_ref, t_ref):
    rows = 512

    def conv(u_ref, w_ref, b_ref, r):
        u = u_ref[r:r + rows, :]
        row = lax.broadcasted_iota(jnp.int32, u.shape, 0)
        before = u_ref[r - 1:r, :] if r > 0 else jnp.zeros_like(u[0:1])
        after = u_ref[r + rows:r + rows + 1, :] if r + rows < DEC_SEQ else jnp.zeros_like(u[0:1])
        prev = jnp.where(row == 0, before, pltpu.roll(u, 1, 0))
        nxt = jnp.where(row == rows - 1, after, pltpu.roll(u, rows - 1, 0))
        return prev * w_ref[0:1, :] + u * w_ref[1:2, :] + nxt * w_ref[2:3, :] + b_ref[...]

    for r in range(0, DEC_SEQ, rows):
        z_ref[r:r + rows, :] = conv(v_ref, wv_ref, bv_ref, r) * conv(x1_ref, w1_ref, b1_ref, r)
    _fft_stage1(z_ref, y_ref, f1_ref[...])
    for k1 in range(FFT_M):
        zr, zi = _fft_stage2(y_ref, g_ref, k1)
        kr = kr_ref[k1 * FFT_H:(k1 + 1) * FFT_H, :]
        ki = ki_ref[k1 * FFT_H:(k1 + 1) * FFT_H, :]
        p = jnp.concatenate([zr * kr - zi * ki, zr * ki + zi * kr], axis=0).astype(BF16)
        u = jnp.dot(gt_ref[k1], p, preferred_element_type=F32)
        y_ref[pl.ds(k1, FFT_R, stride=2 * FFT_M), :] = u[:FFT_R]
        y_ref[pl.ds(FFT_M + k1, FFT_R, stride=2 * FFT_M), :] = u[FFT_R:]
    f1t = f1t_ref[...]
    for b in range(FFT_R):
        yb = jnp.dot(f1t, y_ref[b * 2 * FFT_M:(b + 1) * 2 * FFT_M, :].astype(BF16), preferred_element_type=F32)
        t_ref[pl.ds(b, FFT_A, stride=FFT_R), :] = yb
    for r in range(0, DEC_SEQ, rows):
        y = t_ref[r:r + rows, :] * (1.0 / DEC_SEQ) + z_ref[r:r + rows, :] * skip_ref[...]
        o_ref[r:r + rows, :] = (y * conv(x0_ref, w0_ref, b0_ref, r)).astype(BF16)


def _hy_conv_fft(u, conv_w, conv_b, kr, ki, skip, consts, *, row0, n_seq):
    tc = 128
    nb = D // tc
    r0 = row0 // DEC_SEQ
    f1, f1t, g, gt = consts

    def col(part):
        return [pl.BlockSpec((DEC_SEQ, tc), lambda b, c: (r0 + b, part * nb + c)),
                pl.BlockSpec((3, tc), lambda b, c: (0, part * nb + c)),
                pl.BlockSpec((1, tc), lambda b, c: (0, part * nb + c))]

    specs = [col(p) for p in range(3)]
    chan = pl.BlockSpec((DEC_SEQ, tc), lambda b, c: (0, c))
    const = lambda m: pl.BlockSpec(m.shape, lambda b, c: (0,) * m.ndim)
    cb = conv_b.reshape(1, 3 * D)
    return pl.pallas_call(
        _hy_conv_fft_body,
        out_shape=jax.ShapeDtypeStruct((n_seq * DEC_SEQ, D), BF16),
        grid=(n_seq, nb),
        in_specs=([s[0] for s in specs] + [s[1] for s in specs] + [s[2] for s in specs]
                  + [chan, chan, pl.BlockSpec((1, tc), lambda b, c: (0, c))] + [const(m) for m in consts]),
        out_specs=pl.BlockSpec((DEC_SEQ, tc), lambda b, c: (b, c)),
        scratch_shapes=[pltpu.VMEM((DEC_SEQ, tc), F32), pltpu.VMEM((FFT_R * 2 * FFT_M, tc), F32),
                        pltpu.VMEM((DEC_SEQ, tc), F32)],
        compiler_params=_params(2, 56 * 1024 * 1024),
        name="hyena_conv_fft",
    )(u, u, u, conv_w, conv_w, conv_w, cb, cb, cb, kr, ki, skip.reshape(1, D), f1, f1t, g, gt)


ROW_TILE = D // LANES
ROUTER_TM = 512
EXPERT_TM = 512
N_SLOTS = 2 * N_TOK + N_EXPERTS * EXPERT_TM
N_SLOT_TILES = N_SLOTS // EXPERT_TM
INFO_E1, INFO_E2, INFO_R1, INFO_R2, INFO_W1, INFO_W2 = range(6)


def _to_row_tiles(ref, x):
    rows = x.shape[0]
    for j in range(ROW_TILE):
        ref[pl.ds(j, rows, stride=ROW_TILE), :] = x[:, j * LANES:(j + 1) * LANES]


def _from_row_tiles(ref, rows):
    return jnp.concatenate([ref[pl.ds(j, rows, stride=ROW_TILE), :] for j in range(ROW_TILE)], axis=-1)


def _router_body(x_ref, sh_ref, sc_ref, w_ref, info_ref, incl_ref, cnt_ref):
    @pl.when(pl.program_id(0) == 0)
    def _():
        cnt_ref[...] = jnp.zeros_like(cnt_ref)

    h = _modulate(x_ref, sh_ref, sc_ref)
    logits = jnp.dot(h, w_ref[...], preferred_element_type=F32, precision=HIGHEST)
    lane = lax.broadcasted_iota(jnp.int32, logits.shape, 1).astype(F32)
    logits = jnp.where(lane < N_EXPERTS, logits, -jnp.inf)
    e = jnp.exp(logits - jnp.max(logits, axis=-1, keepdims=True))
    p = e / jnp.sum(e, axis=-1, keepdims=True)
    p1 = jnp.max(p, axis=-1, keepdims=True)
    i1 = jnp.min(jnp.where(p == p1, lane, float(LANES)), axis=-1, keepdims=True)
    rest = jnp.where(lane == i1, -1.0, p)
    p2 = jnp.max(rest, axis=-1, keepdims=True)
    i2 = jnp.min(jnp.where(rest == p2, lane, float(LANES)), axis=-1, keepdims=True)
    total = p1 + p2
    chosen = jnp.where((lane == i1) | (lane == i2), 1.0, 0.0)
    tm = chosen.shape[0]
    earlier = (lax.broadcasted_iota(jnp.int32, (tm, tm), 1) < lax.broadcasted_iota(jnp.int32, (tm, tm), 0))
    rank = jnp.dot(earlier.astype(BF16), chosen.astype(BF16), preferred_element_type=F32) + cnt_ref[...]
    r1 = jnp.sum(jnp.where(lane == i1, rank, 0.0), axis=-1, keepdims=True)
    r2 = jnp.sum(jnp.where(lane == i2, rank, 0.0), axis=-1, keepdims=True)
    cnt_ref[...] += jnp.sum(chosen, axis=0, keepdims=True)
    incl_ref[0] = jnp.broadcast_to(cnt_ref[...], incl_ref.shape[1:])
    info = jnp.zeros_like(p)
    for col, val in ((INFO_E1, i1), (INFO_E2, i2), (INFO_R1, r1), (INFO_R2, r2),
                     (INFO_W1, p1 / total), (INFO_W2, p2 / total)):
        info = jnp.where(lane == col, val, info)
    info_ref[...] = info


def _router(x, sh, sc, w_router):
    tm = ROUTER_TM
    return pl.pallas_call(
        _router_body,
        out_shape=(jax.ShapeDtypeStruct((N_TOK, LANES), F32),
                   jax.ShapeDtypeStruct((N_TOK // tm, 8, LANES), F32)),
        grid=(N_TOK // tm,),
        in_specs=_mod_specs(tm, 1, 0) + [pl.BlockSpec((D, LANES), lambda i: (0, 0))],
        out_specs=(pl.BlockSpec((tm, LANES), lambda i: (i, 0)), pl.BlockSpec((1, 8, LANES), lambda i: (i, 0, 0))),
        scratch_shapes=[pltpu.VMEM((1, LANES), F32)],
        compiler_params=_params(1),
        name="moe_router",
    )(x, sh, sc, jnp.pad(w_router, ((0, 0), (0, LANES - N_EXPERTS))))


def _row_tile(ref, row):
    return ref.at[pl.ds(pl.multiple_of(row * ROW_TILE, ROW_TILE), ROW_TILE)]


def _dispatch_body(pos1_ref, pos2_ref, x_ref, zeros_hbm, xs_hbm, rows_ref, sem, *, tm):
    del zeros_hbm
    base = pl.program_id(0) * tm
    _to_row_tiles(rows_ref, x_ref[...])

    def copies(r):
        src = _row_tile(rows_ref, r)
        return (pltpu.make_async_copy(src, _row_tile(xs_hbm, pos1_ref[base + r]), sem),
                pltpu.make_async_copy(src, _row_tile(xs_hbm, pos2_ref[base + r]), sem))

    def issue(r, carry):
        for queue, cp in enumerate(copies(r)):
            cp.start(priority=queue)
        return carry

    lax.fori_loop(0, tm, issue, 0, unroll=8)
    for _ in range(2):
        pltpu.make_async_copy(rows_ref, xs_hbm.at[pl.ds(0, tm * ROW_TILE)], sem).wait()


def _dispatch(pos1, pos2, x):
    tm = 512
    return pl.pallas_call(
        functools.partial(_dispatch_body, tm=tm),
        out_shape=jax.ShapeDtypeStruct((N_SLOTS * ROW_TILE, LANES), F32),
        grid_spec=pltpu.PrefetchScalarGridSpec(
            num_scalar_prefetch=2, grid=(N_TOK // tm,),
            in_specs=[pl.BlockSpec((tm, D), lambda i, *_: (i, 0)), pl.BlockSpec(memory_space=pl.ANY)],
            out_specs=pl.BlockSpec(memory_space=pl.ANY),
            scratch_shapes=[pltpu.VMEM((tm * ROW_TILE, LANES), F32), pltpu.SemaphoreType.DMA(())]),
        input_output_aliases={3: 0},
        compiler_params=_params(1),
        name="moe_dispatch",
    )(pos1, pos2, x, jnp.zeros((N_SLOTS * ROW_TILE, LANES), F32))


def _expert_swiglu_body(eid_ref, b1_ref, b2_ref, nv_ref, xs_ref, sh_ref, sc_ref, wg_ref, wu_ref, o_ref, *, fc):
    t = pl.program_id(0)

    @pl.when(t < nv_ref[0])
    def _():
        tm = o_ref.shape[0]
        x = _from_row_tiles(xs_ref, tm)
        slot = t * tm + lax.broadcasted_iota(jnp.int32, (tm, 1), 0)
        in1, in2 = slot >= b1_ref[t], slot >= b2_ref[t]
        sc = jnp.where(in2, sc_ref[2], jnp.where(in1, sc_ref[1], sc_ref[0]))
        sh = jnp.where(in2, sh_ref[2], jnp.where(in1, sh_ref[1], sh_ref[0]))
        h = (x * (1.0 + sc) + sh).astype(BF16)
        for c in range(o_ref.shape[1] // fc):
            cs = slice(c * fc, (c + 1) * fc)
            g = jnp.dot(h, wg_ref[0, :, cs], preferred_element_type=F32)
            u = jnp.dot(h, wu_ref[0, :, cs], preferred_element_type=F32)
            o_ref[:, cs] = (g * jax.nn.sigmoid(g) * u).astype(BF16)

    @pl.when(t >= nv_ref[0])
    def _():
        o_ref[...] = jnp.zeros_like(o_ref)


def _expert_swiglu(meta, xs_rt, sh, sc, w_gate, w_up, e0):
    eid, b1, b2, nv = meta
    tm, f = EXPERT_TM, w_gate.shape[2]

    def tile(t, eid, b1, b2, nv):
        return jnp.minimum(t, nv[0] - 1)

    w_spec = pl.BlockSpec((1, D, f), lambda t, eid, b1, b2, nv: (e0 + eid[tile(t, eid, b1, b2, nv)], 0, 0))
    mod = pl.BlockSpec((8, 1, D), lambda t, *_: (0, 0, 0))
    return pl.pallas_call(
        functools.partial(_expert_swiglu_body, fc=512),
        out_shape=jax.ShapeDtypeStruct((N_SLOTS, f), BF16),
        grid_spec=pltpu.PrefetchScalarGridSpec(
            num_scalar_prefetch=4, grid=(N_SLOT_TILES,),
            in_specs=[pl.BlockSpec((tm * ROW_TILE, LANES), lambda t, *m: (tile(t, *m), 0)), mod, mod, w_spec, w_spec],
            out_specs=pl.BlockSpec((tm, f), lambda t, *m: (t, 0))),
        compiler_params=_params(1, 56 * 1024 * 1024),
        name="moe_swiglu",
    )(eid, b1, b2, nv, xs_rt, sh, sc, w_gate, w_up)


def _expert_down_body(eid_ref, nv_ref, a_ref, w_ref, y_ref):
    @pl.when(pl.program_id(0) < nv_ref[0])
    def _():
        _to_row_tiles(y_ref, jnp.dot(a_ref[...], w_ref[0], preferred_element_type=F32))

    @pl.when(pl.program_id(0) >= nv_ref[0])
    def _():
        y_ref[...] = jnp.zeros_like(y_ref)


def _expert_down(meta, act, w_down, e0):
    eid, _, _, nv = meta
    tm, f = EXPERT_TM, act.shape[1]

    def tile(t, eid, nv):
        return jnp.minimum(t, nv[0] - 1)

    return pl.pallas_call(
        _expert_down_body,
        out_shape=jax.ShapeDtypeStruct((N_SLOTS * ROW_TILE, LANES), F32),
        grid_spec=pltpu.PrefetchScalarGridSpec(
            num_scalar_prefetch=2, grid=(N_SLOT_TILES,),
            in_specs=[pl.BlockSpec((tm, f), lambda t, *m: (tile(t, *m), 0)),
                      pl.BlockSpec((1, f, D), lambda t, eid, nv: (e0 + eid[tile(t, eid, nv)], 0, 0))],
            out_specs=pl.BlockSpec((tm * ROW_TILE, LANES), lambda t, *m: (t, 0))),
        compiler_params=_params(1),
        name="moe_down",
    )(eid, nv, act, w_down)


def _combine_body(pos1_ref, pos2_ref, y_hbm, info_ref, x_ref, g_ref, lng_ref, lnb_ref, o_ref, y1_ref, y2_ref, sem,
                  *, tm):
    base = pl.program_id(0) * tm

    def copies(r):
        return (pltpu.make_async_copy(_row_tile(y_hbm, pos1_ref[base + r]), _row_tile(y1_ref, r), sem),
                pltpu.make_async_copy(_row_tile(y_hbm, pos2_ref[base + r]), _row_tile(y2_ref, r), sem))

    def issue(r, carry):
        for queue, cp in enumerate(copies(r)):
            cp.start(priority=queue)
        return carry

    lax.fori_loop(0, tm, issue, 0, unroll=8)
    for y_ref in (y1_ref, y2_ref):
        pltpu.make_async_copy(y_hbm.at[pl.ds(0, tm * ROW_TILE)], y_ref, sem).wait()
    info = info_ref[...]
    ffn = (info[:, INFO_W1:INFO_W1 + 1] * _from_row_tiles(y1_ref, tm)
           + info[:, INFO_W2:INFO_W2 + 1] * _from_row_tiles(y2_ref, tm))
    y = ALPHA * x_ref[...] + g_ref[0] * ffn
    mu = jnp.mean(y, axis=-1, keepdims=True)
    yc = y - mu
    var = jnp.mean(yc * yc, axis=-1, keepdims=True)
    o_ref[...] = yc * lax.rsqrt(var + LN_EPS) * lng_ref[...] + lnb_ref[...]


def _combine(pos1, pos2, y_rt, info, x, gate, ln_g, ln_b):
    tm = 512
    return pl.pallas_call(
        functools.partial(_combine_body, tm=tm),
        out_shape=jax.ShapeDtypeStruct((N_TOK, D), F32),
        grid_spec=pltpu.PrefetchScalarGridSpec(
            num_scalar_prefetch=2, grid=(N_TOK // tm,),
            in_specs=[pl.BlockSpec(memory_space=pl.ANY),
                      pl.BlockSpec((tm, LANES), lambda i, *_: (i, 0)),
                      pl.BlockSpec((tm, D), lambda i, *_: (i, 0)),
                      pl.BlockSpec((1, 1, D), lambda i, *_: (_group_of_row(i * tm), 0, 0)),
                      pl.BlockSpec((1, D), lambda i, *_: (0, 0)),
                      pl.BlockSpec((1, D), lambda i, *_: (0, 0))],
            out_specs=pl.BlockSpec((tm, D), lambda i, *_: (i, 0)),
            scratch_shapes=[pltpu.VMEM((tm * ROW_TILE, LANES), F32), pltpu.VMEM((tm * ROW_TILE, LANES), F32),
                            pltpu.SemaphoreType.DMA(())]),
        compiler_params=_params(1),
        name="moe_combine",
    )(pos1, pos2, y_rt, info, x, gate, ln_g.reshape(1, D), ln_b.reshape(1, D))


def _slot_plan(info, incl):
    row = lambda n_rows: incl[n_rows // ROUTER_TM - 1, 0, :N_EXPERTS].astype(jnp.int32)
    count = row(N_TOK)
    padded = (count + EXPERT_TM - 1) // EXPERT_TM * EXPERT_TM
    end = jnp.cumsum(padded)
    start = end - padded
    tile_row = jnp.arange(N_SLOT_TILES, dtype=jnp.int32) * EXPERT_TM
    eid = jnp.minimum(jnp.sum(tile_row[:, None] >= end[None, :], axis=1), N_EXPERTS - 1).astype(jnp.int32)
    b1 = (start + row(N_CTX))[eid]
    b2 = (start + row(N_CTX + DEC_SEQ))[eid]
    nv = (end[-1:] // EXPERT_TM).astype(jnp.int32)
    e1, e2 = info[:, INFO_E1].astype(jnp.int32), info[:, INFO_E2].astype(jnp.int32)
    pos1 = start[e1] + info[:, INFO_R1].astype(jnp.int32)
    pos2 = start[e2] + info[:, INFO_R2].astype(jnp.int32)
    return pos1, pos2, (eid, b1, b2, nv)


def _even_mixer(x, sh, sc, gate, ln_g, ln_b, w_in, b_igate, b_fgate, ml_norm_g, q_norm_g, k_norm_g, w_out,
                st_c, st_n, st_m, cache_k, cache_v, rope_tabs):
    splits = (4 * ML_W, 4 * ML_W + N_GATES)
    w_main = jnp.concatenate([w_in[:, :splits[0]], w_in[:, splits[1]:]], axis=1).astype(BF16)
    proj = _mod_matmul(x, sh, sc, [w_main[None]], tm=1024, tn=MAIN_W // 2, out_dtype=F32, name="even_in_proj")[0]
    b_gate = jnp.stack([b_igate, b_fgate], axis=1).reshape(N_GATES)
    lic, bc, lir, br = _gates(x, sh, sc, w_in[:, splits[0]:splits[1]], b_gate)

    hf_c, hb_c, new_c, new_n, new_m = _mlstm(proj, lic, bc, lir, br, row0=0, n_seq=BATCH, seq_len=SEQ)
    init = (st_c, st_n, jnp.broadcast_to(st_m[..., None], st_n.shape))
    hf_s, hb_s, _, _, _ = _mlstm(proj, lic, bc, lir, br, row0=N_CTX, n_seq=DEC_BATCH, seq_len=DEC_SEQ, init=init)
    ml = _ml_post((hf_c, hf_s), (hb_c, hb_s), proj, ml_norm_g)

    q_c, kn_c, kb_c = _qk_prep(proj, q_norm_g, k_norm_g, row0=0, rows=N_CTX)
    q_s, _, kb_s = _qk_prep(proj, q_norm_g, k_norm_g, row0=N_CTX, rows=N_LAT, rope_tabs=rope_tabs)
    v_all = proj[:, MAIN_W - KV_W:]
    v_c, v_s = v_all[:N_CTX], v_all[N_CTX:]
    att_c = _token_major(_attention(_head_major(q_c, BATCH), _head_major(kb_c, BATCH).swapaxes(2, 3),
                                    _with_ones(_head_major(v_c.astype(BF16), BATCH)), tq=SEQ))
    k_lat = jnp.concatenate([kb_s.reshape(DEC_BATCH, DEC_SEQ, KV_W),
                             cache_k.reshape(DEC_BATCH, PAST_LEN, KV_W).astype(BF16)], axis=1)
    v_lat = jnp.concatenate([v_s.reshape(DEC_BATCH, DEC_SEQ, KV_W).astype(BF16),
                             cache_v.reshape(DEC_BATCH, PAST_LEN, KV_W).astype(BF16)], axis=1)
    att_s = _token_major(_attention(_head_major(q_s, DEC_BATCH),
                                    _head_major(k_lat.reshape(-1, KV_W), DEC_BATCH).swapaxes(2, 3),
                                    _with_ones(_head_major(v_lat.reshape(-1, KV_W), DEC_BATCH)), tq=256))

    x = _proj_res_ln([ml, (att_c, att_s)], w_out.astype(BF16), x, gate, ln_g, ln_b, tm=512, name="even_out_proj")
    new_k = kn_c.reshape(BATCH, SEQ, ATT_KV_HEADS, ATT_HEAD_DIM)
    new_v = v_c.reshape(BATCH, SEQ, ATT_KV_HEADS, ATT_HEAD_DIM)
    return x, new_k, new_v, new_c, new_n, new_m[..., 0]


def _hyena_mixer(x, sh, sc, gate, ln_g, ln_b, w_in, conv_w, conv_b, w1, b1, w2, b2, w3, sin_freq, skip, w_out, dft):
    u = _mod_matmul(x, sh, sc, [w_in.astype(BF16)[None]], tm=1024, tn=1536, out_dtype=F32, name="hyena_in_proj")[0]
    mats, consts = dft
    tiles = dict(tf=SEQ, tk=SEQ, tn=D)
    h_sum, h_diff, nyq = _hy_filter(SEQ, w1, b1, w2, b2, w3, sin_freq)
    kr, ki = _dft_fwd(mats, h_sum, h_diff, row0=0, n_seq=1, seq_len=SEQ, **tiles)
    kr, ki = kr[0], ki[0].at[0].set(nyq[0])
    z, x0 = _hy_gate(u, conv_w, conv_b, row0=0, n_seq=BATCH, seq_len=SEQ, seqs_per_step=8)
    yr, yi = _dft_fwd(mats, z, z, row0=0, n_seq=BATCH, seq_len=SEQ, filt=(kr, ki), **tiles)
    y_c = _dft_inv(mats, yr, yi, z, x0, skip, row0=0, n_seq=BATCH, seq_len=SEQ, tm=SEQ, tk=SEQ, tn=D)
    h_sum, h_diff, _ = _hy_filter(DEC_SEQ, w1, b1, w2, b2, w3, sin_freq)
    kr, ki = _hy_spectrum(h_sum, h_diff, consts)
    y_s = _hy_conv_fft(u, conv_w, conv_b, kr, ki, skip, consts, row0=N_CTX, n_seq=DEC_BATCH)
    return _proj_res_ln([(y_c, y_s)], w_out.astype(BF16), x, gate, ln_g, ln_b, tm=512, name="hyena_out_proj")


def _dense_ffn(x, sh, sc, gate, ln_g, ln_b, w_gate, w_up, w_down):
    act = _mod_matmul(x, sh, sc, [w_gate.astype(BF16)[None], w_up.astype(BF16)[None]],
                      tm=1024, tn=D_FF // 2, out_dtype=BF16, name="ffn_swiglu")
    return _proj_res_ln([act[0]], w_down.astype(BF16), x, gate, ln_g, ln_b, tm=512, name="ffn_down")


def _moe_ffn(x, sh, sc, gate, ln_g, ln_b, w_router, w_gate, w_up, w_down, e0):
    info, incl = _router(x, sh, sc, w_router)
    pos1, pos2, meta = _slot_plan(info, incl)
    xs_rt = _dispatch(pos1, pos2, x)
    act = _expert_swiglu(meta, xs_rt, sh, sc, w_gate, w_up, e0)
    y_rt = _expert_down(meta, act, w_down, e0)
    return _combine(pos1, pos2, y_rt, info, x, gate, ln_g, ln_b)


def kernel(x_prompt, x_sample, cache_attn_k, cache_attn_v, state_mlstm_C, state_mlstm_n, state_mlstm_m, c, c_ctx, w_ada, b_ada, ln_g, ln_b, w_in_even, b_igate, b_fgate, ml_norm_g, q_norm_g, k_norm_g, w_out_even, w_ffn_gate, w_ffn_up, w_ffn_down, w_in_hy, hy_conv_w, hy_conv_b, hy_filt_w1, hy_filt_b1, hy_filt_w2, hy_filt_b2, hy_filt_w3, hy_sin_freq, hy_skip, w_out_hy, w_router, w_moe_gate, w_moe_up, w_moe_down):
    x = jnp.concatenate([x_prompt.reshape(N_CTX, D), x_sample.reshape(N_LAT, D)])
    cvec = jnp.concatenate([c_ctx[None], c, jnp.zeros((8 - 1 - DEC_BATCH, D), F32)])
    mods = _ada(cvec, w_ada, b_ada)
    rope_tabs = _rope_tables()
    dft = (_dft_matrices(SEQ), _fft_consts())
    moe_w = [w.astype(BF16).reshape((-1,) + w.shape[2:]) for w in (w_moe_gate, w_moe_up, w_moe_down)]
    new_k, new_v, new_c, new_n, new_m = [], [], [], [], []
    for layer in range(DEPTH):
        sh1, sc1, g1, sh2, sc2, g2 = (mods[layer, :, i * D:(i + 1) * D].reshape(8, 1, D) for i in range(6))
        i = layer // 2
        if layer % 2 == 0:
            x, k_c, v_c, st_c, st_n, st_m = _even_mixer(
                x, sh1, sc1, g1, ln_g[layer, 0], ln_b[layer, 0], w_in_even[i], b_igate[i], b_fgate[i], ml_norm_g[i],
                q_norm_g[i], k_norm_g[i], w_out_even[i], state_mlstm_C[:, i], state_mlstm_n[:, i], state_mlstm_m[:, i],
                cache_attn_k[:, i], cache_attn_v[:, i], rope_tabs)
            new_k.append(k_c)
            new_v.append(v_c)
            new_c.append(st_c)
            new_n.append(st_n)
            new_m.append(st_m)
            x = _dense_ffn(x, sh2, sc2, g2, ln_g[layer, 1], ln_b[layer, 1], w_ffn_gate[i], w_ffn_up[i], w_ffn_down[i])
        else:
            x = _hyena_mixer(x, sh1, sc1, g1, ln_g[layer, 0], ln_b[layer, 0], w_in_hy[i], hy_conv_w[i], hy_conv_b[i],
                             hy_filt_w1[i], hy_filt_b1[i], hy_filt_w2[i], hy_filt_b2[i], hy_filt_w3[i], hy_sin_freq[i],
                             hy_skip[i], w_out_hy[i], dft)
            x = _moe_ffn(x, sh2, sc2, g2, ln_g[layer, 1], ln_b[layer, 1], w_router[i], *moe_w, i * N_EXPERTS)
    return (x[:N_CTX].reshape(BATCH, SEQ, D), x[N_CTX:].reshape(DEC_BATCH, DEC_SEQ, D),
            jnp.stack(new_k, axis=1), jnp.stack(new_v, axis=1), jnp.stack(new_c, axis=1),
            jnp.stack(new_n, axis=1), jnp.stack(new_m, axis=1))
```

```python
import functools
import math

import jax
import jax.numpy as jnp
import numpy as np
from jax import lax
from jax.experimental import pallas as pl
from jax.experimental.pallas import tpu as pltpu

F32 = jnp.float32
BF16 = jnp.bfloat16
HIGHEST = lax.Precision.HIGHEST

D = 1024
BATCH, SEQ = 32, 256
DEC_BATCH, DEC_SEQ = 2, 4096
DEPTH = 4
PAST_LEN = 256
GRID_W = 64
N_CTX = BATCH * SEQ
N_LAT = DEC_BATCH * DEC_SEQ
N_TOK = N_CTX + N_LAT

ML_HEADS, ML_HEAD_DIM = 4, 128
ML_W = ML_HEADS * ML_HEAD_DIM
CHUNK = 128
MLSTM_PAR = 2
ATT_HEADS, ATT_KV_HEADS, ATT_HEAD_DIM = 8, 2, 64
ATT_GROUP = ATT_HEADS // ATT_KV_HEADS
ATT_W = ATT_HEADS * ATT_HEAD_DIM
KV_W = ATT_KV_HEADS * ATT_HEAD_DIM
GROUP_W = ATT_GROUP * ATT_HEAD_DIM
ROPE_BASE = 10000.0
N_GATES = 4 * ML_HEADS
MAIN_W = 4 * ML_W + ATT_W + 2 * KV_W

HY_EMB = 33
HY_BANDS = (HY_EMB - 1) // 2
HY_TARGET, HY_SHORT_PCT, HY_LONG_PCT = 1e-2, 0.3, 1.5
D_FF = 2816
N_EXPERTS = 8
MOE_D_FF = 3584
ALPHA = (2 * DEPTH) ** 0.25
LN_EPS = 1e-5
RMS_EPS = 1e-6

LANES = 128
VMEM_LIMIT = 48 * 1024 * 1024


def _params(n_axes, vmem=VMEM_LIMIT):
    return pltpu.CompilerParams(dimension_semantics=("arbitrary",) * n_axes, vmem_limit_bytes=vmem)


def _group_of_row(r):
    return jnp.where(r < N_CTX, 0, 1 + (r - N_CTX) // DEC_SEQ)


def _modulate(x_ref, sh_ref, sc_ref):
    return x_ref[...] * (1.0 + sc_ref[0]) + sh_ref[0]


def _mod_specs(tm, n_axes, row_axis):
    def rows(*ids):
        return (ids[row_axis], 0)

    def grp(*ids):
        return (_group_of_row(ids[row_axis] * tm), 0, 0)

    del n_axes
    return [pl.BlockSpec((tm, D), rows), pl.BlockSpec((1, 1, D), grp), pl.BlockSpec((1, 1, D), grp)]


def _ada_body(c_ref, w_ref, b_ref, o_ref):
    c = c_ref[...]
    s = c * jax.nn.sigmoid(c)
    o_ref[0] = jnp.dot(s, w_ref[0], preferred_element_type=F32, precision=HIGHEST) + b_ref[0]


def _ada(cvec, w_ada, b_ada):
    tn = 1536
    return pl.pallas_call(
        _ada_body,
        out_shape=jax.ShapeDtypeStruct((DEPTH, 8, 6 * D), F32),
        grid=(DEPTH, 6 * D // tn),
        in_specs=[pl.BlockSpec((8, D), lambda l, j: (0, 0)),
                  pl.BlockSpec((1, D, tn), lambda l, j: (l, 0, j)),
                  pl.BlockSpec((1, 1, tn), lambda l, j: (l, 0, j))],
        out_specs=pl.BlockSpec((1, 8, tn), lambda l, j: (l, 0, j)),
        compiler_params=_params(2),
        name="ada_modulation",
    )(cvec, w_ada, b_ada.reshape(DEPTH, 1, 6 * D))


def _mod_mm_body(x_ref, sh_ref, sc_ref, *refs, n_w):
    w_refs, o_ref, h_ref = refs[:n_w], refs[n_w], refs[n_w + 1]

    @pl.when(pl.program_id(2) == 0)
    def _():
        h_ref[...] = _modulate(x_ref, sh_ref, sc_ref).astype(BF16)

    h = h_ref[...]
    if n_w == 1:
        o = jnp.dot(h, w_refs[0][0], preferred_element_type=F32)
    else:
        g = jnp.dot(h, w_refs[0][0], preferred_element_type=F32)
        u = jnp.dot(h, w_refs[1][0], preferred_element_type=F32)
        o = g * jax.nn.sigmoid(g) * u
    o_ref[0] = o.astype(o_ref.dtype)


def _mod_matmul(x, sh, sc, ws, *, tm, tn, out_dtype, name):
    n_e, _, f = ws[0].shape
    return pl.pallas_call(
        functools.partial(_mod_mm_body, n_w=len(ws)),
        out_shape=jax.ShapeDtypeStruct((n_e, N_TOK, f), out_dtype),
        grid=(n_e, N_TOK // tm, f // tn),
        in_specs=_mod_specs(tm, 3, 1) + [pl.BlockSpec((1, D, tn), lambda e, i, j: (e, 0, j)) for _ in ws],
        out_specs=pl.BlockSpec((1, tm, tn), lambda e, i, j: (e, i, j)),
        scratch_shapes=[pltpu.VMEM((tm, D), BF16)],
        compiler_params=_params(3),
        name=name,
    )(x, sh, sc, *ws)


def _split_specs(tm, width):
    nc = N_CTX // tm
    return [pl.BlockSpec((tm, width), lambda i: (jnp.minimum(i, nc - 1), 0)),
            pl.BlockSpec((tm, width), lambda i: (jnp.maximum(i - nc, 0), 0))]


def _pick_split(ctx_ref, lat_ref):
    tm = ctx_ref.shape[0]
    return jnp.where(pl.program_id(0) < N_CTX // tm, ctx_ref[...], lat_ref[...])


def _proj_res_ln_body(*refs, split):
    n_in = sum(2 if s else 1 for s in split)
    part_refs = list(refs[:n_in])
    w_ref, x_ref, g_ref, lng_ref, lnb_ref, o_ref = refs[n_in:]
    cols = [_pick_split(part_refs.pop(0), part_refs.pop(0)) if s else part_refs.pop(0)[...] for s in split]
    a = cols[0] if len(cols) == 1 else jnp.concatenate(cols, axis=-1)
    y = ALPHA * x_ref[...] + g_ref[0] * jnp.dot(a, w_ref[...], preferred_element_type=F32)
    mu = jnp.mean(y, axis=-1, keepdims=True)
    yc = y - mu
    var = jnp.mean(yc * yc, axis=-1, keepdims=True)
    o_ref[...] = yc * lax.rsqrt(var + LN_EPS) * lng_ref[...] + lnb_ref[...]


def _proj_res_ln(parts, w, x, gate, ln_g, ln_b, *, tm, name):
    split = tuple(isinstance(p, tuple) for p in parts)
    in_specs, args = [], []
    for p, s in zip(parts, split):
        if s:
            in_specs += _split_specs(tm, p[0].shape[1])
            args += list(p)
        else:
            in_specs.append(pl.BlockSpec((tm, p.shape[1]), lambda i: (i, 0)))
            args.append(p)
    in_specs += [pl.BlockSpec(w.shape, lambda i: (0, 0)),
                 pl.BlockSpec((tm, D), lambda i: (i, 0)),
                 pl.BlockSpec((1, 1, D), lambda i: (_group_of_row(i * tm), 0, 0)),
                 pl.BlockSpec((1, D), lambda i: (0, 0)),
                 pl.BlockSpec((1, D), lambda i: (0, 0))]
    return pl.pallas_call(
        functools.partial(_proj_res_ln_body, split=split),
        out_shape=jax.ShapeDtypeStruct((N_TOK, D), F32),
        grid=(N_TOK // tm,),
        in_specs=in_specs,
        out_specs=pl.BlockSpec((tm, D), lambda i: (i, 0)),
        compiler_params=_params(1),
        name=name,
    )(*args, w, x, gate, ln_g.reshape(1, D), ln_b.reshape(1, D))


def _log_sigmoid(x):
    return jnp.minimum(x, 0.0) - jnp.log(1.0 + jnp.exp(-jnp.abs(x)))


def _gates_body(x_ref, sh_ref, sc_ref, wg_ref, wgt_ref, b_ref, bt_ref,
                lic_ref, bc_ref, lir_ref, br_ref, *, tm):
    h = _modulate(x_ref, sh_ref, sc_ref).astype(BF16)
    g = jnp.dot(h, wg_ref[...], preferred_element_type=F32) + b_ref[...]
    gt = lax.dot_general(wgt_ref[...], h, (((1,), (1,)), ((), ())), preferred_element_type=F32) + bt_ref[...]
    lic_ref[...] = g
    lir_ref[...] = gt
    lf, lft = _log_sigmoid(g), _log_sigmoid(gt)
    r = lax.broadcasted_iota(jnp.int32, (CHUNK, CHUNK), 0)
    c = lax.broadcasted_iota(jnp.int32, (CHUNK, CHUNK), 1)
    tri_l = (c <= r).astype(F32)
    tri_u = (c >= r).astype(F32)
    fwd_col = lax.broadcasted_iota(jnp.int32, (CHUNK, LANES), 1) < 2 * ML_HEADS
    fwd_row = lax.broadcasted_iota(jnp.int32, (N_GATES, CHUNK), 0) < 2 * ML_HEADS
    for ch in range(tm // CHUNK):
        sl = slice(ch * CHUNK, (ch + 1) * CHUNK)
        lfc, lftc = lf[sl, :], lft[:, sl]
        cum_f = jnp.dot(tri_l, lfc, preferred_element_type=F32, precision=HIGHEST)
        cum_b = jnp.dot(tri_u, lfc, preferred_element_type=F32, precision=HIGHEST)
        bc_ref[sl, :] = jnp.where(fwd_col, cum_f, cum_b)
        cum_f = jnp.dot(lftc, tri_u, preferred_element_type=F32, precision=HIGHEST)
        cum_b = jnp.dot(lftc, tri_l, preferred_element_type=F32, precision=HIGHEST)
        br_ref[:, sl] = jnp.where(fwd_row, cum_f, cum_b)


def _gates(x, sh, sc, wg, b_gate):
    tm = 256
    wg_pad = jnp.pad(wg, ((0, 0), (0, LANES - N_GATES)))
    b_pad = jnp.pad(b_gate, (0, LANES - N_GATES)).reshape(1, LANES)
    col = pl.BlockSpec((tm, LANES), lambda i: (i, 0))
    row = pl.BlockSpec((N_GATES, tm), lambda i: (0, i))
    return pl.pallas_call(
        functools.partial(_gates_body, tm=tm),
        out_shape=(jax.ShapeDtypeStruct((N_TOK, LANES), F32), jax.ShapeDtypeStruct((N_TOK, LANES), F32),
                   jax.ShapeDtypeStruct((N_GATES, N_TOK), F32), jax.ShapeDtypeStruct((N_GATES, N_TOK), F32)),
        grid=(N_TOK // tm,),
        in_specs=_mod_specs(tm, 1, 0) + [pl.BlockSpec((D, LANES), lambda i: (0, 0)),
                                         pl.BlockSpec((N_GATES, D), lambda i: (0, 0)),
                                         pl.BlockSpec((1, LANES), lambda i: (0, 0)),
                                         pl.BlockSpec((N_GATES, 1), lambda i: (0, 0))],
        out_specs=(col, col, row, row),
        compiler_params=_params(1),
        name="mlstm_gates",
    )(x, sh, sc, wg_pad.astype(BF16), wg.T.astype(BF16), b_pad, b_gate.reshape(N_GATES, 1))


def _mlstm_body(*refs, has_init):
    (qf, kf, vf, licf, bcf, lirf, brf, qb, kb, vb, licb, bcb, lirb, brb) = refs[:14]
    refs = refs[14:]
    if has_init:
        c0_ref, n0_ref, m0_ref = refs[:3]
        refs = refs[3:]
    hf_ref, hb_ref, c_ref, n_ref, m_ref = refs

    @pl.when(pl.program_id(1) == 0)
    def _():
        if has_init:
            c_ref[...] = c0_ref[...]
            n_ref[...] = n0_ref[...]
            m_ref[...] = m0_ref[...]
        else:
            c_ref[...] = jnp.zeros_like(c_ref)
            n_ref[...] = jnp.zeros_like(n_ref)
            m_ref[...] = jnp.zeros_like(m_ref)

    t_idx = lax.broadcasted_iota(jnp.int32, (CHUNK, CHUNK), 0)
    s_idx = lax.broadcasted_iota(jnp.int32, (CHUNK, CHUNK), 1)
    nt = (((1,), (1,)), ((), ()))
    stores = []

    def chain(u, d, h, q_ref, k_ref, v_ref, lic_ref, bc_ref, lir_ref, br_ref, h_ref):
        mask = (s_idx <= t_idx) if d == 0 else (s_idx >= t_idx)
        hs = slice(h * ML_HEAD_DIM, (h + 1) * ML_HEAD_DIM)
        gi, gf = d * 2 * ML_HEADS + h, d * 2 * ML_HEADS + ML_HEADS + h
        q = q_ref[u, :, hs]
        k = k_ref[u, :, hs] * (ML_HEAD_DIM ** -0.5)
        v = v_ref[u, :, hs]
        qh, kh, vh = q.astype(BF16), k.astype(BF16), v.astype(BF16)
        li_c, b_c = lic_ref[u, :, gi:gi + 1], bc_ref[u, :, gf:gf + 1]
        li_r, b_r = lir_ref[u, gi:gi + 1, :], br_ref[u, gf:gf + 1, :]
        c_st = c_ref[u, d, h]
        n_st = n_ref[u, d, h:h + 1, :]
        m_st = m_ref[u, d, h:h + 1, :][:, 0:1]
        dmat = jnp.where(mask, b_c - b_r + li_r, -jnp.inf)
        inter = b_c + m_st
        m_out = jnp.maximum(inter, jnp.max(dmat, axis=-1, keepdims=True))
        p = jnp.exp(dmat - m_out)
        w_inter = jnp.exp(inter - m_out)
        yield
        qk = lax.dot_general(qh, kh, nt, preferred_element_type=F32)
        qc = jnp.dot(qh, c_st.astype(BF16), preferred_element_type=F32)
        yield
        s = qk * p
        den = (jnp.sum(s, axis=-1, keepdims=True)
               + w_inter * jnp.sum(q * n_st, axis=-1, keepdims=True))
        sh = s.astype(BF16)
        b_last = b_r[:, CHUNK - 1:CHUNK] if d == 0 else b_r[:, 0:1]
        g_r = b_last - b_r + li_r
        g_c = b_last - b_c + li_c
        m_new = jnp.maximum(b_last + m_st, jnp.max(g_r, axis=-1, keepdims=True))
        decay = jnp.exp(b_last + m_st - m_new)
        kw = k * jnp.exp(g_c - m_new)
        kwt = kw.T.astype(BF16)
        yield
        sv = jnp.dot(sh, vh, preferred_element_type=F32)
        kv = jnp.dot(kwt, vh, preferred_element_type=F32)
        yield
        h_out = (sv + w_inter * qc) / jnp.maximum(jnp.abs(den), jnp.exp(-m_out))
        c_new = decay * c_st + kv
        n_new = decay * n_st + jnp.sum(kw, axis=0, keepdims=True)
        stores.append((h_ref, u, d, h, hs, h_out, c_new, n_new, jnp.broadcast_to(m_new, (1, ML_HEAD_DIM))))
        yield

    chains = [chain(u, d, h, *group)
              for u in range(c_ref.shape[0])
              for d, group in enumerate(((qf, kf, vf, licf, bcf, lirf, brf, hf_ref),
                                         (qb, kb, vb, licb, bcb, lirb, brb, hb_ref)))
              for h in range(ML_HEADS)]
    for _ in range(5):
        for ch in chains:
            next(ch)
    for h_ref, u, d, h, hs, h_out, c_new, n_new, m_new in stores:
        h_ref[u, :, hs] = h_out
        c_ref[u, d, h] = c_new
        n_ref[u, d, h:h + 1, :] = n_new
        m_ref[u, d, h:h + 1, :] = m_new


def _mlstm(proj, lic, bc, lir, br, *, row0, n_seq, seq_len, init=None):
    nc = seq_len // CHUNK
    par = MLSTM_PAR
    g0 = row0 // seq_len // par
    seqs = lambda a: a.reshape(N_TOK // seq_len, seq_len, a.shape[-1])
    rows_of = lambda a: seqs(a.T).transpose(0, 2, 1)

    def chunk_specs(chunk):
        return ([pl.BlockSpec((par, CHUNK, ML_W), lambda b, j, c=c: (g0 + b, chunk(j), c)) for c in range(3)]
                + [pl.BlockSpec((par, CHUNK, LANES), lambda b, j: (g0 + b, chunk(j), 0))] * 2
                + [pl.BlockSpec((par, N_GATES, CHUNK), lambda b, j: (g0 + b, 0, chunk(j)))] * 2)

    fwd, bwd = (lambda j: j), (lambda j: nc - 1 - j)
    st_c = pl.BlockSpec((par, 2, ML_HEADS, ML_HEAD_DIM, ML_HEAD_DIM), lambda b, j: (b, 0, 0, 0, 0))
    st_n = pl.BlockSpec((par, 2, ML_HEADS, ML_HEAD_DIM), lambda b, j: (b, 0, 0, 0))
    in_specs = chunk_specs(fwd) + chunk_specs(bwd)
    args = [seqs(proj)] * 3 + [seqs(lic), seqs(bc), rows_of(lir), rows_of(br)]
    args = args * 2
    if init is not None:
        in_specs += [st_c, st_n, st_n]
        args += list(init)
    h_shape = jax.ShapeDtypeStruct((n_seq, seq_len, ML_W), F32)
    hf, hb, c_st, n_st, m_st = pl.pallas_call(
        functools.partial(_mlstm_body, has_init=init is not None),
        out_shape=(h_shape, h_shape,
                   jax.ShapeDtypeStruct((n_seq, 2, ML_HEADS, ML_HEAD_DIM, ML_HEAD_DIM), F32),
                   jax.ShapeDtypeStruct((n_seq, 2, ML_HEADS, ML_HEAD_DIM), F32),
                   jax.ShapeDtypeStruct((n_seq, 2, ML_HEADS, ML_HEAD_DIM), F32)),
        grid=(n_seq // par, nc),
        in_specs=in_specs,
        out_specs=(pl.BlockSpec((par, CHUNK, ML_W), lambda b, j: (b, fwd(j), 0)),
                   pl.BlockSpec((par, CHUNK, ML_W), lambda b, j: (b, bwd(j), 0)),
                   st_c, st_n, st_n),
        compiler_params=_params(2),
        name="mlstm_scan",
    )(*args)
    return hf.reshape(-1, ML_W), hb.reshape(-1, ML_W), c_st, n_st, m_st


def _ml_post_body(hfc_ref, hfs_ref, hbc_ref, hbs_ref, o_ref, g_ref, out_ref):
    h = _pick_split(hfc_ref, hfs_ref) + _pick_split(hbc_ref, hbs_ref)
    gate = jax.nn.sigmoid(o_ref[...]) * g_ref[...]
    for hd in range(ML_HEADS):
        hs = slice(hd * ML_HEAD_DIM, (hd + 1) * ML_HEAD_DIM)
        x = h[:, hs]
        xc = x - jnp.mean(x, axis=-1, keepdims=True)
        var = jnp.mean(xc * xc, axis=-1, keepdims=True)
        out_ref[:, hs] = (gate[:, hs] * (xc * lax.rsqrt(var + RMS_EPS))).astype(BF16)


def _ml_post(hf, hb, proj, norm_g):
    tm = 512
    blk = pl.BlockSpec((tm, ML_W), lambda i: (i, 0))
    return pl.pallas_call(
        _ml_post_body,
        out_shape=jax.ShapeDtypeStruct((N_TOK, ML_W), BF16),
        grid=(N_TOK // tm,),
        in_specs=(_split_specs(tm, ML_W) * 2
                  + [pl.BlockSpec((tm, ML_W), lambda i: (i, 3)), pl.BlockSpec((1, ML_W), lambda i: (0, 0))]),
        out_specs=blk,
        compiler_params=_params(1),
        name="mlstm_out_norm",
    )(*hf, *hb, proj, norm_g.reshape(1, ML_W))


def _head_rms(x, gain):
    lane_head = lax.broadcasted_iota(jnp.int32, x.shape, 1) // ATT_HEAD_DIM
    sq = x * x
    ms = jnp.zeros_like(x)
    for hd in range(x.shape[1] // ATT_HEAD_DIM):
        sel = lane_head == hd
        ms = jnp.where(sel, jnp.sum(jnp.where(sel, sq, 0.0), axis=-1, keepdims=True), ms)
    return x * lax.rsqrt(ms * (1.0 / ATT_HEAD_DIM) + RMS_EPS) * gain


def _rope(x, cos, sin_signed):
    w = x.shape[1]
    even = lax.broadcasted_iota(jnp.int32, x.shape, 1) % 2 == 0
    partner = jnp.where(even, pltpu.roll(x, w - 1, 1), pltpu.roll(x, 1, 1))
    return x * cos + partner * sin_signed


def _qk_prep_body(q_ref, k_ref, qg_ref, kg_ref, *refs, rope):
    if rope:
        cq_ref, sq_ref, ck_ref, sk_ref, qo_ref, kn_ref, kr_ref = refs
    else:
        qo_ref, kn_ref, kr_ref = refs
    q = _head_rms(q_ref[...], qg_ref[...])
    k = _head_rms(k_ref[...], kg_ref[...])
    kn_ref[...] = k
    if rope:
        q = _rope(q, cq_ref[...], sq_ref[...])
        k = _rope(k, ck_ref[...], sk_ref[...])
    qo_ref[...] = (q * (ATT_HEAD_DIM ** -0.5)).astype(BF16)
    kr_ref[...] = k.astype(BF16)


def _qk_prep(proj, q_gain, k_gain, *, row0, rows, rope_tabs=None):
    tm = 512
    r0 = row0 // tm
    in_specs = [pl.BlockSpec((tm, ATT_W), lambda i: (r0 + i, 4 * ML_W // ATT_W)),
                pl.BlockSpec((tm, KV_W), lambda i: (r0 + i, (4 * ML_W + ATT_W) // KV_W)),
                pl.BlockSpec((1, ATT_W), lambda i: (0, 0)),
                pl.BlockSpec((1, KV_W), lambda i: (0, 0))]
    args = [proj, proj, jnp.tile(q_gain, ATT_HEADS).reshape(1, ATT_W), jnp.tile(k_gain, ATT_KV_HEADS).reshape(1, KV_W)]
    if rope_tabs is not None:
        per_seq = DEC_SEQ // tm
        in_specs += [pl.BlockSpec((tm, ATT_W), lambda i: (i % per_seq, 0))] * 2
        in_specs += [pl.BlockSpec((tm, KV_W), lambda i: (i % per_seq, 0))] * 2
        args += list(rope_tabs)
    return pl.pallas_call(
        functools.partial(_qk_prep_body, rope=rope_tabs is not None),
        out_shape=(jax.ShapeDtypeStruct((rows, ATT_W), BF16), jax.ShapeDtypeStruct((rows, KV_W), F32),
                   jax.ShapeDtypeStruct((rows, KV_W), BF16)),
        grid=(rows // tm,),
        in_specs=in_specs,
        out_specs=(pl.BlockSpec((tm, ATT_W), lambda i: (i, 0)), pl.BlockSpec((tm, KV_W), lambda i: (i, 0)),
                   pl.BlockSpec((tm, KV_W), lambda i: (i, 0))),
        compiler_params=_params(1),
        name="attn_qk_prep",
    )(*args)


def _rope_tables():
    rows = DEC_SEQ // GRID_W
    axis_dim = ATT_HEAD_DIM // 2
    row = jnp.repeat(jnp.arange(rows, dtype=F32), GRID_W)
    col = (jnp.arange(DEC_SEQ) % GRID_W).astype(F32)
    inv = ROPE_BASE ** (-jnp.arange(axis_dim // 2, dtype=F32) * 2.0 / axis_dim)
    ang = jnp.concatenate([row[:, None] * inv, col[:, None] * inv], axis=-1)
    cos = jnp.repeat(jnp.cos(ang), 2, axis=-1)
    sin = jnp.repeat(jnp.sin(ang), 2, axis=-1) * jnp.tile(jnp.array([-1.0, 1.0], F32), axis_dim)
    return (jnp.tile(cos, (1, ATT_HEADS)), jnp.tile(sin, (1, ATT_HEADS)),
            jnp.tile(cos, (1, ATT_KV_HEADS)), jnp.tile(sin, (1, ATT_KV_HEADS)))


def _attn_body(q_ref, k_ref, v_ref, o_ref):
    k, v = k_ref[0, 0], v_ref[0, 0]
    dh = k.shape[0]
    g = q_ref.shape[1] // dh
    pair = 2

    def heads(h0):
        q = jnp.concatenate([q_ref[:, h * dh:(h + 1) * dh] for h in range(h0, h0 + pair)], axis=0)
        s = jnp.dot(q, k, preferred_element_type=F32)
        yield
        e = jnp.exp((s - jnp.max(s, axis=-1, keepdims=True)).astype(BF16))
        yield
        o = jnp.dot(e, v, preferred_element_type=F32)
        o = (o[:, :dh] / o[:, dh:dh + 1]).astype(BF16)
        tq = q_ref.shape[0]
        for j in range(pair):
            o_ref[:, (h0 + j) * dh:(h0 + j + 1) * dh] = o[j * tq:(j + 1) * tq]
        yield

    chains = [heads(h0) for h0 in range(0, g, pair)]
    for step in range(len(chains) + 2):
        for i, chain in enumerate(chains):
            if 0 <= step - i < 3:
                next(chain)


def _attention(q, k_t, v_ones, *, tq):
    n_seq, _, dh, s_len = k_t.shape
    nq = q.shape[0] // n_seq // tq
    qo = pl.BlockSpec((tq, GROUP_W), lambda b, kh, i: (b * nq + i, kh))
    return pl.pallas_call(
        _attn_body,
        out_shape=jax.ShapeDtypeStruct(q.shape, BF16),
        grid=(n_seq, ATT_KV_HEADS, nq),
        in_specs=[qo, pl.BlockSpec((1, 1, dh, s_len), lambda b, kh, i: (b, kh, 0, 0)),
                  pl.BlockSpec((1, 1, s_len, LANES), lambda b, kh, i: (b, kh, 0, 0))],
        out_specs=qo,
        compiler_params=_params(3),
        name="attention",
    )(q, k_t, v_ones)


def _head_major(x, n_seq):
    return x.reshape(n_seq, -1, x.shape[1] // ATT_HEAD_DIM, ATT_HEAD_DIM).transpose(0, 2, 1, 3)


def _with_ones(v):
    pad = jnp.zeros(v.shape[:-1] + (LANES - ATT_HEAD_DIM - 1,), v.dtype)
    return jnp.concatenate([v, jnp.ones(v.shape[:-1] + (1,), v.dtype), pad], axis=-1)


def _hy_filter_body(feat_ref, t_ref, w1_ref, b1_ref, w2_ref, b2_ref, fr_ref, w3f_ref, w3b_ref, dl_ref,
                    hsum_ref, hdiff_ref, nyq_ref, z_ref):
    @pl.when(pl.program_id(0) == 0)
    def _():
        z = jnp.dot(feat_ref[...], w1_ref[...], preferred_element_type=F32, precision=HIGHEST) + b1_ref[...]
        z = jnp.sin(fr_ref[0:1, :] * z)
        z = jnp.dot(z, w2_ref[...], preferred_element_type=F32, precision=HIGHEST) + b2_ref[...]
        z_ref[...] = jnp.sin(fr_ref[1:2, :] * z)

    z = z_ref[...]
    window = jnp.exp(-t_ref[...] * dl_ref[...])
    h_f = jnp.dot(z, w3f_ref[...], preferred_element_type=F32, precision=HIGHEST) * window
    h_b = jnp.dot(z, w3b_ref[...], preferred_element_type=F32, precision=HIGHEST) * window
    row = lax.broadcasted_iota(jnp.int32, h_f.shape, 0)
    h_b = jnp.where(row == 0, 0.0, h_b)
    inv = 1.0 / (jnp.sum(jnp.abs(h_f), axis=0, keepdims=True) + jnp.sum(jnp.abs(h_b), axis=0, keepdims=True))
    h_sum = (h_f + h_b) * inv
    hsum_ref[...] = h_sum
    hdiff_ref[...] = (h_f - h_b) * inv
    nyq_ref[...] = jnp.sum(jnp.where(row % 2 == 0, h_sum, -h_sum), axis=0, keepdims=True)


def _hy_filter(seq_len, w1, b1, w2, b2, w3, sin_freq):
    tc = 256
    fw = w1.shape[1]
    t = jnp.arange(seq_len, dtype=F32)[:, None] / seq_len
    bands = jnp.arange(1, HY_BANDS + 1, dtype=F32)[None, :]
    feat = jnp.concatenate([t, jnp.sin(2.0 * math.pi * bands * t), jnp.cos(2.0 * math.pi * bands * t)], axis=-1)
    feat = jnp.pad(feat, ((0, 0), (0, LANES - HY_EMB)))
    deltas = jnp.abs(jnp.linspace(math.log(HY_TARGET) / HY_LONG_PCT, math.log(HY_TARGET) / HY_SHORT_PCT, D,
                                  dtype=F32)).reshape(1, D)
    pad_w = LANES - fw
    full = lambda shape: pl.BlockSpec(shape, lambda j: (0,) * len(shape))
    return pl.pallas_call(
        _hy_filter_body,
        out_shape=(jax.ShapeDtypeStruct((seq_len, D), F32), jax.ShapeDtypeStruct((seq_len, D), F32),
                   jax.ShapeDtypeStruct((1, D), F32)),
        grid=(D // tc,),
        in_specs=[full((seq_len, LANES)), full((seq_len, 1)), full((LANES, LANES)), full((1, LANES)),
                  full((LANES, LANES)), full((1, LANES)), full((2, LANES)),
                  pl.BlockSpec((LANES, tc), lambda j: (0, j)), pl.BlockSpec((LANES, tc), lambda j: (0, D // tc + j)),
                  pl.BlockSpec((1, tc), lambda j: (0, j))],
        out_specs=(pl.BlockSpec((seq_len, tc), lambda j: (0, j)), pl.BlockSpec((seq_len, tc), lambda j: (0, j)),
                   pl.BlockSpec((1, tc), lambda j: (0, j))),
        scratch_shapes=[pltpu.VMEM((seq_len, LANES), F32)],
        compiler_params=_params(1),
        name="hyena_filter",
    )(feat, t, jnp.pad(w1, ((0, LANES - HY_EMB), (0, pad_w))), jnp.pad(b1, (0, pad_w)).reshape(1, LANES),
      jnp.pad(w2, ((0, pad_w), (0, pad_w))), jnp.pad(b2, (0, pad_w)).reshape(1, LANES),
      jnp.pad(sin_freq, ((0, 0), (0, pad_w))), jnp.pad(w3, ((0, pad_w), (0, 0))), jnp.pad(w3, ((0, pad_w), (0, 0))),
      deltas)


def _dft_matrices(seq_len):
    n = 2 * seq_len
    k = lax.broadcasted_iota(jnp.int32, (seq_len, seq_len), 0)
    t = lax.broadcasted_iota(jnp.int32, (seq_len, seq_len), 1)
    ang = ((k * t) % n).astype(F32) * (2.0 * math.pi / n)
    cr, base = jnp.cos(ang), -jnp.sin(ang)
    ci = jnp.where(k == 0, (1 - 2 * (t % 2)).astype(F32), base)
    cit = jnp.where(t == 0, (1 - 2 * (k % 2)).astype(F32), base)
    return cr.astype(BF16), ci.astype(BF16), cit.astype(BF16)


def _dft_fwd_body(cr_ref, ci_ref, b1_ref, b2_ref, *refs, nk, tf, mult):
    if mult:
        kr_ref, ki_ref, or_ref, oi_ref, accr_ref, acci_ref = refs
    else:
        or_ref, oi_ref, accr_ref, acci_ref = refs
    kk = pl.program_id(3)
    pr = jnp.dot(cr_ref[...], b1_ref[0].astype(BF16), preferred_element_type=F32)
    pi = jnp.dot(ci_ref[...], b2_ref[0].astype(BF16), preferred_element_type=F32)

    @pl.when(kk == 0)
    def _():
        accr_ref[...] = pr
        acci_ref[...] = pi

    @pl.when(kk > 0)
    def _():
        accr_ref[...] += pr
        acci_ref[...] += pi

    @pl.when(kk == nk - 1)
    def _():
        zr, zi = accr_ref[...], acci_ref[...]
        if mult:
            kr, ki = kr_ref[...], ki_ref[...]
            first = (pl.program_id(1) * tf + lax.broadcasted_iota(jnp.int32, zr.shape, 0)) == 0
            or_ref[0] = jnp.where(first, 0.5 * zr * kr, zr * kr - zi * ki).astype(or_ref.dtype)
            oi_ref[0] = jnp.where(first, 0.5 * zi * ki, zr * ki + zi * kr).astype(oi_ref.dtype)
        else:
            or_ref[0] = zr
            oi_ref[0] = zi


def _dft_fwd(mats, b1, b2, *, row0, n_seq, seq_len, tf, tk, tn, filt=None):
    cr, ci, _ = mats
    nk = seq_len // tk
    r0 = row0 // tk
    a_spec = pl.BlockSpec((tf, tk), lambda b, f, c, kk: (f, kk))
    b_spec = pl.BlockSpec((1, tk, tn), lambda b, f, c, kk: (0, r0 + b * nk + kk, c))
    o_spec = pl.BlockSpec((1, tf, tn), lambda b, f, c, kk: (b, f, c))
    in_specs = [a_spec] * 2 + [b_spec] * 2
    args = [cr, ci, b1[None], b2[None]]
    if filt is not None:
        in_specs += [pl.BlockSpec((tf, tn), lambda b, f, c, kk: (f, c))] * 2
        args += list(filt)
    shape = jax.ShapeDtypeStruct((n_seq, seq_len, D), F32 if filt is None else BF16)
    return pl.pallas_call(
        functools.partial(_dft_fwd_body, nk=nk, tf=tf, mult=filt is not None),
        out_shape=(shape, shape),
        grid=(n_seq, seq_len // tf, D // tn, nk),
        in_specs=in_specs,
        out_specs=(o_spec, o_spec),
        scratch_shapes=[pltpu.VMEM((tf, tn), F32), pltpu.VMEM((tf, tn), F32)],
        compiler_params=_params(4),
        name="hyena_dft",
    )(*args)


def _dft_inv_body(cr_ref, ct_ref, yr_ref, yi_ref, z_ref, x0_ref, skip_ref, o_ref, acc_ref, *, nk, scale):
    kk = pl.program_id(3)
    p = (jnp.dot(cr_ref[...], yr_ref[0], preferred_element_type=F32)
         + jnp.dot(ct_ref[...], yi_ref[0], preferred_element_type=F32))

    @pl.when(kk == 0)
    def _():
        acc_ref[...] = p

    @pl.when(kk > 0)
    def _():
        acc_ref[...] += p

    @pl.when(kk == nk - 1)
    def _():
        z = z_ref[...]
        o_ref[...] = ((acc_ref[...] * scale + z * skip_ref[...]) * x0_ref[...]).astype(BF16)


def _dft_inv(mats, yr, yi, z, x0, skip, *, row0, n_seq, seq_len, tm, tk, tn):
    cr, _, cit = mats
    nk = seq_len // tk
    nt = seq_len // tm
    r0 = row0 // tm
    a_spec = pl.BlockSpec((tm, tk), lambda b, i, c, kk: (i, kk))
    y_spec = pl.BlockSpec((1, tk, tn), lambda b, i, c, kk: (b, kk, c))
    tok = pl.BlockSpec((tm, tn), lambda b, i, c, kk: (r0 + b * nt + i, c))
    return pl.pallas_call(
        functools.partial(_dft_inv_body, nk=nk, scale=1.0 / seq_len),
        out_shape=jax.ShapeDtypeStruct((n_seq * seq_len, D), BF16),
        grid=(n_seq, nt, D // tn, nk),
        in_specs=[a_spec] * 2 + [y_spec] * 2 + [tok, tok, pl.BlockSpec((1, tn), lambda b, i, c, kk: (0, c))],
        out_specs=pl.BlockSpec((tm, tn), lambda b, i, c, kk: (b * nt + i, c)),
        scratch_shapes=[pltpu.VMEM((tm, tn), F32)],
        compiler_params=_params(4),
        name="hyena_idft",
    )(cr, cit, yr, yi, z, x0, skip.reshape(1, D))


def _hy_gate_body(x0_ref, x1_ref, v_ref, w0_ref, w1_ref, wv_ref, b0_ref, b1_ref, bv_ref, z_ref, x0o_ref, *, seq_len):
    def conv(u_ref, w_ref, b_ref):
        u = u_ref[...]
        n = u.shape[0]
        pos = lax.broadcasted_iota(jnp.int32, u.shape, 0) % seq_len
        prev = jnp.where(pos == 0, 0.0, pltpu.roll(u, 1, 0))
        nxt = jnp.where(pos == seq_len - 1, 0.0, pltpu.roll(u, n - 1, 0))
        return prev * w_ref[0:1, :] + u * w_ref[1:2, :] + nxt * w_ref[2:3, :] + b_ref[...]

    x0o_ref[...] = conv(x0_ref, w0_ref, b0_ref)
    z_ref[...] = conv(v_ref, wv_ref, bv_ref) * conv(x1_ref, w1_ref, b1_ref)


def _hy_gate(u, conv_w, conv_b, *, row0, n_seq, seq_len, seqs_per_step):
    tc = 128
    nb = D // tc
    rows = seqs_per_step * seq_len
    r0 = row0 // rows

    def col(part):
        return [pl.BlockSpec((rows, tc), lambda b, c: (r0 + b, part * nb + c)),
                pl.BlockSpec((3, tc), lambda b, c: (0, part * nb + c)),
                pl.BlockSpec((1, tc), lambda b, c: (0, part * nb + c))]

    specs = [col(p) for p in range(3)]
    out = pl.BlockSpec((rows, tc), lambda b, c: (b, c))
    shape = jax.ShapeDtypeStruct((n_seq * seq_len, D), F32)
    return pl.pallas_call(
        functools.partial(_hy_gate_body, seq_len=seq_len),
        out_shape=(shape, shape),
        grid=(n_seq // seqs_per_step, nb),
        in_specs=[s[0] for s in specs] + [s[1] for s in specs] + [s[2] for s in specs],
        out_specs=(out, out),
        compiler_params=_params(2),
        name="hyena_short_conv",
    )(u, u, u, conv_w, conv_w, conv_w, conv_b.reshape(1, 3 * D), conv_b.reshape(1, 3 * D), conv_b.reshape(1, 3 * D))


FFT_A, FFT_R = 64, 64
FFT_M = 2 * FFT_A
FFT_H = FFT_R // 2
assert FFT_A * FFT_R == DEC_SEQ


def _fft_consts():
    n = 2 * DEC_SEQ
    th = 2.0 * np.pi * (np.arange(FFT_M)[:, None] + 0.5) * np.arange(FFT_A)[None, :] / FFT_M
    f1 = np.concatenate([np.cos(th), -np.sin(th)], axis=0)
    k = np.arange(FFT_M)[:, None, None] + FFT_M * np.arange(FFT_H)[None, :, None] + 0.5
    ph = 2.0 * np.pi * k * np.arange(FFT_R)[None, None, :] / n
    c, s = np.cos(ph), np.sin(ph)
    g = np.concatenate([np.concatenate([c, s], axis=2), np.concatenate([-s, c], axis=2)], axis=1)
    as_bf16 = lambda m: jnp.asarray(m, dtype=F32).astype(BF16)
    return as_bf16(f1), as_bf16(f1.T), as_bf16(g), as_bf16(g.transpose(0, 2, 1))


def _fft_stage1(src_ref, y_ref, f1):
    for b in range(FFT_R):
        zb = src_ref[pl.ds(b, FFT_A, stride=FFT_R), :].astype(BF16)
        y_ref[b * 2 * FFT_M:(b + 1) * 2 * FFT_M, :] = jnp.dot(f1, zb, preferred_element_type=F32)


def _fft_stage2(y_ref, g_ref, k1, part=None):
    yr = y_ref[pl.ds(k1, FFT_R, stride=2 * FFT_M), :]
    yi = y_ref[pl.ds(FFT_M + k1, FFT_R, stride=2 * FFT_M), :]
    y = jnp.concatenate([yr, yi], axis=0).astype(BF16)
    if part is not None:
        return jnp.dot(g_ref[k1, part * FFT_H:(part + 1) * FFT_H, :], y, preferred_element_type=F32)
    z = jnp.dot(g_ref[k1], y, preferred_element_type=F32)
    return z[:FFT_H], z[FFT_H:]


def _hy_spectrum_body(hs_ref, hd_ref, f1_ref, g_ref, kr_ref, ki_ref, y_ref):
    f1 = f1_ref[...]
    _fft_stage1(hs_ref, y_ref, f1)
    for k1 in range(FFT_M):
        kr_ref[k1 * FFT_H:(k1 + 1) * FFT_H, :] = _fft_stage2(y_ref, g_ref, k1, part=0)
    _fft_stage1(hd_ref, y_ref, f1)
    for k1 in range(FFT_M):
        ki_ref[k1 * FFT_H:(k1 + 1) * FFT_H, :] = _fft_stage2(y_ref, g_ref, k1, part=1)


def _hy_spectrum(h_sum, h_diff, consts):
    tc = 128
    f1, _, g, _ = consts
    blk = pl.BlockSpec((DEC_SEQ, tc), lambda c: (0, c))
    shape = jax.ShapeDtypeStruct((DEC_SEQ, D), F32)
    return pl.pallas_call(
        _hy_spectrum_body,
        out_shape=(shape, shape),
        grid=(D // tc,),
        in_specs=[blk, blk, pl.BlockSpec(f1.shape, lambda c: (0, 0)), pl.BlockSpec(g.shape, lambda c: (0, 0, 0))],
        out_specs=(blk, blk),
        scratch_shapes=[pltpu.VMEM((FFT_R * 2 * FFT_M, tc), F32)],
        compiler_params=_params(1),
        name="hyena_filter_fft",
    )(h_sum, h_diff, f1, g)


def _hy_conv_fft_body(x0_ref, x1_ref, v_ref, w0_ref, w1_ref, wv_ref, b0_ref, b1_ref, bv_ref, kr_ref, ki_ref,
                      skip_ref, f1_ref, f1t_ref, g_ref, gt_ref, o_ref, z_ref, y_ref, t_ref):
    rows = 512

    def conv(u_ref, w_ref, b_ref, r):
        u = u_ref[r:r + rows, :]
        row = lax.broadcasted_iota(jnp.int32, u.shape, 0)
        before = u_ref[r - 1:r, :] if r > 0 else jnp.zeros_like(u[0:1])
        after = u_ref[r + rows:r + rows + 1, :] if r + rows < DEC_SEQ else jnp.zeros_like(u[0:1])
        prev = jnp.where(row == 0, before, pltpu.roll(u, 1, 0))
        nxt = jnp.where(row == rows - 1, after, pltpu.roll(u, rows - 1, 0))
        return prev * w_ref[0:1, :] + u * w_ref[1:2, :] + nxt * w_ref[2:3, :] + b_ref[...]

    for r in range(0, DEC_SEQ, rows):
        z_ref[r:r + rows, :] = conv(v_ref, wv_ref, bv_ref, r) * conv(x1_ref, w1_ref, b1_ref, r)
    _fft_stage1(z_ref, y_ref, f1_ref[...])
    for k1 in range(FFT_M):
        zr, zi = _fft_stage2(y_ref, g_ref, k1)
        kr = kr_ref[k1 * FFT_H:(k1 + 1) * FFT_H, :]
        ki = ki_ref[k1 * FFT_H:(k1 + 1) * FFT_H, :]
        p = jnp.concatenate([zr * kr - zi * ki, zr * ki + zi * kr], axis=0).astype(BF16)
        u = jnp.dot(gt_ref[k1], p, preferred_element_type=F32)
        y_ref[pl.ds(k1, FFT_R, stride=2 * FFT_M), :] = u[:FFT_R]
        y_ref[pl.ds(FFT_M + k1, FFT_R, stride=2 * FFT_M), :] = u[FFT_R:]
    f1t = f1t_ref[...]
    for b in range(FFT_R):
        yb = jnp.dot(f1t, y_ref[b * 2 * FFT_M:(b + 1) * 2 * FFT_M, :].astype(BF16), preferred_element_type=F32)
        t_ref[pl.ds(b, FFT_A, stride=FFT_R), :] = yb
    for r in range(0, DEC_SEQ, rows):
        y = t_ref[r:r + rows, :] * (1.0 / DEC_SEQ) + z_ref[r:r + rows, :] * skip_ref[...]
        o_ref[r:r + rows, :] = (y * conv(x0_ref, w0_ref, b0_ref, r)).astype(BF16)


def _hy_conv_fft(u, conv_w, conv_b, kr, ki, skip, consts, *, row0, n_seq):
    tc = 128
    nb = D // tc
    r0 = row0 // DEC_SEQ
    f1, f1t, g, gt = consts

    def col(part):
        return [pl.BlockSpec((DEC_SEQ, tc), lambda b, c: (r0 + b, part * nb + c)),
                pl.BlockSpec((3, tc), lambda b, c: (0, part * nb + c)),
                pl.BlockSpec((1, tc), lambda b, c: (0, part * nb + c))]

    specs = [col(p) for p in range(3)]
    chan = pl.BlockSpec((DEC_SEQ, tc), lambda b, c: (0, c))
    const = lambda m: pl.BlockSpec(m.shape, lambda b, c: (0,) * m.ndim)
    cb = conv_b.reshape(1, 3 * D)
    return pl.pallas_call(
        _hy_conv_fft_body,
        out_shape=jax.ShapeDtypeStruct((n_seq * DEC_SEQ, D), BF16),
        grid=(n_seq, nb),
        in_specs=([s[0] for s in specs] + [s[1] for s in specs] + [s[2] for s in specs]
                  + [chan, chan, pl.BlockSpec((1, tc), lambda b, c: (0, c))] + [const(m) for m in consts]),
        out_specs=pl.BlockSpec((DEC_SEQ, tc), lambda b, c: (b, c)),
        scratch_shapes=[pltpu.VMEM((DEC_SEQ, tc), F32), pltpu.VMEM((FFT_R * 2 * FFT_M, tc), F32),
                        pltpu.VMEM((DEC_SEQ, tc), F32)],
        compiler_params=_params(2, 56 * 1024 * 1024),
        name="hyena_conv_fft",
    )(u, u, u, conv_w, conv_w, conv_w, cb, cb, cb, kr, ki, skip.reshape(1, D), f1, f1t, g, gt)


ROW_TILE = D // LANES
ROUTER_TM = 512
EXPERT_TM = 512
N_SLOTS = 2 * N_TOK + N_EXPERTS * EXPERT_TM
N_SLOT_TILES = N_SLOTS // EXPERT_TM
INFO_E1, INFO_E2, INFO_R1, INFO_R2, INFO_W1, INFO_W2 = range(6)


def _to_row_tiles(ref, x):
    rows = x.shape[0]
    for j in range(ROW_TILE):
        ref[pl.ds(j, rows, stride=ROW_TILE), :] = x[:, j * LANES:(j + 1) * LANES]


def _from_row_tiles(ref, rows):
    return jnp.concatenate([ref[pl.ds(j, rows, stride=ROW_TILE), :] for j in range(ROW_TILE)], axis=-1)


def _router_body(x_ref, sh_ref, sc_ref, w_ref, info_ref, incl_ref, cnt_ref):
    @pl.when(pl.program_id(0) == 0)
    def _():
        cnt_ref[...] = jnp.zeros_like(cnt_ref)

    h = _modulate(x_ref, sh_ref, sc_ref)
    logits = jnp.dot(h, w_ref[...], preferred_element_type=F32, precision=HIGHEST)
    lane = lax.broadcasted_iota(jnp.int32, logits.shape, 1).astype(F32)
    logits = jnp.where(lane < N_EXPERTS, logits, -jnp.inf)
    e = jnp.exp(logits - jnp.max(logits, axis=-1, keepdims=True))
    p = e / jnp.sum(e, axis=-1, keepdims=True)
    p1 = jnp.max(p, axis=-1, keepdims=True)
    i1 = jnp.min(jnp.where(p == p1, lane, float(LANES)), axis=-1, keepdims=True)
    rest = jnp.where(lane == i1, -1.0, p)
    p2 = jnp.max(rest, axis=-1, keepdims=True)
    i2 = jnp.min(jnp.where(rest == p2, lane, float(LANES)), axis=-1, keepdims=True)
    total = p1 + p2
    chosen = jnp.where((lane == i1) | (lane == i2), 1.0, 0.0)
    tm = chosen.shape[0]
    earlier = (lax.broadcasted_iota(jnp.int32, (tm, tm), 1) < lax.broadcasted_iota(jnp.int32, (tm, tm), 0))
    rank = jnp.dot(earlier.astype(BF16), chosen.astype(BF16), preferred_element_type=F32) + cnt_ref[...]
    r1 = jnp.sum(jnp.where(lane == i1, rank, 0.0), axis=-1, keepdims=True)
    r2 = jnp.sum(jnp.where(lane == i2, rank, 0.0), axis=-1, keepdims=True)
    cnt_ref[...] += jnp.sum(chosen, axis=0, keepdims=True)
    incl_ref[0] = jnp.broadcast_to(cnt_ref[...], incl_ref.shape[1:])
    info = jnp.zeros_like(p)
    for col, val in ((INFO_E1, i1), (INFO_E2, i2), (INFO_R1, r1), (INFO_R2, r2),
                     (INFO_W1, p1 / total), (INFO_W2, p2 / total)):
        info = jnp.where(lane == col, val, info)
    info_ref[...] = info


def _router(x, sh, sc, w_router):
    tm = ROUTER_TM
    return pl.pallas_call(
        _router_body,
        out_shape=(jax.ShapeDtypeStruct((N_TOK, LANES), F32),
                   jax.ShapeDtypeStruct((N_TOK // tm, 8, LANES), F32)),
        grid=(N_TOK // tm,),
        in_specs=_mod_specs(tm, 1, 0) + [pl.BlockSpec((D, LANES), lambda i: (0, 0))],
        out_specs=(pl.BlockSpec((tm, LANES), lambda i: (i, 0)), pl.BlockSpec((1, 8, LANES), lambda i: (i, 0, 0))),
        scratch_shapes=[pltpu.VMEM((1, LANES), F32)],
        compiler_params=_params(1),
        name="moe_router",
    )(x, sh, sc, jnp.pad(w_router, ((0, 0), (0, LANES - N_EXPERTS))))


def _row_tile(ref, row):
    return ref.at[pl.ds(pl.multiple_of(row * ROW_TILE, ROW_TILE), ROW_TILE)]


def _dispatch_body(pos1_ref, pos2_ref, x_ref, zeros_hbm, xs_hbm, rows_ref, sem, *, tm):
    del zeros_hbm
    base = pl.program_id(0) * tm
    _to_row_tiles(rows_ref, x_ref[...])

    def copies(r):
        src = _row_tile(rows_ref, r)
        return (pltpu.make_async_copy(src, _row_tile(xs_hbm, pos1_ref[base + r]), sem),
                pltpu.make_async_copy(src, _row_tile(xs_hbm, pos2_ref[base + r]), sem))

    def issue(r, carry):
        for queue, cp in enumerate(copies(r)):
            cp.start(priority=queue)
        return carry

    lax.fori_loop(0, tm, issue, 0, unroll=8)
    for _ in range(2):
        pltpu.make_async_copy(rows_ref, xs_hbm.at[pl.ds(0, tm * ROW_TILE)], sem).wait()


def _dispatch(pos1, pos2, x):
    tm = 512
    return pl.pallas_call(
        functools.partial(_dispatch_body, tm=tm),
        out_shape=jax.ShapeDtypeStruct((N_SLOTS * ROW_TILE, LANES), F32),
        grid_spec=pltpu.PrefetchScalarGridSpec(
            num_scalar_prefetch=2, grid=(N_TOK // tm,),
            in_specs=[pl.BlockSpec((tm, D), lambda i, *_: (i, 0)), pl.BlockSpec(memory_space=pl.ANY)],
            out_specs=pl.BlockSpec(memory_space=pl.ANY),
            scratch_shapes=[pltpu.VMEM((tm * ROW_TILE, LANES), F32), pltpu.SemaphoreType.DMA(())]),
        input_output_aliases={3: 0},
        compiler_params=_params(1),
        name="moe_dispatch",
    )(pos1, pos2, x, jnp.zeros((N_SLOTS * ROW_TILE, LANES), F32))


def _new_expert(eid_ref, nv_ref, t):
    tt = jnp.minimum(t, nv_ref[0] - 1)
    return (t == 0) | (eid_ref[tt] != eid_ref[jnp.maximum(tt - 1, 0)])


def _expert_swiglu_body(eid_ref, b1_ref, b2_ref, nv_ref, xs_ref, sh_ref, sc_ref, wg_ref, wu_ref, o_ref,
                        wgb_ref, wub_ref, *, fc):
    t = pl.program_id(1)

    @pl.when(_new_expert(eid_ref, nv_ref, t))
    def _():
        wgb_ref[...] = wg_ref[0].astype(BF16)
        wub_ref[...] = wu_ref[0].astype(BF16)

    @pl.when(t < nv_ref[0])
    def _():
        tm = o_ref.shape[0]
        x = _from_row_tiles(xs_ref, tm)
        slot = t * tm + lax.broadcasted_iota(jnp.int32, (tm, 1), 0)
        in1, in2 = slot >= b1_ref[t], slot >= b2_ref[t]
        sc = jnp.where(in2, sc_ref[2], jnp.where(in1, sc_ref[1], sc_ref[0]))
        sh = jnp.where(in2, sh_ref[2], jnp.where(in1, sh_ref[1], sh_ref[0]))
        h = (x * (1.0 + sc) + sh).astype(BF16)
        for c in range(o_ref.shape[1] // fc):
            cs = slice(c * fc, (c + 1) * fc)
            g = jnp.dot(h, wgb_ref[:, cs], preferred_element_type=F32)
            u = jnp.dot(h, wub_ref[:, cs], preferred_element_type=F32)
            o_ref[:, cs] = (g * jax.nn.sigmoid(g) * u).astype(BF16)

    @pl.when(t >= nv_ref[0])
    def _():
        o_ref[...] = jnp.zeros_like(o_ref)


def _expert_swiglu(meta, xs_rt, sh, sc, w_gate, w_up, e0):
    eid, b1, b2, nv = meta
    tm, f = EXPERT_TM, w_gate.shape[2]
    fh = f // 2

    def tile(t, eid, b1, b2, nv):
        return jnp.minimum(t, nv[0] - 1)

    w_spec = pl.BlockSpec((1, D, fh), lambda p, t, eid, b1, b2, nv: (e0 + eid[tile(t, eid, b1, b2, nv)], 0, p))
    mod = pl.BlockSpec((8, 1, D), lambda p, t, *_: (0, 0, 0))
    return pl.pallas_call(
        functools.partial(_expert_swiglu_body, fc=256),
        out_shape=jax.ShapeDtypeStruct((N_SLOTS, f), BF16),
        grid_spec=pltpu.PrefetchScalarGridSpec(
            num_scalar_prefetch=4, grid=(2, N_SLOT_TILES),
            in_specs=[pl.BlockSpec((tm * ROW_TILE, LANES), lambda p, t, *m: (tile(t, *m), 0)), mod, mod,
                      w_spec, w_spec],
            out_specs=pl.BlockSpec((tm, fh), lambda p, t, *m: (t, p)),
            scratch_shapes=[pltpu.VMEM((D, fh), BF16), pltpu.VMEM((D, fh), BF16)]),
        compiler_params=_params(2, 56 * 1024 * 1024),
        name="moe_swiglu",
    )(eid, b1, b2, nv, xs_rt, sh, sc, w_gate, w_up)


def _expert_down_body(eid_ref, nv_ref, a_ref, w_ref, y_ref, wb_ref):
    t = pl.program_id(0)

    @pl.when(_new_expert(eid_ref, nv_ref, t))
    def _():
        wb_ref[...] = w_ref[0].astype(BF16)

    @pl.when(t < nv_ref[0])
    def _():
        _to_row_tiles(y_ref, jnp.dot(a_ref[...], wb_ref[...], preferred_element_type=F32))

    @pl.when(t >= nv_ref[0])
    def _():
        y_ref[...] = jnp.zeros_like(y_ref)


def _expert_down(meta, act, w_down, e0):
    eid, _, _, nv = meta
    tm, f = EXPERT_TM, act.shape[1]

    def tile(t, eid, nv):
        return jnp.minimum(t, nv[0] - 1)

    return pl.pallas_call(
        _expert_down_body,
        out_shape=jax.ShapeDtypeStruct((N_SLOTS * ROW_TILE, LANES), F32),
        grid_spec=pltpu.PrefetchScalarGridSpec(
            num_scalar_prefetch=2, grid=(N_SLOT_TILES,),
            in_specs=[pl.BlockSpec((tm, f), lambda t, *m: (tile(t, *m), 0)),
                      pl.BlockSpec((1, f, D), lambda t, eid, nv: (e0 + eid[tile(t, eid, nv)], 0, 0))],
            out_specs=pl.BlockSpec((tm * ROW_TILE, LANES), lambda t, *m: (t, 0)),
            scratch_shapes=[pltpu.VMEM((f, D), BF16)]),
        compiler_params=_params(1, 58 * 1024 * 1024),
        name="moe_down",
    )(eid, nv, act, w_down)


def _combine_body(pos1_ref, pos2_ref, y_hbm, info_ref, x_ref, g_ref, lng_ref, lnb_ref, o_ref, y1_ref, y2_ref, sem,
                  *, tm):
    base = pl.program_id(0) * tm

    def copies(r):
        return (pltpu.make_async_copy(_row_tile(y_hbm, pos1_ref[base + r]), _row_tile(y1_ref, r), sem),
                pltpu.make_async_copy(_row_tile(y_hbm, pos2_ref[base + r]), _row_tile(y2_ref, r), sem))

    def issue(r, carry):
        for queue, cp in enumerate(copies(r)):
            cp.start(priority=queue)
        return carry

    lax.fori_loop(0, tm, issue, 0, unroll=8)
    for y_ref in (y1_ref, y2_ref):
        pltpu.make_async_copy(y_hbm.at[pl.ds(0, tm * ROW_TILE)], y_ref, sem).wait()
    info = info_ref[...]
    ffn = (info[:, INFO_W1:INFO_W1 + 1] * _from_row_tiles(y1_ref, tm)
           + info[:, INFO_W2:INFO_W2 + 1] * _from_row_tiles(y2_ref, tm))
    y = ALPHA * x_ref[...] + g_ref[0] * ffn
    mu = jnp.mean(y, axis=-1, keepdims=True)
    yc = y - mu
    var = jnp.mean(yc * yc, axis=-1, keepdims=True)
    o_ref[...] = yc * lax.rsqrt(var + LN_EPS) * lng_ref[...] + lnb_ref[...]


def _combine(pos1, pos2, y_rt, info, x, gate, ln_g, ln_b):
    tm = 512
    return pl.pallas_call(
        functools.partial(_combine_body, tm=tm),
        out_shape=jax.ShapeDtypeStruct((N_TOK, D), F32),
        grid_spec=pltpu.PrefetchScalarGridSpec(
            num_scalar_prefetch=2, grid=(N_TOK // tm,),
            in_specs=[pl.BlockSpec(memory_space=pl.ANY),
                      pl.BlockSpec((tm, LANES), lambda i, *_: (i, 0)),
                      pl.BlockSpec((tm, D), lambda i, *_: (i, 0)),
                      pl.BlockSpec((1, 1, D), lambda i, *_: (_group_of_row(i * tm), 0, 0)),
                      pl.BlockSpec((1, D), lambda i, *_: (0, 0)),
                      pl.BlockSpec((1, D), lambda i, *_: (0, 0))],
            out_specs=pl.BlockSpec((tm, D), lambda i, *_: (i, 0)),
            scratch_shapes=[pltpu.VMEM((tm * ROW_TILE, LANES), F32), pltpu.VMEM((tm * ROW_TILE, LANES), F32),
                            pltpu.SemaphoreType.DMA(())]),
        compiler_params=_params(1),
        name="moe_combine",
    )(pos1, pos2, y_rt, info, x, gate, ln_g.reshape(1, D), ln_b.reshape(1, D))


def _slot_plan(info, incl):
    row = lambda n_rows: incl[n_rows // ROUTER_TM - 1, 0, :N_EXPERTS].astype(jnp.int32)
    count = row(N_TOK)
    padded = (count + EXPERT_TM - 1) // EXPERT_TM * EXPERT_TM
    end = jnp.cumsum(padded)
    start = end - padded
    tile_row = jnp.arange(N_SLOT_TILES, dtype=jnp.int32) * EXPERT_TM
    eid = jnp.minimum(jnp.sum(tile_row[:, None] >= end[None, :], axis=1), N_EXPERTS - 1).astype(jnp.int32)
    b1 = (start + row(N_CTX))[eid]
    b2 = (start + row(N_CTX + DEC_SEQ))[eid]
    nv = (end[-1:] // EXPERT_TM).astype(jnp.int32)
    e1, e2 = info[:, INFO_E1].astype(jnp.int32), info[:, INFO_E2].astype(jnp.int32)
    pos1 = start[e1] + info[:, INFO_R1].astype(jnp.int32)
    pos2 = start[e2] + info[:, INFO_R2].astype(jnp.int32)
    return pos1, pos2, (eid, b1, b2, nv)


def _even_mixer(x, sh, sc, gate, ln_g, ln_b, w_in, b_igate, b_fgate, ml_norm_g, q_norm_g, k_norm_g, w_out,
                st_c, st_n, st_m, cache_k, cache_v, rope_tabs):
    splits = (4 * ML_W, 4 * ML_W + N_GATES)
    w_main = jnp.concatenate([w_in[:, :splits[0]], w_in[:, splits[1]:]], axis=1).astype(BF16)
    proj = _mod_matmul(x, sh, sc, [w_main[None]], tm=1024, tn=MAIN_W // 2, out_dtype=F32, name="even_in_proj")[0]
    b_gate = jnp.stack([b_igate, b_fgate], axis=1).reshape(N_GATES)
    lic, bc, lir, br = _gates(x, sh, sc, w_in[:, splits[0]:splits[1]], b_gate)

    hf_c, hb_c, new_c, new_n, new_m = _mlstm(proj, lic, bc, lir, br, row0=0, n_seq=BATCH, seq_len=SEQ)
    init = (st_c, st_n, jnp.broadcast_to(st_m[..., None], st_n.shape))
    hf_s, hb_s, _, _, _ = _mlstm(proj, lic, bc, lir, br, row0=N_CTX, n_seq=DEC_BATCH, seq_len=DEC_SEQ, init=init)
    ml = _ml_post((hf_c, hf_s), (hb_c, hb_s), proj, ml_norm_g)

    q_c, kn_c, kb_c = _qk_prep(proj, q_norm_g, k_norm_g, row0=0, rows=N_CTX)
    q_s, _, kb_s = _qk_prep(proj, q_norm_g, k_norm_g, row0=N_CTX, rows=N_LAT, rope_tabs=rope_tabs)
    v_all = proj[:, MAIN_W - KV_W:]
    v_c, v_s = v_all[:N_CTX], v_all[N_CTX:]
    att_c = _attention(q_c, _head_major(kb_c, BATCH).swapaxes(2, 3),
                       _with_ones(_head_major(v_c.astype(BF16), BATCH)), tq=SEQ)
    k_lat = jnp.concatenate([kb_s.reshape(DEC_BATCH, DEC_SEQ, KV_W),
                             cache_k.reshape(DEC_BATCH, PAST_LEN, KV_W).astype(BF16)], axis=1)
    v_lat = jnp.concatenate([v_s.reshape(DEC_BATCH, DEC_SEQ, KV_W).astype(BF16),
                             cache_v.reshape(DEC_BATCH, PAST_LEN, KV_W).astype(BF16)], axis=1)
    att_s = _attention(q_s, _head_major(k_lat.reshape(-1, KV_W), DEC_BATCH).swapaxes(2, 3),
                       _with_ones(_head_major(v_lat.reshape(-1, KV_W), DEC_BATCH)), tq=256)

    x = _proj_res_ln([ml, (att_c, att_s)], w_out.astype(BF16), x, gate, ln_g, ln_b, tm=512, name="even_out_proj")
    new_k = kn_c.reshape(BATCH, SEQ, ATT_KV_HEADS, ATT_HEAD_DIM)
    new_v = v_c.reshape(BATCH, SEQ, ATT_KV_HEADS, ATT_HEAD_DIM)
    return x, new_k, new_v, new_c, new_n, new_m[..., 0]


def _hyena_mixer(x, sh, sc, gate, ln_g, ln_b, w_in, conv_w, conv_b, w1, b1, w2, b2, w3, sin_freq, skip, w_out, dft):
    u = _mod_matmul(x, sh, sc, [w_in.astype(BF16)[None]], tm=1024, tn=1536, out_dtype=F32, name="hyena_in_proj")[0]
    mats, consts = dft
    tiles = dict(tf=SEQ, tk=SEQ, tn=D)
    h_sum, h_diff, nyq = _hy_filter(SEQ, w1, b1, w2, b2, w3, sin_freq)
    kr, ki = _dft_fwd(mats, h_sum, h_diff, row0=0, n_seq=1, seq_len=SEQ, **tiles)
    kr, ki = kr[0], ki[0].at[0].set(nyq[0])
    z, x0 = _hy_gate(u, conv_w, conv_b, row0=0, n_seq=BATCH, seq_len=SEQ, seqs_per_step=8)
    yr, yi = _dft_fwd(mats, z, z, row0=0, n_seq=BATCH, seq_len=SEQ, filt=(kr, ki), **tiles)
    y_c = _dft_inv(mats, yr, yi, z, x0, skip, row0=0, n_seq=BATCH, seq_len=SEQ, tm=SEQ, tk=SEQ, tn=D)
    h_sum, h_diff, _ = _hy_filter(DEC_SEQ, w1, b1, w2, b2, w3, sin_freq)
    kr, ki = _hy_spectrum(h_sum, h_diff, consts)
    y_s = _hy_conv_fft(u, conv_w, conv_b, kr, ki, skip, consts, row0=N_CTX, n_seq=DEC_BATCH)
    return _proj_res_ln([(y_c, y_s)], w_out.astype(BF16), x, gate, ln_g, ln_b, tm=512, name="hyena_out_proj")


def _dense_ffn(x, sh, sc, gate, ln_g, ln_b, w_gate, w_up, w_down):
    act = _mod_matmul(x, sh, sc, [w_gate.astype(BF16)[None], w_up.astype(BF16)[None]],
                      tm=1024, tn=D_FF // 2, out_dtype=BF16, name="ffn_swiglu")
    return _proj_res_ln([act[0]], w_down.astype(BF16), x, gate, ln_g, ln_b, tm=512, name="ffn_down")


def _moe_ffn(x, sh, sc, gate, ln_g, ln_b, w_router, w_gate, w_up, w_down, e0):
    info, incl = _router(x, sh, sc, w_router)
    pos1, pos2, meta = _slot_plan(info, incl)
    xs_rt = _dispatch(pos1, pos2, x)
    act = _expert_swiglu(meta, xs_rt, sh, sc, w_gate, w_up, e0)
    y_rt = _expert_down(meta, act, w_down, e0)
    return _combine(pos1, pos2, y_rt, info, x, gate, ln_g, ln_b)


def kernel(x_prompt, x_sample, cache_attn_k, cache_attn_v, state_mlstm_C, state_mlstm_n, state_mlstm_m, c, c_ctx, w_ada, b_ada, ln_g, ln_b, w_in_even, b_igate, b_fgate, ml_norm_g, q_norm_g, k_norm_g, w_out_even, w_ffn_gate, w_ffn_up, w_ffn_down, w_in_hy, hy_conv_w, hy_conv_b, hy_filt_w1, hy_filt_b1, hy_filt_w2, hy_filt_b2, hy_filt_w3, hy_sin_freq, hy_skip, w_out_hy, w_router, w_moe_gate, w_moe_up, w_moe_down):
    x = jnp.concatenate([x_prompt.reshape(N_CTX, D), x_sample.reshape(N_LAT, D)])
    cvec = jnp.concatenate([c_ctx[None], c, jnp.zeros((8 - 1 - DEC_BATCH, D), F32)])
    mods = _ada(cvec, w_ada, b_ada)
    rope_tabs = _rope_tables()
    dft = (_dft_matrices(SEQ), _fft_consts())
    moe_w = [w.reshape((-1,) + w.shape[2:]) for w in (w_moe_gate, w_moe_up, w_moe_down)]
    new_k, new_v, new_c, new_n, new_m = [], [], [], [], []
    for layer in range(DEPTH):
        sh1, sc1, g1, sh2, sc2, g2 = (mods[layer, :, i * D:(i + 1) * D].reshape(8, 1, D) for i in range(6))
        i = layer // 2
        if layer % 2 == 0:
            x, k_c, v_c, st_c, st_n, st_m = _even_mixer(
                x, sh1, sc1, g1, ln_g[layer, 0], ln_b[layer, 0], w_in_even[i], b_igate[i], b_fgate[i], ml_norm_g[i],
                q_norm_g[i], k_norm_g[i], w_out_even[i], state_mlstm_C[:, i], state_mlstm_n[:, i], state_mlstm_m[:, i],
                cache_attn_k[:, i], cache_attn_v[:, i], rope_tabs)
            new_k.append(k_c)
            new_v.append(v_c)
            new_c.append(st_c)
            new_n.append(st_n)
            new_m.append(st_m)
            x = _dense_ffn(x, sh2, sc2, g2, ln_g[layer, 1], ln_b[layer, 1], w_ffn_gate[i], w_ffn_up[i], w_ffn_down[i])
        else:
            x = _hyena_mixer(x, sh1, sc1, g1, ln_g[layer, 0], ln_b[layer, 0], w_in_hy[i], hy_conv_w[i], hy_conv_b[i],
                             hy_filt_w1[i], hy_filt_b1[i], hy_filt_w2[i], hy_filt_b2[i], hy_filt_w3[i], hy_sin_freq[i],
                             hy_skip[i], w_out_hy[i], dft)
            x = _moe_ffn(x, sh2, sc2, g2, ln_g[layer, 1], ln_b[layer, 1], w_router[i], *moe_w, i * N_EXPERTS)
    return (x[:N_CTX].reshape(BATCH, SEQ, D), x[N_CTX:].reshape(DEC_BATCH, DEC_SEQ, D),
            jnp.stack(new_k, axis=1), jnp.stack(new_v, axis=1), jnp.stack(new_c, axis=1),
            jnp.stack(new_n, axis=1), jnp.stack(new_m, axis=1))
```

```python
import functools
import math

import jax
import jax.numpy as jnp
import numpy as np
from jax import lax
from jax.experimental import pallas as pl
from jax.experimental.pallas import tpu as pltpu

F32 = jnp.float32
BF16 = jnp.bfloat16
HIGHEST = lax.Precision.HIGHEST

D = 1024
BATCH, SEQ = 32, 256
DEC_BATCH, DEC_SEQ = 2, 4096
DEPTH = 4
PAST_LEN = 256
GRID_W = 64
N_CTX = BATCH * SEQ
N_LAT = DEC_BATCH * DEC_SEQ
N_TOK = N_CTX + N_LAT

ML_HEADS, ML_HEAD_DIM = 4, 128
ML_W = ML_HEADS * ML_HEAD_DIM
CHUNK = 128
MLSTM_PAR = 2
ATT_HEADS, ATT_KV_HEADS, ATT_HEAD_DIM = 8, 2, 64
ATT_GROUP = ATT_HEADS // ATT_KV_HEADS
ATT_W = ATT_HEADS * ATT_HEAD_DIM
KV_W = ATT_KV_HEADS * ATT_HEAD_DIM
GROUP_W = ATT_GROUP * ATT_HEAD_DIM
ROPE_BASE = 10000.0
N_GATES = 4 * ML_HEADS
MAIN_W = 4 * ML_W + ATT_W + 2 * KV_W

HY_EMB = 33
HY_BANDS = (HY_EMB - 1) // 2
HY_TARGET, HY_SHORT_PCT, HY_LONG_PCT = 1e-2, 0.3, 1.5
D_FF = 2816
N_EXPERTS = 8
MOE_D_FF = 3584
ALPHA = (2 * DEPTH) ** 0.25
LN_EPS = 1e-5
RMS_EPS = 1e-6

LANES = 128
VMEM_LIMIT = 48 * 1024 * 1024


def _params(n_axes, vmem=VMEM_LIMIT):
    return pltpu.CompilerParams(dimension_semantics=("arbitrary",) * n_axes, vmem_limit_bytes=vmem)


def _group_of_row(r):
    return jnp.where(r < N_CTX, 0, 1 + (r - N_CTX) // DEC_SEQ)


def _modulate(x_ref, sh_ref, sc_ref):
    return x_ref[...] * (1.0 + sc_ref[0]) + sh_ref[0]


def _mod_specs(tm, row_axis):
    def rows(*ids):
        return (ids[row_axis], 0)

    def grp(*ids):
        return (_group_of_row(ids[row_axis] * tm), 0, 0)

    return [pl.BlockSpec((tm, D), rows), pl.BlockSpec((1, 1, D), grp), pl.BlockSpec((1, 1, D), grp)]


def _ada_body(c_ref, w_ref, b_ref, o_ref):
    c = c_ref[...]
    s = c * jax.nn.sigmoid(c)
    o_ref[0] = jnp.dot(s, w_ref[0], preferred_element_type=F32, precision=HIGHEST) + b_ref[0]


def _ada(cvec, w_ada, b_ada):
    tn = 1536
    return pl.pallas_call(
        _ada_body,
        out_shape=jax.ShapeDtypeStruct((DEPTH, 8, 6 * D), F32),
        grid=(DEPTH, 6 * D // tn),
        in_specs=[pl.BlockSpec((8, D), lambda l, j: (0, 0)),
                  pl.BlockSpec((1, D, tn), lambda l, j: (l, 0, j)),
                  pl.BlockSpec((1, 1, tn), lambda l, j: (l, 0, j))],
        out_specs=pl.BlockSpec((1, 8, tn), lambda l, j: (l, 0, j)),
        compiler_params=_params(2),
        name="ada_modulation",
    )(cvec, w_ada, b_ada.reshape(DEPTH, 1, 6 * D))


def _mod_mm_body(x_ref, sh_ref, sc_ref, *refs, n_w):
    w_refs, o_ref, h_ref = refs[:n_w], refs[n_w], refs[n_w + 1]

    @pl.when(pl.program_id(2) == 0)
    def _():
        h_ref[...] = _modulate(x_ref, sh_ref, sc_ref).astype(BF16)

    h = h_ref[...]
    if n_w == 1:
        o = jnp.dot(h, w_refs[0][0], preferred_element_type=F32)
    else:
        g = jnp.dot(h, w_refs[0][0], preferred_element_type=F32)
        u = jnp.dot(h, w_refs[1][0], preferred_element_type=F32)
        o = g * jax.nn.sigmoid(g) * u
    o_ref[0] = o.astype(o_ref.dtype)


def _mod_matmul(x, sh, sc, ws, *, tm, tn, out_dtype, name):
    n_e, _, f = ws[0].shape
    return pl.pallas_call(
        functools.partial(_mod_mm_body, n_w=len(ws)),
        out_shape=jax.ShapeDtypeStruct((n_e, N_TOK, f), out_dtype),
        grid=(n_e, N_TOK // tm, f // tn),
        in_specs=_mod_specs(tm, 1) + [pl.BlockSpec((1, D, tn), lambda e, i, j: (e, 0, j)) for _ in ws],
        out_specs=pl.BlockSpec((1, tm, tn), lambda e, i, j: (e, i, j)),
        scratch_shapes=[pltpu.VMEM((tm, D), BF16)],
        compiler_params=_params(3),
        name=name,
    )(x, sh, sc, *ws)


def _split_specs(tm, width):
    nc = N_CTX // tm
    return [pl.BlockSpec((tm, width), lambda i: (jnp.minimum(i, nc - 1), 0)),
            pl.BlockSpec((tm, width), lambda i: (jnp.maximum(i - nc, 0), 0))]


def _pick_split(ctx_ref, lat_ref):
    tm = ctx_ref.shape[0]
    return jnp.where(pl.program_id(0) < N_CTX // tm, ctx_ref[...], lat_ref[...])


def _proj_res_ln_body(*refs, split):
    n_in = sum(2 if s else 1 for s in split)
    part_refs = list(refs[:n_in])
    w_ref, x_ref, g_ref, lng_ref, lnb_ref, o_ref = refs[n_in:]
    cols = [_pick_split(part_refs.pop(0), part_refs.pop(0)) if s else part_refs.pop(0)[...] for s in split]
    a = cols[0] if len(cols) == 1 else jnp.concatenate(cols, axis=-1)
    y = ALPHA * x_ref[...] + g_ref[0] * jnp.dot(a, w_ref[...], preferred_element_type=F32)
    mu = jnp.mean(y, axis=-1, keepdims=True)
    yc = y - mu
    var = jnp.mean(yc * yc, axis=-1, keepdims=True)
    o_ref[...] = yc * lax.rsqrt(var + LN_EPS) * lng_ref[...] + lnb_ref[...]


def _proj_res_ln(parts, w, x, gate, ln_g, ln_b, *, tm, name):
    split = tuple(isinstance(p, tuple) for p in parts)
    in_specs, args = [], []
    for p, s in zip(parts, split):
        if s:
            in_specs += _split_specs(tm, p[0].shape[1])
            args += list(p)
        else:
            in_specs.append(pl.BlockSpec((tm, p.shape[1]), lambda i: (i, 0)))
            args.append(p)
    in_specs += [pl.BlockSpec(w.shape, lambda i: (0, 0)),
                 pl.BlockSpec((tm, D), lambda i: (i, 0)),
                 pl.BlockSpec((1, 1, D), lambda i: (_group_of_row(i * tm), 0, 0)),
                 pl.BlockSpec((1, D), lambda i: (0, 0)),
                 pl.BlockSpec((1, D), lambda i: (0, 0))]
    return pl.pallas_call(
        functools.partial(_proj_res_ln_body, split=split),
        out_shape=jax.ShapeDtypeStruct((N_TOK, D), F32),
        grid=(N_TOK // tm,),
        in_specs=in_specs,
        out_specs=pl.BlockSpec((tm, D), lambda i: (i, 0)),
        compiler_params=_params(1),
        name=name,
    )(*args, w, x, gate, ln_g.reshape(1, D), ln_b.reshape(1, D))


def _log_sigmoid(x):
    return jnp.minimum(x, 0.0) - jnp.log(1.0 + jnp.exp(-jnp.abs(x)))


def _gates_body(x_ref, sh_ref, sc_ref, wg_ref, wgt_ref, b_ref, bt_ref,
                lic_ref, bc_ref, lir_ref, br_ref, *, tm):
    h = _modulate(x_ref, sh_ref, sc_ref).astype(BF16)
    g = jnp.dot(h, wg_ref[...], preferred_element_type=F32) + b_ref[...]
    gt = lax.dot_general(wgt_ref[...], h, (((1,), (1,)), ((), ())), preferred_element_type=F32) + bt_ref[...]
    lic_ref[...] = g
    lir_ref[...] = gt
    lf, lft = _log_sigmoid(g), _log_sigmoid(gt)
    r = lax.broadcasted_iota(jnp.int32, (CHUNK, CHUNK), 0)
    c = lax.broadcasted_iota(jnp.int32, (CHUNK, CHUNK), 1)
    tri_l = (c <= r).astype(F32)
    tri_u = (c >= r).astype(F32)
    fwd_col = lax.broadcasted_iota(jnp.int32, (CHUNK, LANES), 1) < 2 * ML_HEADS
    fwd_row = lax.broadcasted_iota(jnp.int32, (N_GATES, CHUNK), 0) < 2 * ML_HEADS
    for ch in range(tm // CHUNK):
        sl = slice(ch * CHUNK, (ch + 1) * CHUNK)
        lfc, lftc = lf[sl, :], lft[:, sl]
        cum_f = jnp.dot(tri_l, lfc, preferred_element_type=F32, precision=HIGHEST)
        cum_b = jnp.dot(tri_u, lfc, preferred_element_type=F32, precision=HIGHEST)
        bc_ref[sl, :] = jnp.where(fwd_col, cum_f, cum_b)
        cum_f = jnp.dot(lftc, tri_u, preferred_element_type=F32, precision=HIGHEST)
        cum_b = jnp.dot(lftc, tri_l, preferred_element_type=F32, precision=HIGHEST)
        br_ref[:, sl] = jnp.where(fwd_row, cum_f, cum_b)


def _gates(x, sh, sc, wg, b_gate):
    tm = 256
    wg_pad = jnp.pad(wg, ((0, 0), (0, LANES - N_GATES)))
    b_pad = jnp.pad(b_gate, (0, LANES - N_GATES)).reshape(1, LANES)
    col = pl.BlockSpec((tm, LANES), lambda i: (i, 0))
    row = pl.BlockSpec((N_GATES, tm), lambda i: (0, i))
    return pl.pallas_call(
        functools.partial(_gates_body, tm=tm),
        out_shape=(jax.ShapeDtypeStruct((N_TOK, LANES), F32), jax.ShapeDtypeStruct((N_TOK, LANES), F32),
                   jax.ShapeDtypeStruct((N_GATES, N_TOK), F32), jax.ShapeDtypeStruct((N_GATES, N_TOK), F32)),
        grid=(N_TOK // tm,),
        in_specs=_mod_specs(tm, 0) + [pl.BlockSpec((D, LANES), lambda i: (0, 0)),
                                         pl.BlockSpec((N_GATES, D), lambda i: (0, 0)),
                                         pl.BlockSpec((1, LANES), lambda i: (0, 0)),
                                         pl.BlockSpec((N_GATES, 1), lambda i: (0, 0))],
        out_specs=(col, col, row, row),
        compiler_params=_params(1),
        name="mlstm_gates",
    )(x, sh, sc, wg_pad.astype(BF16), wg.T.astype(BF16), b_pad, b_gate.reshape(N_GATES, 1))


def _mlstm_body(*refs, has_init):
    (qf, kf, vf, licf, bcf, lirf, brf, qb, kb, vb, licb, bcb, lirb, brb) = refs[:14]
    refs = refs[14:]
    if has_init:
        c0_ref, n0_ref, m0_ref = refs[:3]
        refs = refs[3:]
    hf_ref, hb_ref, c_ref, n_ref, m_ref = refs

    @pl.when(pl.program_id(1) == 0)
    def _():
        if has_init:
            c_ref[...] = c0_ref[...]
            n_ref[...] = n0_ref[...]
            m_ref[...] = m0_ref[...]
        else:
            c_ref[...] = jnp.zeros_like(c_ref)
            n_ref[...] = jnp.zeros_like(n_ref)
            m_ref[...] = jnp.zeros_like(m_ref)

    t_idx = lax.broadcasted_iota(jnp.int32, (CHUNK, CHUNK), 0)
    s_idx = lax.broadcasted_iota(jnp.int32, (CHUNK, CHUNK), 1)
    nt = (((1,), (1,)), ((), ()))
    stores = []

    def chain(u, d, h, q_ref, k_ref, v_ref, lic_ref, bc_ref, lir_ref, br_ref, h_ref):
        mask = (s_idx <= t_idx) if d == 0 else (s_idx >= t_idx)
        hs = slice(h * ML_HEAD_DIM, (h + 1) * ML_HEAD_DIM)
        gi, gf = d * 2 * ML_HEADS + h, d * 2 * ML_HEADS + ML_HEADS + h
        q = q_ref[u, :, hs]
        k = k_ref[u, :, hs] * (ML_HEAD_DIM ** -0.5)
        v = v_ref[u, :, hs]
        qh, kh, vh = q.astype(BF16), k.astype(BF16), v.astype(BF16)
        li_c, b_c = lic_ref[u, :, gi:gi + 1], bc_ref[u, :, gf:gf + 1]
        li_r, b_r = lir_ref[u, gi:gi + 1, :], br_ref[u, gf:gf + 1, :]
        c_st = c_ref[u, d, h]
        n_st = n_ref[u, d, h:h + 1, :]
        m_st = m_ref[u, d, h:h + 1, :][:, 0:1]
        dmat = jnp.where(mask, b_c - b_r + li_r, -jnp.inf)
        inter = b_c + m_st
        m_out = jnp.maximum(inter, jnp.max(dmat, axis=-1, keepdims=True))
        p = jnp.exp(dmat - m_out)
        w_inter = jnp.exp(inter - m_out)
        yield
        qk = lax.dot_general(qh, kh, nt, preferred_element_type=F32)
        qc = jnp.dot(qh, c_st.astype(BF16), preferred_element_type=F32)
        yield
        s = qk * p
        den = (jnp.sum(s, axis=-1, keepdims=True)
               + w_inter * jnp.sum(q * n_st, axis=-1, keepdims=True))
        sh = s.astype(BF16)
        b_last = b_r[:, CHUNK - 1:CHUNK] if d == 0 else b_r[:, 0:1]
        g_r = b_last - b_r + li_r
        g_c = b_last - b_c + li_c
        m_new = jnp.maximum(b_last + m_st, jnp.max(g_r, axis=-1, keepdims=True))
        decay = jnp.exp(b_last + m_st - m_new)
        kw = k * jnp.exp(g_c - m_new)
        kwh = kw.astype(BF16)
        yield
        sv = jnp.dot(sh, vh, preferred_element_type=F32)
        kv = lax.dot_general(kwh, vh, (((0,), (0,)), ((), ())), preferred_element_type=F32)
        yield
        h_out = (sv + w_inter * qc) / jnp.maximum(jnp.abs(den), jnp.exp(-m_out))
        c_new = decay * c_st + kv
        n_new = decay * n_st + jnp.sum(kw, axis=0, keepdims=True)
        stores.append((h_ref, u, d, h, hs, h_out, c_new, n_new, jnp.broadcast_to(m_new, (1, ML_HEAD_DIM))))
        yield

    chains = [chain(u, d, h, *group)
              for u in range(c_ref.shape[0])
              for d, group in enumerate(((qf, kf, vf, licf, bcf, lirf, brf, hf_ref),
                                         (qb, kb, vb, licb, bcb, lirb, brb, hb_ref)))
              for h in range(ML_HEADS)]
    for _ in range(5):
        for ch in chains:
            next(ch)
    for h_ref, u, d, h, hs, h_out, c_new, n_new, m_new in stores:
        h_ref[u, :, hs] = h_out
        c_ref[u, d, h] = c_new
        n_ref[u, d, h:h + 1, :] = n_new
        m_ref[u, d, h:h + 1, :] = m_new


def _mlstm(proj, lic, bc, lir, br, *, row0, n_seq, seq_len, init=None):
    nc = seq_len // CHUNK
    par = MLSTM_PAR
    g0 = row0 // seq_len // par
    seqs = lambda a: a.reshape(N_TOK // seq_len, seq_len, a.shape[-1])
    rows_of = lambda a: seqs(a.T).transpose(0, 2, 1)

    def chunk_specs(chunk):
        return ([pl.BlockSpec((par, CHUNK, ML_W), lambda b, j, c=c: (g0 + b, chunk(j), c)) for c in range(3)]
                + [pl.BlockSpec((par, CHUNK, LANES), lambda b, j: (g0 + b, chunk(j), 0))] * 2
                + [pl.BlockSpec((par, N_GATES, CHUNK), lambda b, j: (g0 + b, 0, chunk(j)))] * 2)

    fwd, bwd = (lambda j: j), (lambda j: nc - 1 - j)
    st_c = pl.BlockSpec((par, 2, ML_HEADS, ML_HEAD_DIM, ML_HEAD_DIM), lambda b, j: (b, 0, 0, 0, 0))
    st_n = pl.BlockSpec((par, 2, ML_HEADS, ML_HEAD_DIM), lambda b, j: (b, 0, 0, 0))
    in_specs = chunk_specs(fwd) + chunk_specs(bwd)
    args = [seqs(proj)] * 3 + [seqs(lic), seqs(bc), rows_of(lir), rows_of(br)]
    args = args * 2
    if init is not None:
        in_specs += [st_c, st_n, st_n]
        args += list(init)
    h_shape = jax.ShapeDtypeStruct((n_seq, seq_len, ML_W), F32)
    hf, hb, c_st, n_st, m_st = pl.pallas_call(
        functools.partial(_mlstm_body, has_init=init is not None),
        out_shape=(h_shape, h_shape,
                   jax.ShapeDtypeStruct((n_seq, 2, ML_HEADS, ML_HEAD_DIM, ML_HEAD_DIM), F32),
                   jax.ShapeDtypeStruct((n_seq, 2, ML_HEADS, ML_HEAD_DIM), F32),
                   jax.ShapeDtypeStruct((n_seq, 2, ML_HEADS, ML_HEAD_DIM), F32)),
        grid=(n_seq // par, nc),
        in_specs=in_specs,
        out_specs=(pl.BlockSpec((par, CHUNK, ML_W), lambda b, j: (b, fwd(j), 0)),
                   pl.BlockSpec((par, CHUNK, ML_W), lambda b, j: (b, bwd(j), 0)),
                   st_c, st_n, st_n),
        compiler_params=_params(2),
        name="mlstm_scan",
    )(*args)
    return hf.reshape(-1, ML_W), hb.reshape(-1, ML_W), c_st, n_st, m_st


def _ml_post_body(hfc_ref, hfs_ref, hbc_ref, hbs_ref, o_ref, g_ref, out_ref):
    h = _pick_split(hfc_ref, hfs_ref) + _pick_split(hbc_ref, hbs_ref)
    gate = jax.nn.sigmoid(o_ref[...]) * g_ref[...]
    for hd in range(ML_HEADS):
        hs = slice(hd * ML_HEAD_DIM, (hd + 1) * ML_HEAD_DIM)
        x = h[:, hs]
        xc = x - jnp.mean(x, axis=-1, keepdims=True)
        var = jnp.mean(xc * xc, axis=-1, keepdims=True)
        out_ref[:, hs] = (gate[:, hs] * (xc * lax.rsqrt(var + RMS_EPS))).astype(BF16)


def _ml_post(hf, hb, proj, norm_g):
    tm = 512
    blk = pl.BlockSpec((tm, ML_W), lambda i: (i, 0))
    return pl.pallas_call(
        _ml_post_body,
        out_shape=jax.ShapeDtypeStruct((N_TOK, ML_W), BF16),
        grid=(N_TOK // tm,),
        in_specs=(_split_specs(tm, ML_W) * 2
                  + [pl.BlockSpec((tm, ML_W), lambda i: (i, 3)), pl.BlockSpec((1, ML_W), lambda i: (0, 0))]),
        out_specs=blk,
        compiler_params=_params(1),
        name="mlstm_out_norm",
    )(*hf, *hb, proj, norm_g.reshape(1, ML_W))


def _head_rms(x, gain):
    lane_head = lax.broadcasted_iota(jnp.int32, x.shape, 1) // ATT_HEAD_DIM
    sq = x * x
    ms = jnp.zeros_like(x)
    for hd in range(x.shape[1] // ATT_HEAD_DIM):
        sel = lane_head == hd
        ms = jnp.where(sel, jnp.sum(jnp.where(sel, sq, 0.0), axis=-1, keepdims=True), ms)
    return x * lax.rsqrt(ms * (1.0 / ATT_HEAD_DIM) + RMS_EPS) * gain


def _rope(x, cos, sin_signed):
    w = x.shape[1]
    even = lax.broadcasted_iota(jnp.int32, x.shape, 1) % 2 == 0
    partner = jnp.where(even, pltpu.roll(x, w - 1, 1), pltpu.roll(x, 1, 1))
    return x * cos + partner * sin_signed


def _qk_prep_body(q_ref, k_ref, qg_ref, kg_ref, *refs, rope):
    if rope:
        cq_ref, sq_ref, ck_ref, sk_ref, qo_ref, kn_ref, kr_ref = refs
    else:
        qo_ref, kn_ref, kr_ref = refs
    q = _head_rms(q_ref[...], qg_ref[...])
    k = _head_rms(k_ref[...], kg_ref[...])
    kn_ref[...] = k
    if rope:
        q = _rope(q, cq_ref[...], sq_ref[...])
        k = _rope(k, ck_ref[...], sk_ref[...])
    qo_ref[...] = (q * (ATT_HEAD_DIM ** -0.5)).astype(BF16)
    kr_ref[...] = k.astype(BF16)


def _qk_prep(proj, q_gain, k_gain, *, row0, rows, rope_tabs=None):
    tm = 512
    r0 = row0 // tm
    in_specs = [pl.BlockSpec((tm, ATT_W), lambda i: (r0 + i, 4 * ML_W // ATT_W)),
                pl.BlockSpec((tm, KV_W), lambda i: (r0 + i, (4 * ML_W + ATT_W) // KV_W)),
                pl.BlockSpec((1, ATT_W), lambda i: (0, 0)),
                pl.BlockSpec((1, KV_W), lambda i: (0, 0))]
    args = [proj, proj, jnp.tile(q_gain, ATT_HEADS).reshape(1, ATT_W), jnp.tile(k_gain, ATT_KV_HEADS).reshape(1, KV_W)]
    if rope_tabs is not None:
        per_seq = DEC_SEQ // tm
        in_specs += [pl.BlockSpec((tm, ATT_W), lambda i: (i % per_seq, 0))] * 2
        in_specs += [pl.BlockSpec((tm, KV_W), lambda i: (i % per_seq, 0))] * 2
        args += list(rope_tabs)
    return pl.pallas_call(
        functools.partial(_qk_prep_body, rope=rope_tabs is not None),
        out_shape=(jax.ShapeDtypeStruct((rows, ATT_W), BF16), jax.ShapeDtypeStruct((rows, KV_W), F32),
                   jax.ShapeDtypeStruct((rows, KV_W), BF16)),
        grid=(rows // tm,),
        in_specs=in_specs,
        out_specs=(pl.BlockSpec((tm, ATT_W), lambda i: (i, 0)), pl.BlockSpec((tm, KV_W), lambda i: (i, 0)),
                   pl.BlockSpec((tm, KV_W), lambda i: (i, 0))),
        compiler_params=_params(1),
        name="attn_qk_prep",
    )(*args)


def _rope_tables():
    rows = DEC_SEQ // GRID_W
    axis_dim = ATT_HEAD_DIM // 2
    row = jnp.repeat(jnp.arange(rows, dtype=F32), GRID_W)
    col = (jnp.arange(DEC_SEQ) % GRID_W).astype(F32)
    inv = ROPE_BASE ** (-jnp.arange(axis_dim // 2, dtype=F32) * 2.0 / axis_dim)
    ang = jnp.concatenate([row[:, None] * inv, col[:, None] * inv], axis=-1)
    cos = jnp.repeat(jnp.cos(ang), 2, axis=-1)
    sin = jnp.repeat(jnp.sin(ang), 2, axis=-1) * jnp.tile(jnp.array([-1.0, 1.0], F32), axis_dim)
    return (jnp.tile(cos, (1, ATT_HEADS)), jnp.tile(sin, (1, ATT_HEADS)),
            jnp.tile(cos, (1, ATT_KV_HEADS)), jnp.tile(sin, (1, ATT_KV_HEADS)))


def _attn_body(q_ref, k_ref, v_ref, o_ref):
    k, v = k_ref[0, 0], v_ref[0, 0]
    dh = k.shape[0]
    g = q_ref.shape[1] // dh
    pair = 2

    def heads(h0):
        q = jnp.concatenate([q_ref[:, h * dh:(h + 1) * dh] for h in range(h0, h0 + pair)], axis=0)
        s = jnp.dot(q, k, preferred_element_type=F32)
        yield
        e = jnp.exp((s - jnp.max(s, axis=-1, keepdims=True)).astype(BF16))
        yield
        o = jnp.dot(e, v, preferred_element_type=F32)
        o = (o[:, :dh] / o[:, dh:dh + 1]).astype(BF16)
        tq = q_ref.shape[0]
        for j in range(pair):
            o_ref[:, (h0 + j) * dh:(h0 + j + 1) * dh] = o[j * tq:(j + 1) * tq]
        yield

    chains = [heads(h0) for h0 in range(0, g, pair)]
    for step in range(len(chains) + 2):
        for i, chain in enumerate(chains):
            if 0 <= step - i < 3:
                next(chain)


def _attention(q, k_t, v_ones, *, tq):
    n_seq, _, dh, s_len = k_t.shape
    nq = q.shape[0] // n_seq // tq
    qo = pl.BlockSpec((tq, GROUP_W), lambda b, kh, i: (b * nq + i, kh))
    return pl.pallas_call(
        _attn_body,
        out_shape=jax.ShapeDtypeStruct(q.shape, BF16),
        grid=(n_seq, ATT_KV_HEADS, nq),
        in_specs=[qo, pl.BlockSpec((1, 1, dh, s_len), lambda b, kh, i: (b, kh, 0, 0)),
                  pl.BlockSpec((1, 1, s_len, LANES), lambda b, kh, i: (b, kh, 0, 0))],
        out_specs=qo,
        compiler_params=_params(3),
        name="attention",
    )(q, k_t, v_ones)


def _head_major(x, n_seq):
    return x.reshape(n_seq, -1, x.shape[1] // ATT_HEAD_DIM, ATT_HEAD_DIM).transpose(0, 2, 1, 3)


def _with_ones(v):
    pad = jnp.zeros(v.shape[:-1] + (LANES - ATT_HEAD_DIM - 1,), v.dtype)
    return jnp.concatenate([v, jnp.ones(v.shape[:-1] + (1,), v.dtype), pad], axis=-1)


def _hy_filter_body(feat_ref, t_ref, w1_ref, b1_ref, w2_ref, b2_ref, fr_ref, w3f_ref, w3b_ref, dl_ref,
                    hsum_ref, hdiff_ref, nyq_ref, z_ref):
    @pl.when(pl.program_id(0) == 0)
    def _():
        z = jnp.dot(feat_ref[...], w1_ref[...], preferred_element_type=F32, precision=HIGHEST) + b1_ref[...]
        z = jnp.sin(fr_ref[0:1, :] * z)
        z = jnp.dot(z, w2_ref[...], preferred_element_type=F32, precision=HIGHEST) + b2_ref[...]
        z_ref[...] = jnp.sin(fr_ref[1:2, :] * z)

    z = z_ref[...]
    window = jnp.exp(-t_ref[...] * dl_ref[...])
    h_f = jnp.dot(z, w3f_ref[...], preferred_element_type=F32, precision=HIGHEST) * window
    h_b = jnp.dot(z, w3b_ref[...], preferred_element_type=F32, precision=HIGHEST) * window
    row = lax.broadcasted_iota(jnp.int32, h_f.shape, 0)
    h_b = jnp.where(row == 0, 0.0, h_b)
    inv = 1.0 / (jnp.sum(jnp.abs(h_f), axis=0, keepdims=True) + jnp.sum(jnp.abs(h_b), axis=0, keepdims=True))
    h_sum = (h_f + h_b) * inv
    hsum_ref[...] = h_sum
    hdiff_ref[...] = (h_f - h_b) * inv
    nyq_ref[...] = jnp.sum(jnp.where(row % 2 == 0, h_sum, -h_sum), axis=0, keepdims=True)


def _hy_filter(seq_len, w1, b1, w2, b2, w3, sin_freq):
    tc = 256
    fw = w1.shape[1]
    t = jnp.arange(seq_len, dtype=F32)[:, None] / seq_len
    bands = jnp.arange(1, HY_BANDS + 1, dtype=F32)[None, :]
    feat = jnp.concatenate([t, jnp.sin(2.0 * math.pi * bands * t), jnp.cos(2.0 * math.pi * bands * t)], axis=-1)
    feat = jnp.pad(feat, ((0, 0), (0, LANES - HY_EMB)))
    deltas = jnp.abs(jnp.linspace(math.log(HY_TARGET) / HY_LONG_PCT, math.log(HY_TARGET) / HY_SHORT_PCT, D,
                                  dtype=F32)).reshape(1, D)
    pad_w = LANES - fw
    full = lambda shape: pl.BlockSpec(shape, lambda j: (0,) * len(shape))
    return pl.pallas_call(
        _hy_filter_body,
        out_shape=(jax.ShapeDtypeStruct((seq_len, D), F32), jax.ShapeDtypeStruct((seq_len, D), F32),
                   jax.ShapeDtypeStruct((1, D), F32)),
        grid=(D // tc,),
        in_specs=[full((seq_len, LANES)), full((seq_len, 1)), full((LANES, LANES)), full((1, LANES)),
                  full((LANES, LANES)), full((1, LANES)), full((2, LANES)),
                  pl.BlockSpec((LANES, tc), lambda j: (0, j)), pl.BlockSpec((LANES, tc), lambda j: (0, D // tc + j)),
                  pl.BlockSpec((1, tc), lambda j: (0, j))],
        out_specs=(pl.BlockSpec((seq_len, tc), lambda j: (0, j)), pl.BlockSpec((seq_len, tc), lambda j: (0, j)),
                   pl.BlockSpec((1, tc), lambda j: (0, j))),
        scratch_shapes=[pltpu.VMEM((seq_len, LANES), F32)],
        compiler_params=_params(1),
        name="hyena_filter",
    )(feat, t, jnp.pad(w1, ((0, LANES - HY_EMB), (0, pad_w))), jnp.pad(b1, (0, pad_w)).reshape(1, LANES),
      jnp.pad(w2, ((0, pad_w), (0, pad_w))), jnp.pad(b2, (0, pad_w)).reshape(1, LANES),
      jnp.pad(sin_freq, ((0, 0), (0, pad_w))), jnp.pad(w3, ((0, pad_w), (0, 0))), jnp.pad(w3, ((0, pad_w), (0, 0))),
      deltas)


def _dft_matrices(seq_len):
    n = 2 * seq_len
    k = lax.broadcasted_iota(jnp.int32, (seq_len, seq_len), 0)
    t = lax.broadcasted_iota(jnp.int32, (seq_len, seq_len), 1)
    ang = ((k * t) % n).astype(F32) * (2.0 * math.pi / n)
    cr, base = jnp.cos(ang), -jnp.sin(ang)
    ci = jnp.where(k == 0, (1 - 2 * (t % 2)).astype(F32), base)
    cit = jnp.where(t == 0, (1 - 2 * (k % 2)).astype(F32), base)
    return cr.astype(BF16), ci.astype(BF16), cit.astype(BF16)


def _dft_fwd_body(cr_ref, ci_ref, b1_ref, b2_ref, *refs, nk, tf, mult):
    if mult:
        kr_ref, ki_ref, or_ref, oi_ref, accr_ref, acci_ref = refs
    else:
        or_ref, oi_ref, accr_ref, acci_ref = refs
    kk = pl.program_id(3)
    pr = jnp.dot(cr_ref[...], b1_ref[0].astype(BF16), preferred_element_type=F32)
    pi = jnp.dot(ci_ref[...], b2_ref[0].astype(BF16), preferred_element_type=F32)

    @pl.when(kk == 0)
    def _():
        accr_ref[...] = pr
        acci_ref[...] = pi

    @pl.when(kk > 0)
    def _():
        accr_ref[...] += pr
        acci_ref[...] += pi

    @pl.when(kk == nk - 1)
    def _():
        zr, zi = accr_ref[...], acci_ref[...]
        if mult:
            kr, ki = kr_ref[...], ki_ref[...]
            first = (pl.program_id(1) * tf + lax.broadcasted_iota(jnp.int32, zr.shape, 0)) == 0
            or_ref[0] = jnp.where(first, 0.5 * zr * kr, zr * kr - zi * ki).astype(or_ref.dtype)
            oi_ref[0] = jnp.where(first, 0.5 * zi * ki, zr * ki + zi * kr).astype(oi_ref.dtype)
        else:
            or_ref[0] = zr
            oi_ref[0] = zi


def _dft_fwd(mats, b1, b2, *, row0, n_seq, seq_len, tf, tk, tn, filt=None):
    cr, ci, _ = mats
    nk = seq_len // tk
    r0 = row0 // tk
    a_spec = pl.BlockSpec((tf, tk), lambda b, f, c, kk: (f, kk))
    b_spec = pl.BlockSpec((1, tk, tn), lambda b, f, c, kk: (0, r0 + b * nk + kk, c))
    o_spec = pl.BlockSpec((1, tf, tn), lambda b, f, c, kk: (b, f, c))
    in_specs = [a_spec] * 2 + [b_spec] * 2
    args = [cr, ci, b1[None], b2[None]]
    if filt is not None:
        in_specs += [pl.BlockSpec((tf, tn), lambda b, f, c, kk: (f, c))] * 2
        args += list(filt)
    shape = jax.ShapeDtypeStruct((n_seq, seq_len, D), F32 if filt is None else BF16)
    return pl.pallas_call(
        functools.partial(_dft_fwd_body, nk=nk, tf=tf, mult=filt is not None),
        out_shape=(shape, shape),
        grid=(n_seq, seq_len // tf, D // tn, nk),
        in_specs=in_specs,
        out_specs=(o_spec, o_spec),
        scratch_shapes=[pltpu.VMEM((tf, tn), F32), pltpu.VMEM((tf, tn), F32)],
        compiler_params=_params(4),
        name="hyena_dft",
    )(*args)


def _dft_inv_body(cr_ref, ct_ref, yr_ref, yi_ref, z_ref, x0_ref, skip_ref, o_ref, acc_ref, *, nk, scale):
    kk = pl.program_id(3)
    p = (jnp.dot(cr_ref[...], yr_ref[0], preferred_element_type=F32)
         + jnp.dot(ct_ref[...], yi_ref[0], preferred_element_type=F32))

    @pl.when(kk == 0)
    def _():
        acc_ref[...] = p

    @pl.when(kk > 0)
    def _():
        acc_ref[...] += p

    @pl.when(kk == nk - 1)
    def _():
        z = z_ref[...]
        o_ref[...] = ((acc_ref[...] * scale + z * skip_ref[...]) * x0_ref[...]).astype(BF16)


def _dft_inv(mats, yr, yi, z, x0, skip, *, row0, n_seq, seq_len, tm, tk, tn):
    cr, _, cit = mats
    nk = seq_len // tk
    nt = seq_len // tm
    r0 = row0 // tm
    a_spec = pl.BlockSpec((tm, tk), lambda b, i, c, kk: (i, kk))
    y_spec = pl.BlockSpec((1, tk, tn), lambda b, i, c, kk: (b, kk, c))
    tok = pl.BlockSpec((tm, tn), lambda b, i, c, kk: (r0 + b * nt + i, c))
    return pl.pallas_call(
        functools.partial(_dft_inv_body, nk=nk, scale=1.0 / seq_len),
        out_shape=jax.ShapeDtypeStruct((n_seq * seq_len, D), BF16),
        grid=(n_seq, nt, D // tn, nk),
        in_specs=[a_spec] * 2 + [y_spec] * 2 + [tok, tok, pl.BlockSpec((1, tn), lambda b, i, c, kk: (0, c))],
        out_specs=pl.BlockSpec((tm, tn), lambda b, i, c, kk: (b * nt + i, c)),
        scratch_shapes=[pltpu.VMEM((tm, tn), F32)],
        compiler_params=_params(4),
        name="hyena_idft",
    )(cr, cit, yr, yi, z, x0, skip.reshape(1, D))


def _hy_gate_body(x0_ref, x1_ref, v_ref, w0_ref, w1_ref, wv_ref, b0_ref, b1_ref, bv_ref, z_ref, x0o_ref, *, seq_len):
    def conv(u_ref, w_ref, b_ref):
        u = u_ref[...]
        n = u.shape[0]
        pos = lax.broadcasted_iota(jnp.int32, u.shape, 0) % seq_len
        prev = jnp.where(pos == 0, 0.0, pltpu.roll(u, 1, 0))
        nxt = jnp.where(pos == seq_len - 1, 0.0, pltpu.roll(u, n - 1, 0))
        return prev * w_ref[0:1, :] + u * w_ref[1:2, :] + nxt * w_ref[2:3, :] + b_ref[...]

    x0o_ref[...] = conv(x0_ref, w0_ref, b0_ref)
    z_ref[...] = conv(v_ref, wv_ref, bv_ref) * conv(x1_ref, w1_ref, b1_ref)


def _hy_gate(u, conv_w, conv_b, *, row0, n_seq, seq_len, seqs_per_step):
    tc = 128
    nb = D // tc
    rows = seqs_per_step * seq_len
    r0 = row0 // rows

    def col(part):
        return [pl.BlockSpec((rows, tc), lambda b, c: (r0 + b, part * nb + c)),
                pl.BlockSpec((3, tc), lambda b, c: (0, part * nb + c)),
                pl.BlockSpec((1, tc), lambda b, c: (0, part * nb + c))]

    specs = [col(p) for p in range(3)]
    out = pl.BlockSpec((rows, tc), lambda b, c: (b, c))
    shape = jax.ShapeDtypeStruct((n_seq * seq_len, D), F32)
    return pl.pallas_call(
        functools.partial(_hy_gate_body, seq_len=seq_len),
        out_shape=(shape, shape),
        grid=(n_seq // seqs_per_step, nb),
        in_specs=[s[0] for s in specs] + [s[1] for s in specs] + [s[2] for s in specs],
        out_specs=(out, out),
        compiler_params=_params(2),
        name="hyena_short_conv",
    )(u, u, u, conv_w, conv_w, conv_w, conv_b.reshape(1, 3 * D), conv_b.reshape(1, 3 * D), conv_b.reshape(1, 3 * D))


FFT_A, FFT_R = 64, 64
FFT_M = 2 * FFT_A
FFT_H = FFT_R // 2
assert FFT_A * FFT_R == DEC_SEQ


def _fft_consts():
    n = 2 * DEC_SEQ
    th = 2.0 * np.pi * (np.arange(FFT_M)[:, None] + 0.5) * np.arange(FFT_A)[None, :] / FFT_M
    f1 = np.concatenate([np.cos(th), -np.sin(th)], axis=0)
    k = np.arange(FFT_M)[:, None, None] + FFT_M * np.arange(FFT_H)[None, :, None] + 0.5
    ph = 2.0 * np.pi * k * np.arange(FFT_R)[None, None, :] / n
    c, s = np.cos(ph), np.sin(ph)
    g = np.concatenate([np.concatenate([c, s], axis=2), np.concatenate([-s, c], axis=2)], axis=1)
    as_bf16 = lambda m: jnp.asarray(m, dtype=F32).astype(BF16)
    return as_bf16(f1), as_bf16(f1.T), as_bf16(g), as_bf16(g.transpose(0, 2, 1))


def _fft_stage1(src_ref, y_ref, f1):
    for b in range(FFT_R):
        zb = src_ref[pl.ds(b, FFT_A, stride=FFT_R), :].astype(BF16)
        y_ref[b * 2 * FFT_M:(b + 1) * 2 * FFT_M, :] = jnp.dot(f1, zb, preferred_element_type=F32)


def _fft_stage2(y_ref, g_ref, k1, part=None):
    yr = y_ref[pl.ds(k1, FFT_R, stride=2 * FFT_M), :]
    yi = y_ref[pl.ds(FFT_M + k1, FFT_R, stride=2 * FFT_M), :]
    y = jnp.concatenate([yr, yi], axis=0).astype(BF16)
    if part is not None:
        return jnp.dot(g_ref[k1, part * FFT_H:(part + 1) * FFT_H, :], y, preferred_element_type=F32)
    z = jnp.dot(g_ref[k1], y, preferred_element_type=F32)
    return z[:FFT_H], z[FFT_H:]


def _hy_spectrum_body(hs_ref, hd_ref, f1_ref, g_ref, kr_ref, ki_ref, y_ref):
    f1 = f1_ref[...]
    _fft_stage1(hs_ref, y_ref, f1)
    for k1 in range(FFT_M):
        kr_ref[k1 * FFT_H:(k1 + 1) * FFT_H, :] = _fft_stage2(y_ref, g_ref, k1, part=0)
    _fft_stage1(hd_ref, y_ref, f1)
    for k1 in range(FFT_M):
        ki_ref[k1 * FFT_H:(k1 + 1) * FFT_H, :] = _fft_stage2(y_ref, g_ref, k1, part=1)


def _hy_spectrum(h_sum, h_diff, consts):
    tc = 128
    f1, _, g, _ = consts
    blk = pl.BlockSpec((DEC_SEQ, tc), lambda c: (0, c))
    shape = jax.ShapeDtypeStruct((DEC_SEQ, D), F32)
    return pl.pallas_call(
        _hy_spectrum_body,
        out_shape=(shape, shape),
        grid=(D // tc,),
        in_specs=[blk, blk, pl.BlockSpec(f1.shape, lambda c: (0, 0)), pl.BlockSpec(g.shape, lambda c: (0, 0, 0))],
        out_specs=(blk, blk),
        scratch_shapes=[pltpu.VMEM((FFT_R * 2 * FFT_M, tc), F32)],
        compiler_params=_params(1),
        name="hyena_filter_fft",
    )(h_sum, h_diff, f1, g)


def _hy_conv_fft_body(x0_ref, x1_ref, v_ref, w0_ref, w1_ref, wv_ref, b0_ref, b1_ref, bv_ref, kr_ref, ki_ref,
                      skip_ref, f1_ref, f1t_ref, g_ref, gt_ref, o_ref, z_ref, y_ref, t_ref):
    rows = 512

    def conv(u_ref, w_ref, b_ref, r):
        u = u_ref[r:r + rows, :]
        row = lax.broadcasted_iota(jnp.int32, u.shape, 0)
        before = u_ref[r - 1:r, :] if r > 0 else jnp.zeros_like(u[0:1])
        after = u_ref[r + rows:r + rows + 1, :] if r + rows < DEC_SEQ else jnp.zeros_like(u[0:1])
        prev = jnp.where(row == 0, before, pltpu.roll(u, 1, 0))
        nxt = jnp.where(row == rows - 1, after, pltpu.roll(u, rows - 1, 0))
        return prev * w_ref[0:1, :] + u * w_ref[1:2, :] + nxt * w_ref[2:3, :] + b_ref[...]

    for r in range(0, DEC_SEQ, rows):
        z_ref[r:r + rows, :] = conv(v_ref, wv_ref, bv_ref, r) * conv(x1_ref, w1_ref, b1_ref, r)
    _fft_stage1(z_ref, y_ref, f1_ref[...])
    for k1 in range(FFT_M):
        zr, zi = _fft_stage2(y_ref, g_ref, k1)
        kr = kr_ref[k1 * FFT_H:(k1 + 1) * FFT_H, :]
        ki = ki_ref[k1 * FFT_H:(k1 + 1) * FFT_H, :]
        p = jnp.concatenate([zr * kr - zi * ki, zr * ki + zi * kr], axis=0).astype(BF16)
        u = jnp.dot(gt_ref[k1], p, preferred_element_type=F32)
        y_ref[pl.ds(k1, FFT_R, stride=2 * FFT_M), :] = u[:FFT_R]
        y_ref[pl.ds(FFT_M + k1, FFT_R, stride=2 * FFT_M), :] = u[FFT_R:]
    f1t = f1t_ref[...]
    for b in range(FFT_R):
        yb = jnp.dot(f1t, y_ref[b * 2 * FFT_M:(b + 1) * 2 * FFT_M, :].astype(BF16), preferred_element_type=F32)
        t_ref[pl.ds(b, FFT_A, stride=FFT_R), :] = yb
    for r in range(0, DEC_SEQ, rows):
        y = t_ref[r:r + rows, :] * (1.0 / DEC_SEQ) + z_ref[r:r + rows, :] * skip_ref[...]
        o_ref[r:r + rows, :] = (y * conv(x0_ref, w0_ref, b0_ref, r)).astype(BF16)


def _hy_conv_fft(u, conv_w, conv_b, kr, ki, skip, consts, *, row0, n_seq):
    tc = 128
    nb = D // tc
    r0 = row0 // DEC_SEQ
    f1, f1t, g, gt = consts

    def col(part):
        return [pl.BlockSpec((DEC_SEQ, tc), lambda b, c: (r0 + b, part * nb + c)),
                pl.BlockSpec((3, tc), lambda b, c: (0, part * nb + c)),
                pl.BlockSpec((1, tc), lambda b, c: (0, part * nb + c))]

    specs = [col(p) for p in range(3)]
    chan = pl.BlockSpec((DEC_SEQ, tc), lambda b, c: (0, c))
    const = lambda m: pl.BlockSpec(m.shape, lambda b, c: (0,) * m.ndim)
    cb = conv_b.reshape(1, 3 * D)
    return pl.pallas_call(
        _hy_conv_fft_body,
        out_shape=jax.ShapeDtypeStruct((n_seq * DEC_SEQ, D), BF16),
        grid=(n_seq, nb),
        in_specs=([s[0] for s in specs] + [s[1] for s in specs] + [s[2] for s in specs]
                  + [chan, chan, pl.BlockSpec((1, tc), lambda b, c: (0, c))] + [const(m) for m in consts]),
        out_specs=pl.BlockSpec((DEC_SEQ, tc), lambda b, c: (b, c)),
        scratch_shapes=[pltpu.VMEM((DEC_SEQ, tc), F32), pltpu.VMEM((FFT_R * 2 * FFT_M, tc), F32),
                        pltpu.VMEM((DEC_SEQ, tc), F32)],
        compiler_params=_params(2, 56 * 1024 * 1024),
        name="hyena_conv_fft",
    )(u, u, u, conv_w, conv_w, conv_w, cb, cb, cb, kr, ki, skip.reshape(1, D), f1, f1t, g, gt)


ROW_TILE = D // LANES
ROUTER_TM = 512
EXPERT_TM = 512
N_SLOTS = 2 * N_TOK + N_EXPERTS * EXPERT_TM
N_SLOT_TILES = N_SLOTS // EXPERT_TM
INFO_E1, INFO_E2, INFO_R1, INFO_R2, INFO_W1, INFO_W2 = range(6)


def _to_row_tiles(ref, x):
    rows = x.shape[0]
    for j in range(ROW_TILE):
        ref[pl.ds(j, rows, stride=ROW_TILE), :] = x[:, j * LANES:(j + 1) * LANES]


def _from_row_tiles(ref, rows):
    return jnp.concatenate([ref[pl.ds(j, rows, stride=ROW_TILE), :] for j in range(ROW_TILE)], axis=-1)


def _router_body(x_ref, sh_ref, sc_ref, w_ref, info_ref, incl_ref, cnt_ref):
    @pl.when(pl.program_id(0) == 0)
    def _():
        cnt_ref[...] = jnp.zeros_like(cnt_ref)

    h = _modulate(x_ref, sh_ref, sc_ref)
    h_hi, w = h.astype(BF16), w_ref[...]
    h_lo, w_hi = (h - h_hi.astype(F32)).astype(BF16), w.astype(BF16)
    w_lo = (w - w_hi.astype(F32)).astype(BF16)
    logits = (jnp.dot(h_hi, w_hi, preferred_element_type=F32) + jnp.dot(h_lo, w_hi, preferred_element_type=F32)
              + jnp.dot(h_hi, w_lo, preferred_element_type=F32))
    lane = lax.broadcasted_iota(jnp.int32, logits.shape, 1).astype(F32)
    logits = jnp.where(lane < N_EXPERTS, logits, -jnp.inf)
    e = jnp.exp(logits - jnp.max(logits, axis=-1, keepdims=True))
    p = e / jnp.sum(e, axis=-1, keepdims=True)
    p1 = jnp.max(p, axis=-1, keepdims=True)
    i1 = jnp.min(jnp.where(p == p1, lane, float(LANES)), axis=-1, keepdims=True)
    rest = jnp.where(lane == i1, -1.0, p)
    p2 = jnp.max(rest, axis=-1, keepdims=True)
    i2 = jnp.min(jnp.where(rest == p2, lane, float(LANES)), axis=-1, keepdims=True)
    total = p1 + p2
    chosen = jnp.where((lane == i1) | (lane == i2), 1.0, 0.0)
    tm = chosen.shape[0]
    earlier = (lax.broadcasted_iota(jnp.int32, (tm, tm), 1) < lax.broadcasted_iota(jnp.int32, (tm, tm), 0))
    rank = jnp.dot(earlier.astype(BF16), chosen.astype(BF16), preferred_element_type=F32) + cnt_ref[...]
    r1 = jnp.sum(jnp.where(lane == i1, rank, 0.0), axis=-1, keepdims=True)
    r2 = jnp.sum(jnp.where(lane == i2, rank, 0.0), axis=-1, keepdims=True)
    cnt_ref[...] += jnp.sum(chosen, axis=0, keepdims=True)
    incl_ref[0] = jnp.broadcast_to(cnt_ref[...], incl_ref.shape[1:])
    info = jnp.zeros_like(p)
    for col, val in ((INFO_E1, i1), (INFO_E2, i2), (INFO_R1, r1), (INFO_R2, r2),
                     (INFO_W1, p1 / total), (INFO_W2, p2 / total)):
        info = jnp.where(lane == col, val, info)
    info_ref[...] = info


def _router(x, sh, sc, w_router):
    tm = ROUTER_TM
    return pl.pallas_call(
        _router_body,
        out_shape=(jax.ShapeDtypeStruct((N_TOK, LANES), F32),
                   jax.ShapeDtypeStruct((N_TOK // tm, 8, LANES), F32)),
        grid=(N_TOK // tm,),
        in_specs=_mod_specs(tm, 0) + [pl.BlockSpec((D, LANES), lambda i: (0, 0))],
        out_specs=(pl.BlockSpec((tm, LANES), lambda i: (i, 0)), pl.BlockSpec((1, 8, LANES), lambda i: (i, 0, 0))),
        scratch_shapes=[pltpu.VMEM((1, LANES), F32)],
        compiler_params=_params(1),
        name="moe_router",
    )(x, sh, sc, jnp.pad(w_router, ((0, 0), (0, LANES - N_EXPERTS))))


def _row_tile(ref, row):
    return ref.at[pl.ds(pl.multiple_of(row * ROW_TILE, ROW_TILE), ROW_TILE)]


def _dispatch_body(pos1_ref, pos2_ref, x_ref, zeros_hbm, xs_hbm, rows_ref, sem, *, tm):
    del zeros_hbm
    base = pl.program_id(0) * tm
    _to_row_tiles(rows_ref, x_ref[...])

    def copies(r):
        src = _row_tile(rows_ref, r)
        return (pltpu.make_async_copy(src, _row_tile(xs_hbm, pos1_ref[base + r]), sem),
                pltpu.make_async_copy(src, _row_tile(xs_hbm, pos2_ref[base + r]), sem))

    def issue(r, carry):
        for queue, cp in enumerate(copies(r)):
            cp.start(priority=queue)
        return carry

    lax.fori_loop(0, tm, issue, 0, unroll=8)
    for _ in range(2):
        pltpu.make_async_copy(rows_ref, xs_hbm.at[pl.ds(0, tm * ROW_TILE)], sem).wait()


def _dispatch(pos1, pos2, x):
    tm = 512
    return pl.pallas_call(
        functools.partial(_dispatch_body, tm=tm),
        out_shape=jax.ShapeDtypeStruct((N_SLOTS * ROW_TILE, LANES), F32),
        grid_spec=pltpu.PrefetchScalarGridSpec(
            num_scalar_prefetch=2, grid=(N_TOK // tm,),
            in_specs=[pl.BlockSpec((tm, D), lambda i, *_: (i, 0)), pl.BlockSpec(memory_space=pl.ANY)],
            out_specs=pl.BlockSpec(memory_space=pl.ANY),
            scratch_shapes=[pltpu.VMEM((tm * ROW_TILE, LANES), F32), pltpu.SemaphoreType.DMA(())]),
        input_output_aliases={3: 0},
        compiler_params=_params(1),
        name="moe_dispatch",
    )(pos1, pos2, x, jnp.zeros((N_SLOTS * ROW_TILE, LANES), F32))


def _new_expert(eid_ref, nv_ref, t):
    tt = jnp.minimum(t, nv_ref[0] - 1)
    return (t == 0) | (eid_ref[tt] != eid_ref[jnp.maximum(tt - 1, 0)])


def _expert_swiglu_body(eid_ref, b1_ref, b2_ref, nv_ref, xs_ref, sh_ref, sc_ref, wg_ref, wu_ref, o_ref,
                        wgb_ref, wub_ref, *, fc):
    t = pl.program_id(1)

    @pl.when(_new_expert(eid_ref, nv_ref, t))
    def _():
        wgb_ref[...] = wg_ref[0].astype(BF16)
        wub_ref[...] = wu_ref[0].astype(BF16)

    @pl.when(t < nv_ref[0])
    def _():
        tm = o_ref.shape[0]
        x = _from_row_tiles(xs_ref, tm)
        slot = t * tm + lax.broadcasted_iota(jnp.int32, (tm, 1), 0)
        in1, in2 = slot >= b1_ref[t], slot >= b2_ref[t]
        sc = jnp.where(in2, sc_ref[2], jnp.where(in1, sc_ref[1], sc_ref[0]))
        sh = jnp.where(in2, sh_ref[2], jnp.where(in1, sh_ref[1], sh_ref[0]))
        h = (x * (1.0 + sc) + sh).astype(BF16)
        for c in range(o_ref.shape[1] // fc):
            cs = slice(c * fc, (c + 1) * fc)
            g = jnp.dot(h, wgb_ref[:, cs], preferred_element_type=F32)
            u = jnp.dot(h, wub_ref[:, cs], preferred_element_type=F32)
            o_ref[:, cs] = (g * jax.nn.sigmoid(g) * u).astype(BF16)

    @pl.when(t >= nv_ref[0])
    def _():
        o_ref[...] = jnp.zeros_like(o_ref)


def _expert_swiglu(meta, xs_rt, sh, sc, w_gate, w_up, e0):
    eid, b1, b2, nv = meta
    tm, f = EXPERT_TM, w_gate.shape[2]
    fh = f // 2

    def tile(t, eid, b1, b2, nv):
        return jnp.minimum(t, nv[0] - 1)

    w_spec = pl.BlockSpec((1, D, fh), lambda p, t, eid, b1, b2, nv: (e0 + eid[tile(t, eid, b1, b2, nv)], 0, p))
    mod = pl.BlockSpec((8, 1, D), lambda p, t, *_: (0, 0, 0))
    return pl.pallas_call(
        functools.partial(_expert_swiglu_body, fc=256),
        out_shape=jax.ShapeDtypeStruct((N_SLOTS, f), BF16),
        grid_spec=pltpu.PrefetchScalarGridSpec(
            num_scalar_prefetch=4, grid=(2, N_SLOT_TILES),
            in_specs=[pl.BlockSpec((tm * ROW_TILE, LANES), lambda p, t, *m: (tile(t, *m), 0)), mod, mod,
                      w_spec, w_spec],
            out_specs=pl.BlockSpec((tm, fh), lambda p, t, *m: (t, p)),
            scratch_shapes=[pltpu.VMEM((D, fh), BF16), pltpu.VMEM((D, fh), BF16)]),
        compiler_params=_params(2, 56 * 1024 * 1024),
        name="moe_swiglu",
    )(eid, b1, b2, nv, xs_rt, sh, sc, w_gate, w_up)


def _expert_down_body(eid_ref, nv_ref, a_ref, w_ref, y_ref, wb_ref):
    t = pl.program_id(0)

    @pl.when(_new_expert(eid_ref, nv_ref, t))
    def _():
        wb_ref[...] = w_ref[0].astype(BF16)

    @pl.when(t < nv_ref[0])
    def _():
        _to_row_tiles(y_ref, jnp.dot(a_ref[...], wb_ref[...], preferred_element_type=F32))

    @pl.when(t >= nv_ref[0])
    def _():
        y_ref[...] = jnp.zeros_like(y_ref)


def _expert_down(meta, act, w_down, e0):
    eid, _, _, nv = meta
    tm, f = EXPERT_TM, act.shape[1]

    def tile(t, eid, nv):
        return jnp.minimum(t, nv[0] - 1)

    return pl.pallas_call(
        _expert_down_body,
        out_shape=jax.ShapeDtypeStruct((N_SLOTS * ROW_TILE, LANES), F32),
        grid_spec=pltpu.PrefetchScalarGridSpec(
            num_scalar_prefetch=2, grid=(N_SLOT_TILES,),
            in_specs=[pl.BlockSpec((tm, f), lambda t, *m: (tile(t, *m), 0)),
                      pl.BlockSpec((1, f, D), lambda t, eid, nv: (e0 + eid[tile(t, eid, nv)], 0, 0))],
            out_specs=pl.BlockSpec((tm * ROW_TILE, LANES), lambda t, *m: (t, 0)),
            scratch_shapes=[pltpu.VMEM((f, D), BF16)]),
        compiler_params=_params(1, 58 * 1024 * 1024),
        name="moe_down",
    )(eid, nv, act, w_down)


def _combine_body(pos1_ref, pos2_ref, y_hbm, info_ref, x_ref, g_ref, lng_ref, lnb_ref, o_ref, y1_ref, y2_ref, sem,
                  *, tm):
    base = pl.program_id(0) * tm

    def copies(r):
        return (pltpu.make_async_copy(_row_tile(y_hbm, pos1_ref[base + r]), _row_tile(y1_ref, r), sem),
                pltpu.make_async_copy(_row_tile(y_hbm, pos2_ref[base + r]), _row_tile(y2_ref, r), sem))

    def issue(r, carry):
        for queue, cp in enumerate(copies(r)):
            cp.start(priority=queue)
        return carry

    lax.fori_loop(0, tm, issue, 0, unroll=8)
    for y_ref in (y1_ref, y2_ref):
        pltpu.make_async_copy(y_hbm.at[pl.ds(0, tm * ROW_TILE)], y_ref, sem).wait()
    info = info_ref[...]
    ffn = (info[:, INFO_W1:INFO_W1 + 1] * _from_row_tiles(y1_ref, tm)
           + info[:, INFO_W2:INFO_W2 + 1] * _from_row_tiles(y2_ref, tm))
    y = ALPHA * x_ref[...] + g_ref[0] * ffn
    mu = jnp.mean(y, axis=-1, keepdims=True)
    yc = y - mu
    var = jnp.mean(yc * yc, axis=-1, keepdims=True)
    o_ref[...] = yc * lax.rsqrt(var + LN_EPS) * lng_ref[...] + lnb_ref[...]


def _combine(pos1, pos2, y_rt, info, x, gate, ln_g, ln_b):
    tm = 512
    return pl.pallas_call(
        functools.partial(_combine_body, tm=tm),
        out_shape=jax.ShapeDtypeStruct((N_TOK, D), F32),
        grid_spec=pltpu.PrefetchScalarGridSpec(
            num_scalar_prefetch=2, grid=(N_TOK // tm,),
            in_specs=[pl.BlockSpec(memory_space=pl.ANY),
                      pl.BlockSpec((tm, LANES), lambda i, *_: (i, 0)),
                      pl.BlockSpec((tm, D), lambda i, *_: (i, 0)),
                      pl.BlockSpec((1, 1, D), lambda i, *_: (_group_of_row(i * tm), 0, 0)),
                      pl.BlockSpec((1, D), lambda i, *_: (0, 0)),
                      pl.BlockSpec((1, D), lambda i, *_: (0, 0))],
            out_specs=pl.BlockSpec((tm, D), lambda i, *_: (i, 0)),
            scratch_shapes=[pltpu.VMEM((tm * ROW_TILE, LANES), F32), pltpu.VMEM((tm * ROW_TILE, LANES), F32),
                            pltpu.SemaphoreType.DMA(())]),
        compiler_params=_params(1),
        name="moe_combine",
    )(pos1, pos2, y_rt, info, x, gate, ln_g.reshape(1, D), ln_b.reshape(1, D))


def _slot_plan(info, incl):
    row = lambda n_rows: incl[n_rows // ROUTER_TM - 1, 0, :N_EXPERTS].astype(jnp.int32)
    count = row(N_TOK)
    padded = (count + EXPERT_TM - 1) // EXPERT_TM * EXPERT_TM
    end = jnp.cumsum(padded)
    start = end - padded
    tile_row = jnp.arange(N_SLOT_TILES, dtype=jnp.int32) * EXPERT_TM
    eid = jnp.minimum(jnp.sum(tile_row[:, None] >= end[None, :], axis=1), N_EXPERTS - 1).astype(jnp.int32)
    b1 = (start + row(N_CTX))[eid]
    b2 = (start + row(N_CTX + DEC_SEQ))[eid]
    nv = (end[-1:] // EXPERT_TM).astype(jnp.int32)
    experts = jnp.arange(N_EXPERTS, dtype=jnp.int32)
    start_of = lambda col: jnp.sum(jnp.where(info[:, col:col + 1].astype(jnp.int32) == experts, start, 0), axis=1)
    pos1 = start_of(INFO_E1) + info[:, INFO_R1].astype(jnp.int32)
    pos2 = start_of(INFO_E2) + info[:, INFO_R2].astype(jnp.int32)
    return pos1, pos2, (eid, b1, b2, nv)


def _even_mixer(x, sh, sc, gate, ln_g, ln_b, w_in, b_igate, b_fgate, ml_norm_g, q_norm_g, k_norm_g, w_out,
                st_c, st_n, st_m, cache_k, cache_v, rope_tabs):
    splits = (4 * ML_W, 4 * ML_W + N_GATES)
    w_main = jnp.concatenate([w_in[:, :splits[0]], w_in[:, splits[1]:]], axis=1).astype(BF16)
    proj = _mod_matmul(x, sh, sc, [w_main[None]], tm=1024, tn=MAIN_W // 2, out_dtype=F32, name="even_in_proj")[0]
    b_gate = jnp.stack([b_igate, b_fgate], axis=1).reshape(N_GATES)
    lic, bc, lir, br = _gates(x, sh, sc, w_in[:, splits[0]:splits[1]], b_gate)

    hf_c, hb_c, new_c, new_n, new_m = _mlstm(proj, lic, bc, lir, br, row0=0, n_seq=BATCH, seq_len=SEQ)
    init = (st_c, st_n, jnp.broadcast_to(st_m[..., None], st_n.shape))
    hf_s, hb_s, _, _, _ = _mlstm(proj, lic, bc, lir, br, row0=N_CTX, n_seq=DEC_BATCH, seq_len=DEC_SEQ, init=init)
    ml = _ml_post((hf_c, hf_s), (hb_c, hb_s), proj, ml_norm_g)

    q_c, kn_c, kb_c = _qk_prep(proj, q_norm_g, k_norm_g, row0=0, rows=N_CTX)
    q_s, _, kb_s = _qk_prep(proj, q_norm_g, k_norm_g, row0=N_CTX, rows=N_LAT, rope_tabs=rope_tabs)
    v_all = proj[:, MAIN_W - KV_W:]
    v_c, v_s = v_all[:N_CTX], v_all[N_CTX:]
    att_c = _attention(q_c, _head_major(kb_c, BATCH).swapaxes(2, 3),
                       _with_ones(_head_major(v_c.astype(BF16), BATCH)), tq=SEQ)
    k_lat = jnp.concatenate([kb_s.reshape(DEC_BATCH, DEC_SEQ, KV_W),
                             cache_k.reshape(DEC_BATCH, PAST_LEN, KV_W).astype(BF16)], axis=1)
    v_lat = jnp.concatenate([v_s.reshape(DEC_BATCH, DEC_SEQ, KV_W).astype(BF16),
                             cache_v.reshape(DEC_BATCH, PAST_LEN, KV_W).astype(BF16)], axis=1)
    att_s = _attention(q_s, _head_major(k_lat.reshape(-1, KV_W), DEC_BATCH).swapaxes(2, 3),
                       _with_ones(_head_major(v_lat.reshape(-1, KV_W), DEC_BATCH)), tq=256)

    x = _proj_res_ln([ml, (att_c, att_s)], w_out.astype(BF16), x, gate, ln_g, ln_b, tm=512, name="even_out_proj")
    new_k = kn_c.reshape(BATCH, SEQ, ATT_KV_HEADS, ATT_HEAD_DIM)
    new_v = v_c.reshape(BATCH, SEQ, ATT_KV_HEADS, ATT_HEAD_DIM)
    return x, new_k, new_v, new_c, new_n, new_m[..., 0]


def _hyena_mixer(x, sh, sc, gate, ln_g, ln_b, w_in, conv_w, conv_b, w1, b1, w2, b2, w3, sin_freq, skip, w_out, dft):
    u = _mod_matmul(x, sh, sc, [w_in.astype(BF16)[None]], tm=1024, tn=1536, out_dtype=F32, name="hyena_in_proj")[0]
    mats, consts = dft
    tiles = dict(tf=SEQ, tk=SEQ, tn=D)
    h_sum, h_diff, nyq = _hy_filter(SEQ, w1, b1, w2, b2, w3, sin_freq)
    kr, ki = _dft_fwd(mats, h_sum, h_diff, row0=0, n_seq=1, seq_len=SEQ, **tiles)
    kr, ki = kr[0], ki[0].at[0].set(nyq[0])
    z, x0 = _hy_gate(u, conv_w, conv_b, row0=0, n_seq=BATCH, seq_len=SEQ, seqs_per_step=8)
    yr, yi = _dft_fwd(mats, z, z, row0=0, n_seq=BATCH, seq_len=SEQ, filt=(kr, ki), **tiles)
    y_c = _dft_inv(mats, yr, yi, z, x0, skip, row0=0, n_seq=BATCH, seq_len=SEQ, tm=SEQ, tk=SEQ, tn=D)
    h_sum, h_diff, _ = _hy_filter(DEC_SEQ, w1, b1, w2, b2, w3, sin_freq)
    kr, ki = _hy_spectrum(h_sum, h_diff, consts)
    y_s = _hy_conv_fft(u, conv_w, conv_b, kr, ki, skip, consts, row0=N_CTX, n_seq=DEC_BATCH)
    return _proj_res_ln([(y_c, y_s)], w_out.astype(BF16), x, gate, ln_g, ln_b, tm=512, name="hyena_out_proj")


def _dense_ffn(x, sh, sc, gate, ln_g, ln_b, w_gate, w_up, w_down):
    act = _mod_matmul(x, sh, sc, [w_gate.astype(BF16)[None], w_up.astype(BF16)[None]],
                      tm=1024, tn=D_FF // 2, out_dtype=BF16, name="ffn_swiglu")
    return _proj_res_ln([act[0]], w_down.astype(BF16), x, gate, ln_g, ln_b, tm=512, name="ffn_down")


def _moe_ffn(x, sh, sc, gate, ln_g, ln_b, w_router, w_gate, w_up, w_down, e0):
    info, incl = _router(x, sh, sc, w_router)
    pos1, pos2, meta = _slot_plan(info, incl)
    xs_rt = _dispatch(pos1, pos2, x)
    act = _expert_swiglu(meta, xs_rt, sh, sc, w_gate, w_up, e0)
    y_rt = _expert_down(meta, act, w_down, e0)
    return _combine(pos1, pos2, y_rt, info, x, gate, ln_g, ln_b)


def kernel(x_prompt, x_sample, cache_attn_k, cache_attn_v, state_mlstm_C, state_mlstm_n, state_mlstm_m, c, c_ctx, w_ada, b_ada, ln_g, ln_b, w_in_even, b_igate, b_fgate, ml_norm_g, q_norm_g, k_norm_g, w_out_even, w_ffn_gate, w_ffn_up, w_ffn_down, w_in_hy, hy_conv_w, hy_conv_b, hy_filt_w1, hy_filt_b1, hy_filt_w2, hy_filt_b2, hy_filt_w3, hy_sin_freq, hy_skip, w_out_hy, w_router, w_moe_gate, w_moe_up, w_moe_down):
    x = jnp.concatenate([x_prompt.reshape(N_CTX, D), x_sample.reshape(N_LAT, D)])
    cvec = jnp.concatenate([c_ctx[None], c, jnp.zeros((8 - 1 - DEC_BATCH, D), F32)])
    mods = _ada(cvec, w_ada, b_ada)
    rope_tabs = _rope_tables()
    dft = (_dft_matrices(SEQ), _fft_consts())
    moe_w = [w.reshape((-1,) + w.shape[2:]) for w in (w_moe_gate, w_moe_up, w_moe_down)]
    new_k, new_v, new_c, new_n, new_m = [], [], [], [], []
    for layer in range(DEPTH):
        sh1, sc1, g1, sh2, sc2, g2 = (mods[layer, :, i * D:(i + 1) * D].reshape(8, 1, D) for i in range(6))
        i = layer // 2
        if layer % 2 == 0:
            x, k_c, v_c, st_c, st_n, st_m = _even_mixer(
                x, sh1, sc1, g1, ln_g[layer, 0], ln_b[layer, 0], w_in_even[i], b_igate[i], b_fgate[i], ml_norm_g[i],
                q_norm_g[i], k_norm_g[i], w_out_even[i], state_mlstm_C[:, i], state_mlstm_n[:, i], state_mlstm_m[:, i],
                cache_attn_k[:, i], cache_attn_v[:, i], rope_tabs)
            new_k.append(k_c)
            new_v.append(v_c)
            new_c.append(st_c)
            new_n.append(st_n)
            new_m.append(st_m)
            x = _dense_ffn(x, sh2, sc2, g2, ln_g[layer, 1], ln_b[layer, 1], w_ffn_gate[i], w_ffn_up[i], w_ffn_down[i])
        else:
            x = _hyena_mixer(x, sh1, sc1, g1, ln_g[layer, 0], ln_b[layer, 0], w_in_hy[i], hy_conv_w[i], hy_conv_b[i],
                             hy_filt_w1[i], hy_filt_b1[i], hy_filt_w2[i], hy_filt_b2[i], hy_filt_w3[i], hy_sin_freq[i],
                             hy_skip[i], w_out_hy[i], dft)
            x = _moe_ffn(x, sh2, sc2, g2, ln_g[layer, 1], ln_b[layer, 1], w_router[i], *moe_w, i * N_EXPERTS)
    return (x[:N_CTX].reshape(BATCH, SEQ, D), x[N_CTX:].reshape(DEC_BATCH, DEC_SEQ, D),
            jnp.stack(new_k, axis=1), jnp.stack(new_v, axis=1), jnp.stack(new_c, axis=1),
            jnp.stack(new_n, axis=1), jnp.stack(new_m, axis=1))
```

```python
import functools
import math

import jax
import jax.numpy as jnp
import numpy as np
from jax import lax
from jax.experimental import pallas as pl
from jax.experimental.pallas import tpu as pltpu

F32 = jnp.float32
BF16 = jnp.bfloat16
HIGHEST = lax.Precision.HIGHEST

D = 1024
BATCH, SEQ = 32, 256
DEC_BATCH, DEC_SEQ = 2, 4096
DEPTH = 4
PAST_LEN = 256
GRID_W = 64
N_CTX = BATCH * SEQ
N_LAT = DEC_BATCH * DEC_SEQ
N_TOK = N_CTX + N_LAT

ML_HEADS, ML_HEAD_DIM = 4, 128
ML_W = ML_HEADS * ML_HEAD_DIM
CHUNK = 128
MLSTM_PAR = 2
ATT_HEADS, ATT_KV_HEADS, ATT_HEAD_DIM = 8, 2, 64
ATT_GROUP = ATT_HEADS // ATT_KV_HEADS
ATT_W = ATT_HEADS * ATT_HEAD_DIM
KV_W = ATT_KV_HEADS * ATT_HEAD_DIM
GROUP_W = ATT_GROUP * ATT_HEAD_DIM
ROPE_BASE = 10000.0
N_GATES = 4 * ML_HEADS
MAIN_W = 4 * ML_W + ATT_W + 2 * KV_W

HY_EMB = 33
HY_BANDS = (HY_EMB - 1) // 2
HY_TARGET, HY_SHORT_PCT, HY_LONG_PCT = 1e-2, 0.3, 1.5
D_FF = 2816
N_EXPERTS = 8
MOE_D_FF = 3584
ALPHA = (2 * DEPTH) ** 0.25
LN_EPS = 1e-5
RMS_EPS = 1e-6

LANES = 128
VMEM_LIMIT = 48 * 1024 * 1024


def _params(n_axes, vmem=VMEM_LIMIT):
    return pltpu.CompilerParams(dimension_semantics=("arbitrary",) * n_axes, vmem_limit_bytes=vmem)


def _group_of_row(r):
    return jnp.where(r < N_CTX, 0, 1 + (r - N_CTX) // DEC_SEQ)


def _modulate(x_ref, sh_ref, sc_ref):
    return x_ref[...] * (1.0 + sc_ref[0]) + sh_ref[0]


def _mod_specs(tm, row_axis):
    def rows(*ids):
        return (ids[row_axis], 0)

    def grp(*ids):
        return (_group_of_row(ids[row_axis] * tm), 0, 0)

    return [pl.BlockSpec((tm, D), rows), pl.BlockSpec((1, 1, D), grp), pl.BlockSpec((1, 1, D), grp)]


def _ada_body(c_ref, w_ref, b_ref, o_ref):
    c = c_ref[...]
    s = c * jax.nn.sigmoid(c)
    o_ref[0] = jnp.dot(s, w_ref[0], preferred_element_type=F32, precision=HIGHEST) + b_ref[0]


def _ada(cvec, w_ada, b_ada):
    tn = 1536
    return pl.pallas_call(
        _ada_body,
        out_shape=jax.ShapeDtypeStruct((DEPTH, 8, 6 * D), F32),
        grid=(DEPTH, 6 * D // tn),
        in_specs=[pl.BlockSpec((8, D), lambda l, j: (0, 0)),
                  pl.BlockSpec((1, D, tn), lambda l, j: (l, 0, j)),
                  pl.BlockSpec((1, 1, tn), lambda l, j: (l, 0, j))],
        out_specs=pl.BlockSpec((1, 8, tn), lambda l, j: (l, 0, j)),
        compiler_params=_params(2),
        name="ada_modulation",
    )(cvec, w_ada, b_ada.reshape(DEPTH, 1, 6 * D))


def _mod_mm_body(x_ref, sh_ref, sc_ref, *refs, n_w):
    w_refs, o_ref, h_ref = refs[:n_w], refs[n_w], refs[n_w + 1]

    @pl.when(pl.program_id(2) == 0)
    def _():
        h_ref[...] = _modulate(x_ref, sh_ref, sc_ref).astype(BF16)

    h = h_ref[...]
    if n_w == 1:
        o = jnp.dot(h, w_refs[0][0], preferred_element_type=F32)
    else:
        g = jnp.dot(h, w_refs[0][0], preferred_element_type=F32)
        u = jnp.dot(h, w_refs[1][0], preferred_element_type=F32)
        o = g * jax.nn.sigmoid(g) * u
    o_ref[0] = o.astype(o_ref.dtype)


def _mod_matmul(x, sh, sc, ws, *, tm, tn, out_dtype, name):
    n_e, _, f = ws[0].shape
    return pl.pallas_call(
        functools.partial(_mod_mm_body, n_w=len(ws)),
        out_shape=jax.ShapeDtypeStruct((n_e, N_TOK, f), out_dtype),
        grid=(n_e, N_TOK // tm, f // tn),
        in_specs=_mod_specs(tm, 1) + [pl.BlockSpec((1, D, tn), lambda e, i, j: (e, 0, j)) for _ in ws],
        out_specs=pl.BlockSpec((1, tm, tn), lambda e, i, j: (e, i, j)),
        scratch_shapes=[pltpu.VMEM((tm, D), BF16)],
        compiler_params=_params(3),
        name=name,
    )(x, sh, sc, *ws)


def _split_specs(tm, width):
    nc = N_CTX // tm
    return [pl.BlockSpec((tm, width), lambda i: (jnp.minimum(i, nc - 1), 0)),
            pl.BlockSpec((tm, width), lambda i: (jnp.maximum(i - nc, 0), 0))]


def _pick_split(ctx_ref, lat_ref):
    tm = ctx_ref.shape[0]
    return jnp.where(pl.program_id(0) < N_CTX // tm, ctx_ref[...], lat_ref[...])


def _proj_res_ln_body(*refs, split):
    n_in = sum(2 if s else 1 for s in split)
    part_refs = list(refs[:n_in])
    w_ref, x_ref, g_ref, lng_ref, lnb_ref, o_ref = refs[n_in:]
    cols = [_pick_split(part_refs.pop(0), part_refs.pop(0)) if s else part_refs.pop(0)[...] for s in split]
    a = cols[0] if len(cols) == 1 else jnp.concatenate(cols, axis=-1)
    y = ALPHA * x_ref[...] + g_ref[0] * jnp.dot(a, w_ref[...], preferred_element_type=F32)
    mu = jnp.mean(y, axis=-1, keepdims=True)
    yc = y - mu
    var = jnp.mean(yc * yc, axis=-1, keepdims=True)
    o_ref[...] = yc * lax.rsqrt(var + LN_EPS) * lng_ref[...] + lnb_ref[...]


def _proj_res_ln(parts, w, x, gate, ln_g, ln_b, *, tm, name):
    split = tuple(isinstance(p, tuple) for p in parts)
    in_specs, args = [], []
    for p, s in zip(parts, split):
        if s:
            in_specs += _split_specs(tm, p[0].shape[1])
            args += list(p)
        else:
            in_specs.append(pl.BlockSpec((tm, p.shape[1]), lambda i: (i, 0)))
            args.append(p)
    in_specs += [pl.BlockSpec(w.shape, lambda i: (0, 0)),
                 pl.BlockSpec((tm, D), lambda i: (i, 0)),
                 pl.BlockSpec((1, 1, D), lambda i: (_group_of_row(i * tm), 0, 0)),
                 pl.BlockSpec((1, D), lambda i: (0, 0)),
                 pl.BlockSpec((1, D), lambda i: (0, 0))]
    return pl.pallas_call(
        functools.partial(_proj_res_ln_body, split=split),
        out_shape=jax.ShapeDtypeStruct((N_TOK, D), F32),
        grid=(N_TOK // tm,),
        in_specs=in_specs,
        out_specs=pl.BlockSpec((tm, D), lambda i: (i, 0)),
        compiler_params=_params(1),
        name=name,
    )(*args, w, x, gate, ln_g.reshape(1, D), ln_b.reshape(1, D))


def _log_sigmoid(x):
    return jnp.minimum(x, 0.0) - jnp.log(1.0 + jnp.exp(-jnp.abs(x)))


def _gates_body(x_ref, sh_ref, sc_ref, wg_ref, wgt_ref, b_ref, bt_ref,
                lic_ref, bc_ref, lir_ref, br_ref, *, tm):
    h = _modulate(x_ref, sh_ref, sc_ref).astype(BF16)
    g = jnp.dot(h, wg_ref[...], preferred_element_type=F32) + b_ref[...]
    gt = lax.dot_general(wgt_ref[...], h, (((1,), (1,)), ((), ())), preferred_element_type=F32) + bt_ref[...]
    lic_ref[...] = g
    lir_ref[...] = gt
    lf, lft = _log_sigmoid(g), _log_sigmoid(gt)
    r = lax.broadcasted_iota(jnp.int32, (CHUNK, CHUNK), 0)
    c = lax.broadcasted_iota(jnp.int32, (CHUNK, CHUNK), 1)
    tri_l = (c <= r).astype(F32)
    tri_u = (c >= r).astype(F32)
    fwd_col = lax.broadcasted_iota(jnp.int32, (CHUNK, LANES), 1) < 2 * ML_HEADS
    fwd_row = lax.broadcasted_iota(jnp.int32, (N_GATES, CHUNK), 0) < 2 * ML_HEADS
    for ch in range(tm // CHUNK):
        sl = slice(ch * CHUNK, (ch + 1) * CHUNK)
        lfc, lftc = lf[sl, :], lft[:, sl]
        cum_f = jnp.dot(tri_l, lfc, preferred_element_type=F32, precision=HIGHEST)
        cum_b = jnp.dot(tri_u, lfc, preferred_element_type=F32, precision=HIGHEST)
        bc_ref[sl, :] = jnp.where(fwd_col, cum_f, cum_b)
        cum_f = jnp.dot(lftc, tri_u, preferred_element_type=F32, precision=HIGHEST)
        cum_b = jnp.dot(lftc, tri_l, preferred_element_type=F32, precision=HIGHEST)
        br_ref[:, sl] = jnp.where(fwd_row, cum_f, cum_b)


def _gates(x, sh, sc, wg, b_gate):
    tm = 256
    wg_pad = jnp.pad(wg, ((0, 0), (0, LANES - N_GATES)))
    b_pad = jnp.pad(b_gate, (0, LANES - N_GATES)).reshape(1, LANES)
    col = pl.BlockSpec((tm, LANES), lambda i: (i, 0))
    row = pl.BlockSpec((N_GATES, tm), lambda i: (0, i))
    return pl.pallas_call(
        functools.partial(_gates_body, tm=tm),
        out_shape=(jax.ShapeDtypeStruct((N_TOK, LANES), F32), jax.ShapeDtypeStruct((N_TOK, LANES), F32),
                   jax.ShapeDtypeStruct((N_GATES, N_TOK), F32), jax.ShapeDtypeStruct((N_GATES, N_TOK), F32)),
        grid=(N_TOK // tm,),
        in_specs=_mod_specs(tm, 0) + [pl.BlockSpec((D, LANES), lambda i: (0, 0)),
                                         pl.BlockSpec((N_GATES, D), lambda i: (0, 0)),
                                         pl.BlockSpec((1, LANES), lambda i: (0, 0)),
                                         pl.BlockSpec((N_GATES, 1), lambda i: (0, 0))],
        out_specs=(col, col, row, row),
        compiler_params=_params(1),
        name="mlstm_gates",
    )(x, sh, sc, wg_pad.astype(BF16), wg.T.astype(BF16), b_pad, b_gate.reshape(N_GATES, 1))


def _mlstm_body(*refs, has_init):
    (qf, kf, vf, licf, bcf, lirf, brf, qb, kb, vb, licb, bcb, lirb, brb) = refs[:14]
    refs = refs[14:]
    if has_init:
        c0_ref, n0_ref, m0_ref = refs[:3]
        refs = refs[3:]
    hf_ref, hb_ref, c_ref, n_ref, m_ref = refs

    @pl.when(pl.program_id(1) == 0)
    def _():
        if has_init:
            c_ref[...] = c0_ref[...]
            n_ref[...] = n0_ref[...]
            m_ref[...] = m0_ref[...]
        else:
            c_ref[...] = jnp.zeros_like(c_ref)
            n_ref[...] = jnp.zeros_like(n_ref)
            m_ref[...] = jnp.zeros_like(m_ref)

    t_idx = lax.broadcasted_iota(jnp.int32, (CHUNK, CHUNK), 0)
    s_idx = lax.broadcasted_iota(jnp.int32, (CHUNK, CHUNK), 1)
    nt = (((1,), (1,)), ((), ()))
    stores = []

    def chain(u, d, h, q_ref, k_ref, v_ref, lic_ref, bc_ref, lir_ref, br_ref, h_ref):
        mask = (s_idx <= t_idx) if d == 0 else (s_idx >= t_idx)
        hs = slice(h * ML_HEAD_DIM, (h + 1) * ML_HEAD_DIM)
        gi, gf = d * 2 * ML_HEADS + h, d * 2 * ML_HEADS + ML_HEADS + h
        q = q_ref[u, :, hs]
        k = k_ref[u, :, hs] * (ML_HEAD_DIM ** -0.5)
        v = v_ref[u, :, hs]
        qh, kh, vh = q.astype(BF16), k.astype(BF16), v.astype(BF16)
        li_c, b_c = lic_ref[u, :, gi:gi + 1], bc_ref[u, :, gf:gf + 1]
        li_r, b_r = lir_ref[u, gi:gi + 1, :], br_ref[u, gf:gf + 1, :]
        c_st = c_ref[u, d, h]
        n_st = n_ref[u, d, h:h + 1, :]
        m_st = m_ref[u, d, h:h + 1, :][:, 0:1]
        dmat = jnp.where(mask, b_c - b_r + li_r, -jnp.inf)
        inter = b_c + m_st
        m_out = jnp.maximum(inter, jnp.max(dmat, axis=-1, keepdims=True))
        p = jnp.exp(dmat - m_out)
        w_inter = jnp.exp(inter - m_out)
        yield
        qk = lax.dot_general(qh, kh, nt, preferred_element_type=F32)
        qc = jnp.dot(qh, c_st.astype(BF16), preferred_element_type=F32)
        yield
        s = qk * p
        den = (jnp.sum(s, axis=-1, keepdims=True)
               + w_inter * jnp.sum(q * n_st, axis=-1, keepdims=True))
        sh = s.astype(BF16)
        b_last = b_r[:, CHUNK - 1:CHUNK] if d == 0 else b_r[:, 0:1]
        g_r = b_last - b_r + li_r
        g_c = b_last - b_c + li_c
        m_new = jnp.maximum(b_last + m_st, jnp.max(g_r, axis=-1, keepdims=True))
        decay = jnp.exp(b_last + m_st - m_new)
        kw = k * jnp.exp(g_c - m_new)
        kwh = kw.astype(BF16)
        yield
        sv = jnp.dot(sh, vh, preferred_element_type=F32)
        kv = lax.dot_general(kwh, vh, (((0,), (0,)), ((), ())), preferred_element_type=F32)
        yield
        h_out = (sv + w_inter * qc) / jnp.maximum(jnp.abs(den), jnp.exp(-m_out))
        c_new = decay * c_st + kv
        n_new = decay * n_st + jnp.sum(kw, axis=0, keepdims=True)
        stores.append((h_ref, u, d, h, hs, h_out, c_new, n_new, jnp.broadcast_to(m_new, (1, ML_HEAD_DIM))))
        yield

    chains = [chain(u, d, h, *group)
              for u in range(c_ref.shape[0])
              for d, group in enumerate(((qf, kf, vf, licf, bcf, lirf, brf, hf_ref),
                                         (qb, kb, vb, licb, bcb, lirb, brb, hb_ref)))
              for h in range(ML_HEADS)]
    for _ in range(5):
        for ch in chains:
            next(ch)
    for h_ref, u, d, h, hs, h_out, c_new, n_new, m_new in stores:
        h_ref[u, :, hs] = h_out
        c_ref[u, d, h] = c_new
        n_ref[u, d, h:h + 1, :] = n_new
        m_ref[u, d, h:h + 1, :] = m_new


def _mlstm(proj, lic, bc, lir, br, *, row0, n_seq, seq_len, init=None):
    nc = seq_len // CHUNK
    par = MLSTM_PAR
    g0 = row0 // seq_len // par
    seqs = lambda a: a.reshape(N_TOK // seq_len, seq_len, a.shape[-1])
    rows_of = lambda a: seqs(a.T).transpose(0, 2, 1)

    def chunk_specs(chunk):
        return ([pl.BlockSpec((par, CHUNK, ML_W), lambda b, j, c=c: (g0 + b, chunk(j), c)) for c in range(3)]
                + [pl.BlockSpec((par, CHUNK, LANES), lambda b, j: (g0 + b, chunk(j), 0))] * 2
                + [pl.BlockSpec((par, N_GATES, CHUNK), lambda b, j: (g0 + b, 0, chunk(j)))] * 2)

    fwd, bwd = (lambda j: j), (lambda j: nc - 1 - j)
    st_c = pl.BlockSpec((par, 2, ML_HEADS, ML_HEAD_DIM, ML_HEAD_DIM), lambda b, j: (b, 0, 0, 0, 0))
    st_n = pl.BlockSpec((par, 2, ML_HEADS, ML_HEAD_DIM), lambda b, j: (b, 0, 0, 0))
    in_specs = chunk_specs(fwd) + chunk_specs(bwd)
    args = [seqs(proj)] * 3 + [seqs(lic), seqs(bc), rows_of(lir), rows_of(br)]
    args = args * 2
    if init is not None:
        in_specs += [st_c, st_n, st_n]
        args += list(init)
    h_shape = jax.ShapeDtypeStruct((n_seq, seq_len, ML_W), F32)
    hf, hb, c_st, n_st, m_st = pl.pallas_call(
        functools.partial(_mlstm_body, has_init=init is not None),
        out_shape=(h_shape, h_shape,
                   jax.ShapeDtypeStruct((n_seq, 2, ML_HEADS, ML_HEAD_DIM, ML_HEAD_DIM), F32),
                   jax.ShapeDtypeStruct((n_seq, 2, ML_HEADS, ML_HEAD_DIM), F32),
                   jax.ShapeDtypeStruct((n_seq, 2, ML_HEADS, ML_HEAD_DIM), F32)),
        grid=(n_seq // par, nc),
        in_specs=in_specs,
        out_specs=(pl.BlockSpec((par, CHUNK, ML_W), lambda b, j: (b, fwd(j), 0)),
                   pl.BlockSpec((par, CHUNK, ML_W), lambda b, j: (b, bwd(j), 0)),
                   st_c, st_n, st_n),
        compiler_params=_params(2),
        name="mlstm_scan",
    )(*args)
    return hf.reshape(-1, ML_W), hb.reshape(-1, ML_W), c_st, n_st, m_st


def _ml_post_body(hfc_ref, hfs_ref, hbc_ref, hbs_ref, o_ref, g_ref, out_ref):
    h = _pick_split(hfc_ref, hfs_ref) + _pick_split(hbc_ref, hbs_ref)
    gate = jax.nn.sigmoid(o_ref[...]) * g_ref[...]
    for hd in range(ML_HEADS):
        hs = slice(hd * ML_HEAD_DIM, (hd + 1) * ML_HEAD_DIM)
        x = h[:, hs]
        xc = x - jnp.mean(x, axis=-1, keepdims=True)
        var = jnp.mean(xc * xc, axis=-1, keepdims=True)
        out_ref[:, hs] = (gate[:, hs] * (xc * lax.rsqrt(var + RMS_EPS))).astype(BF16)


def _ml_post(hf, hb, proj, norm_g):
    tm = 512
    blk = pl.BlockSpec((tm, ML_W), lambda i: (i, 0))
    return pl.pallas_call(
        _ml_post_body,
        out_shape=jax.ShapeDtypeStruct((N_TOK, ML_W), BF16),
        grid=(N_TOK // tm,),
        in_specs=(_split_specs(tm, ML_W) * 2
                  + [pl.BlockSpec((tm, ML_W), lambda i: (i, 3)), pl.BlockSpec((1, ML_W), lambda i: (0, 0))]),
        out_specs=blk,
        compiler_params=_params(1),
        name="mlstm_out_norm",
    )(*hf, *hb, proj, norm_g.reshape(1, ML_W))


def _head_rms(x, gain):
    lane_head = lax.broadcasted_iota(jnp.int32, x.shape, 1) // ATT_HEAD_DIM
    sq = x * x
    ms = jnp.zeros_like(x)
    for hd in range(x.shape[1] // ATT_HEAD_DIM):
        sel = lane_head == hd
        ms = jnp.where(sel, jnp.sum(jnp.where(sel, sq, 0.0), axis=-1, keepdims=True), ms)
    return x * lax.rsqrt(ms * (1.0 / ATT_HEAD_DIM) + RMS_EPS) * gain


def _rope(x, cos, sin_signed):
    w = x.shape[1]
    even = lax.broadcasted_iota(jnp.int32, x.shape, 1) % 2 == 0
    partner = jnp.where(even, pltpu.roll(x, w - 1, 1), pltpu.roll(x, 1, 1))
    return x * cos + partner * sin_signed


def _qk_prep_body(q_ref, k_ref, qg_ref, kg_ref, *refs, rope):
    if rope:
        cq_ref, sq_ref, ck_ref, sk_ref, qo_ref, kn_ref, kr_ref = refs
    else:
        qo_ref, kn_ref, kr_ref = refs
    q = _head_rms(q_ref[...], qg_ref[...])
    k = _head_rms(k_ref[...], kg_ref[...])
    kn_ref[...] = k
    if rope:
        q = _rope(q, cq_ref[...], sq_ref[...])
        k = _rope(k, ck_ref[...], sk_ref[...])
    qo_ref[...] = (q * (ATT_HEAD_DIM ** -0.5)).astype(BF16)
    kr_ref[...] = k.astype(BF16)


def _qk_prep(proj, q_gain, k_gain, *, row0, rows, rope_tabs=None):
    tm = 512
    r0 = row0 // tm
    in_specs = [pl.BlockSpec((tm, ATT_W), lambda i: (r0 + i, 4 * ML_W // ATT_W)),
                pl.BlockSpec((tm, KV_W), lambda i: (r0 + i, (4 * ML_W + ATT_W) // KV_W)),
                pl.BlockSpec((1, ATT_W), lambda i: (0, 0)),
                pl.BlockSpec((1, KV_W), lambda i: (0, 0))]
    args = [proj, proj, jnp.tile(q_gain, ATT_HEADS).reshape(1, ATT_W), jnp.tile(k_gain, ATT_KV_HEADS).reshape(1, KV_W)]
    if rope_tabs is not None:
        per_seq = DEC_SEQ // tm
        in_specs += [pl.BlockSpec((tm, ATT_W), lambda i: (i % per_seq, 0))] * 2
        in_specs += [pl.BlockSpec((tm, KV_W), lambda i: (i % per_seq, 0))] * 2
        args += list(rope_tabs)
    return pl.pallas_call(
        functools.partial(_qk_prep_body, rope=rope_tabs is not None),
        out_shape=(jax.ShapeDtypeStruct((rows, ATT_W), BF16), jax.ShapeDtypeStruct((rows, KV_W), F32),
                   jax.ShapeDtypeStruct((rows, KV_W), BF16)),
        grid=(rows // tm,),
        in_specs=in_specs,
        out_specs=(pl.BlockSpec((tm, ATT_W), lambda i: (i, 0)), pl.BlockSpec((tm, KV_W), lambda i: (i, 0)),
                   pl.BlockSpec((tm, KV_W), lambda i: (i, 0))),
        compiler_params=_params(1),
        name="attn_qk_prep",
    )(*args)


def _rope_tables():
    rows = DEC_SEQ // GRID_W
    axis_dim = ATT_HEAD_DIM // 2
    row = jnp.repeat(jnp.arange(rows, dtype=F32), GRID_W)
    col = (jnp.arange(DEC_SEQ) % GRID_W).astype(F32)
    inv = ROPE_BASE ** (-jnp.arange(axis_dim // 2, dtype=F32) * 2.0 / axis_dim)
    ang = jnp.concatenate([row[:, None] * inv, col[:, None] * inv], axis=-1)
    cos = jnp.repeat(jnp.cos(ang), 2, axis=-1)
    sin = jnp.repeat(jnp.sin(ang), 2, axis=-1) * jnp.tile(jnp.array([-1.0, 1.0], F32), axis_dim)
    return (jnp.tile(cos, (1, ATT_HEADS)), jnp.tile(sin, (1, ATT_HEADS)),
            jnp.tile(cos, (1, ATT_KV_HEADS)), jnp.tile(sin, (1, ATT_KV_HEADS)))


def _attn_body(q_ref, k_ref, v_ref, o_ref):
    k, v = k_ref[0, 0], v_ref[0, 0]
    dh = k.shape[0]
    g = q_ref.shape[1] // dh
    pair = 2

    def heads(h0):
        q = jnp.concatenate([q_ref[:, h * dh:(h + 1) * dh] for h in range(h0, h0 + pair)], axis=0)
        s = jnp.dot(q, k, preferred_element_type=F32)
        yield
        e = jnp.exp((s - jnp.max(s, axis=-1, keepdims=True)).astype(BF16))
        yield
        o = jnp.dot(e, v, preferred_element_type=F32)
        o = (o[:, :dh] / o[:, dh:dh + 1]).astype(BF16)
        tq = q_ref.shape[0]
        for j in range(pair):
            o_ref[:, (h0 + j) * dh:(h0 + j + 1) * dh] = o[j * tq:(j + 1) * tq]
        yield

    chains = [heads(h0) for h0 in range(0, g, pair)]
    for step in range(len(chains) + 2):
        for i, chain in enumerate(chains):
            if 0 <= step - i < 3:
                next(chain)


def _attention(q, k_t, v_ones, *, tq):
    n_seq, _, dh, s_len = k_t.shape
    nq = q.shape[0] // n_seq // tq
    qo = pl.BlockSpec((tq, GROUP_W), lambda b, kh, i: (b * nq + i, kh))
    return pl.pallas_call(
        _attn_body,
        out_shape=jax.ShapeDtypeStruct(q.shape, BF16),
        grid=(n_seq, ATT_KV_HEADS, nq),
        in_specs=[qo, pl.BlockSpec((1, 1, dh, s_len), lambda b, kh, i: (b, kh, 0, 0)),
                  pl.BlockSpec((1, 1, s_len, LANES), lambda b, kh, i: (b, kh, 0, 0))],
        out_specs=qo,
        compiler_params=_params(3),
        name="attention",
    )(q, k_t, v_ones)


def _head_major(x, n_seq):
    return x.reshape(n_seq, -1, x.shape[1] // ATT_HEAD_DIM, ATT_HEAD_DIM).transpose(0, 2, 1, 3)


def _with_ones(v):
    pad = jnp.zeros(v.shape[:-1] + (LANES - ATT_HEAD_DIM - 1,), v.dtype)
    return jnp.concatenate([v, jnp.ones(v.shape[:-1] + (1,), v.dtype), pad], axis=-1)


def _hy_filter_body(feat_ref, t_ref, w1_ref, b1_ref, w2_ref, b2_ref, fr_ref, w3f_ref, w3b_ref, dl_ref,
                    hsum_ref, hdiff_ref, nyq_ref, z_ref):
    @pl.when(pl.program_id(0) == 0)
    def _():
        z = jnp.dot(feat_ref[...], w1_ref[...], preferred_element_type=F32, precision=HIGHEST) + b1_ref[...]
        z = jnp.sin(fr_ref[0:1, :] * z)
        z = jnp.dot(z, w2_ref[...], preferred_element_type=F32, precision=HIGHEST) + b2_ref[...]
        z_ref[...] = jnp.sin(fr_ref[1:2, :] * z)

    z = z_ref[...]
    window = jnp.exp(-t_ref[...] * dl_ref[...])
    h_f = jnp.dot(z, w3f_ref[...], preferred_element_type=F32, precision=HIGHEST) * window
    h_b = jnp.dot(z, w3b_ref[...], preferred_element_type=F32, precision=HIGHEST) * window
    row = lax.broadcasted_iota(jnp.int32, h_f.shape, 0)
    h_b = jnp.where(row == 0, 0.0, h_b)
    inv = 1.0 / (jnp.sum(jnp.abs(h_f), axis=0, keepdims=True) + jnp.sum(jnp.abs(h_b), axis=0, keepdims=True))
    h_sum = (h_f + h_b) * inv
    hsum_ref[...] = h_sum
    hdiff_ref[...] = (h_f - h_b) * inv
    nyq_ref[...] = jnp.sum(jnp.where(row % 2 == 0, h_sum, -h_sum), axis=0, keepdims=True)


def _hy_filter(seq_len, w1, b1, w2, b2, w3, sin_freq):
    tc = 256
    fw = w1.shape[1]
    t = jnp.arange(seq_len, dtype=F32)[:, None] / seq_len
    bands = jnp.arange(1, HY_BANDS + 1, dtype=F32)[None, :]
    feat = jnp.concatenate([t, jnp.sin(2.0 * math.pi * bands * t), jnp.cos(2.0 * math.pi * bands * t)], axis=-1)
    feat = jnp.pad(feat, ((0, 0), (0, LANES - HY_EMB)))
    deltas = jnp.abs(jnp.linspace(math.log(HY_TARGET) / HY_LONG_PCT, math.log(HY_TARGET) / HY_SHORT_PCT, D,
                                  dtype=F32)).reshape(1, D)
    pad_w = LANES - fw
    full = lambda shape: pl.BlockSpec(shape, lambda j: (0,) * len(shape))
    return pl.pallas_call(
        _hy_filter_body,
        out_shape=(jax.ShapeDtypeStruct((seq_len, D), F32), jax.ShapeDtypeStruct((seq_len, D), F32),
                   jax.ShapeDtypeStruct((1, D), F32)),
        grid=(D // tc,),
        in_specs=[full((seq_len, LANES)), full((seq_len, 1)), full((LANES, LANES)), full((1, LANES)),
                  full((LANES, LANES)), full((1, LANES)), full((2, LANES)),
                  pl.BlockSpec((LANES, tc), lambda j: (0, j)), pl.BlockSpec((LANES, tc), lambda j: (0, D // tc + j)),
                  pl.BlockSpec((1, tc), lambda j: (0, j))],
        out_specs=(pl.BlockSpec((seq_len, tc), lambda j: (0, j)), pl.BlockSpec((seq_len, tc), lambda j: (0, j)),
                   pl.BlockSpec((1, tc), lambda j: (0, j))),
        scratch_shapes=[pltpu.VMEM((seq_len, LANES), F32)],
        compiler_params=_params(1),
        name="hyena_filter",
    )(feat, t, jnp.pad(w1, ((0, LANES - HY_EMB), (0, pad_w))), jnp.pad(b1, (0, pad_w)).reshape(1, LANES),
      jnp.pad(w2, ((0, pad_w), (0, pad_w))), jnp.pad(b2, (0, pad_w)).reshape(1, LANES),
      jnp.pad(sin_freq, ((0, 0), (0, pad_w))), jnp.pad(w3, ((0, pad_w), (0, 0))), jnp.pad(w3, ((0, pad_w), (0, 0))),
      deltas)


def _dft_matrices(seq_len):
    n = 2 * seq_len
    k = lax.broadcasted_iota(jnp.int32, (seq_len, seq_len), 0)
    t = lax.broadcasted_iota(jnp.int32, (seq_len, seq_len), 1)
    ang = ((k * t) % n).astype(F32) * (2.0 * math.pi / n)
    cr, base = jnp.cos(ang), -jnp.sin(ang)
    ci = jnp.where(k == 0, (1 - 2 * (t % 2)).astype(F32), base)
    cit = jnp.where(t == 0, (1 - 2 * (k % 2)).astype(F32), base)
    return cr.astype(BF16), ci.astype(BF16), cit.astype(BF16)


def _ctx_spectrum_body(cr_ref, ci_ref, hs_ref, hd_ref, nyq_ref, kr_ref, ki_ref):
    kr_ref[...] = jnp.dot(cr_ref[...], hs_ref[...].astype(BF16), preferred_element_type=F32)
    ki = jnp.dot(ci_ref[...], hd_ref[...].astype(BF16), preferred_element_type=F32)
    first = lax.broadcasted_iota(jnp.int32, ki.shape, 0) == 0
    ki_ref[...] = jnp.where(first, nyq_ref[...], ki)


def _ctx_spectrum(mats, h_sum, h_diff, nyq):
    tn = 512
    cr, ci, _ = mats
    mat = pl.BlockSpec((SEQ, SEQ), lambda c: (0, 0))
    chan = pl.BlockSpec((SEQ, tn), lambda c: (0, c))
    shape = jax.ShapeDtypeStruct((SEQ, D), F32)
    return pl.pallas_call(
        _ctx_spectrum_body,
        out_shape=(shape, shape),
        grid=(D // tn,),
        in_specs=[mat, mat, chan, chan, pl.BlockSpec((1, tn), lambda c: (0, c))],
        out_specs=(chan, chan),
        compiler_params=_params(1),
        name="hyena_filter_dft",
    )(cr, ci, h_sum, h_diff, nyq)


def _hy_conv_ctx_body(x0_ref, x1_ref, v_ref, w0_ref, w1_ref, wv_ref, b0_ref, b1_ref, bv_ref, kr_ref, ki_ref,
                      skip_ref, cr_ref, ci_ref, ct_ref, o_ref):
    n = x0_ref.shape[0]
    pos = lax.broadcasted_iota(jnp.int32, x0_ref.shape, 0) % SEQ

    def conv(u_ref, w_ref, b_ref):
        u = u_ref[...]
        prev = jnp.where(pos == 0, 0.0, pltpu.roll(u, 1, 0))
        nxt = jnp.where(pos == SEQ - 1, 0.0, pltpu.roll(u, n - 1, 0))
        return prev * w_ref[0:1, :] + u * w_ref[1:2, :] + nxt * w_ref[2:3, :] + b_ref[...]

    z = conv(v_ref, wv_ref, bv_ref) * conv(x1_ref, w1_ref, b1_ref)
    gated = z * skip_ref[...]
    x0 = conv(x0_ref, w0_ref, b0_ref)
    kr, ki = kr_ref[...], ki_ref[...]
    first = lax.broadcasted_iota(jnp.int32, kr.shape, 0) == 0
    for s in range(n // SEQ):
        rows = slice(s * SEQ, (s + 1) * SEQ)
        zs = z[rows].astype(BF16)
        zr = jnp.dot(cr_ref[...], zs, preferred_element_type=F32)
        zi = jnp.dot(ci_ref[...], zs, preferred_element_type=F32)
        yr = jnp.where(first, 0.5 * zr * kr, zr * kr - zi * ki).astype(BF16)
        yi = jnp.where(first, 0.5 * zi * ki, zr * ki + zi * kr).astype(BF16)
        y = (jnp.dot(cr_ref[...], yr, preferred_element_type=F32)
             + jnp.dot(ct_ref[...], yi, preferred_element_type=F32))
        o_ref[rows, :] = ((y * (1.0 / SEQ) + gated[rows]) * x0[rows]).astype(BF16)


def _hy_conv_ctx(u, conv_w, conv_b, kr, ki, skip, mats, *, seqs_per_step):
    tc = 256
    nb = D // tc
    rows = seqs_per_step * SEQ

    def col(part):
        return [pl.BlockSpec((rows, tc), lambda b, c: (b, part * nb + c)),
                pl.BlockSpec((3, tc), lambda b, c: (0, part * nb + c)),
                pl.BlockSpec((1, tc), lambda b, c: (0, part * nb + c))]

    specs = [col(p) for p in range(3)]
    chan = pl.BlockSpec((SEQ, tc), lambda b, c: (0, c))
    mat = pl.BlockSpec((SEQ, SEQ), lambda b, c: (0, 0))
    cb = conv_b.reshape(1, 3 * D)
    return pl.pallas_call(
        _hy_conv_ctx_body,
        out_shape=jax.ShapeDtypeStruct((N_CTX, D), BF16),
        grid=(BATCH // seqs_per_step, nb),
        in_specs=([s[0] for s in specs] + [s[1] for s in specs] + [s[2] for s in specs]
                  + [chan, chan, pl.BlockSpec((1, tc), lambda b, c: (0, c)), mat, mat, mat]),
        out_specs=pl.BlockSpec((rows, tc), lambda b, c: (b, c)),
        compiler_params=_params(2),
        name="hyena_conv_ctx",
    )(u, u, u, conv_w, conv_w, conv_w, cb, cb, cb, kr, ki, skip.reshape(1, D), *mats)


FFT_A, FFT_R = 64, 64
FFT_M = 2 * FFT_A
FFT_H = FFT_R // 2
assert FFT_A * FFT_R == DEC_SEQ


def _fft_consts():
    n = 2 * DEC_SEQ
    th = 2.0 * np.pi * (np.arange(FFT_M)[:, None] + 0.5) * np.arange(FFT_A)[None, :] / FFT_M
    f1 = np.concatenate([np.cos(th), -np.sin(th)], axis=0)
    k = np.arange(FFT_M)[:, None, None] + FFT_M * np.arange(FFT_H)[None, :, None] + 0.5
    ph = 2.0 * np.pi * k * np.arange(FFT_R)[None, None, :] / n
    c, s = np.cos(ph), np.sin(ph)
    g = np.concatenate([np.concatenate([c, s], axis=2), np.concatenate([-s, c], axis=2)], axis=1)
    as_bf16 = lambda m: jnp.asarray(m, dtype=F32).astype(BF16)
    return as_bf16(f1), as_bf16(f1.T), as_bf16(g), as_bf16(g.transpose(0, 2, 1))


def _fft_stage1(src_ref, y_ref, f1):
    for b in range(FFT_R):
        zb = src_ref[pl.ds(b, FFT_A, stride=FFT_R), :].astype(BF16)
        y_ref[b * 2 * FFT_M:(b + 1) * 2 * FFT_M, :] = jnp.dot(f1, zb, preferred_element_type=F32)


def _fft_stage2(y_ref, g_ref, k1, part=None):
    yr = y_ref[pl.ds(k1, FFT_R, stride=2 * FFT_M), :]
    yi = y_ref[pl.ds(FFT_M + k1, FFT_R, stride=2 * FFT_M), :]
    y = jnp.concatenate([yr, yi], axis=0).astype(BF16)
    if part is not None:
        return jnp.dot(g_ref[k1, part * FFT_H:(part + 1) * FFT_H, :], y, preferred_element_type=F32)
    z = jnp.dot(g_ref[k1], y, preferred_element_type=F32)
    return z[:FFT_H], z[FFT_H:]


def _hy_spectrum_body(hs_ref, hd_ref, f1_ref, g_ref, kr_ref, ki_ref, y_ref):
    f1 = f1_ref[...]
    _fft_stage1(hs_ref, y_ref, f1)
    for k1 in range(FFT_M):
        kr_ref[k1 * FFT_H:(k1 + 1) * FFT_H, :] = _fft_stage2(y_ref, g_ref, k1, part=0)
    _fft_stage1(hd_ref, y_ref, f1)
    for k1 in range(FFT_M):
        ki_ref[k1 * FFT_H:(k1 + 1) * FFT_H, :] = _fft_stage2(y_ref, g_ref, k1, part=1)


def _hy_spectrum(h_sum, h_diff, consts):
    tc = 128
    f1, _, g, _ = consts
    blk = pl.BlockSpec((DEC_SEQ, tc), lambda c: (0, c))
    shape = jax.ShapeDtypeStruct((DEC_SEQ, D), F32)
    return pl.pallas_call(
        _hy_spectrum_body,
        out_shape=(shape, shape),
        grid=(D // tc,),
        in_specs=[blk, blk, pl.BlockSpec(f1.shape, lambda c: (0, 0)), pl.BlockSpec(g.shape, lambda c: (0, 0, 0))],
        out_specs=(blk, blk),
        scratch_shapes=[pltpu.VMEM((FFT_R * 2 * FFT_M, tc), F32)],
        compiler_params=_params(1),
        name="hyena_filter_fft",
    )(h_sum, h_diff, f1, g)


def _hy_conv_fft_body(x0_ref, x1_ref, v_ref, w0_ref, w1_ref, wv_ref, b0_ref, b1_ref, bv_ref, kr_ref, ki_ref,
                      skip_ref, f1_ref, f1t_ref, g_ref, gt_ref, o_ref, z_ref, y_ref, t_ref):
    rows = 512

    def conv(u_ref, w_ref, b_ref, r):
        u = u_ref[r:r + rows, :]
        row = lax.broadcasted_iota(jnp.int32, u.shape, 0)
        before = u_ref[r - 1:r, :] if r > 0 else jnp.zeros_like(u[0:1])
        after = u_ref[r + rows:r + rows + 1, :] if r + rows < DEC_SEQ else jnp.zeros_like(u[0:1])
        prev = jnp.where(row == 0, before, pltpu.roll(u, 1, 0))
        nxt = jnp.where(row == rows - 1, after, pltpu.roll(u, rows - 1, 0))
        return prev * w_ref[0:1, :] + u * w_ref[1:2, :] + nxt * w_ref[2:3, :] + b_ref[...]

    for r in range(0, DEC_SEQ, rows):
        z_ref[r:r + rows, :] = conv(v_ref, wv_ref, bv_ref, r) * conv(x1_ref, w1_ref, b1_ref, r)
    _fft_stage1(z_ref, y_ref, f1_ref[...])
    for k1 in range(FFT_M):
        zr, zi = _fft_stage2(y_ref, g_ref, k1)
        kr = kr_ref[k1 * FFT_H:(k1 + 1) * FFT_H, :]
        ki = ki_ref[k1 * FFT_H:(k1 + 1) * FFT_H, :]
        p = jnp.concatenate([zr * kr - zi * ki, zr * ki + zi * kr], axis=0).astype(BF16)
        u = jnp.dot(gt_ref[k1], p, preferred_element_type=F32)
        y_ref[pl.ds(k1, FFT_R, stride=2 * FFT_M), :] = u[:FFT_R]
        y_ref[pl.ds(FFT_M + k1, FFT_R, stride=2 * FFT_M), :] = u[FFT_R:]
    f1t = f1t_ref[...]
    for b in range(FFT_R):
        yb = jnp.dot(f1t, y_ref[b * 2 * FFT_M:(b + 1) * 2 * FFT_M, :].astype(BF16), preferred_element_type=F32)
        t_ref[pl.ds(b, FFT_A, stride=FFT_R), :] = yb
    for r in range(0, DEC_SEQ, rows):
        y = t_ref[r:r + rows, :] * (1.0 / DEC_SEQ) + z_ref[r:r + rows, :] * skip_ref[...]
        o_ref[r:r + rows, :] = (y * conv(x0_ref, w0_ref, b0_ref, r)).astype(BF16)


def _hy_conv_fft(u, conv_w, conv_b, kr, ki, skip, consts, *, row0, n_seq):
    tc = 128
    nb = D // tc
    r0 = row0 // DEC_SEQ
    f1, f1t, g, gt = consts

    def col(part):
        return [pl.BlockSpec((DEC_SEQ, tc), lambda b, c: (r0 + b, part * nb + c)),
                pl.BlockSpec((3, tc), lambda b, c: (0, part * nb + c)),
                pl.BlockSpec((1, tc), lambda b, c: (0, part * nb + c))]

    specs = [col(p) for p in range(3)]
    chan = pl.BlockSpec((DEC_SEQ, tc), lambda b, c: (0, c))
    const = lambda m: pl.BlockSpec(m.shape, lambda b, c: (0,) * m.ndim)
    cb = conv_b.reshape(1, 3 * D)
    return pl.pallas_call(
        _hy_conv_fft_body,
        out_shape=jax.ShapeDtypeStruct((n_seq * DEC_SEQ, D), BF16),
        grid=(n_seq, nb),
        in_specs=([s[0] for s in specs] + [s[1] for s in specs] + [s[2] for s in specs]
                  + [chan, chan, pl.BlockSpec((1, tc), lambda b, c: (0, c))] + [const(m) for m in consts]),
        out_specs=pl.BlockSpec((DEC_SEQ, tc), lambda b, c: (b, c)),
        scratch_shapes=[pltpu.VMEM((DEC_SEQ, tc), F32), pltpu.VMEM((FFT_R * 2 * FFT_M, tc), F32),
                        pltpu.VMEM((DEC_SEQ, tc), F32)],
        compiler_params=_params(2, 56 * 1024 * 1024),
        name="hyena_conv_fft",
    )(u, u, u, conv_w, conv_w, conv_w, cb, cb, cb, kr, ki, skip.reshape(1, D), f1, f1t, g, gt)


ROW_TILE = D // LANES
ROUTER_TM = 512
EXPERT_TM = 512
N_SLOTS = 2 * N_TOK + N_EXPERTS * EXPERT_TM
N_SLOT_TILES = N_SLOTS // EXPERT_TM
INFO_E1, INFO_E2, INFO_R1, INFO_R2, INFO_W1, INFO_W2 = range(6)


def _to_row_tiles(ref, x):
    rows = x.shape[0]
    for j in range(ROW_TILE):
        ref[pl.ds(j, rows, stride=ROW_TILE), :] = x[:, j * LANES:(j + 1) * LANES]


def _from_row_tiles(ref, rows):
    return jnp.concatenate([ref[pl.ds(j, rows, stride=ROW_TILE), :] for j in range(ROW_TILE)], axis=-1)


def _router_body(x_ref, sh_ref, sc_ref, w_ref, info_ref, incl_ref, cnt_ref):
    @pl.when(pl.program_id(0) == 0)
    def _():
        cnt_ref[...] = jnp.zeros_like(cnt_ref)

    h = _modulate(x_ref, sh_ref, sc_ref)
    h_hi, w = h.astype(BF16), w_ref[...]
    h_lo, w_hi = (h - h_hi.astype(F32)).astype(BF16), w.astype(BF16)
    w_lo = (w - w_hi.astype(F32)).astype(BF16)
    logits = (jnp.dot(h_hi, w_hi, preferred_element_type=F32) + jnp.dot(h_lo, w_hi, preferred_element_type=F32)
              + jnp.dot(h_hi, w_lo, preferred_element_type=F32))
    lane = lax.broadcasted_iota(jnp.int32, logits.shape, 1).astype(F32)
    logits = jnp.where(lane < N_EXPERTS, logits, -jnp.inf)
    e = jnp.exp(logits - jnp.max(logits, axis=-1, keepdims=True))
    p = e / jnp.sum(e, axis=-1, keepdims=True)
    p1 = jnp.max(p, axis=-1, keepdims=True)
    i1 = jnp.min(jnp.where(p == p1, lane, float(LANES)), axis=-1, keepdims=True)
    rest = jnp.where(lane == i1, -1.0, p)
    p2 = jnp.max(rest, axis=-1, keepdims=True)
    i2 = jnp.min(jnp.where(rest == p2, lane, float(LANES)), axis=-1, keepdims=True)
    total = p1 + p2
    chosen = jnp.where((lane == i1) | (lane == i2), 1.0, 0.0)
    tm = chosen.shape[0]
    earlier = (lax.broadcasted_iota(jnp.int32, (tm, tm), 1) < lax.broadcasted_iota(jnp.int32, (tm, tm), 0))
    rank = jnp.dot(earlier.astype(BF16), chosen.astype(BF16), preferred_element_type=F32) + cnt_ref[...]
    r1 = jnp.sum(jnp.where(lane == i1, rank, 0.0), axis=-1, keepdims=True)
    r2 = jnp.sum(jnp.where(lane == i2, rank, 0.0), axis=-1, keepdims=True)
    cnt_ref[...] += jnp.sum(chosen, axis=0, keepdims=True)
    incl_ref[0] = jnp.broadcast_to(cnt_ref[...], incl_ref.shape[1:])
    info = jnp.zeros_like(p)
    for col, val in ((INFO_E1, i1), (INFO_E2, i2), (INFO_R1, r1), (INFO_R2, r2),
                     (INFO_W1, p1 / total), (INFO_W2, p2 / total)):
        info = jnp.where(lane == col, val, info)
    info_ref[...] = info


def _router(x, sh, sc, w_router):
    tm = ROUTER_TM
    return pl.pallas_call(
        _router_body,
        out_shape=(jax.ShapeDtypeStruct((N_TOK, LANES), F32),
                   jax.ShapeDtypeStruct((N_TOK // tm, 8, LANES), F32)),
        grid=(N_TOK // tm,),
        in_specs=_mod_specs(tm, 0) + [pl.BlockSpec((D, LANES), lambda i: (0, 0))],
        out_specs=(pl.BlockSpec((tm, LANES), lambda i: (i, 0)), pl.BlockSpec((1, 8, LANES), lambda i: (i, 0, 0))),
        scratch_shapes=[pltpu.VMEM((1, LANES), F32)],
        compiler_params=_params(1),
        name="moe_router",
    )(x, sh, sc, jnp.pad(w_router, ((0, 0), (0, LANES - N_EXPERTS))))


def _row_tile(ref, row):
    return ref.at[pl.ds(pl.multiple_of(row * ROW_TILE, ROW_TILE), ROW_TILE)]


def _dispatch_body(pos1_ref, pos2_ref, x_ref, zeros_hbm, xs_hbm, rows_ref, sem, *, tm):
    del zeros_hbm
    base = pl.program_id(0) * tm
    _to_row_tiles(rows_ref, x_ref[...])

    def copies(r):
        src = _row_tile(rows_ref, r)
        return (pltpu.make_async_copy(src, _row_tile(xs_hbm, pos1_ref[base + r]), sem),
                pltpu.make_async_copy(src, _row_tile(xs_hbm, pos2_ref[base + r]), sem))

    def issue(r, carry):
        for queue, cp in enumerate(copies(r)):
            cp.start(priority=queue)
        return carry

    lax.fori_loop(0, tm, issue, 0, unroll=8)
    for _ in range(2):
        pltpu.make_async_copy(rows_ref, xs_hbm.at[pl.ds(0, tm * ROW_TILE)], sem).wait()


def _dispatch(pos1, pos2, x):
    tm = 512
    return pl.pallas_call(
        functools.partial(_dispatch_body, tm=tm),
        out_shape=jax.ShapeDtypeStruct((N_SLOTS * ROW_TILE, LANES), F32),
        grid_spec=pltpu.PrefetchScalarGridSpec(
            num_scalar_prefetch=2, grid=(N_TOK // tm,),
            in_specs=[pl.BlockSpec((tm, D), lambda i, *_: (i, 0)), pl.BlockSpec(memory_space=pl.ANY)],
            out_specs=pl.BlockSpec(memory_space=pl.ANY),
            scratch_shapes=[pltpu.VMEM((tm * ROW_TILE, LANES), F32), pltpu.SemaphoreType.DMA(())]),
        input_output_aliases={3: 0},
        compiler_params=_params(1),
        name="moe_dispatch",
    )(pos1, pos2, x, jnp.zeros((N_SLOTS * ROW_TILE, LANES), F32))


def _new_expert(eid_ref, nv_ref, t):
    tt = jnp.minimum(t, nv_ref[0] - 1)
    return (t == 0) | (eid_ref[tt] != eid_ref[jnp.maximum(tt - 1, 0)])


def _expert_swiglu_body(eid_ref, b1_ref, b2_ref, nv_ref, xs_ref, sh_ref, sc_ref, wg_ref, wu_ref, o_ref,
                        wgb_ref, wub_ref, *, fc):
    t = pl.program_id(1)

    @pl.when(_new_expert(eid_ref, nv_ref, t))
    def _():
        wgb_ref[...] = wg_ref[0].astype(BF16)
        wub_ref[...] = wu_ref[0].astype(BF16)

    @pl.when(t < nv_ref[0])
    def _():
        tm = o_ref.shape[0]
        x = _from_row_tiles(xs_ref, tm)
        slot = t * tm + lax.broadcasted_iota(jnp.int32, (tm, 1), 0)
        in1, in2 = slot >= b1_ref[t], slot >= b2_ref[t]
        sc = jnp.where(in2, sc_ref[2], jnp.where(in1, sc_ref[1], sc_ref[0]))
        sh = jnp.where(in2, sh_ref[2], jnp.where(in1, sh_ref[1], sh_ref[0]))
        h = (x * (1.0 + sc) + sh).astype(BF16)
        for c in range(o_ref.shape[1] // fc):
            cs = slice(c * fc, (c + 1) * fc)
            g = jnp.dot(h, wgb_ref[:, cs], preferred_element_type=F32)
            u = jnp.dot(h, wub_ref[:, cs], preferred_element_type=F32)
            o_ref[:, cs] = (g * jax.nn.sigmoid(g) * u).astype(BF16)

    @pl.when(t >= nv_ref[0])
    def _():
        o_ref[...] = jnp.zeros_like(o_ref)


def _expert_swiglu(meta, xs_rt, sh, sc, w_gate, w_up, e0):
    eid, b1, b2, nv = meta
    tm, f = EXPERT_TM, w_gate.shape[2]
    fh = f // 2

    def tile(t, eid, b1, b2, nv):
        return jnp.minimum(t, nv[0] - 1)

    w_spec = pl.BlockSpec((1, D, fh), lambda p, t, eid, b1, b2, nv: (e0 + eid[tile(t, eid, b1, b2, nv)], 0, p))
    mod = pl.BlockSpec((8, 1, D), lambda p, t, *_: (0, 0, 0))
    return pl.pallas_call(
        functools.partial(_expert_swiglu_body, fc=256),
        out_shape=jax.ShapeDtypeStruct((N_SLOTS, f), BF16),
        grid_spec=pltpu.PrefetchScalarGridSpec(
            num_scalar_prefetch=4, grid=(2, N_SLOT_TILES),
            in_specs=[pl.BlockSpec((tm * ROW_TILE, LANES), lambda p, t, *m: (tile(t, *m), 0)), mod, mod,
                      w_spec, w_spec],
            out_specs=pl.BlockSpec((tm, fh), lambda p, t, *m: (t, p)),
            scratch_shapes=[pltpu.VMEM((D, fh), BF16), pltpu.VMEM((D, fh), BF16)]),
        compiler_params=_params(2, 56 * 1024 * 1024),
        name="moe_swiglu",
    )(eid, b1, b2, nv, xs_rt, sh, sc, w_gate, w_up)


def _expert_down_body(eid_ref, nv_ref, a_ref, w_ref, y_ref, wb_ref):
    t = pl.program_id(0)

    @pl.when(_new_expert(eid_ref, nv_ref, t))
    def _():
        wb_ref[...] = w_ref[0].astype(BF16)

    @pl.when(t < nv_ref[0])
    def _():
        _to_row_tiles(y_ref, jnp.dot(a_ref[...], wb_ref[...], preferred_element_type=F32))

    @pl.when(t >= nv_ref[0])
    def _():
        y_ref[...] = jnp.zeros_like(y_ref)


def _expert_down(meta, act, w_down, e0):
    eid, _, _, nv = meta
    tm, f = EXPERT_TM, act.shape[1]

    def tile(t, eid, nv):
        return jnp.minimum(t, nv[0] - 1)

    return pl.pallas_call(
        _expert_down_body,
        out_shape=jax.ShapeDtypeStruct((N_SLOTS * ROW_TILE, LANES), F32),
        grid_spec=pltpu.PrefetchScalarGridSpec(
            num_scalar_prefetch=2, grid=(N_SLOT_TILES,),
            in_specs=[pl.BlockSpec((tm, f), lambda t, *m: (tile(t, *m), 0)),
                      pl.BlockSpec((1, f, D), lambda t, eid, nv: (e0 + eid[tile(t, eid, nv)], 0, 0))],
            out_specs=pl.BlockSpec((tm * ROW_TILE, LANES), lambda t, *m: (t, 0)),
            scratch_shapes=[pltpu.VMEM((f, D), BF16)]),
        compiler_params=_params(1, 58 * 1024 * 1024),
        name="moe_down",
    )(eid, nv, act, w_down)


def _combine_body(pos1_ref, pos2_ref, y_hbm, info_ref, x_ref, g_ref, lng_ref, lnb_ref, o_ref, y1_ref, y2_ref, sem,
                  *, tm):
    base = pl.program_id(0) * tm

    def copies(r):
        return (pltpu.make_async_copy(_row_tile(y_hbm, pos1_ref[base + r]), _row_tile(y1_ref, r), sem),
                pltpu.make_async_copy(_row_tile(y_hbm, pos2_ref[base + r]), _row_tile(y2_ref, r), sem))

    def issue(r, carry):
        for queue, cp in enumerate(copies(r)):
            cp.start(priority=queue)
        return carry

    lax.fori_loop(0, tm, issue, 0, unroll=8)
    for y_ref in (y1_ref, y2_ref):
        pltpu.make_async_copy(y_hbm.at[pl.ds(0, tm * ROW_TILE)], y_ref, sem).wait()
    info = info_ref[...]
    ffn = (info[:, INFO_W1:INFO_W1 + 1] * _from_row_tiles(y1_ref, tm)
           + info[:, INFO_W2:INFO_W2 + 1] * _from_row_tiles(y2_ref, tm))
    y = ALPHA * x_ref[...] + g_ref[0] * ffn
    mu = jnp.mean(y, axis=-1, keepdims=True)
    yc = y - mu
    var = jnp.mean(yc * yc, axis=-1, keepdims=True)
    o_ref[...] = yc * lax.rsqrt(var + LN_EPS) * lng_ref[...] + lnb_ref[...]


def _combine(pos1, pos2, y_rt, info, x, gate, ln_g, ln_b):
    tm = 512
    return pl.pallas_call(
        functools.partial(_combine_body, tm=tm),
        out_shape=jax.ShapeDtypeStruct((N_TOK, D), F32),
        grid_spec=pltpu.PrefetchScalarGridSpec(
            num_scalar_prefetch=2, grid=(N_TOK // tm,),
            in_specs=[pl.BlockSpec(memory_space=pl.ANY),
                      pl.BlockSpec((tm, LANES), lambda i, *_: (i, 0)),
                      pl.BlockSpec((tm, D), lambda i, *_: (i, 0)),
                      pl.BlockSpec((1, 1, D), lambda i, *_: (_group_of_row(i * tm), 0, 0)),
                      pl.BlockSpec((1, D), lambda i, *_: (0, 0)),
                      pl.BlockSpec((1, D), lambda i, *_: (0, 0))],
            out_specs=pl.BlockSpec((tm, D), lambda i, *_: (i, 0)),
            scratch_shapes=[pltpu.VMEM((tm * ROW_TILE, LANES), F32), pltpu.VMEM((tm * ROW_TILE, LANES), F32),
                            pltpu.SemaphoreType.DMA(())]),
        compiler_params=_params(1),
        name="moe_combine",
    )(pos1, pos2, y_rt, info, x, gate, ln_g.reshape(1, D), ln_b.reshape(1, D))


def _slot_plan(info, incl):
    row = lambda n_rows: incl[n_rows // ROUTER_TM - 1, 0, :N_EXPERTS].astype(jnp.int32)
    count = row(N_TOK)
    padded = (count + EXPERT_TM - 1) // EXPERT_TM * EXPERT_TM
    end = jnp.cumsum(padded)
    start = end - padded
    tile_row = jnp.arange(N_SLOT_TILES, dtype=jnp.int32) * EXPERT_TM
    eid = jnp.minimum(jnp.sum(tile_row[:, None] >= end[None, :], axis=1), N_EXPERTS - 1).astype(jnp.int32)
    b1 = (start + row(N_CTX))[eid]
    b2 = (start + row(N_CTX + DEC_SEQ))[eid]
    nv = (end[-1:] // EXPERT_TM).astype(jnp.int32)
    experts = jnp.arange(N_EXPERTS, dtype=jnp.int32)
    start_of = lambda col: jnp.sum(jnp.where(info[:, col:col + 1].astype(jnp.int32) == experts, start, 0), axis=1)
    pos1 = start_of(INFO_E1) + info[:, INFO_R1].astype(jnp.int32)
    pos2 = start_of(INFO_E2) + info[:, INFO_R2].astype(jnp.int32)
    return pos1, pos2, (eid, b1, b2, nv)


def _even_mixer(x, sh, sc, gate, ln_g, ln_b, w_in, b_igate, b_fgate, ml_norm_g, q_norm_g, k_norm_g, w_out,
                st_c, st_n, st_m, cache_k, cache_v, rope_tabs):
    splits = (4 * ML_W, 4 * ML_W + N_GATES)
    w_main = jnp.concatenate([w_in[:, :splits[0]], w_in[:, splits[1]:]], axis=1).astype(BF16)
    proj = _mod_matmul(x, sh, sc, [w_main[None]], tm=1024, tn=MAIN_W // 2, out_dtype=F32, name="even_in_proj")[0]
    b_gate = jnp.stack([b_igate, b_fgate], axis=1).reshape(N_GATES)
    lic, bc, lir, br = _gates(x, sh, sc, w_in[:, splits[0]:splits[1]], b_gate)

    hf_c, hb_c, new_c, new_n, new_m = _mlstm(proj, lic, bc, lir, br, row0=0, n_seq=BATCH, seq_len=SEQ)
    init = (st_c, st_n, jnp.broadcast_to(st_m[..., None], st_n.shape))
    hf_s, hb_s, _, _, _ = _mlstm(proj, lic, bc, lir, br, row0=N_CTX, n_seq=DEC_BATCH, seq_len=DEC_SEQ, init=init)
    ml = _ml_post((hf_c, hf_s), (hb_c, hb_s), proj, ml_norm_g)

    q_c, kn_c, kb_c = _qk_prep(proj, q_norm_g, k_norm_g, row0=0, rows=N_CTX)
    q_s, _, kb_s = _qk_prep(proj, q_norm_g, k_norm_g, row0=N_CTX, rows=N_LAT, rope_tabs=rope_tabs)
    v_all = proj[:, MAIN_W - KV_W:]
    v_c, v_s = v_all[:N_CTX], v_all[N_CTX:]
    att_c = _attention(q_c, _head_major(kb_c, BATCH).swapaxes(2, 3),
                       _with_ones(_head_major(v_c.astype(BF16), BATCH)), tq=SEQ)
    k_lat = jnp.concatenate([kb_s.reshape(DEC_BATCH, DEC_SEQ, KV_W),
                             cache_k.reshape(DEC_BATCH, PAST_LEN, KV_W).astype(BF16)], axis=1)
    v_lat = jnp.concatenate([v_s.reshape(DEC_BATCH, DEC_SEQ, KV_W).astype(BF16),
                             cache_v.reshape(DEC_BATCH, PAST_LEN, KV_W).astype(BF16)], axis=1)
    att_s = _attention(q_s, _head_major(k_lat.reshape(-1, KV_W), DEC_BATCH).swapaxes(2, 3),
                       _with_ones(_head_major(v_lat.reshape(-1, KV_W), DEC_BATCH)), tq=256)

    x = _proj_res_ln([ml, (att_c, att_s)], w_out.astype(BF16), x, gate, ln_g, ln_b, tm=512, name="even_out_proj")
    new_k = kn_c.reshape(BATCH, SEQ, ATT_KV_HEADS, ATT_HEAD_DIM)
    new_v = v_c.reshape(BATCH, SEQ, ATT_KV_HEADS, ATT_HEAD_DIM)
    return x, new_k, new_v, new_c, new_n, new_m[..., 0]


def _hyena_mixer(x, sh, sc, gate, ln_g, ln_b, w_in, conv_w, conv_b, w1, b1, w2, b2, w3, sin_freq, skip, w_out, dft):
    u = _mod_matmul(x, sh, sc, [w_in.astype(BF16)[None]], tm=1024, tn=1536, out_dtype=F32, name="hyena_in_proj")[0]
    mats, consts = dft
    h_sum, h_diff, nyq = _hy_filter(SEQ, w1, b1, w2, b2, w3, sin_freq)
    kr, ki = _ctx_spectrum(mats, h_sum, h_diff, nyq)
    y_c = _hy_conv_ctx(u, conv_w, conv_b, kr, ki, skip, mats, seqs_per_step=8)
    h_sum, h_diff, _ = _hy_filter(DEC_SEQ, w1, b1, w2, b2, w3, sin_freq)
    kr, ki = _hy_spectrum(h_sum, h_diff, consts)
    y_s = _hy_conv_fft(u, conv_w, conv_b, kr, ki, skip, consts, row0=N_CTX, n_seq=DEC_BATCH)
    return _proj_res_ln([(y_c, y_s)], w_out.astype(BF16), x, gate, ln_g, ln_b, tm=512, name="hyena_out_proj")


def _dense_ffn(x, sh, sc, gate, ln_g, ln_b, w_gate, w_up, w_down):
    act = _mod_matmul(x, sh, sc, [w_gate.astype(BF16)[None], w_up.astype(BF16)[None]],
                      tm=1024, tn=D_FF // 2, out_dtype=BF16, name="ffn_swiglu")
    return _proj_res_ln([act[0]], w_down.astype(BF16), x, gate, ln_g, ln_b, tm=512, name="ffn_down")


def _moe_ffn(x, sh, sc, gate, ln_g, ln_b, w_router, w_gate, w_up, w_down, e0):
    info, incl = _router(x, sh, sc, w_router)
    pos1, pos2, meta = _slot_plan(info, incl)
    xs_rt = _dispatch(pos1, pos2, x)
    act = _expert_swiglu(meta, xs_rt, sh, sc, w_gate, w_up, e0)
    y_rt = _expert_down(meta, act, w_down, e0)
    return _combine(pos1, pos2, y_rt, info, x, gate, ln_g, ln_b)


def kernel(x_prompt, x_sample, cache_attn_k, cache_attn_v, state_mlstm_C, state_mlstm_n, state_mlstm_m, c, c_ctx, w_ada, b_ada, ln_g, ln_b, w_in_even, b_igate, b_fgate, ml_norm_g, q_norm_g, k_norm_g, w_out_even, w_ffn_gate, w_ffn_up, w_ffn_down, w_in_hy, hy_conv_w, hy_conv_b, hy_filt_w1, hy_filt_b1, hy_filt_w2, hy_filt_b2, hy_filt_w3, hy_sin_freq, hy_skip, w_out_hy, w_router, w_moe_gate, w_moe_up, w_moe_down):
    x = jnp.concatenate([x_prompt.reshape(N_CTX, D), x_sample.reshape(N_LAT, D)])
    cvec = jnp.concatenate([c_ctx[None], c, jnp.zeros((8 - 1 - DEC_BATCH, D), F32)])
    mods = _ada(cvec, w_ada, b_ada)
    rope_tabs = _rope_tables()
    dft = (_dft_matrices(SEQ), _fft_consts())
    moe_w = [w.reshape((-1,) + w.shape[2:]) for w in (w_moe_gate, w_moe_up, w_moe_down)]
    new_k, new_v, new_c, new_n, new_m = [], [], [], [], []
    for layer in range(DEPTH):
        sh1, sc1, g1, sh2, sc2, g2 = (mods[layer, :, i * D:(i + 1) * D].reshape(8, 1, D) for i in range(6))
        i = layer // 2
        if layer % 2 == 0:
            x, k_c, v_c, st_c, st_n, st_m = _even_mixer(
                x, sh1, sc1, g1, ln_g[layer, 0], ln_b[layer, 0], w_in_even[i], b_igate[i], b_fgate[i], ml_norm_g[i],
                q_norm_g[i], k_norm_g[i], w_out_even[i], state_mlstm_C[:, i], state_mlstm_n[:, i], state_mlstm_m[:, i],
                cache_attn_k[:, i], cache_attn_v[:, i], rope_tabs)
            new_k.append(k_c)
            new_v.append(v_c)
            new_c.append(st_c)
            new_n.append(st_n)
            new_m.append(st_m)
            x = _dense_ffn(x, sh2, sc2, g2, ln_g[layer, 1], ln_b[layer, 1], w_ffn_gate[i], w_ffn_up[i], w_ffn_down[i])
        else:
            x = _hyena_mixer(x, sh1, sc1, g1, ln_g[layer, 0], ln_b[layer, 0], w_in_hy[i], hy_conv_w[i], hy_conv_b[i],
                             hy_filt_w1[i], hy_filt_b1[i], hy_filt_w2[i], hy_filt_b2[i], hy_filt_w3[i], hy_sin_freq[i],
                             hy_skip[i], w_out_hy[i], dft)
            x = _moe_ffn(x, sh2, sc2, g2, ln_g[layer, 1], ln_b[layer, 1], w_router[i], *moe_w, i * N_EXPERTS)
    return (x[:N_CTX].reshape(BATCH, SEQ, D), x[N_CTX:].reshape(DEC_BATCH, DEC_SEQ, D),
            jnp.stack(new_k, axis=1), jnp.stack(new_v, axis=1), jnp.stack(new_c, axis=1),
            jnp.stack(new_n, axis=1), jnp.stack(new_m, axis=1))
```

```python
import functools
import math

import jax
import jax.numpy as jnp
import numpy as np
from jax import lax
from jax.experimental import pallas as pl
from jax.experimental.pallas import tpu as pltpu

F32 = jnp.float32
BF16 = jnp.bfloat16
HIGHEST = lax.Precision.HIGHEST

D = 1024
BATCH, SEQ = 32, 256
DEC_BATCH, DEC_SEQ = 2, 4096
DEPTH = 4
PAST_LEN = 256
GRID_W = 64
N_CTX = BATCH * SEQ
N_LAT = DEC_BATCH * DEC_SEQ
N_TOK = N_CTX + N_LAT

ML_HEADS, ML_HEAD_DIM = 4, 128
ML_W = ML_HEADS * ML_HEAD_DIM
CHUNK = 128
MLSTM_PAR = 2
ATT_HEADS, ATT_KV_HEADS, ATT_HEAD_DIM = 8, 2, 64
ATT_GROUP = ATT_HEADS // ATT_KV_HEADS
ATT_W = ATT_HEADS * ATT_HEAD_DIM
KV_W = ATT_KV_HEADS * ATT_HEAD_DIM
GROUP_W = ATT_GROUP * ATT_HEAD_DIM
ROPE_BASE = 10000.0
N_GATES = 4 * ML_HEADS
MAIN_W = 4 * ML_W + ATT_W + 2 * KV_W

HY_EMB = 33
HY_BANDS = (HY_EMB - 1) // 2
HY_TARGET, HY_SHORT_PCT, HY_LONG_PCT = 1e-2, 0.3, 1.5
D_FF = 2816
N_EXPERTS = 8
MOE_D_FF = 3584
ALPHA = (2 * DEPTH) ** 0.25
LN_EPS = 1e-5
RMS_EPS = 1e-6

LANES = 128
VMEM_LIMIT = 48 * 1024 * 1024


def _params(n_axes, vmem=VMEM_LIMIT):
    return pltpu.CompilerParams(dimension_semantics=("arbitrary",) * n_axes, vmem_limit_bytes=vmem)


def _group_of_row(r):
    return jnp.where(r < N_CTX, 0, 1 + (r - N_CTX) // DEC_SEQ)


def _modulate(x_ref, sh_ref, sc_ref):
    return x_ref[...] * (1.0 + sc_ref[0]) + sh_ref[0]


def _mod_specs(tm, row_axis):
    def rows(*ids):
        return (ids[row_axis], 0)

    def grp(*ids):
        return (_group_of_row(ids[row_axis] * tm), 0, 0)

    return [pl.BlockSpec((tm, D), rows), pl.BlockSpec((1, 1, D), grp), pl.BlockSpec((1, 1, D), grp)]


def _ada_body(c_ref, w_ref, b_ref, o_ref):
    c = c_ref[...]
    s = c * jax.nn.sigmoid(c)
    o_ref[0] = jnp.dot(s, w_ref[0], preferred_element_type=F32, precision=HIGHEST) + b_ref[0]


def _ada(cvec, w_ada, b_ada):
    tn = 1536
    return pl.pallas_call(
        _ada_body,
        out_shape=jax.ShapeDtypeStruct((DEPTH, 8, 6 * D), F32),
        grid=(DEPTH, 6 * D // tn),
        in_specs=[pl.BlockSpec((8, D), lambda l, j: (0, 0)),
                  pl.BlockSpec((1, D, tn), lambda l, j: (l, 0, j)),
                  pl.BlockSpec((1, 1, tn), lambda l, j: (l, 0, j))],
        out_specs=pl.BlockSpec((1, 8, tn), lambda l, j: (l, 0, j)),
        compiler_params=_params(2),
        name="ada_modulation",
    )(cvec, w_ada, b_ada.reshape(DEPTH, 1, 6 * D))


def _mod_mm_body(x_ref, sh_ref, sc_ref, *refs, n_w):
    w_refs, o_ref, h_ref = refs[:n_w], refs[n_w], refs[n_w + 1]

    @pl.when(pl.program_id(2) == 0)
    def _():
        h_ref[...] = _modulate(x_ref, sh_ref, sc_ref).astype(BF16)

    h = h_ref[...]
    if n_w == 1:
        o = jnp.dot(h, w_refs[0][0], preferred_element_type=F32)
    else:
        g = jnp.dot(h, w_refs[0][0], preferred_element_type=F32)
        u = jnp.dot(h, w_refs[1][0], preferred_element_type=F32)
        o = g * jax.nn.sigmoid(g) * u
    o_ref[0] = o.astype(o_ref.dtype)


def _mod_matmul(x, sh, sc, ws, *, tm, tn, out_dtype, name):
    n_e, _, f = ws[0].shape
    return pl.pallas_call(
        functools.partial(_mod_mm_body, n_w=len(ws)),
        out_shape=jax.ShapeDtypeStruct((n_e, N_TOK, f), out_dtype),
        grid=(n_e, N_TOK // tm, f // tn),
        in_specs=_mod_specs(tm, 1) + [pl.BlockSpec((1, D, tn), lambda e, i, j: (e, 0, j)) for _ in ws],
        out_specs=pl.BlockSpec((1, tm, tn), lambda e, i, j: (e, i, j)),
        scratch_shapes=[pltpu.VMEM((tm, D), BF16)],
        compiler_params=_params(3),
        name=name,
    )(x, sh, sc, *ws)


def _split_specs(tm, width):
    nc = N_CTX // tm
    return [pl.BlockSpec((tm, width), lambda i: (jnp.minimum(i, nc - 1), 0)),
            pl.BlockSpec((tm, width), lambda i: (jnp.maximum(i - nc, 0), 0))]


def _pick_split(ctx_ref, lat_ref):
    tm = ctx_ref.shape[0]
    return jnp.where(pl.program_id(0) < N_CTX // tm, ctx_ref[...], lat_ref[...])


def _proj_res_ln_body(*refs, split):
    n_in = sum(2 if s else 1 for s in split)
    part_refs = list(refs[:n_in])
    w_ref, x_ref, g_ref, lng_ref, lnb_ref, o_ref = refs[n_in:]
    cols = [_pick_split(part_refs.pop(0), part_refs.pop(0)) if s else part_refs.pop(0)[...] for s in split]
    a = cols[0] if len(cols) == 1 else jnp.concatenate(cols, axis=-1)
    y = ALPHA * x_ref[...] + g_ref[0] * jnp.dot(a, w_ref[...], preferred_element_type=F32)
    mu = jnp.mean(y, axis=-1, keepdims=True)
    yc = y - mu
    var = jnp.mean(yc * yc, axis=-1, keepdims=True)
    o_ref[...] = yc * lax.rsqrt(var + LN_EPS) * lng_ref[...] + lnb_ref[...]


def _proj_res_ln(parts, w, x, gate, ln_g, ln_b, *, tm, name):
    split = tuple(isinstance(p, tuple) for p in parts)
    in_specs, args = [], []
    for p, s in zip(parts, split):
        if s:
            in_specs += _split_specs(tm, p[0].shape[1])
            args += list(p)
        else:
            in_specs.append(pl.BlockSpec((tm, p.shape[1]), lambda i: (i, 0)))
            args.append(p)
    in_specs += [pl.BlockSpec(w.shape, lambda i: (0, 0)),
                 pl.BlockSpec((tm, D), lambda i: (i, 0)),
                 pl.BlockSpec((1, 1, D), lambda i: (_group_of_row(i * tm), 0, 0)),
                 pl.BlockSpec((1, D), lambda i: (0, 0)),
                 pl.BlockSpec((1, D), lambda i: (0, 0))]
    return pl.pallas_call(
        functools.partial(_proj_res_ln_body, split=split),
        out_shape=jax.ShapeDtypeStruct((N_TOK, D), F32),
        grid=(N_TOK // tm,),
        in_specs=in_specs,
        out_specs=pl.BlockSpec((tm, D), lambda i: (i, 0)),
        compiler_params=_params(1),
        name=name,
    )(*args, w, x, gate, ln_g.reshape(1, D), ln_b.reshape(1, D))


def _log_sigmoid(x):
    return jnp.minimum(x, 0.0) - jnp.log(1.0 + jnp.exp(-jnp.abs(x)))


def _gates_body(x_ref, sh_ref, sc_ref, wg_ref, wgt_ref, b_ref, bt_ref,
                lic_ref, bc_ref, lir_ref, br_ref, *, tm):
    h = _modulate(x_ref, sh_ref, sc_ref).astype(BF16)
    g = jnp.dot(h, wg_ref[...], preferred_element_type=F32) + b_ref[...]
    gt = lax.dot_general(wgt_ref[...], h, (((1,), (1,)), ((), ())), preferred_element_type=F32) + bt_ref[...]
    lic_ref[...] = g
    lir_ref[...] = gt
    lf, lft = _log_sigmoid(g), _log_sigmoid(gt)
    r = lax.broadcasted_iota(jnp.int32, (CHUNK, CHUNK), 0)
    c = lax.broadcasted_iota(jnp.int32, (CHUNK, CHUNK), 1)
    tri_l = (c <= r).astype(F32)
    tri_u = (c >= r).astype(F32)
    fwd_col = lax.broadcasted_iota(jnp.int32, (CHUNK, LANES), 1) < 2 * ML_HEADS
    fwd_row = lax.broadcasted_iota(jnp.int32, (N_GATES, CHUNK), 0) < 2 * ML_HEADS
    for ch in range(tm // CHUNK):
        sl = slice(ch * CHUNK, (ch + 1) * CHUNK)
        lfc, lftc = lf[sl, :], lft[:, sl]
        cum_f = jnp.dot(tri_l, lfc, preferred_element_type=F32, precision=HIGHEST)
        cum_b = jnp.dot(tri_u, lfc, preferred_element_type=F32, precision=HIGHEST)
        bc_ref[sl, :] = jnp.where(fwd_col, cum_f, cum_b)
        cum_f = jnp.dot(lftc, tri_u, preferred_element_type=F32, precision=HIGHEST)
        cum_b = jnp.dot(lftc, tri_l, preferred_element_type=F32, precision=HIGHEST)
        br_ref[:, sl] = jnp.where(fwd_row, cum_f, cum_b)


def _gates(x, sh, sc, wg, b_gate):
    tm = 256
    wg_pad = jnp.pad(wg, ((0, 0), (0, LANES - N_GATES)))
    b_pad = jnp.pad(b_gate, (0, LANES - N_GATES)).reshape(1, LANES)
    col = pl.BlockSpec((tm, LANES), lambda i: (i, 0))
    row = pl.BlockSpec((N_GATES, tm), lambda i: (0, i))
    return pl.pallas_call(
        functools.partial(_gates_body, tm=tm),
        out_shape=(jax.ShapeDtypeStruct((N_TOK, LANES), F32), jax.ShapeDtypeStruct((N_TOK, LANES), F32),
                   jax.ShapeDtypeStruct((N_GATES, N_TOK), F32), jax.ShapeDtypeStruct((N_GATES, N_TOK), F32)),
        grid=(N_TOK // tm,),
        in_specs=_mod_specs(tm, 0) + [pl.BlockSpec((D, LANES), lambda i: (0, 0)),
                                         pl.BlockSpec((N_GATES, D), lambda i: (0, 0)),
                                         pl.BlockSpec((1, LANES), lambda i: (0, 0)),
                                         pl.BlockSpec((N_GATES, 1), lambda i: (0, 0))],
        out_specs=(col, col, row, row),
        compiler_params=_params(1),
        name="mlstm_gates",
    )(x, sh, sc, wg_pad.astype(BF16), wg.T.astype(BF16), b_pad, b_gate.reshape(N_GATES, 1))


def _mlstm_body(*refs, has_init):
    (qf, kf, vf, licf, bcf, lirf, brf, qb, kb, vb, licb, bcb, lirb, brb) = refs[:14]
    refs = refs[14:]
    if has_init:
        c0_ref, n0_ref, m0_ref = refs[:3]
        refs = refs[3:]
    hf_ref, hb_ref, c_ref, n_ref, m_ref = refs

    @pl.when(pl.program_id(1) == 0)
    def _():
        if has_init:
            c_ref[...] = c0_ref[...]
            n_ref[...] = n0_ref[...]
            m_ref[...] = m0_ref[...]
        else:
            c_ref[...] = jnp.zeros_like(c_ref)
            n_ref[...] = jnp.zeros_like(n_ref)
            m_ref[...] = jnp.zeros_like(m_ref)

    t_idx = lax.broadcasted_iota(jnp.int32, (CHUNK, CHUNK), 0)
    s_idx = lax.broadcasted_iota(jnp.int32, (CHUNK, CHUNK), 1)
    nt = (((1,), (1,)), ((), ()))
    stores = []

    def chain(u, d, h, q_ref, k_ref, v_ref, lic_ref, bc_ref, lir_ref, br_ref, h_ref):
        mask = (s_idx <= t_idx) if d == 0 else (s_idx >= t_idx)
        hs = slice(h * ML_HEAD_DIM, (h + 1) * ML_HEAD_DIM)
        gi, gf = d * 2 * ML_HEADS + h, d * 2 * ML_HEADS + ML_HEADS + h
        q = q_ref[u, :, hs]
        k = k_ref[u, :, hs] * (ML_HEAD_DIM ** -0.5)
        v = v_ref[u, :, hs]
        qh, kh, vh = q.astype(BF16), k.astype(BF16), v.astype(BF16)
        li_c, b_c = lic_ref[u, :, gi:gi + 1], bc_ref[u, :, gf:gf + 1]
        li_r, b_r = lir_ref[u, gi:gi + 1, :], br_ref[u, gf:gf + 1, :]
        c_st = c_ref[u, d, h]
        n_st = n_ref[u, d, h:h + 1, :]
        m_st = m_ref[u, d, h:h + 1, :][:, 0:1]
        dmat = jnp.where(mask, b_c - b_r + li_r, -jnp.inf)
        inter = b_c + m_st
        m_out = jnp.maximum(inter, jnp.max(dmat, axis=-1, keepdims=True))
        p = jnp.exp(dmat - m_out)
        w_inter = jnp.exp(inter - m_out)
        yield
        qk = lax.dot_general(qh, kh, nt, preferred_element_type=F32)
        qc = jnp.dot(qh, c_st.astype(BF16), preferred_element_type=F32)
        yield
        s = qk * p
        den = (jnp.sum(s, axis=-1, keepdims=True)
               + w_inter * jnp.sum(q * n_st, axis=-1, keepdims=True))
        sh = s.astype(BF16)
        b_last = b_r[:, CHUNK - 1:CHUNK] if d == 0 else b_r[:, 0:1]
        g_r = b_last - b_r + li_r
        g_c = b_last - b_c + li_c
        m_new = jnp.maximum(b_last + m_st, jnp.max(g_r, axis=-1, keepdims=True))
        decay = jnp.exp(b_last + m_st - m_new)
        kw = k * jnp.exp(g_c - m_new)
        kwh = kw.astype(BF16)
        yield
        sv = jnp.dot(sh, vh, preferred_element_type=F32)
        kv = lax.dot_general(kwh, vh, (((0,), (0,)), ((), ())), preferred_element_type=F32)
        yield
        h_out = (sv + w_inter * qc) / jnp.maximum(jnp.abs(den), jnp.exp(-m_out))
        c_new = decay * c_st + kv
        n_new = decay * n_st + jnp.sum(kw, axis=0, keepdims=True)
        stores.append((h_ref, u, d, h, hs, h_out, c_new, n_new, jnp.broadcast_to(m_new, (1, ML_HEAD_DIM))))
        yield

    chains = [chain(u, d, h, *group)
              for u in range(c_ref.shape[0])
              for d, group in enumerate(((qf, kf, vf, licf, bcf, lirf, brf, hf_ref),
                                         (qb, kb, vb, licb, bcb, lirb, brb, hb_ref)))
              for h in range(ML_HEADS)]
    for _ in range(5):
        for ch in chains:
            next(ch)
    for h_ref, u, d, h, hs, h_out, c_new, n_new, m_new in stores:
        h_ref[u, :, hs] = h_out
        c_ref[u, d, h] = c_new
        n_ref[u, d, h:h + 1, :] = n_new
        m_ref[u, d, h:h + 1, :] = m_new


def _mlstm(proj, lic, bc, lir, br, *, row0, n_seq, seq_len, init=None):
    nc = seq_len // CHUNK
    par = MLSTM_PAR
    g0 = row0 // seq_len // par
    seqs = lambda a: a.reshape(N_TOK // seq_len, seq_len, a.shape[-1])
    rows_of = lambda a: seqs(a.T).transpose(0, 2, 1)

    def chunk_specs(chunk):
        return ([pl.BlockSpec((par, CHUNK, ML_W), lambda b, j, c=c: (g0 + b, chunk(j), c)) for c in range(3)]
                + [pl.BlockSpec((par, CHUNK, LANES), lambda b, j: (g0 + b, chunk(j), 0))] * 2
                + [pl.BlockSpec((par, N_GATES, CHUNK), lambda b, j: (g0 + b, 0, chunk(j)))] * 2)

    fwd, bwd = (lambda j: j), (lambda j: nc - 1 - j)
    st_c = pl.BlockSpec((par, 2, ML_HEADS, ML_HEAD_DIM, ML_HEAD_DIM), lambda b, j: (b, 0, 0, 0, 0))
    st_n = pl.BlockSpec((par, 2, ML_HEADS, ML_HEAD_DIM), lambda b, j: (b, 0, 0, 0))
    in_specs = chunk_specs(fwd) + chunk_specs(bwd)
    args = [seqs(proj)] * 3 + [seqs(lic), seqs(bc), rows_of(lir), rows_of(br)]
    args = args * 2
    if init is not None:
        in_specs += [st_c, st_n, st_n]
        args += list(init)
    h_shape = jax.ShapeDtypeStruct((n_seq, seq_len, ML_W), F32)
    hf, hb, c_st, n_st, m_st = pl.pallas_call(
        functools.partial(_mlstm_body, has_init=init is not None),
        out_shape=(h_shape, h_shape,
                   jax.ShapeDtypeStruct((n_seq, 2, ML_HEADS, ML_HEAD_DIM, ML_HEAD_DIM), F32),
                   jax.ShapeDtypeStruct((n_seq, 2, ML_HEADS, ML_HEAD_DIM), F32),
                   jax.ShapeDtypeStruct((n_seq, 2, ML_HEADS, ML_HEAD_DIM), F32)),
        grid=(n_seq // par, nc),
        in_specs=in_specs,
        out_specs=(pl.BlockSpec((par, CHUNK, ML_W), lambda b, j: (b, fwd(j), 0)),
                   pl.BlockSpec((par, CHUNK, ML_W), lambda b, j: (b, bwd(j), 0)),
                   st_c, st_n, st_n),
        compiler_params=_params(2),
        name="mlstm_scan",
    )(*args)
    return hf.reshape(-1, ML_W), hb.reshape(-1, ML_W), c_st, n_st, m_st


def _ml_post_body(hfc_ref, hfs_ref, hbc_ref, hbs_ref, o_ref, g_ref, out_ref):
    h = _pick_split(hfc_ref, hfs_ref) + _pick_split(hbc_ref, hbs_ref)
    gate = jax.nn.sigmoid(o_ref[...]) * g_ref[...]
    for hd in range(ML_HEADS):
        hs = slice(hd * ML_HEAD_DIM, (hd + 1) * ML_HEAD_DIM)
        x = h[:, hs]
        xc = x - jnp.mean(x, axis=-1, keepdims=True)
        var = jnp.mean(xc * xc, axis=-1, keepdims=True)
        out_ref[:, hs] = (gate[:, hs] * (xc * lax.rsqrt(var + RMS_EPS))).astype(BF16)


def _ml_post(hf, hb, proj, norm_g):
    tm = 512
    blk = pl.BlockSpec((tm, ML_W), lambda i: (i, 0))
    return pl.pallas_call(
        _ml_post_body,
        out_shape=jax.ShapeDtypeStruct((N_TOK, ML_W), BF16),
        grid=(N_TOK // tm,),
        in_specs=(_split_specs(tm, ML_W) * 2
                  + [pl.BlockSpec((tm, ML_W), lambda i: (i, 3)), pl.BlockSpec((1, ML_W), lambda i: (0, 0))]),
        out_specs=blk,
        compiler_params=_params(1),
        name="mlstm_out_norm",
    )(*hf, *hb, proj, norm_g.reshape(1, ML_W))


def _head_rms(x, gain):
    lane_head = lax.broadcasted_iota(jnp.int32, x.shape, 1) // ATT_HEAD_DIM
    sq = x * x
    ms = jnp.zeros_like(x)
    for hd in range(x.shape[1] // ATT_HEAD_DIM):
        sel = lane_head == hd
        ms = jnp.where(sel, jnp.sum(jnp.where(sel, sq, 0.0), axis=-1, keepdims=True), ms)
    return x * lax.rsqrt(ms * (1.0 / ATT_HEAD_DIM) + RMS_EPS) * gain


def _rope(x, cos, sin_signed):
    w = x.shape[1]
    even = lax.broadcasted_iota(jnp.int32, x.shape, 1) % 2 == 0
    partner = jnp.where(even, pltpu.roll(x, w - 1, 1), pltpu.roll(x, 1, 1))
    return x * cos + partner * sin_signed


def _qk_prep_body(q_ref, k_ref, qg_ref, kg_ref, *refs, rope):
    if rope:
        cq_ref, sq_ref, ck_ref, sk_ref, qo_ref, kn_ref, kr_ref = refs
    else:
        qo_ref, kn_ref, kr_ref = refs
    q = _head_rms(q_ref[...], qg_ref[...])
    k = _head_rms(k_ref[...], kg_ref[...])
    kn_ref[...] = k
    if rope:
        q = _rope(q, cq_ref[...], sq_ref[...])
        k = _rope(k, ck_ref[...], sk_ref[...])
    qo_ref[...] = (q * (ATT_HEAD_DIM ** -0.5)).astype(BF16)
    kr_ref[...] = k.astype(BF16)


def _qk_prep(proj, q_gain, k_gain, *, row0, rows, rope_tabs=None):
    tm = 512
    r0 = row0 // tm
    in_specs = [pl.BlockSpec((tm, ATT_W), lambda i: (r0 + i, 4 * ML_W // ATT_W)),
                pl.BlockSpec((tm, KV_W), lambda i: (r0 + i, (4 * ML_W + ATT_W) // KV_W)),
                pl.BlockSpec((1, ATT_W), lambda i: (0, 0)),
                pl.BlockSpec((1, KV_W), lambda i: (0, 0))]
    args = [proj, proj, jnp.tile(q_gain, ATT_HEADS).reshape(1, ATT_W), jnp.tile(k_gain, ATT_KV_HEADS).reshape(1, KV_W)]
    if rope_tabs is not None:
        per_seq = DEC_SEQ // tm
        in_specs += [pl.BlockSpec((tm, ATT_W), lambda i: (i % per_seq, 0))] * 2
        in_specs += [pl.BlockSpec((tm, KV_W), lambda i: (i % per_seq, 0))] * 2
        args += list(rope_tabs)
    return pl.pallas_call(
        functools.partial(_qk_prep_body, rope=rope_tabs is not None),
        out_shape=(jax.ShapeDtypeStruct((rows, ATT_W), BF16), jax.ShapeDtypeStruct((rows, KV_W), F32),
                   jax.ShapeDtypeStruct((rows, KV_W), BF16)),
        grid=(rows // tm,),
        in_specs=in_specs,
        out_specs=(pl.BlockSpec((tm, ATT_W), lambda i: (i, 0)), pl.BlockSpec((tm, KV_W), lambda i: (i, 0)),
                   pl.BlockSpec((tm, KV_W), lambda i: (i, 0))),
        compiler_params=_params(1),
        name="attn_qk_prep",
    )(*args)


def _rope_tables():
    rows = DEC_SEQ // GRID_W
    axis_dim = ATT_HEAD_DIM // 2
    row = jnp.repeat(jnp.arange(rows, dtype=F32), GRID_W)
    col = (jnp.arange(DEC_SEQ) % GRID_W).astype(F32)
    inv = ROPE_BASE ** (-jnp.arange(axis_dim // 2, dtype=F32) * 2.0 / axis_dim)
    ang = jnp.concatenate([row[:, None] * inv, col[:, None] * inv], axis=-1)
    cos = jnp.repeat(jnp.cos(ang), 2, axis=-1)
    sin = jnp.repeat(jnp.sin(ang), 2, axis=-1) * jnp.tile(jnp.array([-1.0, 1.0], F32), axis_dim)
    return (jnp.tile(cos, (1, ATT_HEADS)), jnp.tile(sin, (1, ATT_HEADS)),
            jnp.tile(cos, (1, ATT_KV_HEADS)), jnp.tile(sin, (1, ATT_KV_HEADS)))


def _attn_body(q_ref, k_ref, v_ref, o_ref):
    k, v = k_ref[0, 0], v_ref[0, 0]
    dh = k.shape[0]
    g = q_ref.shape[1] // dh
    pair = 2

    def heads(h0):
        q = jnp.concatenate([q_ref[:, h * dh:(h + 1) * dh] for h in range(h0, h0 + pair)], axis=0)
        s = jnp.dot(q, k, preferred_element_type=F32)
        yield
        e = jnp.exp((s - jnp.max(s, axis=-1, keepdims=True)).astype(BF16))
        yield
        o = jnp.dot(e, v, preferred_element_type=F32)
        o = (o[:, :dh] / o[:, dh:dh + 1]).astype(BF16)
        tq = q_ref.shape[0]
        for j in range(pair):
            o_ref[:, (h0 + j) * dh:(h0 + j + 1) * dh] = o[j * tq:(j + 1) * tq]
        yield

    chains = [heads(h0) for h0 in range(0, g, pair)]
    for step in range(len(chains) + 2):
        for i, chain in enumerate(chains):
            if 0 <= step - i < 3:
                next(chain)


def _attention(q, k_t, v_ones, *, tq):
    n_seq, _, dh, s_len = k_t.shape
    nq = q.shape[0] // n_seq // tq
    qo = pl.BlockSpec((tq, GROUP_W), lambda b, kh, i: (b * nq + i, kh))
    return pl.pallas_call(
        _attn_body,
        out_shape=jax.ShapeDtypeStruct(q.shape, BF16),
        grid=(n_seq, ATT_KV_HEADS, nq),
        in_specs=[qo, pl.BlockSpec((1, 1, dh, s_len), lambda b, kh, i: (b, kh, 0, 0)),
                  pl.BlockSpec((1, 1, s_len, LANES), lambda b, kh, i: (b, kh, 0, 0))],
        out_specs=qo,
        compiler_params=_params(3),
        name="attention",
    )(q, k_t, v_ones)


def _head_major(x, n_seq):
    return x.reshape(n_seq, -1, x.shape[1] // ATT_HEAD_DIM, ATT_HEAD_DIM).transpose(0, 2, 1, 3)


def _with_ones(v):
    pad = jnp.zeros(v.shape[:-1] + (LANES - ATT_HEAD_DIM - 1,), v.dtype)
    return jnp.concatenate([v, jnp.ones(v.shape[:-1] + (1,), v.dtype), pad], axis=-1)


def _hy_filter_body(feat_ref, t_ref, w1_ref, b1_ref, w2_ref, b2_ref, fr_ref, w3f_ref, w3b_ref, dl_ref,
                    hsum_ref, hdiff_ref, nyq_ref, z_ref):
    @pl.when(pl.program_id(0) == 0)
    def _():
        z = jnp.dot(feat_ref[...], w1_ref[...], preferred_element_type=F32, precision=HIGHEST) + b1_ref[...]
        z = jnp.sin(fr_ref[0:1, :] * z)
        z = jnp.dot(z, w2_ref[...], preferred_element_type=F32, precision=HIGHEST) + b2_ref[...]
        z_ref[...] = jnp.sin(fr_ref[1:2, :] * z)

    z = z_ref[...]
    window = jnp.exp(-t_ref[...] * dl_ref[...])
    h_f = jnp.dot(z, w3f_ref[...], preferred_element_type=F32, precision=HIGHEST) * window
    h_b = jnp.dot(z, w3b_ref[...], preferred_element_type=F32, precision=HIGHEST) * window
    row = lax.broadcasted_iota(jnp.int32, h_f.shape, 0)
    h_b = jnp.where(row == 0, 0.0, h_b)
    inv = 1.0 / (jnp.sum(jnp.abs(h_f), axis=0, keepdims=True) + jnp.sum(jnp.abs(h_b), axis=0, keepdims=True))
    h_sum = (h_f + h_b) * inv
    hsum_ref[...] = h_sum
    hdiff_ref[...] = (h_f - h_b) * inv
    nyq_ref[...] = jnp.sum(jnp.where(row % 2 == 0, h_sum, -h_sum), axis=0, keepdims=True)


def _hy_filter(seq_len, w1, b1, w2, b2, w3, sin_freq):
    tc = 256
    fw = w1.shape[1]
    t = jnp.arange(seq_len, dtype=F32)[:, None] / seq_len
    bands = jnp.arange(1, HY_BANDS + 1, dtype=F32)[None, :]
    feat = jnp.concatenate([t, jnp.sin(2.0 * math.pi * bands * t), jnp.cos(2.0 * math.pi * bands * t)], axis=-1)
    feat = jnp.pad(feat, ((0, 0), (0, LANES - HY_EMB)))
    deltas = jnp.abs(jnp.linspace(math.log(HY_TARGET) / HY_LONG_PCT, math.log(HY_TARGET) / HY_SHORT_PCT, D,
                                  dtype=F32)).reshape(1, D)
    pad_w = LANES - fw
    full = lambda shape: pl.BlockSpec(shape, lambda j: (0,) * len(shape))
    return pl.pallas_call(
        _hy_filter_body,
        out_shape=(jax.ShapeDtypeStruct((seq_len, D), F32), jax.ShapeDtypeStruct((seq_len, D), F32),
                   jax.ShapeDtypeStruct((1, D), F32)),
        grid=(D // tc,),
        in_specs=[full((seq_len, LANES)), full((seq_len, 1)), full((LANES, LANES)), full((1, LANES)),
                  full((LANES, LANES)), full((1, LANES)), full((2, LANES)),
                  pl.BlockSpec((LANES, tc), lambda j: (0, j)), pl.BlockSpec((LANES, tc), lambda j: (0, D // tc + j)),
                  pl.BlockSpec((1, tc), lambda j: (0, j))],
        out_specs=(pl.BlockSpec((seq_len, tc), lambda j: (0, j)), pl.BlockSpec((seq_len, tc), lambda j: (0, j)),
                   pl.BlockSpec((1, tc), lambda j: (0, j))),
        scratch_shapes=[pltpu.VMEM((seq_len, LANES), F32)],
        compiler_params=_params(1),
        name="hyena_filter",
    )(feat, t, jnp.pad(w1, ((0, LANES - HY_EMB), (0, pad_w))), jnp.pad(b1, (0, pad_w)).reshape(1, LANES),
      jnp.pad(w2, ((0, pad_w), (0, pad_w))), jnp.pad(b2, (0, pad_w)).reshape(1, LANES),
      jnp.pad(sin_freq, ((0, 0), (0, pad_w))), jnp.pad(w3, ((0, pad_w), (0, 0))), jnp.pad(w3, ((0, pad_w), (0, 0))),
      deltas)


def _dft_matrices(seq_len):
    n = 2 * seq_len
    k = lax.broadcasted_iota(jnp.int32, (seq_len, seq_len), 0)
    t = lax.broadcasted_iota(jnp.int32, (seq_len, seq_len), 1)
    ang = ((k * t) % n).astype(F32) * (2.0 * math.pi / n)
    cr, base = jnp.cos(ang), -jnp.sin(ang)
    ci = jnp.where(k == 0, (1 - 2 * (t % 2)).astype(F32), base)
    cit = jnp.where(t == 0, (1 - 2 * (k % 2)).astype(F32), base)
    return cr.astype(BF16), ci.astype(BF16), cit.astype(BF16)


def _ctx_spectrum_body(cr_ref, ci_ref, hs_ref, hd_ref, nyq_ref, kr_ref, ki_ref):
    kr_ref[...] = jnp.dot(cr_ref[...], hs_ref[...].astype(BF16), preferred_element_type=F32)
    ki = jnp.dot(ci_ref[...], hd_ref[...].astype(BF16), preferred_element_type=F32)
    first = lax.broadcasted_iota(jnp.int32, ki.shape, 0) == 0
    ki_ref[...] = jnp.where(first, nyq_ref[...], ki)


def _ctx_spectrum(mats, h_sum, h_diff, nyq):
    tn = 512
    cr, ci, _ = mats
    mat = pl.BlockSpec((SEQ, SEQ), lambda c: (0, 0))
    chan = pl.BlockSpec((SEQ, tn), lambda c: (0, c))
    shape = jax.ShapeDtypeStruct((SEQ, D), F32)
    return pl.pallas_call(
        _ctx_spectrum_body,
        out_shape=(shape, shape),
        grid=(D // tn,),
        in_specs=[mat, mat, chan, chan, pl.BlockSpec((1, tn), lambda c: (0, c))],
        out_specs=(chan, chan),
        compiler_params=_params(1),
        name="hyena_filter_dft",
    )(cr, ci, h_sum, h_diff, nyq)


def _hy_conv_ctx_body(x0_ref, x1_ref, v_ref, w0_ref, w1_ref, wv_ref, b0_ref, b1_ref, bv_ref, kr_ref, ki_ref,
                      skip_ref, cr_ref, ci_ref, ct_ref, o_ref):
    n = x0_ref.shape[0]
    pos = lax.broadcasted_iota(jnp.int32, x0_ref.shape, 0) % SEQ

    def conv(u_ref, w_ref, b_ref):
        u = u_ref[...]
        prev = jnp.where(pos == 0, 0.0, pltpu.roll(u, 1, 0))
        nxt = jnp.where(pos == SEQ - 1, 0.0, pltpu.roll(u, n - 1, 0))
        return prev * w_ref[0:1, :] + u * w_ref[1:2, :] + nxt * w_ref[2:3, :] + b_ref[...]

    z = conv(v_ref, wv_ref, bv_ref) * conv(x1_ref, w1_ref, b1_ref)
    gated = z * skip_ref[...]
    x0 = conv(x0_ref, w0_ref, b0_ref)
    kr, ki = kr_ref[...], ki_ref[...]
    first = lax.broadcasted_iota(jnp.int32, kr.shape, 0) == 0
    for s in range(n // SEQ):
        rows = slice(s * SEQ, (s + 1) * SEQ)
        zs = z[rows].astype(BF16)
        zr = jnp.dot(cr_ref[...], zs, preferred_element_type=F32)
        zi = jnp.dot(ci_ref[...], zs, preferred_element_type=F32)
        yr = jnp.where(first, 0.5 * zr * kr, zr * kr - zi * ki).astype(BF16)
        yi = jnp.where(first, 0.5 * zi * ki, zr * ki + zi * kr).astype(BF16)
        y = (jnp.dot(cr_ref[...], yr, preferred_element_type=F32)
             + jnp.dot(ct_ref[...], yi, preferred_element_type=F32))
        o_ref[rows, :] = ((y * (1.0 / SEQ) + gated[rows]) * x0[rows]).astype(BF16)


def _hy_conv_ctx(u, conv_w, conv_b, kr, ki, skip, mats, *, seqs_per_step):
    tc = 256
    nb = D // tc
    rows = seqs_per_step * SEQ

    def col(part):
        return [pl.BlockSpec((rows, tc), lambda b, c: (b, part * nb + c)),
                pl.BlockSpec((3, tc), lambda b, c: (0, part * nb + c)),
                pl.BlockSpec((1, tc), lambda b, c: (0, part * nb + c))]

    specs = [col(p) for p in range(3)]
    chan = pl.BlockSpec((SEQ, tc), lambda b, c: (0, c))
    mat = pl.BlockSpec((SEQ, SEQ), lambda b, c: (0, 0))
    cb = conv_b.reshape(1, 3 * D)
    return pl.pallas_call(
        _hy_conv_ctx_body,
        out_shape=jax.ShapeDtypeStruct((N_CTX, D), BF16),
        grid=(BATCH // seqs_per_step, nb),
        in_specs=([s[0] for s in specs] + [s[1] for s in specs] + [s[2] for s in specs]
                  + [chan, chan, pl.BlockSpec((1, tc), lambda b, c: (0, c)), mat, mat, mat]),
        out_specs=pl.BlockSpec((rows, tc), lambda b, c: (b, c)),
        compiler_params=_params(2),
        name="hyena_conv_ctx",
    )(u, u, u, conv_w, conv_w, conv_w, cb, cb, cb, kr, ki, skip.reshape(1, D), *mats)


FFT_A, FFT_R = 64, 64
FFT_M = 2 * FFT_A
FFT_H = FFT_R // 2
assert FFT_A * FFT_R == DEC_SEQ


SUB = 8
FFT_BLK = 2 * FFT_M * SUB


def _fft_consts():
    n = 2 * DEC_SEQ
    th = 2.0 * np.pi * (np.arange(FFT_M)[:, None] + 0.5) * np.arange(FFT_A)[None, :] / FFT_M
    f1 = np.concatenate([np.cos(th), -np.sin(th)], axis=0)
    k = np.arange(FFT_M)[:, None, None] + FFT_M * np.arange(FFT_H)[None, :, None] + 0.5
    ph = 2.0 * np.pi * k * np.arange(FFT_R)[None, None, :] / n
    c, s = np.cos(ph), np.sin(ph)
    g = np.concatenate([np.concatenate([c, s], axis=2), np.concatenate([-s, c], axis=2)], axis=1)
    as_bf16 = lambda m: jnp.asarray(m, dtype=F32).astype(BF16)
    return as_bf16(np.kron(f1, np.eye(SUB))), as_bf16(g)


def _fft_stage1(src_ref, y_ref, f1k):
    for b1 in range(FFT_R // SUB):
        x = jnp.concatenate([src_ref[a * FFT_R + b1 * SUB:a * FFT_R + (b1 + 1) * SUB, :] for a in range(FFT_A)],
                            axis=0)
        y_ref[b1 * FFT_BLK:(b1 + 1) * FFT_BLK, :] = jnp.dot(f1k, x.astype(BF16), preferred_element_type=F32)


def _fft_rows(k1):
    return [b1 * FFT_BLK + ri * FFT_M * SUB + k1 * SUB for ri in range(2) for b1 in range(FFT_R // SUB)]


def _fft_stage2(y_ref, g_ref, k1, part=None):
    y = jnp.concatenate([y_ref[r:r + SUB, :] for r in _fft_rows(k1)], axis=0).astype(BF16)
    if part is not None:
        return jnp.dot(g_ref[k1, part * FFT_H:(part + 1) * FFT_H, :], y, preferred_element_type=F32)
    z = jnp.dot(g_ref[k1], y, preferred_element_type=F32)
    return z[:FFT_H], z[FFT_H:]


def _hy_spectrum_body(hs_ref, hd_ref, f1_ref, g_ref, kr_ref, ki_ref, y_ref):
    f1k = f1_ref[...]
    _fft_stage1(hs_ref, y_ref, f1k)
    for k1 in range(FFT_M):
        kr_ref[k1 * FFT_H:(k1 + 1) * FFT_H, :] = _fft_stage2(y_ref, g_ref, k1, part=0)
    _fft_stage1(hd_ref, y_ref, f1k)
    for k1 in range(FFT_M):
        ki_ref[k1 * FFT_H:(k1 + 1) * FFT_H, :] = _fft_stage2(y_ref, g_ref, k1, part=1)


def _hy_spectrum(h_sum, h_diff, consts):
    tc = 128
    f1k, g = consts
    blk = pl.BlockSpec((DEC_SEQ, tc), lambda c: (0, c))
    shape = jax.ShapeDtypeStruct((DEC_SEQ, D), F32)
    return pl.pallas_call(
        _hy_spectrum_body,
        out_shape=(shape, shape),
        grid=(D // tc,),
        in_specs=[blk, blk, pl.BlockSpec(f1k.shape, lambda c: (0, 0)), pl.BlockSpec(g.shape, lambda c: (0, 0, 0))],
        out_specs=(blk, blk),
        scratch_shapes=[pltpu.VMEM((FFT_R * 2 * FFT_M, tc), F32)],
        compiler_params=_params(1),
        name="hyena_filter_fft",
    )(h_sum, h_diff, f1k, g)


def _hy_conv_fft_body(x0_ref, x1_ref, v_ref, w0_ref, w1_ref, wv_ref, b0_ref, b1_ref, bv_ref, kr_ref, ki_ref,
                      skip_ref, f1_ref, g_ref, o_ref, z_ref, y_ref, t_ref):
    rows = 512
    tn = (((0,), (0,)), ((), ()))

    def conv(u_ref, w_ref, b_ref, r):
        u = u_ref[r:r + rows, :]
        row = lax.broadcasted_iota(jnp.int32, u.shape, 0)
        before = u_ref[r - 1:r, :] if r > 0 else jnp.zeros_like(u[0:1])
        after = u_ref[r + rows:r + rows + 1, :] if r + rows < DEC_SEQ else jnp.zeros_like(u[0:1])
        prev = jnp.where(row == 0, before, pltpu.roll(u, 1, 0))
        nxt = jnp.where(row == rows - 1, after, pltpu.roll(u, rows - 1, 0))
        return prev * w_ref[0:1, :] + u * w_ref[1:2, :] + nxt * w_ref[2:3, :] + b_ref[...]

    for r in range(0, DEC_SEQ, rows):
        z_ref[r:r + rows, :] = conv(v_ref, wv_ref, bv_ref, r) * conv(x1_ref, w1_ref, b1_ref, r)
    f1k = f1_ref[...]
    _fft_stage1(z_ref, y_ref, f1k)
    for k1 in range(FFT_M):
        zr, zi = _fft_stage2(y_ref, g_ref, k1)
        kr = kr_ref[k1 * FFT_H:(k1 + 1) * FFT_H, :]
        ki = ki_ref[k1 * FFT_H:(k1 + 1) * FFT_H, :]
        p = jnp.concatenate([zr * kr - zi * ki, zr * ki + zi * kr], axis=0).astype(BF16)
        u = lax.dot_general(g_ref[k1], p, tn, preferred_element_type=F32)
        for j, r in enumerate(_fft_rows(k1)):
            y_ref[r:r + SUB, :] = u[j * SUB:(j + 1) * SUB]
    for b1 in range(FFT_R // SUB):
        blk = y_ref[b1 * FFT_BLK:(b1 + 1) * FFT_BLK, :].astype(BF16)
        yb = lax.dot_general(f1k, blk, tn, preferred_element_type=F32)
        for a in range(FFT_A):
            t_ref[a * FFT_R + b1 * SUB:a * FFT_R + (b1 + 1) * SUB, :] = yb[a * SUB:(a + 1) * SUB]
    for r in range(0, DEC_SEQ, rows):
        y = t_ref[r:r + rows, :] * (1.0 / DEC_SEQ) + z_ref[r:r + rows, :] * skip_ref[...]
        o_ref[r:r + rows, :] = (y * conv(x0_ref, w0_ref, b0_ref, r)).astype(BF16)


def _hy_conv_fft(u, conv_w, conv_b, kr, ki, skip, consts, *, row0, n_seq):
    tc = 128
    nb = D // tc
    r0 = row0 // DEC_SEQ

    def col(part):
        return [pl.BlockSpec((DEC_SEQ, tc), lambda b, c: (r0 + b, part * nb + c)),
                pl.BlockSpec((3, tc), lambda b, c: (0, part * nb + c)),
                pl.BlockSpec((1, tc), lambda b, c: (0, part * nb + c))]

    specs = [col(p) for p in range(3)]
    chan = pl.BlockSpec((DEC_SEQ, tc), lambda b, c: (0, c))
    const = lambda m: pl.BlockSpec(m.shape, lambda b, c: (0,) * m.ndim)
    cb = conv_b.reshape(1, 3 * D)
    return pl.pallas_call(
        _hy_conv_fft_body,
        out_shape=jax.ShapeDtypeStruct((n_seq * DEC_SEQ, D), BF16),
        grid=(n_seq, nb),
        in_specs=([s[0] for s in specs] + [s[1] for s in specs] + [s[2] for s in specs]
                  + [chan, chan, pl.BlockSpec((1, tc), lambda b, c: (0, c))] + [const(m) for m in consts]),
        out_specs=pl.BlockSpec((DEC_SEQ, tc), lambda b, c: (b, c)),
        scratch_shapes=[pltpu.VMEM((DEC_SEQ, tc), F32), pltpu.VMEM((FFT_R * 2 * FFT_M, tc), F32),
                        pltpu.VMEM((DEC_SEQ, tc), F32)],
        compiler_params=_params(2, 56 * 1024 * 1024),
        name="hyena_conv_fft",
    )(u, u, u, conv_w, conv_w, conv_w, cb, cb, cb, kr, ki, skip.reshape(1, D), *consts)


ROW_TILE = D // LANES
ROUTER_TM = 512
EXPERT_TM = 512
N_SLOTS = 2 * N_TOK + N_EXPERTS * EXPERT_TM
N_SLOT_TILES = N_SLOTS // EXPERT_TM
INFO_E1, INFO_E2, INFO_R1, INFO_R2, INFO_W1, INFO_W2 = range(6)


def _to_row_tiles(ref, x):
    rows = x.shape[0]
    for j in range(ROW_TILE):
        ref[pl.ds(j, rows, stride=ROW_TILE), :] = x[:, j * LANES:(j + 1) * LANES]


def _from_row_tiles(ref, rows):
    return jnp.concatenate([ref[pl.ds(j, rows, stride=ROW_TILE), :] for j in range(ROW_TILE)], axis=-1)


def _router_body(x_ref, sh_ref, sc_ref, w_ref, info_ref, incl_ref, cnt_ref):
    @pl.when(pl.program_id(0) == 0)
    def _():
        cnt_ref[...] = jnp.zeros_like(cnt_ref)

    h = _modulate(x_ref, sh_ref, sc_ref)
    h_hi, w = h.astype(BF16), w_ref[...]
    h_lo, w_hi = (h - h_hi.astype(F32)).astype(BF16), w.astype(BF16)
    w_lo = (w - w_hi.astype(F32)).astype(BF16)
    logits = (jnp.dot(h_hi, w_hi, preferred_element_type=F32) + jnp.dot(h_lo, w_hi, preferred_element_type=F32)
              + jnp.dot(h_hi, w_lo, preferred_element_type=F32))
    lane = lax.broadcasted_iota(jnp.int32, logits.shape, 1).astype(F32)
    logits = jnp.where(lane < N_EXPERTS, logits, -jnp.inf)
    e = jnp.exp(logits - jnp.max(logits, axis=-1, keepdims=True))
    p = e / jnp.sum(e, axis=-1, keepdims=True)
    p1 = jnp.max(p, axis=-1, keepdims=True)
    i1 = jnp.min(jnp.where(p == p1, lane, float(LANES)), axis=-1, keepdims=True)
    rest = jnp.where(lane == i1, -1.0, p)
    p2 = jnp.max(rest, axis=-1, keepdims=True)
    i2 = jnp.min(jnp.where(rest == p2, lane, float(LANES)), axis=-1, keepdims=True)
    total = p1 + p2
    chosen = jnp.where((lane == i1) | (lane == i2), 1.0, 0.0)
    tm = chosen.shape[0]
    earlier = (lax.broadcasted_iota(jnp.int32, (tm, tm), 1) < lax.broadcasted_iota(jnp.int32, (tm, tm), 0))
    rank = jnp.dot(earlier.astype(BF16), chosen.astype(BF16), preferred_element_type=F32) + cnt_ref[...]
    r1 = jnp.sum(jnp.where(lane == i1, rank, 0.0), axis=-1, keepdims=True)
    r2 = jnp.sum(jnp.where(lane == i2, rank, 0.0), axis=-1, keepdims=True)
    cnt_ref[...] += jnp.sum(chosen, axis=0, keepdims=True)
    incl_ref[0] = jnp.broadcast_to(cnt_ref[...], incl_ref.shape[1:])
    info = jnp.zeros_like(p)
    for col, val in ((INFO_E1, i1), (INFO_E2, i2), (INFO_R1, r1), (INFO_R2, r2),
                     (INFO_W1, p1 / total), (INFO_W2, p2 / total)):
        info = jnp.where(lane == col, val, info)
    info_ref[...] = info


def _router(x, sh, sc, w_router):
    tm = ROUTER_TM
    return pl.pallas_call(
        _router_body,
        out_shape=(jax.ShapeDtypeStruct((N_TOK, LANES), F32),
                   jax.ShapeDtypeStruct((N_TOK // tm, 8, LANES), F32)),
        grid=(N_TOK // tm,),
        in_specs=_mod_specs(tm, 0) + [pl.BlockSpec((D, LANES), lambda i: (0, 0))],
        out_specs=(pl.BlockSpec((tm, LANES), lambda i: (i, 0)), pl.BlockSpec((1, 8, LANES), lambda i: (i, 0, 0))),
        scratch_shapes=[pltpu.VMEM((1, LANES), F32)],
        compiler_params=_params(1),
        name="moe_router",
    )(x, sh, sc, jnp.pad(w_router, ((0, 0), (0, LANES - N_EXPERTS))))


def _row_tile(ref, row):
    return ref.at[pl.ds(pl.multiple_of(row * ROW_TILE, ROW_TILE), ROW_TILE)]


def _dispatch_body(pos1_ref, pos2_ref, x_ref, zeros_hbm, xs_hbm, rows_ref, sem, *, tm):
    del zeros_hbm
    base = pl.program_id(0) * tm
    _to_row_tiles(rows_ref, x_ref[...])

    def copies(r):
        src = _row_tile(rows_ref, r)
        return (pltpu.make_async_copy(src, _row_tile(xs_hbm, pos1_ref[base + r]), sem),
                pltpu.make_async_copy(src, _row_tile(xs_hbm, pos2_ref[base + r]), sem))

    def issue(r, carry):
        for queue, cp in enumerate(copies(r)):
            cp.start(priority=queue)
        return carry

    lax.fori_loop(0, tm, issue, 0, unroll=8)
    for _ in range(2):
        pltpu.make_async_copy(rows_ref, xs_hbm.at[pl.ds(0, tm * ROW_TILE)], sem).wait()


def _dispatch(pos1, pos2, x):
    tm = 512
    return pl.pallas_call(
        functools.partial(_dispatch_body, tm=tm),
        out_shape=jax.ShapeDtypeStruct((N_SLOTS * ROW_TILE, LANES), F32),
        grid_spec=pltpu.PrefetchScalarGridSpec(
            num_scalar_prefetch=2, grid=(N_TOK // tm,),
            in_specs=[pl.BlockSpec((tm, D), lambda i, *_: (i, 0)), pl.BlockSpec(memory_space=pl.ANY)],
            out_specs=pl.BlockSpec(memory_space=pl.ANY),
            scratch_shapes=[pltpu.VMEM((tm * ROW_TILE, LANES), F32), pltpu.SemaphoreType.DMA(())]),
        input_output_aliases={3: 0},
        compiler_params=_params(1),
        name="moe_dispatch",
    )(pos1, pos2, x, jnp.zeros((N_SLOTS * ROW_TILE, LANES), F32))


def _new_expert(eid_ref, nv_ref, t):
    tt = jnp.minimum(t, nv_ref[0] - 1)
    return (t == 0) | (eid_ref[tt] != eid_ref[jnp.maximum(tt - 1, 0)])


def _expert_swiglu_body(eid_ref, b1_ref, b2_ref, nv_ref, xs_ref, sh_ref, sc_ref, wg_ref, wu_ref, o_ref,
                        wgb_ref, wub_ref, *, fc):
    t = pl.program_id(1)

    @pl.when(_new_expert(eid_ref, nv_ref, t))
    def _():
        wgb_ref[...] = wg_ref[0].astype(BF16)
        wub_ref[...] = wu_ref[0].astype(BF16)

    @pl.when(t < nv_ref[0])
    def _():
        tm = o_ref.shape[0]
        x = _from_row_tiles(xs_ref, tm)
        slot = t * tm + lax.broadcasted_iota(jnp.int32, (tm, 1), 0)
        in1, in2 = slot >= b1_ref[t], slot >= b2_ref[t]
        sc = jnp.where(in2, sc_ref[2], jnp.where(in1, sc_ref[1], sc_ref[0]))
        sh = jnp.where(in2, sh_ref[2], jnp.where(in1, sh_ref[1], sh_ref[0]))
        h = (x * (1.0 + sc) + sh).astype(BF16)
        for c in range(o_ref.shape[1] // fc):
            cs = slice(c * fc, (c + 1) * fc)
            g = jnp.dot(h, wgb_ref[:, cs], preferred_element_type=F32)
            u = jnp.dot(h, wub_ref[:, cs], preferred_element_type=F32)
            o_ref[:, cs] = (g * jax.nn.sigmoid(g) * u).astype(BF16)

    @pl.when(t >= nv_ref[0])
    def _():
        o_ref[...] = jnp.zeros_like(o_ref)


def _expert_swiglu(meta, xs_rt, sh, sc, w_gate, w_up, e0):
    eid, b1, b2, nv = meta
    tm, f = EXPERT_TM, w_gate.shape[2]
    fh = f // 2

    def tile(t, eid, b1, b2, nv):
        return jnp.minimum(t, nv[0] - 1)

    w_spec = pl.BlockSpec((1, D, fh), lambda p, t, eid, b1, b2, nv: (e0 + eid[tile(t, eid, b1, b2, nv)], 0, p))
    mod = pl.BlockSpec((8, 1, D), lambda p, t, *_: (0, 0, 0))
    return pl.pallas_call(
        functools.partial(_expert_swiglu_body, fc=256),
        out_shape=jax.ShapeDtypeStruct((N_SLOTS, f), BF16),
        grid_spec=pltpu.PrefetchScalarGridSpec(
            num_scalar_prefetch=4, grid=(2, N_SLOT_TILES),
            in_specs=[pl.BlockSpec((tm * ROW_TILE, LANES), lambda p, t, *m: (tile(t, *m), 0)), mod, mod,
                      w_spec, w_spec],
            out_specs=pl.BlockSpec((tm, fh), lambda p, t, *m: (t, p)),
            scratch_shapes=[pltpu.VMEM((D, fh), BF16), pltpu.VMEM((D, fh), BF16)]),
        compiler_params=_params(2, 56 * 1024 * 1024),
        name="moe_swiglu",
    )(eid, b1, b2, nv, xs_rt, sh, sc, w_gate, w_up)


def _expert_down_body(eid_ref, nv_ref, a_ref, w_ref, y_ref, wb_ref):
    t = pl.program_id(0)

    @pl.when(_new_expert(eid_ref, nv_ref, t))
    def _():
        wb_ref[...] = w_ref[0].astype(BF16)

    @pl.when(t < nv_ref[0])
    def _():
        _to_row_tiles(y_ref, jnp.dot(a_ref[...], wb_ref[...], preferred_element_type=F32))

    @pl.when(t >= nv_ref[0])
    def _():
        y_ref[...] = jnp.zeros_like(y_ref)


def _expert_down(meta, act, w_down, e0):
    eid, _, _, nv = meta
    tm, f = EXPERT_TM, act.shape[1]

    def tile(t, eid, nv):
        return jnp.minimum(t, nv[0] - 1)

    return pl.pallas_call(
        _expert_down_body,
        out_shape=jax.ShapeDtypeStruct((N_SLOTS * ROW_TILE, LANES), F32),
        grid_spec=pltpu.PrefetchScalarGridSpec(
            num_scalar_prefetch=2, grid=(N_SLOT_TILES,),
            in_specs=[pl.BlockSpec((tm, f), lambda t, *m: (tile(t, *m), 0)),
                      pl.BlockSpec((1, f, D), lambda t, eid, nv: (e0 + eid[tile(t, eid, nv)], 0, 0))],
            out_specs=pl.BlockSpec((tm * ROW_TILE, LANES), lambda t, *m: (t, 0)),
            scratch_shapes=[pltpu.VMEM((f, D), BF16)]),
        compiler_params=_params(1, 58 * 1024 * 1024),
        name="moe_down",
    )(eid, nv, act, w_down)


def _combine_body(pos1_ref, pos2_ref, y_hbm, info_ref, x_ref, g_ref, lng_ref, lnb_ref, o_ref, y1_ref, y2_ref, sem,
                  *, tm):
    base = pl.program_id(0) * tm

    def copies(r):
        return (pltpu.make_async_copy(_row_tile(y_hbm, pos1_ref[base + r]), _row_tile(y1_ref, r), sem),
                pltpu.make_async_copy(_row_tile(y_hbm, pos2_ref[base + r]), _row_tile(y2_ref, r), sem))

    def issue(r, carry):
        for queue, cp in enumerate(copies(r)):
            cp.start(priority=queue)
        return carry

    lax.fori_loop(0, tm, issue, 0, unroll=8)
    for y_ref in (y1_ref, y2_ref):
        pltpu.make_async_copy(y_hbm.at[pl.ds(0, tm * ROW_TILE)], y_ref, sem).wait()
    info = info_ref[...]
    ffn = (info[:, INFO_W1:INFO_W1 + 1] * _from_row_tiles(y1_ref, tm)
           + info[:, INFO_W2:INFO_W2 + 1] * _from_row_tiles(y2_ref, tm))
    y = ALPHA * x_ref[...] + g_ref[0] * ffn
    mu = jnp.mean(y, axis=-1, keepdims=True)
    yc = y - mu
    var = jnp.mean(yc * yc, axis=-1, keepdims=True)
    o_ref[...] = yc * lax.rsqrt(var + LN_EPS) * lng_ref[...] + lnb_ref[...]


def _combine(pos1, pos2, y_rt, info, x, gate, ln_g, ln_b):
    tm = 512
    return pl.pallas_call(
        functools.partial(_combine_body, tm=tm),
        out_shape=jax.ShapeDtypeStruct((N_TOK, D), F32),
        grid_spec=pltpu.PrefetchScalarGridSpec(
            num_scalar_prefetch=2, grid=(N_TOK // tm,),
            in_specs=[pl.BlockSpec(memory_space=pl.ANY),
                      pl.BlockSpec((tm, LANES), lambda i, *_: (i, 0)),
                      pl.BlockSpec((tm, D), lambda i, *_: (i, 0)),
                      pl.BlockSpec((1, 1, D), lambda i, *_: (_group_of_row(i * tm), 0, 0)),
                      pl.BlockSpec((1, D), lambda i, *_: (0, 0)),
                      pl.BlockSpec((1, D), lambda i, *_: (0, 0))],
            out_specs=pl.BlockSpec((tm, D), lambda i, *_: (i, 0)),
            scratch_shapes=[pltpu.VMEM((tm * ROW_TILE, LANES), F32), pltpu.VMEM((tm * ROW_TILE, LANES), F32),
                            pltpu.SemaphoreType.DMA(())]),
        compiler_params=_params(1),
        name="moe_combine",
    )(pos1, pos2, y_rt, info, x, gate, ln_g.reshape(1, D), ln_b.reshape(1, D))


def _slot_plan(info, incl):
    row = lambda n_rows: incl[n_rows // ROUTER_TM - 1, 0, :N_EXPERTS].astype(jnp.int32)
    count = row(N_TOK)
    padded = (count + EXPERT_TM - 1) // EXPERT_TM * EXPERT_TM
    end = jnp.cumsum(padded)
    start = end - padded
    tile_row = jnp.arange(N_SLOT_TILES, dtype=jnp.int32) * EXPERT_TM
    eid = jnp.minimum(jnp.sum(tile_row[:, None] >= end[None, :], axis=1), N_EXPERTS - 1).astype(jnp.int32)
    b1 = (start + row(N_CTX))[eid]
    b2 = (start + row(N_CTX + DEC_SEQ))[eid]
    nv = (end[-1:] // EXPERT_TM).astype(jnp.int32)
    experts = jnp.arange(N_EXPERTS, dtype=jnp.int32)
    start_of = lambda col: jnp.sum(jnp.where(info[:, col:col + 1].astype(jnp.int32) == experts, start, 0), axis=1)
    pos1 = start_of(INFO_E1) + info[:, INFO_R1].astype(jnp.int32)
    pos2 = start_of(INFO_E2) + info[:, INFO_R2].astype(jnp.int32)
    return pos1, pos2, (eid, b1, b2, nv)


def _even_mixer(x, sh, sc, gate, ln_g, ln_b, w_in, b_igate, b_fgate, ml_norm_g, q_norm_g, k_norm_g, w_out,
                st_c, st_n, st_m, cache_k, cache_v, rope_tabs):
    splits = (4 * ML_W, 4 * ML_W + N_GATES)
    w_main = jnp.concatenate([w_in[:, :splits[0]], w_in[:, splits[1]:]], axis=1).astype(BF16)
    proj = _mod_matmul(x, sh, sc, [w_main[None]], tm=1024, tn=MAIN_W // 2, out_dtype=F32, name="even_in_proj")[0]
    b_gate = jnp.stack([b_igate, b_fgate], axis=1).reshape(N_GATES)
    lic, bc, lir, br = _gates(x, sh, sc, w_in[:, splits[0]:splits[1]], b_gate)

    hf_c, hb_c, new_c, new_n, new_m = _mlstm(proj, lic, bc, lir, br, row0=0, n_seq=BATCH, seq_len=SEQ)
    init = (st_c, st_n, jnp.broadcast_to(st_m[..., None], st_n.shape))
    hf_s, hb_s, _, _, _ = _mlstm(proj, lic, bc, lir, br, row0=N_CTX, n_seq=DEC_BATCH, seq_len=DEC_SEQ, init=init)
    ml = _ml_post((hf_c, hf_s), (hb_c, hb_s), proj, ml_norm_g)

    q_c, kn_c, kb_c = _qk_prep(proj, q_norm_g, k_norm_g, row0=0, rows=N_CTX)
    q_s, _, kb_s = _qk_prep(proj, q_norm_g, k_norm_g, row0=N_CTX, rows=N_LAT, rope_tabs=rope_tabs)
    v_all = proj[:, MAIN_W - KV_W:]
    v_c, v_s = v_all[:N_CTX], v_all[N_CTX:]
    att_c = _attention(q_c, _head_major(kb_c, BATCH).swapaxes(2, 3),
                       _with_ones(_head_major(v_c.astype(BF16), BATCH)), tq=SEQ)
    k_lat = jnp.concatenate([kb_s.reshape(DEC_BATCH, DEC_SEQ, KV_W),
                             cache_k.reshape(DEC_BATCH, PAST_LEN, KV_W).astype(BF16)], axis=1)
    v_lat = jnp.concatenate([v_s.reshape(DEC_BATCH, DEC_SEQ, KV_W).astype(BF16),
                             cache_v.reshape(DEC_BATCH, PAST_LEN, KV_W).astype(BF16)], axis=1)
    att_s = _attention(q_s, _head_major(k_lat.reshape(-1, KV_W), DEC_BATCH).swapaxes(2, 3),
                       _with_ones(_head_major(v_lat.reshape(-1, KV_W), DEC_BATCH)), tq=256)

    x = _proj_res_ln([ml, (att_c, att_s)], w_out.astype(BF16), x, gate, ln_g, ln_b, tm=512, name="even_out_proj")
    new_k = kn_c.reshape(BATCH, SEQ, ATT_KV_HEADS, ATT_HEAD_DIM)
    new_v = v_c.reshape(BATCH, SEQ, ATT_KV_HEADS, ATT_HEAD_DIM)
    return x, new_k, new_v, new_c, new_n, new_m[..., 0]


def _hyena_mixer(x, sh, sc, gate, ln_g, ln_b, w_in, conv_w, conv_b, w1, b1, w2, b2, w3, sin_freq, skip, w_out, dft):
    u = _mod_matmul(x, sh, sc, [w_in.astype(BF16)[None]], tm=1024, tn=1536, out_dtype=F32, name="hyena_in_proj")[0]
    mats, consts = dft
    h_sum, h_diff, nyq = _hy_filter(SEQ, w1, b1, w2, b2, w3, sin_freq)
    kr, ki = _ctx_spectrum(mats, h_sum, h_diff, nyq)
    y_c = _hy_conv_ctx(u, conv_w, conv_b, kr, ki, skip, mats, seqs_per_step=8)
    h_sum, h_diff, _ = _hy_filter(DEC_SEQ, w1, b1, w2, b2, w3, sin_freq)
    kr, ki = _hy_spectrum(h_sum, h_diff, consts)
    y_s = _hy_conv_fft(u, conv_w, conv_b, kr, ki, skip, consts, row0=N_CTX, n_seq=DEC_BATCH)
    return _proj_res_ln([(y_c, y_s)], w_out.astype(BF16), x, gate, ln_g, ln_b, tm=512, name="hyena_out_proj")


def _dense_ffn(x, sh, sc, gate, ln_g, ln_b, w_gate, w_up, w_down):
    act = _mod_matmul(x, sh, sc, [w_gate.astype(BF16)[None], w_up.astype(BF16)[None]],
                      tm=1024, tn=D_FF // 2, out_dtype=BF16, name="ffn_swiglu")
    return _proj_res_ln([act[0]], w_down.astype(BF16), x, gate, ln_g, ln_b, tm=512, name="ffn_down")


def _moe_ffn(x, sh, sc, gate, ln_g, ln_b, w_router, w_gate, w_up, w_down, e0):
    info, incl = _router(x, sh, sc, w_router)
    pos1, pos2, meta = _slot_plan(info, incl)
    xs_rt = _dispatch(pos1, pos2, x)
    act = _expert_swiglu(meta, xs_rt, sh, sc, w_gate, w_up, e0)
    y_rt = _expert_down(meta, act, w_down, e0)
    return _combine(pos1, pos2, y_rt, info, x, gate, ln_g, ln_b)


def kernel(x_prompt, x_sample, cache_attn_k, cache_attn_v, state_mlstm_C, state_mlstm_n, state_mlstm_m, c, c_ctx, w_ada, b_ada, ln_g, ln_b, w_in_even, b_igate, b_fgate, ml_norm_g, q_norm_g, k_norm_g, w_out_even, w_ffn_gate, w_ffn_up, w_ffn_down, w_in_hy, hy_conv_w, hy_conv_b, hy_filt_w1, hy_filt_b1, hy_filt_w2, hy_filt_b2, hy_filt_w3, hy_sin_freq, hy_skip, w_out_hy, w_router, w_moe_gate, w_moe_up, w_moe_down):
    x = jnp.concatenate([x_prompt.reshape(N_CTX, D), x_sample.reshape(N_LAT, D)])
    cvec = jnp.concatenate([c_ctx[None], c, jnp.zeros((8 - 1 - DEC_BATCH, D), F32)])
    mods = _ada(cvec, w_ada, b_ada)
    rope_tabs = _rope_tables()
    dft = (_dft_matrices(SEQ), _fft_consts())
    moe_w = [w.reshape((-1,) + w.shape[2:]) for w in (w_moe_gate, w_moe_up, w_moe_down)]
    new_k, new_v, new_c, new_n, new_m = [], [], [], [], []
    for layer in range(DEPTH):
        sh1, sc1, g1, sh2, sc2, g2 = (mods[layer, :, i * D:(i + 1) * D].reshape(8, 1, D) for i in range(6))
        i = layer // 2
        if layer % 2 == 0:
            x, k_c, v_c, st_c, st_n, st_m = _even_mixer(
                x, sh1, sc1, g1, ln_g[layer, 0], ln_b[layer, 0], w_in_even[i], b_igate[i], b_fgate[i], ml_norm_g[i],
                q_norm_g[i], k_norm_g[i], w_out_even[i], state_mlstm_C[:, i], state_mlstm_n[:, i], state_mlstm_m[:, i],
                cache_attn_k[:, i], cache_attn_v[:, i], rope_tabs)
            new_k.append(k_c)
            new_v.append(v_c)
            new_c.append(st_c)
            new_n.append(st_n)
            new_m.append(st_m)
            x = _dense_ffn(x, sh2, sc2, g2, ln_g[layer, 1], ln_b[layer, 1], w_ffn_gate[i], w_ffn_up[i], w_ffn_down[i])
        else:
            x = _hyena_mixer(x, sh1, sc1, g1, ln_g[layer, 0], ln_b[layer, 0], w_in_hy[i], hy_conv_w[i], hy_conv_b[i],
                             hy_filt_w1[i], hy_filt_b1[i], hy_filt_w2[i], hy_filt_b2[i], hy_filt_w3[i], hy_sin_freq[i],
                             hy_skip[i], w_out_hy[i], dft)
            x = _moe_ffn(x, sh2, sc2, g2, ln_g[layer, 1], ln_b[layer, 1], w_router[i], *moe_w, i * N_EXPERTS)
    return (x[:N_CTX].reshape(BATCH, SEQ, D), x[N_CTX:].reshape(DEC_BATCH, DEC_SEQ, D),
            jnp.stack(new_k, axis=1), jnp.stack(new_v, axis=1), jnp.stack(new_c, axis=1),
            jnp.stack(new_n, axis=1), jnp.stack(new_m, axis=1))
```

```python
import functools
import math

import jax
import jax.numpy as jnp
import numpy as np
from jax import lax
from jax.experimental import pallas as pl
from jax.experimental.pallas import tpu as pltpu

F32 = jnp.float32
BF16 = jnp.bfloat16
HIGHEST = lax.Precision.HIGHEST

D = 1024
BATCH, SEQ = 32, 256
DEC_BATCH, DEC_SEQ = 2, 4096
DEPTH = 4
PAST_LEN = 256
GRID_W = 64
N_CTX = BATCH * SEQ
N_LAT = DEC_BATCH * DEC_SEQ
N_TOK = N_CTX + N_LAT

ML_HEADS, ML_HEAD_DIM = 4, 128
ML_W = ML_HEADS * ML_HEAD_DIM
CHUNK = 128
MLSTM_PAR = 2
ATT_HEADS, ATT_KV_HEADS, ATT_HEAD_DIM = 8, 2, 64
ATT_GROUP = ATT_HEADS // ATT_KV_HEADS
ATT_W = ATT_HEADS * ATT_HEAD_DIM
KV_W = ATT_KV_HEADS * ATT_HEAD_DIM
GROUP_W = ATT_GROUP * ATT_HEAD_DIM
ROPE_BASE = 10000.0
N_GATES = 4 * ML_HEADS
MAIN_W = 4 * ML_W + ATT_W + 2 * KV_W

HY_EMB = 33
HY_BANDS = (HY_EMB - 1) // 2
HY_TARGET, HY_SHORT_PCT, HY_LONG_PCT = 1e-2, 0.3, 1.5
D_FF = 2816
N_EXPERTS = 8
MOE_D_FF = 3584
ALPHA = (2 * DEPTH) ** 0.25
LN_EPS = 1e-5
RMS_EPS = 1e-6

LANES = 128
VMEM_LIMIT = 48 * 1024 * 1024


def _params(n_axes, vmem=VMEM_LIMIT):
    return pltpu.CompilerParams(dimension_semantics=("arbitrary",) * n_axes, vmem_limit_bytes=vmem)


def _group_of_row(r):
    return jnp.where(r < N_CTX, 0, 1 + (r - N_CTX) // DEC_SEQ)


def _modulate(x_ref, sh_ref, sc_ref):
    return x_ref[...] * (1.0 + sc_ref[0]) + sh_ref[0]


def _mod_specs(tm, row_axis):
    def rows(*ids):
        return (ids[row_axis], 0)

    def grp(*ids):
        return (_group_of_row(ids[row_axis] * tm), 0, 0)

    return [pl.BlockSpec((tm, D), rows), pl.BlockSpec((1, 1, D), grp), pl.BlockSpec((1, 1, D), grp)]


def _ada_body(c_ref, w_ref, b_ref, o_ref):
    c = c_ref[...]
    s = c * jax.nn.sigmoid(c)
    o_ref[0] = jnp.dot(s, w_ref[0], preferred_element_type=F32, precision=HIGHEST) + b_ref[0]


def _ada(cvec, w_ada, b_ada):
    tn = 1536
    return pl.pallas_call(
        _ada_body,
        out_shape=jax.ShapeDtypeStruct((DEPTH, 8, 6 * D), F32),
        grid=(DEPTH, 6 * D // tn),
        in_specs=[pl.BlockSpec((8, D), lambda l, j: (0, 0)),
                  pl.BlockSpec((1, D, tn), lambda l, j: (l, 0, j)),
                  pl.BlockSpec((1, 1, tn), lambda l, j: (l, 0, j))],
        out_specs=pl.BlockSpec((1, 8, tn), lambda l, j: (l, 0, j)),
        compiler_params=_params(2),
        name="ada_modulation",
    )(cvec, w_ada, b_ada.reshape(DEPTH, 1, 6 * D))


def _mod_mm_body(x_ref, sh_ref, sc_ref, *refs, n_w):
    w_refs, o_ref, h_ref = refs[:n_w], refs[n_w], refs[n_w + 1]

    @pl.when(pl.program_id(2) == 0)
    def _():
        h_ref[...] = _modulate(x_ref, sh_ref, sc_ref).astype(BF16)

    h = h_ref[...]
    if n_w == 1:
        o = jnp.dot(h, w_refs[0][0], preferred_element_type=F32)
    else:
        g = jnp.dot(h, w_refs[0][0], preferred_element_type=F32)
        u = jnp.dot(h, w_refs[1][0], preferred_element_type=F32)
        o = g * jax.nn.sigmoid(g) * u
    o_ref[0] = o.astype(o_ref.dtype)


def _mod_matmul(x, sh, sc, ws, *, tm, tn, out_dtype, name):
    n_e, _, f = ws[0].shape
    return pl.pallas_call(
        functools.partial(_mod_mm_body, n_w=len(ws)),
        out_shape=jax.ShapeDtypeStruct((n_e, N_TOK, f), out_dtype),
        grid=(n_e, N_TOK // tm, f // tn),
        in_specs=_mod_specs(tm, 1) + [pl.BlockSpec((1, D, tn), lambda e, i, j: (e, 0, j)) for _ in ws],
        out_specs=pl.BlockSpec((1, tm, tn), lambda e, i, j: (e, i, j)),
        scratch_shapes=[pltpu.VMEM((tm, D), BF16)],
        compiler_params=_params(3),
        name=name,
    )(x, sh, sc, *ws)


def _split_specs(tm, width):
    nc = N_CTX // tm
    return [pl.BlockSpec((tm, width), lambda i: (jnp.minimum(i, nc - 1), 0)),
            pl.BlockSpec((tm, width), lambda i: (jnp.maximum(i - nc, 0), 0))]


def _pick_split(ctx_ref, lat_ref):
    tm = ctx_ref.shape[0]
    return jnp.where(pl.program_id(0) < N_CTX // tm, ctx_ref[...], lat_ref[...])


def _proj_res_ln_body(*refs, split):
    n_in = sum(2 if s else 1 for s in split)
    part_refs = list(refs[:n_in])
    w_ref, x_ref, g_ref, lng_ref, lnb_ref, o_ref = refs[n_in:]
    cols = [_pick_split(part_refs.pop(0), part_refs.pop(0)) if s else part_refs.pop(0)[...] for s in split]
    a = cols[0] if len(cols) == 1 else jnp.concatenate(cols, axis=-1)
    y = ALPHA * x_ref[...] + g_ref[0] * jnp.dot(a, w_ref[...], preferred_element_type=F32)
    mu = jnp.mean(y, axis=-1, keepdims=True)
    yc = y - mu
    var = jnp.mean(yc * yc, axis=-1, keepdims=True)
    o_ref[...] = yc * lax.rsqrt(var + LN_EPS) * lng_ref[...] + lnb_ref[...]


def _proj_res_ln(parts, w, x, gate, ln_g, ln_b, *, tm, name):
    split = tuple(isinstance(p, tuple) for p in parts)
    in_specs, args = [], []
    for p, s in zip(parts, split):
        if s:
            in_specs += _split_specs(tm, p[0].shape[1])
            args += list(p)
        else:
            in_specs.append(pl.BlockSpec((tm, p.shape[1]), lambda i: (i, 0)))
            args.append(p)
    in_specs += [pl.BlockSpec(w.shape, lambda i: (0, 0)),
                 pl.BlockSpec((tm, D), lambda i: (i, 0)),
                 pl.BlockSpec((1, 1, D), lambda i: (_group_of_row(i * tm), 0, 0)),
                 pl.BlockSpec((1, D), lambda i: (0, 0)),
                 pl.BlockSpec((1, D), lambda i: (0, 0))]
    return pl.pallas_call(
        functools.partial(_proj_res_ln_body, split=split),
        out_shape=jax.ShapeDtypeStruct((N_TOK, D), F32),
        grid=(N_TOK // tm,),
        in_specs=in_specs,
        out_specs=pl.BlockSpec((tm, D), lambda i: (i, 0)),
        compiler_params=_params(1),
        name=name,
    )(*args, w, x, gate, ln_g.reshape(1, D), ln_b.reshape(1, D))


def _log_sigmoid(x):
    return jnp.minimum(x, 0.0) - jnp.log(1.0 + jnp.exp(-jnp.abs(x)))


def _gates_body(x_ref, sh_ref, sc_ref, wg_ref, wgt_ref, b_ref, bt_ref,
                lic_ref, bc_ref, lir_ref, br_ref, *, tm):
    h = _modulate(x_ref, sh_ref, sc_ref).astype(BF16)
    g = jnp.dot(h, wg_ref[...], preferred_element_type=F32) + b_ref[...]
    gt = lax.dot_general(wgt_ref[...], h, (((1,), (1,)), ((), ())), preferred_element_type=F32) + bt_ref[...]
    lic_ref[...] = g
    lir_ref[...] = gt
    lf, lft = _log_sigmoid(g), _log_sigmoid(gt)
    r = lax.broadcasted_iota(jnp.int32, (CHUNK, CHUNK), 0)
    c = lax.broadcasted_iota(jnp.int32, (CHUNK, CHUNK), 1)
    tri_l = (c <= r).astype(BF16)
    tri_u = (c >= r).astype(BF16)

    def pieces(v):
        out = []
        for _ in range(3):
            out.append(v.astype(BF16))
            v = v - out[-1].astype(F32)
        return out

    def cumsum(tri, v, tri_first):
        dots = [jnp.dot(tri, p, preferred_element_type=F32) if tri_first else jnp.dot(p, tri, preferred_element_type=F32)
                for p in pieces(v)]
        return dots[0] + dots[1] + dots[2]

    fwd_col = lax.broadcasted_iota(jnp.int32, (CHUNK, LANES), 1) < 2 * ML_HEADS
    fwd_row = lax.broadcasted_iota(jnp.int32, (N_GATES, CHUNK), 0) < 2 * ML_HEADS
    for ch in range(tm // CHUNK):
        sl = slice(ch * CHUNK, (ch + 1) * CHUNK)
        lfc, lftc = lf[sl, :], lft[:, sl]
        bc_ref[sl, :] = jnp.where(fwd_col, cumsum(tri_l, lfc, True), cumsum(tri_u, lfc, True))
        br_ref[:, sl] = jnp.where(fwd_row, cumsum(tri_u, lftc, False), cumsum(tri_l, lftc, False))


def _gates(x, sh, sc, wg, b_gate):
    tm = 512
    wg_pad = jnp.pad(wg, ((0, 0), (0, LANES - N_GATES)))
    b_pad = jnp.pad(b_gate, (0, LANES - N_GATES)).reshape(1, LANES)
    col = pl.BlockSpec((tm, LANES), lambda i: (i, 0))
    row = pl.BlockSpec((N_GATES, tm), lambda i: (0, i))
    return pl.pallas_call(
        functools.partial(_gates_body, tm=tm),
        out_shape=(jax.ShapeDtypeStruct((N_TOK, LANES), F32), jax.ShapeDtypeStruct((N_TOK, LANES), F32),
                   jax.ShapeDtypeStruct((N_GATES, N_TOK), F32), jax.ShapeDtypeStruct((N_GATES, N_TOK), F32)),
        grid=(N_TOK // tm,),
        in_specs=_mod_specs(tm, 0) + [pl.BlockSpec((D, LANES), lambda i: (0, 0)),
                                         pl.BlockSpec((N_GATES, D), lambda i: (0, 0)),
                                         pl.BlockSpec((1, LANES), lambda i: (0, 0)),
                                         pl.BlockSpec((N_GATES, 1), lambda i: (0, 0))],
        out_specs=(col, col, row, row),
        compiler_params=_params(1),
        name="mlstm_gates",
    )(x, sh, sc, wg_pad.astype(BF16), wg.T.astype(BF16), b_pad, b_gate.reshape(N_GATES, 1))


def _mlstm_body(*refs, has_init):
    (qf, kf, vf, licf, bcf, lirf, brf, qb, kb, vb, licb, bcb, lirb, brb) = refs[:14]
    refs = refs[14:]
    if has_init:
        c0_ref, n0_ref, m0_ref = refs[:3]
        refs = refs[3:]
    hf_ref, hb_ref, c_ref, n_ref, m_ref = refs

    @pl.when(pl.program_id(1) == 0)
    def _():
        if has_init:
            c_ref[...] = c0_ref[...]
            n_ref[...] = n0_ref[...]
            m_ref[...] = m0_ref[...]
        else:
            c_ref[...] = jnp.zeros_like(c_ref)
            n_ref[...] = jnp.zeros_like(n_ref)
            m_ref[...] = jnp.zeros_like(m_ref)

    t_idx = lax.broadcasted_iota(jnp.int32, (CHUNK, CHUNK), 0)
    s_idx = lax.broadcasted_iota(jnp.int32, (CHUNK, CHUNK), 1)
    nt = (((1,), (1,)), ((), ()))
    stores = []

    def chain(u, d, h, q_ref, k_ref, v_ref, lic_ref, bc_ref, lir_ref, br_ref, h_ref):
        mask = (s_idx <= t_idx) if d == 0 else (s_idx >= t_idx)
        hs = slice(h * ML_HEAD_DIM, (h + 1) * ML_HEAD_DIM)
        gi, gf = d * 2 * ML_HEADS + h, d * 2 * ML_HEADS + ML_HEADS + h
        q = q_ref[u, :, hs]
        k = k_ref[u, :, hs] * (ML_HEAD_DIM ** -0.5)
        v = v_ref[u, :, hs]
        qh, kh, vh = q.astype(BF16), k.astype(BF16), v.astype(BF16)
        li_c, b_c = lic_ref[u, :, gi:gi + 1], bc_ref[u, :, gf:gf + 1]
        li_r, b_r = lir_ref[u, gi:gi + 1, :], br_ref[u, gf:gf + 1, :]
        c_st = c_ref[u, d, h]
        n_st = n_ref[u, d, h:h + 1, :]
        m_st = m_ref[u, d, h:h + 1, :][:, 0:1]
        dmat = jnp.where(mask, b_c - b_r + li_r, -jnp.inf)
        inter = b_c + m_st
        m_out = jnp.maximum(inter, jnp.max(dmat, axis=-1, keepdims=True))
        p = jnp.exp(dmat - m_out)
        w_inter = jnp.exp(inter - m_out)
        yield
        qk = lax.dot_general(qh, kh, nt, preferred_element_type=F32)
        qc = jnp.dot(qh, c_st.astype(BF16), preferred_element_type=F32)
        yield
        s = qk * p
        den = (jnp.sum(s, axis=-1, keepdims=True)
               + w_inter * jnp.sum(q * n_st, axis=-1, keepdims=True))
        sh = s.astype(BF16)
        b_last = b_r[:, CHUNK - 1:CHUNK] if d == 0 else b_r[:, 0:1]
        g_r = b_last - b_r + li_r
        g_c = b_last - b_c + li_c
        m_new = jnp.maximum(b_last + m_st, jnp.max(g_r, axis=-1, keepdims=True))
        decay = jnp.exp(b_last + m_st - m_new)
        kw = k * jnp.exp(g_c - m_new)
        kwh = kw.astype(BF16)
        yield
        sv = jnp.dot(sh, vh, preferred_element_type=F32)
        kv = lax.dot_general(kwh, vh, (((0,), (0,)), ((), ())), preferred_element_type=F32)
        yield
        h_out = (sv + w_inter * qc) / jnp.maximum(jnp.abs(den), jnp.exp(-m_out))
        c_new = decay * c_st + kv
        n_new = decay * n_st + jnp.sum(kw, axis=0, keepdims=True)
        stores.append((h_ref, u, d, h, hs, h_out, c_new, n_new, jnp.broadcast_to(m_new, (1, ML_HEAD_DIM))))
        yield

    chains = [chain(u, d, h, *group)
              for u in range(c_ref.shape[0])
              for d, group in enumerate(((qf, kf, vf, licf, bcf, lirf, brf, hf_ref),
                                         (qb, kb, vb, licb, bcb, lirb, brb, hb_ref)))
              for h in range(ML_HEADS)]
    for _ in range(5):
        for ch in chains:
            next(ch)
    for h_ref, u, d, h, hs, h_out, c_new, n_new, m_new in stores:
        h_ref[u, :, hs] = h_out
        c_ref[u, d, h] = c_new
        n_ref[u, d, h:h + 1, :] = n_new
        m_ref[u, d, h:h + 1, :] = m_new


def _mlstm(proj, lic, bc, lir, br, *, row0, n_seq, seq_len, init=None):
    nc = seq_len // CHUNK
    par = MLSTM_PAR
    g0 = row0 // seq_len // par
    seqs = lambda a: a.reshape(N_TOK // seq_len, seq_len, a.shape[-1])
    rows_of = lambda a: seqs(a.T).transpose(0, 2, 1)

    def chunk_specs(chunk):
        return ([pl.BlockSpec((par, CHUNK, ML_W), lambda b, j, c=c: (g0 + b, chunk(j), c)) for c in range(3)]
                + [pl.BlockSpec((par, CHUNK, LANES), lambda b, j: (g0 + b, chunk(j), 0))] * 2
                + [pl.BlockSpec((par, N_GATES, CHUNK), lambda b, j: (g0 + b, 0, chunk(j)))] * 2)

    fwd, bwd = (lambda j: j), (lambda j: nc - 1 - j)
    st_c = pl.BlockSpec((par, 2, ML_HEADS, ML_HEAD_DIM, ML_HEAD_DIM), lambda b, j: (b, 0, 0, 0, 0))
    st_n = pl.BlockSpec((par, 2, ML_HEADS, ML_HEAD_DIM), lambda b, j: (b, 0, 0, 0))
    in_specs = chunk_specs(fwd) + chunk_specs(bwd)
    args = [seqs(proj)] * 3 + [seqs(lic), seqs(bc), rows_of(lir), rows_of(br)]
    args = args * 2
    if init is not None:
        in_specs += [st_c, st_n, st_n]
        args += list(init)
    h_shape = jax.ShapeDtypeStruct((n_seq, seq_len, ML_W), F32)
    hf, hb, c_st, n_st, m_st = pl.pallas_call(
        functools.partial(_mlstm_body, has_init=init is not None),
        out_shape=(h_shape, h_shape,
                   jax.ShapeDtypeStruct((n_seq, 2, ML_HEADS, ML_HEAD_DIM, ML_HEAD_DIM), F32),
                   jax.ShapeDtypeStruct((n_seq, 2, ML_HEADS, ML_HEAD_DIM), F32),
                   jax.ShapeDtypeStruct((n_seq, 2, ML_HEADS, ML_HEAD_DIM), F32)),
        grid=(n_seq // par, nc),
        in_specs=in_specs,
        out_specs=(pl.BlockSpec((par, CHUNK, ML_W), lambda b, j: (b, fwd(j), 0)),
                   pl.BlockSpec((par, CHUNK, ML_W), lambda b, j: (b, bwd(j), 0)),
                   st_c, st_n, st_n),
        compiler_params=_params(2),
        name="mlstm_scan",
    )(*args)
    return hf.reshape(-1, ML_W), hb.reshape(-1, ML_W), c_st, n_st, m_st


def _ml_post_body(hfc_ref, hfs_ref, hbc_ref, hbs_ref, o_ref, g_ref, out_ref):
    h = _pick_split(hfc_ref, hfs_ref) + _pick_split(hbc_ref, hbs_ref)
    gate = jax.nn.sigmoid(o_ref[...]) * g_ref[...]
    for hd in range(ML_HEADS):
        hs = slice(hd * ML_HEAD_DIM, (hd + 1) * ML_HEAD_DIM)
        x = h[:, hs]
        xc = x - jnp.mean(x, axis=-1, keepdims=True)
        var = jnp.mean(xc * xc, axis=-1, keepdims=True)
        out_ref[:, hs] = (gate[:, hs] * (xc * lax.rsqrt(var + RMS_EPS))).astype(BF16)


def _ml_post(hf, hb, proj, norm_g):
    tm = 512
    blk = pl.BlockSpec((tm, ML_W), lambda i: (i, 0))
    return pl.pallas_call(
        _ml_post_body,
        out_shape=jax.ShapeDtypeStruct((N_TOK, ML_W), BF16),
        grid=(N_TOK // tm,),
        in_specs=(_split_specs(tm, ML_W) * 2
                  + [pl.BlockSpec((tm, ML_W), lambda i: (i, 3)), pl.BlockSpec((1, ML_W), lambda i: (0, 0))]),
        out_specs=blk,
        compiler_params=_params(1),
        name="mlstm_out_norm",
    )(*hf, *hb, proj, norm_g.reshape(1, ML_W))


def _head_rms(x, gain):
    lane_head = lax.broadcasted_iota(jnp.int32, x.shape, 1) // ATT_HEAD_DIM
    sq = x * x
    ms = jnp.zeros_like(x)
    for hd in range(x.shape[1] // ATT_HEAD_DIM):
        sel = lane_head == hd
        ms = jnp.where(sel, jnp.sum(jnp.where(sel, sq, 0.0), axis=-1, keepdims=True), ms)
    return x * lax.rsqrt(ms * (1.0 / ATT_HEAD_DIM) + RMS_EPS) * gain


def _rope(x, cos, sin_signed):
    w = x.shape[1]
    even = lax.broadcasted_iota(jnp.int32, x.shape, 1) % 2 == 0
    partner = jnp.where(even, pltpu.roll(x, w - 1, 1), pltpu.roll(x, 1, 1))
    return x * cos + partner * sin_signed


def _qk_prep_body(q_ref, k_ref, qg_ref, kg_ref, *refs, rope):
    if rope:
        cq_ref, sq_ref, ck_ref, sk_ref, qo_ref, kn_ref, kr_ref = refs
    else:
        qo_ref, kn_ref, kr_ref = refs
    q = _head_rms(q_ref[...], qg_ref[...])
    k = _head_rms(k_ref[...], kg_ref[...])
    kn_ref[...] = k
    if rope:
        q = _rope(q, cq_ref[...], sq_ref[...])
        k = _rope(k, ck_ref[...], sk_ref[...])
    qo_ref[...] = (q * (ATT_HEAD_DIM ** -0.5)).astype(BF16)
    kr_ref[...] = k.astype(BF16)


def _qk_prep(proj, q_gain, k_gain, *, row0, rows, rope_tabs=None):
    tm = 512
    r0 = row0 // tm
    in_specs = [pl.BlockSpec((tm, ATT_W), lambda i: (r0 + i, 4 * ML_W // ATT_W)),
                pl.BlockSpec((tm, KV_W), lambda i: (r0 + i, (4 * ML_W + ATT_W) // KV_W)),
                pl.BlockSpec((1, ATT_W), lambda i: (0, 0)),
                pl.BlockSpec((1, KV_W), lambda i: (0, 0))]
    args = [proj, proj, jnp.tile(q_gain, ATT_HEADS).reshape(1, ATT_W), jnp.tile(k_gain, ATT_KV_HEADS).reshape(1, KV_W)]
    if rope_tabs is not None:
        per_seq = DEC_SEQ // tm
        in_specs += [pl.BlockSpec((tm, ATT_W), lambda i: (i % per_seq, 0))] * 2
        in_specs += [pl.BlockSpec((tm, KV_W), lambda i: (i % per_seq, 0))] * 2
        args += list(rope_tabs)
    return pl.pallas_call(
        functools.partial(_qk_prep_body, rope=rope_tabs is not None),
        out_shape=(jax.ShapeDtypeStruct((rows, ATT_W), BF16), jax.ShapeDtypeStruct((rows, KV_W), F32),
                   jax.ShapeDtypeStruct((rows, KV_W), BF16)),
        grid=(rows // tm,),
        in_specs=in_specs,
        out_specs=(pl.BlockSpec((tm, ATT_W), lambda i: (i, 0)), pl.BlockSpec((tm, KV_W), lambda i: (i, 0)),
                   pl.BlockSpec((tm, KV_W), lambda i: (i, 0))),
        compiler_params=_params(1),
        name="attn_qk_prep",
    )(*args)


def _rope_tables():
    rows = DEC_SEQ // GRID_W
    axis_dim = ATT_HEAD_DIM // 2
    row = jnp.repeat(jnp.arange(rows, dtype=F32), GRID_W)
    col = (jnp.arange(DEC_SEQ) % GRID_W).astype(F32)
    inv = ROPE_BASE ** (-jnp.arange(axis_dim // 2, dtype=F32) * 2.0 / axis_dim)
    ang = jnp.concatenate([row[:, None] * inv, col[:, None] * inv], axis=-1)
    cos = jnp.repeat(jnp.cos(ang), 2, axis=-1)
    sin = jnp.repeat(jnp.sin(ang), 2, axis=-1) * jnp.tile(jnp.array([-1.0, 1.0], F32), axis_dim)
    return (jnp.tile(cos, (1, ATT_HEADS)), jnp.tile(sin, (1, ATT_HEADS)),
            jnp.tile(cos, (1, ATT_KV_HEADS)), jnp.tile(sin, (1, ATT_KV_HEADS)))


def _attn_body(q_ref, k_ref, v_ref, o_ref):
    k, v = k_ref[0, 0], v_ref[0, 0]
    dh = k.shape[0]
    g = q_ref.shape[1] // dh
    pair = 2

    def heads(h0):
        q = jnp.concatenate([q_ref[:, h * dh:(h + 1) * dh] for h in range(h0, h0 + pair)], axis=0)
        s = jnp.dot(q, k, preferred_element_type=F32)
        yield
        e = jnp.exp((s - jnp.max(s, axis=-1, keepdims=True)).astype(BF16))
        yield
        o = jnp.dot(e, v, preferred_element_type=F32)
        o = (o[:, :dh] / o[:, dh:dh + 1]).astype(BF16)
        tq = q_ref.shape[0]
        for j in range(pair):
            o_ref[:, (h0 + j) * dh:(h0 + j + 1) * dh] = o[j * tq:(j + 1) * tq]
        yield

    chains = [heads(h0) for h0 in range(0, g, pair)]
    for step in range(len(chains) + 2):
        for i, chain in enumerate(chains):
            if 0 <= step - i < 3:
                next(chain)


def _attention(q, k_t, v_ones, *, tq):
    n_seq, _, dh, s_len = k_t.shape
    nq = q.shape[0] // n_seq // tq
    qo = pl.BlockSpec((tq, GROUP_W), lambda b, kh, i: (b * nq + i, kh))
    return pl.pallas_call(
        _attn_body,
        out_shape=jax.ShapeDtypeStruct(q.shape, BF16),
        grid=(n_seq, ATT_KV_HEADS, nq),
        in_specs=[qo, pl.BlockSpec((1, 1, dh, s_len), lambda b, kh, i: (b, kh, 0, 0)),
                  pl.BlockSpec((1, 1, s_len, LANES), lambda b, kh, i: (b, kh, 0, 0))],
        out_specs=qo,
        compiler_params=_params(3),
        name="attention",
    )(q, k_t, v_ones)


def _head_major(x, n_seq):
    return x.reshape(n_seq, -1, x.shape[1] // ATT_HEAD_DIM, ATT_HEAD_DIM).transpose(0, 2, 1, 3)


def _with_ones(v):
    pad = jnp.zeros(v.shape[:-1] + (LANES - ATT_HEAD_DIM - 1,), v.dtype)
    return jnp.concatenate([v, jnp.ones(v.shape[:-1] + (1,), v.dtype), pad], axis=-1)


def _hy_filter_body(feat_ref, t_ref, w1_ref, b1_ref, w2_ref, b2_ref, fr_ref, w3f_ref, w3b_ref, dl_ref,
                    hsum_ref, hdiff_ref, nyq_ref, z_ref):
    @pl.when(pl.program_id(0) == 0)
    def _():
        z = jnp.dot(feat_ref[...], w1_ref[...], preferred_element_type=F32, precision=HIGHEST) + b1_ref[...]
        z = jnp.sin(fr_ref[0:1, :] * z)
        z = jnp.dot(z, w2_ref[...], preferred_element_type=F32, precision=HIGHEST) + b2_ref[...]
        z_ref[...] = jnp.sin(fr_ref[1:2, :] * z)

    z = z_ref[...]
    window = jnp.exp(-t_ref[...] * dl_ref[...])
    h_f = jnp.dot(z, w3f_ref[...], preferred_element_type=F32, precision=HIGHEST) * window
    h_b = jnp.dot(z, w3b_ref[...], preferred_element_type=F32, precision=HIGHEST) * window
    row = lax.broadcasted_iota(jnp.int32, h_f.shape, 0)
    h_b = jnp.where(row == 0, 0.0, h_b)
    inv = 1.0 / (jnp.sum(jnp.abs(h_f), axis=0, keepdims=True) + jnp.sum(jnp.abs(h_b), axis=0, keepdims=True))
    h_sum = (h_f + h_b) * inv
    hsum_ref[...] = h_sum
    hdiff_ref[...] = (h_f - h_b) * inv
    nyq_ref[...] = jnp.sum(jnp.where(row % 2 == 0, h_sum, -h_sum), axis=0, keepdims=True)


def _hy_filter(seq_len, w1, b1, w2, b2, w3, sin_freq):
    tc = 256
    fw = w1.shape[1]
    t = jnp.arange(seq_len, dtype=F32)[:, None] / seq_len
    bands = jnp.arange(1, HY_BANDS + 1, dtype=F32)[None, :]
    feat = jnp.concatenate([t, jnp.sin(2.0 * math.pi * bands * t), jnp.cos(2.0 * math.pi * bands * t)], axis=-1)
    feat = jnp.pad(feat, ((0, 0), (0, LANES - HY_EMB)))
    deltas = jnp.abs(jnp.linspace(math.log(HY_TARGET) / HY_LONG_PCT, math.log(HY_TARGET) / HY_SHORT_PCT, D,
                                  dtype=F32)).reshape(1, D)
    pad_w = LANES - fw
    full = lambda shape: pl.BlockSpec(shape, lambda j: (0,) * len(shape))
    return pl.pallas_call(
        _hy_filter_body,
        out_shape=(jax.ShapeDtypeStruct((seq_len, D), F32), jax.ShapeDtypeStruct((seq_len, D), F32),
                   jax.ShapeDtypeStruct((1, D), F32)),
        grid=(D // tc,),
        in_specs=[full((seq_len, LANES)), full((seq_len, 1)), full((LANES, LANES)), full((1, LANES)),
                  full((LANES, LANES)), full((1, LANES)), full((2, LANES)),
                  pl.BlockSpec((LANES, tc), lambda j: (0, j)), pl.BlockSpec((LANES, tc), lambda j: (0, D // tc + j)),
                  pl.BlockSpec((1, tc), lambda j: (0, j))],
        out_specs=(pl.BlockSpec((seq_len, tc), lambda j: (0, j)), pl.BlockSpec((seq_len, tc), lambda j: (0, j)),
                   pl.BlockSpec((1, tc), lambda j: (0, j))),
        scratch_shapes=[pltpu.VMEM((seq_len, LANES), F32)],
        compiler_params=_params(1),
        name="hyena_filter",
    )(feat, t, jnp.pad(w1, ((0, LANES - HY_EMB), (0, pad_w))), jnp.pad(b1, (0, pad_w)).reshape(1, LANES),
      jnp.pad(w2, ((0, pad_w), (0, pad_w))), jnp.pad(b2, (0, pad_w)).reshape(1, LANES),
      jnp.pad(sin_freq, ((0, 0), (0, pad_w))), jnp.pad(w3, ((0, pad_w), (0, 0))), jnp.pad(w3, ((0, pad_w), (0, 0))),
      deltas)


def _dft_matrices(seq_len):
    n = 2 * seq_len
    k = lax.broadcasted_iota(jnp.int32, (seq_len, seq_len), 0)
    t = lax.broadcasted_iota(jnp.int32, (seq_len, seq_len), 1)
    ang = ((k * t) % n).astype(F32) * (2.0 * math.pi / n)
    cr, base = jnp.cos(ang), -jnp.sin(ang)
    ci = jnp.where(k == 0, (1 - 2 * (t % 2)).astype(F32), base)
    cit = jnp.where(t == 0, (1 - 2 * (k % 2)).astype(F32), base)
    return cr.astype(BF16), ci.astype(BF16), cit.astype(BF16)


def _ctx_spectrum_body(cr_ref, ci_ref, hs_ref, hd_ref, nyq_ref, kr_ref, ki_ref):
    kr_ref[...] = jnp.dot(cr_ref[...], hs_ref[...].astype(BF16), preferred_element_type=F32)
    ki = jnp.dot(ci_ref[...], hd_ref[...].astype(BF16), preferred_element_type=F32)
    first = lax.broadcasted_iota(jnp.int32, ki.shape, 0) == 0
    ki_ref[...] = jnp.where(first, nyq_ref[...], ki)


def _ctx_spectrum(mats, h_sum, h_diff, nyq):
    tn = 512
    cr, ci, _ = mats
    mat = pl.BlockSpec((SEQ, SEQ), lambda c: (0, 0))
    chan = pl.BlockSpec((SEQ, tn), lambda c: (0, c))
    shape = jax.ShapeDtypeStruct((SEQ, D), F32)
    return pl.pallas_call(
        _ctx_spectrum_body,
        out_shape=(shape, shape),
        grid=(D // tn,),
        in_specs=[mat, mat, chan, chan, pl.BlockSpec((1, tn), lambda c: (0, c))],
        out_specs=(chan, chan),
        compiler_params=_params(1),
        name="hyena_filter_dft",
    )(cr, ci, h_sum, h_diff, nyq)


def _hy_conv_ctx_body(x0_ref, x1_ref, v_ref, w0_ref, w1_ref, wv_ref, b0_ref, b1_ref, bv_ref, kr_ref, ki_ref,
                      skip_ref, cr_ref, ci_ref, ct_ref, o_ref):
    n = x0_ref.shape[0]
    pos = lax.broadcasted_iota(jnp.int32, x0_ref.shape, 0) % SEQ

    def conv(u_ref, w_ref, b_ref):
        u = u_ref[...]
        prev = jnp.where(pos == 0, 0.0, pltpu.roll(u, 1, 0))
        nxt = jnp.where(pos == SEQ - 1, 0.0, pltpu.roll(u, n - 1, 0))
        return prev * w_ref[0:1, :] + u * w_ref[1:2, :] + nxt * w_ref[2:3, :] + b_ref[...]

    z = conv(v_ref, wv_ref, bv_ref) * conv(x1_ref, w1_ref, b1_ref)
    gated = z * skip_ref[...]
    x0 = conv(x0_ref, w0_ref, b0_ref)
    kr, ki = kr_ref[...], ki_ref[...]
    first = lax.broadcasted_iota(jnp.int32, kr.shape, 0) == 0
    for s in range(n // SEQ):
        rows = slice(s * SEQ, (s + 1) * SEQ)
        zs = z[rows].astype(BF16)
        zr = jnp.dot(cr_ref[...], zs, preferred_element_type=F32)
        zi = jnp.dot(ci_ref[...], zs, preferred_element_type=F32)
        yr = jnp.where(first, 0.5 * zr * kr, zr * kr - zi * ki).astype(BF16)
        yi = jnp.where(first, 0.5 * zi * ki, zr * ki + zi * kr).astype(BF16)
        y = (jnp.dot(cr_ref[...], yr, preferred_element_type=F32)
             + jnp.dot(ct_ref[...], yi, preferred_element_type=F32))
        o_ref[rows, :] = ((y * (1.0 / SEQ) + gated[rows]) * x0[rows]).astype(BF16)


def _hy_conv_ctx(u, conv_w, conv_b, kr, ki, skip, mats, *, seqs_per_step):
    tc = 256
    nb = D // tc
    rows = seqs_per_step * SEQ

    def col(part):
        return [pl.BlockSpec((rows, tc), lambda b, c: (b, part * nb + c)),
                pl.BlockSpec((3, tc), lambda b, c: (0, part * nb + c)),
                pl.BlockSpec((1, tc), lambda b, c: (0, part * nb + c))]

    specs = [col(p) for p in range(3)]
    chan = pl.BlockSpec((SEQ, tc), lambda b, c: (0, c))
    mat = pl.BlockSpec((SEQ, SEQ), lambda b, c: (0, 0))
    cb = conv_b.reshape(1, 3 * D)
    return pl.pallas_call(
        _hy_conv_ctx_body,
        out_shape=jax.ShapeDtypeStruct((N_CTX, D), BF16),
        grid=(BATCH // seqs_per_step, nb),
        in_specs=([s[0] for s in specs] + [s[1] for s in specs] + [s[2] for s in specs]
                  + [chan, chan, pl.BlockSpec((1, tc), lambda b, c: (0, c)), mat, mat, mat]),
        out_specs=pl.BlockSpec((rows, tc), lambda b, c: (b, c)),
        compiler_params=_params(2),
        name="hyena_conv_ctx",
    )(u, u, u, conv_w, conv_w, conv_w, cb, cb, cb, kr, ki, skip.reshape(1, D), *mats)


FFT_A, FFT_R = 64, 64
FFT_M = 2 * FFT_A
FFT_H = FFT_R // 2
assert FFT_A * FFT_R == DEC_SEQ


SUB = 8
FFT_BLK = 2 * FFT_M * SUB


def _fft_consts():
    n = 2 * DEC_SEQ
    th = 2.0 * np.pi * (np.arange(FFT_M)[:, None] + 0.5) * np.arange(FFT_A)[None, :] / FFT_M
    f1 = np.concatenate([np.cos(th), -np.sin(th)], axis=0)
    k = np.arange(FFT_M)[:, None, None] + FFT_M * np.arange(FFT_H)[None, :, None] + 0.5
    ph = 2.0 * np.pi * k * np.arange(FFT_R)[None, None, :] / n
    c, s = np.cos(ph), np.sin(ph)
    g = np.concatenate([np.concatenate([c, s], axis=2), np.concatenate([-s, c], axis=2)], axis=1)
    as_bf16 = lambda m: jnp.asarray(m, dtype=F32).astype(BF16)
    return as_bf16(np.kron(f1, np.eye(SUB))), as_bf16(g)


def _fft_stage1(src_ref, y_ref, f1k):
    for b1 in range(FFT_R // SUB):
        x = jnp.concatenate([src_ref[a * FFT_R + b1 * SUB:a * FFT_R + (b1 + 1) * SUB, :] for a in range(FFT_A)],
                            axis=0)
        y_ref[b1 * FFT_BLK:(b1 + 1) * FFT_BLK, :] = jnp.dot(f1k, x.astype(BF16), preferred_element_type=F32)


def _fft_rows(k1):
    return [b1 * FFT_BLK + ri * FFT_M * SUB + k1 * SUB for ri in range(2) for b1 in range(FFT_R // SUB)]


def _fft_stage2(y_ref, g_ref, k1, part=None):
    y = jnp.concatenate([y_ref[r:r + SUB, :] for r in _fft_rows(k1)], axis=0).astype(BF16)
    if part is not None:
        return jnp.dot(g_ref[k1, part * FFT_H:(part + 1) * FFT_H, :], y, preferred_element_type=F32)
    z = jnp.dot(g_ref[k1], y, preferred_element_type=F32)
    return z[:FFT_H], z[FFT_H:]


def _hy_spectrum_body(hs_ref, hd_ref, f1_ref, g_ref, kr_ref, ki_ref, y_ref):
    f1k = f1_ref[...]
    _fft_stage1(hs_ref, y_ref, f1k)
    for k1 in range(FFT_M):
        kr_ref[k1 * FFT_H:(k1 + 1) * FFT_H, :] = _fft_stage2(y_ref, g_ref, k1, part=0)
    _fft_stage1(hd_ref, y_ref, f1k)
    for k1 in range(FFT_M):
        ki_ref[k1 * FFT_H:(k1 + 1) * FFT_H, :] = _fft_stage2(y_ref, g_ref, k1, part=1)


def _hy_spectrum(h_sum, h_diff, consts):
    tc = 128
    f1k, g = consts
    blk = pl.BlockSpec((DEC_SEQ, tc), lambda c: (0, c))
    shape = jax.ShapeDtypeStruct((DEC_SEQ, D), F32)
    return pl.pallas_call(
        _hy_spectrum_body,
        out_shape=(shape, shape),
        grid=(D // tc,),
        in_specs=[blk, blk, pl.BlockSpec(f1k.shape, lambda c: (0, 0)), pl.BlockSpec(g.shape, lambda c: (0, 0, 0))],
        out_specs=(blk, blk),
        scratch_shapes=[pltpu.VMEM((FFT_R * 2 * FFT_M, tc), F32)],
        compiler_params=_params(1),
        name="hyena_filter_fft",
    )(h_sum, h_diff, f1k, g)


def _hy_conv_fft_body(x0_ref, x1_ref, v_ref, w0_ref, w1_ref, wv_ref, b0_ref, b1_ref, bv_ref, kr_ref, ki_ref,
                      skip_ref, f1_ref, g_ref, o_ref, z_ref, y_ref, t_ref):
    rows = 512
    tn = (((0,), (0,)), ((), ()))

    def conv(u_ref, w_ref, b_ref, r):
        u = u_ref[r:r + rows, :]
        row = lax.broadcasted_iota(jnp.int32, u.shape, 0)
        before = u_ref[r - 1:r, :] if r > 0 else jnp.zeros_like(u[0:1])
        after = u_ref[r + rows:r + rows + 1, :] if r + rows < DEC_SEQ else jnp.zeros_like(u[0:1])
        prev = jnp.where(row == 0, before, pltpu.roll(u, 1, 0))
        nxt = jnp.where(row == rows - 1, after, pltpu.roll(u, rows - 1, 0))
        return prev * w_ref[0:1, :] + u * w_ref[1:2, :] + nxt * w_ref[2:3, :] + b_ref[...]

    for r in range(0, DEC_SEQ, rows):
        z_ref[r:r + rows, :] = conv(v_ref, wv_ref, bv_ref, r) * conv(x1_ref, w1_ref, b1_ref, r)
    f1k = f1_ref[...]
    _fft_stage1(z_ref, y_ref, f1k)
    for k1 in range(FFT_M):
        zr, zi = _fft_stage2(y_ref, g_ref, k1)
        kr = kr_ref[k1 * FFT_H:(k1 + 1) * FFT_H, :]
        ki = ki_ref[k1 * FFT_H:(k1 + 1) * FFT_H, :]
        p = jnp.concatenate([zr * kr - zi * ki, zr * ki + zi * kr], axis=0).astype(BF16)
        u = lax.dot_general(g_ref[k1], p, tn, preferred_element_type=F32)
        for j, r in enumerate(_fft_rows(k1)):
            y_ref[r:r + SUB, :] = u[j * SUB:(j + 1) * SUB]
    for b1 in range(FFT_R // SUB):
        blk = y_ref[b1 * FFT_BLK:(b1 + 1) * FFT_BLK, :].astype(BF16)
        yb = lax.dot_general(f1k, blk, tn, preferred_element_type=F32)
        for a in range(FFT_A):
            t_ref[a * FFT_R + b1 * SUB:a * FFT_R + (b1 + 1) * SUB, :] = yb[a * SUB:(a + 1) * SUB]
    for r in range(0, DEC_SEQ, rows):
        y = t_ref[r:r + rows, :] * (1.0 / DEC_SEQ) + z_ref[r:r + rows, :] * skip_ref[...]
        o_ref[r:r + rows, :] = (y * conv(x0_ref, w0_ref, b0_ref, r)).astype(BF16)


def _hy_conv_fft(u, conv_w, conv_b, kr, ki, skip, consts, *, row0, n_seq):
    tc = 128
    nb = D // tc
    r0 = row0 // DEC_SEQ

    def col(part):
        return [pl.BlockSpec((DEC_SEQ, tc), lambda b, c: (r0 + b, part * nb + c)),
                pl.BlockSpec((3, tc), lambda b, c: (0, part * nb + c)),
                pl.BlockSpec((1, tc), lambda b, c: (0, part * nb + c))]

    specs = [col(p) for p in range(3)]
    chan = pl.BlockSpec((DEC_SEQ, tc), lambda b, c: (0, c))
    const = lambda m: pl.BlockSpec(m.shape, lambda b, c: (0,) * m.ndim)
    cb = conv_b.reshape(1, 3 * D)
    return pl.pallas_call(
        _hy_conv_fft_body,
        out_shape=jax.ShapeDtypeStruct((n_seq * DEC_SEQ, D), BF16),
        grid=(n_seq, nb),
        in_specs=([s[0] for s in specs] + [s[1] for s in specs] + [s[2] for s in specs]
                  + [chan, chan, pl.BlockSpec((1, tc), lambda b, c: (0, c))] + [const(m) for m in consts]),
        out_specs=pl.BlockSpec((DEC_SEQ, tc), lambda b, c: (b, c)),
        scratch_shapes=[pltpu.VMEM((DEC_SEQ, tc), F32), pltpu.VMEM((FFT_R * 2 * FFT_M, tc), F32),
                        pltpu.VMEM((DEC_SEQ, tc), F32)],
        compiler_params=_params(2, 56 * 1024 * 1024),
        name="hyena_conv_fft",
    )(u, u, u, conv_w, conv_w, conv_w, cb, cb, cb, kr, ki, skip.reshape(1, D), *consts)


ROW_TILE = D // LANES
ROUTER_TM = 512
EXPERT_TM = 512
N_SLOTS = 2 * N_TOK + N_EXPERTS * EXPERT_TM
N_SLOT_TILES = N_SLOTS // EXPERT_TM
INFO_E1, INFO_E2, INFO_R1, INFO_R2, INFO_W1, INFO_W2 = range(6)


def _to_row_tiles(ref, x):
    rows = x.shape[0]
    for j in range(ROW_TILE):
        ref[pl.ds(j, rows, stride=ROW_TILE), :] = x[:, j * LANES:(j + 1) * LANES]


def _from_row_tiles(ref, rows):
    return jnp.concatenate([ref[pl.ds(j, rows, stride=ROW_TILE), :] for j in range(ROW_TILE)], axis=-1)


def _router_body(x_ref, sh_ref, sc_ref, w_ref, info_ref, incl_ref, cnt_ref):
    @pl.when(pl.program_id(0) == 0)
    def _():
        cnt_ref[...] = jnp.zeros_like(cnt_ref)

    h = _modulate(x_ref, sh_ref, sc_ref)
    h_hi, w = h.astype(BF16), w_ref[...]
    h_lo, w_hi = (h - h_hi.astype(F32)).astype(BF16), w.astype(BF16)
    w_lo = (w - w_hi.astype(F32)).astype(BF16)
    logits = (jnp.dot(h_hi, w_hi, preferred_element_type=F32) + jnp.dot(h_lo, w_hi, preferred_element_type=F32)
              + jnp.dot(h_hi, w_lo, preferred_element_type=F32))
    lane = lax.broadcasted_iota(jnp.int32, logits.shape, 1).astype(F32)
    logits = jnp.where(lane < N_EXPERTS, logits, -jnp.inf)
    e = jnp.exp(logits - jnp.max(logits, axis=-1, keepdims=True))
    p = e / jnp.sum(e, axis=-1, keepdims=True)
    p1 = jnp.max(p, axis=-1, keepdims=True)
    i1 = jnp.min(jnp.where(p == p1, lane, float(LANES)), axis=-1, keepdims=True)
    rest = jnp.where(lane == i1, -1.0, p)
    p2 = jnp.max(rest, axis=-1, keepdims=True)
    i2 = jnp.min(jnp.where(rest == p2, lane, float(LANES)), axis=-1, keepdims=True)
    total = p1 + p2
    chosen = jnp.where((lane == i1) | (lane == i2), 1.0, 0.0)
    tm = chosen.shape[0]
    earlier = (lax.broadcasted_iota(jnp.int32, (tm, tm), 1) < lax.broadcasted_iota(jnp.int32, (tm, tm), 0))
    rank = jnp.dot(earlier.astype(BF16), chosen.astype(BF16), preferred_element_type=F32) + cnt_ref[...]
    r1 = jnp.sum(jnp.where(lane == i1, rank, 0.0), axis=-1, keepdims=True)
    r2 = jnp.sum(jnp.where(lane == i2, rank, 0.0), axis=-1, keepdims=True)
    cnt_ref[...] += jnp.sum(chosen, axis=0, keepdims=True)
    incl_ref[0] = jnp.broadcast_to(cnt_ref[...], incl_ref.shape[1:])
    info = jnp.zeros_like(p)
    for col, val in ((INFO_E1, i1), (INFO_E2, i2), (INFO_R1, r1), (INFO_R2, r2),
                     (INFO_W1, p1 / total), (INFO_W2, p2 / total)):
        info = jnp.where(lane == col, val, info)
    info_ref[...] = info


def _router(x, sh, sc, w_router):
    tm = ROUTER_TM
    return pl.pallas_call(
        _router_body,
        out_shape=(jax.ShapeDtypeStruct((N_TOK, LANES), F32),
                   jax.ShapeDtypeStruct((N_TOK // tm, 8, LANES), F32)),
        grid=(N_TOK // tm,),
        in_specs=_mod_specs(tm, 0) + [pl.BlockSpec((D, LANES), lambda i: (0, 0))],
        out_specs=(pl.BlockSpec((tm, LANES), lambda i: (i, 0)), pl.BlockSpec((1, 8, LANES), lambda i: (i, 0, 0))),
        scratch_shapes=[pltpu.VMEM((1, LANES), F32)],
        compiler_params=_params(1),
        name="moe_router",
    )(x, sh, sc, jnp.pad(w_router, ((0, 0), (0, LANES - N_EXPERTS))))


def _row_tile(ref, row):
    return ref.at[pl.ds(pl.multiple_of(row * ROW_TILE, ROW_TILE), ROW_TILE)]


def _dispatch_body(pos1_ref, pos2_ref, x_ref, zeros_hbm, xs_hbm, rows_ref, sem, *, tm):
    del zeros_hbm
    base = pl.program_id(0) * tm
    _to_row_tiles(rows_ref, x_ref[...])

    def copies(r):
        src = _row_tile(rows_ref, r)
        return (pltpu.make_async_copy(src, _row_tile(xs_hbm, pos1_ref[base + r]), sem),
                pltpu.make_async_copy(src, _row_tile(xs_hbm, pos2_ref[base + r]), sem))

    def issue(r, carry):
        for queue, cp in enumerate(copies(r)):
            cp.start(priority=queue)
        return carry

    lax.fori_loop(0, tm, issue, 0, unroll=8)
    for _ in range(2):
        pltpu.make_async_copy(rows_ref, xs_hbm.at[pl.ds(0, tm * ROW_TILE)], sem).wait()


def _dispatch(pos1, pos2, x):
    tm = 512
    return pl.pallas_call(
        functools.partial(_dispatch_body, tm=tm),
        out_shape=jax.ShapeDtypeStruct((N_SLOTS * ROW_TILE, LANES), F32),
        grid_spec=pltpu.PrefetchScalarGridSpec(
            num_scalar_prefetch=2, grid=(N_TOK // tm,),
            in_specs=[pl.BlockSpec((tm, D), lambda i, *_: (i, 0)), pl.BlockSpec(memory_space=pl.ANY)],
            out_specs=pl.BlockSpec(memory_space=pl.ANY),
            scratch_shapes=[pltpu.VMEM((tm * ROW_TILE, LANES), F32), pltpu.SemaphoreType.DMA(())]),
        input_output_aliases={3: 0},
        compiler_params=_params(1),
        name="moe_dispatch",
    )(pos1, pos2, x, jnp.zeros((N_SLOTS * ROW_TILE, LANES), F32))


def _new_expert(eid_ref, nv_ref, t):
    tt = jnp.minimum(t, nv_ref[0] - 1)
    return (t == 0) | (eid_ref[tt] != eid_ref[jnp.maximum(tt - 1, 0)])


def _expert_swiglu_body(eid_ref, b1_ref, b2_ref, nv_ref, xs_ref, sh_ref, sc_ref, wg_ref, wu_ref, o_ref,
                        wgb_ref, wub_ref, *, fc):
    t = pl.program_id(1)

    @pl.when(_new_expert(eid_ref, nv_ref, t))
    def _():
        wgb_ref[...] = wg_ref[0].astype(BF16)
        wub_ref[...] = wu_ref[0].astype(BF16)

    @pl.when(t < nv_ref[0])
    def _():
        tm = o_ref.shape[0]
        x = _from_row_tiles(xs_ref, tm)
        slot = t * tm + lax.broadcasted_iota(jnp.int32, (tm, 1), 0)
        in1, in2 = slot >= b1_ref[t], slot >= b2_ref[t]
        sc = jnp.where(in2, sc_ref[2], jnp.where(in1, sc_ref[1], sc_ref[0]))
        sh = jnp.where(in2, sh_ref[2], jnp.where(in1, sh_ref[1], sh_ref[0]))
        h = (x * (1.0 + sc) + sh).astype(BF16)
        for c in range(o_ref.shape[1] // fc):
            cs = slice(c * fc, (c + 1) * fc)
            g = jnp.dot(h, wgb_ref[:, cs], preferred_element_type=F32)
            u = jnp.dot(h, wub_ref[:, cs], preferred_element_type=F32)
            o_ref[:, cs] = (g * jax.nn.sigmoid(g) * u).astype(BF16)

    @pl.when(t >= nv_ref[0])
    def _():
        o_ref[...] = jnp.zeros_like(o_ref)


def _expert_swiglu(meta, xs_rt, sh, sc, w_gate, w_up, e0):
    eid, b1, b2, nv = meta
    tm, f = EXPERT_TM, w_gate.shape[2]
    fh = f // 2

    def tile(t, eid, b1, b2, nv):
        return jnp.minimum(t, nv[0] - 1)

    w_spec = pl.BlockSpec((1, D, fh), lambda p, t, eid, b1, b2, nv: (e0 + eid[tile(t, eid, b1, b2, nv)], 0, p))
    mod = pl.BlockSpec((8, 1, D), lambda p, t, *_: (0, 0, 0))
    return pl.pallas_call(
        functools.partial(_expert_swiglu_body, fc=256),
        out_shape=jax.ShapeDtypeStruct((N_SLOTS, f), BF16),
        grid_spec=pltpu.PrefetchScalarGridSpec(
            num_scalar_prefetch=4, grid=(2, N_SLOT_TILES),
            in_specs=[pl.BlockSpec((tm * ROW_TILE, LANES), lambda p, t, *m: (tile(t, *m), 0)), mod, mod,
                      w_spec, w_spec],
            out_specs=pl.BlockSpec((tm, fh), lambda p, t, *m: (t, p)),
            scratch_shapes=[pltpu.VMEM((D, fh), BF16), pltpu.VMEM((D, fh), BF16)]),
        compiler_params=_params(2, 56 * 1024 * 1024),
        name="moe_swiglu",
    )(eid, b1, b2, nv, xs_rt, sh, sc, w_gate, w_up)


def _expert_down_body(eid_ref, nv_ref, a_ref, w_ref, y_ref, wb_ref):
    t = pl.program_id(0)

    @pl.when(_new_expert(eid_ref, nv_ref, t))
    def _():
        wb_ref[...] = w_ref[0].astype(BF16)

    @pl.when(t < nv_ref[0])
    def _():
        _to_row_tiles(y_ref, jnp.dot(a_ref[...], wb_ref[...], preferred_element_type=F32))

    @pl.when(t >= nv_ref[0])
    def _():
        y_ref[...] = jnp.zeros_like(y_ref)


def _expert_down(meta, act, w_down, e0):
    eid, _, _, nv = meta
    tm, f = EXPERT_TM, act.shape[1]

    def tile(t, eid, nv):
        return jnp.minimum(t, nv[0] - 1)

    return pl.pallas_call(
        _expert_down_body,
        out_shape=jax.ShapeDtypeStruct((N_SLOTS * ROW_TILE, LANES), F32),
        grid_spec=pltpu.PrefetchScalarGridSpec(
            num_scalar_prefetch=2, grid=(N_SLOT_TILES,),
            in_specs=[pl.BlockSpec((tm, f), lambda t, *m: (tile(t, *m), 0)),
                      pl.BlockSpec((1, f, D), lambda t, eid, nv: (e0 + eid[tile(t, eid, nv)], 0, 0))],
            out_specs=pl.BlockSpec((tm * ROW_TILE, LANES), lambda t, *m: (t, 0)),
            scratch_shapes=[pltpu.VMEM((f, D), BF16)]),
        compiler_params=_params(1, 58 * 1024 * 1024),
        name="moe_down",
    )(eid, nv, act, w_down)


def _combine_body(pos1_ref, pos2_ref, y_hbm, info_ref, x_ref, g_ref, lng_ref, lnb_ref, o_ref, y1_ref, y2_ref, sem,
                  *, tm):
    base = pl.program_id(0) * tm

    def copies(r):
        return (pltpu.make_async_copy(_row_tile(y_hbm, pos1_ref[base + r]), _row_tile(y1_ref, r), sem),
                pltpu.make_async_copy(_row_tile(y_hbm, pos2_ref[base + r]), _row_tile(y2_ref, r), sem))

    def issue(r, carry):
        for queue, cp in enumerate(copies(r)):
            cp.start(priority=queue)
        return carry

    lax.fori_loop(0, tm, issue, 0, unroll=8)
    for y_ref in (y1_ref, y2_ref):
        pltpu.make_async_copy(y_hbm.at[pl.ds(0, tm * ROW_TILE)], y_ref, sem).wait()
    info = info_ref[...]
    ffn = (info[:, INFO_W1:INFO_W1 + 1] * _from_row_tiles(y1_ref, tm)
           + info[:, INFO_W2:INFO_W2 + 1] * _from_row_tiles(y2_ref, tm))
    y = ALPHA * x_ref[...] + g_ref[0] * ffn
    mu = jnp.mean(y, axis=-1, keepdims=True)
    yc = y - mu
    var = jnp.mean(yc * yc, axis=-1, keepdims=True)
    o_ref[...] = yc * lax.rsqrt(var + LN_EPS) * lng_ref[...] + lnb_ref[...]


def _combine(pos1, pos2, y_rt, info, x, gate, ln_g, ln_b):
    tm = 512
    return pl.pallas_call(
        functools.partial(_combine_body, tm=tm),
        out_shape=jax.ShapeDtypeStruct((N_TOK, D), F32),
        grid_spec=pltpu.PrefetchScalarGridSpec(
            num_scalar_prefetch=2, grid=(N_TOK // tm,),
            in_specs=[pl.BlockSpec(memory_space=pl.ANY),
                      pl.BlockSpec((tm, LANES), lambda i, *_: (i, 0)),
                      pl.BlockSpec((tm, D), lambda i, *_: (i, 0)),
                      pl.BlockSpec((1, 1, D), lambda i, *_: (_group_of_row(i * tm), 0, 0)),
                      pl.BlockSpec((1, D), lambda i, *_: (0, 0)),
                      pl.BlockSpec((1, D), lambda i, *_: (0, 0))],
            out_specs=pl.BlockSpec((tm, D), lambda i, *_: (i, 0)),
            scratch_shapes=[pltpu.VMEM((tm * ROW_TILE, LANES), F32), pltpu.VMEM((tm * ROW_TILE, LANES), F32),
                            pltpu.SemaphoreType.DMA(())]),
        compiler_params=_params(1),
        name="moe_combine",
    )(pos1, pos2, y_rt, info, x, gate, ln_g.reshape(1, D), ln_b.reshape(1, D))


def _slot_plan(info, incl):
    row = lambda n_rows: incl[n_rows // ROUTER_TM - 1, 0, :N_EXPERTS].astype(jnp.int32)
    count = row(N_TOK)
    padded = (count + EXPERT_TM - 1) // EXPERT_TM * EXPERT_TM
    end = jnp.cumsum(padded)
    start = end - padded
    tile_row = jnp.arange(N_SLOT_TILES, dtype=jnp.int32) * EXPERT_TM
    eid = jnp.minimum(jnp.sum(tile_row[:, None] >= end[None, :], axis=1), N_EXPERTS - 1).astype(jnp.int32)
    b1 = (start + row(N_CTX))[eid]
    b2 = (start + row(N_CTX + DEC_SEQ))[eid]
    nv = (end[-1:] // EXPERT_TM).astype(jnp.int32)
    experts = jnp.arange(N_EXPERTS, dtype=jnp.int32)
    start_of = lambda col: jnp.sum(jnp.where(info[:, col:col + 1].astype(jnp.int32) == experts, start, 0), axis=1)
    pos1 = start_of(INFO_E1) + info[:, INFO_R1].astype(jnp.int32)
    pos2 = start_of(INFO_E2) + info[:, INFO_R2].astype(jnp.int32)
    return pos1, pos2, (eid, b1, b2, nv)


def _even_mixer(x, sh, sc, gate, ln_g, ln_b, w_in, b_igate, b_fgate, ml_norm_g, q_norm_g, k_norm_g, w_out,
                st_c, st_n, st_m, cache_k, cache_v, rope_tabs):
    splits = (4 * ML_W, 4 * ML_W + N_GATES)
    w_main = jnp.concatenate([w_in[:, :splits[0]], w_in[:, splits[1]:]], axis=1).astype(BF16)
    proj = _mod_matmul(x, sh, sc, [w_main[None]], tm=1024, tn=MAIN_W // 2, out_dtype=F32, name="even_in_proj")[0]
    b_gate = jnp.stack([b_igate, b_fgate], axis=1).reshape(N_GATES)
    lic, bc, lir, br = _gates(x, sh, sc, w_in[:, splits[0]:splits[1]], b_gate)

    hf_c, hb_c, new_c, new_n, new_m = _mlstm(proj, lic, bc, lir, br, row0=0, n_seq=BATCH, seq_len=SEQ)
    init = (st_c, st_n, jnp.broadcast_to(st_m[..., None], st_n.shape))
    hf_s, hb_s, _, _, _ = _mlstm(proj, lic, bc, lir, br, row0=N_CTX, n_seq=DEC_BATCH, seq_len=DEC_SEQ, init=init)
    ml = _ml_post((hf_c, hf_s), (hb_c, hb_s), proj, ml_norm_g)

    q_c, kn_c, kb_c = _qk_prep(proj, q_norm_g, k_norm_g, row0=0, rows=N_CTX)
    q_s, _, kb_s = _qk_prep(proj, q_norm_g, k_norm_g, row0=N_CTX, rows=N_LAT, rope_tabs=rope_tabs)
    v_all = proj[:, MAIN_W - KV_W:]
    v_c, v_s = v_all[:N_CTX], v_all[N_CTX:]
    att_c = _attention(q_c, _head_major(kb_c, BATCH).swapaxes(2, 3),
                       _with_ones(_head_major(v_c.astype(BF16), BATCH)), tq=SEQ)
    k_lat = jnp.concatenate([kb_s.reshape(DEC_BATCH, DEC_SEQ, KV_W),
                             cache_k.reshape(DEC_BATCH, PAST_LEN, KV_W).astype(BF16)], axis=1)
    v_lat = jnp.concatenate([v_s.reshape(DEC_BATCH, DEC_SEQ, KV_W).astype(BF16),
                             cache_v.reshape(DEC_BATCH, PAST_LEN, KV_W).astype(BF16)], axis=1)
    att_s = _attention(q_s, _head_major(k_lat.reshape(-1, KV_W), DEC_BATCH).swapaxes(2, 3),
                       _with_ones(_head_major(v_lat.reshape(-1, KV_W), DEC_BATCH)), tq=256)

    x = _proj_res_ln([ml, (att_c, att_s)], w_out.astype(BF16), x, gate, ln_g, ln_b, tm=512, name="even_out_proj")
    new_k = kn_c.reshape(BATCH, SEQ, ATT_KV_HEADS, ATT_HEAD_DIM)
    new_v = v_c.reshape(BATCH, SEQ, ATT_KV_HEADS, ATT_HEAD_DIM)
    return x, new_k, new_v, new_c, new_n, new_m[..., 0]


def _hyena_mixer(x, sh, sc, gate, ln_g, ln_b, w_in, conv_w, conv_b, w1, b1, w2, b2, w3, sin_freq, skip, w_out, dft):
    u = _mod_matmul(x, sh, sc, [w_in.astype(BF16)[None]], tm=1024, tn=1536, out_dtype=F32, name="hyena_in_proj")[0]
    mats, consts = dft
    h_sum, h_diff, nyq = _hy_filter(SEQ, w1, b1, w2, b2, w3, sin_freq)
    kr, ki = _ctx_spectrum(mats, h_sum, h_diff, nyq)
    y_c = _hy_conv_ctx(u, conv_w, conv_b, kr, ki, skip, mats, seqs_per_step=8)
    h_sum, h_diff, _ = _hy_filter(DEC_SEQ, w1, b1, w2, b2, w3, sin_freq)
    kr, ki = _hy_spectrum(h_sum, h_diff, consts)
    y_s = _hy_conv_fft(u, conv_w, conv_b, kr, ki, skip, consts, row0=N_CTX, n_seq=DEC_BATCH)
    return _proj_res_ln([(y_c, y_s)], w_out.astype(BF16), x, gate, ln_g, ln_b, tm=512, name="hyena_out_proj")


def _dense_ffn(x, sh, sc, gate, ln_g, ln_b, w_gate, w_up, w_down):
    act = _mod_matmul(x, sh, sc, [w_gate.astype(BF16)[None], w_up.astype(BF16)[None]],
                      tm=1024, tn=D_FF // 2, out_dtype=BF16, name="ffn_swiglu")
    return _proj_res_ln([act[0]], w_down.astype(BF16), x, gate, ln_g, ln_b, tm=512, name="ffn_down")


def _moe_ffn(x, sh, sc, gate, ln_g, ln_b, w_router, w_gate, w_up, w_down, e0):
    info, incl = _router(x, sh, sc, w_router)
    pos1, pos2, meta = _slot_plan(info, incl)
    xs_rt = _dispatch(pos1, pos2, x)
    act = _expert_swiglu(meta, xs_rt, sh, sc, w_gate, w_up, e0)
    y_rt = _expert_down(meta, act, w_down, e0)
    return _combine(pos1, pos2, y_rt, info, x, gate, ln_g, ln_b)


def kernel(x_prompt, x_sample, cache_attn_k, cache_attn_v, state_mlstm_C, state_mlstm_n, state_mlstm_m, c, c_ctx, w_ada, b_ada, ln_g, ln_b, w_in_even, b_igate, b_fgate, ml_norm_g, q_norm_g, k_norm_g, w_out_even, w_ffn_gate, w_ffn_up, w_ffn_down, w_in_hy, hy_conv_w, hy_conv_b, hy_filt_w1, hy_filt_b1, hy_filt_w2, hy_filt_b2, hy_filt_w3, hy_sin_freq, hy_skip, w_out_hy, w_router, w_moe_gate, w_moe_up, w_moe_down):
    x = jnp.concatenate([x_prompt.reshape(N_CTX, D), x_sample.reshape(N_LAT, D)])
    cvec = jnp.concatenate([c_ctx[None], c, jnp.zeros((8 - 1 - DEC_BATCH, D), F32)])
    mods = _ada(cvec, w_ada, b_ada)
    rope_tabs = _rope_tables()
    dft = (_dft_matrices(SEQ), _fft_consts())
    moe_w = [w.reshape((-1,) + w.shape[2:]) for w in (w_moe_gate, w_moe_up, w_moe_down)]
    new_k, new_v, new_c, new_n, new_m = [], [], [], [], []
    for layer in range(DEPTH):
        sh1, sc1, g1, sh2, sc2, g2 = (mods[layer, :, i * D:(i + 1) * D].reshape(8, 1, D) for i in range(6))
        i = layer // 2
        if layer % 2 == 0:
            x, k_c, v_c, st_c, st_n, st_m = _even_mixer(
                x, sh1, sc1, g1, ln_g[layer, 0], ln_b[layer, 0], w_in_even[i], b_igate[i], b_fgate[i], ml_norm_g[i],
                q_norm_g[i], k_norm_g[i], w_out_even[i], state_mlstm_C[:, i], state_mlstm_n[:, i], state_mlstm_m[:, i],
                cache_attn_k[:, i], cache_attn_v[:, i], rope_tabs)
            new_k.append(k_c)
            new_v.append(v_c)
            new_c.append(st_c)
            new_n.append(st_n)
            new_m.append(st_m)
            x = _dense_ffn(x, sh2, sc2, g2, ln_g[layer, 1], ln_b[layer, 1], w_ffn_gate[i], w_ffn_up[i], w_ffn_down[i])
        else:
            x = _hyena_mixer(x, sh1, sc1, g1, ln_g[layer, 0], ln_b[layer, 0], w_in_hy[i], hy_conv_w[i], hy_conv_b[i],
                             hy_filt_w1[i], hy_filt_b1[i], hy_filt_w2[i], hy_filt_b2[i], hy_filt_w3[i], hy_sin_freq[i],
                             hy_skip[i], w_out_hy[i], dft)
            x = _moe_ffn(x, sh2, sc2, g2, ln_g[layer, 1], ln_b[layer, 1], w_router[i], *moe_w, i * N_EXPERTS)
    return (x[:N_CTX].reshape(BATCH, SEQ, D), x[N_CTX:].reshape(DEC_BATCH, DEC_SEQ, D),
            jnp.stack(new_k, axis=1), jnp.stack(new_v, axis=1), jnp.stack(new_c, axis=1),
            jnp.stack(new_n, axis=1), jnp.stack(new_m, axis=1))
```

```python
import functools
import math

import jax
import jax.numpy as jnp
import numpy as np
from jax import lax
from jax.experimental import pallas as pl
from jax.experimental.pallas import tpu as pltpu

F32 = jnp.float32
BF16 = jnp.bfloat16
HIGHEST = lax.Precision.HIGHEST

D = 1024
BATCH, SEQ = 32, 256
DEC_BATCH, DEC_SEQ = 2, 4096
DEPTH = 4
PAST_LEN = 256
GRID_W = 64
N_CTX = BATCH * SEQ
N_LAT = DEC_BATCH * DEC_SEQ
N_TOK = N_CTX + N_LAT

ML_HEADS, ML_HEAD_DIM = 4, 128
ML_W = ML_HEADS * ML_HEAD_DIM
CHUNK = 128
MLSTM_PAR = 2
ATT_HEADS, ATT_KV_HEADS, ATT_HEAD_DIM = 8, 2, 64
ATT_GROUP = ATT_HEADS // ATT_KV_HEADS
ATT_W = ATT_HEADS * ATT_HEAD_DIM
KV_W = ATT_KV_HEADS * ATT_HEAD_DIM
GROUP_W = ATT_GROUP * ATT_HEAD_DIM
ROPE_BASE = 10000.0
N_GATES = 4 * ML_HEADS
MAIN_W = 4 * ML_W + ATT_W + 2 * KV_W

HY_EMB = 33
HY_BANDS = (HY_EMB - 1) // 2
HY_TARGET, HY_SHORT_PCT, HY_LONG_PCT = 1e-2, 0.3, 1.5
D_FF = 2816
N_EXPERTS = 8
MOE_D_FF = 3584
ALPHA = (2 * DEPTH) ** 0.25
LN_EPS = 1e-5
RMS_EPS = 1e-6

LANES = 128
VMEM_LIMIT = 48 * 1024 * 1024


def _params(n_axes, vmem=VMEM_LIMIT):
    return pltpu.CompilerParams(dimension_semantics=("arbitrary",) * n_axes, vmem_limit_bytes=vmem)


def _group_of_row(r):
    return jnp.where(r < N_CTX, 0, 1 + (r - N_CTX) // DEC_SEQ)


def _modulate(x_ref, sh_ref, sc_ref):
    return x_ref[...] * (1.0 + sc_ref[0]) + sh_ref[0]


def _mod_specs(tm, row_axis):
    def rows(*ids):
        return (ids[row_axis], 0)

    def grp(*ids):
        return (_group_of_row(ids[row_axis] * tm), 0, 0)

    return [pl.BlockSpec((tm, D), rows), pl.BlockSpec((1, 1, D), grp), pl.BlockSpec((1, 1, D), grp)]


def _ada_body(c_ref, w_ref, b_ref, o_ref):
    c = c_ref[...]
    s = c * jax.nn.sigmoid(c)
    o_ref[0] = jnp.dot(s, w_ref[0], preferred_element_type=F32, precision=HIGHEST) + b_ref[0]


def _ada(cvec, w_ada, b_ada):
    tn = 1536
    return pl.pallas_call(
        _ada_body,
        out_shape=jax.ShapeDtypeStruct((DEPTH, 8, 6 * D), F32),
        grid=(DEPTH, 6 * D // tn),
        in_specs=[pl.BlockSpec((8, D), lambda l, j: (0, 0)),
                  pl.BlockSpec((1, D, tn), lambda l, j: (l, 0, j)),
                  pl.BlockSpec((1, 1, tn), lambda l, j: (l, 0, j))],
        out_specs=pl.BlockSpec((1, 8, tn), lambda l, j: (l, 0, j)),
        compiler_params=_params(2),
        name="ada_modulation",
    )(cvec, w_ada, b_ada.reshape(DEPTH, 1, 6 * D))


def _mod_mm_body(x_ref, sh_ref, sc_ref, *refs, n_w):
    w_refs, o_ref, h_ref = refs[:n_w], refs[n_w], refs[n_w + 1]

    @pl.when(pl.program_id(2) == 0)
    def _():
        h_ref[...] = _modulate(x_ref, sh_ref, sc_ref).astype(BF16)

    h = h_ref[...]
    if n_w == 1:
        o = jnp.dot(h, w_refs[0][0], preferred_element_type=F32)
    else:
        g = jnp.dot(h, w_refs[0][0], preferred_element_type=F32)
        u = jnp.dot(h, w_refs[1][0], preferred_element_type=F32)
        o = g * jax.nn.sigmoid(g) * u
    o_ref[0] = o.astype(o_ref.dtype)


def _mod_matmul(x, sh, sc, ws, *, tm, tn, out_dtype, name):
    n_e, _, f = ws[0].shape
    return pl.pallas_call(
        functools.partial(_mod_mm_body, n_w=len(ws)),
        out_shape=jax.ShapeDtypeStruct((n_e, N_TOK, f), out_dtype),
        grid=(n_e, N_TOK // tm, f // tn),
        in_specs=_mod_specs(tm, 1) + [pl.BlockSpec((1, D, tn), lambda e, i, j: (e, 0, j)) for _ in ws],
        out_specs=pl.BlockSpec((1, tm, tn), lambda e, i, j: (e, i, j)),
        scratch_shapes=[pltpu.VMEM((tm, D), BF16)],
        compiler_params=_params(3),
        name=name,
    )(x, sh, sc, *ws)


def _split_specs(tm, width):
    nc = N_CTX // tm
    return [pl.BlockSpec((tm, width), lambda i: (jnp.minimum(i, nc - 1), 0)),
            pl.BlockSpec((tm, width), lambda i: (jnp.maximum(i - nc, 0), 0))]


def _pick_split(ctx_ref, lat_ref):
    tm = ctx_ref.shape[0]
    return jnp.where(pl.program_id(0) < N_CTX // tm, ctx_ref[...], lat_ref[...])


def _proj_res_ln_body(*refs, split):
    n_in = sum(2 if s else 1 for s in split)
    part_refs = list(refs[:n_in])
    w_ref, x_ref, g_ref, lng_ref, lnb_ref, o_ref = refs[n_in:]
    cols = [_pick_split(part_refs.pop(0), part_refs.pop(0)) if s else part_refs.pop(0)[...] for s in split]
    a = cols[0] if len(cols) == 1 else jnp.concatenate(cols, axis=-1)
    y = ALPHA * x_ref[...] + g_ref[0] * jnp.dot(a, w_ref[...], preferred_element_type=F32)
    mu = jnp.mean(y, axis=-1, keepdims=True)
    yc = y - mu
    var = jnp.mean(yc * yc, axis=-1, keepdims=True)
    o_ref[...] = yc * lax.rsqrt(var + LN_EPS) * lng_ref[...] + lnb_ref[...]


def _proj_res_ln(parts, w, x, gate, ln_g, ln_b, *, tm, name):
    split = tuple(isinstance(p, tuple) for p in parts)
    in_specs, args = [], []
    for p, s in zip(parts, split):
        if s:
            in_specs += _split_specs(tm, p[0].shape[1])
            args += list(p)
        else:
            in_specs.append(pl.BlockSpec((tm, p.shape[1]), lambda i: (i, 0)))
            args.append(p)
    in_specs += [pl.BlockSpec(w.shape, lambda i: (0, 0)),
                 pl.BlockSpec((tm, D), lambda i: (i, 0)),
                 pl.BlockSpec((1, 1, D), lambda i: (_group_of_row(i * tm), 0, 0)),
                 pl.BlockSpec((1, D), lambda i: (0, 0)),
                 pl.BlockSpec((1, D), lambda i: (0, 0))]
    return pl.pallas_call(
        functools.partial(_proj_res_ln_body, split=split),
        out_shape=jax.ShapeDtypeStruct((N_TOK, D), F32),
        grid=(N_TOK // tm,),
        in_specs=in_specs,
        out_specs=pl.BlockSpec((tm, D), lambda i: (i, 0)),
        compiler_params=_params(1),
        name=name,
    )(*args, w, x, gate, ln_g.reshape(1, D), ln_b.reshape(1, D))


def _log_sigmoid(x):
    return jnp.minimum(x, 0.0) - jnp.log(1.0 + jnp.exp(-jnp.abs(x)))


def _gates_body(x_ref, sh_ref, sc_ref, wg_ref, wgt_ref, b_ref, bt_ref,
                lic_ref, bc_ref, lir_ref, br_ref, *, tm):
    h = _modulate(x_ref, sh_ref, sc_ref).astype(BF16)
    g = jnp.dot(h, wg_ref[...], preferred_element_type=F32) + b_ref[...]
    gt = lax.dot_general(wgt_ref[...], h, (((1,), (1,)), ((), ())), preferred_element_type=F32) + bt_ref[...]
    lic_ref[...] = g
    lir_ref[...] = gt
    lf, lft = _log_sigmoid(g), _log_sigmoid(gt)
    r = lax.broadcasted_iota(jnp.int32, (CHUNK, CHUNK), 0)
    c = lax.broadcasted_iota(jnp.int32, (CHUNK, CHUNK), 1)
    tri_l = (c <= r).astype(BF16)
    tri_u = (c >= r).astype(BF16)

    def pieces(v):
        out = []
        for _ in range(3):
            out.append(v.astype(BF16))
            v = v - out[-1].astype(F32)
        return out

    def cumsum(tri, v, tri_first):
        dots = [jnp.dot(tri, p, preferred_element_type=F32) if tri_first else jnp.dot(p, tri, preferred_element_type=F32)
                for p in pieces(v)]
        return dots[0] + dots[1] + dots[2]

    fwd_col = lax.broadcasted_iota(jnp.int32, (CHUNK, LANES), 1) < 2 * ML_HEADS
    fwd_row = lax.broadcasted_iota(jnp.int32, (N_GATES, CHUNK), 0) < 2 * ML_HEADS
    for ch in range(tm // CHUNK):
        sl = slice(ch * CHUNK, (ch + 1) * CHUNK)
        lfc, lftc = lf[sl, :], lft[:, sl]
        bc_ref[sl, :] = jnp.where(fwd_col, cumsum(tri_l, lfc, True), cumsum(tri_u, lfc, True))
        br_ref[:, sl] = jnp.where(fwd_row, cumsum(tri_u, lftc, False), cumsum(tri_l, lftc, False))


def _gates(x, sh, sc, wg, b_gate):
    tm = 512
    wg_pad = jnp.pad(wg, ((0, 0), (0, LANES - N_GATES)))
    b_pad = jnp.pad(b_gate, (0, LANES - N_GATES)).reshape(1, LANES)
    col = pl.BlockSpec((tm, LANES), lambda i: (i, 0))
    row = pl.BlockSpec((N_GATES, tm), lambda i: (0, i))
    return pl.pallas_call(
        functools.partial(_gates_body, tm=tm),
        out_shape=(jax.ShapeDtypeStruct((N_TOK, LANES), F32), jax.ShapeDtypeStruct((N_TOK, LANES), F32),
                   jax.ShapeDtypeStruct((N_GATES, N_TOK), F32), jax.ShapeDtypeStruct((N_GATES, N_TOK), F32)),
        grid=(N_TOK // tm,),
        in_specs=_mod_specs(tm, 0) + [pl.BlockSpec((D, LANES), lambda i: (0, 0)),
                                         pl.BlockSpec((N_GATES, D), lambda i: (0, 0)),
                                         pl.BlockSpec((1, LANES), lambda i: (0, 0)),
                                         pl.BlockSpec((N_GATES, 1), lambda i: (0, 0))],
        out_specs=(col, col, row, row),
        compiler_params=_params(1),
        name="mlstm_gates",
    )(x, sh, sc, wg_pad.astype(BF16), wg.T.astype(BF16), b_pad, b_gate.reshape(N_GATES, 1))


def _mlstm_body(*refs, has_init):
    (qf, kf, vf, licf, bcf, lirf, brf, qb, kb, vb, licb, bcb, lirb, brb) = refs[:14]
    refs = refs[14:]
    if has_init:
        c0_ref, n0_ref, m0_ref = refs[:3]
        refs = refs[3:]
    hf_ref, hb_ref, c_ref, n_ref, m_ref = refs

    @pl.when(pl.program_id(1) == 0)
    def _():
        if has_init:
            c_ref[...] = c0_ref[...]
            n_ref[...] = n0_ref[...]
            m_ref[...] = m0_ref[...]
        else:
            c_ref[...] = jnp.zeros_like(c_ref)
            n_ref[...] = jnp.zeros_like(n_ref)
            m_ref[...] = jnp.zeros_like(m_ref)

    t_idx = lax.broadcasted_iota(jnp.int32, (CHUNK, CHUNK), 0)
    s_idx = lax.broadcasted_iota(jnp.int32, (CHUNK, CHUNK), 1)
    nt = (((1,), (1,)), ((), ()))
    stores = []

    def chain(u, d, h, q_ref, k_ref, v_ref, lic_ref, bc_ref, lir_ref, br_ref, h_ref):
        mask = (s_idx <= t_idx) if d == 0 else (s_idx >= t_idx)
        hs = slice(h * ML_HEAD_DIM, (h + 1) * ML_HEAD_DIM)
        gi, gf = d * 2 * ML_HEADS + h, d * 2 * ML_HEADS + ML_HEADS + h
        q = q_ref[u, :, hs]
        k = k_ref[u, :, hs] * (ML_HEAD_DIM ** -0.5)
        v = v_ref[u, :, hs]
        qh, kh, vh = q.astype(BF16), k.astype(BF16), v.astype(BF16)
        li_c, b_c = lic_ref[u, :, gi:gi + 1], bc_ref[u, :, gf:gf + 1]
        li_r, b_r = lir_ref[u, gi:gi + 1, :], br_ref[u, gf:gf + 1, :]
        c_st = c_ref[u, d, h]
        n_st = n_ref[u, d, h:h + 1, :]
        m_st = m_ref[u, d, h:h + 1, :][:, 0:1]
        dmat = jnp.where(mask, b_c - b_r + li_r, -jnp.inf)
        inter = b_c + m_st
        m_out = jnp.maximum(inter, jnp.max(dmat, axis=-1, keepdims=True))
        p = jnp.exp(dmat - m_out)
        w_inter = jnp.exp(inter - m_out)
        yield
        qk = lax.dot_general(qh, kh, nt, preferred_element_type=F32)
        qc = jnp.dot(qh, c_st.astype(BF16), preferred_element_type=F32)
        yield
        s = qk * p
        den = (jnp.sum(s, axis=-1, keepdims=True)
               + w_inter * jnp.sum(q * n_st, axis=-1, keepdims=True))
        sh = s.astype(BF16)
        b_last = b_r[:, CHUNK - 1:CHUNK] if d == 0 else b_r[:, 0:1]
        g_r = b_last - b_r + li_r
        g_c = b_last - b_c + li_c
        m_new = jnp.maximum(b_last + m_st, jnp.max(g_r, axis=-1, keepdims=True))
        decay = jnp.exp(b_last + m_st - m_new)
        kw = k * jnp.exp(g_c - m_new)
        kwh = kw.astype(BF16)
        yield
        sv = jnp.dot(sh, vh, preferred_element_type=F32)
        kv = lax.dot_general(kwh, vh, (((0,), (0,)), ((), ())), preferred_element_type=F32)
        yield
        h_out = (sv + w_inter * qc) / jnp.maximum(jnp.abs(den), jnp.exp(-m_out))
        c_new = decay * c_st + kv
        n_new = decay * n_st + jnp.sum(kw, axis=0, keepdims=True)
        stores.append((h_ref, u, d, h, hs, h_out, c_new, n_new, jnp.broadcast_to(m_new, (1, ML_HEAD_DIM))))
        yield

    chains = [chain(u, d, h, *group)
              for u in range(c_ref.shape[0])
              for d, group in enumerate(((qf, kf, vf, licf, bcf, lirf, brf, hf_ref),
                                         (qb, kb, vb, licb, bcb, lirb, brb, hb_ref)))
              for h in range(ML_HEADS)]
    for _ in range(5):
        for ch in chains:
            next(ch)
    for h_ref, u, d, h, hs, h_out, c_new, n_new, m_new in stores:
        h_ref[u, :, hs] = h_out
        c_ref[u, d, h] = c_new
        n_ref[u, d, h:h + 1, :] = n_new
        m_ref[u, d, h:h + 1, :] = m_new


def _mlstm(proj, lic, bc, lir, br, *, row0, n_seq, seq_len, init=None):
    nc = seq_len // CHUNK
    par = MLSTM_PAR
    g0 = row0 // seq_len // par
    seqs = lambda a: a.reshape(N_TOK // seq_len, seq_len, a.shape[-1])
    rows_of = lambda a: seqs(a.T).transpose(0, 2, 1)

    def chunk_specs(chunk):
        return ([pl.BlockSpec((par, CHUNK, ML_W), lambda b, j, c=c: (g0 + b, chunk(j), c)) for c in range(3)]
                + [pl.BlockSpec((par, CHUNK, LANES), lambda b, j: (g0 + b, chunk(j), 0))] * 2
                + [pl.BlockSpec((par, N_GATES, CHUNK), lambda b, j: (g0 + b, 0, chunk(j)))] * 2)

    fwd, bwd = (lambda j: j), (lambda j: nc - 1 - j)
    st_c = pl.BlockSpec((par, 2, ML_HEADS, ML_HEAD_DIM, ML_HEAD_DIM), lambda b, j: (b, 0, 0, 0, 0))
    st_n = pl.BlockSpec((par, 2, ML_HEADS, ML_HEAD_DIM), lambda b, j: (b, 0, 0, 0))
    in_specs = chunk_specs(fwd) + chunk_specs(bwd)
    args = [seqs(proj)] * 3 + [seqs(lic), seqs(bc), rows_of(lir), rows_of(br)]
    args = args * 2
    if init is not None:
        in_specs += [st_c, st_n, st_n]
        args += list(init)
    h_shape = jax.ShapeDtypeStruct((n_seq, seq_len, ML_W), F32)
    hf, hb, c_st, n_st, m_st = pl.pallas_call(
        functools.partial(_mlstm_body, has_init=init is not None),
        out_shape=(h_shape, h_shape,
                   jax.ShapeDtypeStruct((n_seq, 2, ML_HEADS, ML_HEAD_DIM, ML_HEAD_DIM), F32),
                   jax.ShapeDtypeStruct((n_seq, 2, ML_HEADS, ML_HEAD_DIM), F32),
                   jax.ShapeDtypeStruct((n_seq, 2, ML_HEADS, ML_HEAD_DIM), F32)),
        grid=(n_seq // par, nc),
        in_specs=in_specs,
        out_specs=(pl.BlockSpec((par, CHUNK, ML_W), lambda b, j: (b, fwd(j), 0)),
                   pl.BlockSpec((par, CHUNK, ML_W), lambda b, j: (b, bwd(j), 0)),
                   st_c, st_n, st_n),
        compiler_params=_params(2),
        name="mlstm_scan",
    )(*args)
    return hf.reshape(-1, ML_W), hb.reshape(-1, ML_W), c_st, n_st, m_st


def _ml_post_body(hfc_ref, hfs_ref, hbc_ref, hbs_ref, o_ref, g_ref, out_ref):
    h = _pick_split(hfc_ref, hfs_ref) + _pick_split(hbc_ref, hbs_ref)
    gate = jax.nn.sigmoid(o_ref[...]) * g_ref[...]
    for hd in range(ML_HEADS):
        hs = slice(hd * ML_HEAD_DIM, (hd + 1) * ML_HEAD_DIM)
        x = h[:, hs]
        xc = x - jnp.mean(x, axis=-1, keepdims=True)
        var = jnp.mean(xc * xc, axis=-1, keepdims=True)
        out_ref[:, hs] = (gate[:, hs] * (xc * lax.rsqrt(var + RMS_EPS))).astype(BF16)


def _ml_post(hf, hb, proj, norm_g):
    tm = 512
    blk = pl.BlockSpec((tm, ML_W), lambda i: (i, 0))
    return pl.pallas_call(
        _ml_post_body,
        out_shape=jax.ShapeDtypeStruct((N_TOK, ML_W), BF16),
        grid=(N_TOK // tm,),
        in_specs=(_split_specs(tm, ML_W) * 2
                  + [pl.BlockSpec((tm, ML_W), lambda i: (i, 3)), pl.BlockSpec((1, ML_W), lambda i: (0, 0))]),
        out_specs=blk,
        compiler_params=_params(1),
        name="mlstm_out_norm",
    )(*hf, *hb, proj, norm_g.reshape(1, ML_W))


def _head_rms(x, gain):
    lane_head = lax.broadcasted_iota(jnp.int32, x.shape, 1) // ATT_HEAD_DIM
    sq = x * x
    ms = jnp.zeros_like(x)
    for hd in range(x.shape[1] // ATT_HEAD_DIM):
        sel = lane_head == hd
        ms = jnp.where(sel, jnp.sum(jnp.where(sel, sq, 0.0), axis=-1, keepdims=True), ms)
    return x * lax.rsqrt(ms * (1.0 / ATT_HEAD_DIM) + RMS_EPS) * gain


def _rope(x, cos, sin_signed):
    w = x.shape[1]
    even = lax.broadcasted_iota(jnp.int32, x.shape, 1) % 2 == 0
    partner = jnp.where(even, pltpu.roll(x, w - 1, 1), pltpu.roll(x, 1, 1))
    return x * cos + partner * sin_signed


def _qk_prep_body(q_ref, k_ref, qg_ref, kg_ref, *refs, rope):
    if rope:
        cq_ref, sq_ref, ck_ref, sk_ref, qo_ref, kn_ref, kr_ref = refs
    else:
        qo_ref, kn_ref, kr_ref = refs
    q = _head_rms(q_ref[...], qg_ref[...])
    k = _head_rms(k_ref[...], kg_ref[...])
    kn_ref[...] = k
    if rope:
        q = _rope(q, cq_ref[...], sq_ref[...])
        k = _rope(k, ck_ref[...], sk_ref[...])
    qo_ref[...] = (q * (ATT_HEAD_DIM ** -0.5)).astype(BF16)
    kr_ref[...] = k.astype(BF16)


def _qk_prep(proj, q_gain, k_gain, *, row0, rows, rope_tabs=None):
    tm = 512
    r0 = row0 // tm
    in_specs = [pl.BlockSpec((tm, ATT_W), lambda i: (r0 + i, 4 * ML_W // ATT_W)),
                pl.BlockSpec((tm, KV_W), lambda i: (r0 + i, (4 * ML_W + ATT_W) // KV_W)),
                pl.BlockSpec((1, ATT_W), lambda i: (0, 0)),
                pl.BlockSpec((1, KV_W), lambda i: (0, 0))]
    args = [proj, proj, jnp.tile(q_gain, ATT_HEADS).reshape(1, ATT_W), jnp.tile(k_gain, ATT_KV_HEADS).reshape(1, KV_W)]
    if rope_tabs is not None:
        per_seq = DEC_SEQ // tm
        in_specs += [pl.BlockSpec((tm, ATT_W), lambda i: (i % per_seq, 0))] * 2
        in_specs += [pl.BlockSpec((tm, KV_W), lambda i: (i % per_seq, 0))] * 2
        args += list(rope_tabs)
    return pl.pallas_call(
        functools.partial(_qk_prep_body, rope=rope_tabs is not None),
        out_shape=(jax.ShapeDtypeStruct((rows, ATT_W), BF16), jax.ShapeDtypeStruct((rows, KV_W), F32),
                   jax.ShapeDtypeStruct((rows, KV_W), BF16)),
        grid=(rows // tm,),
        in_specs=in_specs,
        out_specs=(pl.BlockSpec((tm, ATT_W), lambda i: (i, 0)), pl.BlockSpec((tm, KV_W), lambda i: (i, 0)),
                   pl.BlockSpec((tm, KV_W), lambda i: (i, 0))),
        compiler_params=_params(1),
        name="attn_qk_prep",
    )(*args)


def _rope_tables():
    rows = DEC_SEQ // GRID_W
    axis_dim = ATT_HEAD_DIM // 2
    row = jnp.repeat(jnp.arange(rows, dtype=F32), GRID_W)
    col = (jnp.arange(DEC_SEQ) % GRID_W).astype(F32)
    inv = ROPE_BASE ** (-jnp.arange(axis_dim // 2, dtype=F32) * 2.0 / axis_dim)
    ang = jnp.concatenate([row[:, None] * inv, col[:, None] * inv], axis=-1)
    cos = jnp.repeat(jnp.cos(ang), 2, axis=-1)
    sin = jnp.repeat(jnp.sin(ang), 2, axis=-1) * jnp.tile(jnp.array([-1.0, 1.0], F32), axis_dim)
    return (jnp.tile(cos, (1, ATT_HEADS)), jnp.tile(sin, (1, ATT_HEADS)),
            jnp.tile(cos, (1, ATT_KV_HEADS)), jnp.tile(sin, (1, ATT_KV_HEADS)))


def _attn_body(q_ref, k_ref, v_ref, o_ref):
    k, v = k_ref[0, 0], v_ref[0, 0]
    dh = k.shape[0]
    g = q_ref.shape[1] // dh
    pair = 2

    def heads(h0):
        q = jnp.concatenate([q_ref[:, h * dh:(h + 1) * dh] for h in range(h0, h0 + pair)], axis=0)
        s = jnp.dot(q, k, preferred_element_type=F32)
        yield
        e = jnp.exp((s - jnp.max(s, axis=-1, keepdims=True)).astype(BF16))
        yield
        o = jnp.dot(e, v, preferred_element_type=F32)
        o = (o[:, :dh] / o[:, dh:dh + 1]).astype(BF16)
        tq = q_ref.shape[0]
        for j in range(pair):
            o_ref[:, (h0 + j) * dh:(h0 + j + 1) * dh] = o[j * tq:(j + 1) * tq]
        yield

    chains = [heads(h0) for h0 in range(0, g, pair)]
    for step in range(len(chains) + 2):
        for i, chain in enumerate(chains):
            if 0 <= step - i < 3:
                next(chain)


def _attention(q, k_t, v_ones, *, tq):
    n_seq, _, dh, s_len = k_t.shape
    nq = q.shape[0] // n_seq // tq
    qo = pl.BlockSpec((tq, GROUP_W), lambda b, kh, i: (b * nq + i, kh))
    return pl.pallas_call(
        _attn_body,
        out_shape=jax.ShapeDtypeStruct(q.shape, BF16),
        grid=(n_seq, ATT_KV_HEADS, nq),
        in_specs=[qo, pl.BlockSpec((1, 1, dh, s_len), lambda b, kh, i: (b, kh, 0, 0)),
                  pl.BlockSpec((1, 1, s_len, LANES), lambda b, kh, i: (b, kh, 0, 0))],
        out_specs=qo,
        compiler_params=_params(3),
        name="attention",
    )(q, k_t, v_ones)


def _head_major(x, n_seq):
    return x.reshape(n_seq, -1, x.shape[1] // ATT_HEAD_DIM, ATT_HEAD_DIM).transpose(0, 2, 1, 3)


def _with_ones(v):
    pad = jnp.zeros(v.shape[:-1] + (LANES - ATT_HEAD_DIM - 1,), v.dtype)
    return jnp.concatenate([v, jnp.ones(v.shape[:-1] + (1,), v.dtype), pad], axis=-1)


def _hy_filter_body(feat_ref, t_ref, w1_ref, b1_ref, w2_ref, b2_ref, fr_ref, w3f_ref, w3b_ref, dl_ref,
                    hsum_ref, hdiff_ref, nyq_ref, z_ref):
    @pl.when(pl.program_id(0) == 0)
    def _():
        z = jnp.dot(feat_ref[...], w1_ref[...], preferred_element_type=F32, precision=HIGHEST) + b1_ref[...]
        z = jnp.sin(fr_ref[0:1, :] * z)
        z = jnp.dot(z, w2_ref[...], preferred_element_type=F32, precision=HIGHEST) + b2_ref[...]
        z_ref[...] = jnp.sin(fr_ref[1:2, :] * z)

    z = z_ref[...]
    window = jnp.exp(-t_ref[...] * dl_ref[...])
    h_f = jnp.dot(z, w3f_ref[...], preferred_element_type=F32, precision=HIGHEST) * window
    h_b = jnp.dot(z, w3b_ref[...], preferred_element_type=F32, precision=HIGHEST) * window
    row = lax.broadcasted_iota(jnp.int32, h_f.shape, 0)
    h_b = jnp.where(row == 0, 0.0, h_b)
    inv = 1.0 / (jnp.sum(jnp.abs(h_f), axis=0, keepdims=True) + jnp.sum(jnp.abs(h_b), axis=0, keepdims=True))
    h_sum = (h_f + h_b) * inv
    hsum_ref[...] = h_sum
    hdiff_ref[...] = (h_f - h_b) * inv
    nyq_ref[...] = jnp.sum(jnp.where(row % 2 == 0, h_sum, -h_sum), axis=0, keepdims=True)


def _hy_filter(seq_len, w1, b1, w2, b2, w3, sin_freq):
    tc = 256
    fw = w1.shape[1]
    t = jnp.arange(seq_len, dtype=F32)[:, None] / seq_len
    bands = jnp.arange(1, HY_BANDS + 1, dtype=F32)[None, :]
    feat = jnp.concatenate([t, jnp.sin(2.0 * math.pi * bands * t), jnp.cos(2.0 * math.pi * bands * t)], axis=-1)
    feat = jnp.pad(feat, ((0, 0), (0, LANES - HY_EMB)))
    deltas = jnp.abs(jnp.linspace(math.log(HY_TARGET) / HY_LONG_PCT, math.log(HY_TARGET) / HY_SHORT_PCT, D,
                                  dtype=F32)).reshape(1, D)
    pad_w = LANES - fw
    full = lambda shape: pl.BlockSpec(shape, lambda j: (0,) * len(shape))
    return pl.pallas_call(
        _hy_filter_body,
        out_shape=(jax.ShapeDtypeStruct((seq_len, D), F32), jax.ShapeDtypeStruct((seq_len, D), F32),
                   jax.ShapeDtypeStruct((1, D), F32)),
        grid=(D // tc,),
        in_specs=[full((seq_len, LANES)), full((seq_len, 1)), full((LANES, LANES)), full((1, LANES)),
                  full((LANES, LANES)), full((1, LANES)), full((2, LANES)),
                  pl.BlockSpec((LANES, tc), lambda j: (0, j)), pl.BlockSpec((LANES, tc), lambda j: (0, D // tc + j)),
                  pl.BlockSpec((1, tc), lambda j: (0, j))],
        out_specs=(pl.BlockSpec((seq_len, tc), lambda j: (0, j)), pl.BlockSpec((seq_len, tc), lambda j: (0, j)),
                   pl.BlockSpec((1, tc), lambda j: (0, j))),
        scratch_shapes=[pltpu.VMEM((seq_len, LANES), F32)],
        compiler_params=_params(1),
        name="hyena_filter",
    )(feat, t, jnp.pad(w1, ((0, LANES - HY_EMB), (0, pad_w))), jnp.pad(b1, (0, pad_w)).reshape(1, LANES),
      jnp.pad(w2, ((0, pad_w), (0, pad_w))), jnp.pad(b2, (0, pad_w)).reshape(1, LANES),
      jnp.pad(sin_freq, ((0, 0), (0, pad_w))), jnp.pad(w3, ((0, pad_w), (0, 0))), jnp.pad(w3, ((0, pad_w), (0, 0))),
      deltas)


def _dft_matrices(seq_len):
    n = 2 * seq_len
    k = lax.broadcasted_iota(jnp.int32, (seq_len, seq_len), 0)
    t = lax.broadcasted_iota(jnp.int32, (seq_len, seq_len), 1)
    ang = ((k * t) % n).astype(F32) * (2.0 * math.pi / n)
    cr, base = jnp.cos(ang), -jnp.sin(ang)
    ci = jnp.where(k == 0, (1 - 2 * (t % 2)).astype(F32), base)
    cit = jnp.where(t == 0, (1 - 2 * (k % 2)).astype(F32), base)
    return cr.astype(BF16), ci.astype(BF16), cit.astype(BF16)


def _ctx_spectrum_body(cr_ref, ci_ref, hs_ref, hd_ref, nyq_ref, kr_ref, ki_ref):
    kr_ref[...] = jnp.dot(cr_ref[...], hs_ref[...].astype(BF16), preferred_element_type=F32)
    ki = jnp.dot(ci_ref[...], hd_ref[...].astype(BF16), preferred_element_type=F32)
    first = lax.broadcasted_iota(jnp.int32, ki.shape, 0) == 0
    ki_ref[...] = jnp.where(first, nyq_ref[...], ki)


def _ctx_spectrum(mats, h_sum, h_diff, nyq):
    tn = 512
    cr, ci, _ = mats
    mat = pl.BlockSpec((SEQ, SEQ), lambda c: (0, 0))
    chan = pl.BlockSpec((SEQ, tn), lambda c: (0, c))
    shape = jax.ShapeDtypeStruct((SEQ, D), F32)
    return pl.pallas_call(
        _ctx_spectrum_body,
        out_shape=(shape, shape),
        grid=(D // tn,),
        in_specs=[mat, mat, chan, chan, pl.BlockSpec((1, tn), lambda c: (0, c))],
        out_specs=(chan, chan),
        compiler_params=_params(1),
        name="hyena_filter_dft",
    )(cr, ci, h_sum, h_diff, nyq)


def _hy_conv_ctx_body(x0_ref, x1_ref, v_ref, w0_ref, w1_ref, wv_ref, b0_ref, b1_ref, bv_ref, kr_ref, ki_ref,
                      skip_ref, cr_ref, ci_ref, ct_ref, o_ref):
    n = x0_ref.shape[0]
    pos = lax.broadcasted_iota(jnp.int32, x0_ref.shape, 0) % SEQ

    def conv(u_ref, w_ref, b_ref):
        u = u_ref[...]
        prev = jnp.where(pos == 0, 0.0, pltpu.roll(u, 1, 0))
        nxt = jnp.where(pos == SEQ - 1, 0.0, pltpu.roll(u, n - 1, 0))
        return prev * w_ref[0:1, :] + u * w_ref[1:2, :] + nxt * w_ref[2:3, :] + b_ref[...]

    z = conv(v_ref, wv_ref, bv_ref) * conv(x1_ref, w1_ref, b1_ref)
    gated = z * skip_ref[...]
    x0 = conv(x0_ref, w0_ref, b0_ref)
    kr, ki = kr_ref[...], ki_ref[...]
    first = lax.broadcasted_iota(jnp.int32, kr.shape, 0) == 0
    for s in range(n // SEQ):
        rows = slice(s * SEQ, (s + 1) * SEQ)
        zs = z[rows].astype(BF16)
        zr = jnp.dot(cr_ref[...], zs, preferred_element_type=F32)
        zi = jnp.dot(ci_ref[...], zs, preferred_element_type=F32)
        yr = jnp.where(first, 0.5 * zr * kr, zr * kr - zi * ki).astype(BF16)
        yi = jnp.where(first, 0.5 * zi * ki, zr * ki + zi * kr).astype(BF16)
        y = (jnp.dot(cr_ref[...], yr, preferred_element_type=F32)
             + jnp.dot(ct_ref[...], yi, preferred_element_type=F32))
        o_ref[rows, :] = ((y * (1.0 / SEQ) + gated[rows]) * x0[rows]).astype(BF16)


def _hy_conv_ctx(u, conv_w, conv_b, kr, ki, skip, mats, *, seqs_per_step):
    tc = 256
    nb = D // tc
    rows = seqs_per_step * SEQ

    def col(part):
        return [pl.BlockSpec((rows, tc), lambda b, c: (b, part * nb + c)),
                pl.BlockSpec((3, tc), lambda b, c: (0, part * nb + c)),
                pl.BlockSpec((1, tc), lambda b, c: (0, part * nb + c))]

    specs = [col(p) for p in range(3)]
    chan = pl.BlockSpec((SEQ, tc), lambda b, c: (0, c))
    mat = pl.BlockSpec((SEQ, SEQ), lambda b, c: (0, 0))
    cb = conv_b.reshape(1, 3 * D)
    return pl.pallas_call(
        _hy_conv_ctx_body,
        out_shape=jax.ShapeDtypeStruct((N_CTX, D), BF16),
        grid=(BATCH // seqs_per_step, nb),
        in_specs=([s[0] for s in specs] + [s[1] for s in specs] + [s[2] for s in specs]
                  + [chan, chan, pl.BlockSpec((1, tc), lambda b, c: (0, c)), mat, mat, mat]),
        out_specs=pl.BlockSpec((rows, tc), lambda b, c: (b, c)),
        compiler_params=_params(2),
        name="hyena_conv_ctx",
    )(u, u, u, conv_w, conv_w, conv_w, cb, cb, cb, kr, ki, skip.reshape(1, D), *mats)


FFT_A, FFT_R = 64, 64
FFT_M = 2 * FFT_A
FFT_H = FFT_R // 2
assert FFT_A * FFT_R == DEC_SEQ


SUB = 8
FFT_BLK = 2 * FFT_M * SUB


def _fft_consts():
    n = 2 * DEC_SEQ
    th = 2.0 * np.pi * (np.arange(FFT_M)[:, None] + 0.5) * np.arange(FFT_A)[None, :] / FFT_M
    f1 = np.concatenate([np.cos(th), -np.sin(th)], axis=0)
    k = np.arange(FFT_M)[:, None, None] + FFT_M * np.arange(FFT_H)[None, :, None] + 0.5
    ph = 2.0 * np.pi * k * np.arange(FFT_R)[None, None, :] / n
    c, s = np.cos(ph), np.sin(ph)
    g = np.concatenate([np.concatenate([c, s], axis=2), np.concatenate([-s, c], axis=2)], axis=1)
    as_bf16 = lambda m: jnp.asarray(m, dtype=F32).astype(BF16)
    return as_bf16(np.kron(f1, np.eye(SUB))), as_bf16(g)


def _fft_stage1(src_ref, y_ref, f1k):
    for b1 in range(FFT_R // SUB):
        x = jnp.concatenate([src_ref[a * FFT_R + b1 * SUB:a * FFT_R + (b1 + 1) * SUB, :] for a in range(FFT_A)],
                            axis=0)
        y_ref[b1 * FFT_BLK:(b1 + 1) * FFT_BLK, :] = jnp.dot(f1k, x.astype(BF16), preferred_element_type=F32)


def _fft_rows(k1):
    return [b1 * FFT_BLK + ri * FFT_M * SUB + k1 * SUB for ri in range(2) for b1 in range(FFT_R // SUB)]


def _fft_stage2(y_ref, g_ref, k1):
    y = jnp.concatenate([y_ref[r:r + SUB, :] for r in _fft_rows(k1)], axis=0).astype(BF16)
    z = jnp.dot(g_ref[k1], y, preferred_element_type=F32)
    return z[:FFT_H], z[FFT_H:]


def _hy_spectrum_body(hs_ref, hd_ref, f1_ref, g_ref, kr_ref, ki_ref, h_ref, y_ref):
    tc = hs_ref.shape[1]
    h_ref[:, :tc] = hs_ref[...]
    h_ref[:, tc:] = hd_ref[...]
    _fft_stage1(h_ref, y_ref, f1_ref[...])
    for k1 in range(FFT_M):
        zr, zi = _fft_stage2(y_ref, g_ref, k1)
        kr_ref[k1 * FFT_H:(k1 + 1) * FFT_H, :] = zr[:, :tc]
        ki_ref[k1 * FFT_H:(k1 + 1) * FFT_H, :] = zi[:, tc:]


def _hy_spectrum(h_sum, h_diff, consts):
    tc = 128
    f1k, g = consts
    blk = pl.BlockSpec((DEC_SEQ, tc), lambda c: (0, c))
    shape = jax.ShapeDtypeStruct((DEC_SEQ, D), F32)
    return pl.pallas_call(
        _hy_spectrum_body,
        out_shape=(shape, shape),
        grid=(D // tc,),
        in_specs=[blk, blk, pl.BlockSpec(f1k.shape, lambda c: (0, 0)), pl.BlockSpec(g.shape, lambda c: (0, 0, 0))],
        out_specs=(blk, blk),
        scratch_shapes=[pltpu.VMEM((DEC_SEQ, 2 * tc), F32), pltpu.VMEM((FFT_R * 2 * FFT_M, 2 * tc), F32)],
        compiler_params=_params(1),
        name="hyena_filter_fft",
    )(h_sum, h_diff, f1k, g)


def _hy_conv_fft_body(x0_ref, x1_ref, v_ref, w0_ref, w1_ref, wv_ref, b0_ref, b1_ref, bv_ref, kr_ref, ki_ref,
                      skip_ref, f1_ref, g_ref, o_ref, z_ref, y_ref, t_ref):
    rows = 512
    tn = (((0,), (0,)), ((), ()))

    def conv(u_ref, w_ref, b_ref, r):
        u = u_ref[r:r + rows, :]
        row = lax.broadcasted_iota(jnp.int32, u.shape, 0)
        before = u_ref[r - 1:r, :] if r > 0 else jnp.zeros_like(u[0:1])
        after = u_ref[r + rows:r + rows + 1, :] if r + rows < DEC_SEQ else jnp.zeros_like(u[0:1])
        prev = jnp.where(row == 0, before, pltpu.roll(u, 1, 0))
        nxt = jnp.where(row == rows - 1, after, pltpu.roll(u, rows - 1, 0))
        return prev * w_ref[0:1, :] + u * w_ref[1:2, :] + nxt * w_ref[2:3, :] + b_ref[...]

    for r in range(0, DEC_SEQ, rows):
        z_ref[r:r + rows, :] = conv(v_ref, wv_ref, bv_ref, r) * conv(x1_ref, w1_ref, b1_ref, r)
    f1k = f1_ref[...]
    _fft_stage1(z_ref, y_ref, f1k)
    for k1 in range(FFT_M):
        zr, zi = _fft_stage2(y_ref, g_ref, k1)
        kr = kr_ref[k1 * FFT_H:(k1 + 1) * FFT_H, :]
        ki = ki_ref[k1 * FFT_H:(k1 + 1) * FFT_H, :]
        p = jnp.concatenate([zr * kr - zi * ki, zr * ki + zi * kr], axis=0).astype(BF16)
        u = lax.dot_general(g_ref[k1], p, tn, preferred_element_type=F32)
        for j, r in enumerate(_fft_rows(k1)):
            y_ref[r:r + SUB, :] = u[j * SUB:(j + 1) * SUB]
    for b1 in range(FFT_R // SUB):
        blk = y_ref[b1 * FFT_BLK:(b1 + 1) * FFT_BLK, :].astype(BF16)
        yb = lax.dot_general(f1k, blk, tn, preferred_element_type=F32)
        for a in range(FFT_A):
            t_ref[a * FFT_R + b1 * SUB:a * FFT_R + (b1 + 1) * SUB, :] = yb[a * SUB:(a + 1) * SUB]
    for r in range(0, DEC_SEQ, rows):
        y = t_ref[r:r + rows, :] * (1.0 / DEC_SEQ) + z_ref[r:r + rows, :] * skip_ref[...]
        o_ref[r:r + rows, :] = (y * conv(x0_ref, w0_ref, b0_ref, r)).astype(BF16)


def _hy_conv_fft(u, conv_w, conv_b, kr, ki, skip, consts, *, row0, n_seq):
    tc = 128
    nb = D // tc
    r0 = row0 // DEC_SEQ

    def col(part):
        return [pl.BlockSpec((DEC_SEQ, tc), lambda b, c: (r0 + b, part * nb + c)),
                pl.BlockSpec((3, tc), lambda b, c: (0, part * nb + c)),
                pl.BlockSpec((1, tc), lambda b, c: (0, part * nb + c))]

    specs = [col(p) for p in range(3)]
    chan = pl.BlockSpec((DEC_SEQ, tc), lambda b, c: (0, c))
    const = lambda m: pl.BlockSpec(m.shape, lambda b, c: (0,) * m.ndim)
    cb = conv_b.reshape(1, 3 * D)
    return pl.pallas_call(
        _hy_conv_fft_body,
        out_shape=jax.ShapeDtypeStruct((n_seq * DEC_SEQ, D), BF16),
        grid=(n_seq, nb),
        in_specs=([s[0] for s in specs] + [s[1] for s in specs] + [s[2] for s in specs]
                  + [chan, chan, pl.BlockSpec((1, tc), lambda b, c: (0, c))] + [const(m) for m in consts]),
        out_specs=pl.BlockSpec((DEC_SEQ, tc), lambda b, c: (b, c)),
        scratch_shapes=[pltpu.VMEM((DEC_SEQ, tc), F32), pltpu.VMEM((FFT_R * 2 * FFT_M, tc), F32),
                        pltpu.VMEM((DEC_SEQ, tc), F32)],
        compiler_params=_params(2, 56 * 1024 * 1024),
        name="hyena_conv_fft",
    )(u, u, u, conv_w, conv_w, conv_w, cb, cb, cb, kr, ki, skip.reshape(1, D), *consts)


ROW_TILE = D // LANES
ROUTER_TM = 512
EXPERT_TM = 512
N_SLOTS = 2 * N_TOK + N_EXPERTS * EXPERT_TM
N_SLOT_TILES = N_SLOTS // EXPERT_TM
INFO_E1, INFO_E2, INFO_R1, INFO_R2, INFO_W1, INFO_W2 = range(6)


def _to_row_tiles(ref, x):
    rows = x.shape[0]
    for j in range(ROW_TILE):
        ref[pl.ds(j, rows, stride=ROW_TILE), :] = x[:, j * LANES:(j + 1) * LANES]


def _from_row_tiles(ref, rows):
    return jnp.concatenate([ref[pl.ds(j, rows, stride=ROW_TILE), :] for j in range(ROW_TILE)], axis=-1)


def _router_body(x_ref, sh_ref, sc_ref, w_ref, info_ref, incl_ref, cnt_ref):
    @pl.when(pl.program_id(0) == 0)
    def _():
        cnt_ref[...] = jnp.zeros_like(cnt_ref)

    h = _modulate(x_ref, sh_ref, sc_ref)
    h_hi, w = h.astype(BF16), w_ref[...]
    h_lo, w_hi = (h - h_hi.astype(F32)).astype(BF16), w.astype(BF16)
    w_lo = (w - w_hi.astype(F32)).astype(BF16)
    logits = (jnp.dot(h_hi, w_hi, preferred_element_type=F32) + jnp.dot(h_lo, w_hi, preferred_element_type=F32)
              + jnp.dot(h_hi, w_lo, preferred_element_type=F32))
    lane = lax.broadcasted_iota(jnp.int32, logits.shape, 1).astype(F32)
    logits = jnp.where(lane < N_EXPERTS, logits, -jnp.inf)
    e = jnp.exp(logits - jnp.max(logits, axis=-1, keepdims=True))
    p = e / jnp.sum(e, axis=-1, keepdims=True)
    p1 = jnp.max(p, axis=-1, keepdims=True)
    i1 = jnp.min(jnp.where(p == p1, lane, float(LANES)), axis=-1, keepdims=True)
    rest = jnp.where(lane == i1, -1.0, p)
    p2 = jnp.max(rest, axis=-1, keepdims=True)
    i2 = jnp.min(jnp.where(rest == p2, lane, float(LANES)), axis=-1, keepdims=True)
    total = p1 + p2
    chosen = jnp.where((lane == i1) | (lane == i2), 1.0, 0.0)
    tm = chosen.shape[0]
    earlier = (lax.broadcasted_iota(jnp.int32, (tm, tm), 1) < lax.broadcasted_iota(jnp.int32, (tm, tm), 0))
    rank = jnp.dot(earlier.astype(BF16), chosen.astype(BF16), preferred_element_type=F32) + cnt_ref[...]
    r1 = jnp.sum(jnp.where(lane == i1, rank, 0.0), axis=-1, keepdims=True)
    r2 = jnp.sum(jnp.where(lane == i2, rank, 0.0), axis=-1, keepdims=True)
    cnt_ref[...] += jnp.sum(chosen, axis=0, keepdims=True)
    incl_ref[0] = jnp.broadcast_to(cnt_ref[...], incl_ref.shape[1:])
    info = jnp.zeros_like(p)
    for col, val in ((INFO_E1, i1), (INFO_E2, i2), (INFO_R1, r1), (INFO_R2, r2),
                     (INFO_W1, p1 / total), (INFO_W2, p2 / total)):
        info = jnp.where(lane == col, val, info)
    info_ref[...] = info


def _router(x, sh, sc, w_router):
    tm = ROUTER_TM
    return pl.pallas_call(
        _router_body,
        out_shape=(jax.ShapeDtypeStruct((N_TOK, LANES), F32),
                   jax.ShapeDtypeStruct((N_TOK // tm, 8, LANES), F32)),
        grid=(N_TOK // tm,),
        in_specs=_mod_specs(tm, 0) + [pl.BlockSpec((D, LANES), lambda i: (0, 0))],
        out_specs=(pl.BlockSpec((tm, LANES), lambda i: (i, 0)), pl.BlockSpec((1, 8, LANES), lambda i: (i, 0, 0))),
        scratch_shapes=[pltpu.VMEM((1, LANES), F32)],
        compiler_params=_params(1),
        name="moe_router",
    )(x, sh, sc, jnp.pad(w_router, ((0, 0), (0, LANES - N_EXPERTS))))


def _row_tile(ref, row):
    return ref.at[pl.ds(pl.multiple_of(row * ROW_TILE, ROW_TILE), ROW_TILE)]


def _dispatch_body(pos1_ref, pos2_ref, x_ref, zeros_hbm, xs_hbm, rows_ref, sem, *, tm):
    del zeros_hbm
    base = pl.program_id(0) * tm
    _to_row_tiles(rows_ref, x_ref[...])

    def copies(r):
        src = _row_tile(rows_ref, r)
        return (pltpu.make_async_copy(src, _row_tile(xs_hbm, pos1_ref[base + r]), sem),
                pltpu.make_async_copy(src, _row_tile(xs_hbm, pos2_ref[base + r]), sem))

    def issue(r, carry):
        for queue, cp in enumerate(copies(r)):
            cp.start(priority=queue)
        return carry

    lax.fori_loop(0, tm, issue, 0, unroll=8)
    for _ in range(2):
        pltpu.make_async_copy(rows_ref, xs_hbm.at[pl.ds(0, tm * ROW_TILE)], sem).wait()


def _dispatch(pos1, pos2, x):
    tm = 512
    return pl.pallas_call(
        functools.partial(_dispatch_body, tm=tm),
        out_shape=jax.ShapeDtypeStruct((N_SLOTS * ROW_TILE, LANES), F32),
        grid_spec=pltpu.PrefetchScalarGridSpec(
            num_scalar_prefetch=2, grid=(N_TOK // tm,),
            in_specs=[pl.BlockSpec((tm, D), lambda i, *_: (i, 0)), pl.BlockSpec(memory_space=pl.ANY)],
            out_specs=pl.BlockSpec(memory_space=pl.ANY),
            scratch_shapes=[pltpu.VMEM((tm * ROW_TILE, LANES), F32), pltpu.SemaphoreType.DMA(())]),
        input_output_aliases={3: 0},
        compiler_params=_params(1),
        name="moe_dispatch",
    )(pos1, pos2, x, jnp.zeros((N_SLOTS * ROW_TILE, LANES), F32))


def _new_expert(eid_ref, nv_ref, t):
    tt = jnp.minimum(t, nv_ref[0] - 1)
    return (t == 0) | (eid_ref[tt] != eid_ref[jnp.maximum(tt - 1, 0)])


def _expert_swiglu_body(eid_ref, b1_ref, b2_ref, nv_ref, xs_ref, sh_ref, sc_ref, wg_ref, wu_ref, o_ref,
                        wgb_ref, wub_ref, *, fc):
    t = pl.program_id(1)

    @pl.when(_new_expert(eid_ref, nv_ref, t))
    def _():
        wgb_ref[...] = wg_ref[0].astype(BF16)
        wub_ref[...] = wu_ref[0].astype(BF16)

    @pl.when(t < nv_ref[0])
    def _():
        tm = o_ref.shape[0]
        x = _from_row_tiles(xs_ref, tm)
        slot = t * tm + lax.broadcasted_iota(jnp.int32, (tm, 1), 0)
        in1, in2 = slot >= b1_ref[t], slot >= b2_ref[t]
        sc = jnp.where(in2, sc_ref[2], jnp.where(in1, sc_ref[1], sc_ref[0]))
        sh = jnp.where(in2, sh_ref[2], jnp.where(in1, sh_ref[1], sh_ref[0]))
        h = (x * (1.0 + sc) + sh).astype(BF16)
        for c in range(o_ref.shape[1] // fc):
            cs = slice(c * fc, (c + 1) * fc)
            g = jnp.dot(h, wgb_ref[:, cs], preferred_element_type=F32)
            u = jnp.dot(h, wub_ref[:, cs], preferred_element_type=F32)
            o_ref[:, cs] = (g * jax.nn.sigmoid(g) * u).astype(BF16)

    @pl.when(t >= nv_ref[0])
    def _():
        o_ref[...] = jnp.zeros_like(o_ref)


def _expert_swiglu(meta, xs_rt, sh, sc, w_gate, w_up, e0):
    eid, b1, b2, nv = meta
    tm, f = EXPERT_TM, w_gate.shape[2]
    fh = f // 2

    def tile(t, eid, b1, b2, nv):
        return jnp.minimum(t, nv[0] - 1)

    w_spec = pl.BlockSpec((1, D, fh), lambda p, t, eid, b1, b2, nv: (e0 + eid[tile(t, eid, b1, b2, nv)], 0, p))
    mod = pl.BlockSpec((8, 1, D), lambda p, t, *_: (0, 0, 0))
    return pl.pallas_call(
        functools.partial(_expert_swiglu_body, fc=256),
        out_shape=jax.ShapeDtypeStruct((N_SLOTS, f), BF16),
        grid_spec=pltpu.PrefetchScalarGridSpec(
            num_scalar_prefetch=4, grid=(2, N_SLOT_TILES),
            in_specs=[pl.BlockSpec((tm * ROW_TILE, LANES), lambda p, t, *m: (tile(t, *m), 0)), mod, mod,
                      w_spec, w_spec],
            out_specs=pl.BlockSpec((tm, fh), lambda p, t, *m: (t, p)),
            scratch_shapes=[pltpu.VMEM((D, fh), BF16), pltpu.VMEM((D, fh), BF16)]),
        compiler_params=_params(2, 56 * 1024 * 1024),
        name="moe_swiglu",
    )(eid, b1, b2, nv, xs_rt, sh, sc, w_gate, w_up)


def _expert_down_body(eid_ref, nv_ref, a_ref, w_ref, y_ref, wb_ref):
    t = pl.program_id(0)

    @pl.when(_new_expert(eid_ref, nv_ref, t))
    def _():
        wb_ref[...] = w_ref[0].astype(BF16)

    @pl.when(t < nv_ref[0])
    def _():
        _to_row_tiles(y_ref, jnp.dot(a_ref[...], wb_ref[...], preferred_element_type=F32))

    @pl.when(t >= nv_ref[0])
    def _():
        y_ref[...] = jnp.zeros_like(y_ref)


def _expert_down(meta, act, w_down, e0):
    eid, _, _, nv = meta
    tm, f = EXPERT_TM, act.shape[1]

    def tile(t, eid, nv):
        return jnp.minimum(t, nv[0] - 1)

    return pl.pallas_call(
        _expert_down_body,
        out_shape=jax.ShapeDtypeStruct((N_SLOTS * ROW_TILE, LANES), F32),
        grid_spec=pltpu.PrefetchScalarGridSpec(
            num_scalar_prefetch=2, grid=(N_SLOT_TILES,),
            in_specs=[pl.BlockSpec((tm, f), lambda t, *m: (tile(t, *m), 0)),
                      pl.BlockSpec((1, f, D), lambda t, eid, nv: (e0 + eid[tile(t, eid, nv)], 0, 0))],
            out_specs=pl.BlockSpec((tm * ROW_TILE, LANES), lambda t, *m: (t, 0)),
            scratch_shapes=[pltpu.VMEM((f, D), BF16)]),
        compiler_params=_params(1, 58 * 1024 * 1024),
        name="moe_down",
    )(eid, nv, act, w_down)


def _combine_body(pos1_ref, pos2_ref, y_hbm, info_ref, x_ref, g_ref, lng_ref, lnb_ref, o_ref, y1_ref, y2_ref, sem,
                  *, tm):
    base = pl.program_id(0) * tm

    def copies(r):
        return (pltpu.make_async_copy(_row_tile(y_hbm, pos1_ref[base + r]), _row_tile(y1_ref, r), sem),
                pltpu.make_async_copy(_row_tile(y_hbm, pos2_ref[base + r]), _row_tile(y2_ref, r), sem))

    def issue(r, carry):
        for queue, cp in enumerate(copies(r)):
            cp.start(priority=queue)
        return carry

    lax.fori_loop(0, tm, issue, 0, unroll=8)
    for y_ref in (y1_ref, y2_ref):
        pltpu.make_async_copy(y_hbm.at[pl.ds(0, tm * ROW_TILE)], y_ref, sem).wait()
    info = info_ref[...]
    ffn = (info[:, INFO_W1:INFO_W1 + 1] * _from_row_tiles(y1_ref, tm)
           + info[:, INFO_W2:INFO_W2 + 1] * _from_row_tiles(y2_ref, tm))
    y = ALPHA * x_ref[...] + g_ref[0] * ffn
    mu = jnp.mean(y, axis=-1, keepdims=True)
    yc = y - mu
    var = jnp.mean(yc * yc, axis=-1, keepdims=True)
    o_ref[...] = yc * lax.rsqrt(var + LN_EPS) * lng_ref[...] + lnb_ref[...]


def _combine(pos1, pos2, y_rt, info, x, gate, ln_g, ln_b):
    tm = 512
    return pl.pallas_call(
        functools.partial(_combine_body, tm=tm),
        out_shape=jax.ShapeDtypeStruct((N_TOK, D), F32),
        grid_spec=pltpu.PrefetchScalarGridSpec(
            num_scalar_prefetch=2, grid=(N_TOK // tm,),
            in_specs=[pl.BlockSpec(memory_space=pl.ANY),
                      pl.BlockSpec((tm, LANES), lambda i, *_: (i, 0)),
                      pl.BlockSpec((tm, D), lambda i, *_: (i, 0)),
                      pl.BlockSpec((1, 1, D), lambda i, *_: (_group_of_row(i * tm), 0, 0)),
                      pl.BlockSpec((1, D), lambda i, *_: (0, 0)),
                      pl.BlockSpec((1, D), lambda i, *_: (0, 0))],
            out_specs=pl.BlockSpec((tm, D), lambda i, *_: (i, 0)),
            scratch_shapes=[pltpu.VMEM((tm * ROW_TILE, LANES), F32), pltpu.VMEM((tm * ROW_TILE, LANES), F32),
                            pltpu.SemaphoreType.DMA(())]),
        compiler_params=_params(1),
        name="moe_combine",
    )(pos1, pos2, y_rt, info, x, gate, ln_g.reshape(1, D), ln_b.reshape(1, D))


def _slot_plan(info, incl):
    row = lambda n_rows: incl[n_rows // ROUTER_TM - 1, 0, :N_EXPERTS].astype(jnp.int32)
    count = row(N_TOK)
    padded = (count + EXPERT_TM - 1) // EXPERT_TM * EXPERT_TM
    end = jnp.cumsum(padded)
    start = end - padded
    tile_row = jnp.arange(N_SLOT_TILES, dtype=jnp.int32) * EXPERT_TM
    eid = jnp.minimum(jnp.sum(tile_row[:, None] >= end[None, :], axis=1), N_EXPERTS - 1).astype(jnp.int32)
    b1 = (start + row(N_CTX))[eid]
    b2 = (start + row(N_CTX + DEC_SEQ))[eid]
    nv = (end[-1:] // EXPERT_TM).astype(jnp.int32)
    experts = jnp.arange(N_EXPERTS, dtype=jnp.int32)
    start_of = lambda col: jnp.sum(jnp.where(info[:, col:col + 1].astype(jnp.int32) == experts, start, 0), axis=1)
    pos1 = start_of(INFO_E1) + info[:, INFO_R1].astype(jnp.int32)
    pos2 = start_of(INFO_E2) + info[:, INFO_R2].astype(jnp.int32)
    return pos1, pos2, (eid, b1, b2, nv)


def _even_mixer(x, sh, sc, gate, ln_g, ln_b, w_in, b_igate, b_fgate, ml_norm_g, q_norm_g, k_norm_g, w_out,
                st_c, st_n, st_m, cache_k, cache_v, rope_tabs):
    splits = (4 * ML_W, 4 * ML_W + N_GATES)
    w_main = jnp.concatenate([w_in[:, :splits[0]], w_in[:, splits[1]:]], axis=1).astype(BF16)
    proj = _mod_matmul(x, sh, sc, [w_main[None]], tm=1024, tn=MAIN_W // 2, out_dtype=F32, name="even_in_proj")[0]
    b_gate = jnp.stack([b_igate, b_fgate], axis=1).reshape(N_GATES)
    lic, bc, lir, br = _gates(x, sh, sc, w_in[:, splits[0]:splits[1]], b_gate)

    hf_c, hb_c, new_c, new_n, new_m = _mlstm(proj, lic, bc, lir, br, row0=0, n_seq=BATCH, seq_len=SEQ)
    init = (st_c, st_n, jnp.broadcast_to(st_m[..., None], st_n.shape))
    hf_s, hb_s, _, _, _ = _mlstm(proj, lic, bc, lir, br, row0=N_CTX, n_seq=DEC_BATCH, seq_len=DEC_SEQ, init=init)
    ml = _ml_post((hf_c, hf_s), (hb_c, hb_s), proj, ml_norm_g)

    q_c, kn_c, kb_c = _qk_prep(proj, q_norm_g, k_norm_g, row0=0, rows=N_CTX)
    q_s, _, kb_s = _qk_prep(proj, q_norm_g, k_norm_g, row0=N_CTX, rows=N_LAT, rope_tabs=rope_tabs)
    v_all = proj[:, MAIN_W - KV_W:]
    v_c, v_s = v_all[:N_CTX], v_all[N_CTX:]
    att_c = _attention(q_c, _head_major(kb_c, BATCH).swapaxes(2, 3),
                       _with_ones(_head_major(v_c.astype(BF16), BATCH)), tq=SEQ)
    k_lat = jnp.concatenate([kb_s.reshape(DEC_BATCH, DEC_SEQ, KV_W),
                             cache_k.reshape(DEC_BATCH, PAST_LEN, KV_W).astype(BF16)], axis=1)
    v_lat = jnp.concatenate([v_s.reshape(DEC_BATCH, DEC_SEQ, KV_W).astype(BF16),
                             cache_v.reshape(DEC_BATCH, PAST_LEN, KV_W).astype(BF16)], axis=1)
    att_s = _attention(q_s, _head_major(k_lat.reshape(-1, KV_W), DEC_BATCH).swapaxes(2, 3),
                       _with_ones(_head_major(v_lat.reshape(-1, KV_W), DEC_BATCH)), tq=256)

    x = _proj_res_ln([ml, (att_c, att_s)], w_out.astype(BF16), x, gate, ln_g, ln_b, tm=512, name="even_out_proj")
    new_k = kn_c.reshape(BATCH, SEQ, ATT_KV_HEADS, ATT_HEAD_DIM)
    new_v = v_c.reshape(BATCH, SEQ, ATT_KV_HEADS, ATT_HEAD_DIM)
    return x, new_k, new_v, new_c, new_n, new_m[..., 0]


def _hyena_mixer(x, sh, sc, gate, ln_g, ln_b, w_in, conv_w, conv_b, w1, b1, w2, b2, w3, sin_freq, skip, w_out, dft):
    u = _mod_matmul(x, sh, sc, [w_in.astype(BF16)[None]], tm=1024, tn=1536, out_dtype=F32, name="hyena_in_proj")[0]
    mats, consts = dft
    h_sum, h_diff, nyq = _hy_filter(SEQ, w1, b1, w2, b2, w3, sin_freq)
    kr, ki = _ctx_spectrum(mats, h_sum, h_diff, nyq)
    y_c = _hy_conv_ctx(u, conv_w, conv_b, kr, ki, skip, mats, seqs_per_step=8)
    h_sum, h_diff, _ = _hy_filter(DEC_SEQ, w1, b1, w2, b2, w3, sin_freq)
    kr, ki = _hy_spectrum(h_sum, h_diff, consts)
    y_s = _hy_conv_fft(u, conv_w, conv_b, kr, ki, skip, consts, row0=N_CTX, n_seq=DEC_BATCH)
    return _proj_res_ln([(y_c, y_s)], w_out.astype(BF16), x, gate, ln_g, ln_b, tm=512, name="hyena_out_proj")


def _dense_ffn(x, sh, sc, gate, ln_g, ln_b, w_gate, w_up, w_down):
    act = _mod_matmul(x, sh, sc, [w_gate.astype(BF16)[None], w_up.astype(BF16)[None]],
                      tm=1024, tn=D_FF // 2, out_dtype=BF16, name="ffn_swiglu")
    return _proj_res_ln([act[0]], w_down.astype(BF16), x, gate, ln_g, ln_b, tm=512, name="ffn_down")


def _moe_ffn(x, sh, sc, gate, ln_g, ln_b, w_router, w_gate, w_up, w_down, e0):
    info, incl = _router(x, sh, sc, w_router)
    pos1, pos2, meta = _slot_plan(info, incl)
    xs_rt = _dispatch(pos1, pos2, x)
    act = _expert_swiglu(meta, xs_rt, sh, sc, w_gate, w_up, e0)
    y_rt = _expert_down(meta, act, w_down, e0)
    return _combine(pos1, pos2, y_rt, info, x, gate, ln_g, ln_b)


def kernel(x_prompt, x_sample, cache_attn_k, cache_attn_v, state_mlstm_C, state_mlstm_n, state_mlstm_m, c, c_ctx, w_ada, b_ada, ln_g, ln_b, w_in_even, b_igate, b_fgate, ml_norm_g, q_norm_g, k_norm_g, w_out_even, w_ffn_gate, w_ffn_up, w_ffn_down, w_in_hy, hy_conv_w, hy_conv_b, hy_filt_w1, hy_filt_b1, hy_filt_w2, hy_filt_b2, hy_filt_w3, hy_sin_freq, hy_skip, w_out_hy, w_router, w_moe_gate, w_moe_up, w_moe_down):
    x = jnp.concatenate([x_prompt.reshape(N_CTX, D), x_sample.reshape(N_LAT, D)])
    cvec = jnp.concatenate([c_ctx[None], c, jnp.zeros((8 - 1 - DEC_BATCH, D), F32)])
    mods = _ada(cvec, w_ada, b_ada)
    rope_tabs = _rope_tables()
    dft = (_dft_matrices(SEQ), _fft_consts())
    moe_w = [w.reshape((-1,) + w.shape[2:]) for w in (w_moe_gate, w_moe_up, w_moe_down)]
    new_k, new_v, new_c, new_n, new_m = [], [], [], [], []
    for layer in range(DEPTH):
        sh1, sc1, g1, sh2, sc2, g2 = (mods[layer, :, i * D:(i + 1) * D].reshape(8, 1, D) for i in range(6))
        i = layer // 2
        if layer % 2 == 0:
            x, k_c, v_c, st_c, st_n, st_m = _even_mixer(
                x, sh1, sc1, g1, ln_g[layer, 0], ln_b[layer, 0], w_in_even[i], b_igate[i], b_fgate[i], ml_norm_g[i],
                q_norm_g[i], k_norm_g[i], w_out_even[i], state_mlstm_C[:, i], state_mlstm_n[:, i], state_mlstm_m[:, i],
                cache_attn_k[:, i], cache_attn_v[:, i], rope_tabs)
            new_k.append(k_c)
            new_v.append(v_c)
            new_c.append(st_c)
            new_n.append(st_n)
            new_m.append(st_m)
            x = _dense_ffn(x, sh2, sc2, g2, ln_g[layer, 1], ln_b[layer, 1], w_ffn_gate[i], w_ffn_up[i], w_ffn_down[i])
        else:
            x = _hyena_mixer(x, sh1, sc1, g1, ln_g[layer, 0], ln_b[layer, 0], w_in_hy[i], hy_conv_w[i], hy_conv_b[i],
                             hy_filt_w1[i], hy_filt_b1[i], hy_filt_w2[i], hy_filt_b2[i], hy_filt_w3[i], hy_sin_freq[i],
                             hy_skip[i], w_out_hy[i], dft)
            x = _moe_ffn(x, sh2, sc2, g2, ln_g[layer, 1], ln_b[layer, 1], w_router[i], *moe_w, i * N_EXPERTS)
    return (x[:N_CTX].reshape(BATCH, SEQ, D), x[N_CTX:].reshape(DEC_BATCH, DEC_SEQ, D),
            jnp.stack(new_k, axis=1), jnp.stack(new_v, axis=1), jnp.stack(new_c, axis=1),
            jnp.stack(new_n, axis=1), jnp.stack(new_m, axis=1))
```

```python
import functools
import math

import jax
import jax.numpy as jnp
import numpy as np
from jax import lax
from jax.experimental import pallas as pl
from jax.experimental.pallas import tpu as pltpu

F32 = jnp.float32
BF16 = jnp.bfloat16
HIGHEST = lax.Precision.HIGHEST

D = 1024
BATCH, SEQ = 32, 256
DEC_BATCH, DEC_SEQ = 2, 4096
DEPTH = 4
PAST_LEN = 256
GRID_W = 64
N_CTX = BATCH * SEQ
N_LAT = DEC_BATCH * DEC_SEQ
N_TOK = N_CTX + N_LAT

ML_HEADS, ML_HEAD_DIM = 4, 128
ML_W = ML_HEADS * ML_HEAD_DIM
CHUNK = 128
MLSTM_PAR = 2
ATT_HEADS, ATT_KV_HEADS, ATT_HEAD_DIM = 8, 2, 64
ATT_GROUP = ATT_HEADS // ATT_KV_HEADS
ATT_W = ATT_HEADS * ATT_HEAD_DIM
KV_W = ATT_KV_HEADS * ATT_HEAD_DIM
GROUP_W = ATT_GROUP * ATT_HEAD_DIM
ROPE_BASE = 10000.0
N_GATES = 4 * ML_HEADS
MAIN_W = 4 * ML_W + ATT_W + 2 * KV_W

HY_EMB = 33
HY_BANDS = (HY_EMB - 1) // 2
HY_TARGET, HY_SHORT_PCT, HY_LONG_PCT = 1e-2, 0.3, 1.5
D_FF = 2816
N_EXPERTS = 8
MOE_D_FF = 3584
ALPHA = (2 * DEPTH) ** 0.25
LN_EPS = 1e-5
RMS_EPS = 1e-6

LANES = 128
VMEM_LIMIT = 48 * 1024 * 1024


def _params(n_axes, vmem=VMEM_LIMIT):
    return pltpu.CompilerParams(dimension_semantics=("arbitrary",) * n_axes, vmem_limit_bytes=vmem)


def _group_of_row(r):
    return jnp.where(r < N_CTX, 0, 1 + (r - N_CTX) // DEC_SEQ)


def _modulate(x_ref, sh_ref, sc_ref):
    return x_ref[...] * (1.0 + sc_ref[0]) + sh_ref[0]


def _mod_specs(tm, row_axis):
    def rows(*ids):
        return (ids[row_axis], 0)

    def grp(*ids):
        return (_group_of_row(ids[row_axis] * tm), 0, 0)

    return [pl.BlockSpec((tm, D), rows), pl.BlockSpec((1, 1, D), grp), pl.BlockSpec((1, 1, D), grp)]


def _ada_body(c_ref, w_ref, b_ref, o_ref):
    c = c_ref[...]
    s = c * jax.nn.sigmoid(c)
    o_ref[0] = jnp.dot(s, w_ref[0], preferred_element_type=F32, precision=HIGHEST) + b_ref[0]


def _ada(cvec, w_ada, b_ada):
    tn = 1536
    return pl.pallas_call(
        _ada_body,
        out_shape=jax.ShapeDtypeStruct((DEPTH, 8, 6 * D), F32),
        grid=(DEPTH, 6 * D // tn),
        in_specs=[pl.BlockSpec((8, D), lambda l, j: (0, 0)),
                  pl.BlockSpec((1, D, tn), lambda l, j: (l, 0, j)),
                  pl.BlockSpec((1, 1, tn), lambda l, j: (l, 0, j))],
        out_specs=pl.BlockSpec((1, 8, tn), lambda l, j: (l, 0, j)),
        compiler_params=_params(2),
        name="ada_modulation",
    )(cvec, w_ada, b_ada.reshape(DEPTH, 1, 6 * D))


def _mod_mm_body(x_ref, sh_ref, sc_ref, *refs, n_w):
    w_refs, o_ref, h_ref = refs[:n_w], refs[n_w], refs[n_w + 1]

    @pl.when(pl.program_id(2) == 0)
    def _():
        h_ref[...] = _modulate(x_ref, sh_ref, sc_ref).astype(BF16)

    h = h_ref[...]
    if n_w == 1:
        o = jnp.dot(h, w_refs[0][0], preferred_element_type=F32)
    else:
        g = jnp.dot(h, w_refs[0][0], preferred_element_type=F32)
        u = jnp.dot(h, w_refs[1][0], preferred_element_type=F32)
        o = g * jax.nn.sigmoid(g) * u
    o_ref[0] = o.astype(o_ref.dtype)


def _mod_matmul(x, sh, sc, ws, *, tm, tn, out_dtype, name):
    n_e, _, f = ws[0].shape
    return pl.pallas_call(
        functools.partial(_mod_mm_body, n_w=len(ws)),
        out_shape=jax.ShapeDtypeStruct((n_e, N_TOK, f), out_dtype),
        grid=(n_e, N_TOK // tm, f // tn),
        in_specs=_mod_specs(tm, 1) + [pl.BlockSpec((1, D, tn), lambda e, i, j: (e, 0, j)) for _ in ws],
        out_specs=pl.BlockSpec((1, tm, tn), lambda e, i, j: (e, i, j)),
        scratch_shapes=[pltpu.VMEM((tm, D), BF16)],
        compiler_params=_params(3),
        name=name,
    )(x, sh, sc, *ws)


def _split_specs(tm, width):
    nc = N_CTX // tm
    return [pl.BlockSpec((tm, width), lambda i: (jnp.minimum(i, nc - 1), 0)),
            pl.BlockSpec((tm, width), lambda i: (jnp.maximum(i - nc, 0), 0))]


def _pick_split(ctx_ref, lat_ref):
    tm = ctx_ref.shape[0]
    return jnp.where(pl.program_id(0) < N_CTX // tm, ctx_ref[...], lat_ref[...])


def _proj_res_ln_body(*refs, split):
    n_in = sum(2 if s else 1 for s in split)
    part_refs = list(refs[:n_in])
    w_ref, x_ref, g_ref, lng_ref, lnb_ref, o_ref = refs[n_in:]
    cols = [_pick_split(part_refs.pop(0), part_refs.pop(0)) if s else part_refs.pop(0)[...] for s in split]
    a = cols[0] if len(cols) == 1 else jnp.concatenate(cols, axis=-1)
    y = ALPHA * x_ref[...] + g_ref[0] * jnp.dot(a, w_ref[...], preferred_element_type=F32)
    mu = jnp.mean(y, axis=-1, keepdims=True)
    yc = y - mu
    var = jnp.mean(yc * yc, axis=-1, keepdims=True)
    o_ref[...] = yc * lax.rsqrt(var + LN_EPS) * lng_ref[...] + lnb_ref[...]


def _proj_res_ln(parts, w, x, gate, ln_g, ln_b, *, tm, name):
    split = tuple(isinstance(p, tuple) for p in parts)
    in_specs, args = [], []
    for p, s in zip(parts, split):
        if s:
            in_specs += _split_specs(tm, p[0].shape[1])
            args += list(p)
        else:
            in_specs.append(pl.BlockSpec((tm, p.shape[1]), lambda i: (i, 0)))
            args.append(p)
    in_specs += [pl.BlockSpec(w.shape, lambda i: (0, 0)),
                 pl.BlockSpec((tm, D), lambda i: (i, 0)),
                 pl.BlockSpec((1, 1, D), lambda i: (_group_of_row(i * tm), 0, 0)),
                 pl.BlockSpec((1, D), lambda i: (0, 0)),
                 pl.BlockSpec((1, D), lambda i: (0, 0))]
    return pl.pallas_call(
        functools.partial(_proj_res_ln_body, split=split),
        out_shape=jax.ShapeDtypeStruct((N_TOK, D), F32),
        grid=(N_TOK // tm,),
        in_specs=in_specs,
        out_specs=pl.BlockSpec((tm, D), lambda i: (i, 0)),
        compiler_params=_params(1),
        name=name,
    )(*args, w, x, gate, ln_g.reshape(1, D), ln_b.reshape(1, D))


def _log_sigmoid(x):
    return jnp.minimum(x, 0.0) - jnp.log(1.0 + jnp.exp(-jnp.abs(x)))


def _gates_body(x_ref, sh_ref, sc_ref, wg_ref, wgt_ref, b_ref, bt_ref,
                lic_ref, bc_ref, lir_ref, br_ref, *, tm):
    h = _modulate(x_ref, sh_ref, sc_ref).astype(BF16)
    g = jnp.dot(h, wg_ref[...], preferred_element_type=F32) + b_ref[...]
    gt = lax.dot_general(wgt_ref[...], h, (((1,), (1,)), ((), ())), preferred_element_type=F32) + bt_ref[...]
    lic_ref[...] = g
    lir_ref[...] = gt
    lf, lft = _log_sigmoid(g), _log_sigmoid(gt)
    r = lax.broadcasted_iota(jnp.int32, (CHUNK, CHUNK), 0)
    c = lax.broadcasted_iota(jnp.int32, (CHUNK, CHUNK), 1)
    tri_l = (c <= r).astype(BF16)
    tri_u = (c >= r).astype(BF16)

    def pieces(v):
        out = []
        for _ in range(3):
            out.append(v.astype(BF16))
            v = v - out[-1].astype(F32)
        return out

    def cumsum(tri, v, tri_first):
        dots = [jnp.dot(tri, p, preferred_element_type=F32) if tri_first else jnp.dot(p, tri, preferred_element_type=F32)
                for p in pieces(v)]
        return dots[0] + dots[1] + dots[2]

    fwd_col = lax.broadcasted_iota(jnp.int32, (CHUNK, LANES), 1) < 2 * ML_HEADS
    fwd_row = lax.broadcasted_iota(jnp.int32, (N_GATES, CHUNK), 0) < 2 * ML_HEADS
    for ch in range(tm // CHUNK):
        sl = slice(ch * CHUNK, (ch + 1) * CHUNK)
        lfc, lftc = lf[sl, :], lft[:, sl]
        bc_ref[sl, :] = jnp.where(fwd_col, cumsum(tri_l, lfc, True), cumsum(tri_u, lfc, True))
        br_ref[:, sl] = jnp.where(fwd_row, cumsum(tri_u, lftc, False), cumsum(tri_l, lftc, False))


def _gates(x, sh, sc, wg, b_gate):
    tm = 512
    wg_pad = jnp.pad(wg, ((0, 0), (0, LANES - N_GATES)))
    b_pad = jnp.pad(b_gate, (0, LANES - N_GATES)).reshape(1, LANES)
    col = pl.BlockSpec((tm, LANES), lambda i: (i, 0))
    row = pl.BlockSpec((N_GATES, tm), lambda i: (0, i))
    return pl.pallas_call(
        functools.partial(_gates_body, tm=tm),
        out_shape=(jax.ShapeDtypeStruct((N_TOK, LANES), F32), jax.ShapeDtypeStruct((N_TOK, LANES), F32),
                   jax.ShapeDtypeStruct((N_GATES, N_TOK), F32), jax.ShapeDtypeStruct((N_GATES, N_TOK), F32)),
        grid=(N_TOK // tm,),
        in_specs=_mod_specs(tm, 0) + [pl.BlockSpec((D, LANES), lambda i: (0, 0)),
                                         pl.BlockSpec((N_GATES, D), lambda i: (0, 0)),
                                         pl.BlockSpec((1, LANES), lambda i: (0, 0)),
                                         pl.BlockSpec((N_GATES, 1), lambda i: (0, 0))],
        out_specs=(col, col, row, row),
        compiler_params=_params(1),
        name="mlstm_gates",
    )(x, sh, sc, wg_pad.astype(BF16), wg.T.astype(BF16), b_pad, b_gate.reshape(N_GATES, 1))


def _mlstm_body(*refs, has_init):
    (qf, kf, vf, licf, bcf, lirf, brf, qb, kb, vb, licb, bcb, lirb, brb) = refs[:14]
    refs = refs[14:]
    if has_init:
        c0_ref, n0_ref, m0_ref = refs[:3]
        refs = refs[3:]
    hf_ref, hb_ref, c_ref, n_ref, m_ref = refs

    @pl.when(pl.program_id(1) == 0)
    def _():
        if has_init:
            c_ref[...] = c0_ref[...]
            n_ref[...] = n0_ref[...]
            m_ref[...] = m0_ref[...]
        else:
            c_ref[...] = jnp.zeros_like(c_ref)
            n_ref[...] = jnp.zeros_like(n_ref)
            m_ref[...] = jnp.zeros_like(m_ref)

    t_idx = lax.broadcasted_iota(jnp.int32, (CHUNK, CHUNK), 0)
    s_idx = lax.broadcasted_iota(jnp.int32, (CHUNK, CHUNK), 1)
    nt = (((1,), (1,)), ((), ()))
    stores = []

    def chain(u, d, h, q_ref, k_ref, v_ref, lic_ref, bc_ref, lir_ref, br_ref, h_ref):
        mask = (s_idx <= t_idx) if d == 0 else (s_idx >= t_idx)
        hs = slice(h * ML_HEAD_DIM, (h + 1) * ML_HEAD_DIM)
        gi, gf = d * 2 * ML_HEADS + h, d * 2 * ML_HEADS + ML_HEADS + h
        q = q_ref[u, :, hs]
        k = k_ref[u, :, hs] * (ML_HEAD_DIM ** -0.5)
        v = v_ref[u, :, hs]
        qh, kh, vh = q.astype(BF16), k.astype(BF16), v.astype(BF16)
        li_c, b_c = lic_ref[u, :, gi:gi + 1], bc_ref[u, :, gf:gf + 1]
        li_r, b_r = lir_ref[u, gi:gi + 1, :], br_ref[u, gf:gf + 1, :]
        c_st = c_ref[u, d, h]
        n_st = n_ref[u, d, h:h + 1, :]
        m_st = m_ref[u, d, h:h + 1, :][:, 0:1]
        dmat = jnp.where(mask, b_c - b_r + li_r, -jnp.inf)
        inter = b_c + m_st
        m_out = jnp.maximum(inter, jnp.max(dmat, axis=-1, keepdims=True))
        p = jnp.exp(dmat - m_out)
        w_inter = jnp.exp(inter - m_out)
        yield
        qk = lax.dot_general(qh, kh, nt, preferred_element_type=F32)
        qc = jnp.dot(qh, c_st.astype(BF16), preferred_element_type=F32)
        yield
        s = qk * p
        den = (jnp.sum(s, axis=-1, keepdims=True)
               + w_inter * jnp.sum(q * n_st, axis=-1, keepdims=True))
        sh = s.astype(BF16)
        b_last = b_r[:, CHUNK - 1:CHUNK] if d == 0 else b_r[:, 0:1]
        g_r = b_last - b_r + li_r
        g_c = b_last - b_c + li_c
        m_new = jnp.maximum(b_last + m_st, jnp.max(g_r, axis=-1, keepdims=True))
        decay = jnp.exp(b_last + m_st - m_new)
        kw = k * jnp.exp(g_c - m_new)
        kwh = kw.astype(BF16)
        yield
        sv = jnp.dot(sh, vh, preferred_element_type=F32)
        kv = lax.dot_general(kwh, vh, (((0,), (0,)), ((), ())), preferred_element_type=F32)
        yield
        h_out = (sv + w_inter * qc) / jnp.maximum(jnp.abs(den), jnp.exp(-m_out))
        c_new = decay * c_st + kv
        n_new = decay * n_st + jnp.sum(kw, axis=0, keepdims=True)
        stores.append((h_ref, u, d, h, hs, h_out, c_new, n_new, jnp.broadcast_to(m_new, (1, ML_HEAD_DIM))))
        yield

    chains = [chain(u, d, h, *group)
              for u in range(c_ref.shape[0])
              for d, group in enumerate(((qf, kf, vf, licf, bcf, lirf, brf, hf_ref),
                                         (qb, kb, vb, licb, bcb, lirb, brb, hb_ref)))
              for h in range(ML_HEADS)]
    for _ in range(5):
        for ch in chains:
            next(ch)
    for h_ref, u, d, h, hs, h_out, c_new, n_new, m_new in stores:
        h_ref[u, :, hs] = h_out
        c_ref[u, d, h] = c_new
        n_ref[u, d, h:h + 1, :] = n_new
        m_ref[u, d, h:h + 1, :] = m_new


def _mlstm(proj, lic, bc, lir, br, *, row0, n_seq, seq_len, init=None):
    nc = seq_len // CHUNK
    par = MLSTM_PAR
    g0 = row0 // seq_len // par
    seqs = lambda a: a.reshape(N_TOK // seq_len, seq_len, a.shape[-1])
    rows_of = lambda a: seqs(a.T).transpose(0, 2, 1)

    def chunk_specs(chunk):
        return ([pl.BlockSpec((par, CHUNK, ML_W), lambda b, j, c=c: (g0 + b, chunk(j), c)) for c in range(3)]
                + [pl.BlockSpec((par, CHUNK, LANES), lambda b, j: (g0 + b, chunk(j), 0))] * 2
                + [pl.BlockSpec((par, N_GATES, CHUNK), lambda b, j: (g0 + b, 0, chunk(j)))] * 2)

    fwd, bwd = (lambda j: j), (lambda j: nc - 1 - j)
    st_c = pl.BlockSpec((par, 2, ML_HEADS, ML_HEAD_DIM, ML_HEAD_DIM), lambda b, j: (b, 0, 0, 0, 0))
    st_n = pl.BlockSpec((par, 2, ML_HEADS, ML_HEAD_DIM), lambda b, j: (b, 0, 0, 0))
    in_specs = chunk_specs(fwd) + chunk_specs(bwd)
    args = [seqs(proj)] * 3 + [seqs(lic), seqs(bc), rows_of(lir), rows_of(br)]
    args = args * 2
    if init is not None:
        in_specs += [st_c, st_n, st_n]
        args += list(init)
    h_shape = jax.ShapeDtypeStruct((n_seq, seq_len, ML_W), F32)
    hf, hb, c_st, n_st, m_st = pl.pallas_call(
        functools.partial(_mlstm_body, has_init=init is not None),
        out_shape=(h_shape, h_shape,
                   jax.ShapeDtypeStruct((n_seq, 2, ML_HEADS, ML_HEAD_DIM, ML_HEAD_DIM), F32),
                   jax.ShapeDtypeStruct((n_seq, 2, ML_HEADS, ML_HEAD_DIM), F32),
                   jax.ShapeDtypeStruct((n_seq, 2, ML_HEADS, ML_HEAD_DIM), F32)),
        grid=(n_seq // par, nc),
        in_specs=in_specs,
        out_specs=(pl.BlockSpec((par, CHUNK, ML_W), lambda b, j: (b, fwd(j), 0)),
                   pl.BlockSpec((par, CHUNK, ML_W), lambda b, j: (b, bwd(j), 0)),
                   st_c, st_n, st_n),
        compiler_params=_params(2),
        name="mlstm_scan",
    )(*args)
    return hf.reshape(-1, ML_W), hb.reshape(-1, ML_W), c_st, n_st, m_st


def _ml_post_body(hfc_ref, hfs_ref, hbc_ref, hbs_ref, o_ref, g_ref, out_ref):
    h = _pick_split(hfc_ref, hfs_ref) + _pick_split(hbc_ref, hbs_ref)
    gate = jax.nn.sigmoid(o_ref[...]) * g_ref[...]
    for hd in range(ML_HEADS):
        hs = slice(hd * ML_HEAD_DIM, (hd + 1) * ML_HEAD_DIM)
        x = h[:, hs]
        xc = x - jnp.mean(x, axis=-1, keepdims=True)
        var = jnp.mean(xc * xc, axis=-1, keepdims=True)
        out_ref[:, hs] = (gate[:, hs] * (xc * lax.rsqrt(var + RMS_EPS))).astype(BF16)


def _ml_post(hf, hb, proj, norm_g):
    tm = 512
    blk = pl.BlockSpec((tm, ML_W), lambda i: (i, 0))
    return pl.pallas_call(
        _ml_post_body,
        out_shape=jax.ShapeDtypeStruct((N_TOK, ML_W), BF16),
        grid=(N_TOK // tm,),
        in_specs=(_split_specs(tm, ML_W) * 2
                  + [pl.BlockSpec((tm, ML_W), lambda i: (i, 3)), pl.BlockSpec((1, ML_W), lambda i: (0, 0))]),
        out_specs=blk,
        compiler_params=_params(1),
        name="mlstm_out_norm",
    )(*hf, *hb, proj, norm_g.reshape(1, ML_W))


def _head_rms(x, gain):
    lane_head = lax.broadcasted_iota(jnp.int32, x.shape, 1) // ATT_HEAD_DIM
    sq = x * x
    ms = jnp.zeros_like(x)
    for hd in range(x.shape[1] // ATT_HEAD_DIM):
        sel = lane_head == hd
        ms = jnp.where(sel, jnp.sum(jnp.where(sel, sq, 0.0), axis=-1, keepdims=True), ms)
    return x * lax.rsqrt(ms * (1.0 / ATT_HEAD_DIM) + RMS_EPS) * gain


def _rope(x, cos, sin_signed):
    w = x.shape[1]
    even = lax.broadcasted_iota(jnp.int32, x.shape, 1) % 2 == 0
    partner = jnp.where(even, pltpu.roll(x, w - 1, 1), pltpu.roll(x, 1, 1))
    return x * cos + partner * sin_signed


def _qk_prep_body(q_ref, k_ref, qg_ref, kg_ref, *refs, rope):
    if rope:
        cq_ref, sq_ref, ck_ref, sk_ref, qo_ref, kn_ref, kr_ref = refs
    else:
        qo_ref, kn_ref, kr_ref = refs
    q = _head_rms(q_ref[...], qg_ref[...])
    k = _head_rms(k_ref[...], kg_ref[...])
    kn_ref[...] = k
    if rope:
        q = _rope(q, cq_ref[...], sq_ref[...])
        k = _rope(k, ck_ref[...], sk_ref[...])
    qo_ref[...] = (q * (ATT_HEAD_DIM ** -0.5)).astype(BF16)
    kr_ref[...] = k.astype(BF16)


def _qk_prep(proj, q_gain, k_gain, *, row0, rows, rope_tabs=None):
    tm = 512
    r0 = row0 // tm
    in_specs = [pl.BlockSpec((tm, ATT_W), lambda i: (r0 + i, 4 * ML_W // ATT_W)),
                pl.BlockSpec((tm, KV_W), lambda i: (r0 + i, (4 * ML_W + ATT_W) // KV_W)),
                pl.BlockSpec((1, ATT_W), lambda i: (0, 0)),
                pl.BlockSpec((1, KV_W), lambda i: (0, 0))]
    args = [proj, proj, jnp.tile(q_gain, ATT_HEADS).reshape(1, ATT_W), jnp.tile(k_gain, ATT_KV_HEADS).reshape(1, KV_W)]
    if rope_tabs is not None:
        per_seq = DEC_SEQ // tm
        in_specs += [pl.BlockSpec((tm, ATT_W), lambda i: (i % per_seq, 0))] * 2
        in_specs += [pl.BlockSpec((tm, KV_W), lambda i: (i % per_seq, 0))] * 2
        args += list(rope_tabs)
    return pl.pallas_call(
        functools.partial(_qk_prep_body, rope=rope_tabs is not None),
        out_shape=(jax.ShapeDtypeStruct((rows, ATT_W), BF16), jax.ShapeDtypeStruct((rows, KV_W), F32),
                   jax.ShapeDtypeStruct((rows, KV_W), BF16)),
        grid=(rows // tm,),
        in_specs=in_specs,
        out_specs=(pl.BlockSpec((tm, ATT_W), lambda i: (i, 0)), pl.BlockSpec((tm, KV_W), lambda i: (i, 0)),
                   pl.BlockSpec((tm, KV_W), lambda i: (i, 0))),
        compiler_params=_params(1),
        name="attn_qk_prep",
    )(*args)


def _rope_tables():
    rows = DEC_SEQ // GRID_W
    axis_dim = ATT_HEAD_DIM // 2
    row = jnp.repeat(jnp.arange(rows, dtype=F32), GRID_W)
    col = (jnp.arange(DEC_SEQ) % GRID_W).astype(F32)
    inv = ROPE_BASE ** (-jnp.arange(axis_dim // 2, dtype=F32) * 2.0 / axis_dim)
    ang = jnp.concatenate([row[:, None] * inv, col[:, None] * inv], axis=-1)
    cos = jnp.repeat(jnp.cos(ang), 2, axis=-1)
    sin = jnp.repeat(jnp.sin(ang), 2, axis=-1) * jnp.tile(jnp.array([-1.0, 1.0], F32), axis_dim)
    return (jnp.tile(cos, (1, ATT_HEADS)), jnp.tile(sin, (1, ATT_HEADS)),
            jnp.tile(cos, (1, ATT_KV_HEADS)), jnp.tile(sin, (1, ATT_KV_HEADS)))


def _attn_body(q_ref, k_ref, v_ref, o_ref):
    k, v = k_ref[0, 0], v_ref[0, 0]
    dh = k.shape[0]
    g = q_ref.shape[1] // dh
    pair = 2

    def heads(h0):
        q = jnp.concatenate([q_ref[:, h * dh:(h + 1) * dh] for h in range(h0, h0 + pair)], axis=0)
        s = jnp.dot(q, k, preferred_element_type=F32)
        yield
        e = jnp.exp((s - jnp.max(s, axis=-1, keepdims=True)).astype(BF16))
        yield
        o = jnp.dot(e, v, preferred_element_type=F32)
        o = (o[:, :dh] / o[:, dh:dh + 1]).astype(BF16)
        tq = q_ref.shape[0]
        for j in range(pair):
            o_ref[:, (h0 + j) * dh:(h0 + j + 1) * dh] = o[j * tq:(j + 1) * tq]
        yield

    chains = [heads(h0) for h0 in range(0, g, pair)]
    for step in range(len(chains) + 2):
        for i, chain in enumerate(chains):
            if 0 <= step - i < 3:
                next(chain)


def _attention(q, k_t, v_ones, *, tq):
    n_seq, _, dh, s_len = k_t.shape
    nq = q.shape[0] // n_seq // tq
    qo = pl.BlockSpec((tq, GROUP_W), lambda b, kh, i: (b * nq + i, kh))
    return pl.pallas_call(
        _attn_body,
        out_shape=jax.ShapeDtypeStruct(q.shape, BF16),
        grid=(n_seq, ATT_KV_HEADS, nq),
        in_specs=[qo, pl.BlockSpec((1, 1, dh, s_len), lambda b, kh, i: (b, kh, 0, 0)),
                  pl.BlockSpec((1, 1, s_len, LANES), lambda b, kh, i: (b, kh, 0, 0))],
        out_specs=qo,
        compiler_params=_params(3),
        name="attention",
    )(q, k_t, v_ones)


def _head_major(x, n_seq):
    return x.reshape(n_seq, -1, x.shape[1] // ATT_HEAD_DIM, ATT_HEAD_DIM).transpose(0, 2, 1, 3)


def _with_ones(v):
    pad = jnp.zeros(v.shape[:-1] + (LANES - ATT_HEAD_DIM - 1,), v.dtype)
    return jnp.concatenate([v, jnp.ones(v.shape[:-1] + (1,), v.dtype), pad], axis=-1)


def _hy_filter_body(feat_ref, t_ref, w1_ref, b1_ref, w2_ref, b2_ref, fr_ref, w3f_ref, w3b_ref, dl_ref,
                    hsum_ref, hdiff_ref, nyq_ref, z_ref):
    @pl.when(pl.program_id(0) == 0)
    def _():
        z = jnp.dot(feat_ref[...], w1_ref[...], preferred_element_type=F32, precision=HIGHEST) + b1_ref[...]
        z = jnp.sin(fr_ref[0:1, :] * z)
        z = jnp.dot(z, w2_ref[...], preferred_element_type=F32, precision=HIGHEST) + b2_ref[...]
        z_ref[...] = jnp.sin(fr_ref[1:2, :] * z)

    z = z_ref[...].astype(BF16)
    window = jnp.exp(-t_ref[...] * dl_ref[...])
    h_f = jnp.dot(z, w3f_ref[...].astype(BF16), preferred_element_type=F32) * window
    h_b = jnp.dot(z, w3b_ref[...].astype(BF16), preferred_element_type=F32) * window
    row = lax.broadcasted_iota(jnp.int32, h_f.shape, 0)
    h_b = jnp.where(row == 0, 0.0, h_b)
    inv = 1.0 / (jnp.sum(jnp.abs(h_f), axis=0, keepdims=True) + jnp.sum(jnp.abs(h_b), axis=0, keepdims=True))
    h_sum = (h_f + h_b) * inv
    hsum_ref[...] = h_sum
    hdiff_ref[...] = (h_f - h_b) * inv
    nyq_ref[...] = jnp.sum(jnp.where(row % 2 == 0, h_sum, -h_sum), axis=0, keepdims=True)


def _hy_filter(seq_len, w1, b1, w2, b2, w3, sin_freq):
    tc = 256
    fw = w1.shape[1]
    t = jnp.arange(seq_len, dtype=F32)[:, None] / seq_len
    bands = jnp.arange(1, HY_BANDS + 1, dtype=F32)[None, :]
    feat = jnp.concatenate([t, jnp.sin(2.0 * math.pi * bands * t), jnp.cos(2.0 * math.pi * bands * t)], axis=-1)
    feat = jnp.pad(feat, ((0, 0), (0, LANES - HY_EMB)))
    deltas = jnp.abs(jnp.linspace(math.log(HY_TARGET) / HY_LONG_PCT, math.log(HY_TARGET) / HY_SHORT_PCT, D,
                                  dtype=F32)).reshape(1, D)
    pad_w = LANES - fw
    full = lambda shape: pl.BlockSpec(shape, lambda j: (0,) * len(shape))
    return pl.pallas_call(
        _hy_filter_body,
        out_shape=(jax.ShapeDtypeStruct((seq_len, D), F32), jax.ShapeDtypeStruct((seq_len, D), F32),
                   jax.ShapeDtypeStruct((1, D), F32)),
        grid=(D // tc,),
        in_specs=[full((seq_len, LANES)), full((seq_len, 1)), full((LANES, LANES)), full((1, LANES)),
                  full((LANES, LANES)), full((1, LANES)), full((2, LANES)),
                  pl.BlockSpec((LANES, tc), lambda j: (0, j)), pl.BlockSpec((LANES, tc), lambda j: (0, D // tc + j)),
                  pl.BlockSpec((1, tc), lambda j: (0, j))],
        out_specs=(pl.BlockSpec((seq_len, tc), lambda j: (0, j)), pl.BlockSpec((seq_len, tc), lambda j: (0, j)),
                   pl.BlockSpec((1, tc), lambda j: (0, j))),
        scratch_shapes=[pltpu.VMEM((seq_len, LANES), F32)],
        compiler_params=_params(1),
        name="hyena_filter",
    )(feat, t, jnp.pad(w1, ((0, LANES - HY_EMB), (0, pad_w))), jnp.pad(b1, (0, pad_w)).reshape(1, LANES),
      jnp.pad(w2, ((0, pad_w), (0, pad_w))), jnp.pad(b2, (0, pad_w)).reshape(1, LANES),
      jnp.pad(sin_freq, ((0, 0), (0, pad_w))), jnp.pad(w3, ((0, pad_w), (0, 0))), jnp.pad(w3, ((0, pad_w), (0, 0))),
      deltas)


def _dft_matrices(seq_len):
    n = 2 * seq_len
    k = lax.broadcasted_iota(jnp.int32, (seq_len, seq_len), 0)
    t = lax.broadcasted_iota(jnp.int32, (seq_len, seq_len), 1)
    ang = ((k * t) % n).astype(F32) * (2.0 * math.pi / n)
    cr, base = jnp.cos(ang), -jnp.sin(ang)
    ci = jnp.where(k == 0, (1 - 2 * (t % 2)).astype(F32), base)
    cit = jnp.where(t == 0, (1 - 2 * (k % 2)).astype(F32), base)
    return cr.astype(BF16), ci.astype(BF16), cit.astype(BF16)


def _ctx_spectrum_body(cr_ref, ci_ref, hs_ref, hd_ref, nyq_ref, kr_ref, ki_ref):
    kr_ref[...] = jnp.dot(cr_ref[...], hs_ref[...].astype(BF16), preferred_element_type=F32)
    ki = jnp.dot(ci_ref[...], hd_ref[...].astype(BF16), preferred_element_type=F32)
    first = lax.broadcasted_iota(jnp.int32, ki.shape, 0) == 0
    ki_ref[...] = jnp.where(first, nyq_ref[...], ki)


def _ctx_spectrum(mats, h_sum, h_diff, nyq):
    tn = 512
    cr, ci, _ = mats
    mat = pl.BlockSpec((SEQ, SEQ), lambda c: (0, 0))
    chan = pl.BlockSpec((SEQ, tn), lambda c: (0, c))
    shape = jax.ShapeDtypeStruct((SEQ, D), F32)
    return pl.pallas_call(
        _ctx_spectrum_body,
        out_shape=(shape, shape),
        grid=(D // tn,),
        in_specs=[mat, mat, chan, chan, pl.BlockSpec((1, tn), lambda c: (0, c))],
        out_specs=(chan, chan),
        compiler_params=_params(1),
        name="hyena_filter_dft",
    )(cr, ci, h_sum, h_diff, nyq)


def _hy_conv_ctx_body(x0_ref, x1_ref, v_ref, w0_ref, w1_ref, wv_ref, b0_ref, b1_ref, bv_ref, kr_ref, ki_ref,
                      skip_ref, cr_ref, ci_ref, ct_ref, o_ref):
    n = x0_ref.shape[0]
    pos = lax.broadcasted_iota(jnp.int32, x0_ref.shape, 0) % SEQ

    def conv(u_ref, w_ref, b_ref):
        u = u_ref[...]
        prev = jnp.where(pos == 0, 0.0, pltpu.roll(u, 1, 0))
        nxt = jnp.where(pos == SEQ - 1, 0.0, pltpu.roll(u, n - 1, 0))
        return prev * w_ref[0:1, :] + u * w_ref[1:2, :] + nxt * w_ref[2:3, :] + b_ref[...]

    z = conv(v_ref, wv_ref, bv_ref) * conv(x1_ref, w1_ref, b1_ref)
    gated = z * skip_ref[...]
    x0 = conv(x0_ref, w0_ref, b0_ref)
    kr, ki = kr_ref[...], ki_ref[...]
    first = lax.broadcasted_iota(jnp.int32, kr.shape, 0) == 0
    for s in range(n // SEQ):
        rows = slice(s * SEQ, (s + 1) * SEQ)
        zs = z[rows].astype(BF16)
        zr = jnp.dot(cr_ref[...], zs, preferred_element_type=F32)
        zi = jnp.dot(ci_ref[...], zs, preferred_element_type=F32)
        yr = jnp.where(first, 0.5 * zr * kr, zr * kr - zi * ki).astype(BF16)
        yi = jnp.where(first, 0.5 * zi * ki, zr * ki + zi * kr).astype(BF16)
        y = (jnp.dot(cr_ref[...], yr, preferred_element_type=F32)
             + jnp.dot(ct_ref[...], yi, preferred_element_type=F32))
        o_ref[rows, :] = ((y * (1.0 / SEQ) + gated[rows]) * x0[rows]).astype(BF16)


def _hy_conv_ctx(u, conv_w, conv_b, kr, ki, skip, mats, *, seqs_per_step):
    tc = 256
    nb = D // tc
    rows = seqs_per_step * SEQ

    def col(part):
        return [pl.BlockSpec((rows, tc), lambda b, c: (b, part * nb + c)),
                pl.BlockSpec((3, tc), lambda b, c: (0, part * nb + c)),
                pl.BlockSpec((1, tc), lambda b, c: (0, part * nb + c))]

    specs = [col(p) for p in range(3)]
    chan = pl.BlockSpec((SEQ, tc), lambda b, c: (0, c))
    mat = pl.BlockSpec((SEQ, SEQ), lambda b, c: (0, 0))
    cb = conv_b.reshape(1, 3 * D)
    return pl.pallas_call(
        _hy_conv_ctx_body,
        out_shape=jax.ShapeDtypeStruct((N_CTX, D), BF16),
        grid=(BATCH // seqs_per_step, nb),
        in_specs=([s[0] for s in specs] + [s[1] for s in specs] + [s[2] for s in specs]
                  + [chan, chan, pl.BlockSpec((1, tc), lambda b, c: (0, c)), mat, mat, mat]),
        out_specs=pl.BlockSpec((rows, tc), lambda b, c: (b, c)),
        compiler_params=_params(2),
        name="hyena_conv_ctx",
    )(u, u, u, conv_w, conv_w, conv_w, cb, cb, cb, kr, ki, skip.reshape(1, D), *mats)


FFT_A, FFT_R = 64, 64
FFT_M = 2 * FFT_A
FFT_H = FFT_R // 2
assert FFT_A * FFT_R == DEC_SEQ


SUB = 8
FFT_BLK = 2 * FFT_M * SUB


def _fft_consts():
    n = 2 * DEC_SEQ
    th = 2.0 * np.pi * (np.arange(FFT_M)[:, None] + 0.5) * np.arange(FFT_A)[None, :] / FFT_M
    f1 = np.concatenate([np.cos(th), -np.sin(th)], axis=0)
    k = np.arange(FFT_M)[:, None, None] + FFT_M * np.arange(FFT_H)[None, :, None] + 0.5
    ph = 2.0 * np.pi * k * np.arange(FFT_R)[None, None, :] / n
    c, s = np.cos(ph), np.sin(ph)
    g = np.concatenate([np.concatenate([c, s], axis=2), np.concatenate([-s, c], axis=2)], axis=1)
    as_bf16 = lambda m: jnp.asarray(m, dtype=F32).astype(BF16)
    return as_bf16(np.kron(f1, np.eye(SUB))), as_bf16(g)


def _fft_stage1(src_ref, y_ref, f1k):
    for b1 in range(FFT_R // SUB):
        x = jnp.concatenate([src_ref[a * FFT_R + b1 * SUB:a * FFT_R + (b1 + 1) * SUB, :] for a in range(FFT_A)],
                            axis=0)
        y_ref[b1 * FFT_BLK:(b1 + 1) * FFT_BLK, :] = jnp.dot(f1k, x.astype(BF16), preferred_element_type=F32)


def _fft_rows(k1):
    return [b1 * FFT_BLK + ri * FFT_M * SUB + k1 * SUB for ri in range(2) for b1 in range(FFT_R // SUB)]


def _fft_stage2(y_ref, g_ref, k1):
    y = jnp.concatenate([y_ref[r:r + SUB, :] for r in _fft_rows(k1)], axis=0).astype(BF16)
    z = jnp.dot(g_ref[k1], y, preferred_element_type=F32)
    return z[:FFT_H], z[FFT_H:]


def _hy_spectrum_body(hs_ref, hd_ref, f1_ref, g_ref, kr_ref, ki_ref, h_ref, y_ref):
    tc = hs_ref.shape[1]
    h_ref[:, :tc] = hs_ref[...]
    h_ref[:, tc:] = hd_ref[...]
    _fft_stage1(h_ref, y_ref, f1_ref[...])
    for k1 in range(FFT_M):
        zr, zi = _fft_stage2(y_ref, g_ref, k1)
        kr_ref[k1 * FFT_H:(k1 + 1) * FFT_H, :] = zr[:, :tc]
        ki_ref[k1 * FFT_H:(k1 + 1) * FFT_H, :] = zi[:, tc:]


def _hy_spectrum(h_sum, h_diff, consts):
    tc = 128
    f1k, g = consts
    blk = pl.BlockSpec((DEC_SEQ, tc), lambda c: (0, c))
    shape = jax.ShapeDtypeStruct((DEC_SEQ, D), F32)
    return pl.pallas_call(
        _hy_spectrum_body,
        out_shape=(shape, shape),
        grid=(D // tc,),
        in_specs=[blk, blk, pl.BlockSpec(f1k.shape, lambda c: (0, 0)), pl.BlockSpec(g.shape, lambda c: (0, 0, 0))],
        out_specs=(blk, blk),
        scratch_shapes=[pltpu.VMEM((DEC_SEQ, 2 * tc), F32), pltpu.VMEM((FFT_R * 2 * FFT_M, 2 * tc), F32)],
        compiler_params=_params(1),
        name="hyena_filter_fft",
    )(h_sum, h_diff, f1k, g)


def _hy_conv_fft_body(x0_ref, x1_ref, v_ref, w0_ref, w1_ref, wv_ref, b0_ref, b1_ref, bv_ref, kr_ref, ki_ref,
                      skip_ref, f1_ref, g_ref, o_ref, z_ref, y_ref, t_ref):
    rows = 512
    tn = (((0,), (0,)), ((), ()))

    def conv(u_ref, w_ref, b_ref, r):
        u = u_ref[r:r + rows, :]
        row = lax.broadcasted_iota(jnp.int32, u.shape, 0)
        before = u_ref[r - 1:r, :] if r > 0 else jnp.zeros_like(u[0:1])
        after = u_ref[r + rows:r + rows + 1, :] if r + rows < DEC_SEQ else jnp.zeros_like(u[0:1])
        prev = jnp.where(row == 0, before, pltpu.roll(u, 1, 0))
        nxt = jnp.where(row == rows - 1, after, pltpu.roll(u, rows - 1, 0))
        return prev * w_ref[0:1, :] + u * w_ref[1:2, :] + nxt * w_ref[2:3, :] + b_ref[...]

    for r in range(0, DEC_SEQ, rows):
        z_ref[r:r + rows, :] = conv(v_ref, wv_ref, bv_ref, r) * conv(x1_ref, w1_ref, b1_ref, r)
    f1k = f1_ref[...]
    _fft_stage1(z_ref, y_ref, f1k)
    for k1 in range(FFT_M):
        zr, zi = _fft_stage2(y_ref, g_ref, k1)
        kr = kr_ref[k1 * FFT_H:(k1 + 1) * FFT_H, :]
        ki = ki_ref[k1 * FFT_H:(k1 + 1) * FFT_H, :]
        p = jnp.concatenate([zr * kr - zi * ki, zr * ki + zi * kr], axis=0).astype(BF16)
        u = lax.dot_general(g_ref[k1], p, tn, preferred_element_type=F32)
        for j, r in enumerate(_fft_rows(k1)):
            y_ref[r:r + SUB, :] = u[j * SUB:(j + 1) * SUB]
    for b1 in range(FFT_R // SUB):
        blk = y_ref[b1 * FFT_BLK:(b1 + 1) * FFT_BLK, :].astype(BF16)
        yb = lax.dot_general(f1k, blk, tn, preferred_element_type=F32)
        for a in range(FFT_A):
            t_ref[a * FFT_R + b1 * SUB:a * FFT_R + (b1 + 1) * SUB, :] = yb[a * SUB:(a + 1) * SUB]
    for r in range(0, DEC_SEQ, rows):
        y = t_ref[r:r + rows, :] * (1.0 / DEC_SEQ) + z_ref[r:r + rows, :] * skip_ref[...]
        o_ref[r:r + rows, :] = (y * conv(x0_ref, w0_ref, b0_ref, r)).astype(BF16)


def _hy_conv_fft(u, conv_w, conv_b, kr, ki, skip, consts, *, row0, n_seq):
    tc = 128
    nb = D // tc
    r0 = row0 // DEC_SEQ

    def col(part):
        return [pl.BlockSpec((DEC_SEQ, tc), lambda b, c: (r0 + b, part * nb + c)),
                pl.BlockSpec((3, tc), lambda b, c: (0, part * nb + c)),
                pl.BlockSpec((1, tc), lambda b, c: (0, part * nb + c))]

    specs = [col(p) for p in range(3)]
    chan = pl.BlockSpec((DEC_SEQ, tc), lambda b, c: (0, c))
    const = lambda m: pl.BlockSpec(m.shape, lambda b, c: (0,) * m.ndim)
    cb = conv_b.reshape(1, 3 * D)
    return pl.pallas_call(
        _hy_conv_fft_body,
        out_shape=jax.ShapeDtypeStruct((n_seq * DEC_SEQ, D), BF16),
        grid=(n_seq, nb),
        in_specs=([s[0] for s in specs] + [s[1] for s in specs] + [s[2] for s in specs]
                  + [chan, chan, pl.BlockSpec((1, tc), lambda b, c: (0, c))] + [const(m) for m in consts]),
        out_specs=pl.BlockSpec((DEC_SEQ, tc), lambda b, c: (b, c)),
        scratch_shapes=[pltpu.VMEM((DEC_SEQ, tc), F32), pltpu.VMEM((FFT_R * 2 * FFT_M, tc), F32),
                        pltpu.VMEM((DEC_SEQ, tc), F32)],
        compiler_params=_params(2, 56 * 1024 * 1024),
        name="hyena_conv_fft",
    )(u, u, u, conv_w, conv_w, conv_w, cb, cb, cb, kr, ki, skip.reshape(1, D), *consts)


ROW_TILE = D // LANES
ROUTER_TM = 512
EXPERT_TM = 512
N_SLOTS = 2 * N_TOK + N_EXPERTS * EXPERT_TM
N_SLOT_TILES = N_SLOTS // EXPERT_TM
INFO_E1, INFO_E2, INFO_R1, INFO_R2, INFO_W1, INFO_W2 = range(6)


def _to_row_tiles(ref, x):
    rows = x.shape[0]
    for j in range(ROW_TILE):
        ref[pl.ds(j, rows, stride=ROW_TILE), :] = x[:, j * LANES:(j + 1) * LANES]


def _from_row_tiles(ref, rows):
    return jnp.concatenate([ref[pl.ds(j, rows, stride=ROW_TILE), :] for j in range(ROW_TILE)], axis=-1)


def _router_body(x_ref, sh_ref, sc_ref, w_ref, info_ref, incl_ref, cnt_ref):
    @pl.when(pl.program_id(0) == 0)
    def _():
        cnt_ref[...] = jnp.zeros_like(cnt_ref)

    h = _modulate(x_ref, sh_ref, sc_ref)
    h_hi, w = h.astype(BF16), w_ref[...]
    h_lo, w_hi = (h - h_hi.astype(F32)).astype(BF16), w.astype(BF16)
    w_lo = (w - w_hi.astype(F32)).astype(BF16)
    logits = (jnp.dot(h_hi, w_hi, preferred_element_type=F32) + jnp.dot(h_lo, w_hi, preferred_element_type=F32)
              + jnp.dot(h_hi, w_lo, preferred_element_type=F32))
    lane = lax.broadcasted_iota(jnp.int32, logits.shape, 1).astype(F32)
    logits = jnp.where(lane < N_EXPERTS, logits, -jnp.inf)
    e = jnp.exp(logits - jnp.max(logits, axis=-1, keepdims=True))
    p = e / jnp.sum(e, axis=-1, keepdims=True)
    p1 = jnp.max(p, axis=-1, keepdims=True)
    i1 = jnp.min(jnp.where(p == p1, lane, float(LANES)), axis=-1, keepdims=True)
    rest = jnp.where(lane == i1, -1.0, p)
    p2 = jnp.max(rest, axis=-1, keepdims=True)
    i2 = jnp.min(jnp.where(rest == p2, lane, float(LANES)), axis=-1, keepdims=True)
    total = p1 + p2
    chosen = jnp.where((lane == i1) | (lane == i2), 1.0, 0.0)
    tm = chosen.shape[0]
    earlier = (lax.broadcasted_iota(jnp.int32, (tm, tm), 1) < lax.broadcasted_iota(jnp.int32, (tm, tm), 0))
    rank = jnp.dot(earlier.astype(BF16), chosen.astype(BF16), preferred_element_type=F32) + cnt_ref[...]
    r1 = jnp.sum(jnp.where(lane == i1, rank, 0.0), axis=-1, keepdims=True)
    r2 = jnp.sum(jnp.where(lane == i2, rank, 0.0), axis=-1, keepdims=True)
    cnt_ref[...] += jnp.sum(chosen, axis=0, keepdims=True)
    incl_ref[0] = jnp.broadcast_to(cnt_ref[...], incl_ref.shape[1:])
    info = jnp.zeros_like(p)
    for col, val in ((INFO_E1, i1), (INFO_E2, i2), (INFO_R1, r1), (INFO_R2, r2),
                     (INFO_W1, p1 / total), (INFO_W2, p2 / total)):
        info = jnp.where(lane == col, val, info)
    info_ref[...] = info


def _router(x, sh, sc, w_router):
    tm = ROUTER_TM
    return pl.pallas_call(
        _router_body,
        out_shape=(jax.ShapeDtypeStruct((N_TOK, LANES), F32),
                   jax.ShapeDtypeStruct((N_TOK // tm, 8, LANES), F32)),
        grid=(N_TOK // tm,),
        in_specs=_mod_specs(tm, 0) + [pl.BlockSpec((D, LANES), lambda i: (0, 0))],
        out_specs=(pl.BlockSpec((tm, LANES), lambda i: (i, 0)), pl.BlockSpec((1, 8, LANES), lambda i: (i, 0, 0))),
        scratch_shapes=[pltpu.VMEM((1, LANES), F32)],
        compiler_params=_params(1),
        name="moe_router",
    )(x, sh, sc, jnp.pad(w_router, ((0, 0), (0, LANES - N_EXPERTS))))


def _row_tile(ref, row):
    return ref.at[pl.ds(pl.multiple_of(row * ROW_TILE, ROW_TILE), ROW_TILE)]


def _dispatch_body(pos1_ref, pos2_ref, x_ref, zeros_hbm, xs_hbm, rows_ref, sem, *, tm):
    del zeros_hbm
    base = pl.program_id(0) * tm
    _to_row_tiles(rows_ref, x_ref[...])

    def copies(r):
        src = _row_tile(rows_ref, r)
        return (pltpu.make_async_copy(src, _row_tile(xs_hbm, pos1_ref[base + r]), sem),
                pltpu.make_async_copy(src, _row_tile(xs_hbm, pos2_ref[base + r]), sem))

    def issue(r, carry):
        for queue, cp in enumerate(copies(r)):
            cp.start(priority=queue)
        return carry

    lax.fori_loop(0, tm, issue, 0, unroll=8)
    for _ in range(2):
        pltpu.make_async_copy(rows_ref, xs_hbm.at[pl.ds(0, tm * ROW_TILE)], sem).wait()


def _dispatch(pos1, pos2, x):
    tm = 512
    return pl.pallas_call(
        functools.partial(_dispatch_body, tm=tm),
        out_shape=jax.ShapeDtypeStruct((N_SLOTS * ROW_TILE, LANES), F32),
        grid_spec=pltpu.PrefetchScalarGridSpec(
            num_scalar_prefetch=2, grid=(N_TOK // tm,),
            in_specs=[pl.BlockSpec((tm, D), lambda i, *_: (i, 0)), pl.BlockSpec(memory_space=pl.ANY)],
            out_specs=pl.BlockSpec(memory_space=pl.ANY),
            scratch_shapes=[pltpu.VMEM((tm * ROW_TILE, LANES), F32), pltpu.SemaphoreType.DMA(())]),
        input_output_aliases={3: 0},
        compiler_params=_params(1),
        name="moe_dispatch",
    )(pos1, pos2, x, jnp.zeros((N_SLOTS * ROW_TILE, LANES), F32))


def _new_expert(eid_ref, nv_ref, t):
    tt = jnp.minimum(t, nv_ref[0] - 1)
    return (t == 0) | (eid_ref[tt] != eid_ref[jnp.maximum(tt - 1, 0)])


def _expert_swiglu_body(eid_ref, b1_ref, b2_ref, nv_ref, xs_ref, sh_ref, sc_ref, wg_ref, wu_ref, o_ref,
                        wgb_ref, wub_ref, *, fc):
    t = pl.program_id(1)

    @pl.when(_new_expert(eid_ref, nv_ref, t))
    def _():
        wgb_ref[...] = wg_ref[0].astype(BF16)
        wub_ref[...] = wu_ref[0].astype(BF16)

    @pl.when(t < nv_ref[0])
    def _():
        tm = o_ref.shape[0]
        x = _from_row_tiles(xs_ref, tm)
        slot = t * tm + lax.broadcasted_iota(jnp.int32, (tm, 1), 0)
        in1, in2 = slot >= b1_ref[t], slot >= b2_ref[t]
        sc = jnp.where(in2, sc_ref[2], jnp.where(in1, sc_ref[1], sc_ref[0]))
        sh = jnp.where(in2, sh_ref[2], jnp.where(in1, sh_ref[1], sh_ref[0]))
        h = (x * (1.0 + sc) + sh).astype(BF16)
        for c in range(o_ref.shape[1] // fc):
            cs = slice(c * fc, (c + 1) * fc)
            g = jnp.dot(h, wgb_ref[:, cs], preferred_element_type=F32)
            u = jnp.dot(h, wub_ref[:, cs], preferred_element_type=F32)
            o_ref[:, cs] = (g * jax.nn.sigmoid(g) * u).astype(BF16)

    @pl.when(t >= nv_ref[0])
    def _():
        o_ref[...] = jnp.zeros_like(o_ref)


def _expert_swiglu(meta, xs_rt, sh, sc, w_gate, w_up, e0):
    eid, b1, b2, nv = meta
    tm, f = EXPERT_TM, w_gate.shape[2]
    fh = f // 2

    def tile(t, eid, b1, b2, nv):
        return jnp.minimum(t, nv[0] - 1)

    w_spec = pl.BlockSpec((1, D, fh), lambda p, t, eid, b1, b2, nv: (e0 + eid[tile(t, eid, b1, b2, nv)], 0, p))
    mod = pl.BlockSpec((8, 1, D), lambda p, t, *_: (0, 0, 0))
    return pl.pallas_call(
        functools.partial(_expert_swiglu_body, fc=256),
        out_shape=jax.ShapeDtypeStruct((N_SLOTS, f), BF16),
        grid_spec=pltpu.PrefetchScalarGridSpec(
            num_scalar_prefetch=4, grid=(2, N_SLOT_TILES),
            in_specs=[pl.BlockSpec((tm * ROW_TILE, LANES), lambda p, t, *m: (tile(t, *m), 0)), mod, mod,
                      w_spec, w_spec],
            out_specs=pl.BlockSpec((tm, fh), lambda p, t, *m: (t, p)),
            scratch_shapes=[pltpu.VMEM((D, fh), BF16), pltpu.VMEM((D, fh), BF16)]),
        compiler_params=_params(2, 56 * 1024 * 1024),
        name="moe_swiglu",
    )(eid, b1, b2, nv, xs_rt, sh, sc, w_gate, w_up)


def _expert_down_body(eid_ref, nv_ref, a_ref, w_ref, y_ref, wb_ref):
    t = pl.program_id(0)

    @pl.when(_new_expert(eid_ref, nv_ref, t))
    def _():
        wb_ref[...] = w_ref[0].astype(BF16)

    @pl.when(t < nv_ref[0])
    def _():
        _to_row_tiles(y_ref, jnp.dot(a_ref[...], wb_ref[...], preferred_element_type=F32))

    @pl.when(t >= nv_ref[0])
    def _():
        y_ref[...] = jnp.zeros_like(y_ref)


def _expert_down(meta, act, w_down, e0):
    eid, _, _, nv = meta
    tm, f = EXPERT_TM, act.shape[1]

    def tile(t, eid, nv):
        return jnp.minimum(t, nv[0] - 1)

    return pl.pallas_call(
        _expert_down_body,
        out_shape=jax.ShapeDtypeStruct((N_SLOTS * ROW_TILE, LANES), F32),
        grid_spec=pltpu.PrefetchScalarGridSpec(
            num_scalar_prefetch=2, grid=(N_SLOT_TILES,),
            in_specs=[pl.BlockSpec((tm, f), lambda t, *m: (tile(t, *m), 0)),
                      pl.BlockSpec((1, f, D), lambda t, eid, nv: (e0 + eid[tile(t, eid, nv)], 0, 0))],
            out_specs=pl.BlockSpec((tm * ROW_TILE, LANES), lambda t, *m: (t, 0)),
            scratch_shapes=[pltpu.VMEM((f, D), BF16)]),
        compiler_params=_params(1, 58 * 1024 * 1024),
        name="moe_down",
    )(eid, nv, act, w_down)


def _combine_body(pos1_ref, pos2_ref, y_hbm, info_ref, x_ref, g_ref, lng_ref, lnb_ref, o_ref, y1_ref, y2_ref, sem,
                  *, tm):
    base = pl.program_id(0) * tm

    def copies(r):
        return (pltpu.make_async_copy(_row_tile(y_hbm, pos1_ref[base + r]), _row_tile(y1_ref, r), sem),
                pltpu.make_async_copy(_row_tile(y_hbm, pos2_ref[base + r]), _row_tile(y2_ref, r), sem))

    def issue(r, carry):
        for queue, cp in enumerate(copies(r)):
            cp.start(priority=queue)
        return carry

    lax.fori_loop(0, tm, issue, 0, unroll=8)
    for y_ref in (y1_ref, y2_ref):
        pltpu.make_async_copy(y_hbm.at[pl.ds(0, tm * ROW_TILE)], y_ref, sem).wait()
    info = info_ref[...]
    ffn = (info[:, INFO_W1:INFO_W1 + 1] * _from_row_tiles(y1_ref, tm)
           + info[:, INFO_W2:INFO_W2 + 1] * _from_row_tiles(y2_ref, tm))
    y = ALPHA * x_ref[...] + g_ref[0] * ffn
    mu = jnp.mean(y, axis=-1, keepdims=True)
    yc = y - mu
    var = jnp.mean(yc * yc, axis=-1, keepdims=True)
    o_ref[...] = yc * lax.rsqrt(var + LN_EPS) * lng_ref[...] + lnb_ref[...]


def _combine(pos1, pos2, y_rt, info, x, gate, ln_g, ln_b):
    tm = 512
    return pl.pallas_call(
        functools.partial(_combine_body, tm=tm),
        out_shape=jax.ShapeDtypeStruct((N_TOK, D), F32),
        grid_spec=pltpu.PrefetchScalarGridSpec(
            num_scalar_prefetch=2, grid=(N_TOK // tm,),
            in_specs=[pl.BlockSpec(memory_space=pl.ANY),
                      pl.BlockSpec((tm, LANES), lambda i, *_: (i, 0)),
                      pl.BlockSpec((tm, D), lambda i, *_: (i, 0)),
                      pl.BlockSpec((1, 1, D), lambda i, *_: (_group_of_row(i * tm), 0, 0)),
                      pl.BlockSpec((1, D), lambda i, *_: (0, 0)),
                      pl.BlockSpec((1, D), lambda i, *_: (0, 0))],
            out_specs=pl.BlockSpec((tm, D), lambda i, *_: (i, 0)),
            scratch_shapes=[pltpu.VMEM((tm * ROW_TILE, LANES), F32), pltpu.VMEM((tm * ROW_TILE, LANES), F32),
                            pltpu.SemaphoreType.DMA(())]),
        compiler_params=_params(1),
        name="moe_combine",
    )(pos1, pos2, y_rt, info, x, gate, ln_g.reshape(1, D), ln_b.reshape(1, D))


def _slot_plan(info, incl):
    row = lambda n_rows: incl[n_rows // ROUTER_TM - 1, 0, :N_EXPERTS].astype(jnp.int32)
    count = row(N_TOK)
    padded = (count + EXPERT_TM - 1) // EXPERT_TM * EXPERT_TM
    end = jnp.cumsum(padded)
    start = end - padded
    tile_row = jnp.arange(N_SLOT_TILES, dtype=jnp.int32) * EXPERT_TM
    eid = jnp.minimum(jnp.sum(tile_row[:, None] >= end[None, :], axis=1), N_EXPERTS - 1).astype(jnp.int32)
    b1 = (start + row(N_CTX))[eid]
    b2 = (start + row(N_CTX + DEC_SEQ))[eid]
    nv = (end[-1:] // EXPERT_TM).astype(jnp.int32)
    experts = jnp.arange(N_EXPERTS, dtype=jnp.int32)
    start_of = lambda col: jnp.sum(jnp.where(info[:, col:col + 1].astype(jnp.int32) == experts, start, 0), axis=1)
    pos1 = start_of(INFO_E1) + info[:, INFO_R1].astype(jnp.int32)
    pos2 = start_of(INFO_E2) + info[:, INFO_R2].astype(jnp.int32)
    return pos1, pos2, (eid, b1, b2, nv)


def _even_mixer(x, sh, sc, gate, ln_g, ln_b, w_in, b_igate, b_fgate, ml_norm_g, q_norm_g, k_norm_g, w_out,
                st_c, st_n, st_m, cache_k, cache_v, rope_tabs):
    splits = (4 * ML_W, 4 * ML_W + N_GATES)
    w_main = jnp.concatenate([w_in[:, :splits[0]], w_in[:, splits[1]:]], axis=1).astype(BF16)
    proj = _mod_matmul(x, sh, sc, [w_main[None]], tm=1024, tn=MAIN_W // 2, out_dtype=F32, name="even_in_proj")[0]
    b_gate = jnp.stack([b_igate, b_fgate], axis=1).reshape(N_GATES)
    lic, bc, lir, br = _gates(x, sh, sc, w_in[:, splits[0]:splits[1]], b_gate)

    hf_c, hb_c, new_c, new_n, new_m = _mlstm(proj, lic, bc, lir, br, row0=0, n_seq=BATCH, seq_len=SEQ)
    init = (st_c, st_n, jnp.broadcast_to(st_m[..., None], st_n.shape))
    hf_s, hb_s, _, _, _ = _mlstm(proj, lic, bc, lir, br, row0=N_CTX, n_seq=DEC_BATCH, seq_len=DEC_SEQ, init=init)
    ml = _ml_post((hf_c, hf_s), (hb_c, hb_s), proj, ml_norm_g)

    q_c, kn_c, kb_c = _qk_prep(proj, q_norm_g, k_norm_g, row0=0, rows=N_CTX)
    q_s, _, kb_s = _qk_prep(proj, q_norm_g, k_norm_g, row0=N_CTX, rows=N_LAT, rope_tabs=rope_tabs)
    v_all = proj[:, MAIN_W - KV_W:]
    v_c, v_s = v_all[:N_CTX], v_all[N_CTX:]
    att_c = _attention(q_c, _head_major(kb_c, BATCH).swapaxes(2, 3),
                       _with_ones(_head_major(v_c.astype(BF16), BATCH)), tq=SEQ)
    k_lat = jnp.concatenate([kb_s.reshape(DEC_BATCH, DEC_SEQ, KV_W),
                             cache_k.reshape(DEC_BATCH, PAST_LEN, KV_W).astype(BF16)], axis=1)
    v_lat = jnp.concatenate([v_s.reshape(DEC_BATCH, DEC_SEQ, KV_W).astype(BF16),
                             cache_v.reshape(DEC_BATCH, PAST_LEN, KV_W).astype(BF16)], axis=1)
    att_s = _attention(q_s, _head_major(k_lat.reshape(-1, KV_W), DEC_BATCH).swapaxes(2, 3),
                       _with_ones(_head_major(v_lat.reshape(-1, KV_W), DEC_BATCH)), tq=256)

    x = _proj_res_ln([ml, (att_c, att_s)], w_out.astype(BF16), x, gate, ln_g, ln_b, tm=512, name="even_out_proj")
    new_k = kn_c.reshape(BATCH, SEQ, ATT_KV_HEADS, ATT_HEAD_DIM)
    new_v = v_c.reshape(BATCH, SEQ, ATT_KV_HEADS, ATT_HEAD_DIM)
    return x, new_k, new_v, new_c, new_n, new_m[..., 0]


def _hyena_mixer(x, sh, sc, gate, ln_g, ln_b, w_in, conv_w, conv_b, w1, b1, w2, b2, w3, sin_freq, skip, w_out, dft):
    u = _mod_matmul(x, sh, sc, [w_in.astype(BF16)[None]], tm=1024, tn=1536, out_dtype=F32, name="hyena_in_proj")[0]
    mats, consts = dft
    h_sum, h_diff, nyq = _hy_filter(SEQ, w1, b1, w2, b2, w3, sin_freq)
    kr, ki = _ctx_spectrum(mats, h_sum, h_diff, nyq)
    y_c = _hy_conv_ctx(u, conv_w, conv_b, kr, ki, skip, mats, seqs_per_step=8)
    h_sum, h_diff, _ = _hy_filter(DEC_SEQ, w1, b1, w2, b2, w3, sin_freq)
    kr, ki = _hy_spectrum(h_sum, h_diff, consts)
    y_s = _hy_conv_fft(u, conv_w, conv_b, kr, ki, skip, consts, row0=N_CTX, n_seq=DEC_BATCH)
    return _proj_res_ln([(y_c, y_s)], w_out.astype(BF16), x, gate, ln_g, ln_b, tm=512, name="hyena_out_proj")


def _dense_ffn(x, sh, sc, gate, ln_g, ln_b, w_gate, w_up, w_down):
    act = _mod_matmul(x, sh, sc, [w_gate.astype(BF16)[None], w_up.astype(BF16)[None]],
                      tm=1024, tn=D_FF // 2, out_dtype=BF16, name="ffn_swiglu")
    return _proj_res_ln([act[0]], w_down.astype(BF16), x, gate, ln_g, ln_b, tm=512, name="ffn_down")


def _moe_ffn(x, sh, sc, gate, ln_g, ln_b, w_router, w_gate, w_up, w_down, e0):
    info, incl = _router(x, sh, sc, w_router)
    pos1, pos2, meta = _slot_plan(info, incl)
    xs_rt = _dispatch(pos1, pos2, x)
    act = _expert_swiglu(meta, xs_rt, sh, sc, w_gate, w_up, e0)
    y_rt = _expert_down(meta, act, w_down, e0)
    return _combine(pos1, pos2, y_rt, info, x, gate, ln_g, ln_b)


def kernel(x_prompt, x_sample, cache_attn_k, cache_attn_v, state_mlstm_C, state_mlstm_n, state_mlstm_m, c, c_ctx, w_ada, b_ada, ln_g, ln_b, w_in_even, b_igate, b_fgate, ml_norm_g, q_norm_g, k_norm_g, w_out_even, w_ffn_gate, w_ffn_up, w_ffn_down, w_in_hy, hy_conv_w, hy_conv_b, hy_filt_w1, hy_filt_b1, hy_filt_w2, hy_filt_b2, hy_filt_w3, hy_sin_freq, hy_skip, w_out_hy, w_router, w_moe_gate, w_moe_up, w_moe_down):
    x = jnp.concatenate([x_prompt.reshape(N_CTX, D), x_sample.reshape(N_LAT, D)])
    cvec = jnp.concatenate([c_ctx[None], c, jnp.zeros((8 - 1 - DEC_BATCH, D), F32)])
    mods = _ada(cvec, w_ada, b_ada)
    rope_tabs = _rope_tables()
    dft = (_dft_matrices(SEQ), _fft_consts())
    moe_w = [w.reshape((-1,) + w.shape[2:]) for w in (w_moe_gate, w_moe_up, w_moe_down)]
    new_k, new_v, new_c, new_n, new_m = [], [], [], [], []
    for layer in range(DEPTH):
        sh1, sc1, g1, sh2, sc2, g2 = (mods[layer, :, i * D:(i + 1) * D].reshape(8, 1, D) for i in range(6))
        i = layer // 2
        if layer % 2 == 0:
            x, k_c, v_c, st_c, st_n, st_m = _even_mixer(
                x, sh1, sc1, g1, ln_g[layer, 0], ln_b[layer, 0], w_in_even[i], b_igate[i], b_fgate[i], ml_norm_g[i],
                q_norm_g[i], k_norm_g[i], w_out_even[i], state_mlstm_C[:, i], state_mlstm_n[:, i], state_mlstm_m[:, i],
                cache_attn_k[:, i], cache_attn_v[:, i], rope_tabs)
            new_k.append(k_c)
            new_v.append(v_c)
            new_c.append(st_c)
            new_n.append(st_n)
            new_m.append(st_m)
            x = _dense_ffn(x, sh2, sc2, g2, ln_g[layer, 1], ln_b[layer, 1], w_ffn_gate[i], w_ffn_up[i], w_ffn_down[i])
        else:
            x = _hyena_mixer(x, sh1, sc1, g1, ln_g[layer, 0], ln_b[layer, 0], w_in_hy[i], hy_conv_w[i], hy_conv_b[i],
                             hy_filt_w1[i], hy_filt_b1[i], hy_filt_w2[i], hy_filt_b2[i], hy_filt_w3[i], hy_sin_freq[i],
                             hy_skip[i], w_out_hy[i], dft)
            x = _moe_ffn(x, sh2, sc2, g2, ln_g[layer, 1], ln_b[layer, 1], w_router[i], *moe_w, i * N_EXPERTS)
    return (x[:N_CTX].reshape(BATCH, SEQ, D), x[N_CTX:].reshape(DEC_BATCH, DEC_SEQ, D),
            jnp.stack(new_k, axis=1), jnp.stack(new_v, axis=1), jnp.stack(new_c, axis=1),
            jnp.stack(new_n, axis=1), jnp.stack(new_m, axis=1))
```

```python
import functools
import math

import jax
import jax.numpy as jnp
import numpy as np
from jax import lax
from jax.experimental import pallas as pl
from jax.experimental.pallas import tpu as pltpu

F32 = jnp.float32
BF16 = jnp.bfloat16
HIGHEST = lax.Precision.HIGHEST

D = 1024
BATCH, SEQ = 32, 256
DEC_BATCH, DEC_SEQ = 2, 4096
DEPTH = 4
PAST_LEN = 256
GRID_W = 64
N_CTX = BATCH * SEQ
N_LAT = DEC_BATCH * DEC_SEQ
N_TOK = N_CTX + N_LAT

ML_HEADS, ML_HEAD_DIM = 4, 128
ML_W = ML_HEADS * ML_HEAD_DIM
CHUNK = 128
MLSTM_PAR = 2
ATT_HEADS, ATT_KV_HEADS, ATT_HEAD_DIM = 8, 2, 64
ATT_GROUP = ATT_HEADS // ATT_KV_HEADS
ATT_W = ATT_HEADS * ATT_HEAD_DIM
KV_W = ATT_KV_HEADS * ATT_HEAD_DIM
GROUP_W = ATT_GROUP * ATT_HEAD_DIM
ROPE_BASE = 10000.0
N_GATES = 4 * ML_HEADS
MAIN_W = 4 * ML_W + ATT_W + 2 * KV_W

HY_EMB = 33
HY_BANDS = (HY_EMB - 1) // 2
HY_TARGET, HY_SHORT_PCT, HY_LONG_PCT = 1e-2, 0.3, 1.5
D_FF = 2816
N_EXPERTS = 8
MOE_D_FF = 3584
ALPHA = (2 * DEPTH) ** 0.25
LN_EPS = 1e-5
RMS_EPS = 1e-6

LANES = 128
VMEM_LIMIT = 48 * 1024 * 1024


def _params(n_axes, vmem=VMEM_LIMIT):
    return pltpu.CompilerParams(dimension_semantics=("arbitrary",) * n_axes, vmem_limit_bytes=vmem)


def _group_of_row(r):
    return jnp.where(r < N_CTX, 0, 1 + (r - N_CTX) // DEC_SEQ)


def _modulate(x_ref, sh_ref, sc_ref):
    return x_ref[...] * (1.0 + sc_ref[0]) + sh_ref[0]


def _mod_specs(tm, row_axis):
    def rows(*ids):
        return (ids[row_axis], 0)

    def grp(*ids):
        return (_group_of_row(ids[row_axis] * tm), 0, 0)

    return [pl.BlockSpec((tm, D), rows), pl.BlockSpec((1, 1, D), grp), pl.BlockSpec((1, 1, D), grp)]


def _ada_body(c_ref, w_ref, b_ref, o_ref):
    c = c_ref[...]
    s = c * jax.nn.sigmoid(c)
    o_ref[0] = jnp.dot(s, w_ref[0], preferred_element_type=F32, precision=HIGHEST) + b_ref[0]


def _ada(cvec, w_ada, b_ada):
    tn = 1536
    return pl.pallas_call(
        _ada_body,
        out_shape=jax.ShapeDtypeStruct((DEPTH, 8, 6 * D), F32),
        grid=(DEPTH, 6 * D // tn),
        in_specs=[pl.BlockSpec((8, D), lambda l, j: (0, 0)),
                  pl.BlockSpec((1, D, tn), lambda l, j: (l, 0, j)),
                  pl.BlockSpec((1, 1, tn), lambda l, j: (l, 0, j))],
        out_specs=pl.BlockSpec((1, 8, tn), lambda l, j: (l, 0, j)),
        compiler_params=_params(2),
        name="ada_modulation",
    )(cvec, w_ada, b_ada.reshape(DEPTH, 1, 6 * D))


def _mod_mm_body(x_ref, sh_ref, sc_ref, *refs, n_w):
    w_refs, o_ref, h_ref = refs[:n_w], refs[n_w], refs[n_w + 1]

    @pl.when(pl.program_id(2) == 0)
    def _():
        h_ref[...] = _modulate(x_ref, sh_ref, sc_ref).astype(BF16)

    h = h_ref[...]
    if n_w == 1:
        o = jnp.dot(h, w_refs[0][0], preferred_element_type=F32)
    else:
        g = jnp.dot(h, w_refs[0][0], preferred_element_type=F32)
        u = jnp.dot(h, w_refs[1][0], preferred_element_type=F32)
        o = g * jax.nn.sigmoid(g) * u
    o_ref[0] = o.astype(o_ref.dtype)


def _mod_matmul(x, sh, sc, ws, *, tm, tn, out_dtype, name):
    n_e, _, f = ws[0].shape
    return pl.pallas_call(
        functools.partial(_mod_mm_body, n_w=len(ws)),
        out_shape=jax.ShapeDtypeStruct((n_e, N_TOK, f), out_dtype),
        grid=(n_e, N_TOK // tm, f // tn),
        in_specs=_mod_specs(tm, 1) + [pl.BlockSpec((1, D, tn), lambda e, i, j: (e, 0, j)) for _ in ws],
        out_specs=pl.BlockSpec((1, tm, tn), lambda e, i, j: (e, i, j)),
        scratch_shapes=[pltpu.VMEM((tm, D), BF16)],
        compiler_params=_params(3),
        name=name,
    )(x, sh, sc, *ws)


def _split_specs(tm, width):
    nc = N_CTX // tm
    return [pl.BlockSpec((tm, width), lambda i: (jnp.minimum(i, nc - 1), 0)),
            pl.BlockSpec((tm, width), lambda i: (jnp.maximum(i - nc, 0), 0))]


def _pick_split(ctx_ref, lat_ref):
    tm = ctx_ref.shape[0]
    return jnp.where(pl.program_id(0) < N_CTX // tm, ctx_ref[...], lat_ref[...])


def _proj_res_ln_body(*refs, split):
    n_in = sum(2 if s else 1 for s in split)
    part_refs = list(refs[:n_in])
    w_ref, x_ref, g_ref, lng_ref, lnb_ref, o_ref = refs[n_in:]
    cols = [_pick_split(part_refs.pop(0), part_refs.pop(0)) if s else part_refs.pop(0)[...] for s in split]
    a = cols[0] if len(cols) == 1 else jnp.concatenate(cols, axis=-1)
    y = ALPHA * x_ref[...] + g_ref[0] * jnp.dot(a, w_ref[...], preferred_element_type=F32)
    mu = jnp.mean(y, axis=-1, keepdims=True)
    yc = y - mu
    var = jnp.mean(yc * yc, axis=-1, keepdims=True)
    o_ref[...] = yc * lax.rsqrt(var + LN_EPS) * lng_ref[...] + lnb_ref[...]


def _proj_res_ln(parts, w, x, gate, ln_g, ln_b, *, tm, name):
    split = tuple(isinstance(p, tuple) for p in parts)
    in_specs, args = [], []
    for p, s in zip(parts, split):
        if s:
            in_specs += _split_specs(tm, p[0].shape[1])
            args += list(p)
        else:
            in_specs.append(pl.BlockSpec((tm, p.shape[1]), lambda i: (i, 0)))
            args.append(p)
    in_specs += [pl.BlockSpec(w.shape, lambda i: (0, 0)),
                 pl.BlockSpec((tm, D), lambda i: (i, 0)),
                 pl.BlockSpec((1, 1, D), lambda i: (_group_of_row(i * tm), 0, 0)),
                 pl.BlockSpec((1, D), lambda i: (0, 0)),
                 pl.BlockSpec((1, D), lambda i: (0, 0))]
    return pl.pallas_call(
        functools.partial(_proj_res_ln_body, split=split),
        out_shape=jax.ShapeDtypeStruct((N_TOK, D), F32),
        grid=(N_TOK // tm,),
        in_specs=in_specs,
        out_specs=pl.BlockSpec((tm, D), lambda i: (i, 0)),
        compiler_params=_params(1),
        name=name,
    )(*args, w, x, gate, ln_g.reshape(1, D), ln_b.reshape(1, D))


def _log_sigmoid(x):
    return jnp.minimum(x, 0.0) - jnp.log(1.0 + jnp.exp(-jnp.abs(x)))


def _gates_body(x_ref, sh_ref, sc_ref, wg_ref, wgt_ref, b_ref, bt_ref,
                lic_ref, bc_ref, lir_ref, br_ref, *, tm):
    h = _modulate(x_ref, sh_ref, sc_ref).astype(BF16)
    g = jnp.dot(h, wg_ref[...], preferred_element_type=F32) + b_ref[...]
    gt = lax.dot_general(wgt_ref[...], h, (((1,), (1,)), ((), ())), preferred_element_type=F32) + bt_ref[...]
    lic_ref[...] = g
    lir_ref[...] = gt
    lf, lft = _log_sigmoid(g), _log_sigmoid(gt)
    r = lax.broadcasted_iota(jnp.int32, (CHUNK, CHUNK), 0)
    c = lax.broadcasted_iota(jnp.int32, (CHUNK, CHUNK), 1)
    tri_l = (c <= r).astype(BF16)
    tri_u = (c >= r).astype(BF16)

    def pieces(v):
        out = []
        for _ in range(3):
            out.append(v.astype(BF16))
            v = v - out[-1].astype(F32)
        return out

    def cumsum(tri, v, tri_first):
        dots = [jnp.dot(tri, p, preferred_element_type=F32) if tri_first else jnp.dot(p, tri, preferred_element_type=F32)
                for p in pieces(v)]
        return dots[0] + dots[1] + dots[2]

    fwd_col = lax.broadcasted_iota(jnp.int32, (CHUNK, LANES), 1) < 2 * ML_HEADS
    fwd_row = lax.broadcasted_iota(jnp.int32, (N_GATES, CHUNK), 0) < 2 * ML_HEADS
    for ch in range(tm // CHUNK):
        sl = slice(ch * CHUNK, (ch + 1) * CHUNK)
        lfc, lftc = lf[sl, :], lft[:, sl]
        bc_ref[sl, :] = jnp.where(fwd_col, cumsum(tri_l, lfc, True), cumsum(tri_u, lfc, True))
        br_ref[:, sl] = jnp.where(fwd_row, cumsum(tri_u, lftc, False), cumsum(tri_l, lftc, False))


def _gates(x, sh, sc, wg, b_gate):
    tm = 512
    wg_pad = jnp.pad(wg, ((0, 0), (0, LANES - N_GATES)))
    b_pad = jnp.pad(b_gate, (0, LANES - N_GATES)).reshape(1, LANES)
    col = pl.BlockSpec((tm, LANES), lambda i: (i, 0))
    row = pl.BlockSpec((N_GATES, tm), lambda i: (0, i))
    return pl.pallas_call(
        functools.partial(_gates_body, tm=tm),
        out_shape=(jax.ShapeDtypeStruct((N_TOK, LANES), F32), jax.ShapeDtypeStruct((N_TOK, LANES), F32),
                   jax.ShapeDtypeStruct((N_GATES, N_TOK), F32), jax.ShapeDtypeStruct((N_GATES, N_TOK), F32)),
        grid=(N_TOK // tm,),
        in_specs=_mod_specs(tm, 0) + [pl.BlockSpec((D, LANES), lambda i: (0, 0)),
                                         pl.BlockSpec((N_GATES, D), lambda i: (0, 0)),
                                         pl.BlockSpec((1, LANES), lambda i: (0, 0)),
                                         pl.BlockSpec((N_GATES, 1), lambda i: (0, 0))],
        out_specs=(col, col, row, row),
        compiler_params=_params(1),
        name="mlstm_gates",
    )(x, sh, sc, wg_pad.astype(BF16), wg.T.astype(BF16), b_pad, b_gate.reshape(N_GATES, 1))


def _mlstm_body(*refs, has_init):
    (qf, kf, vf, licf, bcf, lirf, brf, qb, kb, vb, licb, bcb, lirb, brb) = refs[:14]
    refs = refs[14:]
    if has_init:
        c0_ref, n0_ref, m0_ref = refs[:3]
        refs = refs[3:]
    hf_ref, hb_ref, c_ref, n_ref, m_ref = refs

    @pl.when(pl.program_id(1) == 0)
    def _():
        if has_init:
            c_ref[...] = c0_ref[...]
            n_ref[...] = n0_ref[...]
            m_ref[...] = m0_ref[...]
        else:
            c_ref[...] = jnp.zeros_like(c_ref)
            n_ref[...] = jnp.zeros_like(n_ref)
            m_ref[...] = jnp.zeros_like(m_ref)

    t_idx = lax.broadcasted_iota(jnp.int32, (CHUNK, CHUNK), 0)
    s_idx = lax.broadcasted_iota(jnp.int32, (CHUNK, CHUNK), 1)
    nt = (((1,), (1,)), ((), ()))
    stores = []

    def chain(u, d, h, q_ref, k_ref, v_ref, lic_ref, bc_ref, lir_ref, br_ref, h_ref):
        mask = (s_idx <= t_idx) if d == 0 else (s_idx >= t_idx)
        hs = slice(h * ML_HEAD_DIM, (h + 1) * ML_HEAD_DIM)
        gi, gf = d * 2 * ML_HEADS + h, d * 2 * ML_HEADS + ML_HEADS + h
        q = q_ref[u, :, hs]
        k = k_ref[u, :, hs] * (ML_HEAD_DIM ** -0.5)
        v = v_ref[u, :, hs]
        qh, kh, vh = q.astype(BF16), k.astype(BF16), v.astype(BF16)
        li_c, b_c = lic_ref[u, :, gi:gi + 1], bc_ref[u, :, gf:gf + 1]
        li_r, b_r = lir_ref[u, gi:gi + 1, :], br_ref[u, gf:gf + 1, :]
        c_st = c_ref[u, d, h]
        n_st = n_ref[u, d, h:h + 1, :]
        m_st = m_ref[u, d, h:h + 1, :][:, 0:1]
        dmat = jnp.where(mask, b_c - b_r + li_r, -jnp.inf)
        inter = b_c + m_st
        m_out = jnp.maximum(inter, jnp.max(dmat, axis=-1, keepdims=True))
        p = jnp.exp(dmat - m_out)
        w_inter = jnp.exp(inter - m_out)
        yield
        qk = lax.dot_general(qh, kh, nt, preferred_element_type=F32)
        qc = jnp.dot(qh, c_st.astype(BF16), preferred_element_type=F32)
        yield
        s = qk * p
        den = (jnp.sum(s, axis=-1, keepdims=True)
               + w_inter * jnp.sum(q * n_st, axis=-1, keepdims=True))
        sh = s.astype(BF16)
        b_last = b_r[:, CHUNK - 1:CHUNK] if d == 0 else b_r[:, 0:1]
        g_r = b_last - b_r + li_r
        g_c = b_last - b_c + li_c
        m_new = jnp.maximum(b_last + m_st, jnp.max(g_r, axis=-1, keepdims=True))
        decay = jnp.exp(b_last + m_st - m_new)
        kw = k * jnp.exp(g_c - m_new)
        kwh = kw.astype(BF16)
        yield
        sv = jnp.dot(sh, vh, preferred_element_type=F32)
        kv = lax.dot_general(kwh, vh, (((0,), (0,)), ((), ())), preferred_element_type=F32)
        yield
        h_out = (sv + w_inter * qc) / jnp.maximum(jnp.abs(den), jnp.exp(-m_out))
        c_new = decay * c_st + kv
        n_new = decay * n_st + jnp.sum(kw, axis=0, keepdims=True)
        stores.append((h_ref, u, d, h, hs, h_out, c_new, n_new, jnp.broadcast_to(m_new, (1, ML_HEAD_DIM))))
        yield

    chains = [chain(u, d, h, *group)
              for u in range(c_ref.shape[0])
              for d, group in enumerate(((qf, kf, vf, licf, bcf, lirf, brf, hf_ref),
                                         (qb, kb, vb, licb, bcb, lirb, brb, hb_ref)))
              for h in range(ML_HEADS)]
    for _ in range(5):
        for ch in chains:
            next(ch)
    for h_ref, u, d, h, hs, h_out, c_new, n_new, m_new in stores:
        h_ref[u, :, hs] = h_out
        c_ref[u, d, h] = c_new
        n_ref[u, d, h:h + 1, :] = n_new
        m_ref[u, d, h:h + 1, :] = m_new


def _mlstm(proj, lic, bc, lir, br, *, row0, n_seq, seq_len, init=None):
    nc = seq_len // CHUNK
    par = MLSTM_PAR
    g0 = row0 // seq_len // par
    seqs = lambda a: a.reshape(N_TOK // seq_len, seq_len, a.shape[-1])
    rows_of = lambda a: seqs(a.T).transpose(0, 2, 1)

    def chunk_specs(chunk):
        return ([pl.BlockSpec((par, CHUNK, ML_W), lambda b, j, c=c: (g0 + b, chunk(j), c)) for c in range(3)]
                + [pl.BlockSpec((par, CHUNK, LANES), lambda b, j: (g0 + b, chunk(j), 0))] * 2
                + [pl.BlockSpec((par, N_GATES, CHUNK), lambda b, j: (g0 + b, 0, chunk(j)))] * 2)

    fwd, bwd = (lambda j: j), (lambda j: nc - 1 - j)
    st_c = pl.BlockSpec((par, 2, ML_HEADS, ML_HEAD_DIM, ML_HEAD_DIM), lambda b, j: (b, 0, 0, 0, 0))
    st_n = pl.BlockSpec((par, 2, ML_HEADS, ML_HEAD_DIM), lambda b, j: (b, 0, 0, 0))
    in_specs = chunk_specs(fwd) + chunk_specs(bwd)
    args = [seqs(proj)] * 3 + [seqs(lic), seqs(bc), rows_of(lir), rows_of(br)]
    args = args * 2
    if init is not None:
        in_specs += [st_c, st_n, st_n]
        args += list(init)
    h_shape = jax.ShapeDtypeStruct((n_seq, seq_len, ML_W), F32)
    hf, hb, c_st, n_st, m_st = pl.pallas_call(
        functools.partial(_mlstm_body, has_init=init is not None),
        out_shape=(h_shape, h_shape,
                   jax.ShapeDtypeStruct((n_seq, 2, ML_HEADS, ML_HEAD_DIM, ML_HEAD_DIM), F32),
                   jax.ShapeDtypeStruct((n_seq, 2, ML_HEADS, ML_HEAD_DIM), F32),
                   jax.ShapeDtypeStruct((n_seq, 2, ML_HEADS, ML_HEAD_DIM), F32)),
        grid=(n_seq // par, nc),
        in_specs=in_specs,
        out_specs=(pl.BlockSpec((par, CHUNK, ML_W), lambda b, j: (b, fwd(j), 0)),
                   pl.BlockSpec((par, CHUNK, ML_W), lambda b, j: (b, bwd(j), 0)),
                   st_c, st_n, st_n),
        compiler_params=_params(2),
        name="mlstm_scan",
    )(*args)
    return hf.reshape(-1, ML_W), hb.reshape(-1, ML_W), c_st, n_st, m_st


def _ml_post_body(hfc_ref, hfs_ref, hbc_ref, hbs_ref, o_ref, g_ref, out_ref):
    h = _pick_split(hfc_ref, hfs_ref) + _pick_split(hbc_ref, hbs_ref)
    gate = jax.nn.sigmoid(o_ref[...]) * g_ref[...]
    for hd in range(ML_HEADS):
        hs = slice(hd * ML_HEAD_DIM, (hd + 1) * ML_HEAD_DIM)
        x = h[:, hs]
        xc = x - jnp.mean(x, axis=-1, keepdims=True)
        var = jnp.mean(xc * xc, axis=-1, keepdims=True)
        out_ref[:, hs] = (gate[:, hs] * (xc * lax.rsqrt(var + RMS_EPS))).astype(BF16)


def _ml_post(hf, hb, proj, norm_g):
    tm = 512
    blk = pl.BlockSpec((tm, ML_W), lambda i: (i, 0))
    return pl.pallas_call(
        _ml_post_body,
        out_shape=jax.ShapeDtypeStruct((N_TOK, ML_W), BF16),
        grid=(N_TOK // tm,),
        in_specs=(_split_specs(tm, ML_W) * 2
                  + [pl.BlockSpec((tm, ML_W), lambda i: (i, 3)), pl.BlockSpec((1, ML_W), lambda i: (0, 0))]),
        out_specs=blk,
        compiler_params=_params(1),
        name="mlstm_out_norm",
    )(*hf, *hb, proj, norm_g.reshape(1, ML_W))


def _head_rms(x, gain):
    lane_head = lax.broadcasted_iota(jnp.int32, x.shape, 1) // ATT_HEAD_DIM
    sq = x * x
    ms = jnp.zeros_like(x)
    for hd in range(x.shape[1] // ATT_HEAD_DIM):
        sel = lane_head == hd
        ms = jnp.where(sel, jnp.sum(jnp.where(sel, sq, 0.0), axis=-1, keepdims=True), ms)
    return x * lax.rsqrt(ms * (1.0 / ATT_HEAD_DIM) + RMS_EPS) * gain


def _rope(x, cos, sin_signed):
    w = x.shape[1]
    even = lax.broadcasted_iota(jnp.int32, x.shape, 1) % 2 == 0
    partner = jnp.where(even, pltpu.roll(x, w - 1, 1), pltpu.roll(x, 1, 1))
    return x * cos + partner * sin_signed


def _qk_prep_body(q_ref, k_ref, qg_ref, kg_ref, *refs, rope):
    if rope:
        cq_ref, sq_ref, ck_ref, sk_ref, qo_ref, kn_ref, kr_ref = refs
    else:
        qo_ref, kn_ref, kr_ref = refs
    q = _head_rms(q_ref[...], qg_ref[...])
    k = _head_rms(k_ref[...], kg_ref[...])
    kn_ref[...] = k
    if rope:
        q = _rope(q, cq_ref[...], sq_ref[...])
        k = _rope(k, ck_ref[...], sk_ref[...])
    qo_ref[...] = (q * (ATT_HEAD_DIM ** -0.5)).astype(BF16)
    kr_ref[...] = k.astype(BF16)


def _qk_prep(proj, q_gain, k_gain, *, row0, rows, rope_tabs=None):
    tm = 512
    r0 = row0 // tm
    in_specs = [pl.BlockSpec((tm, ATT_W), lambda i: (r0 + i, 4 * ML_W // ATT_W)),
                pl.BlockSpec((tm, KV_W), lambda i: (r0 + i, (4 * ML_W + ATT_W) // KV_W)),
                pl.BlockSpec((1, ATT_W), lambda i: (0, 0)),
                pl.BlockSpec((1, KV_W), lambda i: (0, 0))]
    args = [proj, proj, jnp.tile(q_gain, ATT_HEADS).reshape(1, ATT_W), jnp.tile(k_gain, ATT_KV_HEADS).reshape(1, KV_W)]
    if rope_tabs is not None:
        per_seq = DEC_SEQ // tm
        in_specs += [pl.BlockSpec((tm, ATT_W), lambda i: (i % per_seq, 0))] * 2
        in_specs += [pl.BlockSpec((tm, KV_W), lambda i: (i % per_seq, 0))] * 2
        args += list(rope_tabs)
    return pl.pallas_call(
        functools.partial(_qk_prep_body, rope=rope_tabs is not None),
        out_shape=(jax.ShapeDtypeStruct((rows, ATT_W), BF16), jax.ShapeDtypeStruct((rows, KV_W), F32),
                   jax.ShapeDtypeStruct((rows, KV_W), BF16)),
        grid=(rows // tm,),
        in_specs=in_specs,
        out_specs=(pl.BlockSpec((tm, ATT_W), lambda i: (i, 0)), pl.BlockSpec((tm, KV_W), lambda i: (i, 0)),
                   pl.BlockSpec((tm, KV_W), lambda i: (i, 0))),
        compiler_params=_params(1),
        name="attn_qk_prep",
    )(*args)


def _rope_tables():
    rows = DEC_SEQ // GRID_W
    axis_dim = ATT_HEAD_DIM // 2
    row = jnp.repeat(jnp.arange(rows, dtype=F32), GRID_W)
    col = (jnp.arange(DEC_SEQ) % GRID_W).astype(F32)
    inv = ROPE_BASE ** (-jnp.arange(axis_dim // 2, dtype=F32) * 2.0 / axis_dim)
    ang = jnp.concatenate([row[:, None] * inv, col[:, None] * inv], axis=-1)
    cos = jnp.repeat(jnp.cos(ang), 2, axis=-1)
    sin = jnp.repeat(jnp.sin(ang), 2, axis=-1) * jnp.tile(jnp.array([-1.0, 1.0], F32), axis_dim)
    return (jnp.tile(cos, (1, ATT_HEADS)), jnp.tile(sin, (1, ATT_HEADS)),
            jnp.tile(cos, (1, ATT_KV_HEADS)), jnp.tile(sin, (1, ATT_KV_HEADS)))


def _attn_body(q_ref, k_ref, v_ref, o_ref):
    k, v = k_ref[0, 0], v_ref[0, 0]
    dh = k.shape[0]
    g = q_ref.shape[1] // dh
    pair = 2

    def heads(h0):
        q = jnp.concatenate([q_ref[:, h * dh:(h + 1) * dh] for h in range(h0, h0 + pair)], axis=0)
        s = jnp.dot(q, k, preferred_element_type=F32)
        yield
        e = jnp.exp((s - jnp.max(s, axis=-1, keepdims=True)).astype(BF16))
        yield
        o = jnp.dot(e, v, preferred_element_type=F32)
        o = (o[:, :dh] / o[:, dh:dh + 1]).astype(BF16)
        tq = q_ref.shape[0]
        for j in range(pair):
            o_ref[:, (h0 + j) * dh:(h0 + j + 1) * dh] = o[j * tq:(j + 1) * tq]
        yield

    chains = [heads(h0) for h0 in range(0, g, pair)]
    for step in range(len(chains) + 2):
        for i, chain in enumerate(chains):
            if 0 <= step - i < 3:
                next(chain)


def _attention(q, k_t, v_ones, *, tq):
    n_seq, _, dh, s_len = k_t.shape
    nq = q.shape[0] // n_seq // tq
    qo = pl.BlockSpec((tq, GROUP_W), lambda b, kh, i: (b * nq + i, kh))
    return pl.pallas_call(
        _attn_body,
        out_shape=jax.ShapeDtypeStruct(q.shape, BF16),
        grid=(n_seq, ATT_KV_HEADS, nq),
        in_specs=[qo, pl.BlockSpec((1, 1, dh, s_len), lambda b, kh, i: (b, kh, 0, 0)),
                  pl.BlockSpec((1, 1, s_len, LANES), lambda b, kh, i: (b, kh, 0, 0))],
        out_specs=qo,
        compiler_params=_params(3),
        name="attention",
    )(q, k_t, v_ones)


def _head_major(x, n_seq):
    return x.reshape(n_seq, -1, x.shape[1] // ATT_HEAD_DIM, ATT_HEAD_DIM).transpose(0, 2, 1, 3)


def _with_ones(v):
    pad = jnp.zeros(v.shape[:-1] + (LANES - ATT_HEAD_DIM - 1,), v.dtype)
    return jnp.concatenate([v, jnp.ones(v.shape[:-1] + (1,), v.dtype), pad], axis=-1)


def _hy_filter_body(feat_ref, t_ref, w1_ref, b1_ref, w2_ref, b2_ref, fr_ref, w3f_ref, w3b_ref, dl_ref,
                    hsum_ref, hdiff_ref, nyq_ref, z_ref):
    @pl.when(pl.program_id(0) == 0)
    def _():
        z = jnp.dot(feat_ref[...], w1_ref[...], preferred_element_type=F32, precision=HIGHEST) + b1_ref[...]
        z = jnp.sin(fr_ref[0:1, :] * z)
        z = jnp.dot(z, w2_ref[...], preferred_element_type=F32, precision=HIGHEST) + b2_ref[...]
        z_ref[...] = jnp.sin(fr_ref[1:2, :] * z)

    z = z_ref[...].astype(BF16)
    window = jnp.exp(-t_ref[...] * dl_ref[...])
    h_f = jnp.dot(z, w3f_ref[...].astype(BF16), preferred_element_type=F32) * window
    h_b = jnp.dot(z, w3b_ref[...].astype(BF16), preferred_element_type=F32) * window
    row = lax.broadcasted_iota(jnp.int32, h_f.shape, 0)
    h_b = jnp.where(row == 0, 0.0, h_b)
    inv = 1.0 / (jnp.sum(jnp.abs(h_f), axis=0, keepdims=True) + jnp.sum(jnp.abs(h_b), axis=0, keepdims=True))
    h_sum = (h_f + h_b) * inv
    hsum_ref[...] = h_sum
    hdiff_ref[...] = (h_f - h_b) * inv
    nyq_ref[...] = jnp.sum(jnp.where(row % 2 == 0, h_sum, -h_sum), axis=0, keepdims=True)


def _hy_filter(seq_len, w1, b1, w2, b2, w3, sin_freq):
    tc = 256
    fw = w1.shape[1]
    t = jnp.arange(seq_len, dtype=F32)[:, None] / seq_len
    bands = jnp.arange(1, HY_BANDS + 1, dtype=F32)[None, :]
    feat = jnp.concatenate([t, jnp.sin(2.0 * math.pi * bands * t), jnp.cos(2.0 * math.pi * bands * t)], axis=-1)
    feat = jnp.pad(feat, ((0, 0), (0, LANES - HY_EMB)))
    deltas = jnp.abs(jnp.linspace(math.log(HY_TARGET) / HY_LONG_PCT, math.log(HY_TARGET) / HY_SHORT_PCT, D,
                                  dtype=F32)).reshape(1, D)
    pad_w = LANES - fw
    full = lambda shape: pl.BlockSpec(shape, lambda j: (0,) * len(shape))
    return pl.pallas_call(
        _hy_filter_body,
        out_shape=(jax.ShapeDtypeStruct((seq_len, D), F32), jax.ShapeDtypeStruct((seq_len, D), F32),
                   jax.ShapeDtypeStruct((1, D), F32)),
        grid=(D // tc,),
        in_specs=[full((seq_len, LANES)), full((seq_len, 1)), full((LANES, LANES)), full((1, LANES)),
                  full((LANES, LANES)), full((1, LANES)), full((2, LANES)),
                  pl.BlockSpec((LANES, tc), lambda j: (0, j)), pl.BlockSpec((LANES, tc), lambda j: (0, D // tc + j)),
                  pl.BlockSpec((1, tc), lambda j: (0, j))],
        out_specs=(pl.BlockSpec((seq_len, tc), lambda j: (0, j)), pl.BlockSpec((seq_len, tc), lambda j: (0, j)),
                   pl.BlockSpec((1, tc), lambda j: (0, j))),
        scratch_shapes=[pltpu.VMEM((seq_len, LANES), F32)],
        compiler_params=_params(1),
        name="hyena_filter",
    )(feat, t, jnp.pad(w1, ((0, LANES - HY_EMB), (0, pad_w))), jnp.pad(b1, (0, pad_w)).reshape(1, LANES),
      jnp.pad(w2, ((0, pad_w), (0, pad_w))), jnp.pad(b2, (0, pad_w)).reshape(1, LANES),
      jnp.pad(sin_freq, ((0, 0), (0, pad_w))), jnp.pad(w3, ((0, pad_w), (0, 0))), jnp.pad(w3, ((0, pad_w), (0, 0))),
      deltas)


def _dft_matrices(seq_len):
    n = 2 * seq_len
    k = lax.broadcasted_iota(jnp.int32, (seq_len, seq_len), 0)
    t = lax.broadcasted_iota(jnp.int32, (seq_len, seq_len), 1)
    ang = ((k * t) % n).astype(F32) * (2.0 * math.pi / n)
    cr, base = jnp.cos(ang), -jnp.sin(ang)
    ci = jnp.where(k == 0, (1 - 2 * (t % 2)).astype(F32), base)
    cit = jnp.where(t == 0, (1 - 2 * (k % 2)).astype(F32), base)
    return cr.astype(BF16), ci.astype(BF16), cit.astype(BF16)


def _ctx_spectrum_body(cr_ref, ci_ref, hs_ref, hd_ref, nyq_ref, kr_ref, ki_ref):
    kr_ref[...] = jnp.dot(cr_ref[...], hs_ref[...].astype(BF16), preferred_element_type=F32)
    ki = jnp.dot(ci_ref[...], hd_ref[...].astype(BF16), preferred_element_type=F32)
    first = lax.broadcasted_iota(jnp.int32, ki.shape, 0) == 0
    ki_ref[...] = jnp.where(first, nyq_ref[...], ki)


def _ctx_spectrum(mats, h_sum, h_diff, nyq):
    tn = 512
    cr, ci, _ = mats
    mat = pl.BlockSpec((SEQ, SEQ), lambda c: (0, 0))
    chan = pl.BlockSpec((SEQ, tn), lambda c: (0, c))
    shape = jax.ShapeDtypeStruct((SEQ, D), F32)
    return pl.pallas_call(
        _ctx_spectrum_body,
        out_shape=(shape, shape),
        grid=(D // tn,),
        in_specs=[mat, mat, chan, chan, pl.BlockSpec((1, tn), lambda c: (0, c))],
        out_specs=(chan, chan),
        compiler_params=_params(1),
        name="hyena_filter_dft",
    )(cr, ci, h_sum, h_diff, nyq)


def _hy_conv_ctx_body(x0_ref, x1_ref, v_ref, w0_ref, w1_ref, wv_ref, b0_ref, b1_ref, bv_ref, kr_ref, ki_ref,
                      skip_ref, cr_ref, ci_ref, ct_ref, o_ref):
    n = x0_ref.shape[0]
    pos = lax.broadcasted_iota(jnp.int32, x0_ref.shape, 0) % SEQ

    def conv(u_ref, w_ref, b_ref):
        u = u_ref[...]
        prev = jnp.where(pos == 0, 0.0, pltpu.roll(u, 1, 0))
        nxt = jnp.where(pos == SEQ - 1, 0.0, pltpu.roll(u, n - 1, 0))
        return prev * w_ref[0:1, :] + u * w_ref[1:2, :] + nxt * w_ref[2:3, :] + b_ref[...]

    z = conv(v_ref, wv_ref, bv_ref) * conv(x1_ref, w1_ref, b1_ref)
    gated = z * skip_ref[...]
    x0 = conv(x0_ref, w0_ref, b0_ref)
    kr, ki = kr_ref[...], ki_ref[...]
    first = lax.broadcasted_iota(jnp.int32, kr.shape, 0) == 0
    for s in range(n // SEQ):
        rows = slice(s * SEQ, (s + 1) * SEQ)
        zs = z[rows].astype(BF16)
        zr = jnp.dot(cr_ref[...], zs, preferred_element_type=F32)
        zi = jnp.dot(ci_ref[...], zs, preferred_element_type=F32)
        yr = jnp.where(first, 0.5 * zr * kr, zr * kr - zi * ki).astype(BF16)
        yi = jnp.where(first, 0.5 * zi * ki, zr * ki + zi * kr).astype(BF16)
        y = (jnp.dot(cr_ref[...], yr, preferred_element_type=F32)
             + jnp.dot(ct_ref[...], yi, preferred_element_type=F32))
        o_ref[rows, :] = ((y * (1.0 / SEQ) + gated[rows]) * x0[rows]).astype(BF16)


def _hy_conv_ctx(u, conv_w, conv_b, kr, ki, skip, mats, *, seqs_per_step):
    tc = 256
    nb = D // tc
    rows = seqs_per_step * SEQ

    def col(part):
        return [pl.BlockSpec((rows, tc), lambda b, c: (b, part * nb + c)),
                pl.BlockSpec((3, tc), lambda b, c: (0, part * nb + c)),
                pl.BlockSpec((1, tc), lambda b, c: (0, part * nb + c))]

    specs = [col(p) for p in range(3)]
    chan = pl.BlockSpec((SEQ, tc), lambda b, c: (0, c))
    mat = pl.BlockSpec((SEQ, SEQ), lambda b, c: (0, 0))
    cb = conv_b.reshape(1, 3 * D)
    return pl.pallas_call(
        _hy_conv_ctx_body,
        out_shape=jax.ShapeDtypeStruct((N_CTX, D), BF16),
        grid=(BATCH // seqs_per_step, nb),
        in_specs=([s[0] for s in specs] + [s[1] for s in specs] + [s[2] for s in specs]
                  + [chan, chan, pl.BlockSpec((1, tc), lambda b, c: (0, c)), mat, mat, mat]),
        out_specs=pl.BlockSpec((rows, tc), lambda b, c: (b, c)),
        compiler_params=_params(2),
        name="hyena_conv_ctx",
    )(u, u, u, conv_w, conv_w, conv_w, cb, cb, cb, kr, ki, skip.reshape(1, D), *mats)


FFT_A, FFT_R = 64, 64
FFT_M = 2 * FFT_A
FFT_H = FFT_R // 2
assert FFT_A * FFT_R == DEC_SEQ


SUB = 8
FFT_BLK = 2 * FFT_M * SUB


def _fft_consts():
    n = 2 * DEC_SEQ
    th = 2.0 * np.pi * (np.arange(FFT_M)[:, None] + 0.5) * np.arange(FFT_A)[None, :] / FFT_M
    f1 = np.concatenate([np.cos(th), -np.sin(th)], axis=0)
    k = np.arange(FFT_M)[:, None, None] + FFT_M * np.arange(FFT_H)[None, :, None] + 0.5
    ph = 2.0 * np.pi * k * np.arange(FFT_R)[None, None, :] / n
    c, s = np.cos(ph), np.sin(ph)
    g = np.concatenate([np.concatenate([c, s], axis=2), np.concatenate([-s, c], axis=2)], axis=1)
    as_bf16 = lambda m: jnp.asarray(m, dtype=F32).astype(BF16)
    return as_bf16(np.kron(f1, np.eye(SUB))), as_bf16(g)


def _fft_stage1(src_ref, y_ref, f1k):
    for b1 in range(FFT_R // SUB):
        x = jnp.concatenate([src_ref[a * FFT_R + b1 * SUB:a * FFT_R + (b1 + 1) * SUB, :] for a in range(FFT_A)],
                            axis=0)
        y_ref[b1 * FFT_BLK:(b1 + 1) * FFT_BLK, :] = jnp.dot(f1k, x.astype(BF16), preferred_element_type=F32)


def _fft_rows(k1):
    return [b1 * FFT_BLK + ri * FFT_M * SUB + k1 * SUB for ri in range(2) for b1 in range(FFT_R // SUB)]


def _fft_stage2(y_ref, g_ref, k1):
    y = jnp.concatenate([y_ref[r:r + SUB, :] for r in _fft_rows(k1)], axis=0).astype(BF16)
    z = jnp.dot(g_ref[k1], y, preferred_element_type=F32)
    return z[:FFT_H], z[FFT_H:]


def _hy_spectrum_body(hs_ref, hd_ref, f1_ref, g_ref, kr_ref, ki_ref, h_ref, y_ref):
    tc = hs_ref.shape[1]
    h_ref[:, :tc] = hs_ref[...]
    h_ref[:, tc:] = hd_ref[...]
    _fft_stage1(h_ref, y_ref, f1_ref[...])
    for k1 in range(FFT_M):
        zr, zi = _fft_stage2(y_ref, g_ref, k1)
        kr_ref[k1 * FFT_H:(k1 + 1) * FFT_H, :] = zr[:, :tc]
        ki_ref[k1 * FFT_H:(k1 + 1) * FFT_H, :] = zi[:, tc:]


def _hy_spectrum(h_sum, h_diff, consts):
    tc = 128
    f1k, g = consts
    blk = pl.BlockSpec((DEC_SEQ, tc), lambda c: (0, c))
    shape = jax.ShapeDtypeStruct((DEC_SEQ, D), F32)
    return pl.pallas_call(
        _hy_spectrum_body,
        out_shape=(shape, shape),
        grid=(D // tc,),
        in_specs=[blk, blk, pl.BlockSpec(f1k.shape, lambda c: (0, 0)), pl.BlockSpec(g.shape, lambda c: (0, 0, 0))],
        out_specs=(blk, blk),
        scratch_shapes=[pltpu.VMEM((DEC_SEQ, 2 * tc), F32), pltpu.VMEM((FFT_R * 2 * FFT_M, 2 * tc), F32)],
        compiler_params=_params(1),
        name="hyena_filter_fft",
    )(h_sum, h_diff, f1k, g)


def _hy_conv_fft_body(x0_ref, x1_ref, v_ref, w0_ref, w1_ref, wv_ref, b0_ref, b1_ref, bv_ref, kr_ref, ki_ref,
                      skip_ref, f1_ref, g_ref, o_ref, z_ref, y_ref, t_ref):
    rows = 512
    tn = (((0,), (0,)), ((), ()))

    def conv(u_ref, w_ref, b_ref, r):
        u = u_ref[r:r + rows, :]
        row = lax.broadcasted_iota(jnp.int32, u.shape, 0)
        before = u_ref[r - 1:r, :] if r > 0 else jnp.zeros_like(u[0:1])
        after = u_ref[r + rows:r + rows + 1, :] if r + rows < DEC_SEQ else jnp.zeros_like(u[0:1])
        prev = jnp.where(row == 0, before, pltpu.roll(u, 1, 0))
        nxt = jnp.where(row == rows - 1, after, pltpu.roll(u, rows - 1, 0))
        return prev * w_ref[0:1, :] + u * w_ref[1:2, :] + nxt * w_ref[2:3, :] + b_ref[...]

    for r in range(0, DEC_SEQ, rows):
        z_ref[r:r + rows, :] = conv(v_ref, wv_ref, bv_ref, r) * conv(x1_ref, w1_ref, b1_ref, r)
    f1k = f1_ref[...]
    _fft_stage1(z_ref, y_ref, f1k)
    for k1 in range(FFT_M):
        zr, zi = _fft_stage2(y_ref, g_ref, k1)
        kr = kr_ref[k1 * FFT_H:(k1 + 1) * FFT_H, :]
        ki = ki_ref[k1 * FFT_H:(k1 + 1) * FFT_H, :]
        p = jnp.concatenate([zr * kr - zi * ki, zr * ki + zi * kr], axis=0).astype(BF16)
        u = lax.dot_general(g_ref[k1], p, tn, preferred_element_type=F32)
        for j, r in enumerate(_fft_rows(k1)):
            y_ref[r:r + SUB, :] = u[j * SUB:(j + 1) * SUB]
    for b1 in range(FFT_R // SUB):
        blk = y_ref[b1 * FFT_BLK:(b1 + 1) * FFT_BLK, :].astype(BF16)
        yb = lax.dot_general(f1k, blk, tn, preferred_element_type=F32)
        for a in range(FFT_A):
            t_ref[a * FFT_R + b1 * SUB:a * FFT_R + (b1 + 1) * SUB, :] = yb[a * SUB:(a + 1) * SUB]
    for r in range(0, DEC_SEQ, rows):
        y = t_ref[r:r + rows, :] * (1.0 / DEC_SEQ) + z_ref[r:r + rows, :] * skip_ref[...]
        o_ref[r:r + rows, :] = (y * conv(x0_ref, w0_ref, b0_ref, r)).astype(BF16)


def _hy_conv_fft(u, conv_w, conv_b, kr, ki, skip, consts, *, row0, n_seq):
    tc = 128
    nb = D // tc
    r0 = row0 // DEC_SEQ

    def col(part):
        return [pl.BlockSpec((DEC_SEQ, tc), lambda b, c: (r0 + b, part * nb + c)),
                pl.BlockSpec((3, tc), lambda b, c: (0, part * nb + c)),
                pl.BlockSpec((1, tc), lambda b, c: (0, part * nb + c))]

    specs = [col(p) for p in range(3)]
    chan = pl.BlockSpec((DEC_SEQ, tc), lambda b, c: (0, c))
    const = lambda m: pl.BlockSpec(m.shape, lambda b, c: (0,) * m.ndim)
    cb = conv_b.reshape(1, 3 * D)
    return pl.pallas_call(
        _hy_conv_fft_body,
        out_shape=jax.ShapeDtypeStruct((n_seq * DEC_SEQ, D), BF16),
        grid=(n_seq, nb),
        in_specs=([s[0] for s in specs] + [s[1] for s in specs] + [s[2] for s in specs]
                  + [chan, chan, pl.BlockSpec((1, tc), lambda b, c: (0, c))] + [const(m) for m in consts]),
        out_specs=pl.BlockSpec((DEC_SEQ, tc), lambda b, c: (b, c)),
        scratch_shapes=[pltpu.VMEM((DEC_SEQ, tc), F32), pltpu.VMEM((FFT_R * 2 * FFT_M, tc), F32),
                        pltpu.VMEM((DEC_SEQ, tc), F32)],
        compiler_params=_params(2, 56 * 1024 * 1024),
        name="hyena_conv_fft",
    )(u, u, u, conv_w, conv_w, conv_w, cb, cb, cb, kr, ki, skip.reshape(1, D), *consts)


ROW_TILE = D // LANES
ROUTER_TM = 512
EXPERT_TM = 512
N_SLOTS = 2 * N_TOK + N_EXPERTS * EXPERT_TM
N_SLOT_TILES = N_SLOTS // EXPERT_TM
INFO_E1, INFO_E2, INFO_R1, INFO_R2, INFO_W1, INFO_W2 = range(6)


def _to_row_tiles(ref, x):
    rows = x.shape[0]
    for j in range(ROW_TILE):
        ref[pl.ds(j, rows, stride=ROW_TILE), :] = x[:, j * LANES:(j + 1) * LANES]


def _from_row_tiles(ref, rows):
    return jnp.concatenate([ref[pl.ds(j, rows, stride=ROW_TILE), :] for j in range(ROW_TILE)], axis=-1)


def _router_body(x_ref, sh_ref, sc_ref, w_ref, info_ref, incl_ref, cnt_ref):
    @pl.when(pl.program_id(0) == 0)
    def _():
        cnt_ref[...] = jnp.zeros_like(cnt_ref)

    h = _modulate(x_ref, sh_ref, sc_ref)
    h_hi, w = h.astype(BF16), w_ref[...]
    h_lo, w_hi = (h - h_hi.astype(F32)).astype(BF16), w.astype(BF16)
    w_lo = (w - w_hi.astype(F32)).astype(BF16)
    logits = (jnp.dot(h_hi, w_hi, preferred_element_type=F32) + jnp.dot(h_lo, w_hi, preferred_element_type=F32)
              + jnp.dot(h_hi, w_lo, preferred_element_type=F32))
    lane = lax.broadcasted_iota(jnp.int32, logits.shape, 1).astype(F32)
    logits = jnp.where(lane < N_EXPERTS, logits, -jnp.inf)
    e = jnp.exp(logits - jnp.max(logits, axis=-1, keepdims=True))
    p = e / jnp.sum(e, axis=-1, keepdims=True)
    p1 = jnp.max(p, axis=-1, keepdims=True)
    i1 = jnp.min(jnp.where(p == p1, lane, float(LANES)), axis=-1, keepdims=True)
    rest = jnp.where(lane == i1, -1.0, p)
    p2 = jnp.max(rest, axis=-1, keepdims=True)
    i2 = jnp.min(jnp.where(rest == p2, lane, float(LANES)), axis=-1, keepdims=True)
    total = p1 + p2
    chosen = jnp.where((lane == i1) | (lane == i2), 1.0, 0.0)
    tm = chosen.shape[0]
    earlier = (lax.broadcasted_iota(jnp.int32, (tm, tm), 1) < lax.broadcasted_iota(jnp.int32, (tm, tm), 0))
    rank = jnp.dot(earlier.astype(BF16), chosen.astype(BF16), preferred_element_type=F32) + cnt_ref[...]
    r1 = jnp.sum(jnp.where(lane == i1, rank, 0.0), axis=-1, keepdims=True)
    r2 = jnp.sum(jnp.where(lane == i2, rank, 0.0), axis=-1, keepdims=True)
    cnt_ref[...] += jnp.sum(chosen, axis=0, keepdims=True)
    incl_ref[0] = jnp.broadcast_to(cnt_ref[...], incl_ref.shape[1:])
    info = jnp.zeros_like(p)
    for col, val in ((INFO_E1, i1), (INFO_E2, i2), (INFO_R1, r1), (INFO_R2, r2),
                     (INFO_W1, p1 / total), (INFO_W2, p2 / total)):
        info = jnp.where(lane == col, val, info)
    info_ref[...] = info


def _router(x, sh, sc, w_router):
    tm = ROUTER_TM
    return pl.pallas_call(
        _router_body,
        out_shape=(jax.ShapeDtypeStruct((N_TOK, LANES), F32),
                   jax.ShapeDtypeStruct((N_TOK // tm, 8, LANES), F32)),
        grid=(N_TOK // tm,),
        in_specs=_mod_specs(tm, 0) + [pl.BlockSpec((D, LANES), lambda i: (0, 0))],
        out_specs=(pl.BlockSpec((tm, LANES), lambda i: (i, 0)), pl.BlockSpec((1, 8, LANES), lambda i: (i, 0, 0))),
        scratch_shapes=[pltpu.VMEM((1, LANES), F32)],
        compiler_params=_params(1),
        name="moe_router",
    )(x, sh, sc, jnp.pad(w_router, ((0, 0), (0, LANES - N_EXPERTS))))


def _row_tile(ref, row):
    return ref.at[pl.ds(pl.multiple_of(row * ROW_TILE, ROW_TILE), ROW_TILE)]


def _dispatch_body(pos1_ref, pos2_ref, x_ref, zeros_hbm, xs_hbm, rows_ref, sem, *, tm):
    del zeros_hbm
    base = pl.program_id(0) * tm
    _to_row_tiles(rows_ref, x_ref[...])

    def copies(r):
        src = _row_tile(rows_ref, r)
        return (pltpu.make_async_copy(src, _row_tile(xs_hbm, pos1_ref[base + r]), sem),
                pltpu.make_async_copy(src, _row_tile(xs_hbm, pos2_ref[base + r]), sem))

    def issue(r, carry):
        for queue, cp in enumerate(copies(r)):
            cp.start(priority=queue)
        return carry

    lax.fori_loop(0, tm, issue, 0, unroll=8)
    for _ in range(2):
        pltpu.make_async_copy(rows_ref, xs_hbm.at[pl.ds(0, tm * ROW_TILE)], sem).wait()


def _dispatch(pos1, pos2, x):
    tm = 1024
    return pl.pallas_call(
        functools.partial(_dispatch_body, tm=tm),
        out_shape=jax.ShapeDtypeStruct((N_SLOTS * ROW_TILE, LANES), F32),
        grid_spec=pltpu.PrefetchScalarGridSpec(
            num_scalar_prefetch=2, grid=(N_TOK // tm,),
            in_specs=[pl.BlockSpec((tm, D), lambda i, *_: (i, 0)), pl.BlockSpec(memory_space=pl.ANY)],
            out_specs=pl.BlockSpec(memory_space=pl.ANY),
            scratch_shapes=[pltpu.VMEM((tm * ROW_TILE, LANES), F32), pltpu.SemaphoreType.DMA(())]),
        input_output_aliases={3: 0},
        compiler_params=_params(1),
        name="moe_dispatch",
    )(pos1, pos2, x, jnp.zeros((N_SLOTS * ROW_TILE, LANES), F32))


def _new_expert(eid_ref, nv_ref, t):
    tt = jnp.minimum(t, nv_ref[0] - 1)
    return (t == 0) | (eid_ref[tt] != eid_ref[jnp.maximum(tt - 1, 0)])


def _expert_swiglu_body(eid_ref, b1_ref, b2_ref, nv_ref, xs_ref, sh_ref, sc_ref, wg_ref, wu_ref, o_ref,
                        wgb_ref, wub_ref, *, fc):
    t = pl.program_id(1)

    @pl.when(_new_expert(eid_ref, nv_ref, t))
    def _():
        wgb_ref[...] = wg_ref[0].astype(BF16)
        wub_ref[...] = wu_ref[0].astype(BF16)

    @pl.when(t < nv_ref[0])
    def _():
        tm = o_ref.shape[0]
        x = _from_row_tiles(xs_ref, tm)
        slot = t * tm + lax.broadcasted_iota(jnp.int32, (tm, 1), 0)
        in1, in2 = slot >= b1_ref[t], slot >= b2_ref[t]
        sc = jnp.where(in2, sc_ref[2], jnp.where(in1, sc_ref[1], sc_ref[0]))
        sh = jnp.where(in2, sh_ref[2], jnp.where(in1, sh_ref[1], sh_ref[0]))
        h = (x * (1.0 + sc) + sh).astype(BF16)
        for c in range(o_ref.shape[1] // fc):
            cs = slice(c * fc, (c + 1) * fc)
            g = jnp.dot(h, wgb_ref[:, cs], preferred_element_type=F32)
            u = jnp.dot(h, wub_ref[:, cs], preferred_element_type=F32)
            o_ref[:, cs] = (g * jax.nn.sigmoid(g) * u).astype(BF16)

    @pl.when(t >= nv_ref[0])
    def _():
        o_ref[...] = jnp.zeros_like(o_ref)


def _expert_swiglu(meta, xs_rt, sh, sc, w_gate, w_up, e0):
    eid, b1, b2, nv = meta
    tm, f = EXPERT_TM, w_gate.shape[2]
    fh = f // 2

    def tile(t, eid, b1, b2, nv):
        return jnp.minimum(t, nv[0] - 1)

    w_spec = pl.BlockSpec((1, D, fh), lambda p, t, eid, b1, b2, nv: (e0 + eid[tile(t, eid, b1, b2, nv)], 0, p))
    mod = pl.BlockSpec((8, 1, D), lambda p, t, *_: (0, 0, 0))
    return pl.pallas_call(
        functools.partial(_expert_swiglu_body, fc=256),
        out_shape=jax.ShapeDtypeStruct((N_SLOTS, f), BF16),
        grid_spec=pltpu.PrefetchScalarGridSpec(
            num_scalar_prefetch=4, grid=(2, N_SLOT_TILES),
            in_specs=[pl.BlockSpec((tm * ROW_TILE, LANES), lambda p, t, *m: (tile(t, *m), 0)), mod, mod,
                      w_spec, w_spec],
            out_specs=pl.BlockSpec((tm, fh), lambda p, t, *m: (t, p)),
            scratch_shapes=[pltpu.VMEM((D, fh), BF16), pltpu.VMEM((D, fh), BF16)]),
        compiler_params=_params(2, 56 * 1024 * 1024),
        name="moe_swiglu",
    )(eid, b1, b2, nv, xs_rt, sh, sc, w_gate, w_up)


def _expert_down_body(eid_ref, nv_ref, a_ref, w_ref, y_ref, wb_ref):
    t = pl.program_id(0)

    @pl.when(_new_expert(eid_ref, nv_ref, t))
    def _():
        wb_ref[...] = w_ref[0].astype(BF16)

    @pl.when(t < nv_ref[0])
    def _():
        _to_row_tiles(y_ref, jnp.dot(a_ref[...], wb_ref[...], preferred_element_type=F32))

    @pl.when(t >= nv_ref[0])
    def _():
        y_ref[...] = jnp.zeros_like(y_ref)


def _expert_down(meta, act, w_down, e0):
    eid, _, _, nv = meta
    tm, f = EXPERT_TM, act.shape[1]

    def tile(t, eid, nv):
        return jnp.minimum(t, nv[0] - 1)

    return pl.pallas_call(
        _expert_down_body,
        out_shape=jax.ShapeDtypeStruct((N_SLOTS * ROW_TILE, LANES), F32),
        grid_spec=pltpu.PrefetchScalarGridSpec(
            num_scalar_prefetch=2, grid=(N_SLOT_TILES,),
            in_specs=[pl.BlockSpec((tm, f), lambda t, *m: (tile(t, *m), 0)),
                      pl.BlockSpec((1, f, D), lambda t, eid, nv: (e0 + eid[tile(t, eid, nv)], 0, 0))],
            out_specs=pl.BlockSpec((tm * ROW_TILE, LANES), lambda t, *m: (t, 0)),
            scratch_shapes=[pltpu.VMEM((f, D), BF16)]),
        compiler_params=_params(1, 58 * 1024 * 1024),
        name="moe_down",
    )(eid, nv, act, w_down)


def _combine_body(pos1_ref, pos2_ref, y_hbm, info_ref, x_ref, g_ref, lng_ref, lnb_ref, o_ref, y1_ref, y2_ref, sem,
                  *, tm):
    base = pl.program_id(0) * tm

    def copies(r):
        return (pltpu.make_async_copy(_row_tile(y_hbm, pos1_ref[base + r]), _row_tile(y1_ref, r), sem),
                pltpu.make_async_copy(_row_tile(y_hbm, pos2_ref[base + r]), _row_tile(y2_ref, r), sem))

    def issue(r, carry):
        for queue, cp in enumerate(copies(r)):
            cp.start(priority=queue)
        return carry

    lax.fori_loop(0, tm, issue, 0, unroll=8)
    for y_ref in (y1_ref, y2_ref):
        pltpu.make_async_copy(y_hbm.at[pl.ds(0, tm * ROW_TILE)], y_ref, sem).wait()
    info = info_ref[...]
    ffn = (info[:, INFO_W1:INFO_W1 + 1] * _from_row_tiles(y1_ref, tm)
           + info[:, INFO_W2:INFO_W2 + 1] * _from_row_tiles(y2_ref, tm))
    y = ALPHA * x_ref[...] + g_ref[0] * ffn
    mu = jnp.mean(y, axis=-1, keepdims=True)
    yc = y - mu
    var = jnp.mean(yc * yc, axis=-1, keepdims=True)
    o_ref[...] = yc * lax.rsqrt(var + LN_EPS) * lng_ref[...] + lnb_ref[...]


def _combine(pos1, pos2, y_rt, info, x, gate, ln_g, ln_b):
    tm = 1024
    return pl.pallas_call(
        functools.partial(_combine_body, tm=tm),
        out_shape=jax.ShapeDtypeStruct((N_TOK, D), F32),
        grid_spec=pltpu.PrefetchScalarGridSpec(
            num_scalar_prefetch=2, grid=(N_TOK // tm,),
            in_specs=[pl.BlockSpec(memory_space=pl.ANY),
                      pl.BlockSpec((tm, LANES), lambda i, *_: (i, 0)),
                      pl.BlockSpec((tm, D), lambda i, *_: (i, 0)),
                      pl.BlockSpec((1, 1, D), lambda i, *_: (_group_of_row(i * tm), 0, 0)),
                      pl.BlockSpec((1, D), lambda i, *_: (0, 0)),
                      pl.BlockSpec((1, D), lambda i, *_: (0, 0))],
            out_specs=pl.BlockSpec((tm, D), lambda i, *_: (i, 0)),
            scratch_shapes=[pltpu.VMEM((tm * ROW_TILE, LANES), F32), pltpu.VMEM((tm * ROW_TILE, LANES), F32),
                            pltpu.SemaphoreType.DMA(())]),
        compiler_params=_params(1),
        name="moe_combine",
    )(pos1, pos2, y_rt, info, x, gate, ln_g.reshape(1, D), ln_b.reshape(1, D))


def _slot_plan(info, incl):
    row = lambda n_rows: incl[n_rows // ROUTER_TM - 1, 0, :N_EXPERTS].astype(jnp.int32)
    count = row(N_TOK)
    padded = (count + EXPERT_TM - 1) // EXPERT_TM * EXPERT_TM
    end = jnp.cumsum(padded)
    start = end - padded
    tile_row = jnp.arange(N_SLOT_TILES, dtype=jnp.int32) * EXPERT_TM
    eid = jnp.minimum(jnp.sum(tile_row[:, None] >= end[None, :], axis=1), N_EXPERTS - 1).astype(jnp.int32)
    b1 = (start + row(N_CTX))[eid]
    b2 = (start + row(N_CTX + DEC_SEQ))[eid]
    nv = (end[-1:] // EXPERT_TM).astype(jnp.int32)
    experts = jnp.arange(N_EXPERTS, dtype=jnp.int32)
    start_of = lambda col: jnp.sum(jnp.where(info[:, col:col + 1].astype(jnp.int32) == experts, start, 0), axis=1)
    pos1 = start_of(INFO_E1) + info[:, INFO_R1].astype(jnp.int32)
    pos2 = start_of(INFO_E2) + info[:, INFO_R2].astype(jnp.int32)
    return pos1, pos2, (eid, b1, b2, nv)


def _even_mixer(x, sh, sc, gate, ln_g, ln_b, w_in, b_igate, b_fgate, ml_norm_g, q_norm_g, k_norm_g, w_out,
                st_c, st_n, st_m, cache_k, cache_v, rope_tabs):
    splits = (4 * ML_W, 4 * ML_W + N_GATES)
    w_main = jnp.concatenate([w_in[:, :splits[0]], w_in[:, splits[1]:]], axis=1).astype(BF16)
    proj = _mod_matmul(x, sh, sc, [w_main[None]], tm=1024, tn=MAIN_W // 2, out_dtype=F32, name="even_in_proj")[0]
    b_gate = jnp.stack([b_igate, b_fgate], axis=1).reshape(N_GATES)
    lic, bc, lir, br = _gates(x, sh, sc, w_in[:, splits[0]:splits[1]], b_gate)

    hf_c, hb_c, new_c, new_n, new_m = _mlstm(proj, lic, bc, lir, br, row0=0, n_seq=BATCH, seq_len=SEQ)
    init = (st_c, st_n, jnp.broadcast_to(st_m[..., None], st_n.shape))
    hf_s, hb_s, _, _, _ = _mlstm(proj, lic, bc, lir, br, row0=N_CTX, n_seq=DEC_BATCH, seq_len=DEC_SEQ, init=init)
    ml = _ml_post((hf_c, hf_s), (hb_c, hb_s), proj, ml_norm_g)

    q_c, kn_c, kb_c = _qk_prep(proj, q_norm_g, k_norm_g, row0=0, rows=N_CTX)
    q_s, _, kb_s = _qk_prep(proj, q_norm_g, k_norm_g, row0=N_CTX, rows=N_LAT, rope_tabs=rope_tabs)
    v_all = proj[:, MAIN_W - KV_W:]
    v_c, v_s = v_all[:N_CTX], v_all[N_CTX:]
    att_c = _attention(q_c, _head_major(kb_c, BATCH).swapaxes(2, 3),
                       _with_ones(_head_major(v_c.astype(BF16), BATCH)), tq=SEQ)
    k_lat = jnp.concatenate([kb_s.reshape(DEC_BATCH, DEC_SEQ, KV_W),
                             cache_k.reshape(DEC_BATCH, PAST_LEN, KV_W).astype(BF16)], axis=1)
    v_lat = jnp.concatenate([v_s.reshape(DEC_BATCH, DEC_SEQ, KV_W).astype(BF16),
                             cache_v.reshape(DEC_BATCH, PAST_LEN, KV_W).astype(BF16)], axis=1)
    att_s = _attention(q_s, _head_major(k_lat.reshape(-1, KV_W), DEC_BATCH).swapaxes(2, 3),
                       _with_ones(_head_major(v_lat.reshape(-1, KV_W), DEC_BATCH)), tq=256)

    x = _proj_res_ln([ml, (att_c, att_s)], w_out.astype(BF16), x, gate, ln_g, ln_b, tm=512, name="even_out_proj")
    new_k = kn_c.reshape(BATCH, SEQ, ATT_KV_HEADS, ATT_HEAD_DIM)
    new_v = v_c.reshape(BATCH, SEQ, ATT_KV_HEADS, ATT_HEAD_DIM)
    return x, new_k, new_v, new_c, new_n, new_m[..., 0]


def _hyena_mixer(x, sh, sc, gate, ln_g, ln_b, w_in, conv_w, conv_b, w1, b1, w2, b2, w3, sin_freq, skip, w_out, dft):
    u = _mod_matmul(x, sh, sc, [w_in.astype(BF16)[None]], tm=1024, tn=1536, out_dtype=F32, name="hyena_in_proj")[0]
    mats, consts = dft
    h_sum, h_diff, nyq = _hy_filter(SEQ, w1, b1, w2, b2, w3, sin_freq)
    kr, ki = _ctx_spectrum(mats, h_sum, h_diff, nyq)
    y_c = _hy_conv_ctx(u, conv_w, conv_b, kr, ki, skip, mats, seqs_per_step=8)
    h_sum, h_diff, _ = _hy_filter(DEC_SEQ, w1, b1, w2, b2, w3, sin_freq)
    kr, ki = _hy_spectrum(h_sum, h_diff, consts)
    y_s = _hy_conv_fft(u, conv_w, conv_b, kr, ki, skip, consts, row0=N_CTX, n_seq=DEC_BATCH)
    return _proj_res_ln([(y_c, y_s)], w_out.astype(BF16), x, gate, ln_g, ln_b, tm=512, name="hyena_out_proj")


def _dense_ffn(x, sh, sc, gate, ln_g, ln_b, w_gate, w_up, w_down):
    act = _mod_matmul(x, sh, sc, [w_gate.astype(BF16)[None], w_up.astype(BF16)[None]],
                      tm=1024, tn=D_FF // 2, out_dtype=BF16, name="ffn_swiglu")
    return _proj_res_ln([act[0]], w_down.astype(BF16), x, gate, ln_g, ln_b, tm=512, name="ffn_down")


def _moe_ffn(x, sh, sc, gate, ln_g, ln_b, w_router, w_gate, w_up, w_down, e0):
    info, incl = _router(x, sh, sc, w_router)
    pos1, pos2, meta = _slot_plan(info, incl)
    xs_rt = _dispatch(pos1, pos2, x)
    act = _expert_swiglu(meta, xs_rt, sh, sc, w_gate, w_up, e0)
    y_rt = _expert_down(meta, act, w_down, e0)
    return _combine(pos1, pos2, y_rt, info, x, gate, ln_g, ln_b)


def kernel(x_prompt, x_sample, cache_attn_k, cache_attn_v, state_mlstm_C, state_mlstm_n, state_mlstm_m, c, c_ctx, w_ada, b_ada, ln_g, ln_b, w_in_even, b_igate, b_fgate, ml_norm_g, q_norm_g, k_norm_g, w_out_even, w_ffn_gate, w_ffn_up, w_ffn_down, w_in_hy, hy_conv_w, hy_conv_b, hy_filt_w1, hy_filt_b1, hy_filt_w2, hy_filt_b2, hy_filt_w3, hy_sin_freq, hy_skip, w_out_hy, w_router, w_moe_gate, w_moe_up, w_moe_down):
    x = jnp.concatenate([x_prompt.reshape(N_CTX, D), x_sample.reshape(N_LAT, D)])
    cvec = jnp.concatenate([c_ctx[None], c, jnp.zeros((8 - 1 - DEC_BATCH, D), F32)])
    mods = _ada(cvec, w_ada, b_ada)
    rope_tabs = _rope_tables()
    dft = (_dft_matrices(SEQ), _fft_consts())
    moe_w = [w.reshape((-1,) + w.shape[2:]) for w in (w_moe_gate, w_moe_up, w_moe_down)]
    new_k, new_v, new_c, new_n, new_m = [], [], [], [], []
    for layer in range(DEPTH):
        sh1, sc1, g1, sh2, sc2, g2 = (mods[layer, :, i * D:(i + 1) * D].reshape(8, 1, D) for i in range(6))
        i = layer // 2
        if layer % 2 == 0:
            x, k_c, v_c, st_c, st_n, st_m = _even_mixer(
                x, sh1, sc1, g1, ln_g[layer, 0], ln_b[layer, 0], w_in_even[i], b_igate[i], b_fgate[i], ml_norm_g[i],
                q_norm_g[i], k_norm_g[i], w_out_even[i], state_mlstm_C[:, i], state_mlstm_n[:, i], state_mlstm_m[:, i],
                cache_attn_k[:, i], cache_attn_v[:, i], rope_tabs)
            new_k.append(k_c)
            new_v.append(v_c)
            new_c.append(st_c)
            new_n.append(st_n)
            new_m.append(st_m)
            x = _dense_ffn(x, sh2, sc2, g2, ln_g[layer, 1], ln_b[layer, 1], w_ffn_gate[i], w_ffn_up[i], w_ffn_down[i])
        else:
            x = _hyena_mixer(x, sh1, sc1, g1, ln_g[layer, 0], ln_b[layer, 0], w_in_hy[i], hy_conv_w[i], hy_conv_b[i],
                             hy_filt_w1[i], hy_filt_b1[i], hy_filt_w2[i], hy_filt_b2[i], hy_filt_w3[i], hy_sin_freq[i],
                             hy_skip[i], w_out_hy[i], dft)
            x = _moe_ffn(x, sh2, sc2, g2, ln_g[layer, 1], ln_b[layer, 1], w_router[i], *moe_w, i * N_EXPERTS)
    return (x[:N_CTX].reshape(BATCH, SEQ, D), x[N_CTX:].reshape(DEC_BATCH, DEC_SEQ, D),
            jnp.stack(new_k, axis=1), jnp.stack(new_v, axis=1), jnp.stack(new_c, axis=1),
            jnp.stack(new_n, axis=1), jnp.stack(new_m, axis=1))
```
